```python
import jax, jax.numpy as jnp
from jax import lax
import numpy as np

D_MODEL = 1024
BATCH = 8
SEQ = 8192
DEPTH = 4

GRID_W = 64
CTX_LEN = 256
N_MIXERS = 2
HGRN_HEAD_DIM = 128
HGRN_HEADS = D_MODEL // HGRN_HEAD_DIM
CHUNK = 64
CONV_WIDTH = 3
D_FF = 4 * D_MODEL
N_REC_LAYERS = (DEPTH + N_MIXERS - 1) // N_MIXERS
N_CONV_LAYERS = DEPTH // N_MIXERS
EPS = 1e-6

kernel_name = 'hybrid_hgrn2_shortconv_dit'


def _rmsnorm(x, gain):
    x32 = x.astype(jnp.float32)
    y = x32 * lax.rsqrt(jnp.mean(x32 * x32, axis=-1, keepdims=True) + EPS)
    return (y * gain.astype(jnp.float32)).astype(x.dtype)


def _modulate(h, shift, scale):
    return h * (1 + scale) + shift


def _mlp(h, w1, w2):
    return jnp.square(jax.nn.relu(h @ w1)) @ w2


def _heads(t):
    return t.reshape(*t.shape[:-1], HGRN_HEADS, HGRN_HEAD_DIM).astype(jnp.float32)


def _to_chunks(t):
    b, l, h, e = t.shape
    return t.reshape(b, l // CHUNK, CHUNK, h, e).transpose(1, 0, 3, 2, 4)


def _gla_chunkwise(q, k, v, g, s0):
    bsz, l, h, _ = q.shape
    lower = jnp.tril(jnp.ones((CHUNK, CHUNK), dtype=bool))

    def step(s, blk):
        qb, kb, vb, gb = blk
        cum = jnp.cumsum(gb, axis=2)
        ref = cum[:, :, CHUNK // 2 - 1:CHUNK // 2]
        last = cum[:, :, -1:]
        o_inter = jnp.einsum('bhck,bhkv->bhcv', qb * jnp.exp(cum), s)
        scores = jnp.einsum('bhck,bhsk->bhcs', qb * jnp.exp(cum - ref), kb * jnp.exp(ref - cum))
        scores = jnp.where(lower, scores, 0.0)
        o_intra = jnp.einsum('bhcs,bhsv->bhcv', scores, vb)
        s_new = jnp.exp(last[:, :, 0, :, None]) * s + jnp.einsum(
            'bhsk,bhsv->bhkv', kb * jnp.exp(last - cum), vb)
        return s_new, o_inter + o_intra

    s_fin, o = lax.scan(step, s0, (_to_chunks(q), _to_chunks(k), _to_chunks(v), _to_chunks(g)))
    o = o.transpose(1, 0, 3, 2, 4).reshape(bsz, l, h, v.shape[-1])
    return o, s_fin


def _gla_final_state(k, v, g):
    cum = jnp.cumsum(g, axis=1)
    return jnp.einsum('blhk,blhv->bhkv', k * jnp.exp(cum[:, -1:] - cum), v)


def _flip(t, direction):
    return jnp.flip(t, axis=1) if direction == 1 else t


def _hgrn2_inputs(h, w_in, lb, with_query):
    n_parts = 5 if with_query else 3
    parts = jnp.split(h @ w_in[:, :n_parts * D_MODEL], n_parts, axis=-1)
    v = _heads(parts[2])
    dirs = []
    for d in range(2):
        lb_d = lb[d].reshape(HGRN_HEADS, HGRN_HEAD_DIM)
        f = lb_d + (1.0 - lb_d) * jax.nn.sigmoid(_heads(parts[d]))
        dirs.append((1.0 - f, jnp.log(f)))
    if with_query:
        return v, dirs, jax.nn.silu(_heads(parts[3])), parts[4]
    return v, dirs, None, None


def _hgrn2_readout(o, gate, gnorm, w_out, dtype):
    o = o * lax.rsqrt(jnp.mean(o * o, axis=-1, keepdims=True) + EPS)
    o = o * gnorm.astype(jnp.float32).reshape(HGRN_HEADS, HGRN_HEAD_DIM) * jax.nn.silu(_heads(gate))
    return o.reshape(*o.shape[:-2], D_MODEL).astype(dtype) @ w_out


def _hgrn2_mixer(h, hc, w_in, lb, gnorm, w_out, ctx_out):
    v, dirs, q, gate = _hgrn2_inputs(h, w_in, lb, True)
    vc, dirs_c, qc, gate_c = _hgrn2_inputs(hc, w_in, lb, ctx_out)
    bsz = h.shape[0]
    o_dirs, oc_dirs = [], []
    for d in range(2):
        k, g = dirs[d]
        kc, gc = dirs_c[d]
        if ctx_out:
            s0 = jnp.zeros((bsz, HGRN_HEADS, HGRN_HEAD_DIM, HGRN_HEAD_DIM), jnp.float32)
            oc_d, s_ctx = _gla_chunkwise(_flip(qc, d), _flip(kc, d), _flip(vc, d), _flip(gc, d), s0)
            oc_dirs.append(_flip(oc_d, d))
        else:
            s_ctx = _gla_final_state(_flip(kc, d), _flip(vc, d), _flip(gc, d))
        o_d, _ = _gla_chunkwise(_flip(q, d), _flip(k, d), _flip(v, d), _flip(g, d), s_ctx)
        o_dirs.append(_flip(o_d, d))
    y = _hgrn2_readout(o_dirs[0] + o_dirs[1], gate, gnorm, w_out, h.dtype)
    yc = _hgrn2_readout(oc_dirs[0] + oc_dirs[1], gate_c, gnorm, w_out, hc.dtype) if ctx_out else None
    return y, yc


def _dwconv(z, w, b, axis):
    n = z.shape[axis]
    half = CONV_WIDTH // 2
    pad = [(0, 0)] * z.ndim
    pad[axis] = (half, half)
    zp = jnp.pad(z, pad)
    y = b
    for tap in range(CONV_WIDTH):
        y = y + w[tap] * lax.slice_in_dim(zp, tap, tap + n, axis=axis)
    return y


def _shortconv_mixer(h, w_in, w, b, w_out, axis):
    gate_b, gate_c, xin = jnp.split(h @ w_in, 3, axis=-1)
    return (gate_b * _dwconv(gate_c * xin, w, b, axis)) @ w_out


def _fwd_setup_inputs(seed: int = 0) -> dict:
    key = jax.random.key(seed)
    ks = jax.random.split(key, 19)
    nrm = jax.random.normal
    f32 = jnp.float32
    d = D_MODEL
    return {
        'x': nrm(ks[0], (BATCH, SEQ, d), f32),
        'c': nrm(ks[1], (BATCH, d), f32),
        'ctx': nrm(ks[2], (BATCH, CTX_LEN, d), f32),
        'c_ctx': nrm(ks[3], (d,), f32),
        'ada_w': nrm(ks[4], (DEPTH, d, 6 * d), f32) * (0.5 * d ** -0.5),
        'ada_b': 0.02 * nrm(ks[5], (DEPTH, 6 * d), f32),
        'norm1': 1.0 + 0.02 * nrm(ks[6], (DEPTH, d), f32),
        'norm2': 1.0 + 0.02 * nrm(ks[7], (DEPTH, d), f32),
        'norm_f': 1.0 + 0.02 * nrm(ks[8], (d,), f32),
        'mlp_w1': nrm(ks[9], (DEPTH, d, D_FF), f32) * d ** -0.5,
        'mlp_w2': nrm(ks[10], (DEPTH, D_FF, d), f32) * D_FF ** -0.5,
        'hgrn_w_in': nrm(ks[11], (N_REC_LAYERS, d, 5 * d), f32) * d ** -0.5,
        'hgrn_lb': nrm(ks[12], (2, N_REC_LAYERS, d), f32),
        'hgrn_gnorm': 1.0 + 0.02 * nrm(ks[13], (N_REC_LAYERS, d), f32),
        'hgrn_w_out': nrm(ks[14], (N_REC_LAYERS, d, d), f32) * d ** -0.5,
        'conv_w_in': nrm(ks[15], (N_CONV_LAYERS, d, 3 * d), f32) * d ** -0.5,
        'conv_w': nrm(ks[16], (N_CONV_LAYERS, CONV_WIDTH, d), f32) * CONV_WIDTH ** -0.5,
        'conv_b': 0.02 * nrm(ks[17], (N_CONV_LAYERS, d), f32),
        'conv_w_out': nrm(ks[18], (N_CONV_LAYERS, d, d), f32) * d ** -0.5,
    }


def _fwd_reference(x, c, ctx, c_ctx, ada_w, ada_b, norm1, norm2, norm_f, mlp_w1, mlp_w2,
              hgrn_w_in, hgrn_lb, hgrn_gnorm, hgrn_w_out, conv_w_in, conv_w, conv_b, conv_w_out):
    bsz, seq, _ = x.shape
    rows = seq // GRID_W
    lb_p = jax.nn.softmax(hgrn_lb.astype(jnp.float32), axis=1)
    lower_bounds = jnp.cumsum(lb_p, axis=1) - lb_p[:, :1]
    silu_c = jax.nn.silu(c)
    silu_cc = jax.nn.silu(c_ctx)
    last_rec = ((DEPTH - 1) // N_MIXERS) * N_MIXERS
    x_ctx = ctx
    for i in range(DEPTH):
        j = i // N_MIXERS
        recurrent = i % N_MIXERS == 0
        ctx_live = i < last_rec
        sh1, sc1, g1, sh2, sc2, g2 = jnp.split((silu_c @ ada_w[i] + ada_b[i])[:, None, :], 6, axis=-1)
        h = _modulate(_rmsnorm(x, norm1[i]), sh1, sc1)
        if ctx_live or recurrent:
            csh1, csc1, cg1, csh2, csc2, cg2 = jnp.split(silu_cc @ ada_w[i] + ada_b[i], 6)
            hc = _modulate(_rmsnorm(x_ctx, norm1[i]), csh1, csc1)
        if recurrent:
            y, yc = _hgrn2_mixer(h, hc, hgrn_w_in[j], lower_bounds[:, j], hgrn_gnorm[j],
                                 hgrn_w_out[j], ctx_live)
        else:
            axis = 2 if j % 2 == 0 else 1
            y = _shortconv_mixer(h.reshape(bsz, rows, GRID_W, D_MODEL), conv_w_in[j], conv_w[j],
                                 conv_b[j], conv_w_out[j], axis).reshape(bsz, seq, D_MODEL)
            yc = _shortconv_mixer(hc, conv_w_in[j], conv_w[j], conv_b[j], conv_w_out[j], 1) if ctx_live else None
        x = x + g1 * y
        x = x + g2 * _mlp(_modulate(_rmsnorm(x, norm2[i]), sh2, sc2), mlp_w1[i], mlp_w2[i])
        if ctx_live:
            x_ctx = x_ctx + cg1 * yc
            x_ctx = x_ctx + cg2 * _mlp(_modulate(_rmsnorm(x_ctx, norm2[i]), csh2, csc2), mlp_w1[i], mlp_w2[i])
    return _rmsnorm(x, norm_f)


import jax as _jax
import jax.numpy as _jnp

TWIN_FORMAT = 'train_step'
FWD_PARAMS = ['x', 'c', 'ctx', 'c_ctx', 'ada_w', 'ada_b', 'norm1', 'norm2', 'norm_f', 'mlp_w1', 'mlp_w2', 'hgrn_w_in', 'hgrn_lb', 'hgrn_gnorm', 'hgrn_w_out', 'conv_w_in', 'conv_w', 'conv_b', 'conv_w_out']
TWIN_WEIGHTS = ['c_ctx', 'ada_w', 'ada_b', 'norm1', 'norm2', 'norm_f', 'mlp_w1', 'mlp_w2', 'hgrn_w_in', 'hgrn_lb', 'hgrn_gnorm', 'hgrn_w_out', 'conv_w_in', 'conv_w', 'conv_b', 'conv_w_out']
TWIN_DIFF_INPUT = 'x'
TWIN_INPUTS = ['x', 'c', 'ctx', 'c_ctx', 'ada_w', 'ada_b', 'norm1', 'norm2', 'norm_f', 'mlp_w1', 'mlp_w2', 'hgrn_w_in', 'hgrn_lb', 'hgrn_gnorm', 'hgrn_w_out', 'conv_w_in', 'conv_w', 'conv_b', 'conv_w_out', 'loss_target', 'm_c_ctx', 'm_ada_w', 'm_ada_b', 'm_norm1', 'm_norm2', 'm_norm_f', 'm_mlp_w1', 'm_mlp_w2', 'm_hgrn_w_in', 'm_hgrn_lb', 'm_hgrn_gnorm', 'm_hgrn_w_out', 'm_conv_w_in', 'm_conv_w', 'm_conv_b', 'm_conv_w_out', 'v_c_ctx', 'v_ada_w', 'v_ada_b', 'v_norm1', 'v_norm2', 'v_norm_f', 'v_mlp_w1', 'v_mlp_w2', 'v_hgrn_w_in', 'v_hgrn_lb', 'v_hgrn_gnorm', 'v_hgrn_w_out', 'v_conv_w_in', 'v_conv_w', 'v_conv_b', 'v_conv_w_out']
TWIN_OUTPUTS = ['loss', 'grad_x', 'grad_c_ctx', 'grad_ada_w', 'grad_ada_b', 'grad_norm1', 'grad_norm2', 'grad_norm_f', 'grad_mlp_w1', 'grad_mlp_w2', 'grad_hgrn_w_in', 'grad_hgrn_lb', 'grad_hgrn_gnorm', 'grad_hgrn_w_out', 'grad_conv_w_in', 'grad_conv_w', 'grad_conv_b', 'grad_conv_w_out', 'delta_c_ctx', 'delta_ada_w', 'delta_ada_b', 'delta_norm1', 'delta_norm2', 'delta_norm_f', 'delta_mlp_w1', 'delta_mlp_w2', 'delta_hgrn_w_in', 'delta_hgrn_lb', 'delta_hgrn_gnorm', 'delta_hgrn_w_out', 'delta_conv_w_in', 'delta_conv_w', 'delta_conv_b', 'delta_conv_w_out', 'new_m_c_ctx', 'new_m_ada_w', 'new_m_ada_b', 'new_m_norm1', 'new_m_norm2', 'new_m_norm_f', 'new_m_mlp_w1', 'new_m_mlp_w2', 'new_m_hgrn_w_in', 'new_m_hgrn_lb', 'new_m_hgrn_gnorm', 'new_m_hgrn_w_out', 'new_m_conv_w_in', 'new_m_conv_w', 'new_m_conv_b', 'new_m_conv_w_out', 'new_v_c_ctx', 'new_v_ada_w', 'new_v_ada_b', 'new_v_norm1', 'new_v_norm2', 'new_v_norm_f', 'new_v_mlp_w1', 'new_v_mlp_w2', 'new_v_hgrn_w_in', 'new_v_hgrn_lb', 'new_v_hgrn_gnorm', 'new_v_hgrn_w_out', 'new_v_conv_w_in', 'new_v_conv_w', 'new_v_conv_b', 'new_v_conv_w_out']
TWIN_LEAF_KINDS = {'loss': 'loss', 'grad_x': 'grad_x', 'grad_c_ctx': 'grad_w', 'grad_ada_w': 'grad_w', 'grad_ada_b': 'grad_w', 'grad_norm1': 'grad_w', 'grad_norm2': 'grad_w', 'grad_norm_f': 'grad_w', 'grad_mlp_w1': 'grad_w', 'grad_mlp_w2': 'grad_w', 'grad_hgrn_w_in': 'grad_w', 'grad_hgrn_lb': 'grad_w', 'grad_hgrn_gnorm': 'grad_w', 'grad_hgrn_w_out': 'grad_w', 'grad_conv_w_in': 'grad_w', 'grad_conv_w': 'grad_w', 'grad_conv_b': 'grad_w', 'grad_conv_w_out': 'grad_w', 'delta_c_ctx': 'delta_w', 'delta_ada_w': 'delta_w', 'delta_ada_b': 'delta_w', 'delta_norm1': 'delta_w', 'delta_norm2': 'delta_w', 'delta_norm_f': 'delta_w', 'delta_mlp_w1': 'delta_w', 'delta_mlp_w2': 'delta_w', 'delta_hgrn_w_in': 'delta_w', 'delta_hgrn_lb': 'delta_w', 'delta_hgrn_gnorm': 'delta_w', 'delta_hgrn_w_out': 'delta_w', 'delta_conv_w_in': 'delta_w', 'delta_conv_w': 'delta_w', 'delta_conv_b': 'delta_w', 'delta_conv_w_out': 'delta_w', 'new_m_c_ctx': 'new_m', 'new_m_ada_w': 'new_m', 'new_m_ada_b': 'new_m', 'new_m_norm1': 'new_m', 'new_m_norm2': 'new_m', 'new_m_norm_f': 'new_m', 'new_m_mlp_w1': 'new_m', 'new_m_mlp_w2': 'new_m', 'new_m_hgrn_w_in': 'new_m', 'new_m_hgrn_lb': 'new_m', 'new_m_hgrn_gnorm': 'new_m', 'new_m_hgrn_w_out': 'new_m', 'new_m_conv_w_in': 'new_m', 'new_m_conv_w': 'new_m', 'new_m_conv_b': 'new_m', 'new_m_conv_w_out': 'new_m', 'new_v_c_ctx': 'new_v', 'new_v_ada_w': 'new_v', 'new_v_ada_b': 'new_v', 'new_v_norm1': 'new_v', 'new_v_norm2': 'new_v', 'new_v_norm_f': 'new_v', 'new_v_mlp_w1': 'new_v', 'new_v_mlp_w2': 'new_v', 'new_v_hgrn_w_in': 'new_v', 'new_v_hgrn_lb': 'new_v', 'new_v_hgrn_gnorm': 'new_v', 'new_v_hgrn_w_out': 'new_v', 'new_v_conv_w_in': 'new_v', 'new_v_conv_w': 'new_v', 'new_v_conv_b': 'new_v', 'new_v_conv_w_out': 'new_v'}


def _forward(args):
    return _fwd_reference(*[args[k] for k in FWD_PARAMS])


def _output_shape():
    def fwd():
        inp = _fwd_setup_inputs(0)
        return _fwd_reference(*[inp[k] for k in FWD_PARAMS])
    out = _jax.eval_shape(fwd)
    return out.shape, out.dtype

N_MICROBATCH = 1
ADAM_LR = 0.001
ADAM_B1 = 0.9
ADAM_B2 = 0.999
ADAM_EPS = 1e-08
ADAM_WD = 0.01
ADAM_STEP = 10
PER_EXAMPLE_BATCH_AXIS = {'x': 0, 'c': 0, 'ctx': 0, 'loss_target': 0}
SHARED_INPUTS = []
_WEIGHT_DTYPES = {'c_ctx': _jnp.float32, 'ada_w': _jnp.float32, 'ada_b': _jnp.float32, 'norm1': _jnp.float32, 'norm2': _jnp.float32, 'norm_f': _jnp.float32, 'mlp_w1': _jnp.float32, 'mlp_w2': _jnp.float32, 'hgrn_w_in': _jnp.float32, 'hgrn_lb': _jnp.float32, 'hgrn_gnorm': _jnp.float32, 'hgrn_w_out': _jnp.float32, 'conv_w_in': _jnp.float32, 'conv_w': _jnp.float32, 'conv_b': _jnp.float32, 'conv_w_out': _jnp.float32}
MOMENT_SCALE = {'c_ctx': 2.884765e-03, 'ada_w': 1.209699e-01, 'ada_b': 2.121722e-01, 'norm1': 1.060491e-01, 'norm2': 1.048084e-01, 'norm_f': 6.451089e+01, 'mlp_w1': 5.604354e-02, 'mlp_w2': 1.011630e-01, 'hgrn_w_in': 3.410264e-02, 'hgrn_lb': 1.592675e-03, 'hgrn_gnorm': 5.174896e-02, 'hgrn_w_out': 5.261388e-02, 'conv_w_in': 8.241033e-02, 'conv_w': 8.490911e-02, 'conv_b': 7.074858e-02, 'conv_w_out': 8.238171e-02}


def _to_microbatches(a, axis):
    t = _jnp.moveaxis(a, axis, 0)
    t = t.reshape((N_MICROBATCH, t.shape[0] // N_MICROBATCH) + t.shape[1:])
    return _jnp.moveaxis(t, 1, axis + 1)


def setup_inputs(seed: int = 0) -> dict:
    inp = _fwd_setup_inputs(seed)
    key = _jax.random.fold_in(_jax.random.key(seed), 7919)
    shape, _ = _output_shape()
    out = dict(inp)
    out["loss_target"] = _jax.random.normal(_jax.random.fold_in(key, 0), shape, _jnp.float32)
    for i, name in enumerate(TWIN_WEIGHTS):
        w = inp[name].astype(_jnp.float32)
        if MOMENT_SCALE is None:
            s = _jnp.sqrt(_jnp.mean(_jnp.square(w)) + 1e-30)
        else:
            s = MOMENT_SCALE[name]
        km, kv = _jax.random.split(_jax.random.fold_in(key, i + 1))
        out[name] = w
        out["m_" + name] = s * _jax.random.normal(km, w.shape, _jnp.float32)
        out["v_" + name] = (s * s) * _jax.random.uniform(kv, w.shape, _jnp.float32, 0.5, 1.5)
    if N_MICROBATCH > 1:
        for name, axis in PER_EXAMPLE_BATCH_AXIS.items():
            out[name] = _to_microbatches(out[name], axis)
    return {'x': out['x'], 'c': out['c'], 'ctx': out['ctx'], 'c_ctx': out['c_ctx'], 'ada_w': out['ada_w'], 'ada_b': out['ada_b'], 'norm1': out['norm1'], 'norm2': out['norm2'], 'norm_f': out['norm_f'], 'mlp_w1': out['mlp_w1'], 'mlp_w2': out['mlp_w2'], 'hgrn_w_in': out['hgrn_w_in'], 'hgrn_lb': out['hgrn_lb'], 'hgrn_gnorm': out['hgrn_gnorm'], 'hgrn_w_out': out['hgrn_w_out'], 'conv_w_in': out['conv_w_in'], 'conv_w': out['conv_w'], 'conv_b': out['conv_b'], 'conv_w_out': out['conv_w_out'], 'loss_target': out['loss_target'], 'm_c_ctx': out['m_c_ctx'], 'm_ada_w': out['m_ada_w'], 'm_ada_b': out['m_ada_b'], 'm_norm1': out['m_norm1'], 'm_norm2': out['m_norm2'], 'm_norm_f': out['m_norm_f'], 'm_mlp_w1': out['m_mlp_w1'], 'm_mlp_w2': out['m_mlp_w2'], 'm_hgrn_w_in': out['m_hgrn_w_in'], 'm_hgrn_lb': out['m_hgrn_lb'], 'm_hgrn_gnorm': out['m_hgrn_gnorm'], 'm_hgrn_w_out': out['m_hgrn_w_out'], 'm_conv_w_in': out['m_conv_w_in'], 'm_conv_w': out['m_conv_w'], 'm_conv_b': out['m_conv_b'], 'm_conv_w_out': out['m_conv_w_out'], 'v_c_ctx': out['v_c_ctx'], 'v_ada_w': out['v_ada_w'], 'v_ada_b': out['v_ada_b'], 'v_norm1': out['v_norm1'], 'v_norm2': out['v_norm2'], 'v_norm_f': out['v_norm_f'], 'v_mlp_w1': out['v_mlp_w1'], 'v_mlp_w2': out['v_mlp_w2'], 'v_hgrn_w_in': out['v_hgrn_w_in'], 'v_hgrn_lb': out['v_hgrn_lb'], 'v_hgrn_gnorm': out['v_hgrn_gnorm'], 'v_hgrn_w_out': out['v_hgrn_w_out'], 'v_conv_w_in': out['v_conv_w_in'], 'v_conv_w': out['v_conv_w'], 'v_conv_b': out['v_conv_b'], 'v_conv_w_out': out['v_conv_w_out']}


def _loss(weights, diff, rest, loss_target):
    with _jax.named_scope("forward"):
        args = {**rest, TWIN_DIFF_INPUT: diff, **{k: w.astype(_WEIGHT_DTYPES[k]) for k, w in weights.items()}}
        y = _forward(args)
    with _jax.named_scope("loss_head"):
        err = _jnp.square(y.astype(_jnp.float32) - loss_target)
        return 0.5 * _jnp.sum(_jnp.mean(err, axis=-1)) if err.ndim else 0.5 * err


def _adamw(w, g, m, v):
    m = ADAM_B1 * m + (1.0 - ADAM_B1) * g
    v = ADAM_B2 * v + (1.0 - ADAM_B2) * _jnp.square(g)
    m_hat = m / (1.0 - ADAM_B1 ** ADAM_STEP)
    v_hat = v / (1.0 - ADAM_B2 ** ADAM_STEP)
    delta = -ADAM_LR * (m_hat / (_jnp.sqrt(v_hat) + ADAM_EPS) + ADAM_WD * w)
    return delta, m, v


def reference(x, c, ctx, c_ctx, ada_w, ada_b, norm1, norm2, norm_f, mlp_w1, mlp_w2, hgrn_w_in, hgrn_lb, hgrn_gnorm, hgrn_w_out, conv_w_in, conv_w, conv_b, conv_w_out, loss_target, m_c_ctx, m_ada_w, m_ada_b, m_norm1, m_norm2, m_norm_f, m_mlp_w1, m_mlp_w2, m_hgrn_w_in, m_hgrn_lb, m_hgrn_gnorm, m_hgrn_w_out, m_conv_w_in, m_conv_w, m_conv_b, m_conv_w_out, v_c_ctx, v_ada_w, v_ada_b, v_norm1, v_norm2, v_norm_f, v_mlp_w1, v_mlp_w2, v_hgrn_w_in, v_hgrn_lb, v_hgrn_gnorm, v_hgrn_w_out, v_conv_w_in, v_conv_w, v_conv_b, v_conv_w_out):
    given = dict(x=x, c=c, ctx=ctx, c_ctx=c_ctx, ada_w=ada_w, ada_b=ada_b, norm1=norm1, norm2=norm2, norm_f=norm_f, mlp_w1=mlp_w1, mlp_w2=mlp_w2, hgrn_w_in=hgrn_w_in, hgrn_lb=hgrn_lb, hgrn_gnorm=hgrn_gnorm, hgrn_w_out=hgrn_w_out, conv_w_in=conv_w_in, conv_w=conv_w, conv_b=conv_b, conv_w_out=conv_w_out, loss_target=loss_target, m_c_ctx=m_c_ctx, m_ada_w=m_ada_w, m_ada_b=m_ada_b, m_norm1=m_norm1, m_norm2=m_norm2, m_norm_f=m_norm_f, m_mlp_w1=m_mlp_w1, m_mlp_w2=m_mlp_w2, m_hgrn_w_in=m_hgrn_w_in, m_hgrn_lb=m_hgrn_lb, m_hgrn_gnorm=m_hgrn_gnorm, m_hgrn_w_out=m_hgrn_w_out, m_conv_w_in=m_conv_w_in, m_conv_w=m_conv_w, m_conv_b=m_conv_b, m_conv_w_out=m_conv_w_out, v_c_ctx=v_c_ctx, v_ada_w=v_ada_w, v_ada_b=v_ada_b, v_norm1=v_norm1, v_norm2=v_norm2, v_norm_f=v_norm_f, v_mlp_w1=v_mlp_w1, v_mlp_w2=v_mlp_w2, v_hgrn_w_in=v_hgrn_w_in, v_hgrn_lb=v_hgrn_lb, v_hgrn_gnorm=v_hgrn_gnorm, v_hgrn_w_out=v_hgrn_w_out, v_conv_w_in=v_conv_w_in, v_conv_w=v_conv_w, v_conv_b=v_conv_b, v_conv_w_out=v_conv_w_out)
    weights = {n: given[n] for n in TWIN_WEIGHTS}
    shared = {n: given[n] for n in SHARED_INPUTS}
    per_example = {n: given[n] for n in ['x', 'c', 'ctx']}
    grad_fn = _jax.value_and_grad(_loss, argnums=(0, 1))

    def one_microbatch(ex, loss_target):
        ex = dict(ex)
        diff = ex.pop(TWIN_DIFF_INPUT)
        return grad_fn(weights, diff, {**shared, **ex}, loss_target)

    if N_MICROBATCH == 1:
        loss, (grad_w, grad_x) = one_microbatch(per_example, given["loss_target"])
    else:
        def body(carry, xs):
            loss_sum, grad_sum = carry
            l_k, (gw_k, gx_k) = one_microbatch(xs[0], xs[1])
            with _jax.named_scope("update"):
                return (loss_sum + l_k, _jax.tree.map(_jnp.add, grad_sum, gw_k)), gx_k

        init = (_jnp.zeros((), _jnp.float32), _jax.tree.map(_jnp.zeros_like, weights))
        (loss, grad_w), grad_x = _jax.lax.scan(body, init, (per_example, given["loss_target"]))
    with _jax.named_scope("update"):
        delta_w, new_m, new_v = {}, {}, {}
        for n in TWIN_WEIGHTS:
            delta_w[n], new_m[n], new_v[n] = _adamw(weights[n], grad_w[n], given["m_" + n], given["v_" + n])
    return (loss, grad_x, *[grad_w[n] for n in TWIN_WEIGHTS], *[delta_w[n] for n in TWIN_WEIGHTS],
            *[new_m[n] for n in TWIN_WEIGHTS], *[new_v[n] for n in TWIN_WEIGHTS])
```

```python
import functools

import jax
import jax.numpy as jnp
from jax import lax
from jax.experimental import pallas as pl
from jax.experimental.pallas import tpu as pltpu

F32 = jnp.float32
BF16 = jnp.bfloat16
MESH = pl.DeviceIdType.MESH

D = 1024
HD = 128
NH = D // HD
CH = 64
TM = 256
TMW = 768
EPS = 1e-6
DEPTH = 4
VMEM_LIMIT = 56 * 1024 * 1024

ADAM_LR = 0.001
ADAM_B1 = 0.9
ADAM_B2 = 0.999
ADAM_EPS = 1e-08
ADAM_WD = 0.01
ADAM_STEP = 10


def _cp(n_grid):
    return pltpu.CompilerParams(dimension_semantics=("arbitrary",) * n_grid, vmem_limit_bytes=VMEM_LIMIT)


def _dot(a, b):
    return jnp.dot(a, b, preferred_element_type=F32)


def _dot_nt(a, b):
    return lax.dot_general(a, b, (((1,), (1,)), ((), ())), preferred_element_type=F32)


def _dot_tn(a, b):
    return lax.dot_general(a, b, (((0,), (0,)), ((), ())), preferred_element_type=F32)


def _sigmoid(z):
    return 1.0 / (1.0 + jnp.exp(-z))


def _norm_mod(x, gain, sh, sc):
    r = lax.rsqrt(jnp.mean(x * x, axis=-1, keepdims=True) + EPS)
    xn = x * r
    yn = xn * gain
    return r, xn, yn, yn * (1.0 + sc) + sh


def _row_spec(width):
    return pl.BlockSpec((TM, width), lambda i: (i, 0))


def _col_spec(col):
    return pl.BlockSpec((TM, D), lambda i: (i, col))


def _full_spec(shape):
    nd = len(shape)
    return pl.BlockSpec(shape, lambda i: (0,) * nd)


def _mod_spec(n_lat):
    return pl.BlockSpec((1, 8, D), lambda i: (i // n_lat, 0, 0))


def proj_fwd(x, gain, mod, m0, w4, n_lat, name):
    t = x.shape[0]
    nb, _, ns = w4.shape

    def body(x_ref, gain_ref, mod_ref, w_ref, p_ref):
        _, _, _, h = _norm_mod(x_ref[...], gain_ref[...], mod_ref[0, m0:m0 + 1, :], mod_ref[0, m0 + 1:m0 + 2, :])
        hb = h.astype(BF16)
        for c in range(nb):
            p_ref[:, c * ns:(c + 1) * ns] = _dot(hb, w_ref[c])

    return pl.pallas_call(
        body, name=name, grid=(t // TM,),
        in_specs=[_row_spec(D), _full_spec((1, D)), _mod_spec(n_lat), _full_spec(w4.shape)],
        out_specs=_row_spec(nb * ns),
        out_shape=jax.ShapeDtypeStruct((t, nb * ns), F32),
        compiler_params=_cp(1),
    )(x, gain, mod, w4)


def proj_bwd(parts, w4, x, gain, mod, m0, dx_in, n_lat, name):
    t = x.shape[0]
    nb, _, ns = w4.shape
    n = nb * ns
    n_parts = len(parts)
    widths = [p.shape[1] for p in parts]
    offs = [sum(widths[:k]) for k in range(n_parts)]
    assert sum(widths) == n
    single = n_parts == 1

    def body(*refs):
        part_refs = refs[:n_parts]
        w_ref, x_ref, gain_ref, mod_ref, dxin_ref = refs[n_parts:n_parts + 5]
        rest = refs[n_parts + 5:]
        if single:
            dx_ref, hb_ref, acc_ref = rest
            src = part_refs[0]
        else:
            dx_ref, hb_ref, acc_ref, dpb_ref = rest
            for p_ref, off, w in zip(part_refs, offs, widths):
                dpb_ref[:, off:off + w] = p_ref[...]
            src = dpb_ref
        i = pl.program_id(0)

        @pl.when(i == 0)
        def _():
            acc_ref[...] = jnp.zeros_like(acc_ref)

        gain = gain_ref[...]
        sc = mod_ref[0, m0 + 1:m0 + 2, :]
        r, xn, yn, h = _norm_mod(x_ref[...], gain, mod_ref[0, m0:m0 + 1, :], sc)
        hb_ref[...] = h.astype(BF16)
        dh = _dot_nt(src[:, 0:ns], w_ref[0])
        for c in range(1, nb):
            dh = dh + _dot_nt(src[:, c * ns:(c + 1) * ns], w_ref[c])
        dsh = jnp.sum(dh, axis=0, keepdims=True)
        dsc = jnp.sum(dh * yn, axis=0, keepdims=True)
        dyn = dh * (1.0 + sc)
        dgain = jnp.sum(dyn * xn, axis=0, keepdims=True)
        dxn = dyn * gain
        dx = r * (dxn - xn * jnp.mean(dxn * xn, axis=-1, keepdims=True))
        dx_ref[...] = dxin_ref[...] + dx
        latf = (i < n_lat).astype(F32)
        ctxf = 1.0 - latf
        acc_ref[0:1, :] += dgain
        acc_ref[1:2, :] += dsh * latf
        acc_ref[2:3, :] += dsc * latf
        acc_ref[3:4, :] += dsh * ctxf
        acc_ref[4:5, :] += dsc * ctxf

    out_specs = [_row_spec(D), _row_spec(D), _full_spec((8, D))]
    out_shape = [jax.ShapeDtypeStruct((t, D), F32), jax.ShapeDtypeStruct((t, D), BF16), jax.ShapeDtypeStruct((8, D), F32)]
    if not single:
        out_specs.append(_row_spec(n))
        out_shape.append(jax.ShapeDtypeStruct((t, n), BF16))
    outs = pl.pallas_call(
        body, name=name, grid=(t // TM,),
        in_specs=[_row_spec(w) for w in widths]
        + [_full_spec(w4.shape), _row_spec(D), _full_spec((1, D)), _mod_spec(n_lat), _row_spec(D)],
        out_specs=out_specs, out_shape=out_shape, compiler_params=_cp(1),
    )(*parts, w4, x, gain, mod, dx_in)
    if single:
        return outs[0], outs[1], parts[0], outs[2]
    return outs[0], outs[1], outs[3], outs[2]


def dw_tn(a, b, nb, a_blocked, name):
    t = a.shape[0]
    ka = a.shape[1] // nb if a_blocked else a.shape[1]
    kb = b.shape[1] if a_blocked else b.shape[1] // nb
    n_k = t // TMW

    def body(a_ref, b_ref, o_ref, acc):
        k = pl.program_id(1)

        @pl.when(k == 0)
        def _():
            acc[...] = jnp.zeros_like(acc)

        acc[...] += _dot_tn(a_ref[...], b_ref[...])

        @pl.when(k == n_k - 1)
        def _():
            o_ref[0] = acc[...].astype(BF16)

    a_spec = pl.BlockSpec((TMW, ka), (lambda j, k: (k, j)) if a_blocked else (lambda j, k: (k, 0)))
    b_spec = pl.BlockSpec((TMW, kb), (lambda j, k: (k, 0)) if a_blocked else (lambda j, k: (k, j)))
    return pl.pallas_call(
        body, name=name, grid=(nb, n_k),
        in_specs=[a_spec, b_spec],
        out_specs=pl.BlockSpec((1, ka, kb), lambda j, k: (j, 0, 0)),
        out_shape=jax.ShapeDtypeStruct((nb, ka, kb), BF16),
        scratch_shapes=[pltpu.VMEM((ka, kb), F32)],
        compiler_params=_cp(2),
    )(a, b)


def outproj_fwd(prologue, extras, extra_specs, w, x, mod, m0, n_lat, name):
    t = x.shape[0]
    k = w.shape[0]
    n_extra = len(extras)

    def body(*refs):
        ex = refs[:n_extra]
        w_ref, x_ref, mod_ref, xo_ref, y_ref, ab_ref = refs[n_extra:]
        ab = prologue(pl.program_id(0), *ex).astype(BF16)
        ab_ref[...] = ab
        y = _dot(ab, w_ref[...])
        y_ref[...] = y
        xo_ref[...] = x_ref[...] + mod_ref[0, m0 + 2:m0 + 3, :] * y

    return pl.pallas_call(
        body, name=name, grid=(t // TM,),
        in_specs=list(extra_specs) + [_full_spec(w.shape), _row_spec(D), _mod_spec(n_lat)],
        out_specs=[_row_spec(D), _row_spec(D), _row_spec(k)],
        out_shape=[jax.ShapeDtypeStruct((t, D), F32), jax.ShapeDtypeStruct((t, D), F32), jax.ShapeDtypeStruct((t, k), BF16)],
        compiler_params=_cp(1),
    )(*extras, w, x, mod)


def outproj_bwd(epilogue, extras, extra_specs, ep_out_specs, ep_out_shapes, w, dxn, y, mod, m0, n_lat, name):
    t = dxn.shape[0]
    n_extra = len(extras)

    def body(*refs):
        ex = refs[:n_extra]
        w_ref, dxn_ref, y_ref, mod_ref, dyb_ref, acc_ref = refs[n_extra:n_extra + 6]
        ep_outs = refs[n_extra + 6:]
        i = pl.program_id(0)

        @pl.when(i == 0)
        def _():
            acc_ref[...] = jnp.zeros_like(acc_ref)

        dxv = dxn_ref[...]
        dyb = (dxv * mod_ref[0, m0 + 2:m0 + 3, :]).astype(BF16)
        dyb_ref[...] = dyb
        dg = jnp.sum(dxv * y_ref[...], axis=0, keepdims=True)
        latf = (i < n_lat).astype(F32)
        acc_ref[0:1, :] += dg * latf
        acc_ref[1:2, :] += dg * (1.0 - latf)
        epilogue(i, _dot_nt(dyb, w_ref[...]), ex, ep_outs, acc_ref)

    outs = pl.pallas_call(
        body, name=name, grid=(t // TM,),
        in_specs=list(extra_specs) + [_full_spec(w.shape), _row_spec(D), _row_spec(D), _mod_spec(n_lat)],
        out_specs=[_row_spec(D), _full_spec((8, D))] + list(ep_out_specs),
        out_shape=[jax.ShapeDtypeStruct((t, D), BF16), jax.ShapeDtypeStruct((8, D), F32)] + list(ep_out_shapes),
        compiler_params=_cp(1),
    )(*extras, w, dxn, y, mod)
    return outs[0], outs[1], outs[2:]


def mlp_prologue(i, p_ref):
    return jnp.square(jnp.maximum(p_ref[...], 0.0))


def mlp_epilogue(i, da, ex, outs, acc_ref):
    outs[0][...] = (da * (2.0 * jnp.maximum(ex[0][...], 0.0))).astype(BF16)


def readout_prologue(i, o0_ref, o1_ref, gate_ref, gn_ref):
    o = o0_ref[...] + o1_ref[...]
    gate = gate_ref[...]
    w = gn_ref[...] * (gate * _sigmoid(gate))
    pieces = []
    for h in range(NH):
        sl = slice(h * HD, (h + 1) * HD)
        oh = o[:, sl]
        pieces.append(oh * lax.rsqrt(jnp.mean(oh * oh, axis=-1, keepdims=True) + EPS) * w[:, sl])
    return jnp.concatenate(pieces, axis=1)


def readout_epilogue(i, da, ex, outs, acc_ref):
    o0_ref, o1_ref, gate_ref, gn_ref = ex
    do_ref, dgate_ref = outs
    o = o0_ref[...] + o1_ref[...]
    gate = gate_ref[...]
    gn = gn_ref[...]
    sg = _sigmoid(gate)
    silu = gate * sg
    dsilu = sg * (1.0 + gate * (1.0 - sg))
    for h in range(NH):
        sl = slice(h * HD, (h + 1) * HD)
        oh = o[:, sl]
        r = lax.rsqrt(jnp.mean(oh * oh, axis=-1, keepdims=True) + EPS)
        nh = oh * r
        dah = da[:, sl]
        acc_ref[2:3, sl] += jnp.sum(dah * nh * silu[:, sl], axis=0, keepdims=True)
        dgate_ref[:, sl] = (dah * nh * gn[:, sl] * dsilu[:, sl]).astype(BF16)
        dn = dah * gn[:, sl] * silu[:, sl]
        do_ref[:, sl] = r * (dn - nh * jnp.mean(dn * nh, axis=-1, keepdims=True))


def _seg_masks(i, n_lat):
    rows = lax.broadcasted_iota(jnp.int32, (TM, 1), 0)
    latf = (i < n_lat).astype(F32)
    ctxf = 1.0 - latf
    prev_ok = (rows % CH != 0).astype(F32) * latf + (rows != 0).astype(F32) * ctxf
    next_ok = (rows % CH != CH - 1).astype(F32) * latf + (rows != TM - 1).astype(F32) * ctxf
    return prev_ok, next_ok


def _shifts(i, n_lat, sft, cur, halo_prev, halo_next):
    if sft == 1:
        prev_ok, next_ok = _seg_masks(i, n_lat)
        return pltpu.roll(cur, 1, 0) * prev_ok, pltpu.roll(cur, TM - 1, 0) * next_ok
    has_prev = jnp.logical_and(i > 0, i < n_lat).astype(F32)
    has_next = (i < n_lat - 1).astype(F32)
    prev = jnp.concatenate([halo_prev * has_prev, cur[:TM - CH]], axis=0)
    nxt = jnp.concatenate([cur[CH:], halo_next * has_next], axis=0)
    return prev, nxt


def _conv_u(sft, ex):
    if sft == 1:
        gb_ref, gc_ref, xi_ref, cw_ref, cb_ref = ex
        return gb_ref, gc_ref[...] * xi_ref[...], None, None, cw_ref, cb_ref
    gb_ref, gc_ref, xi_ref, gcp_ref, xip_ref, gcn_ref, xin_ref, cw_ref, cb_ref = ex
    return gb_ref, gc_ref[...] * xi_ref[...], gcp_ref[...] * xip_ref[...], gcn_ref[...] * xin_ref[...], cw_ref, cb_ref


def _conv_value(i, n_lat, sft, ex):
    gb_ref, u, up, un, cw_ref, cb_ref = _conv_u(sft, ex)
    u_prev, u_next = _shifts(i, n_lat, sft, u, up, un)
    return gb_ref, cb_ref[...] + cw_ref[0:1, :] * u_prev + cw_ref[1:2, :] * u + cw_ref[2:3, :] * u_next


def make_conv_prologue(n_lat, sft):
    def prologue(i, *ex):
        gb_ref, conv = _conv_value(i, n_lat, sft, ex)
        return gb_ref[...] * conv
    return prologue


def make_conv_epilogue(n_lat, sft):
    def epilogue(i, da, ex, outs, acc_ref):
        gb_ref, conv = _conv_value(i, n_lat, sft, ex)
        outs[0][...] = da * gb_ref[...]
        outs[1][...] = (da * conv).astype(BF16)
    return epilogue


def _conv_specs(sft, t):
    specs = [_col_spec(0), _col_spec(1), _col_spec(2)]
    if sft != 1:
        per = TM // CH
        last = t // CH - 1
        for fn in (lambda i: jnp.maximum(i * per - 1, 0), lambda i: jnp.minimum(i * per + per, last)):
            for col in (1, 2):
                specs.append(pl.BlockSpec((CH, D), functools.partial(lambda i, f, c: (f(i), c), f=fn, c=col)))
    return specs + [_full_spec((8, D)), _full_spec((1, D))]


def _conv_args(sft, p, cw8, cb):
    return [p] * (3 if sft == 1 else 7) + [cw8, cb]


def conv_bwd(dconv, p, cw8, sft, n_lat, name):
    t = dconv.shape[0]
    halo = sft != 1

    def body(*refs):
        if halo:
            dc_ref, dcp_ref, dcn_ref, gc_ref, xi_ref, gcp_ref, xip_ref, gcn_ref, xin_ref, cw_ref, dgc_ref, dxi_ref, acc_ref = refs
            up, un = gcp_ref[...] * xip_ref[...], gcn_ref[...] * xin_ref[...]
            dcp, dcn = dcp_ref[...], dcn_ref[...]
        else:
            dc_ref, gc_ref, xi_ref, cw_ref, dgc_ref, dxi_ref, acc_ref = refs
            up = un = dcp = dcn = None
        i = pl.program_id(0)

        @pl.when(i == 0)
        def _():
            acc_ref[...] = jnp.zeros_like(acc_ref)

        gc = gc_ref[...]
        xi = xi_ref[...]
        u = gc * xi
        dc = dc_ref[...]
        u_prev, u_next = _shifts(i, n_lat, sft, u, up, un)
        dc_prev, dc_next = _shifts(i, n_lat, sft, dc, dcp, dcn)
        acc_ref[0:1, :] += jnp.sum(dc * u_prev, axis=0, keepdims=True)
        acc_ref[1:2, :] += jnp.sum(dc * u, axis=0, keepdims=True)
        acc_ref[2:3, :] += jnp.sum(dc * u_next, axis=0, keepdims=True)
        acc_ref[3:4, :] += jnp.sum(dc, axis=0, keepdims=True)
        du = cw_ref[0:1, :] * dc_next + cw_ref[1:2, :] * dc + cw_ref[2:3, :] * dc_prev
        dgc_ref[...] = (du * xi).astype(BF16)
        dxi_ref[...] = (du * gc).astype(BF16)

    per = TM // CH
    last = t // CH - 1
    prev_i = lambda i: jnp.maximum(i * per - 1, 0)
    next_i = lambda i: jnp.minimum(i * per + per, last)
    if halo:
        in_specs = [_row_spec(D), pl.BlockSpec((CH, D), lambda i: (prev_i(i), 0)), pl.BlockSpec((CH, D), lambda i: (next_i(i), 0)),
                    _col_spec(1), _col_spec(2),
                    pl.BlockSpec((CH, D), lambda i: (prev_i(i), 1)), pl.BlockSpec((CH, D), lambda i: (prev_i(i), 2)),
                    pl.BlockSpec((CH, D), lambda i: (next_i(i), 1)), pl.BlockSpec((CH, D), lambda i: (next_i(i), 2)),
                    _full_spec((8, D))]
        args = [dconv, dconv, dconv, p, p, p, p, p, p, cw8]
    else:
        in_specs = [_row_spec(D), _col_spec(1), _col_spec(2), _full_spec((8, D))]
        args = [dconv, p, p, cw8]
    return pl.pallas_call(
        body, name=name, grid=(t // TM,), in_specs=in_specs,
        out_specs=[_row_spec(D), _row_spec(D), _full_spec((8, D))],
        out_shape=[jax.ShapeDtypeStruct((t, D), BF16), jax.ShapeDtypeStruct((t, D), BF16), jax.ShapeDtypeStruct((8, D), F32)],
        compiler_params=_cp(1),
    )(*args)


def _chunk_cumsum(g, reverse):
    rows = lax.broadcasted_iota(jnp.int32, (g.shape[0], 1), 0) % CH
    n = g.shape[0]
    s = 1
    while s < CH:
        if reverse:
            g = g + pltpu.roll(g, n - s, 0) * (rows < CH - s).astype(F32)
        else:
            g = g + pltpu.roll(g, s, 0) * (rows >= s).astype(F32)
        s *= 2
    return g


def _gate_values(z, lb):
    sig = _sigmoid(z)
    f = lb + (1.0 - lb) * sig
    return sig, f


def _tri(direction, transposed):
    r = lax.broadcasted_iota(jnp.int32, (CH, CH), 0)
    c = lax.broadcasted_iota(jnp.int32, (CH, CH), 1)
    lower = (direction == 0) != transposed
    return r >= c if lower else r <= c


def _gla_rows(direction):
    return (CH // 2 - 1, CH - 1) if direction == 0 else (CH // 2, 0)


def gla_fwd(p, lb2, direction, name):
    t = p.shape[0]
    nt = t // TM
    per = TM // CH
    ref_row, last_row = _gla_rows(direction)
    tile = (lambda i: (i + nt - 1) % nt) if direction == 0 else (lambda i: nt - 1 - i)

    def body(z_ref, v_ref, qr_ref, lb_ref, o_ref, s_ref, st, q_s, k_s, c_s):
        @pl.when(pl.program_id(0) == 0)
        def _():
            st[...] = jnp.zeros_like(st)

        _, f = _gate_values(z_ref[...], lb_ref[direction:direction + 1, :])
        k_s[...] = 1.0 - f
        c_s[...] = _chunk_cumsum(jnp.log(f), direction == 1)
        qr = qr_ref[...]
        q_s[...] = qr * _sigmoid(qr)
        mask = _tri(direction, False)
        for it in range(per):
            ci = it if direction == 0 else per - 1 - it
            r0 = ci * CH
            cum = c_s[r0:r0 + CH, :]
            ref = c_s[r0 + ref_row:r0 + ref_row + 1, :]
            last = c_s[r0 + last_row:r0 + last_row + 1, :]
            q = q_s[r0:r0 + CH, :]
            k = k_s[r0:r0 + CH, :]
            qh = (q * jnp.exp(cum)).astype(BF16)
            qt = (q * jnp.exp(cum - ref)).astype(BF16)
            kt = (k * jnp.exp(ref - cum)).astype(BF16)
            kb = (k * jnp.exp(last - cum)).astype(BF16)
            el = jnp.exp(last)
            vb = v_ref[r0:r0 + CH, :].astype(BF16)
            for h in range(NH):
                sl = slice(h * HD, (h + 1) * HD)
                s_t = st[h]
                s_ref[ci, h] = s_t
                sc = jnp.where(mask, _dot_nt(qt[:, sl], kt[:, sl]), 0.0)
                o_ref[r0:r0 + CH, sl] = _dot_nt(qh[:, sl], s_t.astype(BF16)) + _dot(sc.astype(BF16), vb[:, sl])
                st[h] = s_t * el[:, sl] + _dot_tn(vb[:, sl], kb[:, sl])

    tspec = lambda col: pl.BlockSpec((TM, D), lambda i: (tile(i), col))
    return pl.pallas_call(
        body, name=name, grid=(nt,),
        in_specs=[tspec(direction), tspec(2), tspec(3), _full_spec((2, D))],
        out_specs=[pl.BlockSpec((TM, D), lambda i: (tile(i), 0)), pl.BlockSpec((per, NH, HD, HD), lambda i: (tile(i), 0, 0, 0))],
        out_shape=[jax.ShapeDtypeStruct((t, D), F32), jax.ShapeDtypeStruct((t // CH, NH, HD, HD), F32)],
        scratch_shapes=[pltpu.VMEM((NH, HD, HD), F32), pltpu.VMEM((TM, D), F32), pltpu.VMEM((TM, D), F32), pltpu.VMEM((TM, D), F32)],
        compiler_params=_cp(1),
    )(p, p, p, lb2)


def gla_bwd(p, lb2, do, states, direction, prev, name):
    t = p.shape[0]
    nt = t // TM
    per = TM // CH
    ref_row, last_row = _gla_rows(direction)
    tile = (lambda i: (2 * nt - 2 - i) % nt) if direction == 0 else (lambda i: i)
    final = prev is not None
    n_in = 8 if final else 6

    def body(*refs):
        z_ref, v_ref, qr_ref, lb_ref, do_ref, s_ref = refs[:6]
        dz_ref, dv_ref, dq_ref, acc_ref, dst, q_s, k_s, c_s, dq_s, dk_s, dl_s = refs[n_in:]

        @pl.when(pl.program_id(0) == 0)
        def _():
            dst[...] = jnp.zeros_like(dst)
            acc_ref[...] = jnp.zeros_like(acc_ref)

        lb = lb_ref[direction:direction + 1, :]
        sig, f = _gate_values(z_ref[...], lb)
        k_s[...] = 1.0 - f
        c_s[...] = _chunk_cumsum(jnp.log(f), direction == 1)
        qr = qr_ref[...]
        sq = _sigmoid(qr)
        q_s[...] = qr * sq
        mask = _tri(direction, False)
        mask_t = _tri(direction, True)
        is_last = lax.broadcasted_iota(jnp.int32, (CH, 1), 0) == last_row
        for it in range(per):
            ci = per - 1 - it if direction == 0 else it
            r0 = ci * CH
            cum = c_s[r0:r0 + CH, :]
            ref = c_s[r0 + ref_row:r0 + ref_row + 1, :]
            last = c_s[r0 + last_row:r0 + last_row + 1, :]
            q = q_s[r0:r0 + CH, :]
            k = k_s[r0:r0 + CH, :]
            e_h = jnp.exp(cum)
            e_t = jnp.exp(cum - ref)
            e_kt = jnp.exp(ref - cum)
            e_kb = jnp.exp(last - cum)
            el = jnp.exp(last)
            qh = (q * e_h).astype(BF16)
            qt = (q * e_t).astype(BF16)
            kt = (k * e_kt).astype(BF16)
            kbf = k * e_kb
            kb = kbf.astype(BF16)
            vb = v_ref[r0:r0 + CH, :].astype(BF16)
            dob = do_ref[r0:r0 + CH, :].astype(BF16)
            for h in range(NH):
                sl = slice(h * HD, (h + 1) * HD)
                s_t = s_ref[ci, h]
                ds_t = dst[h]
                ds_b = ds_t.astype(BF16)
                d_a = jnp.where(mask, _dot_nt(dob[:, sl], vb[:, sl]), 0.0).astype(BF16)
                a_t = jnp.where(mask_t, _dot_nt(kt[:, sl], qt[:, sl]), 0.0).astype(BF16)
                d_at = jnp.where(mask_t, _dot_nt(vb[:, sl], dob[:, sl]), 0.0).astype(BF16)
                dv = _dot(a_t, dob[:, sl]) + _dot_nt(kb[:, sl], ds_b)
                dqh = _dot(dob[:, sl], s_t.astype(BF16))
                dqt = _dot(d_a, kt[:, sl])
                dkt = _dot(d_at, qt[:, sl])
                dkb = _dot(vb[:, sl], ds_b)
                dl_s[it:it + 1, sl] = (el[:, sl] * jnp.sum(ds_t * s_t, axis=0, keepdims=True)
                                       + jnp.sum(dkb * kbf[:, sl], axis=0, keepdims=True))
                dst[h] = ds_t * el[:, sl] + _dot_tn(dob[:, sl], qh[:, sl])
                dq_s[r0:r0 + CH, sl] = dqh * e_h[:, sl] + dqt * e_t[:, sl]
                dk_s[r0:r0 + CH, sl] = dkt * e_kt[:, sl] + dkb * e_kb[:, sl]
                if final:
                    dv_ref[r0:r0 + CH, sl] = (refs[6][r0:r0 + CH, sl] + dv).astype(BF16)
                else:
                    dv_ref[r0:r0 + CH, sl] = dv
            dcum = dq_s[r0:r0 + CH, :] * q - dk_s[r0:r0 + CH, :] * k
            c_s[r0:r0 + CH, :] = dcum + jnp.where(is_last, dl_s[it:it + 1, :], 0.0)
        dg = _chunk_cumsum(c_s[...], direction == 0)
        df = dg / f - dk_s[...]
        acc_ref[0:1, :] += jnp.sum(df * (1.0 - sig), axis=0, keepdims=True)
        dz_ref[...] = (df * (1.0 - lb) * sig * (1.0 - sig)).astype(BF16)
        if final:
            dq_ref[...] = ((refs[7][...] + dq_s[...]) * (sq * (1.0 + qr * (1.0 - sq)))).astype(BF16)
        else:
            dq_ref[...] = dq_s[...]

    tspec = lambda col: pl.BlockSpec((TM, D), lambda i: (tile(i), col))
    sspec = pl.BlockSpec((per, NH, HD, HD), lambda i: (tile(i), 0, 0, 0))
    in_specs = [tspec(direction), tspec(2), tspec(3), _full_spec((2, D)), tspec(0), sspec]
    args = [p, p, p, lb2, do, states]
    if final:
        in_specs += [tspec(0), tspec(0)]
        args += list(prev)
    odt = BF16 if final else F32
    return pl.pallas_call(
        body, name=name, grid=(nt,), in_specs=in_specs,
        out_specs=[tspec(0), tspec(0), tspec(0), _full_spec((8, D))],
        out_shape=[jax.ShapeDtypeStruct((t, D), BF16), jax.ShapeDtypeStruct((t, D), odt), jax.ShapeDtypeStruct((t, D), odt),
                   jax.ShapeDtypeStruct((8, D), F32)],
        scratch_shapes=[pltpu.VMEM((NH, HD, HD), F32)] + [pltpu.VMEM((TM, D), F32)] * 5 + [pltpu.VMEM((8, D), F32)],
        compiler_params=_cp(1),
    )(*args)


def loss_bwd(x, gain, target, n_lat, name):
    t = x.shape[0]

    def body(x_ref, gain_ref, tg_ref, dx_ref, acc_ref):
        i = pl.program_id(0)

        @pl.when(i == 0)
        def _():
            acc_ref[...] = jnp.zeros_like(acc_ref)

        latf = (i < n_lat).astype(F32)
        x = x_ref[...]
        gain = gain_ref[...]
        r = lax.rsqrt(jnp.mean(x * x, axis=-1, keepdims=True) + EPS)
        xn = x * r
        err = (xn * gain - tg_ref[...]) * latf
        dy = err * (1.0 / D)
        dxn = dy * gain
        dx_ref[...] = r * (dxn - xn * jnp.mean(dxn * xn, axis=-1, keepdims=True))
        acc_ref[0:1, :] += jnp.sum(dy * xn, axis=0, keepdims=True)
        acc_ref[1:2, :] += jnp.sum(err * err, axis=0, keepdims=True)

    return pl.pallas_call(
        body, name=name, grid=(t // TM,),
        in_specs=[_row_spec(D), _full_spec((1, D)), pl.BlockSpec((TM, D), lambda i: (jnp.minimum(i, n_lat - 1), 0))],
        out_specs=[_row_spec(D), _full_spec((8, D))],
        out_shape=[jax.ShapeDtypeStruct((t, D), F32), jax.ShapeDtypeStruct((8, D), F32)],
        compiler_params=_cp(1),
    )(x, gain, target)


def local_step(xs, target, mods, norm1, norm2, norm_f, lbs, gnorm, cw8, cb, wts, n_lat):
    t = xs.shape[0]
    saved = []
    x = xs
    for i in range(DEPTH):
        j = i // 2
        rec = i % 2 == 0
        n1 = norm1[i:i + 1]
        n2 = norm2[i:i + 1]
        s = {"x_in": x}
        if rec:
            p = proj_fwd(x, n1, mods[i], 0, wts["hin"][j], n_lat, f"hin_fwd_{i}")
            o0, st0 = gla_fwd(p, lbs[j], 0, f"gla_fwd0_{i}")
            o1, st1 = gla_fwd(p, lbs[j], 1, f"gla_fwd1_{i}")
            ex = [o0, o1, p, gnorm[j:j + 1]]
            ex_specs = [_row_spec(D), _row_spec(D), _col_spec(4), _full_spec((1, D))]
            xm, y, ab = outproj_fwd(readout_prologue, ex, ex_specs, wts["hout"][j], x, mods[i], 0, n_lat, f"hout_fwd_{i}")
            s.update(st0=st0, st1=st1)
        else:
            sft = 1 if j % 2 == 0 else CH
            p = proj_fwd(x, n1, mods[i], 0, wts["cin"][j], n_lat, f"cin_fwd_{i}")
            ex = _conv_args(sft, p, cw8[j], cb[j])
            ex_specs = _conv_specs(sft, t)
            xm, y, ab = outproj_fwd(make_conv_prologue(n_lat, sft), ex, ex_specs, wts["cout"][j], x, mods[i], 0, n_lat, f"cout_fwd_{i}")
        s.update(p=p, ex=ex, ex_specs=ex_specs, y_mix=y, ab_mix=ab, x_mid=xm)
        p1 = proj_fwd(xm, n2, mods[i], 3, wts["w1"][i], n_lat, f"w1_fwd_{i}")
        x, y2, ab2 = outproj_fwd(mlp_prologue, [p1], [_row_spec(4 * D)], wts["w2"][i], xm, mods[i], 3, n_lat, f"w2_fwd_{i}")
        s.update(p1=p1, y_mlp=y2, ab_mlp=ab2)
        saved.append(s)

    dx, acc_loss = loss_bwd(x, norm_f, target, n_lat, "loss")
    g = {k: [None] * len(v) for k, v in wts.items()}
    small = {"norm_f": acc_loss[0:1], "norm1": [None] * DEPTH, "norm2": [None] * DEPTH, "dmod": [None] * DEPTH,
             "gnorm": [None] * 2, "lb": [None] * 2, "cw": [None] * 2, "cb": [None] * 2}
    bshape = lambda w: jax.ShapeDtypeStruct((t, w), BF16)
    for i in reversed(range(DEPTH)):
        j = i // 2
        rec = i % 2 == 0
        s = saved[i]
        n1 = norm1[i:i + 1]
        n2 = norm2[i:i + 1]
        dyb, acc_g2, (dp1,) = outproj_bwd(mlp_epilogue, [s["p1"]], [_row_spec(4 * D)], [_row_spec(4 * D)], [bshape(4 * D)],
                                          wts["w2"][i], dx, s["y_mlp"], mods[i], 3, n_lat, f"w2_bwd_{i}")
        g["w2"][i] = dw_tn(s["ab_mlp"], dyb, 4, True, f"w2_dw_{i}")
        dx, hb, dpb, acc_n2 = proj_bwd([dp1], wts["w1"][i], s["x_mid"], n2, mods[i], 3, dx, n_lat, f"w1_bwd_{i}")
        g["w1"][i] = dw_tn(hb, dpb, 4, False, f"w1_dw_{i}")
        if rec:
            dyb, acc_g1, (do, dgate) = outproj_bwd(
                readout_epilogue, s["ex"], s["ex_specs"], [_row_spec(D), _row_spec(D)],
                [jax.ShapeDtypeStruct((t, D), F32), bshape(D)], wts["hout"][j], dx, s["y_mix"], mods[i], 0, n_lat, f"hout_bwd_{i}")
            g["hout"][j] = dw_tn(s["ab_mix"], dyb, 1, False, f"hout_dw_{i}")
            dz0, dv0, dq0, acc_l0 = gla_bwd(s["p"], lbs[j], do, s["st0"], 0, None, f"gla_bwd0_{i}")
            dz1, dv, dq, acc_l1 = gla_bwd(s["p"], lbs[j], do, s["st1"], 1, (dv0, dq0), f"gla_bwd1_{i}")
            dx, hb, dpb, acc_n1 = proj_bwd([dz0, dz1, dv, dq, dgate], wts["hin"][j], s["x_in"], n1, mods[i], 0, dx, n_lat, f"hin_bwd_{i}")
            g["hin"][j] = dw_tn(hb, dpb, 4, False, f"hin_dw_{i}")
            small["gnorm"][j] = acc_g1[2:3]
            small["lb"][j] = jnp.concatenate([acc_l0[0:1], acc_l1[0:1]], axis=0)
        else:
            sft = 1 if j % 2 == 0 else CH
            dyb, acc_g1, (dconv, dgb) = outproj_bwd(
                make_conv_epilogue(n_lat, sft), s["ex"], s["ex_specs"], [_row_spec(D), _row_spec(D)],
                [jax.ShapeDtypeStruct((t, D), F32), bshape(D)], wts["cout"][j], dx, s["y_mix"], mods[i], 0, n_lat, f"cout_bwd_{i}")
            g["cout"][j] = dw_tn(s["ab_mix"], dyb, 1, False, f"cout_dw_{i}")
            dgc, dxi, acc_c = conv_bwd(dconv, s["p"], cw8[j], sft, n_lat, f"conv_bwd_{i}")
            dx, hb, dpb, acc_n1 = proj_bwd([dgb, dgc, dxi], wts["cin"][j], s["x_in"], n1, mods[i], 0, dx, n_lat, f"cin_bwd_{i}")
            g["cin"][j] = dw_tn(hb, dpb, 4, False, f"cin_dw_{i}")
            small["cw"][j] = acc_c[0:3]
            small["cb"][j] = acc_c[3:4]
        small["norm1"][i] = acc_n1[0:1]
        small["norm2"][i] = acc_n2[0:1]
        z2 = jnp.zeros((2, D), F32)
        small["dmod"][i] = jnp.concatenate([acc_n1[1:3], acc_g1[0:1], acc_n2[1:3], acc_g2[0:1], z2,
                                            acc_n1[3:5], acc_g1[1:2], acc_n2[3:5], acc_g2[1:2], z2], axis=0)
    return acc_loss[1:2], dx, g, small


RB = 256


def cast_bf16(w2d, name):
    r, c = w2d.shape

    def body(w_ref, o_ref):
        o_ref[...] = w_ref[...].astype(BF16)

    spec = pl.BlockSpec((RB, c), lambda i: (i, 0))
    return pl.pallas_call(body, name=name, grid=(r // RB,), in_specs=[spec], out_specs=spec,
                          out_shape=jax.ShapeDtypeStruct((r, c), BF16), compiler_params=_cp(1))(w2d)


def sum_slots(recv, name):
    _, r, c = recv.shape

    def body(r_ref, o_ref):
        acc = r_ref[0].astype(F32)
        for k in range(1, 4):
            acc = acc + r_ref[k].astype(F32)
        o_ref[...] = acc

    return pl.pallas_call(body, name=name, grid=(r // RB,),
                          in_specs=[pl.BlockSpec((4, RB, c), lambda i: (0, i, 0))],
                          out_specs=pl.BlockSpec((RB, c), lambda i: (i, 0)),
                          out_shape=jax.ShapeDtypeStruct((r, c), F32), compiler_params=_cp(1))(recv)


def _adamw_math(w, g, m, v):
    m = ADAM_B1 * m + (1.0 - ADAM_B1) * g
    v = ADAM_B2 * v + (1.0 - ADAM_B2) * jnp.square(g)
    m_hat = m / (1.0 - ADAM_B1 ** ADAM_STEP)
    v_hat = v / (1.0 - ADAM_B2 ** ADAM_STEP)
    delta = -ADAM_LR * (m_hat / (jnp.sqrt(v_hat) + ADAM_EPS) + ADAM_WD * w)
    return delta, m, v


def adamw(gsrcs, w, m, v, name):
    r, c = w.shape
    rb = RB if r % RB == 0 else r
    n_g = len(gsrcs)

    def body(*refs):
        g = refs[0][...]
        for k in range(1, n_g):
            g = g + refs[k][...]
        w_ref, m_ref, v_ref, g_ref, d_ref, mo_ref, vo_ref = refs[n_g:]
        delta, mo, vo = _adamw_math(w_ref[...], g, m_ref[...], v_ref[...])
        g_ref[...] = g
        d_ref[...] = delta
        mo_ref[...] = mo
        vo_ref[...] = vo

    spec = pl.BlockSpec((rb, c), lambda i: (i, 0))
    shp = jax.ShapeDtypeStruct((r, c), F32)
    return pl.pallas_call(body, name=name, grid=(r // rb,), in_specs=[spec] * (n_g + 3), out_specs=[spec] * 4,
                          out_shape=[shp] * 4, compiler_params=_cp(1))(*gsrcs, w, m, v)


ADA_CB = 512


def ada_fwd(cvec, ada_w, bias, name):
    _, _, nc = ada_w.shape

    def body(c_ref, w_ref, b_ref, o_ref):
        cv = c_ref[...]
        a = (cv * _sigmoid(cv)).astype(BF16)
        o_ref[0] = _dot(a, w_ref[0].astype(BF16)) + b_ref[0]

    return pl.pallas_call(
        body, name=name, grid=(DEPTH, nc // ADA_CB),
        in_specs=[pl.BlockSpec((16, D), lambda i, j: (0, 0)), pl.BlockSpec((1, D, ADA_CB), lambda i, j: (i, 0, j)),
                  pl.BlockSpec((1, 1, ADA_CB), lambda i, j: (i, 0, j))],
        out_specs=pl.BlockSpec((1, 16, ADA_CB), lambda i, j: (i, 0, j)),
        out_shape=jax.ShapeDtypeStruct((DEPTH, 16, nc), F32), compiler_params=_cp(2),
    )(cvec, ada_w, bias)


def ada_bwd(cvec, dcols, ada_w, m, v, name):
    _, _, nc = ada_w.shape

    def body(c_ref, d_ref, w_ref, m_ref, v_ref, g_ref, dl_ref, mo_ref, vo_ref, acc_ref):
        @pl.when(jnp.logical_and(pl.program_id(0) == 0, pl.program_id(1) == 0))
        def _():
            acc_ref[...] = jnp.zeros_like(acc_ref)

        cv = c_ref[...]
        a = (cv * _sigmoid(cv)).astype(BF16)
        db = d_ref[0].astype(BF16)
        w = w_ref[0]
        g = _dot_tn(a, db)
        delta, mo, vo = _adamw_math(w, g, m_ref[0], v_ref[0])
        g_ref[0] = g
        dl_ref[0] = delta
        mo_ref[0] = mo
        vo_ref[0] = vo
        acc_ref[...] += _dot_nt(db[8:16, :], w.astype(BF16))

    wspec = pl.BlockSpec((1, D, ADA_CB), lambda i, j: (i, 0, j))
    wshape = jax.ShapeDtypeStruct(ada_w.shape, F32)
    return pl.pallas_call(
        body, name=name, grid=(DEPTH, nc // ADA_CB),
        in_specs=[pl.BlockSpec((16, D), lambda i, j: (0, 0)), pl.BlockSpec((1, 16, ADA_CB), lambda i, j: (i, 0, j)), wspec, wspec, wspec],
        out_specs=[wspec, wspec, wspec, wspec, pl.BlockSpec((8, D), lambda i, j: (0, 0))],
        out_shape=[wshape, wshape, wshape, wshape, jax.ShapeDtypeStruct((8, D), F32)], compiler_params=_cp(2),
    )(cvec, dcols, ada_w, m, v)


def _place():
    return lax.axis_index("x"), lax.axis_index("y"), lax.axis_index("c")


ANY = pl.BlockSpec(memory_space=pl.ANY)
VMEM_SPEC = pl.BlockSpec(memory_space=pltpu.VMEM)


def small_allgather(buf, name):
    r, c = buf.shape

    def body(in_ref, out_ref, send_sems, recv_sems, loc_sem):
        x, y, cc = _place()
        me = 4 * x + 2 * y + cc
        loc = pltpu.make_async_copy(in_ref, out_ref.at[me], loc_sem)
        loc.start()
        peers = []
        for k in range(1, 8):
            px = 1 - x if k & 4 else x
            py = 1 - y if k & 2 else y
            pc = 1 - cc if k & 1 else cc
            peers.append((px, py, pc))
        sends = []
        for k, peer in enumerate(peers):
            cp = pltpu.make_async_remote_copy(src_ref=in_ref, dst_ref=out_ref.at[me], send_sem=send_sems.at[k],
                                              recv_sem=recv_sems.at[k], device_id=peer, device_id_type=MESH)
            cp.start()
            sends.append(cp)
        for k, (px, py, pc) in enumerate(peers):
            pltpu.make_async_remote_copy(src_ref=in_ref, dst_ref=out_ref.at[4 * px + 2 * py + pc], send_sem=send_sems.at[k],
                                         recv_sem=recv_sems.at[k], device_id=(px, py, pc), device_id_type=MESH).wait_recv()
        for cp in sends:
            cp.wait_send()
        loc.wait()

    return pl.pallas_call(
        body, name=name, in_specs=[VMEM_SPEC], out_specs=VMEM_SPEC,
        out_shape=jax.ShapeDtypeStruct((8, r, c), buf.dtype),
        scratch_shapes=[pltpu.SemaphoreType.DMA((7,)), pltpu.SemaphoreType.DMA((7,)), pltpu.SemaphoreType.DMA],
    )(buf)


def _chip_peers(x, y):
    return [(1 - x, y), (x, 1 - y), (1 - x, 1 - y)]


def weight_allgather(srcs, rows, name):
    layers = [s.shape[0] // k for s, k in zip(srcs, rows)]
    n_in = len(srcs)
    out_shapes = []
    for s, k, nl in zip(srcs, rows, layers):
        out_shapes += [jax.ShapeDtypeStruct((4, k, s.shape[1]), BF16)] * nl
    n_out = len(out_shapes)

    def body(*refs):
        in_refs = refs[:n_in]
        out_refs = refs[n_in:n_in + n_out]
        send_sems, recv_sems, loc_sems = refs[n_in + n_out:]
        x, y, cc = _place()
        chip = 2 * x + y
        peers = _chip_peers(x, y)
        locs, sends = [], []
        o = 0
        for in_ref, k, nl in zip(in_refs, rows, layers):
            for l in range(nl):
                src = in_ref.at[pl.ds(l * k, k), :]
                loc = pltpu.make_async_copy(src, out_refs[o].at[chip], loc_sems.at[o])
                loc.start()
                locs.append(loc)
                for p, (px, py) in enumerate(peers):
                    cp = pltpu.make_async_remote_copy(src_ref=src, dst_ref=out_refs[o].at[chip], send_sem=send_sems.at[o, p],
                                                      recv_sem=recv_sems.at[o, p], device_id=(px, py, cc), device_id_type=MESH)
                    cp.start()
                    sends.append(cp)
                o += 1
        o = 0
        for in_ref, k, nl in zip(in_refs, rows, layers):
            for l in range(nl):
                src = in_ref.at[pl.ds(l * k, k), :]
                for p, (px, py) in enumerate(peers):
                    pltpu.make_async_remote_copy(src_ref=src, dst_ref=out_refs[o].at[2 * px + py], send_sem=send_sems.at[o, p],
                                                 recv_sem=recv_sems.at[o, p], device_id=(px, py, cc), device_id_type=MESH).wait_recv()
                o += 1
        for cp in sends:
            cp.wait_send()
        for loc in locs:
            loc.wait()

    return pl.pallas_call(
        body, name=name, in_specs=[ANY] * n_in, out_specs=[ANY] * n_out, out_shape=out_shapes,
        scratch_shapes=[pltpu.SemaphoreType.DMA((n_out, 3)), pltpu.SemaphoreType.DMA((n_out, 3)), pltpu.SemaphoreType.DMA((n_out,))],
    )(*srcs)


def grad_exchange(grads, groups, name):
    n_in = len(grads)
    out_shapes = []
    o = 0
    for nl in groups:
        _, k, c = grads[o].shape
        out_shapes.append(jax.ShapeDtypeStruct((4, nl, k, c), BF16))
        o += nl
    n_t = len(groups)

    def body(*refs):
        in_refs = refs[:n_in]
        out_refs = refs[n_in:n_in + n_t]
        send_sems, recv_sems, loc_sems = refs[n_in + n_t:]
        x, y, cc = _place()
        chip = 2 * x + y
        peers = _chip_peers(x, y)
        locs, sends = [], []
        o = 0
        for t, nl in enumerate(groups):
            for l in range(nl):
                loc = pltpu.make_async_copy(in_refs[o].at[chip], out_refs[t].at[chip, l], loc_sems.at[o])
                loc.start()
                locs.append(loc)
                for p, (px, py) in enumerate(peers):
                    cp = pltpu.make_async_remote_copy(src_ref=in_refs[o].at[2 * px + py], dst_ref=out_refs[t].at[chip, l],
                                                      send_sem=send_sems.at[o, p], recv_sem=recv_sems.at[o, p],
                                                      device_id=(px, py, cc), device_id_type=MESH)
                    cp.start()
                    sends.append(cp)
                o += 1
        o = 0
        for t, nl in enumerate(groups):
            for l in range(nl):
                for p, (px, py) in enumerate(peers):
                    pltpu.make_async_remote_copy(src_ref=in_refs[o].at[chip], dst_ref=out_refs[t].at[2 * px + py, l],
                                                 send_sem=send_sems.at[o, p], recv_sem=recv_sems.at[o, p],
                                                 device_id=(px, py, cc), device_id_type=MESH).wait_recv()
                o += 1
        for cp in sends:
            cp.wait_send()
        for loc in locs:
            loc.wait()

    return pl.pallas_call(
        body, name=name, in_specs=[ANY] * n_in, out_specs=[ANY] * n_t, out_shape=out_shapes,
        scratch_shapes=[pltpu.SemaphoreType.DMA((n_in, 3)), pltpu.SemaphoreType.DMA((n_in, 3)), pltpu.SemaphoreType.DMA((n_in,))],
    )(*grads)


def sibling_exchange(parts, name):
    n = len(parts)

    def body(*refs):
        in_refs = refs[:n]
        out_refs = refs[n:2 * n]
        send_sems, recv_sems = refs[2 * n:]
        x, y, cc = _place()
        sib = (x, y, 1 - cc)
        cps = []
        for k in range(n):
            cp = pltpu.make_async_remote_copy(src_ref=in_refs[k], dst_ref=out_refs[k], send_sem=send_sems.at[k],
                                              recv_sem=recv_sems.at[k], device_id=sib, device_id_type=MESH)
            cp.start()
            cps.append(cp)
        for cp in cps:
            cp.wait_recv()
        for cp in cps:
            cp.wait_send()

    return pl.pallas_call(
        body, name=name, in_specs=[ANY] * n, out_specs=[ANY] * n,
        out_shape=[jax.ShapeDtypeStruct(p.shape, p.dtype) for p in parts],
        scratch_shapes=[pltpu.SemaphoreType.DMA((n,)), pltpu.SemaphoreType.DMA((n,))],
    )(*parts)


SMALL_ROWS = 88
FIN_ROWS = 72


def small_finish(g3, g4, c_ctx, lbp, name):
    def body(g3_ref, g4_ref, cc_ref, lbp_ref, o_ref, s_ref):
        s = g3_ref[0]
        for k in range(1, 8):
            s = s + g3_ref[k]
        s_ref[...] = s
        for i in range(DEPTH):
            o_ref[8 * i:8 * i + 8, :] = s_ref[16 * i:16 * i + 8, :] + s_ref[16 * i + 8:16 * i + 16, :]
        acc = g4_ref[0]
        for k in (2, 4, 6):
            acc = acc + g4_ref[k]
        cc = cc_ref[...]
        sg = _sigmoid(cc)
        row = jnp.sum(acc, axis=0, keepdims=True) * (sg * (1.0 + cc * (1.0 - sg)))
        o_ref[32:40, :] = jnp.broadcast_to(row, (8, D))
        o_ref[40:64, :] = s_ref[64:88, :]
        o_ref[64:72, :] = jnp.zeros((8, D), F32)
        for d in range(2):
            pp = lbp_ref[2 * d:2 * d + 1, :] * lbp_ref[2 * d + 1:2 * d + 2, :] * s_ref[75 + d:76 + d, :]
            o_ref[64 + 2 * d:65 + 2 * d, :] = -pp
            o_ref[65 + 2 * d:66 + 2 * d, :] = pp

    return pl.pallas_call(
        body, name=name, in_specs=[VMEM_SPEC] * 4, out_specs=VMEM_SPEC,
        out_shape=jax.ShapeDtypeStruct((FIN_ROWS, D), F32),
        scratch_shapes=[pltpu.VMEM((SMALL_ROWS, D), F32)],
    )(g3, g4, c_ctx, lbp)


def _pack_rows(arrs):
    flat = jnp.concatenate([a.reshape(-1) for a in arrs])
    n = -(-flat.shape[0] // (8 * D)) * 8 * D
    return jnp.pad(flat, (0, n - flat.shape[0])).reshape(n // D, D)


def _unpack_rows(packed, shapes):
    flat = packed.reshape(-1)
    outs, off = [], 0
    for s in shapes:
        size = 1
        for k in s:
            size *= k
        outs.append(flat[off:off + size].reshape(s))
        off += size
    return outs


def _pad8(a):
    return jnp.pad(a, ((0, 8 - a.shape[0]), (0, 0)))


def kernel(x, c, ctx, c_ctx, ada_w, ada_b, norm1, norm2, norm_f, mlp_w1, mlp_w2, hgrn_w_in, hgrn_lb, hgrn_gnorm, hgrn_w_out, conv_w_in, conv_w, conv_b, conv_w_out, loss_target, m_c_ctx, m_ada_w, m_ada_b, m_norm1, m_norm2, m_norm_f, m_mlp_w1, m_mlp_w2, m_hgrn_w_in, m_hgrn_lb, m_hgrn_gnorm, m_hgrn_w_out, m_conv_w_in, m_conv_w, m_conv_b, m_conv_w_out, v_c_ctx, v_ada_w, v_ada_b, v_norm1, v_norm2, v_norm_f, v_mlp_w1, v_mlp_w2, v_hgrn_w_in, v_hgrn_lb, v_hgrn_gnorm, v_hgrn_w_out, v_conv_w_in, v_conv_w, v_conv_b, v_conv_w_out):
    xi, yi, ci = _place()
    me = 4 * xi + 2 * yi + ci
    chip = 2 * xi + yi
    seq = x.shape[1]
    assert ctx.shape[1] == TM and seq % TM == 0 and (seq + TM) % TMW == 0
    n_lat = seq // TM
    sd = D // 4
    nca = ada_w.shape[2]
    xs = jnp.concatenate([x[0], ctx[0]], axis=0)

    sh_rows = jnp.concatenate([hgrn_lb.reshape(4, sd), conv_w.reshape(6, sd), conv_b.reshape(2, sd)], axis=0)
    buf1 = jnp.concatenate([c, jnp.pad(sh_rows, ((0, 0), (0, D - sd))), jnp.zeros((3, D), F32)], axis=0)
    g1 = small_allgather(buf1, "gather_small_in")
    cvec = jnp.concatenate([g1[:, 0, :], jnp.broadcast_to(c_ctx[None], (8, D))], axis=0)
    shf = g1[0::2, 1:13, :sd].transpose(1, 0, 2).reshape(12, D)
    lb_p = jax.nn.softmax(shf[0:4].reshape(2, 2, D), axis=1)
    lower = jnp.cumsum(lb_p, axis=1) - lb_p[:, :1]
    lbs = [lower[:, 0], lower[:, 1]]
    cw8 = [_pad8(shf[4:7]), _pad8(shf[7:10])]
    cb = [shf[10:11], shf[11:12]]

    bias = lax.dynamic_slice_in_dim(ada_b, chip * nca, nca, axis=1).reshape(DEPTH, 1, nca)
    ada_part = ada_fwd(cvec, ada_w, bias, "ada_fwd")
    g2 = small_allgather(ada_part.reshape(DEPTH * 16, nca), "gather_ada")
    ada_full = g2[0::2].reshape(4, DEPTH, 16, nca).transpose(1, 2, 0, 3).reshape(DEPTH, 16, 4 * nca)
    lat = lax.dynamic_slice_in_dim(ada_full, me, 1, axis=1)[:, 0]
    mods = [jnp.stack([_pad8(lat[i].reshape(6, D)), _pad8(ada_full[i, 8].reshape(6, D))]) for i in range(DEPTH)]

    big = [(mlp_w1, m_mlp_w1, v_mlp_w1), (mlp_w2, m_mlp_w2, v_mlp_w2), (hgrn_w_in, m_hgrn_w_in, v_hgrn_w_in),
           (hgrn_w_out, m_hgrn_w_out, v_hgrn_w_out), (conv_w_in, m_conv_w_in, v_conv_w_in), (conv_w_out, m_conv_w_out, v_conv_w_out)]
    big_names = ["w1", "w2", "hin", "hout", "cin", "cout"]
    flat2 = lambda a: a.reshape(a.shape[0] * a.shape[1], a.shape[2])
    srcs = [cast_bf16(flat2(w), f"cast_{n}") for (w, _, _), n in zip(big, big_names)]
    rows = [w.shape[1] for w, _, _ in big]
    groups = [w.shape[0] for w, _, _ in big]
    gathered = weight_allgather(srcs, rows, "weight_allgather")
    wts, o = {}, 0
    for n, nl in zip(big_names, groups):
        wts[n] = list(gathered[o:o + nl])
        o += nl
    for n in ("w2", "hout", "cout"):
        wts[n] = [w.reshape(w.shape[0] * w.shape[1], w.shape[2]) for w in wts[n]]

    lane, dx, g, small = local_step(xs, loss_target[0], mods, norm1, norm2, norm_f[None], lbs, hgrn_gnorm, cw8, cb, wts, n_lat)
    loss = lax.psum(0.5 * jnp.sum(lane) / D, ("x", "y", "c"))
    grad_x = dx[:seq][None]

    z3 = jnp.zeros((3, D), F32)
    rows3 = jnp.concatenate(small["dmod"] + small["norm1"] + small["norm2"] + [small["norm_f"]] + small["gnorm"]
                            + [small["lb"][1]] + small["cw"] + small["cb"] + [z3], axis=0)
    g3 = small_allgather(rows3, "gather_small_out")
    dmat = g3[:, :64].reshape(8, DEPTH, 2, 8, D)[:, :, :, :6].transpose(1, 2, 0, 3, 4).reshape(DEPTH, 16, 6 * D)
    dcols = lax.dynamic_slice_in_dim(dmat, chip * nca, nca, axis=2)
    g_ada_w, d_ada_w, nm_ada_w, nv_ada_w, acc4 = ada_bwd(cvec, dcols, ada_w, m_ada_w, v_ada_w, "ada_bwd")
    g4 = small_allgather(acc4, "gather_cctx")
    fin = small_finish(g3, g4, c_ctx[None], _pad8(lb_p.reshape(4, D)), "small_finish")
    cols = lambda a: lax.dynamic_slice_in_dim(a, chip * sd, sd, axis=a.ndim - 1)
    small_g = [fin[32], fin[0:32].reshape(DEPTH, 8, D)[:, :6].reshape(DEPTH, 6 * D), fin[40:44], fin[44:48], fin[48], fin[49:51],
               cols(fin[64:68].reshape(2, 2, D)), cols(fin[53:59].reshape(2, 3, D)), cols(fin[59:61])]
    small_w = [c_ctx, ada_b, norm1, norm2, norm_f, hgrn_gnorm, hgrn_lb, conv_w, conv_b]
    small_m = [m_c_ctx, m_ada_b, m_norm1, m_norm2, m_norm_f, m_hgrn_gnorm, m_hgrn_lb, m_conv_w, m_conv_b]
    small_v = [v_c_ctx, v_ada_b, v_norm1, v_norm2, v_norm_f, v_hgrn_gnorm, v_hgrn_lb, v_conv_w, v_conv_b]
    shapes = [w.shape for w in small_w]
    packed = adamw([_pack_rows(small_g)], _pack_rows(small_w), _pack_rows(small_m), _pack_rows(small_v), "adamw_small")
    s_g, s_d, s_m, s_v = [_unpack_rows(p, shapes) for p in packed]

    grads = []
    for n in big_names:
        grads += [a.reshape(4, a.shape[0] * a.shape[1] // 4, a.shape[2]) for a in g[n]]
    recv = grad_exchange(grads, groups, "grad_exchange")
    partial = [sum_slots(r.reshape(4, r.shape[1] * r.shape[2], r.shape[3]), f"sum_{n}") for r, n in zip(recv, big_names)]
    other = sibling_exchange(partial, "sibling_exchange")
    b_g, b_d, b_m, b_v = [], [], [], []
    for (w, m, v), pm, po, n in zip(big, partial, other, big_names):
        outs = adamw([pm, po], flat2(w), flat2(m), flat2(v), f"adamw_{n}")
        for lst, a in zip((b_g, b_d, b_m, b_v), outs):
            lst.append(a.reshape(w.shape))

    def ordered(s, a, b):
        return [s[0], a, s[1], s[2], s[3], s[4], b[0], b[1], b[2], s[6], s[5], b[3], b[4], s[7], s[8], b[5]]

    return (loss, grad_x, *ordered(s_g, g_ada_w, b_g), *ordered(s_d, d_ada_w, b_d), *ordered(s_m, nm_ada_w, b_m),
            *ordered(s_v, nv_ada_w, b_v))
```

```python
import functools

import jax
import jax.numpy as jnp
from jax import lax
from jax.experimental import pallas as pl
from jax.experimental.pallas import tpu as pltpu

F32 = jnp.float32
BF16 = jnp.bfloat16
MESH = pl.DeviceIdType.MESH

D = 1024
HD = 128
NH = D // HD
CH = 64
TM = 256
TMW = 768
EPS = 1e-6
DEPTH = 4
VMEM_LIMIT = 56 * 1024 * 1024

ADAM_LR = 0.001
ADAM_B1 = 0.9
ADAM_B2 = 0.999
ADAM_EPS = 1e-08
ADAM_WD = 0.01
ADAM_STEP = 10


def _cp(n_grid):
    return pltpu.CompilerParams(dimension_semantics=("arbitrary",) * n_grid, vmem_limit_bytes=VMEM_LIMIT)


def _dot(a, b):
    return jnp.dot(a, b, preferred_element_type=F32)


def _dot_nt(a, b):
    return lax.dot_general(a, b, (((1,), (1,)), ((), ())), preferred_element_type=F32)


def _dot_tn(a, b):
    return lax.dot_general(a, b, (((0,), (0,)), ((), ())), preferred_element_type=F32)


def _sigmoid(z):
    return 1.0 / (1.0 + jnp.exp(-z))


def _norm_mod(x, gain, sh, sc):
    r = lax.rsqrt(jnp.mean(x * x, axis=-1, keepdims=True) + EPS)
    xn = x * r
    yn = xn * gain
    return r, xn, yn, yn * (1.0 + sc) + sh


def _row_spec(width):
    return pl.BlockSpec((TM, width), lambda i: (i, 0))


def _col_spec(col):
    return pl.BlockSpec((TM, D), lambda i: (i, col))


def _full_spec(shape):
    nd = len(shape)
    return pl.BlockSpec(shape, lambda i: (0,) * nd)


def _mod_spec(n_lat):
    return pl.BlockSpec((1, 8, D), lambda i: (i // n_lat, 0, 0))


def proj_fwd(x, gain, mod, m0, w4, n_lat, name):
    t = x.shape[0]
    nb, _, ns = w4.shape

    def body(x_ref, gain_ref, mod_ref, w_ref, p_ref):
        _, _, _, h = _norm_mod(x_ref[...], gain_ref[...], mod_ref[0, m0:m0 + 1, :], mod_ref[0, m0 + 1:m0 + 2, :])
        hb = h.astype(BF16)
        for c in range(nb):
            p_ref[:, c * ns:(c + 1) * ns] = _dot(hb, w_ref[c])

    return pl.pallas_call(
        body, name=name, grid=(t // TM,),
        in_specs=[_row_spec(D), _full_spec((1, D)), _mod_spec(n_lat), _full_spec(w4.shape)],
        out_specs=_row_spec(nb * ns),
        out_shape=jax.ShapeDtypeStruct((t, nb * ns), F32),
        compiler_params=_cp(1),
    )(x, gain, mod, w4)


def proj_bwd(parts, w4, x, gain, mod, m0, dx_in, n_lat, name):
    t = x.shape[0]
    nb, _, ns = w4.shape
    n = nb * ns
    n_parts = len(parts)
    widths = [p.shape[1] for p in parts]
    offs = [sum(widths[:k]) for k in range(n_parts)]
    assert sum(widths) == n
    single = n_parts == 1

    def body(*refs):
        part_refs = refs[:n_parts]
        w_ref, x_ref, gain_ref, mod_ref, dxin_ref = refs[n_parts:n_parts + 5]
        rest = refs[n_parts + 5:]
        if single:
            dx_ref, hb_ref, acc_ref = rest
            src = part_refs[0]
        else:
            dx_ref, hb_ref, acc_ref, dpb_ref = rest
            for p_ref, off, w in zip(part_refs, offs, widths):
                dpb_ref[:, off:off + w] = p_ref[...]
            src = dpb_ref
        i = pl.program_id(0)

        @pl.when(i == 0)
        def _():
            acc_ref[...] = jnp.zeros_like(acc_ref)

        gain = gain_ref[...]
        sc = mod_ref[0, m0 + 1:m0 + 2, :]
        r, xn, yn, h = _norm_mod(x_ref[...], gain, mod_ref[0, m0:m0 + 1, :], sc)
        hb_ref[...] = h.astype(BF16)
        dh = _dot_nt(src[:, 0:ns], w_ref[0])
        for c in range(1, nb):
            dh = dh + _dot_nt(src[:, c * ns:(c + 1) * ns], w_ref[c])
        dsh = jnp.sum(dh, axis=0, keepdims=True)
        dsc = jnp.sum(dh * yn, axis=0, keepdims=True)
        dyn = dh * (1.0 + sc)
        dgain = jnp.sum(dyn * xn, axis=0, keepdims=True)
        dxn = dyn * gain
        dx = r * (dxn - xn * jnp.mean(dxn * xn, axis=-1, keepdims=True))
        dx_ref[...] = dxin_ref[...] + dx
        latf = (i < n_lat).astype(F32)
        ctxf = 1.0 - latf
        acc_ref[0:1, :] += dgain
        acc_ref[1:2, :] += dsh * latf
        acc_ref[2:3, :] += dsc * latf
        acc_ref[3:4, :] += dsh * ctxf
        acc_ref[4:5, :] += dsc * ctxf

    out_specs = [_row_spec(D), _row_spec(D), _full_spec((8, D))]
    out_shape = [jax.ShapeDtypeStruct((t, D), F32), jax.ShapeDtypeStruct((t, D), BF16), jax.ShapeDtypeStruct((8, D), F32)]
    if not single:
        out_specs.append(_row_spec(n))
        out_shape.append(jax.ShapeDtypeStruct((t, n), BF16))
    outs = pl.pallas_call(
        body, name=name, grid=(t // TM,),
        in_specs=[_row_spec(w) for w in widths]
        + [_full_spec(w4.shape), _row_spec(D), _full_spec((1, D)), _mod_spec(n_lat), _row_spec(D)],
        out_specs=out_specs, out_shape=out_shape, compiler_params=_cp(1),
    )(*parts, w4, x, gain, mod, dx_in)
    if single:
        return outs[0], outs[1], parts[0], outs[2]
    return outs[0], outs[1], outs[3], outs[2]


def dw_tn(a, b, nb, a_blocked, name):
    t = a.shape[0]
    ka = a.shape[1] // nb if a_blocked else a.shape[1]
    kb = b.shape[1] if a_blocked else b.shape[1] // nb
    n_k = t // TMW

    def body(a_ref, b_ref, o_ref, acc):
        k = pl.program_id(1)

        @pl.when(k == 0)
        def _():
            acc[...] = jnp.zeros_like(acc)

        acc[...] += _dot_tn(a_ref[...], b_ref[...])

        @pl.when(k == n_k - 1)
        def _():
            o_ref[0] = acc[...].astype(BF16)

    a_spec = pl.BlockSpec((TMW, ka), (lambda j, k: (k, j)) if a_blocked else (lambda j, k: (k, 0)))
    b_spec = pl.BlockSpec((TMW, kb), (lambda j, k: (k, 0)) if a_blocked else (lambda j, k: (k, j)))
    return pl.pallas_call(
        body, name=name, grid=(nb, n_k),
        in_specs=[a_spec, b_spec],
        out_specs=pl.BlockSpec((1, ka, kb), lambda j, k: (j, 0, 0)),
        out_shape=jax.ShapeDtypeStruct((nb, ka, kb), BF16),
        scratch_shapes=[pltpu.VMEM((ka, kb), F32)],
        compiler_params=_cp(2),
    )(a, b)


def outproj_fwd(prologue, extras, extra_specs, w, x, mod, m0, n_lat, name):
    t = x.shape[0]
    k = w.shape[0]
    n_extra = len(extras)

    def body(*refs):
        ex = refs[:n_extra]
        w_ref, x_ref, mod_ref, xo_ref, y_ref, ab_ref = refs[n_extra:]
        ab = prologue(pl.program_id(0), *ex).astype(BF16)
        ab_ref[...] = ab
        y = _dot(ab, w_ref[...])
        y_ref[...] = y
        xo_ref[...] = x_ref[...] + mod_ref[0, m0 + 2:m0 + 3, :] * y

    return pl.pallas_call(
        body, name=name, grid=(t // TM,),
        in_specs=list(extra_specs) + [_full_spec(w.shape), _row_spec(D), _mod_spec(n_lat)],
        out_specs=[_row_spec(D), _row_spec(D), _row_spec(k)],
        out_shape=[jax.ShapeDtypeStruct((t, D), F32), jax.ShapeDtypeStruct((t, D), F32), jax.ShapeDtypeStruct((t, k), BF16)],
        compiler_params=_cp(1),
    )(*extras, w, x, mod)


def outproj_bwd(epilogue, extras, extra_specs, ep_out_specs, ep_out_shapes, w, dxn, y, mod, m0, n_lat, name):
    t = dxn.shape[0]
    n_extra = len(extras)

    def body(*refs):
        ex = refs[:n_extra]
        w_ref, dxn_ref, y_ref, mod_ref, dyb_ref, acc_ref = refs[n_extra:n_extra + 6]
        ep_outs = refs[n_extra + 6:]
        i = pl.program_id(0)

        @pl.when(i == 0)
        def _():
            acc_ref[...] = jnp.zeros_like(acc_ref)

        dxv = dxn_ref[...]
        dyb = (dxv * mod_ref[0, m0 + 2:m0 + 3, :]).astype(BF16)
        dyb_ref[...] = dyb
        dg = jnp.sum(dxv * y_ref[...], axis=0, keepdims=True)
        latf = (i < n_lat).astype(F32)
        acc_ref[0:1, :] += dg * latf
        acc_ref[1:2, :] += dg * (1.0 - latf)
        epilogue(i, _dot_nt(dyb, w_ref[...]), ex, ep_outs, acc_ref)

    outs = pl.pallas_call(
        body, name=name, grid=(t // TM,),
        in_specs=list(extra_specs) + [_full_spec(w.shape), _row_spec(D), _row_spec(D), _mod_spec(n_lat)],
        out_specs=[_row_spec(D), _full_spec((8, D))] + list(ep_out_specs),
        out_shape=[jax.ShapeDtypeStruct((t, D), BF16), jax.ShapeDtypeStruct((8, D), F32)] + list(ep_out_shapes),
        compiler_params=_cp(1),
    )(*extras, w, dxn, y, mod)
    return outs[0], outs[1], outs[2:]


def mlp_prologue(i, p_ref):
    return jnp.square(jnp.maximum(p_ref[...], 0.0))


def mlp_epilogue(i, da, ex, outs, acc_ref):
    outs[0][...] = (da * (2.0 * jnp.maximum(ex[0][...], 0.0))).astype(BF16)


def readout_prologue(i, o0_ref, o1_ref, gate_ref, gn_ref):
    o = o0_ref[...] + o1_ref[...]
    gate = gate_ref[...]
    w = gn_ref[...] * (gate * _sigmoid(gate))
    pieces = []
    for h in range(NH):
        sl = slice(h * HD, (h + 1) * HD)
        oh = o[:, sl]
        pieces.append(oh * lax.rsqrt(jnp.mean(oh * oh, axis=-1, keepdims=True) + EPS) * w[:, sl])
    return jnp.concatenate(pieces, axis=1)


def readout_epilogue(i, da, ex, outs, acc_ref):
    o0_ref, o1_ref, gate_ref, gn_ref = ex
    do_ref, dgate_ref = outs
    o = o0_ref[...] + o1_ref[...]
    gate = gate_ref[...]
    gn = gn_ref[...]
    sg = _sigmoid(gate)
    silu = gate * sg
    dsilu = sg * (1.0 + gate * (1.0 - sg))
    for h in range(NH):
        sl = slice(h * HD, (h + 1) * HD)
        oh = o[:, sl]
        r = lax.rsqrt(jnp.mean(oh * oh, axis=-1, keepdims=True) + EPS)
        nh = oh * r
        dah = da[:, sl]
        acc_ref[2:3, sl] += jnp.sum(dah * nh * silu[:, sl], axis=0, keepdims=True)
        dgate_ref[:, sl] = (dah * nh * gn[:, sl] * dsilu[:, sl]).astype(BF16)
        dn = dah * gn[:, sl] * silu[:, sl]
        do_ref[:, sl] = r * (dn - nh * jnp.mean(dn * nh, axis=-1, keepdims=True))


def _seg_masks(i, n_lat):
    rows = lax.broadcasted_iota(jnp.int32, (TM, 1), 0)
    latf = (i < n_lat).astype(F32)
    ctxf = 1.0 - latf
    prev_ok = (rows % CH != 0).astype(F32) * latf + (rows != 0).astype(F32) * ctxf
    next_ok = (rows % CH != CH - 1).astype(F32) * latf + (rows != TM - 1).astype(F32) * ctxf
    return prev_ok, next_ok


def _shifts(i, n_lat, sft, cur, halo_prev, halo_next):
    if sft == 1:
        prev_ok, next_ok = _seg_masks(i, n_lat)
        return pltpu.roll(cur, 1, 0) * prev_ok, pltpu.roll(cur, TM - 1, 0) * next_ok
    has_prev = jnp.logical_and(i > 0, i < n_lat).astype(F32)
    has_next = (i < n_lat - 1).astype(F32)
    prev = jnp.concatenate([halo_prev * has_prev, cur[:TM - CH]], axis=0)
    nxt = jnp.concatenate([cur[CH:], halo_next * has_next], axis=0)
    return prev, nxt


def _conv_u(sft, ex):
    if sft == 1:
        gb_ref, gc_ref, xi_ref, cw_ref, cb_ref = ex
        return gb_ref, gc_ref[...] * xi_ref[...], None, None, cw_ref, cb_ref
    gb_ref, gc_ref, xi_ref, gcp_ref, xip_ref, gcn_ref, xin_ref, cw_ref, cb_ref = ex
    return gb_ref, gc_ref[...] * xi_ref[...], gcp_ref[...] * xip_ref[...], gcn_ref[...] * xin_ref[...], cw_ref, cb_ref


def _conv_value(i, n_lat, sft, ex):
    gb_ref, u, up, un, cw_ref, cb_ref = _conv_u(sft, ex)
    u_prev, u_next = _shifts(i, n_lat, sft, u, up, un)
    return gb_ref, cb_ref[...] + cw_ref[0:1, :] * u_prev + cw_ref[1:2, :] * u + cw_ref[2:3, :] * u_next


def make_conv_prologue(n_lat, sft):
    def prologue(i, *ex):
        gb_ref, conv = _conv_value(i, n_lat, sft, ex)
        return gb_ref[...] * conv
    return prologue


def make_conv_epilogue(n_lat, sft):
    def epilogue(i, da, ex, outs, acc_ref):
        gb_ref, conv = _conv_value(i, n_lat, sft, ex)
        outs[0][...] = da * gb_ref[...]
        outs[1][...] = (da * conv).astype(BF16)
    return epilogue


def _conv_specs(sft, t):
    specs = [_col_spec(0), _col_spec(1), _col_spec(2)]
    if sft != 1:
        per = TM // CH
        last = t // CH - 1
        for fn in (lambda i: jnp.maximum(i * per - 1, 0), lambda i: jnp.minimum(i * per + per, last)):
            for col in (1, 2):
                specs.append(pl.BlockSpec((CH, D), functools.partial(lambda i, f, c: (f(i), c), f=fn, c=col)))
    return specs + [_full_spec((8, D)), _full_spec((1, D))]


def _conv_args(sft, p, cw8, cb):
    return [p] * (3 if sft == 1 else 7) + [cw8, cb]


def conv_bwd(dconv, p, cw8, sft, n_lat, name):
    t = dconv.shape[0]
    halo = sft != 1

    def body(*refs):
        if halo:
            dc_ref, dcp_ref, dcn_ref, gc_ref, xi_ref, gcp_ref, xip_ref, gcn_ref, xin_ref, cw_ref, dgc_ref, dxi_ref, acc_ref = refs
            up, un = gcp_ref[...] * xip_ref[...], gcn_ref[...] * xin_ref[...]
            dcp, dcn = dcp_ref[...], dcn_ref[...]
        else:
            dc_ref, gc_ref, xi_ref, cw_ref, dgc_ref, dxi_ref, acc_ref = refs
            up = un = dcp = dcn = None
        i = pl.program_id(0)

        @pl.when(i == 0)
        def _():
            acc_ref[...] = jnp.zeros_like(acc_ref)

        gc = gc_ref[...]
        xi = xi_ref[...]
        u = gc * xi
        dc = dc_ref[...]
        u_prev, u_next = _shifts(i, n_lat, sft, u, up, un)
        dc_prev, dc_next = _shifts(i, n_lat, sft, dc, dcp, dcn)
        acc_ref[0:1, :] += jnp.sum(dc * u_prev, axis=0, keepdims=True)
        acc_ref[1:2, :] += jnp.sum(dc * u, axis=0, keepdims=True)
        acc_ref[2:3, :] += jnp.sum(dc * u_next, axis=0, keepdims=True)
        acc_ref[3:4, :] += jnp.sum(dc, axis=0, keepdims=True)
        du = cw_ref[0:1, :] * dc_next + cw_ref[1:2, :] * dc + cw_ref[2:3, :] * dc_prev
        dgc_ref[...] = (du * xi).astype(BF16)
        dxi_ref[...] = (du * gc).astype(BF16)

    per = TM // CH
    last = t // CH - 1
    prev_i = lambda i: jnp.maximum(i * per - 1, 0)
    next_i = lambda i: jnp.minimum(i * per + per, last)
    if halo:
        in_specs = [_row_spec(D), pl.BlockSpec((CH, D), lambda i: (prev_i(i), 0)), pl.BlockSpec((CH, D), lambda i: (next_i(i), 0)),
                    _col_spec(1), _col_spec(2),
                    pl.BlockSpec((CH, D), lambda i: (prev_i(i), 1)), pl.BlockSpec((CH, D), lambda i: (prev_i(i), 2)),
                    pl.BlockSpec((CH, D), lambda i: (next_i(i), 1)), pl.BlockSpec((CH, D), lambda i: (next_i(i), 2)),
                    _full_spec((8, D))]
        args = [dconv, dconv, dconv, p, p, p, p, p, p, cw8]
    else:
        in_specs = [_row_spec(D), _col_spec(1), _col_spec(2), _full_spec((8, D))]
        args = [dconv, p, p, cw8]
    return pl.pallas_call(
        body, name=name, grid=(t // TM,), in_specs=in_specs,
        out_specs=[_row_spec(D), _row_spec(D), _full_spec((8, D))],
        out_shape=[jax.ShapeDtypeStruct((t, D), BF16), jax.ShapeDtypeStruct((t, D), BF16), jax.ShapeDtypeStruct((8, D), F32)],
        compiler_params=_cp(1),
    )(*args)


def _chunk_cumsum(g, reverse):
    rows = lax.broadcasted_iota(jnp.int32, (g.shape[0], 1), 0) % CH
    n = g.shape[0]
    s = 1
    while s < CH:
        if reverse:
            g = g + pltpu.roll(g, n - s, 0) * (rows < CH - s).astype(F32)
        else:
            g = g + pltpu.roll(g, s, 0) * (rows >= s).astype(F32)
        s *= 2
    return g


def _gate_values(z, lb):
    sig = _sigmoid(z)
    f = lb + (1.0 - lb) * sig
    return sig, f


def _tri(direction, transposed):
    r = lax.broadcasted_iota(jnp.int32, (CH, CH), 0)
    c = lax.broadcasted_iota(jnp.int32, (CH, CH), 1)
    lower = (direction == 0) != transposed
    return r >= c if lower else r <= c


def _gla_rows(direction):
    return (CH // 2 - 1, CH - 1) if direction == 0 else (CH // 2, 0)


def gla_fwd(p, lb2, direction, name):
    t = p.shape[0]
    nt = t // TM
    per = TM // CH
    ref_row, last_row = _gla_rows(direction)
    tile = (lambda i: (i + nt - 1) % nt) if direction == 0 else (lambda i: nt - 1 - i)

    def body(z_ref, v_ref, qr_ref, lb_ref, o_ref, s_ref, st, q_s, k_s, c_s):
        @pl.when(pl.program_id(0) == 0)
        def _():
            st[...] = jnp.zeros_like(st)

        _, f = _gate_values(z_ref[...], lb_ref[direction:direction + 1, :])
        k_s[...] = 1.0 - f
        c_s[...] = _chunk_cumsum(jnp.log(f), direction == 1)
        qr = qr_ref[...]
        q_s[...] = qr * _sigmoid(qr)
        mask = _tri(direction, False)
        for it in range(per):
            ci = it if direction == 0 else per - 1 - it
            r0 = ci * CH
            cum = c_s[r0:r0 + CH, :]
            ref = c_s[r0 + ref_row:r0 + ref_row + 1, :]
            last = c_s[r0 + last_row:r0 + last_row + 1, :]
            q = q_s[r0:r0 + CH, :]
            k = k_s[r0:r0 + CH, :]
            qh = (q * jnp.exp(cum)).astype(BF16)
            qt = (q * jnp.exp(cum - ref)).astype(BF16)
            kt = (k * jnp.exp(ref - cum)).astype(BF16)
            kb = (k * jnp.exp(last - cum)).astype(BF16)
            el = jnp.exp(last)
            vb = v_ref[r0:r0 + CH, :].astype(BF16)
            for h in range(NH):
                sl = slice(h * HD, (h + 1) * HD)
                s_t = st[h]
                s_ref[ci, h] = s_t
                sc = jnp.where(mask, _dot_nt(qt[:, sl], kt[:, sl]), 0.0)
                o_ref[r0:r0 + CH, sl] = _dot_nt(qh[:, sl], s_t.astype(BF16)) + _dot(sc.astype(BF16), vb[:, sl])
                st[h] = s_t * el[:, sl] + _dot_tn(vb[:, sl], kb[:, sl])

    tspec = lambda col: pl.BlockSpec((TM, D), lambda i: (tile(i), col))
    return pl.pallas_call(
        body, name=name, grid=(nt,),
        in_specs=[tspec(direction), tspec(2), tspec(3), _full_spec((2, D))],
        out_specs=[pl.BlockSpec((TM, D), lambda i: (tile(i), 0)), pl.BlockSpec((per, NH, HD, HD), lambda i: (tile(i), 0, 0, 0))],
        out_shape=[jax.ShapeDtypeStruct((t, D), F32), jax.ShapeDtypeStruct((t // CH, NH, HD, HD), F32)],
        scratch_shapes=[pltpu.VMEM((NH, HD, HD), F32), pltpu.VMEM((TM, D), F32), pltpu.VMEM((TM, D), F32), pltpu.VMEM((TM, D), F32)],
        compiler_params=_cp(1),
    )(p, p, p, lb2)


def gla_bwd(p, lb2, do, states, direction, prev, name):
    t = p.shape[0]
    nt = t // TM
    per = TM // CH
    ref_row, last_row = _gla_rows(direction)
    tile = (lambda i: (2 * nt - 2 - i) % nt) if direction == 0 else (lambda i: i)
    final = prev is not None
    n_in = 8 if final else 6

    def body(*refs):
        z_ref, v_ref, qr_ref, lb_ref, do_ref, s_ref = refs[:6]
        dz_ref, dv_ref, dq_ref, acc_ref, dst, q_s, k_s, c_s, dq_s, dk_s, dl_s = refs[n_in:]

        @pl.when(pl.program_id(0) == 0)
        def _():
            dst[...] = jnp.zeros_like(dst)
            acc_ref[...] = jnp.zeros_like(acc_ref)

        lb = lb_ref[direction:direction + 1, :]
        sig, f = _gate_values(z_ref[...], lb)
        k_s[...] = 1.0 - f
        c_s[...] = _chunk_cumsum(jnp.log(f), direction == 1)
        qr = qr_ref[...]
        sq = _sigmoid(qr)
        q_s[...] = qr * sq
        mask = _tri(direction, False)
        mask_t = _tri(direction, True)
        is_last = lax.broadcasted_iota(jnp.int32, (CH, 1), 0) == last_row
        for it in range(per):
            ci = per - 1 - it if direction == 0 else it
            r0 = ci * CH
            cum = c_s[r0:r0 + CH, :]
            ref = c_s[r0 + ref_row:r0 + ref_row + 1, :]
            last = c_s[r0 + last_row:r0 + last_row + 1, :]
            q = q_s[r0:r0 + CH, :]
            k = k_s[r0:r0 + CH, :]
            e_h = jnp.exp(cum)
            e_t = jnp.exp(cum - ref)
            e_kt = jnp.exp(ref - cum)
            e_kb = jnp.exp(last - cum)
            el = jnp.exp(last)
            qh = (q * e_h).astype(BF16)
            qt = (q * e_t).astype(BF16)
            kt = (k * e_kt).astype(BF16)
            kbf = k * e_kb
            kb = kbf.astype(BF16)
            vb = v_ref[r0:r0 + CH, :].astype(BF16)
            dob = do_ref[r0:r0 + CH, :].astype(BF16)
            for h in range(NH):
                sl = slice(h * HD, (h + 1) * HD)
                s_t = s_ref[ci, h]
                ds_t = dst[h]
                ds_b = ds_t.astype(BF16)
                d_a = jnp.where(mask, _dot_nt(dob[:, sl], vb[:, sl]), 0.0).astype(BF16)
                a_t = jnp.where(mask_t, _dot_nt(kt[:, sl], qt[:, sl]), 0.0).astype(BF16)
                d_at = jnp.where(mask_t, _dot_nt(vb[:, sl], dob[:, sl]), 0.0).astype(BF16)
                dv = _dot(a_t, dob[:, sl]) + _dot_nt(kb[:, sl], ds_b)
                dqh = _dot(dob[:, sl], s_t.astype(BF16))
                dqt = _dot(d_a, kt[:, sl])
                dkt = _dot(d_at, qt[:, sl])
                dkb = _dot(vb[:, sl], ds_b)
                dl_s[it:it + 1, sl] = (el[:, sl] * jnp.sum(ds_t * s_t, axis=0, keepdims=True)
                                       + jnp.sum(dkb * kbf[:, sl], axis=0, keepdims=True))
                dst[h] = ds_t * el[:, sl] + _dot_tn(dob[:, sl], qh[:, sl])
                dq_s[r0:r0 + CH, sl] = dqh * e_h[:, sl] + dqt * e_t[:, sl]
                dk_s[r0:r0 + CH, sl] = dkt * e_kt[:, sl] + dkb * e_kb[:, sl]
                if final:
                    dv_ref[r0:r0 + CH, sl] = (refs[6][r0:r0 + CH, sl] + dv).astype(BF16)
                else:
                    dv_ref[r0:r0 + CH, sl] = dv
            dcum = dq_s[r0:r0 + CH, :] * q - dk_s[r0:r0 + CH, :] * k
            c_s[r0:r0 + CH, :] = dcum + jnp.where(is_last, dl_s[it:it + 1, :], 0.0)
        dg = _chunk_cumsum(c_s[...], direction == 0)
        df = dg / f - dk_s[...]
        acc_ref[0:1, :] += jnp.sum(df * (1.0 - sig), axis=0, keepdims=True)
        dz_ref[...] = (df * (1.0 - lb) * sig * (1.0 - sig)).astype(BF16)
        if final:
            dq_ref[...] = ((refs[7][...] + dq_s[...]) * (sq * (1.0 + qr * (1.0 - sq)))).astype(BF16)
        else:
            dq_ref[...] = dq_s[...]

    tspec = lambda col: pl.BlockSpec((TM, D), lambda i: (tile(i), col))
    sspec = pl.BlockSpec((per, NH, HD, HD), lambda i: (tile(i), 0, 0, 0))
    in_specs = [tspec(direction), tspec(2), tspec(3), _full_spec((2, D)), tspec(0), sspec]
    args = [p, p, p, lb2, do, states]
    if final:
        in_specs += [tspec(0), tspec(0)]
        args += list(prev)
    odt = BF16 if final else F32
    return pl.pallas_call(
        body, name=name, grid=(nt,), in_specs=in_specs,
        out_specs=[tspec(0), tspec(0), tspec(0), _full_spec((8, D))],
        out_shape=[jax.ShapeDtypeStruct((t, D), BF16), jax.ShapeDtypeStruct((t, D), odt), jax.ShapeDtypeStruct((t, D), odt),
                   jax.ShapeDtypeStruct((8, D), F32)],
        scratch_shapes=[pltpu.VMEM((NH, HD, HD), F32)] + [pltpu.VMEM((TM, D), F32)] * 5 + [pltpu.VMEM((8, D), F32)],
        compiler_params=_cp(1),
    )(*args)


def loss_bwd(x, gain, target, n_lat, name):
    t = x.shape[0]

    def body(x_ref, gain_ref, tg_ref, dx_ref, acc_ref):
        i = pl.program_id(0)

        @pl.when(i == 0)
        def _():
            acc_ref[...] = jnp.zeros_like(acc_ref)

        latf = (i < n_lat).astype(F32)
        x = x_ref[...]
        gain = gain_ref[...]
        r = lax.rsqrt(jnp.mean(x * x, axis=-1, keepdims=True) + EPS)
        xn = x * r
        err = (xn * gain - tg_ref[...]) * latf
        dy = err * (1.0 / D)
        dxn = dy * gain
        dx_ref[...] = r * (dxn - xn * jnp.mean(dxn * xn, axis=-1, keepdims=True))
        acc_ref[0:1, :] += jnp.sum(dy * xn, axis=0, keepdims=True)
        acc_ref[1:2, :] += jnp.sum(err * err, axis=0, keepdims=True)

    return pl.pallas_call(
        body, name=name, grid=(t // TM,),
        in_specs=[_row_spec(D), _full_spec((1, D)), pl.BlockSpec((TM, D), lambda i: (jnp.minimum(i, n_lat - 1), 0))],
        out_specs=[_row_spec(D), _full_spec((8, D))],
        out_shape=[jax.ShapeDtypeStruct((t, D), F32), jax.ShapeDtypeStruct((8, D), F32)],
        compiler_params=_cp(1),
    )(x, gain, target)


def local_step(xs, target, mods, norm1, norm2, norm_f, lbs, gnorm, cw8, cb, wts, n_lat, on_grads):
    t = xs.shape[0]
    saved = []
    cache = {}

    def W(name, idx, after=None):
        if (name, idx) not in cache:
            cache[(name, idx)] = wts(name, idx, after)
        return cache[(name, idx)]

    x = xs
    for i in range(DEPTH):
        j = i // 2
        rec = i % 2 == 0
        n1 = norm1[i:i + 1]
        n2 = norm2[i:i + 1]
        s = {"x_in": x}
        if rec:
            p = proj_fwd(x, n1, mods[i], 0, W("hin", j, x), n_lat, f"hin_fwd_{i}")
            o0, st0 = gla_fwd(p, lbs[j], 0, f"gla_fwd0_{i}")
            o1, st1 = gla_fwd(p, lbs[j], 1, f"gla_fwd1_{i}")
            ex = [o0, o1, p, gnorm[j:j + 1]]
            ex_specs = [_row_spec(D), _row_spec(D), _col_spec(4), _full_spec((1, D))]
            xm, y, ab = outproj_fwd(readout_prologue, ex, ex_specs, W("hout", j, o1), x, mods[i], 0, n_lat, f"hout_fwd_{i}")
            s.update(st0=st0, st1=st1)
        else:
            sft = 1 if j % 2 == 0 else CH
            p = proj_fwd(x, n1, mods[i], 0, W("cin", j, x), n_lat, f"cin_fwd_{i}")
            ex = _conv_args(sft, p, cw8[j], cb[j])
            ex_specs = _conv_specs(sft, t)
            xm, y, ab = outproj_fwd(make_conv_prologue(n_lat, sft), ex, ex_specs, W("cout", j, p), x, mods[i], 0, n_lat, f"cout_fwd_{i}")
        s.update(p=p, ex=ex, ex_specs=ex_specs, y_mix=y, ab_mix=ab, x_mid=xm)
        p1 = proj_fwd(xm, n2, mods[i], 3, W("w1", i, xm), n_lat, f"w1_fwd_{i}")
        x, y2, ab2 = outproj_fwd(mlp_prologue, [p1], [_row_spec(4 * D)], W("w2", i, p1), xm, mods[i], 3, n_lat, f"w2_fwd_{i}")
        s.update(p1=p1, y_mlp=y2, ab_mlp=ab2)
        saved.append(s)

    dx, acc_loss = loss_bwd(x, norm_f, target, n_lat, "loss")
    small = {"norm_f": acc_loss[0:1], "norm1": [None] * DEPTH, "norm2": [None] * DEPTH, "dmod": [None] * DEPTH,
             "gnorm": [None] * 2, "lb": [None] * 2, "cw": [None] * 2, "cb": [None] * 2}
    bshape = lambda w: jax.ShapeDtypeStruct((t, w), BF16)
    for i in reversed(range(DEPTH)):
        j = i // 2
        rec = i % 2 == 0
        s = saved[i]
        n1 = norm1[i:i + 1]
        n2 = norm2[i:i + 1]
        dyb, acc_g2, (dp1,) = outproj_bwd(mlp_epilogue, [s["p1"]], [_row_spec(4 * D)], [_row_spec(4 * D)], [bshape(4 * D)],
                                          W("w2", i), dx, s["y_mlp"], mods[i], 3, n_lat, f"w2_bwd_{i}")
        g = {"w2": dw_tn(s["ab_mlp"], dyb, 4, True, f"w2_dw_{i}")}
        dx, hb, dpb, acc_n2 = proj_bwd([dp1], W("w1", i), s["x_mid"], n2, mods[i], 3, dx, n_lat, f"w1_bwd_{i}")
        g["w1"] = dw_tn(hb, dpb, 4, False, f"w1_dw_{i}")
        if rec:
            dyb, acc_g1, (do, dgate) = outproj_bwd(
                readout_epilogue, s["ex"], s["ex_specs"], [_row_spec(D), _row_spec(D)],
                [jax.ShapeDtypeStruct((t, D), F32), bshape(D)], W("hout", j), dx, s["y_mix"], mods[i], 0, n_lat, f"hout_bwd_{i}")
            g["hout"] = dw_tn(s["ab_mix"], dyb, 1, False, f"hout_dw_{i}")
            dz0, dv0, dq0, acc_l0 = gla_bwd(s["p"], lbs[j], do, s["st0"], 0, None, f"gla_bwd0_{i}")
            dz1, dv, dq, acc_l1 = gla_bwd(s["p"], lbs[j], do, s["st1"], 1, (dv0, dq0), f"gla_bwd1_{i}")
            dx, hb, dpb, acc_n1 = proj_bwd([dz0, dz1, dv, dq, dgate], W("hin", j), s["x_in"], n1, mods[i], 0, dx, n_lat, f"hin_bwd_{i}")
            g["hin"] = dw_tn(hb, dpb, 4, False, f"hin_dw_{i}")
            small["gnorm"][j] = acc_g1[2:3]
            small["lb"][j] = jnp.concatenate([acc_l0[0:1], acc_l1[0:1]], axis=0)
        else:
            sft = 1 if j % 2 == 0 else CH
            dyb, acc_g1, (dconv, dgb) = outproj_bwd(
                make_conv_epilogue(n_lat, sft), s["ex"], s["ex_specs"], [_row_spec(D), _row_spec(D)],
                [jax.ShapeDtypeStruct((t, D), F32), bshape(D)], W("cout", j), dx, s["y_mix"], mods[i], 0, n_lat, f"cout_bwd_{i}")
            g["cout"] = dw_tn(s["ab_mix"], dyb, 1, False, f"cout_dw_{i}")
            dgc, dxi, acc_c = conv_bwd(dconv, s["p"], cw8[j], sft, n_lat, f"conv_bwd_{i}")
            dx, hb, dpb, acc_n1 = proj_bwd([dgb, dgc, dxi], W("cin", j), s["x_in"], n1, mods[i], 0, dx, n_lat, f"cin_bwd_{i}")
            g["cin"] = dw_tn(hb, dpb, 4, False, f"cin_dw_{i}")
            small["cw"][j] = acc_c[0:3]
            small["cb"][j] = acc_c[3:4]
        small["norm1"][i] = acc_n1[0:1]
        small["norm2"][i] = acc_n2[0:1]
        z2 = jnp.zeros((2, D), F32)
        small["dmod"][i] = jnp.concatenate([acc_n1[1:3], acc_g1[0:1], acc_n2[1:3], acc_g2[0:1], z2,
                                            acc_n1[3:5], acc_g1[1:2], acc_n2[3:5], acc_g2[1:2], z2], axis=0)
        on_grads(i, g)
    return acc_loss[1:2], dx, small


RB = 256


def cast_to_slot(w2d, layer, k, chip1, name):
    c = w2d.shape[1]
    nblk = k // RB

    def body(chip_ref, w_ref, o_ref):
        o_ref[0] = w_ref[...].astype(BF16)

    return pl.pallas_call(
        body, name=name,
        grid_spec=pltpu.PrefetchScalarGridSpec(
            num_scalar_prefetch=1, grid=(nblk,),
            in_specs=[pl.BlockSpec((RB, c), lambda i, ch: (layer * nblk + i, 0))],
            out_specs=pl.BlockSpec((1, RB, c), lambda i, ch: (ch[0], i, 0))),
        out_shape=jax.ShapeDtypeStruct((4, k, c), BF16), compiler_params=_cp(1))(chip1, w2d)


def sum_slots(own, land, acc, layer, chip1, name):
    _, k, c = own.shape
    nblk = k // RB

    def body(chip_ref, own_ref, l1_ref, l2_ref, l3_ref, acc_ref, o_ref):
        o_ref[...] = ((own_ref[0].astype(F32) + l1_ref[0].astype(F32)) + l2_ref[0].astype(F32)) + l3_ref[0].astype(F32)

    slot = lambda d: pl.BlockSpec((1, RB, c), lambda i, ch: ((ch[0] + d) % 4, i, 0))
    return pl.pallas_call(
        body, name=name,
        grid_spec=pltpu.PrefetchScalarGridSpec(
            num_scalar_prefetch=1, grid=(nblk,),
            in_specs=[slot(0), slot(1), slot(2), slot(3), ANY],
            out_specs=pl.BlockSpec((RB, c), lambda i, ch: (layer * nblk + i, 0))),
        out_shape=jax.ShapeDtypeStruct(acc.shape, F32), input_output_aliases={5: 0}, compiler_params=_cp(1),
    )(chip1, own, land, land, land, acc)


def _adamw_math(w, g, m, v):
    m = ADAM_B1 * m + (1.0 - ADAM_B1) * g
    v = ADAM_B2 * v + (1.0 - ADAM_B2) * jnp.square(g)
    m_hat = m / (1.0 - ADAM_B1 ** ADAM_STEP)
    v_hat = v / (1.0 - ADAM_B2 ** ADAM_STEP)
    delta = -ADAM_LR * (m_hat / (jnp.sqrt(v_hat) + ADAM_EPS) + ADAM_WD * w)
    return delta, m, v


def adamw(gsrcs, w, m, v, name):
    r, c = w.shape
    rb = RB if r % RB == 0 else r
    n_g = len(gsrcs)

    def body(*refs):
        g = refs[0][...]
        for k in range(1, n_g):
            g = g + refs[k][...]
        w_ref, m_ref, v_ref, g_ref, d_ref, mo_ref, vo_ref = refs[n_g:]
        delta, mo, vo = _adamw_math(w_ref[...], g, m_ref[...], v_ref[...])
        g_ref[...] = g
        d_ref[...] = delta
        mo_ref[...] = mo
        vo_ref[...] = vo

    spec = pl.BlockSpec((rb, c), lambda i: (i, 0))
    shp = jax.ShapeDtypeStruct((r, c), F32)
    return pl.pallas_call(body, name=name, grid=(r // rb,), in_specs=[spec] * (n_g + 3), out_specs=[spec] * 4,
                          out_shape=[shp] * 4, compiler_params=_cp(1))(*gsrcs, w, m, v)


ADA_CB = 512


def ada_fwd(cvec, ada_w, bias, name):
    _, _, nc = ada_w.shape

    def body(c_ref, w_ref, b_ref, o_ref):
        cv = c_ref[...]
        a = (cv * _sigmoid(cv)).astype(BF16)
        o_ref[0] = _dot(a, w_ref[0].astype(BF16)) + b_ref[0]

    return pl.pallas_call(
        body, name=name, grid=(DEPTH, nc // ADA_CB),
        in_specs=[pl.BlockSpec((16, D), lambda i, j: (0, 0)), pl.BlockSpec((1, D, ADA_CB), lambda i, j: (i, 0, j)),
                  pl.BlockSpec((1, 1, ADA_CB), lambda i, j: (i, 0, j))],
        out_specs=pl.BlockSpec((1, 16, ADA_CB), lambda i, j: (i, 0, j)),
        out_shape=jax.ShapeDtypeStruct((DEPTH, 16, nc), F32), compiler_params=_cp(2),
    )(cvec, ada_w, bias)


def ada_bwd(cvec, dcols, ada_w, m, v, name):
    _, _, nc = ada_w.shape

    def body(c_ref, d_ref, w_ref, m_ref, v_ref, g_ref, dl_ref, mo_ref, vo_ref, acc_ref):
        @pl.when(jnp.logical_and(pl.program_id(0) == 0, pl.program_id(1) == 0))
        def _():
            acc_ref[...] = jnp.zeros_like(acc_ref)

        cv = c_ref[...]
        a = (cv * _sigmoid(cv)).astype(BF16)
        db = d_ref[0].astype(BF16)
        w = w_ref[0]
        g = _dot_tn(a, db)
        delta, mo, vo = _adamw_math(w, g, m_ref[0], v_ref[0])
        g_ref[0] = g
        dl_ref[0] = delta
        mo_ref[0] = mo
        vo_ref[0] = vo
        acc_ref[...] += _dot_nt(db[8:16, :], w.astype(BF16))

    wspec = pl.BlockSpec((1, D, ADA_CB), lambda i, j: (i, 0, j))
    wshape = jax.ShapeDtypeStruct(ada_w.shape, F32)
    return pl.pallas_call(
        body, name=name, grid=(DEPTH, nc // ADA_CB),
        in_specs=[pl.BlockSpec((16, D), lambda i, j: (0, 0)), pl.BlockSpec((1, 16, ADA_CB), lambda i, j: (i, 0, j)), wspec, wspec, wspec],
        out_specs=[wspec, wspec, wspec, wspec, pl.BlockSpec((8, D), lambda i, j: (0, 0))],
        out_shape=[wshape, wshape, wshape, wshape, jax.ShapeDtypeStruct((8, D), F32)], compiler_params=_cp(2),
    )(cvec, dcols, ada_w, m, v)


def _place():
    return lax.axis_index("x"), lax.axis_index("y"), lax.axis_index("c")


ANY = pl.BlockSpec(memory_space=pl.ANY)
VMEM_SPEC = pl.BlockSpec(memory_space=pltpu.VMEM)


def small_allgather(buf, name):
    r, c = buf.shape

    def body(in_ref, out_ref, send_sems, recv_sems, loc_sem):
        x, y, cc = _place()
        me = 4 * x + 2 * y + cc
        loc = pltpu.make_async_copy(in_ref, out_ref.at[me], loc_sem)
        loc.start()
        peers = []
        for k in range(1, 8):
            px = 1 - x if k & 4 else x
            py = 1 - y if k & 2 else y
            pc = 1 - cc if k & 1 else cc
            peers.append((px, py, pc))
        sends = []
        for k, peer in enumerate(peers):
            cp = pltpu.make_async_remote_copy(src_ref=in_ref, dst_ref=out_ref.at[me], send_sem=send_sems.at[k],
                                              recv_sem=recv_sems.at[k], device_id=peer, device_id_type=MESH)
            cp.start()
            sends.append(cp)
        for k, (px, py, pc) in enumerate(peers):
            pltpu.make_async_remote_copy(src_ref=in_ref, dst_ref=out_ref.at[4 * px + 2 * py + pc], send_sem=send_sems.at[k],
                                         recv_sem=recv_sems.at[k], device_id=(px, py, pc), device_id_type=MESH).wait_recv()
        for cp in sends:
            cp.wait_send()
        loc.wait()

    return pl.pallas_call(
        body, name=name, in_specs=[VMEM_SPEC], out_specs=VMEM_SPEC,
        out_shape=jax.ShapeDtypeStruct((8, r, c), buf.dtype),
        scratch_shapes=[pltpu.SemaphoreType.DMA((7,)), pltpu.SemaphoreType.DMA((7,)), pltpu.SemaphoreType.DMA],
    )(buf)


def _chip_peers(x, y):
    return [(1 - x, y), (x, 1 - y), (1 - x, 1 - y)]


HBM_SPEC = pl.BlockSpec(memory_space=pltpu.HBM)
SEM_SPEC = pl.BlockSpec(memory_space=pltpu.SEMAPHORE)
EFFECT = pltpu.SideEffectType.DATAFLOW_SIDE_EFFECTING


def _hbm(a):
    return pltpu.with_memory_space_constraint(a, pltpu.HBM)


def _split_copy(u, p, peer, dst_slot, chip, land_refs, src_refs, sem_refs, cc):
    px, py = peer
    src = land_refs[u].at[chip] if src_refs is None else src_refs[u].at[2 * px + py]
    return pltpu.make_async_remote_copy(src_ref=src, dst_ref=land_refs[u].at[dst_slot], send_sem=sem_refs[2 * u].at[p],
                                        recv_sem=sem_refs[2 * u + 1].at[p], device_id=(px, py, cc), device_id_type=MESH)


def split_start(lands, srcs, name):
    n = len(lands)
    ops = list(lands) + (list(srcs) if srcs is not None else [])
    n_ops = len(ops)

    def body(*refs):
        land_refs = refs[:n]
        src_refs = refs[n:n_ops] if srcs is not None else None
        sem_refs = refs[n_ops:n_ops + 2 * n]
        x, y, cc = _place()
        chip = 2 * x + y
        for u in range(n):
            for p, peer in enumerate(_chip_peers(x, y)):
                _split_copy(u, p, peer, chip, chip, land_refs, src_refs, sem_refs, cc).start()

    outs = pl.pallas_call(
        body, name=name, in_specs=[HBM_SPEC] * n_ops,
        out_specs=[SEM_SPEC] * (2 * n) + [HBM_SPEC] * n_ops,
        out_shape=[pltpu.SemaphoreType.DMA((3,))] * (2 * n) + [pltpu.HBM(a.shape, a.dtype) for a in ops],
        input_output_aliases={k: 2 * n + k for k in range(n_ops)},
        compiler_params=pltpu.CompilerParams(has_side_effects=EFFECT),
    )(*[_hbm(a) for a in ops])
    sems = list(outs[:2 * n])
    thru = list(outs[2 * n:])
    return sems, thru[:n], thru[n:]


def split_wait(lands, srcs, sems, after, name):
    n = len(lands)
    ops = list(lands) + (list(srcs) if srcs is not None else [])
    n_ops = len(ops)

    def body(*refs):
        land_refs = refs[:n]
        src_refs = refs[n:n_ops] if srcs is not None else None
        sem_refs = refs[n_ops:n_ops + 2 * n]
        x, y, cc = _place()
        chip = 2 * x + y
        for u in range(n):
            for p, peer in enumerate(_chip_peers(x, y)):
                cp = _split_copy(u, p, peer, 2 * peer[0] + peer[1], chip, land_refs, src_refs, sem_refs, cc)
                cp.wait_send()
                cp.wait_recv()

    outs = pl.pallas_call(
        body, name=name, in_specs=[HBM_SPEC] * n_ops + [SEM_SPEC] * (2 * n) + [ANY],
        out_specs=[HBM_SPEC] * n_ops, out_shape=[pltpu.HBM(a.shape, a.dtype) for a in ops],
        input_output_aliases={k: k for k in range(n_ops)},
        compiler_params=pltpu.CompilerParams(has_side_effects=EFFECT),
    )(*ops, *sems, after)
    return list(outs[:n]), list(outs[n:])


def sibling_exchange(parts, name):
    n = len(parts)

    def body(*refs):
        in_refs = refs[:n]
        out_refs = refs[n:2 * n]
        send_sems, recv_sems = refs[2 * n:]
        x, y, cc = _place()
        sib = (x, y, 1 - cc)
        cps = []
        for k in range(n):
            cp = pltpu.make_async_remote_copy(src_ref=in_refs[k], dst_ref=out_refs[k], send_sem=send_sems.at[k],
                                              recv_sem=recv_sems.at[k], device_id=sib, device_id_type=MESH)
            cp.start()
            cps.append(cp)
        for cp in cps:
            cp.wait_recv()
        for cp in cps:
            cp.wait_send()

    return pl.pallas_call(
        body, name=name, in_specs=[ANY] * n, out_specs=[ANY] * n,
        out_shape=[jax.ShapeDtypeStruct(p.shape, p.dtype) for p in parts],
        scratch_shapes=[pltpu.SemaphoreType.DMA((n,)), pltpu.SemaphoreType.DMA((n,))],
    )(*parts)


SMALL_ROWS = 88
FIN_ROWS = 72


def small_finish(g3, g4, c_ctx, lbp, name):
    def body(g3_ref, g4_ref, cc_ref, lbp_ref, o_ref, s_ref):
        s = g3_ref[0]
        for k in range(1, 8):
            s = s + g3_ref[k]
        s_ref[...] = s
        for i in range(DEPTH):
            o_ref[8 * i:8 * i + 8, :] = s_ref[16 * i:16 * i + 8, :] + s_ref[16 * i + 8:16 * i + 16, :]
        acc = g4_ref[0]
        for k in (2, 4, 6):
            acc = acc + g4_ref[k]
        cc = cc_ref[...]
        sg = _sigmoid(cc)
        row = jnp.sum(acc, axis=0, keepdims=True) * (sg * (1.0 + cc * (1.0 - sg)))
        o_ref[32:40, :] = jnp.broadcast_to(row, (8, D))
        o_ref[40:64, :] = s_ref[64:88, :]
        o_ref[64:72, :] = jnp.zeros((8, D), F32)
        for d in range(2):
            pp = lbp_ref[2 * d:2 * d + 1, :] * lbp_ref[2 * d + 1:2 * d + 2, :] * s_ref[75 + d:76 + d, :]
            o_ref[64 + 2 * d:65 + 2 * d, :] = -pp
            o_ref[65 + 2 * d:66 + 2 * d, :] = pp

    return pl.pallas_call(
        body, name=name, in_specs=[VMEM_SPEC] * 4, out_specs=VMEM_SPEC,
        out_shape=jax.ShapeDtypeStruct((FIN_ROWS, D), F32),
        scratch_shapes=[pltpu.VMEM((SMALL_ROWS, D), F32)],
    )(g3, g4, c_ctx, lbp)


def _pack_rows(arrs):
    flat = jnp.concatenate([a.reshape(-1) for a in arrs])
    n = -(-flat.shape[0] // (8 * D)) * 8 * D
    return jnp.pad(flat, (0, n - flat.shape[0])).reshape(n // D, D)


def _unpack_rows(packed, shapes):
    flat = packed.reshape(-1)
    outs, off = [], 0
    for s in shapes:
        size = 1
        for k in s:
            size *= k
        outs.append(flat[off:off + size].reshape(s))
        off += size
    return outs


def _pad8(a):
    return jnp.pad(a, ((0, 8 - a.shape[0]), (0, 0)))


def kernel(x, c, ctx, c_ctx, ada_w, ada_b, norm1, norm2, norm_f, mlp_w1, mlp_w2, hgrn_w_in, hgrn_lb, hgrn_gnorm, hgrn_w_out, conv_w_in, conv_w, conv_b, conv_w_out, loss_target, m_c_ctx, m_ada_w, m_ada_b, m_norm1, m_norm2, m_norm_f, m_mlp_w1, m_mlp_w2, m_hgrn_w_in, m_hgrn_lb, m_hgrn_gnorm, m_hgrn_w_out, m_conv_w_in, m_conv_w, m_conv_b, m_conv_w_out, v_c_ctx, v_ada_w, v_ada_b, v_norm1, v_norm2, v_norm_f, v_mlp_w1, v_mlp_w2, v_hgrn_w_in, v_hgrn_lb, v_hgrn_gnorm, v_hgrn_w_out, v_conv_w_in, v_conv_w, v_conv_b, v_conv_w_out):
    xi, yi, ci = _place()
    me = 4 * xi + 2 * yi + ci
    chip = 2 * xi + yi
    seq = x.shape[1]
    assert ctx.shape[1] == TM and seq % TM == 0 and (seq + TM) % TMW == 0
    n_lat = seq // TM
    sd = D // 4
    nca = ada_w.shape[2]
    xs = jnp.concatenate([x[0], ctx[0]], axis=0)

    sh_rows = jnp.concatenate([hgrn_lb.reshape(4, sd), conv_w.reshape(6, sd), conv_b.reshape(2, sd)], axis=0)
    buf1 = jnp.concatenate([c, jnp.pad(sh_rows, ((0, 0), (0, D - sd))), jnp.zeros((3, D), F32)], axis=0)
    g1 = small_allgather(buf1, "gather_small_in")
    cvec = jnp.concatenate([g1[:, 0, :], jnp.broadcast_to(c_ctx[None], (8, D))], axis=0)
    shf = g1[0::2, 1:13, :sd].transpose(1, 0, 2).reshape(12, D)
    lb_p = jax.nn.softmax(shf[0:4].reshape(2, 2, D), axis=1)
    lower = jnp.cumsum(lb_p, axis=1) - lb_p[:, :1]
    lbs = [lower[:, 0], lower[:, 1]]
    cw8 = [_pad8(shf[4:7]), _pad8(shf[7:10])]
    cb = [shf[10:11], shf[11:12]]

    bias = lax.dynamic_slice_in_dim(ada_b, chip * nca, nca, axis=1).reshape(DEPTH, 1, nca)
    ada_part = ada_fwd(cvec, ada_w, bias, "ada_fwd")
    g2 = small_allgather(ada_part.reshape(DEPTH * 16, nca), "gather_ada")
    ada_full = g2[0::2].reshape(4, DEPTH, 16, nca).transpose(1, 2, 0, 3).reshape(DEPTH, 16, 4 * nca)
    lat = lax.dynamic_slice_in_dim(ada_full, me, 1, axis=1)[:, 0]
    mods = [jnp.stack([_pad8(lat[i].reshape(6, D)), _pad8(ada_full[i, 8].reshape(6, D))]) for i in range(DEPTH)]

    big = [(mlp_w1, m_mlp_w1, v_mlp_w1), (mlp_w2, m_mlp_w2, v_mlp_w2), (hgrn_w_in, m_hgrn_w_in, v_hgrn_w_in),
           (hgrn_w_out, m_hgrn_w_out, v_hgrn_w_out), (conv_w_in, m_conv_w_in, v_conv_w_in), (conv_w_out, m_conv_w_out, v_conv_w_out)]
    big_names = ["w1", "w2", "hin", "hout", "cin", "cout"]
    flat2 = lambda a: a.reshape(a.shape[0] * a.shape[1], a.shape[2])
    tensors = dict(zip(big_names, big))
    chip1 = jnp.reshape(chip, (1,)).astype(jnp.int32)
    order = []
    for i in range(DEPTH):
        order += [("hin", i // 2), ("hout", i // 2)] if i % 2 == 0 else [("cin", i // 2), ("cout", i // 2)]
        order += [("w1", i), ("w2", i)]
    lands = [cast_to_slot(flat2(tensors[n][0]), idx, tensors[n][0].shape[1], chip1, f"cast_{n}_{idx}") for n, idx in order]
    w_sems, lands, _ = split_start(lands, None, "gather_start")
    unit = {key: u for u, key in enumerate(order)}

    def wts(n, idx, after):
        u = unit[(n, idx)]
        (w,), _ = split_wait([lands[u]], None, w_sems[2 * u:2 * u + 2], after, f"gather_wait_{n}_{idx}")
        return w.reshape(w.shape[0] * w.shape[1], w.shape[2]) if n in ("w2", "hout", "cout") else w

    started = []

    def on_grads(i, g):
        names = sorted(g)
        gs = [g[n].reshape(4, g[n].shape[0] * g[n].shape[1] // 4, g[n].shape[2]) for n in names]
        sems, zones, srcs = split_start([lax.empty(a.shape, BF16) for a in gs], gs, f"grad_start_{i}")
        started.append(([(n, i if n in ("w1", "w2") else i // 2) for n in names], sems, zones, srcs))

    lane, dx, small = local_step(xs, loss_target[0], mods, norm1, norm2, norm_f[None], lbs, hgrn_gnorm, cw8, cb, wts, n_lat, on_grads)
    loss = lax.psum(0.5 * jnp.sum(lane) / D, ("x", "y", "c"))
    grad_x = dx[:seq][None]

    z3 = jnp.zeros((3, D), F32)
    rows3 = jnp.concatenate(small["dmod"] + small["norm1"] + small["norm2"] + [small["norm_f"]] + small["gnorm"]
                            + [small["lb"][1]] + small["cw"] + small["cb"] + [z3], axis=0)
    g3 = small_allgather(rows3, "gather_small_out")
    dmat = g3[:, :64].reshape(8, DEPTH, 2, 8, D)[:, :, :, :6].transpose(1, 2, 0, 3, 4).reshape(DEPTH, 16, 6 * D)
    dcols = lax.dynamic_slice_in_dim(dmat, chip * nca, nca, axis=2)
    g_ada_w, d_ada_w, nm_ada_w, nv_ada_w, acc4 = ada_bwd(cvec, dcols, ada_w, m_ada_w, v_ada_w, "ada_bwd")
    g4 = small_allgather(acc4, "gather_cctx")
    fin = small_finish(g3, g4, c_ctx[None], _pad8(lb_p.reshape(4, D)), "small_finish")
    cols = lambda a: lax.dynamic_slice_in_dim(a, chip * sd, sd, axis=a.ndim - 1)
    small_g = [fin[32], fin[0:32].reshape(DEPTH, 8, D)[:, :6].reshape(DEPTH, 6 * D), fin[40:44], fin[44:48], fin[48], fin[49:51],
               cols(fin[64:68].reshape(2, 2, D)), cols(fin[53:59].reshape(2, 3, D)), cols(fin[59:61])]
    small_w = [c_ctx, ada_b, norm1, norm2, norm_f, hgrn_gnorm, hgrn_lb, conv_w, conv_b]
    small_m = [m_c_ctx, m_ada_b, m_norm1, m_norm2, m_norm_f, m_hgrn_gnorm, m_hgrn_lb, m_conv_w, m_conv_b]
    small_v = [v_c_ctx, v_ada_b, v_norm1, v_norm2, v_norm_f, v_hgrn_gnorm, v_hgrn_lb, v_conv_w, v_conv_b]
    shapes = [w.shape for w in small_w]
    packed = adamw([_pack_rows(small_g)], _pack_rows(small_w), _pack_rows(small_m), _pack_rows(small_v), "adamw_small")
    s_g, s_d, s_m, s_v = [_unpack_rows(p, shapes) for p in packed]

    keys, g_sems, g_zones, g_srcs = [], [], [], []
    for ks, sems, zones, srcs in started:
        keys += ks
        g_sems += sems
        g_zones += zones
        g_srcs += srcs
    g_zones, g_srcs = split_wait(g_zones, g_srcs, g_sems, fin, "grad_wait")
    acc = {n: lax.empty(flat2(w).shape, F32) for n, (w, _, _) in tensors.items()}
    for (n, idx), zone, own in zip(keys, g_zones, g_srcs):
        acc[n] = sum_slots(own, zone, acc[n], idx, chip1, f"sum_{n}_{idx}")
    partial = [acc[n] for n in big_names]
    other = sibling_exchange(partial, "sibling_exchange")
    b_g, b_d, b_m, b_v = [], [], [], []
    for (w, m, v), pm, po, n in zip(big, partial, other, big_names):
        outs = adamw([pm, po], flat2(w), flat2(m), flat2(v), f"adamw_{n}")
        for lst, a in zip((b_g, b_d, b_m, b_v), outs):
            lst.append(a.reshape(w.shape))

    def ordered(s, a, b):
        return [s[0], a, s[1], s[2], s[3], s[4], b[0], b[1], b[2], s[6], s[5], b[3], b[4], s[7], s[8], b[5]]

    return (loss, grad_x, *ordered(s_g, g_ada_w, b_g), *ordered(s_d, d_ada_w, b_d), *ordered(s_m, nm_ada_w, b_m),
            *ordered(s_v, nv_ada_w, b_v))
```

```python
import functools

import jax
import jax.numpy as jnp
from jax import lax
from jax.experimental import pallas as pl
from jax.experimental.pallas import tpu as pltpu

F32 = jnp.float32
BF16 = jnp.bfloat16
MESH = pl.DeviceIdType.MESH

D = 1024
HD = 128
NH = D // HD
CH = 64
TM = 256
TMW = 768
EPS = 1e-6
DEPTH = 4
VMEM_LIMIT = 56 * 1024 * 1024

ADAM_LR = 0.001
ADAM_B1 = 0.9
ADAM_B2 = 0.999
ADAM_EPS = 1e-08
ADAM_WD = 0.01
ADAM_STEP = 10


def _cp(n_grid):
    return pltpu.CompilerParams(dimension_semantics=("arbitrary",) * n_grid, vmem_limit_bytes=VMEM_LIMIT)


def _dot(a, b):
    return jnp.dot(a, b, preferred_element_type=F32)


def _dot_nt(a, b):
    return lax.dot_general(a, b, (((1,), (1,)), ((), ())), preferred_element_type=F32)


def _dot_tn(a, b):
    return lax.dot_general(a, b, (((0,), (0,)), ((), ())), preferred_element_type=F32)


def _sigmoid(z):
    return 1.0 / (1.0 + jnp.exp(-z))


def _norm_mod(x, gain, sh, sc):
    r = lax.rsqrt(jnp.mean(x * x, axis=-1, keepdims=True) + EPS)
    xn = x * r
    yn = xn * gain
    return r, xn, yn, yn * (1.0 + sc) + sh


def _row_spec(width):
    return pl.BlockSpec((TM, width), lambda i: (i, 0))


def _col_spec(col):
    return pl.BlockSpec((TM, D), lambda i: (i, col))


def _full_spec(shape):
    nd = len(shape)
    return pl.BlockSpec(shape, lambda i: (0,) * nd)


def _mod_spec(n_lat):
    return pl.BlockSpec((1, 8, D), lambda i: (i // n_lat, 0, 0))


def proj_fwd(x, gain, mod, m0, w4, n_lat, name):
    t = x.shape[0]
    nb, _, ns = w4.shape

    def body(x_ref, gain_ref, mod_ref, w_ref, p_ref):
        _, _, _, h = _norm_mod(x_ref[...], gain_ref[...], mod_ref[0, m0:m0 + 1, :], mod_ref[0, m0 + 1:m0 + 2, :])
        hb = h.astype(BF16)
        for c in range(nb):
            p_ref[:, c * ns:(c + 1) * ns] = _dot(hb, w_ref[c])

    return pl.pallas_call(
        body, name=name, grid=(t // TM,),
        in_specs=[_row_spec(D), _full_spec((1, D)), _mod_spec(n_lat), _full_spec(w4.shape)],
        out_specs=_row_spec(nb * ns),
        out_shape=jax.ShapeDtypeStruct((t, nb * ns), F32),
        compiler_params=_cp(1),
    )(x, gain, mod, w4)


def proj_bwd(parts, w4, x, gain, mod, m0, dx_in, n_lat, name):
    t = x.shape[0]
    nb, _, ns = w4.shape
    n = nb * ns
    n_parts = len(parts)
    widths = [p.shape[1] for p in parts]
    offs = [sum(widths[:k]) for k in range(n_parts)]
    assert sum(widths) == n
    single = n_parts == 1

    def body(*refs):
        part_refs = refs[:n_parts]
        w_ref, x_ref, gain_ref, mod_ref, dxin_ref = refs[n_parts:n_parts + 5]
        rest = refs[n_parts + 5:]
        if single:
            dx_ref, hb_ref, acc_ref = rest
            src = part_refs[0]
        else:
            dx_ref, hb_ref, acc_ref, dpb_ref = rest
            for p_ref, off, w in zip(part_refs, offs, widths):
                dpb_ref[:, off:off + w] = p_ref[...]
            src = dpb_ref
        i = pl.program_id(0)

        @pl.when(i == 0)
        def _():
            acc_ref[...] = jnp.zeros_like(acc_ref)

        gain = gain_ref[...]
        sc = mod_ref[0, m0 + 1:m0 + 2, :]
        r, xn, yn, h = _norm_mod(x_ref[...], gain, mod_ref[0, m0:m0 + 1, :], sc)
        hb_ref[...] = h.astype(BF16)
        dh = _dot_nt(src[:, 0:ns], w_ref[0])
        for c in range(1, nb):
            dh = dh + _dot_nt(src[:, c * ns:(c + 1) * ns], w_ref[c])
        dsh = jnp.sum(dh, axis=0, keepdims=True)
        dsc = jnp.sum(dh * yn, axis=0, keepdims=True)
        dyn = dh * (1.0 + sc)
        dgain = jnp.sum(dyn * xn, axis=0, keepdims=True)
        dxn = dyn * gain
        dx = r * (dxn - xn * jnp.mean(dxn * xn, axis=-1, keepdims=True))
        dx_ref[...] = dxin_ref[...] + dx
        latf = (i < n_lat).astype(F32)
        ctxf = 1.0 - latf
        acc_ref[0:1, :] += dgain
        acc_ref[1:2, :] += dsh * latf
        acc_ref[2:3, :] += dsc * latf
        acc_ref[3:4, :] += dsh * ctxf
        acc_ref[4:5, :] += dsc * ctxf

    out_specs = [_row_spec(D), _row_spec(D), _full_spec((8, D))]
    out_shape = [jax.ShapeDtypeStruct((t, D), F32), jax.ShapeDtypeStruct((t, D), BF16), jax.ShapeDtypeStruct((8, D), F32)]
    if not single:
        out_specs.append(_row_spec(n))
        out_shape.append(jax.ShapeDtypeStruct((t, n), BF16))
    outs = pl.pallas_call(
        body, name=name, grid=(t // TM,),
        in_specs=[_row_spec(w) for w in widths]
        + [_full_spec(w4.shape), _row_spec(D), _full_spec((1, D)), _mod_spec(n_lat), _row_spec(D)],
        out_specs=out_specs, out_shape=out_shape, compiler_params=_cp(1),
    )(*parts, w4, x, gain, mod, dx_in)
    if single:
        return outs[0], outs[1], parts[0], outs[2]
    return outs[0], outs[1], outs[3], outs[2]


def dw_tn(a, b, nb, a_blocked, name):
    t = a.shape[0]
    ka = a.shape[1] // nb if a_blocked else a.shape[1]
    kb = b.shape[1] if a_blocked else b.shape[1] // nb
    n_k = t // TMW

    def body(a_ref, b_ref, o_ref, acc):
        k = pl.program_id(1)

        @pl.when(k == 0)
        def _():
            acc[...] = jnp.zeros_like(acc)

        acc[...] += _dot_tn(a_ref[...], b_ref[...])

        @pl.when(k == n_k - 1)
        def _():
            o_ref[0] = acc[...].astype(BF16)

    a_spec = pl.BlockSpec((TMW, ka), (lambda j, k: (k, j)) if a_blocked else (lambda j, k: (k, 0)))
    b_spec = pl.BlockSpec((TMW, kb), (lambda j, k: (k, 0)) if a_blocked else (lambda j, k: (k, j)))
    return pl.pallas_call(
        body, name=name, grid=(nb, n_k),
        in_specs=[a_spec, b_spec],
        out_specs=pl.BlockSpec((1, ka, kb), lambda j, k: (j, 0, 0)),
        out_shape=jax.ShapeDtypeStruct((nb, ka, kb), BF16),
        scratch_shapes=[pltpu.VMEM((ka, kb), F32)],
        compiler_params=_cp(2),
    )(a, b)


def outproj_fwd(prologue, extras, extra_specs, w, x, mod, m0, n_lat, name):
    t = x.shape[0]
    k = w.shape[0]
    n_extra = len(extras)

    def body(*refs):
        ex = refs[:n_extra]
        w_ref, x_ref, mod_ref, xo_ref, y_ref, ab_ref = refs[n_extra:]
        ab = prologue(pl.program_id(0), *ex).astype(BF16)
        ab_ref[...] = ab
        y = _dot(ab, w_ref[...])
        y_ref[...] = y
        xo_ref[...] = x_ref[...] + mod_ref[0, m0 + 2:m0 + 3, :] * y

    return pl.pallas_call(
        body, name=name, grid=(t // TM,),
        in_specs=list(extra_specs) + [_full_spec(w.shape), _row_spec(D), _mod_spec(n_lat)],
        out_specs=[_row_spec(D), _row_spec(D), _row_spec(k)],
        out_shape=[jax.ShapeDtypeStruct((t, D), F32), jax.ShapeDtypeStruct((t, D), F32), jax.ShapeDtypeStruct((t, k), BF16)],
        compiler_params=_cp(1),
    )(*extras, w, x, mod)


def outproj_bwd(epilogue, extras, extra_specs, ep_out_specs, ep_out_shapes, w, dxn, y, mod, m0, n_lat, name):
    t = dxn.shape[0]
    n_extra = len(extras)

    def body(*refs):
        ex = refs[:n_extra]
        w_ref, dxn_ref, y_ref, mod_ref, dyb_ref, acc_ref = refs[n_extra:n_extra + 6]
        ep_outs = refs[n_extra + 6:]
        i = pl.program_id(0)

        @pl.when(i == 0)
        def _():
            acc_ref[...] = jnp.zeros_like(acc_ref)

        dxv = dxn_ref[...]
        dyb = (dxv * mod_ref[0, m0 + 2:m0 + 3, :]).astype(BF16)
        dyb_ref[...] = dyb
        dg = jnp.sum(dxv * y_ref[...], axis=0, keepdims=True)
        latf = (i < n_lat).astype(F32)
        acc_ref[0:1, :] += dg * latf
        acc_ref[1:2, :] += dg * (1.0 - latf)
        epilogue(i, _dot_nt(dyb, w_ref[...]), ex, ep_outs, acc_ref)

    outs = pl.pallas_call(
        body, name=name, grid=(t // TM,),
        in_specs=list(extra_specs) + [_full_spec(w.shape), _row_spec(D), _row_spec(D), _mod_spec(n_lat)],
        out_specs=[_row_spec(D), _full_spec((8, D))] + list(ep_out_specs),
        out_shape=[jax.ShapeDtypeStruct((t, D), BF16), jax.ShapeDtypeStruct((8, D), F32)] + list(ep_out_shapes),
        compiler_params=_cp(1),
    )(*extras, w, dxn, y, mod)
    return outs[0], outs[1], outs[2:]


def mlp_fwd(x, gain, mod, w1, w2, n_lat, name):
    t = x.shape[0]
    nb, _, ns = w1.shape

    def body(x_ref, gain_ref, mod_ref, w1_ref, w2_ref, xo_ref, y_ref, ab_ref):
        x = x_ref[...]
        _, _, _, h = _norm_mod(x, gain_ref[...], mod_ref[0, 3:4, :], mod_ref[0, 4:5, :])
        hb = h.astype(BF16)
        y = None
        for c in range(nb):
            a = jnp.square(jnp.maximum(_dot(hb, w1_ref[c]), 0.0)).astype(BF16)
            ab_ref[:, c * ns:(c + 1) * ns] = a
            yc = _dot(a, w2_ref[c * ns:(c + 1) * ns, :])
            y = yc if y is None else y + yc
        y_ref[...] = y
        xo_ref[...] = x + mod_ref[0, 5:6, :] * y

    return pl.pallas_call(
        body, name=name, grid=(t // TM,),
        in_specs=[_row_spec(D), _full_spec((1, D)), _mod_spec(n_lat), _full_spec(w1.shape), _full_spec(w2.shape)],
        out_specs=[_row_spec(D), _row_spec(D), _row_spec(nb * ns)],
        out_shape=[jax.ShapeDtypeStruct((t, D), F32), jax.ShapeDtypeStruct((t, D), F32), jax.ShapeDtypeStruct((t, nb * ns), BF16)],
        compiler_params=_cp(1),
    )(x, gain, mod, w1, w2)


def mlp_bwd(dxn, y, ab, x, gain, mod, w1, w2, n_lat, name):
    t = x.shape[0]
    nb, _, ns = w1.shape

    def body(dxn_ref, y_ref, ab_ref, x_ref, gain_ref, mod_ref, w1_ref, w2_ref, dx_ref, dyb_ref, dp_ref, hb_ref, acc_ref):
        i = pl.program_id(0)

        @pl.when(i == 0)
        def _():
            acc_ref[...] = jnp.zeros_like(acc_ref)

        dxv = dxn_ref[...]
        dyb = (dxv * mod_ref[0, 5:6, :]).astype(BF16)
        dyb_ref[...] = dyb
        dg = jnp.sum(dxv * y_ref[...], axis=0, keepdims=True)
        gain = gain_ref[...]
        sc = mod_ref[0, 4:5, :]
        r, xn, yn, h = _norm_mod(x_ref[...], gain, mod_ref[0, 3:4, :], sc)
        hb_ref[...] = h.astype(BF16)
        dh = None
        for c in range(nb):
            cols = slice(c * ns, (c + 1) * ns)
            da = _dot_nt(dyb, w2_ref[cols, :])
            dp = (da * (2.0 * jnp.sqrt(ab_ref[:, cols].astype(F32)))).astype(BF16)
            dp_ref[:, cols] = dp
            d = _dot_nt(dp, w1_ref[c])
            dh = d if dh is None else dh + d
        dsh = jnp.sum(dh, axis=0, keepdims=True)
        dsc = jnp.sum(dh * yn, axis=0, keepdims=True)
        dyn = dh * (1.0 + sc)
        dgain = jnp.sum(dyn * xn, axis=0, keepdims=True)
        dxn_ = dyn * gain
        dx_ref[...] = dxv + r * (dxn_ - xn * jnp.mean(dxn_ * xn, axis=-1, keepdims=True))
        latf = (i < n_lat).astype(F32)
        ctxf = 1.0 - latf
        acc_ref[0:1, :] += dgain
        acc_ref[1:2, :] += dsh * latf
        acc_ref[2:3, :] += dsc * latf
        acc_ref[3:4, :] += dsh * ctxf
        acc_ref[4:5, :] += dsc * ctxf
        acc_ref[5:6, :] += dg * latf
        acc_ref[6:7, :] += dg * ctxf

    return pl.pallas_call(
        body, name=name, grid=(t // TM,),
        in_specs=[_row_spec(D), _row_spec(D), _row_spec(nb * ns), _row_spec(D), _full_spec((1, D)), _mod_spec(n_lat),
                  _full_spec(w1.shape), _full_spec(w2.shape)],
        out_specs=[_row_spec(D), _row_spec(D), _row_spec(nb * ns), _row_spec(D), _full_spec((8, D))],
        out_shape=[jax.ShapeDtypeStruct((t, D), F32), jax.ShapeDtypeStruct((t, D), BF16), jax.ShapeDtypeStruct((t, nb * ns), BF16),
                   jax.ShapeDtypeStruct((t, D), BF16), jax.ShapeDtypeStruct((8, D), F32)],
        compiler_params=_cp(1),
    )(dxn, y, ab, x, gain, mod, w1, w2)


def readout_prologue(i, o0_ref, o1_ref, gate_ref, gn_ref):
    o = o0_ref[...] + o1_ref[...]
    gate = gate_ref[...]
    w = gn_ref[...] * (gate * _sigmoid(gate))
    pieces = []
    for h in range(NH):
        sl = slice(h * HD, (h + 1) * HD)
        oh = o[:, sl]
        pieces.append(oh * lax.rsqrt(jnp.mean(oh * oh, axis=-1, keepdims=True) + EPS) * w[:, sl])
    return jnp.concatenate(pieces, axis=1)


def readout_epilogue(i, da, ex, outs, acc_ref):
    o0_ref, o1_ref, gate_ref, gn_ref = ex
    do_ref, dgate_ref = outs
    o = o0_ref[...] + o1_ref[...]
    gate = gate_ref[...]
    gn = gn_ref[...]
    sg = _sigmoid(gate)
    silu = gate * sg
    dsilu = sg * (1.0 + gate * (1.0 - sg))
    for h in range(NH):
        sl = slice(h * HD, (h + 1) * HD)
        oh = o[:, sl]
        r = lax.rsqrt(jnp.mean(oh * oh, axis=-1, keepdims=True) + EPS)
        nh = oh * r
        dah = da[:, sl]
        acc_ref[2:3, sl] += jnp.sum(dah * nh * silu[:, sl], axis=0, keepdims=True)
        dgate_ref[:, sl] = (dah * nh * gn[:, sl] * dsilu[:, sl]).astype(BF16)
        dn = dah * gn[:, sl] * silu[:, sl]
        do_ref[:, sl] = r * (dn - nh * jnp.mean(dn * nh, axis=-1, keepdims=True))


def _seg_masks(i, n_lat):
    rows = lax.broadcasted_iota(jnp.int32, (TM, 1), 0)
    latf = (i < n_lat).astype(F32)
    ctxf = 1.0 - latf
    prev_ok = (rows % CH != 0).astype(F32) * latf + (rows != 0).astype(F32) * ctxf
    next_ok = (rows % CH != CH - 1).astype(F32) * latf + (rows != TM - 1).astype(F32) * ctxf
    return prev_ok, next_ok


def _shifts(i, n_lat, sft, cur, halo_prev, halo_next):
    if sft == 1:
        prev_ok, next_ok = _seg_masks(i, n_lat)
        return pltpu.roll(cur, 1, 0) * prev_ok, pltpu.roll(cur, TM - 1, 0) * next_ok
    has_prev = jnp.logical_and(i > 0, i < n_lat).astype(F32)
    has_next = (i < n_lat - 1).astype(F32)
    prev = jnp.concatenate([halo_prev * has_prev, cur[:TM - CH]], axis=0)
    nxt = jnp.concatenate([cur[CH:], halo_next * has_next], axis=0)
    return prev, nxt


def _conv_u(sft, ex):
    if sft == 1:
        gb_ref, gc_ref, xi_ref, cw_ref, cb_ref = ex
        return gb_ref, gc_ref[...] * xi_ref[...], None, None, cw_ref, cb_ref
    gb_ref, gc_ref, xi_ref, gcp_ref, xip_ref, gcn_ref, xin_ref, cw_ref, cb_ref = ex
    return gb_ref, gc_ref[...] * xi_ref[...], gcp_ref[...] * xip_ref[...], gcn_ref[...] * xin_ref[...], cw_ref, cb_ref


def _conv_value(i, n_lat, sft, ex):
    gb_ref, u, up, un, cw_ref, cb_ref = _conv_u(sft, ex)
    u_prev, u_next = _shifts(i, n_lat, sft, u, up, un)
    return gb_ref, cb_ref[...] + cw_ref[0:1, :] * u_prev + cw_ref[1:2, :] * u + cw_ref[2:3, :] * u_next


def make_conv_prologue(n_lat, sft):
    def prologue(i, *ex):
        gb_ref, conv = _conv_value(i, n_lat, sft, ex)
        return gb_ref[...] * conv
    return prologue


def make_conv_epilogue(n_lat, sft):
    def epilogue(i, da, ex, outs, acc_ref):
        gb_ref, conv = _conv_value(i, n_lat, sft, ex)
        outs[0][...] = da * gb_ref[...]
        outs[1][...] = (da * conv).astype(BF16)
    return epilogue


def _conv_specs(sft, t):
    specs = [_col_spec(0), _col_spec(1), _col_spec(2)]
    if sft != 1:
        per = TM // CH
        last = t // CH - 1
        for fn in (lambda i: jnp.maximum(i * per - 1, 0), lambda i: jnp.minimum(i * per + per, last)):
            for col in (1, 2):
                specs.append(pl.BlockSpec((CH, D), functools.partial(lambda i, f, c: (f(i), c), f=fn, c=col)))
    return specs + [_full_spec((8, D)), _full_spec((1, D))]


def _conv_args(sft, p, cw8, cb):
    return [p] * (3 if sft == 1 else 7) + [cw8, cb]


def conv_bwd(dconv, p, cw8, sft, n_lat, name):
    t = dconv.shape[0]
    halo = sft != 1

    def body(*refs):
        if halo:
            dc_ref, dcp_ref, dcn_ref, gc_ref, xi_ref, gcp_ref, xip_ref, gcn_ref, xin_ref, cw_ref, dgc_ref, dxi_ref, acc_ref = refs
            up, un = gcp_ref[...] * xip_ref[...], gcn_ref[...] * xin_ref[...]
            dcp, dcn = dcp_ref[...], dcn_ref[...]
        else:
            dc_ref, gc_ref, xi_ref, cw_ref, dgc_ref, dxi_ref, acc_ref = refs
            up = un = dcp = dcn = None
        i = pl.program_id(0)

        @pl.when(i == 0)
        def _():
            acc_ref[...] = jnp.zeros_like(acc_ref)

        gc = gc_ref[...]
        xi = xi_ref[...]
        u = gc * xi
        dc = dc_ref[...]
        u_prev, u_next = _shifts(i, n_lat, sft, u, up, un)
        dc_prev, dc_next = _shifts(i, n_lat, sft, dc, dcp, dcn)
        acc_ref[0:1, :] += jnp.sum(dc * u_prev, axis=0, keepdims=True)
        acc_ref[1:2, :] += jnp.sum(dc * u, axis=0, keepdims=True)
        acc_ref[2:3, :] += jnp.sum(dc * u_next, axis=0, keepdims=True)
        acc_ref[3:4, :] += jnp.sum(dc, axis=0, keepdims=True)
        du = cw_ref[0:1, :] * dc_next + cw_ref[1:2, :] * dc + cw_ref[2:3, :] * dc_prev
        dgc_ref[...] = (du * xi).astype(BF16)
        dxi_ref[...] = (du * gc).astype(BF16)

    per = TM // CH
    last = t // CH - 1
    prev_i = lambda i: jnp.maximum(i * per - 1, 0)
    next_i = lambda i: jnp.minimum(i * per + per, last)
    if halo:
        in_specs = [_row_spec(D), pl.BlockSpec((CH, D), lambda i: (prev_i(i), 0)), pl.BlockSpec((CH, D), lambda i: (next_i(i), 0)),
                    _col_spec(1), _col_spec(2),
                    pl.BlockSpec((CH, D), lambda i: (prev_i(i), 1)), pl.BlockSpec((CH, D), lambda i: (prev_i(i), 2)),
                    pl.BlockSpec((CH, D), lambda i: (next_i(i), 1)), pl.BlockSpec((CH, D), lambda i: (next_i(i), 2)),
                    _full_spec((8, D))]
        args = [dconv, dconv, dconv, p, p, p, p, p, p, cw8]
    else:
        in_specs = [_row_spec(D), _col_spec(1), _col_spec(2), _full_spec((8, D))]
        args = [dconv, p, p, cw8]
    return pl.pallas_call(
        body, name=name, grid=(t // TM,), in_specs=in_specs,
        out_specs=[_row_spec(D), _row_spec(D), _full_spec((8, D))],
        out_shape=[jax.ShapeDtypeStruct((t, D), BF16), jax.ShapeDtypeStruct((t, D), BF16), jax.ShapeDtypeStruct((8, D), F32)],
        compiler_params=_cp(1),
    )(*args)


def _chunk_cumsum(g, reverse):
    rows = lax.broadcasted_iota(jnp.int32, (g.shape[0], 1), 0) % CH
    n = g.shape[0]
    s = 1
    while s < CH:
        if reverse:
            g = g + pltpu.roll(g, n - s, 0) * (rows < CH - s).astype(F32)
        else:
            g = g + pltpu.roll(g, s, 0) * (rows >= s).astype(F32)
        s *= 2
    return g


def _gate_values(z, lb):
    sig = _sigmoid(z)
    f = lb + (1.0 - lb) * sig
    return sig, f


def _tri(direction, transposed):
    r = lax.broadcasted_iota(jnp.int32, (CH, CH), 0)
    c = lax.broadcasted_iota(jnp.int32, (CH, CH), 1)
    lower = (direction == 0) != transposed
    return r >= c if lower else r <= c


def _gla_rows(direction):
    return (CH // 2 - 1, CH - 1) if direction == 0 else (CH // 2, 0)


def gla_fwd(p, lb2, direction, name):
    t = p.shape[0]
    nt = t // TM
    per = TM // CH
    ref_row, last_row = _gla_rows(direction)
    tile = (lambda i: (i + nt - 1) % nt) if direction == 0 else (lambda i: nt - 1 - i)

    def body(z_ref, v_ref, qr_ref, lb_ref, o_ref, s_ref, st, q_s, k_s, c_s):
        @pl.when(pl.program_id(0) == 0)
        def _():
            st[...] = jnp.zeros_like(st)

        _, f = _gate_values(z_ref[...], lb_ref[direction:direction + 1, :])
        k_s[...] = 1.0 - f
        c_s[...] = _chunk_cumsum(jnp.log(f), direction == 1)
        qr = qr_ref[...]
        q_s[...] = qr * _sigmoid(qr)
        mask = _tri(direction, False)
        for it in range(per):
            ci = it if direction == 0 else per - 1 - it
            r0 = ci * CH
            cum = c_s[r0:r0 + CH, :]
            ref = c_s[r0 + ref_row:r0 + ref_row + 1, :]
            last = c_s[r0 + last_row:r0 + last_row + 1, :]
            q = q_s[r0:r0 + CH, :]
            k = k_s[r0:r0 + CH, :]
            qh = (q * jnp.exp(cum)).astype(BF16)
            qt = (q * jnp.exp(cum - ref)).astype(BF16)
            kt = (k * jnp.exp(ref - cum)).astype(BF16)
            kb = (k * jnp.exp(last - cum)).astype(BF16)
            el = jnp.exp(last)
            vb = v_ref[r0:r0 + CH, :].astype(BF16)
            for h in range(NH):
                sl = slice(h * HD, (h + 1) * HD)
                s_t = st[h]
                s_ref[ci, h] = s_t
                sc = jnp.where(mask, _dot_nt(qt[:, sl], kt[:, sl]), 0.0)
                o_ref[r0:r0 + CH, sl] = _dot_nt(qh[:, sl], s_t.astype(BF16)) + _dot(sc.astype(BF16), vb[:, sl])
                st[h] = s_t * el[:, sl] + _dot_tn(vb[:, sl], kb[:, sl])

    tspec = lambda col: pl.BlockSpec((TM, D), lambda i: (tile(i), col))
    return pl.pallas_call(
        body, name=name, grid=(nt,),
        in_specs=[tspec(direction), tspec(2), tspec(3), _full_spec((2, D))],
        out_specs=[pl.BlockSpec((TM, D), lambda i: (tile(i), 0)), pl.BlockSpec((per, NH, HD, HD), lambda i: (tile(i), 0, 0, 0))],
        out_shape=[jax.ShapeDtypeStruct((t, D), F32), jax.ShapeDtypeStruct((t // CH, NH, HD, HD), F32)],
        scratch_shapes=[pltpu.VMEM((NH, HD, HD), F32), pltpu.VMEM((TM, D), F32), pltpu.VMEM((TM, D), F32), pltpu.VMEM((TM, D), F32)],
        compiler_params=_cp(1),
    )(p, p, p, lb2)


def gla_bwd(p, lb2, do, states, direction, prev, name):
    t = p.shape[0]
    nt = t // TM
    per = TM // CH
    ref_row, last_row = _gla_rows(direction)
    tile = (lambda i: (2 * nt - 2 - i) % nt) if direction == 0 else (lambda i: i)
    final = prev is not None
    n_in = 8 if final else 6

    def body(*refs):
        z_ref, v_ref, qr_ref, lb_ref, do_ref, s_ref = refs[:6]
        dz_ref, dv_ref, dq_ref, acc_ref, dst, q_s, k_s, c_s, dq_s, dk_s, dl_s = refs[n_in:]

        @pl.when(pl.program_id(0) == 0)
        def _():
            dst[...] = jnp.zeros_like(dst)
            acc_ref[...] = jnp.zeros_like(acc_ref)

        lb = lb_ref[direction:direction + 1, :]
        sig, f = _gate_values(z_ref[...], lb)
        k_s[...] = 1.0 - f
        c_s[...] = _chunk_cumsum(jnp.log(f), direction == 1)
        qr = qr_ref[...]
        sq = _sigmoid(qr)
        q_s[...] = qr * sq
        mask = _tri(direction, False)
        mask_t = _tri(direction, True)
        is_last = lax.broadcasted_iota(jnp.int32, (CH, 1), 0) == last_row
        for it in range(per):
            ci = per - 1 - it if direction == 0 else it
            r0 = ci * CH
            cum = c_s[r0:r0 + CH, :]
            ref = c_s[r0 + ref_row:r0 + ref_row + 1, :]
            last = c_s[r0 + last_row:r0 + last_row + 1, :]
            q = q_s[r0:r0 + CH, :]
            k = k_s[r0:r0 + CH, :]
            e_h = jnp.exp(cum)
            e_t = jnp.exp(cum - ref)
            e_kt = jnp.exp(ref - cum)
            e_kb = jnp.exp(last - cum)
            el = jnp.exp(last)
            qh = (q * e_h).astype(BF16)
            qt = (q * e_t).astype(BF16)
            kt = (k * e_kt).astype(BF16)
            kbf = k * e_kb
            kb = kbf.astype(BF16)
            vb = v_ref[r0:r0 + CH, :].astype(BF16)
            dob = do_ref[r0:r0 + CH, :].astype(BF16)
            for h in range(NH):
                sl = slice(h * HD, (h + 1) * HD)
                s_t = s_ref[ci, h]
                ds_t = dst[h]
                ds_b = ds_t.astype(BF16)
                d_a = jnp.where(mask, _dot_nt(dob[:, sl], vb[:, sl]), 0.0).astype(BF16)
                a_t = jnp.where(mask_t, _dot_nt(kt[:, sl], qt[:, sl]), 0.0).astype(BF16)
                d_at = jnp.where(mask_t, _dot_nt(vb[:, sl], dob[:, sl]), 0.0).astype(BF16)
                dv = _dot(a_t, dob[:, sl]) + _dot_nt(kb[:, sl], ds_b)
                dqh = _dot(dob[:, sl], s_t.astype(BF16))
                dqt = _dot(d_a, kt[:, sl])
                dkt = _dot(d_at, qt[:, sl])
                dkb = _dot(vb[:, sl], ds_b)
                dl_s[it:it + 1, sl] = (el[:, sl] * jnp.sum(ds_t * s_t, axis=0, keepdims=True)
                                       + jnp.sum(dkb * kbf[:, sl], axis=0, keepdims=True))
                dst[h] = ds_t * el[:, sl] + _dot_tn(dob[:, sl], qh[:, sl])
                dq_s[r0:r0 + CH, sl] = dqh * e_h[:, sl] + dqt * e_t[:, sl]
                dk_s[r0:r0 + CH, sl] = dkt * e_kt[:, sl] + dkb * e_kb[:, sl]
                if final:
                    dv_ref[r0:r0 + CH, sl] = (refs[6][r0:r0 + CH, sl] + dv).astype(BF16)
                else:
                    dv_ref[r0:r0 + CH, sl] = dv
            dcum = dq_s[r0:r0 + CH, :] * q - dk_s[r0:r0 + CH, :] * k
            c_s[r0:r0 + CH, :] = dcum + jnp.where(is_last, dl_s[it:it + 1, :], 0.0)
        dg = _chunk_cumsum(c_s[...], direction == 0)
        df = dg / f - dk_s[...]
        acc_ref[0:1, :] += jnp.sum(df * (1.0 - sig), axis=0, keepdims=True)
        dz_ref[...] = (df * (1.0 - lb) * sig * (1.0 - sig)).astype(BF16)
        if final:
            dq_ref[...] = ((refs[7][...] + dq_s[...]) * (sq * (1.0 + qr * (1.0 - sq)))).astype(BF16)
        else:
            dq_ref[...] = dq_s[...]

    tspec = lambda col: pl.BlockSpec((TM, D), lambda i: (tile(i), col))
    sspec = pl.BlockSpec((per, NH, HD, HD), lambda i: (tile(i), 0, 0, 0))
    in_specs = [tspec(direction), tspec(2), tspec(3), _full_spec((2, D)), tspec(0), sspec]
    args = [p, p, p, lb2, do, states]
    if final:
        in_specs += [tspec(0), tspec(0)]
        args += list(prev)
    odt = BF16 if final else F32
    return pl.pallas_call(
        body, name=name, grid=(nt,), in_specs=in_specs,
        out_specs=[tspec(0), tspec(0), tspec(0), _full_spec((8, D))],
        out_shape=[jax.ShapeDtypeStruct((t, D), BF16), jax.ShapeDtypeStruct((t, D), odt), jax.ShapeDtypeStruct((t, D), odt),
                   jax.ShapeDtypeStruct((8, D), F32)],
        scratch_shapes=[pltpu.VMEM((NH, HD, HD), F32)] + [pltpu.VMEM((TM, D), F32)] * 5 + [pltpu.VMEM((8, D), F32)],
        compiler_params=_cp(1),
    )(*args)


def loss_bwd(x, gain, target, n_lat, name):
    t = x.shape[0]

    def body(x_ref, gain_ref, tg_ref, dx_ref, acc_ref):
        i = pl.program_id(0)

        @pl.when(i == 0)
        def _():
            acc_ref[...] = jnp.zeros_like(acc_ref)

        latf = (i < n_lat).astype(F32)
        x = x_ref[...]
        gain = gain_ref[...]
        r = lax.rsqrt(jnp.mean(x * x, axis=-1, keepdims=True) + EPS)
        xn = x * r
        err = (xn * gain - tg_ref[...]) * latf
        dy = err * (1.0 / D)
        dxn = dy * gain
        dx_ref[...] = r * (dxn - xn * jnp.mean(dxn * xn, axis=-1, keepdims=True))
        acc_ref[0:1, :] += jnp.sum(dy * xn, axis=0, keepdims=True)
        acc_ref[1:2, :] += jnp.sum(err * err, axis=0, keepdims=True)

    return pl.pallas_call(
        body, name=name, grid=(t // TM,),
        in_specs=[_row_spec(D), _full_spec((1, D)), pl.BlockSpec((TM, D), lambda i: (jnp.minimum(i, n_lat - 1), 0))],
        out_specs=[_row_spec(D), _full_spec((8, D))],
        out_shape=[jax.ShapeDtypeStruct((t, D), F32), jax.ShapeDtypeStruct((8, D), F32)],
        compiler_params=_cp(1),
    )(x, gain, target)


def local_step(xs, target, mods, norm1, norm2, norm_f, lbs, gnorm, cw8, cb, wts, n_lat, on_grads):
    t = xs.shape[0]
    saved = []
    cache = {}

    def W(name, idx, after=None):
        if (name, idx) not in cache:
            cache[(name, idx)] = wts(name, idx, after)
        return cache[(name, idx)]

    x = xs
    for i in range(DEPTH):
        j = i // 2
        rec = i % 2 == 0
        n1 = norm1[i:i + 1]
        n2 = norm2[i:i + 1]
        s = {"x_in": x}
        if rec:
            p = proj_fwd(x, n1, mods[i], 0, W("hin", j, x), n_lat, f"hin_fwd_{i}")
            o0, st0 = gla_fwd(p, lbs[j], 0, f"gla_fwd0_{i}")
            o1, st1 = gla_fwd(p, lbs[j], 1, f"gla_fwd1_{i}")
            ex = [o0, o1, p, gnorm[j:j + 1]]
            ex_specs = [_row_spec(D), _row_spec(D), _col_spec(4), _full_spec((1, D))]
            xm, y, ab = outproj_fwd(readout_prologue, ex, ex_specs, W("hout", j, o1), x, mods[i], 0, n_lat, f"hout_fwd_{i}")
            s.update(st0=st0, st1=st1)
        else:
            sft = 1 if j % 2 == 0 else CH
            p = proj_fwd(x, n1, mods[i], 0, W("cin", j, x), n_lat, f"cin_fwd_{i}")
            ex = _conv_args(sft, p, cw8[j], cb[j])
            ex_specs = _conv_specs(sft, t)
            xm, y, ab = outproj_fwd(make_conv_prologue(n_lat, sft), ex, ex_specs, W("cout", j, p), x, mods[i], 0, n_lat, f"cout_fwd_{i}")
        s.update(p=p, ex=ex, ex_specs=ex_specs, y_mix=y, ab_mix=ab, x_mid=xm)
        x, y2, ab2 = mlp_fwd(xm, n2, mods[i], W("w1", i, xm), W("w2", i, xm), n_lat, f"mlp_fwd_{i}")
        s.update(y_mlp=y2, ab_mlp=ab2)
        saved.append(s)

    dx, acc_loss = loss_bwd(x, norm_f, target, n_lat, "loss")
    small = {"norm_f": acc_loss[0:1], "norm1": [None] * DEPTH, "norm2": [None] * DEPTH, "dmod": [None] * DEPTH,
             "gnorm": [None] * 2, "lb": [None] * 2, "cw": [None] * 2, "cb": [None] * 2}
    bshape = lambda w: jax.ShapeDtypeStruct((t, w), BF16)
    for i in reversed(range(DEPTH)):
        j = i // 2
        rec = i % 2 == 0
        s = saved[i]
        n1 = norm1[i:i + 1]
        n2 = norm2[i:i + 1]
        dx, dyb, dp1, hb, acc_n2 = mlp_bwd(dx, s["y_mlp"], s["ab_mlp"], s["x_mid"], n2, mods[i], W("w1", i), W("w2", i), n_lat, f"mlp_bwd_{i}")
        on_grads(i, "mlp", {"w2": dw_tn(s["ab_mlp"], dyb, 4, True, f"w2_dw_{i}"), "w1": dw_tn(hb, dp1, 4, False, f"w1_dw_{i}")})
        g = {}
        if rec:
            dyb, acc_g1, (do, dgate) = outproj_bwd(
                readout_epilogue, s["ex"], s["ex_specs"], [_row_spec(D), _row_spec(D)],
                [jax.ShapeDtypeStruct((t, D), F32), bshape(D)], W("hout", j), dx, s["y_mix"], mods[i], 0, n_lat, f"hout_bwd_{i}")
            g["hout"] = dw_tn(s["ab_mix"], dyb, 1, False, f"hout_dw_{i}")
            dz0, dv0, dq0, acc_l0 = gla_bwd(s["p"], lbs[j], do, s["st0"], 0, None, f"gla_bwd0_{i}")
            dz1, dv, dq, acc_l1 = gla_bwd(s["p"], lbs[j], do, s["st1"], 1, (dv0, dq0), f"gla_bwd1_{i}")
            dx, hb, dpb, acc_n1 = proj_bwd([dz0, dz1, dv, dq, dgate], W("hin", j), s["x_in"], n1, mods[i], 0, dx, n_lat, f"hin_bwd_{i}")
            g["hin"] = dw_tn(hb, dpb, 4, False, f"hin_dw_{i}")
            small["gnorm"][j] = acc_g1[2:3]
            small["lb"][j] = jnp.concatenate([acc_l0[0:1], acc_l1[0:1]], axis=0)
        else:
            sft = 1 if j % 2 == 0 else CH
            dyb, acc_g1, (dconv, dgb) = outproj_bwd(
                make_conv_epilogue(n_lat, sft), s["ex"], s["ex_specs"], [_row_spec(D), _row_spec(D)],
                [jax.ShapeDtypeStruct((t, D), F32), bshape(D)], W("cout", j), dx, s["y_mix"], mods[i], 0, n_lat, f"cout_bwd_{i}")
            g["cout"] = dw_tn(s["ab_mix"], dyb, 1, False, f"cout_dw_{i}")
            dgc, dxi, acc_c = conv_bwd(dconv, s["p"], cw8[j], sft, n_lat, f"conv_bwd_{i}")
            dx, hb, dpb, acc_n1 = proj_bwd([dgb, dgc, dxi], W("cin", j), s["x_in"], n1, mods[i], 0, dx, n_lat, f"cin_bwd_{i}")
            g["cin"] = dw_tn(hb, dpb, 4, False, f"cin_dw_{i}")
            small["cw"][j] = acc_c[0:3]
            small["cb"][j] = acc_c[3:4]
        small["norm1"][i] = acc_n1[0:1]
        small["norm2"][i] = acc_n2[0:1]
        z2 = jnp.zeros((2, D), F32)
        small["dmod"][i] = jnp.concatenate([acc_n1[1:3], acc_g1[0:1], acc_n2[1:3], acc_n2[5:6], z2,
                                            acc_n1[3:5], acc_g1[1:2], acc_n2[3:5], acc_n2[6:7], z2], axis=0)
        on_grads(i, "mix", g)
    return acc_loss[1:2], dx, small


RB = 256


def cast_to_slot(w2d, layer, k, chip1, name):
    c = w2d.shape[1]
    nblk = k // RB

    def body(chip_ref, w_ref, o_ref):
        o_ref[0] = w_ref[...].astype(BF16)

    return pl.pallas_call(
        body, name=name,
        grid_spec=pltpu.PrefetchScalarGridSpec(
            num_scalar_prefetch=1, grid=(nblk,),
            in_specs=[pl.BlockSpec((RB, c), lambda i, ch: (layer * nblk + i, 0))],
            out_specs=pl.BlockSpec((1, RB, c), lambda i, ch: (ch[0], i, 0))),
        out_shape=jax.ShapeDtypeStruct((4, k, c), BF16), compiler_params=_cp(1))(chip1, w2d)


def sum_slots(own, land, acc, layer, chip1, name):
    _, k, c = own.shape
    nblk = k // RB

    def body(chip_ref, own_ref, l1_ref, l2_ref, l3_ref, acc_ref, o_ref):
        o_ref[...] = ((own_ref[0].astype(F32) + l1_ref[0].astype(F32)) + l2_ref[0].astype(F32)) + l3_ref[0].astype(F32)

    slot = lambda d: pl.BlockSpec((1, RB, c), lambda i, ch: ((ch[0] + d) % 4, i, 0))
    return pl.pallas_call(
        body, name=name,
        grid_spec=pltpu.PrefetchScalarGridSpec(
            num_scalar_prefetch=1, grid=(nblk,),
            in_specs=[slot(0), slot(1), slot(2), slot(3), ANY],
            out_specs=pl.BlockSpec((RB, c), lambda i, ch: (layer * nblk + i, 0))),
        out_shape=jax.ShapeDtypeStruct(acc.shape, F32), input_output_aliases={5: 0}, compiler_params=_cp(1),
    )(chip1, own, land, land, land, acc)


def _adamw_math(w, g, m, v):
    m = ADAM_B1 * m + (1.0 - ADAM_B1) * g
    v = ADAM_B2 * v + (1.0 - ADAM_B2) * jnp.square(g)
    m_hat = m / (1.0 - ADAM_B1 ** ADAM_STEP)
    v_hat = v / (1.0 - ADAM_B2 ** ADAM_STEP)
    delta = -ADAM_LR * (m_hat / (jnp.sqrt(v_hat) + ADAM_EPS) + ADAM_WD * w)
    return delta, m, v


def adamw(gsrcs, w, m, v, name):
    r, c = w.shape
    rb = RB if r % RB == 0 else r
    n_g = len(gsrcs)

    def body(*refs):
        g = refs[0][...]
        for k in range(1, n_g):
            g = g + refs[k][...]
        w_ref, m_ref, v_ref, g_ref, d_ref, mo_ref, vo_ref = refs[n_g:]
        delta, mo, vo = _adamw_math(w_ref[...], g, m_ref[...], v_ref[...])
        g_ref[...] = g
        d_ref[...] = delta
        mo_ref[...] = mo
        vo_ref[...] = vo

    spec = pl.BlockSpec((rb, c), lambda i: (i, 0))
    shp = jax.ShapeDtypeStruct((r, c), F32)
    return pl.pallas_call(body, name=name, grid=(r // rb,), in_specs=[spec] * (n_g + 3), out_specs=[spec] * 4,
                          out_shape=[shp] * 4, compiler_params=_cp(1))(*gsrcs, w, m, v)


ADA_CB = 512


def ada_fwd(cvec, ada_w, bias, name):
    _, _, nc = ada_w.shape

    def body(c_ref, w_ref, b_ref, o_ref):
        cv = c_ref[...]
        a = (cv * _sigmoid(cv)).astype(BF16)
        o_ref[0] = _dot(a, w_ref[0].astype(BF16)) + b_ref[0]

    return pl.pallas_call(
        body, name=name, grid=(DEPTH, nc // ADA_CB),
        in_specs=[pl.BlockSpec((16, D), lambda i, j: (0, 0)), pl.BlockSpec((1, D, ADA_CB), lambda i, j: (i, 0, j)),
                  pl.BlockSpec((1, 1, ADA_CB), lambda i, j: (i, 0, j))],
        out_specs=pl.BlockSpec((1, 16, ADA_CB), lambda i, j: (i, 0, j)),
        out_shape=jax.ShapeDtypeStruct((DEPTH, 16, nc), F32), compiler_params=_cp(2),
    )(cvec, ada_w, bias)


def ada_bwd(cvec, dcols, ada_w, m, v, name):
    _, _, nc = ada_w.shape

    def body(c_ref, d_ref, w_ref, m_ref, v_ref, g_ref, dl_ref, mo_ref, vo_ref, acc_ref):
        @pl.when(jnp.logical_and(pl.program_id(0) == 0, pl.program_id(1) == 0))
        def _():
            acc_ref[...] = jnp.zeros_like(acc_ref)

        cv = c_ref[...]
        a = (cv * _sigmoid(cv)).astype(BF16)
        db = d_ref[0].astype(BF16)
        w = w_ref[0]
        g = _dot_tn(a, db)
        delta, mo, vo = _adamw_math(w, g, m_ref[0], v_ref[0])
        g_ref[0] = g
        dl_ref[0] = delta
        mo_ref[0] = mo
        vo_ref[0] = vo
        acc_ref[...] += _dot_nt(db[8:16, :], w.astype(BF16))

    wspec = pl.BlockSpec((1, D, ADA_CB), lambda i, j: (i, 0, j))
    wshape = jax.ShapeDtypeStruct(ada_w.shape, F32)
    return pl.pallas_call(
        body, name=name, grid=(DEPTH, nc // ADA_CB),
        in_specs=[pl.BlockSpec((16, D), lambda i, j: (0, 0)), pl.BlockSpec((1, 16, ADA_CB), lambda i, j: (i, 0, j)), wspec, wspec, wspec],
        out_specs=[wspec, wspec, wspec, wspec, pl.BlockSpec((8, D), lambda i, j: (0, 0))],
        out_shape=[wshape, wshape, wshape, wshape, jax.ShapeDtypeStruct((8, D), F32)], compiler_params=_cp(2),
    )(cvec, dcols, ada_w, m, v)


def _place():
    return lax.axis_index("x"), lax.axis_index("y"), lax.axis_index("c")


ANY = pl.BlockSpec(memory_space=pl.ANY)
VMEM_SPEC = pl.BlockSpec(memory_space=pltpu.VMEM)


def small_allgather(buf, name):
    r, c = buf.shape

    def body(in_ref, out_ref, send_sems, recv_sems, loc_sem):
        x, y, cc = _place()
        me = 4 * x + 2 * y + cc
        loc = pltpu.make_async_copy(in_ref, out_ref.at[me], loc_sem)
        loc.start()
        peers = []
        for k in range(1, 8):
            px = 1 - x if k & 4 else x
            py = 1 - y if k & 2 else y
            pc = 1 - cc if k & 1 else cc
            peers.append((px, py, pc))
        sends = []
        for k, peer in enumerate(peers):
            cp = pltpu.make_async_remote_copy(src_ref=in_ref, dst_ref=out_ref.at[me], send_sem=send_sems.at[k],
                                              recv_sem=recv_sems.at[k], device_id=peer, device_id_type=MESH)
            cp.start()
            sends.append(cp)
        for k, (px, py, pc) in enumerate(peers):
            pltpu.make_async_remote_copy(src_ref=in_ref, dst_ref=out_ref.at[4 * px + 2 * py + pc], send_sem=send_sems.at[k],
                                         recv_sem=recv_sems.at[k], device_id=(px, py, pc), device_id_type=MESH).wait_recv()
        for cp in sends:
            cp.wait_send()
        loc.wait()

    return pl.pallas_call(
        body, name=name, in_specs=[VMEM_SPEC], out_specs=VMEM_SPEC,
        out_shape=jax.ShapeDtypeStruct((8, r, c), buf.dtype),
        scratch_shapes=[pltpu.SemaphoreType.DMA((7,)), pltpu.SemaphoreType.DMA((7,)), pltpu.SemaphoreType.DMA],
    )(buf)


def _chip_peers(x, y):
    return [(1 - x, y), (x, 1 - y), (1 - x, 1 - y)]


HBM_SPEC = pl.BlockSpec(memory_space=pltpu.HBM)
SEM_SPEC = pl.BlockSpec(memory_space=pltpu.SEMAPHORE)
EFFECT = pltpu.SideEffectType.DATAFLOW_SIDE_EFFECTING


def _hbm(a):
    return pltpu.with_memory_space_constraint(a, pltpu.HBM)


def _split_copy(u, p, peer, dst_slot, chip, land_refs, src_refs, sem_refs, cc):
    px, py = peer
    src = land_refs[u].at[chip] if src_refs is None else src_refs[u].at[2 * px + py]
    return pltpu.make_async_remote_copy(src_ref=src, dst_ref=land_refs[u].at[dst_slot], send_sem=sem_refs[2 * u].at[p],
                                        recv_sem=sem_refs[2 * u + 1].at[p], device_id=(px, py, cc), device_id_type=MESH)


def split_start(lands, srcs, name):
    n = len(lands)
    ops = list(lands) + (list(srcs) if srcs is not None else [])
    n_ops = len(ops)

    def body(*refs):
        land_refs = refs[:n]
        src_refs = refs[n:n_ops] if srcs is not None else None
        sem_refs = refs[n_ops:n_ops + 2 * n]
        x, y, cc = _place()
        chip = 2 * x + y
        for u in range(n):
            for p, peer in enumerate(_chip_peers(x, y)):
                _split_copy(u, p, peer, chip, chip, land_refs, src_refs, sem_refs, cc).start()

    outs = pl.pallas_call(
        body, name=name, in_specs=[HBM_SPEC] * n_ops,
        out_specs=[SEM_SPEC] * (2 * n) + [HBM_SPEC] * n_ops,
        out_shape=[pltpu.SemaphoreType.DMA((3,))] * (2 * n) + [pltpu.HBM(a.shape, a.dtype) for a in ops],
        input_output_aliases={k: 2 * n + k for k in range(n_ops)},
        compiler_params=pltpu.CompilerParams(has_side_effects=EFFECT),
    )(*[_hbm(a) for a in ops])
    sems = list(outs[:2 * n])
    thru = list(outs[2 * n:])
    return sems, thru[:n], thru[n:]


def split_wait(lands, srcs, sems, after, name):
    n = len(lands)
    ops = list(lands) + (list(srcs) if srcs is not None else [])
    n_ops = len(ops)

    def body(*refs):
        land_refs = refs[:n]
        src_refs = refs[n:n_ops] if srcs is not None else None
        sem_refs = refs[n_ops:n_ops + 2 * n]
        x, y, cc = _place()
        chip = 2 * x + y
        for u in range(n):
            for p, peer in enumerate(_chip_peers(x, y)):
                cp = _split_copy(u, p, peer, 2 * peer[0] + peer[1], chip, land_refs, src_refs, sem_refs, cc)
                cp.wait_send()
                cp.wait_recv()

    outs = pl.pallas_call(
        body, name=name, in_specs=[HBM_SPEC] * n_ops + [SEM_SPEC] * (2 * n) + [ANY],
        out_specs=[HBM_SPEC] * n_ops, out_shape=[pltpu.HBM(a.shape, a.dtype) for a in ops],
        input_output_aliases={k: k for k in range(n_ops)},
        compiler_params=pltpu.CompilerParams(has_side_effects=EFFECT),
    )(*ops, *sems, after)
    return list(outs[:n]), list(outs[n:])


def sibling_exchange(parts, name):
    n = len(parts)

    def body(*refs):
        in_refs = refs[:n]
        out_refs = refs[n:2 * n]
        send_sems, recv_sems = refs[2 * n:]
        x, y, cc = _place()
        sib = (x, y, 1 - cc)
        cps = []
        for k in range(n):
            cp = pltpu.make_async_remote_copy(src_ref=in_refs[k], dst_ref=out_refs[k], send_sem=send_sems.at[k],
                                              recv_sem=recv_sems.at[k], device_id=sib, device_id_type=MESH)
            cp.start()
            cps.append(cp)
        for cp in cps:
            cp.wait_recv()
        for cp in cps:
            cp.wait_send()

    return pl.pallas_call(
        body, name=name, in_specs=[ANY] * n, out_specs=[ANY] * n,
        out_shape=[jax.ShapeDtypeStruct(p.shape, p.dtype) for p in parts],
        scratch_shapes=[pltpu.SemaphoreType.DMA((n,)), pltpu.SemaphoreType.DMA((n,))],
    )(*parts)


SMALL_ROWS = 88
FIN_ROWS = 72


def small_finish(g3, g4, c_ctx, lbp, name):
    def body(g3_ref, g4_ref, cc_ref, lbp_ref, o_ref, s_ref):
        s = g3_ref[0]
        for k in range(1, 8):
            s = s + g3_ref[k]
        s_ref[...] = s
        for i in range(DEPTH):
            o_ref[8 * i:8 * i + 8, :] = s_ref[16 * i:16 * i + 8, :] + s_ref[16 * i + 8:16 * i + 16, :]
        acc = g4_ref[0]
        for k in (2, 4, 6):
            acc = acc + g4_ref[k]
        cc = cc_ref[...]
        sg = _sigmoid(cc)
        row = jnp.sum(acc, axis=0, keepdims=True) * (sg * (1.0 + cc * (1.0 - sg)))
        o_ref[32:40, :] = jnp.broadcast_to(row, (8, D))
        o_ref[40:64, :] = s_ref[64:88, :]
        o_ref[64:72, :] = jnp.zeros((8, D), F32)
        for d in range(2):
            pp = lbp_ref[2 * d:2 * d + 1, :] * lbp_ref[2 * d + 1:2 * d + 2, :] * s_ref[75 + d:76 + d, :]
            o_ref[64 + 2 * d:65 + 2 * d, :] = -pp
            o_ref[65 + 2 * d:66 + 2 * d, :] = pp

    return pl.pallas_call(
        body, name=name, in_specs=[VMEM_SPEC] * 4, out_specs=VMEM_SPEC,
        out_shape=jax.ShapeDtypeStruct((FIN_ROWS, D), F32),
        scratch_shapes=[pltpu.VMEM((SMALL_ROWS, D), F32)],
    )(g3, g4, c_ctx, lbp)


def _pack_rows(arrs):
    flat = jnp.concatenate([a.reshape(-1) for a in arrs])
    n = -(-flat.shape[0] // (8 * D)) * 8 * D
    return jnp.pad(flat, (0, n - flat.shape[0])).reshape(n // D, D)


def _unpack_rows(packed, shapes):
    flat = packed.reshape(-1)
    outs, off = [], 0
    for s in shapes:
        size = 1
        for k in s:
            size *= k
        outs.append(flat[off:off + size].reshape(s))
        off += size
    return outs


def _pad8(a):
    return jnp.pad(a, ((0, 8 - a.shape[0]), (0, 0)))


def kernel(x, c, ctx, c_ctx, ada_w, ada_b, norm1, norm2, norm_f, mlp_w1, mlp_w2, hgrn_w_in, hgrn_lb, hgrn_gnorm, hgrn_w_out, conv_w_in, conv_w, conv_b, conv_w_out, loss_target, m_c_ctx, m_ada_w, m_ada_b, m_norm1, m_norm2, m_norm_f, m_mlp_w1, m_mlp_w2, m_hgrn_w_in, m_hgrn_lb, m_hgrn_gnorm, m_hgrn_w_out, m_conv_w_in, m_conv_w, m_conv_b, m_conv_w_out, v_c_ctx, v_ada_w, v_ada_b, v_norm1, v_norm2, v_norm_f, v_mlp_w1, v_mlp_w2, v_hgrn_w_in, v_hgrn_lb, v_hgrn_gnorm, v_hgrn_w_out, v_conv_w_in, v_conv_w, v_conv_b, v_conv_w_out):
    xi, yi, ci = _place()
    me = 4 * xi + 2 * yi + ci
    chip = 2 * xi + yi
    seq = x.shape[1]
    assert ctx.shape[1] == TM and seq % TM == 0 and (seq + TM) % TMW == 0
    n_lat = seq // TM
    sd = D // 4
    nca = ada_w.shape[2]
    xs = jnp.concatenate([x[0], ctx[0]], axis=0)

    sh_rows = jnp.concatenate([hgrn_lb.reshape(4, sd), conv_w.reshape(6, sd), conv_b.reshape(2, sd)], axis=0)
    buf1 = jnp.concatenate([c, jnp.pad(sh_rows, ((0, 0), (0, D - sd))), jnp.zeros((3, D), F32)], axis=0)
    g1 = small_allgather(buf1, "gather_small_in")
    cvec = jnp.concatenate([g1[:, 0, :], jnp.broadcast_to(c_ctx[None], (8, D))], axis=0)
    shf = g1[0::2, 1:13, :sd].transpose(1, 0, 2).reshape(12, D)
    lb_p = jax.nn.softmax(shf[0:4].reshape(2, 2, D), axis=1)
    lower = jnp.cumsum(lb_p, axis=1) - lb_p[:, :1]
    lbs = [lower[:, 0], lower[:, 1]]
    cw8 = [_pad8(shf[4:7]), _pad8(shf[7:10])]
    cb = [shf[10:11], shf[11:12]]

    bias = lax.dynamic_slice_in_dim(ada_b, chip * nca, nca, axis=1).reshape(DEPTH, 1, nca)
    ada_part = ada_fwd(cvec, ada_w, bias, "ada_fwd")
    g2 = small_allgather(ada_part.reshape(DEPTH * 16, nca), "gather_ada")
    ada_full = g2[0::2].reshape(4, DEPTH, 16, nca).transpose(1, 2, 0, 3).reshape(DEPTH, 16, 4 * nca)
    lat = lax.dynamic_slice_in_dim(ada_full, me, 1, axis=1)[:, 0]
    mods = [jnp.stack([_pad8(lat[i].reshape(6, D)), _pad8(ada_full[i, 8].reshape(6, D))]) for i in range(DEPTH)]

    big = [(mlp_w1, m_mlp_w1, v_mlp_w1), (mlp_w2, m_mlp_w2, v_mlp_w2), (hgrn_w_in, m_hgrn_w_in, v_hgrn_w_in),
           (hgrn_w_out, m_hgrn_w_out, v_hgrn_w_out), (conv_w_in, m_conv_w_in, v_conv_w_in), (conv_w_out, m_conv_w_out, v_conv_w_out)]
    big_names = ["w1", "w2", "hin", "hout", "cin", "cout"]
    flat2 = lambda a: a.reshape(a.shape[0] * a.shape[1], a.shape[2])
    tensors = dict(zip(big_names, big))
    chip1 = jnp.reshape(chip, (1,)).astype(jnp.int32)
    order = []
    for i in range(DEPTH):
        order += [("hin", i // 2), ("hout", i // 2)] if i % 2 == 0 else [("cin", i // 2), ("cout", i // 2)]
        order += [("w1", i), ("w2", i)]
    lands = [cast_to_slot(flat2(tensors[n][0]), idx, tensors[n][0].shape[1], chip1, f"cast_{n}_{idx}") for n, idx in order]
    w_sems, lands, _ = split_start(lands, None, "gather_start")
    unit = {key: u for u, key in enumerate(order)}

    def wts(n, idx, after):
        u = unit[(n, idx)]
        (w,), _ = split_wait([lands[u]], None, w_sems[2 * u:2 * u + 2], after, f"gather_wait_{n}_{idx}")
        return w.reshape(w.shape[0] * w.shape[1], w.shape[2]) if n in ("w2", "hout", "cout") else w

    started = []

    def on_grads(i, tag, g):
        names = sorted(g)
        gs = [g[n].reshape(4, g[n].shape[0] * g[n].shape[1] // 4, g[n].shape[2]) for n in names]
        sems, zones, srcs = split_start([lax.empty(a.shape, BF16) for a in gs], gs, f"grad_start_{tag}_{i}")
        started.append(([(n, i if n in ("w1", "w2") else i // 2) for n in names], sems, zones, srcs))

    lane, dx, small = local_step(xs, loss_target[0], mods, norm1, norm2, norm_f[None], lbs, hgrn_gnorm, cw8, cb, wts, n_lat, on_grads)
    loss = lax.psum(0.5 * jnp.sum(lane) / D, ("x", "y", "c"))
    grad_x = dx[:seq][None]

    z3 = jnp.zeros((3, D), F32)
    rows3 = jnp.concatenate(small["dmod"] + small["norm1"] + small["norm2"] + [small["norm_f"]] + small["gnorm"]
                            + [small["lb"][1]] + small["cw"] + small["cb"] + [z3], axis=0)
    g3 = small_allgather(rows3, "gather_small_out")
    dmat = g3[:, :64].reshape(8, DEPTH, 2, 8, D)[:, :, :, :6].transpose(1, 2, 0, 3, 4).reshape(DEPTH, 16, 6 * D)
    dcols = lax.dynamic_slice_in_dim(dmat, chip * nca, nca, axis=2)
    g_ada_w, d_ada_w, nm_ada_w, nv_ada_w, acc4 = ada_bwd(cvec, dcols, ada_w, m_ada_w, v_ada_w, "ada_bwd")
    g4 = small_allgather(acc4, "gather_cctx")
    fin = small_finish(g3, g4, c_ctx[None], _pad8(lb_p.reshape(4, D)), "small_finish")
    cols = lambda a: lax.dynamic_slice_in_dim(a, chip * sd, sd, axis=a.ndim - 1)
    small_g = [fin[32], fin[0:32].reshape(DEPTH, 8, D)[:, :6].reshape(DEPTH, 6 * D), fin[40:44], fin[44:48], fin[48], fin[49:51],
               cols(fin[64:68].reshape(2, 2, D)), cols(fin[53:59].reshape(2, 3, D)), cols(fin[59:61])]
    small_w = [c_ctx, ada_b, norm1, norm2, norm_f, hgrn_gnorm, hgrn_lb, conv_w, conv_b]
    small_m = [m_c_ctx, m_ada_b, m_norm1, m_norm2, m_norm_f, m_hgrn_gnorm, m_hgrn_lb, m_conv_w, m_conv_b]
    small_v = [v_c_ctx, v_ada_b, v_norm1, v_norm2, v_norm_f, v_hgrn_gnorm, v_hgrn_lb, v_conv_w, v_conv_b]
    shapes = [w.shape for w in small_w]
    packed = adamw([_pack_rows(small_g)], _pack_rows(small_w), _pack_rows(small_m), _pack_rows(small_v), "adamw_small")
    s_g, s_d, s_m, s_v = [_unpack_rows(p, shapes) for p in packed]

    keys, g_sems, g_zones, g_srcs = [], [], [], []
    for ks, sems, zones, srcs in started:
        keys += ks
        g_sems += sems
        g_zones += zones
        g_srcs += srcs
    g_zones, g_srcs = split_wait(g_zones, g_srcs, g_sems, fin, "grad_wait")
    acc = {n: lax.empty(flat2(w).shape, F32) for n, (w, _, _) in tensors.items()}
    for (n, idx), zone, own in zip(keys, g_zones, g_srcs):
        acc[n] = sum_slots(own, zone, acc[n], idx, chip1, f"sum_{n}_{idx}")
    partial = [acc[n] for n in big_names]
    other = sibling_exchange(partial, "sibling_exchange")
    b_g, b_d, b_m, b_v = [], [], [], []
    for (w, m, v), pm, po, n in zip(big, partial, other, big_names):
        outs = adamw([pm, po], flat2(w), flat2(m), flat2(v), f"adamw_{n}")
        for lst, a in zip((b_g, b_d, b_m, b_v), outs):
            lst.append(a.reshape(w.shape))

    def ordered(s, a, b):
        return [s[0], a, s[1], s[2], s[3], s[4], b[0], b[1], b[2], s[6], s[5], b[3], b[4], s[7], s[8], b[5]]

    return (loss, grad_x, *ordered(s_g, g_ada_w, b_g), *ordered(s_d, d_ada_w, b_d), *ordered(s_m, nm_ada_w, b_m),
            *ordered(s_v, nv_ada_w, b_v))
```

```python
import functools

import jax
import jax.numpy as jnp
from jax import lax
from jax.experimental import pallas as pl
from jax.experimental.pallas import tpu as pltpu

F32 = jnp.float32
BF16 = jnp.bfloat16
MESH = pl.DeviceIdType.MESH

D = 1024
HD = 128
NH = D // HD
CH = 64
TM = 256
TMW = 768
EPS = 1e-6
DEPTH = 4
VMEM_LIMIT = 56 * 1024 * 1024

ADAM_LR = 0.001
ADAM_B1 = 0.9
ADAM_B2 = 0.999
ADAM_EPS = 1e-08
ADAM_WD = 0.01
ADAM_STEP = 10


def _cp(n_grid):
    return pltpu.CompilerParams(dimension_semantics=("arbitrary",) * n_grid, vmem_limit_bytes=VMEM_LIMIT)


def _dot(a, b):
    return jnp.dot(a, b, preferred_element_type=F32)


def _dot_nt(a, b):
    return lax.dot_general(a, b, (((1,), (1,)), ((), ())), preferred_element_type=F32)


def _dot_tn(a, b):
    return lax.dot_general(a, b, (((0,), (0,)), ((), ())), preferred_element_type=F32)


def _sigmoid(z):
    return 1.0 / (1.0 + jnp.exp(-z))


def _norm_mod(x, gain, sh, sc):
    r = lax.rsqrt(jnp.mean(x * x, axis=-1, keepdims=True) + EPS)
    xn = x * r
    yn = xn * gain
    return r, xn, yn, yn * (1.0 + sc) + sh


def _row_spec(width):
    return pl.BlockSpec((TM, width), lambda i: (i, 0))


def _col_spec(col):
    return pl.BlockSpec((TM, D), lambda i: (i, col))


def _full_spec(shape):
    nd = len(shape)
    return pl.BlockSpec(shape, lambda i: (0,) * nd)


def _mod_spec(n_lat):
    return pl.BlockSpec((1, 8, D), lambda i: (i // n_lat, 0, 0))


def proj_fwd(x, gain, mod, m0, w4, n_lat, name):
    t = x.shape[0]
    nb, _, ns = w4.shape

    def body(x_ref, gain_ref, mod_ref, w_ref, p_ref):
        _, _, _, h = _norm_mod(x_ref[...], gain_ref[...], mod_ref[0, m0:m0 + 1, :], mod_ref[0, m0 + 1:m0 + 2, :])
        hb = h.astype(BF16)
        for c in range(nb):
            p_ref[:, c * ns:(c + 1) * ns] = _dot(hb, w_ref[c])

    return pl.pallas_call(
        body, name=name, grid=(t // TM,),
        in_specs=[_row_spec(D), _full_spec((1, D)), _mod_spec(n_lat), _full_spec(w4.shape)],
        out_specs=_row_spec(nb * ns),
        out_shape=jax.ShapeDtypeStruct((t, nb * ns), F32),
        compiler_params=_cp(1),
    )(x, gain, mod, w4)


def proj_bwd(parts, w4, x, gain, mod, m0, dx_in, n_lat, name):
    t = x.shape[0]
    nb, _, ns = w4.shape
    n = nb * ns
    n_parts = len(parts)
    widths = [p.shape[1] for p in parts]
    offs = [sum(widths[:k]) for k in range(n_parts)]
    assert sum(widths) == n
    single = n_parts == 1

    def body(*refs):
        part_refs = refs[:n_parts]
        w_ref, x_ref, gain_ref, mod_ref, dxin_ref = refs[n_parts:n_parts + 5]
        rest = refs[n_parts + 5:]
        if single:
            dx_ref, hb_ref, acc_ref = rest
            src = part_refs[0]
        else:
            dx_ref, hb_ref, acc_ref, dpb_ref = rest
            for p_ref, off, w in zip(part_refs, offs, widths):
                dpb_ref[:, off:off + w] = p_ref[...]
            src = dpb_ref
        i = pl.program_id(0)

        @pl.when(i == 0)
        def _():
            acc_ref[...] = jnp.zeros_like(acc_ref)

        gain = gain_ref[...]
        sc = mod_ref[0, m0 + 1:m0 + 2, :]
        r, xn, yn, h = _norm_mod(x_ref[...], gain, mod_ref[0, m0:m0 + 1, :], sc)
        hb_ref[...] = h.astype(BF16)
        dh = _dot_nt(src[:, 0:ns], w_ref[0])
        for c in range(1, nb):
            dh = dh + _dot_nt(src[:, c * ns:(c + 1) * ns], w_ref[c])
        dsh = jnp.sum(dh, axis=0, keepdims=True)
        dsc = jnp.sum(dh * yn, axis=0, keepdims=True)
        dyn = dh * (1.0 + sc)
        dgain = jnp.sum(dyn * xn, axis=0, keepdims=True)
        dxn = dyn * gain
        dx = r * (dxn - xn * jnp.mean(dxn * xn, axis=-1, keepdims=True))
        dx_ref[...] = dxin_ref[...] + dx
        latf = (i < n_lat).astype(F32)
        ctxf = 1.0 - latf
        acc_ref[0:1, :] += dgain
        acc_ref[1:2, :] += dsh * latf
        acc_ref[2:3, :] += dsc * latf
        acc_ref[3:4, :] += dsh * ctxf
        acc_ref[4:5, :] += dsc * ctxf

    out_specs = [_row_spec(D), _row_spec(D), _full_spec((8, D))]
    out_shape = [jax.ShapeDtypeStruct((t, D), F32), jax.ShapeDtypeStruct((t, D), BF16), jax.ShapeDtypeStruct((8, D), F32)]
    if not single:
        out_specs.append(_row_spec(n))
        out_shape.append(jax.ShapeDtypeStruct((t, n), BF16))
    outs = pl.pallas_call(
        body, name=name, grid=(t // TM,),
        in_specs=[_row_spec(w) for w in widths]
        + [_full_spec(w4.shape), _row_spec(D), _full_spec((1, D)), _mod_spec(n_lat), _row_spec(D)],
        out_specs=out_specs, out_shape=out_shape, compiler_params=_cp(1),
    )(*parts, w4, x, gain, mod, dx_in)
    if single:
        return outs[0], outs[1], parts[0], outs[2]
    return outs[0], outs[1], outs[3], outs[2]


def dw_tn(a, b, nb, a_blocked, name):
    t = a.shape[0]
    ka = a.shape[1] // nb if a_blocked else a.shape[1]
    kb = b.shape[1] if a_blocked else b.shape[1] // nb
    n_k = t // TMW

    def body(a_ref, b_ref, o_ref, acc):
        k = pl.program_id(1)

        @pl.when(k == 0)
        def _():
            acc[...] = jnp.zeros_like(acc)

        acc[...] += _dot_tn(a_ref[...], b_ref[...])

        @pl.when(k == n_k - 1)
        def _():
            o_ref[0] = acc[...].astype(BF16)

    a_spec = pl.BlockSpec((TMW, ka), (lambda j, k: (k, j)) if a_blocked else (lambda j, k: (k, 0)))
    b_spec = pl.BlockSpec((TMW, kb), (lambda j, k: (k, 0)) if a_blocked else (lambda j, k: (k, j)))
    return pl.pallas_call(
        body, name=name, grid=(nb, n_k),
        in_specs=[a_spec, b_spec],
        out_specs=pl.BlockSpec((1, ka, kb), lambda j, k: (j, 0, 0)),
        out_shape=jax.ShapeDtypeStruct((nb, ka, kb), BF16),
        scratch_shapes=[pltpu.VMEM((ka, kb), F32)],
        compiler_params=_cp(2),
    )(a, b)


def outproj_fwd(prologue, extras, extra_specs, w, x, mod, m0, n_lat, name):
    t = x.shape[0]
    k = w.shape[0]
    n_extra = len(extras)

    def body(*refs):
        ex = refs[:n_extra]
        w_ref, x_ref, mod_ref, xo_ref, y_ref, ab_ref = refs[n_extra:]
        ab = prologue(pl.program_id(0), *ex).astype(BF16)
        ab_ref[...] = ab
        y = _dot(ab, w_ref[...])
        y_ref[...] = y
        xo_ref[...] = x_ref[...] + mod_ref[0, m0 + 2:m0 + 3, :] * y

    return pl.pallas_call(
        body, name=name, grid=(t // TM,),
        in_specs=list(extra_specs) + [_full_spec(w.shape), _row_spec(D), _mod_spec(n_lat)],
        out_specs=[_row_spec(D), _row_spec(D), _row_spec(k)],
        out_shape=[jax.ShapeDtypeStruct((t, D), F32), jax.ShapeDtypeStruct((t, D), F32), jax.ShapeDtypeStruct((t, k), BF16)],
        compiler_params=_cp(1),
    )(*extras, w, x, mod)


def outproj_bwd(epilogue, extras, extra_specs, ep_out_specs, ep_out_shapes, w, dxn, y, mod, m0, n_lat, name):
    t = dxn.shape[0]
    n_extra = len(extras)

    def body(*refs):
        ex = refs[:n_extra]
        w_ref, dxn_ref, y_ref, mod_ref, dyb_ref, acc_ref = refs[n_extra:n_extra + 6]
        ep_outs = refs[n_extra + 6:]
        i = pl.program_id(0)

        @pl.when(i == 0)
        def _():
            acc_ref[...] = jnp.zeros_like(acc_ref)

        dxv = dxn_ref[...]
        dyb = (dxv * mod_ref[0, m0 + 2:m0 + 3, :]).astype(BF16)
        dyb_ref[...] = dyb
        dg = jnp.sum(dxv * y_ref[...], axis=0, keepdims=True)
        latf = (i < n_lat).astype(F32)
        acc_ref[0:1, :] += dg * latf
        acc_ref[1:2, :] += dg * (1.0 - latf)
        epilogue(i, _dot_nt(dyb, w_ref[...]), ex, ep_outs, acc_ref)

    outs = pl.pallas_call(
        body, name=name, grid=(t // TM,),
        in_specs=list(extra_specs) + [_full_spec(w.shape), _row_spec(D), _row_spec(D), _mod_spec(n_lat)],
        out_specs=[_row_spec(D), _full_spec((8, D))] + list(ep_out_specs),
        out_shape=[jax.ShapeDtypeStruct((t, D), BF16), jax.ShapeDtypeStruct((8, D), F32)] + list(ep_out_shapes),
        compiler_params=_cp(1),
    )(*extras, w, dxn, y, mod)
    return outs[0], outs[1], outs[2:]


def mlp_fwd(x, gain, mod, w1, w2, n_lat, name):
    t = x.shape[0]
    nb, _, ns = w1.shape

    def body(x_ref, gain_ref, mod_ref, w1_ref, w2_ref, xo_ref, y_ref, ab_ref):
        x = x_ref[...]
        _, _, _, h = _norm_mod(x, gain_ref[...], mod_ref[0, 3:4, :], mod_ref[0, 4:5, :])
        hb = h.astype(BF16)
        y = None
        for c in range(nb):
            a = jnp.square(jnp.maximum(_dot(hb, w1_ref[c]), 0.0)).astype(BF16)
            ab_ref[:, c * ns:(c + 1) * ns] = a
            yc = _dot(a, w2_ref[c * ns:(c + 1) * ns, :])
            y = yc if y is None else y + yc
        y_ref[...] = y
        xo_ref[...] = x + mod_ref[0, 5:6, :] * y

    return pl.pallas_call(
        body, name=name, grid=(t // TM,),
        in_specs=[_row_spec(D), _full_spec((1, D)), _mod_spec(n_lat), _full_spec(w1.shape), _full_spec(w2.shape)],
        out_specs=[_row_spec(D), _row_spec(D), _row_spec(nb * ns)],
        out_shape=[jax.ShapeDtypeStruct((t, D), F32), jax.ShapeDtypeStruct((t, D), F32), jax.ShapeDtypeStruct((t, nb * ns), BF16)],
        compiler_params=_cp(1),
    )(x, gain, mod, w1, w2)


def mlp_bwd(dxn, y, ab, x, gain, mod, w1, w2, n_lat, name):
    t = x.shape[0]
    nb, _, ns = w1.shape

    def body(dxn_ref, y_ref, ab_ref, x_ref, gain_ref, mod_ref, w1_ref, w2_ref, dx_ref, dyb_ref, dp_ref, hb_ref, acc_ref):
        i = pl.program_id(0)

        @pl.when(i == 0)
        def _():
            acc_ref[...] = jnp.zeros_like(acc_ref)

        dxv = dxn_ref[...]
        dyb = (dxv * mod_ref[0, 5:6, :]).astype(BF16)
        dyb_ref[...] = dyb
        dg = jnp.sum(dxv * y_ref[...], axis=0, keepdims=True)
        gain = gain_ref[...]
        sc = mod_ref[0, 4:5, :]
        r, xn, yn, h = _norm_mod(x_ref[...], gain, mod_ref[0, 3:4, :], sc)
        hb_ref[...] = h.astype(BF16)
        dh = None
        for c in range(nb):
            cols = slice(c * ns, (c + 1) * ns)
            da = _dot_nt(dyb, w2_ref[cols, :])
            dp = (da * (2.0 * jnp.sqrt(ab_ref[:, cols].astype(F32)))).astype(BF16)
            dp_ref[:, cols] = dp
            d = _dot_nt(dp, w1_ref[c])
            dh = d if dh is None else dh + d
        dsh = jnp.sum(dh, axis=0, keepdims=True)
        dsc = jnp.sum(dh * yn, axis=0, keepdims=True)
        dyn = dh * (1.0 + sc)
        dgain = jnp.sum(dyn * xn, axis=0, keepdims=True)
        dxn_ = dyn * gain
        dx_ref[...] = dxv + r * (dxn_ - xn * jnp.mean(dxn_ * xn, axis=-1, keepdims=True))
        latf = (i < n_lat).astype(F32)
        ctxf = 1.0 - latf
        acc_ref[0:1, :] += dgain
        acc_ref[1:2, :] += dsh * latf
        acc_ref[2:3, :] += dsc * latf
        acc_ref[3:4, :] += dsh * ctxf
        acc_ref[4:5, :] += dsc * ctxf
        acc_ref[5:6, :] += dg * latf
        acc_ref[6:7, :] += dg * ctxf

    return pl.pallas_call(
        body, name=name, grid=(t // TM,),
        in_specs=[_row_spec(D), _row_spec(D), _row_spec(nb * ns), _row_spec(D), _full_spec((1, D)), _mod_spec(n_lat),
                  _full_spec(w1.shape), _full_spec(w2.shape)],
        out_specs=[_row_spec(D), _row_spec(D), _row_spec(nb * ns), _row_spec(D), _full_spec((8, D))],
        out_shape=[jax.ShapeDtypeStruct((t, D), F32), jax.ShapeDtypeStruct((t, D), BF16), jax.ShapeDtypeStruct((t, nb * ns), BF16),
                   jax.ShapeDtypeStruct((t, D), BF16), jax.ShapeDtypeStruct((8, D), F32)],
        compiler_params=_cp(1),
    )(dxn, y, ab, x, gain, mod, w1, w2)


def readout_prologue(i, o0_ref, o1_ref, gate_ref, gn_ref):
    o = o0_ref[...] + o1_ref[...]
    gate = gate_ref[...]
    w = gn_ref[...] * (gate * _sigmoid(gate))
    pieces = []
    for h in range(NH):
        sl = slice(h * HD, (h + 1) * HD)
        oh = o[:, sl]
        pieces.append(oh * lax.rsqrt(jnp.mean(oh * oh, axis=-1, keepdims=True) + EPS) * w[:, sl])
    return jnp.concatenate(pieces, axis=1)


def readout_epilogue(i, da, ex, outs, acc_ref):
    o0_ref, o1_ref, gate_ref, gn_ref = ex
    do_ref, dgate_ref = outs
    o = o0_ref[...] + o1_ref[...]
    gate = gate_ref[...]
    gn = gn_ref[...]
    sg = _sigmoid(gate)
    silu = gate * sg
    dsilu = sg * (1.0 + gate * (1.0 - sg))
    for h in range(NH):
        sl = slice(h * HD, (h + 1) * HD)
        oh = o[:, sl]
        r = lax.rsqrt(jnp.mean(oh * oh, axis=-1, keepdims=True) + EPS)
        nh = oh * r
        dah = da[:, sl]
        acc_ref[2:3, sl] += jnp.sum(dah * nh * silu[:, sl], axis=0, keepdims=True)
        dgate_ref[:, sl] = (dah * nh * gn[:, sl] * dsilu[:, sl]).astype(BF16)
        dn = dah * gn[:, sl] * silu[:, sl]
        do_ref[:, sl] = r * (dn - nh * jnp.mean(dn * nh, axis=-1, keepdims=True))


def _seg_masks(i, n_lat):
    rows = lax.broadcasted_iota(jnp.int32, (TM, 1), 0)
    latf = (i < n_lat).astype(F32)
    ctxf = 1.0 - latf
    prev_ok = (rows % CH != 0).astype(F32) * latf + (rows != 0).astype(F32) * ctxf
    next_ok = (rows % CH != CH - 1).astype(F32) * latf + (rows != TM - 1).astype(F32) * ctxf
    return prev_ok, next_ok


def _shifts(i, n_lat, sft, cur, halo_prev, halo_next):
    if sft == 1:
        prev_ok, next_ok = _seg_masks(i, n_lat)
        return pltpu.roll(cur, 1, 0) * prev_ok, pltpu.roll(cur, TM - 1, 0) * next_ok
    has_prev = jnp.logical_and(i > 0, i < n_lat).astype(F32)
    has_next = (i < n_lat - 1).astype(F32)
    prev = jnp.concatenate([halo_prev * has_prev, cur[:TM - CH]], axis=0)
    nxt = jnp.concatenate([cur[CH:], halo_next * has_next], axis=0)
    return prev, nxt


def _conv_u(sft, ex):
    if sft == 1:
        gb_ref, gc_ref, xi_ref, cw_ref, cb_ref = ex
        return gb_ref, gc_ref[...] * xi_ref[...], None, None, cw_ref, cb_ref
    gb_ref, gc_ref, xi_ref, gcp_ref, xip_ref, gcn_ref, xin_ref, cw_ref, cb_ref = ex
    return gb_ref, gc_ref[...] * xi_ref[...], gcp_ref[...] * xip_ref[...], gcn_ref[...] * xin_ref[...], cw_ref, cb_ref


def _conv_value(i, n_lat, sft, ex):
    gb_ref, u, up, un, cw_ref, cb_ref = _conv_u(sft, ex)
    u_prev, u_next = _shifts(i, n_lat, sft, u, up, un)
    return gb_ref, cb_ref[...] + cw_ref[0:1, :] * u_prev + cw_ref[1:2, :] * u + cw_ref[2:3, :] * u_next


def make_conv_prologue(n_lat, sft):
    def prologue(i, *ex):
        gb_ref, conv = _conv_value(i, n_lat, sft, ex)
        return gb_ref[...] * conv
    return prologue


def make_conv_epilogue(n_lat, sft):
    def epilogue(i, da, ex, outs, acc_ref):
        gb_ref, conv = _conv_value(i, n_lat, sft, ex)
        outs[0][...] = da * gb_ref[...]
        outs[1][...] = (da * conv).astype(BF16)
    return epilogue


def _conv_specs(sft, t):
    specs = [_col_spec(0), _col_spec(1), _col_spec(2)]
    if sft != 1:
        per = TM // CH
        last = t // CH - 1
        for fn in (lambda i: jnp.maximum(i * per - 1, 0), lambda i: jnp.minimum(i * per + per, last)):
            for col in (1, 2):
                specs.append(pl.BlockSpec((CH, D), functools.partial(lambda i, f, c: (f(i), c), f=fn, c=col)))
    return specs + [_full_spec((8, D)), _full_spec((1, D))]


def _conv_args(sft, p, cw8, cb):
    return [p] * (3 if sft == 1 else 7) + [cw8, cb]


def conv_bwd(dconv, p, cw8, sft, n_lat, name):
    t = dconv.shape[0]
    halo = sft != 1

    def body(*refs):
        if halo:
            dc_ref, dcp_ref, dcn_ref, gc_ref, xi_ref, gcp_ref, xip_ref, gcn_ref, xin_ref, cw_ref, dgc_ref, dxi_ref, acc_ref = refs
            up, un = gcp_ref[...] * xip_ref[...], gcn_ref[...] * xin_ref[...]
            dcp, dcn = dcp_ref[...], dcn_ref[...]
        else:
            dc_ref, gc_ref, xi_ref, cw_ref, dgc_ref, dxi_ref, acc_ref = refs
            up = un = dcp = dcn = None
        i = pl.program_id(0)

        @pl.when(i == 0)
        def _():
            acc_ref[...] = jnp.zeros_like(acc_ref)

        gc = gc_ref[...]
        xi = xi_ref[...]
        u = gc * xi
        dc = dc_ref[...]
        u_prev, u_next = _shifts(i, n_lat, sft, u, up, un)
        dc_prev, dc_next = _shifts(i, n_lat, sft, dc, dcp, dcn)
        acc_ref[0:1, :] += jnp.sum(dc * u_prev, axis=0, keepdims=True)
        acc_ref[1:2, :] += jnp.sum(dc * u, axis=0, keepdims=True)
        acc_ref[2:3, :] += jnp.sum(dc * u_next, axis=0, keepdims=True)
        acc_ref[3:4, :] += jnp.sum(dc, axis=0, keepdims=True)
        du = cw_ref[0:1, :] * dc_next + cw_ref[1:2, :] * dc + cw_ref[2:3, :] * dc_prev
        dgc_ref[...] = (du * xi).astype(BF16)
        dxi_ref[...] = (du * gc).astype(BF16)

    per = TM // CH
    last = t // CH - 1
    prev_i = lambda i: jnp.maximum(i * per - 1, 0)
    next_i = lambda i: jnp.minimum(i * per + per, last)
    if halo:
        in_specs = [_row_spec(D), pl.BlockSpec((CH, D), lambda i: (prev_i(i), 0)), pl.BlockSpec((CH, D), lambda i: (next_i(i), 0)),
                    _col_spec(1), _col_spec(2),
                    pl.BlockSpec((CH, D), lambda i: (prev_i(i), 1)), pl.BlockSpec((CH, D), lambda i: (prev_i(i), 2)),
                    pl.BlockSpec((CH, D), lambda i: (next_i(i), 1)), pl.BlockSpec((CH, D), lambda i: (next_i(i), 2)),
                    _full_spec((8, D))]
        args = [dconv, dconv, dconv, p, p, p, p, p, p, cw8]
    else:
        in_specs = [_row_spec(D), _col_spec(1), _col_spec(2), _full_spec((8, D))]
        args = [dconv, p, p, cw8]
    return pl.pallas_call(
        body, name=name, grid=(t // TM,), in_specs=in_specs,
        out_specs=[_row_spec(D), _row_spec(D), _full_spec((8, D))],
        out_shape=[jax.ShapeDtypeStruct((t, D), BF16), jax.ShapeDtypeStruct((t, D), BF16), jax.ShapeDtypeStruct((8, D), F32)],
        compiler_params=_cp(1),
    )(*args)


LOG2E = 1.4426950408889634


def _cumsum_matrix(reverse):
    r = lax.broadcasted_iota(jnp.int32, (CH, CH), 0)
    c = lax.broadcasted_iota(jnp.int32, (CH, CH), 1)
    return (r <= c if reverse else r >= c).astype(BF16)


def _chunk_cumsum(g, tri):
    hi = g.astype(BF16)
    lo = (g - hi.astype(F32)).astype(BF16)
    return _dot(tri, hi) + _dot(tri, lo)


def _gate_values(z, lb):
    sig = _sigmoid(z)
    f = lb + (1.0 - lb) * sig
    return sig, f


def _tri(direction, transposed):
    r = lax.broadcasted_iota(jnp.int32, (CH, CH), 0)
    c = lax.broadcasted_iota(jnp.int32, (CH, CH), 1)
    lower = (direction == 0) != transposed
    return r >= c if lower else r <= c


def _gla_rows(direction):
    return (CH // 2 - 1, CH - 1) if direction == 0 else (CH // 2, 0)


def gla_fwd(p, lb2, direction, name):
    t = p.shape[0]
    nt = t // TM
    per = TM // CH
    ref_row, last_row = _gla_rows(direction)
    tile = (lambda i: (i + nt - 1) % nt) if direction == 0 else (lambda i: nt - 1 - i)

    def body(z_ref, v_ref, qr_ref, lb_ref, o_ref, s_ref, st, c_s):
        @pl.when(pl.program_id(0) == 0)
        def _():
            st[...] = jnp.zeros_like(st)

        mask = _tri(direction, False)
        tri = _cumsum_matrix(direction == 1)
        for it in range(per):
            ci = it if direction == 0 else per - 1 - it
            rows = slice(ci * CH, (ci + 1) * CH)
            for h in range(NH):
                sl = slice(h * HD, (h + 1) * HD)
                _, f = _gate_values(z_ref[rows, sl], lb_ref[direction:direction + 1, sl])
                k = 1.0 - f
                cum = _chunk_cumsum(jnp.log(f) * LOG2E, tri)
                c_s[:, sl] = cum
                ref = c_s[ref_row:ref_row + 1, sl]
                last = c_s[last_row:last_row + 1, sl]
                qr = qr_ref[rows, sl]
                q = qr * _sigmoid(qr)
                qh = (q * jnp.exp2(cum)).astype(BF16)
                qt = (q * jnp.exp2(cum - ref)).astype(BF16)
                kt = (k * jnp.exp2(ref - cum)).astype(BF16)
                kb = (k * jnp.exp2(last - cum)).astype(BF16)
                vb = v_ref[rows, sl].astype(BF16)
                s_t = st[h]
                s_ref[ci, h] = s_t
                sc = jnp.where(mask, _dot_nt(qt, kt), 0.0)
                o_ref[rows, sl] = _dot_nt(qh, s_t.astype(BF16)) + _dot(sc.astype(BF16), vb)
                st[h] = s_t * jnp.exp2(last) + _dot_tn(vb, kb)

    tspec = lambda col: pl.BlockSpec((TM, D), lambda i: (tile(i), col))
    return pl.pallas_call(
        body, name=name, grid=(nt,),
        in_specs=[tspec(direction), tspec(2), tspec(3), _full_spec((2, D))],
        out_specs=[pl.BlockSpec((TM, D), lambda i: (tile(i), 0)), pl.BlockSpec((per, NH, HD, HD), lambda i: (tile(i), 0, 0, 0))],
        out_shape=[jax.ShapeDtypeStruct((t, D), F32), jax.ShapeDtypeStruct((t // CH, NH, HD, HD), F32)],
        scratch_shapes=[pltpu.VMEM((NH, HD, HD), F32), pltpu.VMEM((CH, D), F32)],
        compiler_params=_cp(1),
    )(p, p, p, lb2)


def gla_bwd(p, lb2, do, states, direction, prev, name):
    t = p.shape[0]
    nt = t // TM
    per = TM // CH
    ref_row, last_row = _gla_rows(direction)
    tile = (lambda i: (2 * nt - 2 - i) % nt) if direction == 0 else (lambda i: i)
    final = prev is not None
    n_in = 8 if final else 6

    def body(*refs):
        z_ref, v_ref, qr_ref, lb_ref, do_ref, s_ref = refs[:6]
        dz_ref, dv_ref, dq_ref, acc_ref, dst, c_s = refs[n_in:]

        @pl.when(pl.program_id(0) == 0)
        def _():
            dst[...] = jnp.zeros_like(dst)
            acc_ref[...] = jnp.zeros_like(acc_ref)

        mask = _tri(direction, False)
        mask_t = _tri(direction, True)
        tri = _cumsum_matrix(direction == 1)
        tri_t = _cumsum_matrix(direction == 0)
        is_last = lax.broadcasted_iota(jnp.int32, (CH, 1), 0) == last_row
        for it in range(per):
            ci = per - 1 - it if direction == 0 else it
            rows = slice(ci * CH, (ci + 1) * CH)
            for h in range(NH):
                sl = slice(h * HD, (h + 1) * HD)
                lb = lb_ref[direction:direction + 1, sl]
                sig, f = _gate_values(z_ref[rows, sl], lb)
                k = 1.0 - f
                cum = _chunk_cumsum(jnp.log(f) * LOG2E, tri)
                c_s[:, sl] = cum
                ref = c_s[ref_row:ref_row + 1, sl]
                last = c_s[last_row:last_row + 1, sl]
                qr = qr_ref[rows, sl]
                sq = _sigmoid(qr)
                q = qr * sq
                e_h = jnp.exp2(cum)
                e_t = jnp.exp2(cum - ref)
                e_kt = jnp.exp2(ref - cum)
                e_kb = jnp.exp2(last - cum)
                el = jnp.exp2(last)
                qh = (q * e_h).astype(BF16)
                qt = (q * e_t).astype(BF16)
                kt = (k * e_kt).astype(BF16)
                kbf = k * e_kb
                kb = kbf.astype(BF16)
                vb = v_ref[rows, sl].astype(BF16)
                dob = do_ref[rows, sl].astype(BF16)
                s_t = s_ref[ci, h]
                ds_t = dst[h]
                ds_b = ds_t.astype(BF16)
                d_a = jnp.where(mask, _dot_nt(dob, vb), 0.0).astype(BF16)
                a_t = jnp.where(mask_t, _dot_nt(kt, qt), 0.0).astype(BF16)
                d_at = jnp.where(mask_t, _dot_nt(vb, dob), 0.0).astype(BF16)
                dv = _dot(a_t, dob) + _dot_nt(kb, ds_b)
                dkb = _dot(vb, ds_b)
                dq = _dot(dob, s_t.astype(BF16)) * e_h + _dot(d_a, kt) * e_t
                dk = _dot(d_at, qt) * e_kt + dkb * e_kb
                dlast = el * jnp.sum(ds_t * s_t, axis=0, keepdims=True) + jnp.sum(dkb * kbf, axis=0, keepdims=True)
                dst[h] = ds_t * el + _dot_tn(dob, qh)
                dg = _chunk_cumsum(dq * q - dk * k + jnp.where(is_last, dlast, 0.0), tri_t)
                df = dg / f - dk
                acc_ref[0:1, sl] += jnp.sum(df * (1.0 - sig), axis=0, keepdims=True)
                dz_ref[rows, sl] = (df * (1.0 - lb) * sig * (1.0 - sig)).astype(BF16)
                if final:
                    dv_ref[rows, sl] = (refs[6][rows, sl] + dv).astype(BF16)
                    dq_ref[rows, sl] = ((refs[7][rows, sl] + dq) * (sq * (1.0 + qr * (1.0 - sq)))).astype(BF16)
                else:
                    dv_ref[rows, sl] = dv
                    dq_ref[rows, sl] = dq

    tspec = lambda col: pl.BlockSpec((TM, D), lambda i: (tile(i), col))
    sspec = pl.BlockSpec((per, NH, HD, HD), lambda i: (tile(i), 0, 0, 0))
    in_specs = [tspec(direction), tspec(2), tspec(3), _full_spec((2, D)), tspec(0), sspec]
    args = [p, p, p, lb2, do, states]
    if final:
        in_specs += [tspec(0), tspec(0)]
        args += list(prev)
    odt = BF16 if final else F32
    return pl.pallas_call(
        body, name=name, grid=(nt,), in_specs=in_specs,
        out_specs=[tspec(0), tspec(0), tspec(0), _full_spec((8, D))],
        out_shape=[jax.ShapeDtypeStruct((t, D), BF16), jax.ShapeDtypeStruct((t, D), odt), jax.ShapeDtypeStruct((t, D), odt),
                   jax.ShapeDtypeStruct((8, D), F32)],
        scratch_shapes=[pltpu.VMEM((NH, HD, HD), F32), pltpu.VMEM((CH, D), F32)],
        compiler_params=_cp(1),
    )(*args)


def loss_bwd(x, gain, target, n_lat, name):
    t = x.shape[0]

    def body(x_ref, gain_ref, tg_ref, dx_ref, acc_ref):
        i = pl.program_id(0)

        @pl.when(i == 0)
        def _():
            acc_ref[...] = jnp.zeros_like(acc_ref)

        latf = (i < n_lat).astype(F32)
        x = x_ref[...]
        gain = gain_ref[...]
        r = lax.rsqrt(jnp.mean(x * x, axis=-1, keepdims=True) + EPS)
        xn = x * r
        err = (xn * gain - tg_ref[...]) * latf
        dy = err * (1.0 / D)
        dxn = dy * gain
        dx_ref[...] = r * (dxn - xn * jnp.mean(dxn * xn, axis=-1, keepdims=True))
        acc_ref[0:1, :] += jnp.sum(dy * xn, axis=0, keepdims=True)
        acc_ref[1:2, :] += jnp.sum(err * err, axis=0, keepdims=True)

    return pl.pallas_call(
        body, name=name, grid=(t // TM,),
        in_specs=[_row_spec(D), _full_spec((1, D)), pl.BlockSpec((TM, D), lambda i: (jnp.minimum(i, n_lat - 1), 0))],
        out_specs=[_row_spec(D), _full_spec((8, D))],
        out_shape=[jax.ShapeDtypeStruct((t, D), F32), jax.ShapeDtypeStruct((8, D), F32)],
        compiler_params=_cp(1),
    )(x, gain, target)


def local_step(xs, target, mods, norm1, norm2, norm_f, lbs, gnorm, cw8, cb, wts, n_lat, on_grads):
    t = xs.shape[0]
    saved = []
    cache = {}

    def W(name, idx, after=None):
        if (name, idx) not in cache:
            cache[(name, idx)] = wts(name, idx, after)
        return cache[(name, idx)]

    x = xs
    for i in range(DEPTH):
        j = i // 2
        rec = i % 2 == 0
        n1 = norm1[i:i + 1]
        n2 = norm2[i:i + 1]
        s = {"x_in": x}
        if rec:
            p = proj_fwd(x, n1, mods[i], 0, W("hin", j, x), n_lat, f"hin_fwd_{i}")
            o0, st0 = gla_fwd(p, lbs[j], 0, f"gla_fwd0_{i}")
            o1, st1 = gla_fwd(p, lbs[j], 1, f"gla_fwd1_{i}")
            ex = [o0, o1, p, gnorm[j:j + 1]]
            ex_specs = [_row_spec(D), _row_spec(D), _col_spec(4), _full_spec((1, D))]
            xm, y, ab = outproj_fwd(readout_prologue, ex, ex_specs, W("hout", j, o1), x, mods[i], 0, n_lat, f"hout_fwd_{i}")
            s.update(st0=st0, st1=st1)
        else:
            sft = 1 if j % 2 == 0 else CH
            p = proj_fwd(x, n1, mods[i], 0, W("cin", j, x), n_lat, f"cin_fwd_{i}")
            ex = _conv_args(sft, p, cw8[j], cb[j])
            ex_specs = _conv_specs(sft, t)
            xm, y, ab = outproj_fwd(make_conv_prologue(n_lat, sft), ex, ex_specs, W("cout", j, p), x, mods[i], 0, n_lat, f"cout_fwd_{i}")
        s.update(p=p, ex=ex, ex_specs=ex_specs, y_mix=y, ab_mix=ab, x_mid=xm)
        x, y2, ab2 = mlp_fwd(xm, n2, mods[i], W("w1", i, xm), W("w2", i, xm), n_lat, f"mlp_fwd_{i}")
        s.update(y_mlp=y2, ab_mlp=ab2)
        saved.append(s)

    dx, acc_loss = loss_bwd(x, norm_f, target, n_lat, "loss")
    small = {"norm_f": acc_loss[0:1], "norm1": [None] * DEPTH, "norm2": [None] * DEPTH, "dmod": [None] * DEPTH,
             "gnorm": [None] * 2, "lb": [None] * 2, "cw": [None] * 2, "cb": [None] * 2}
    bshape = lambda w: jax.ShapeDtypeStruct((t, w), BF16)
    for i in reversed(range(DEPTH)):
        j = i // 2
        rec = i % 2 == 0
        s = saved[i]
        n1 = norm1[i:i + 1]
        n2 = norm2[i:i + 1]
        dx, dyb, dp1, hb, acc_n2 = mlp_bwd(dx, s["y_mlp"], s["ab_mlp"], s["x_mid"], n2, mods[i], W("w1", i), W("w2", i), n_lat, f"mlp_bwd_{i}")
        on_grads(i, "mlp", {"w2": dw_tn(s["ab_mlp"], dyb, 4, True, f"w2_dw_{i}"), "w1": dw_tn(hb, dp1, 4, False, f"w1_dw_{i}")})
        g = {}
        if rec:
            dyb, acc_g1, (do, dgate) = outproj_bwd(
                readout_epilogue, s["ex"], s["ex_specs"], [_row_spec(D), _row_spec(D)],
                [jax.ShapeDtypeStruct((t, D), F32), bshape(D)], W("hout", j), dx, s["y_mix"], mods[i], 0, n_lat, f"hout_bwd_{i}")
            g["hout"] = dw_tn(s["ab_mix"], dyb, 1, False, f"hout_dw_{i}")
            dz0, dv0, dq0, acc_l0 = gla_bwd(s["p"], lbs[j], do, s["st0"], 0, None, f"gla_bwd0_{i}")
            dz1, dv, dq, acc_l1 = gla_bwd(s["p"], lbs[j], do, s["st1"], 1, (dv0, dq0), f"gla_bwd1_{i}")
            dx, hb, dpb, acc_n1 = proj_bwd([dz0, dz1, dv, dq, dgate], W("hin", j), s["x_in"], n1, mods[i], 0, dx, n_lat, f"hin_bwd_{i}")
            g["hin"] = dw_tn(hb, dpb, 4, False, f"hin_dw_{i}")
            small["gnorm"][j] = acc_g1[2:3]
            small["lb"][j] = jnp.concatenate([acc_l0[0:1], acc_l1[0:1]], axis=0)
        else:
            sft = 1 if j % 2 == 0 else CH
            dyb, acc_g1, (dconv, dgb) = outproj_bwd(
                make_conv_epilogue(n_lat, sft), s["ex"], s["ex_specs"], [_row_spec(D), _row_spec(D)],
                [jax.ShapeDtypeStruct((t, D), F32), bshape(D)], W("cout", j), dx, s["y_mix"], mods[i], 0, n_lat, f"cout_bwd_{i}")
            g["cout"] = dw_tn(s["ab_mix"], dyb, 1, False, f"cout_dw_{i}")
            dgc, dxi, acc_c = conv_bwd(dconv, s["p"], cw8[j], sft, n_lat, f"conv_bwd_{i}")
            dx, hb, dpb, acc_n1 = proj_bwd([dgb, dgc, dxi], W("cin", j), s["x_in"], n1, mods[i], 0, dx, n_lat, f"cin_bwd_{i}")
            g["cin"] = dw_tn(hb, dpb, 4, False, f"cin_dw_{i}")
            small["cw"][j] = acc_c[0:3]
            small["cb"][j] = acc_c[3:4]
        small["norm1"][i] = acc_n1[0:1]
        small["norm2"][i] = acc_n2[0:1]
        z2 = jnp.zeros((2, D), F32)
        small["dmod"][i] = jnp.concatenate([acc_n1[1:3], acc_g1[0:1], acc_n2[1:3], acc_n2[5:6], z2,
                                            acc_n1[3:5], acc_g1[1:2], acc_n2[3:5], acc_n2[6:7], z2], axis=0)
        on_grads(i, "mix", g)
    return acc_loss[1:2], dx, small


RB = 256


def cast_to_slot(w2d, layer, k, chip1, name):
    c = w2d.shape[1]
    nblk = k // RB

    def body(chip_ref, w_ref, o_ref):
        o_ref[0] = w_ref[...].astype(BF16)

    return pl.pallas_call(
        body, name=name,
        grid_spec=pltpu.PrefetchScalarGridSpec(
            num_scalar_prefetch=1, grid=(nblk,),
            in_specs=[pl.BlockSpec((RB, c), lambda i, ch: (layer * nblk + i, 0))],
            out_specs=pl.BlockSpec((1, RB, c), lambda i, ch: (ch[0], i, 0))),
        out_shape=jax.ShapeDtypeStruct((4, k, c), BF16), compiler_params=_cp(1))(chip1, w2d)


def sum_slots(own, land, acc, layer, chip1, name):
    _, k, c = own.shape
    nblk = k // RB

    def body(chip_ref, own_ref, l1_ref, l2_ref, l3_ref, acc_ref, o_ref):
        o_ref[...] = ((own_ref[0].astype(F32) + l1_ref[0].astype(F32)) + l2_ref[0].astype(F32)) + l3_ref[0].astype(F32)

    slot = lambda d: pl.BlockSpec((1, RB, c), lambda i, ch: ((ch[0] + d) % 4, i, 0))
    return pl.pallas_call(
        body, name=name,
        grid_spec=pltpu.PrefetchScalarGridSpec(
            num_scalar_prefetch=1, grid=(nblk,),
            in_specs=[slot(0), slot(1), slot(2), slot(3), ANY],
            out_specs=pl.BlockSpec((RB, c), lambda i, ch: (layer * nblk + i, 0))),
        out_shape=jax.ShapeDtypeStruct(acc.shape, F32), input_output_aliases={5: 0}, compiler_params=_cp(1),
    )(chip1, own, land, land, land, acc)


def _adamw_math(w, g, m, v):
    m = ADAM_B1 * m + (1.0 - ADAM_B1) * g
    v = ADAM_B2 * v + (1.0 - ADAM_B2) * jnp.square(g)
    m_hat = m / (1.0 - ADAM_B1 ** ADAM_STEP)
    v_hat = v / (1.0 - ADAM_B2 ** ADAM_STEP)
    delta = -ADAM_LR * (m_hat / (jnp.sqrt(v_hat) + ADAM_EPS) + ADAM_WD * w)
    return delta, m, v


def adamw(gsrcs, w, m, v, name):
    r, c = w.shape
    rb = RB if r % RB == 0 else r
    n_g = len(gsrcs)

    def body(*refs):
        g = refs[0][...]
        for k in range(1, n_g):
            g = g + refs[k][...]
        w_ref, m_ref, v_ref, g_ref, d_ref, mo_ref, vo_ref = refs[n_g:]
        delta, mo, vo = _adamw_math(w_ref[...], g, m_ref[...], v_ref[...])
        g_ref[...] = g
        d_ref[...] = delta
        mo_ref[...] = mo
        vo_ref[...] = vo

    spec = pl.BlockSpec((rb, c), lambda i: (i, 0))
    shp = jax.ShapeDtypeStruct((r, c), F32)
    return pl.pallas_call(body, name=name, grid=(r // rb,), in_specs=[spec] * (n_g + 3), out_specs=[spec] * 4,
                          out_shape=[shp] * 4, compiler_params=_cp(1))(*gsrcs, w, m, v)


ADA_CB = 512


def ada_fwd(cvec, ada_w, bias, name):
    _, _, nc = ada_w.shape

    def body(c_ref, w_ref, b_ref, o_ref):
        cv = c_ref[...]
        a = (cv * _sigmoid(cv)).astype(BF16)
        o_ref[0] = _dot(a, w_ref[0].astype(BF16)) + b_ref[0]

    return pl.pallas_call(
        body, name=name, grid=(DEPTH, nc // ADA_CB),
        in_specs=[pl.BlockSpec((16, D), lambda i, j: (0, 0)), pl.BlockSpec((1, D, ADA_CB), lambda i, j: (i, 0, j)),
                  pl.BlockSpec((1, 1, ADA_CB), lambda i, j: (i, 0, j))],
        out_specs=pl.BlockSpec((1, 16, ADA_CB), lambda i, j: (i, 0, j)),
        out_shape=jax.ShapeDtypeStruct((DEPTH, 16, nc), F32), compiler_params=_cp(2),
    )(cvec, ada_w, bias)


def ada_bwd(cvec, dcols, ada_w, m, v, name):
    _, _, nc = ada_w.shape

    def body(c_ref, d_ref, w_ref, m_ref, v_ref, g_ref, dl_ref, mo_ref, vo_ref, acc_ref):
        @pl.when(jnp.logical_and(pl.program_id(0) == 0, pl.program_id(1) == 0))
        def _():
            acc_ref[...] = jnp.zeros_like(acc_ref)

        cv = c_ref[...]
        a = (cv * _sigmoid(cv)).astype(BF16)
        db = d_ref[0].astype(BF16)
        w = w_ref[0]
        g = _dot_tn(a, db)
        delta, mo, vo = _adamw_math(w, g, m_ref[0], v_ref[0])
        g_ref[0] = g
        dl_ref[0] = delta
        mo_ref[0] = mo
        vo_ref[0] = vo
        acc_ref[...] += _dot_nt(db[8:16, :], w.astype(BF16))

    wspec = pl.BlockSpec((1, D, ADA_CB), lambda i, j: (i, 0, j))
    wshape = jax.ShapeDtypeStruct(ada_w.shape, F32)
    return pl.pallas_call(
        body, name=name, grid=(DEPTH, nc // ADA_CB),
        in_specs=[pl.BlockSpec((16, D), lambda i, j: (0, 0)), pl.BlockSpec((1, 16, ADA_CB), lambda i, j: (i, 0, j)), wspec, wspec, wspec],
        out_specs=[wspec, wspec, wspec, wspec, pl.BlockSpec((8, D), lambda i, j: (0, 0))],
        out_shape=[wshape, wshape, wshape, wshape, jax.ShapeDtypeStruct((8, D), F32)], compiler_params=_cp(2),
    )(cvec, dcols, ada_w, m, v)


def _place():
    return lax.axis_index("x"), lax.axis_index("y"), lax.axis_index("c")


ANY = pl.BlockSpec(memory_space=pl.ANY)
VMEM_SPEC = pl.BlockSpec(memory_space=pltpu.VMEM)


def small_allgather(buf, name):
    r, c = buf.shape

    def body(in_ref, out_ref, send_sems, recv_sems, loc_sem):
        x, y, cc = _place()
        me = 4 * x + 2 * y + cc
        loc = pltpu.make_async_copy(in_ref, out_ref.at[me], loc_sem)
        loc.start()
        peers = []
        for k in range(1, 8):
            px = 1 - x if k & 4 else x
            py = 1 - y if k & 2 else y
            pc = 1 - cc if k & 1 else cc
            peers.append((px, py, pc))
        sends = []
        for k, peer in enumerate(peers):
            cp = pltpu.make_async_remote_copy(src_ref=in_ref, dst_ref=out_ref.at[me], send_sem=send_sems.at[k],
                                              recv_sem=recv_sems.at[k], device_id=peer, device_id_type=MESH)
            cp.start()
            sends.append(cp)
        for k, (px, py, pc) in enumerate(peers):
            pltpu.make_async_remote_copy(src_ref=in_ref, dst_ref=out_ref.at[4 * px + 2 * py + pc], send_sem=send_sems.at[k],
                                         recv_sem=recv_sems.at[k], device_id=(px, py, pc), device_id_type=MESH).wait_recv()
        for cp in sends:
            cp.wait_send()
        loc.wait()

    return pl.pallas_call(
        body, name=name, in_specs=[VMEM_SPEC], out_specs=VMEM_SPEC,
        out_shape=jax.ShapeDtypeStruct((8, r, c), buf.dtype),
        scratch_shapes=[pltpu.SemaphoreType.DMA((7,)), pltpu.SemaphoreType.DMA((7,)), pltpu.SemaphoreType.DMA],
    )(buf)


def _chip_peers(x, y):
    return [(1 - x, y), (x, 1 - y), (1 - x, 1 - y)]


HBM_SPEC = pl.BlockSpec(memory_space=pltpu.HBM)
SEM_SPEC = pl.BlockSpec(memory_space=pltpu.SEMAPHORE)
EFFECT = pltpu.SideEffectType.DATAFLOW_SIDE_EFFECTING


def _hbm(a):
    return pltpu.with_memory_space_constraint(a, pltpu.HBM)


def _split_copy(u, p, peer, dst_slot, chip, land_refs, src_refs, sem_refs, cc):
    px, py = peer
    src = land_refs[u].at[chip] if src_refs is None else src_refs[u].at[2 * px + py]
    return pltpu.make_async_remote_copy(src_ref=src, dst_ref=land_refs[u].at[dst_slot], send_sem=sem_refs[2 * u].at[p],
                                        recv_sem=sem_refs[2 * u + 1].at[p], device_id=(px, py, cc), device_id_type=MESH)


def split_start(lands, srcs, name):
    n = len(lands)
    ops = list(lands) + (list(srcs) if srcs is not None else [])
    n_ops = len(ops)

    def body(*refs):
        land_refs = refs[:n]
        src_refs = refs[n:n_ops] if srcs is not None else None
        sem_refs = refs[n_ops:n_ops + 2 * n]
        x, y, cc = _place()
        chip = 2 * x + y
        for u in range(n):
            for p, peer in enumerate(_chip_peers(x, y)):
                _split_copy(u, p, peer, chip, chip, land_refs, src_refs, sem_refs, cc).start()

    outs = pl.pallas_call(
        body, name=name, in_specs=[HBM_SPEC] * n_ops,
        out_specs=[SEM_SPEC] * (2 * n) + [HBM_SPEC] * n_ops,
        out_shape=[pltpu.SemaphoreType.DMA((3,))] * (2 * n) + [pltpu.HBM(a.shape, a.dtype) for a in ops],
        input_output_aliases={k: 2 * n + k for k in range(n_ops)},
        compiler_params=pltpu.CompilerParams(has_side_effects=EFFECT),
    )(*[_hbm(a) for a in ops])
    sems = list(outs[:2 * n])
    thru = list(outs[2 * n:])
    return sems, thru[:n], thru[n:]


def split_wait(lands, srcs, sems, after, name):
    n = len(lands)
    ops = list(lands) + (list(srcs) if srcs is not None else [])
    n_ops = len(ops)

    def body(*refs):
        land_refs = refs[:n]
        src_refs = refs[n:n_ops] if srcs is not None else None
        sem_refs = refs[n_ops:n_ops + 2 * n]
        x, y, cc = _place()
        chip = 2 * x + y
        for u in range(n):
            for p, peer in enumerate(_chip_peers(x, y)):
                cp = _split_copy(u, p, peer, 2 * peer[0] + peer[1], chip, land_refs, src_refs, sem_refs, cc)
                cp.wait_send()
                cp.wait_recv()

    outs = pl.pallas_call(
        body, name=name, in_specs=[HBM_SPEC] * n_ops + [SEM_SPEC] * (2 * n) + [ANY],
        out_specs=[HBM_SPEC] * n_ops, out_shape=[pltpu.HBM(a.shape, a.dtype) for a in ops],
        input_output_aliases={k: k for k in range(n_ops)},
        compiler_params=pltpu.CompilerParams(has_side_effects=EFFECT),
    )(*ops, *sems, after)
    return list(outs[:n]), list(outs[n:])


def _sibling_copy(k, src_refs, zone_refs, sem_refs):
    x, y, cc = _place()
    return pltpu.make_async_remote_copy(src_ref=src_refs[k], dst_ref=zone_refs[k], send_sem=sem_refs[2 * k], recv_sem=sem_refs[2 * k + 1],
                                        device_id=(x, y, 1 - cc), device_id_type=MESH)


def sibling_start(parts, name):
    n = len(parts)
    ops = list(parts) + [lax.empty(p.shape, p.dtype) for p in parts]

    def body(*refs):
        for k in range(n):
            _sibling_copy(k, refs[:n], refs[n:2 * n], refs[2 * n:4 * n]).start()

    outs = pl.pallas_call(
        body, name=name, in_specs=[HBM_SPEC] * (2 * n),
        out_specs=[SEM_SPEC] * (2 * n) + [HBM_SPEC] * (2 * n),
        out_shape=[pltpu.SemaphoreType.DMA(())] * (2 * n) + [pltpu.HBM(a.shape, a.dtype) for a in ops],
        input_output_aliases={k: 2 * n + k for k in range(2 * n)},
        compiler_params=pltpu.CompilerParams(has_side_effects=EFFECT),
    )(*[_hbm(a) for a in ops])
    return list(outs[2 * n:3 * n]), list(outs[3 * n:]), list(outs[:2 * n])


def sibling_wait(parts, zones, sems, after, name):
    n = len(parts)

    def body(*refs):
        for k in range(n):
            cp = _sibling_copy(k, refs[:n], refs[n:2 * n], refs[2 * n:4 * n])
            cp.wait_send()
            cp.wait_recv()

    outs = pl.pallas_call(
        body, name=name, in_specs=[HBM_SPEC] * (2 * n) + [SEM_SPEC] * (2 * n) + [ANY],
        out_specs=[HBM_SPEC] * (2 * n), out_shape=[pltpu.HBM(a.shape, a.dtype) for a in list(parts) + list(zones)],
        input_output_aliases={k: k for k in range(2 * n)},
        compiler_params=pltpu.CompilerParams(has_side_effects=EFFECT),
    )(*parts, *zones, *sems, after)
    return list(outs[:n]), list(outs[n:])


SMALL_ROWS = 88
FIN_ROWS = 72


def small_finish(g3, g4, c_ctx, lbp, name):
    def body(g3_ref, g4_ref, cc_ref, lbp_ref, o_ref, s_ref):
        s = g3_ref[0]
        for k in range(1, 8):
            s = s + g3_ref[k]
        s_ref[...] = s
        for i in range(DEPTH):
            o_ref[8 * i:8 * i + 8, :] = s_ref[16 * i:16 * i + 8, :] + s_ref[16 * i + 8:16 * i + 16, :]
        acc = g4_ref[0]
        for k in (2, 4, 6):
            acc = acc + g4_ref[k]
        cc = cc_ref[...]
        sg = _sigmoid(cc)
        row = jnp.sum(acc, axis=0, keepdims=True) * (sg * (1.0 + cc * (1.0 - sg)))
        o_ref[32:40, :] = jnp.broadcast_to(row, (8, D))
        o_ref[40:64, :] = s_ref[64:88, :]
        o_ref[64:72, :] = jnp.zeros((8, D), F32)
        for d in range(2):
            pp = lbp_ref[2 * d:2 * d + 1, :] * lbp_ref[2 * d + 1:2 * d + 2, :] * s_ref[75 + d:76 + d, :]
            o_ref[64 + 2 * d:65 + 2 * d, :] = -pp
            o_ref[65 + 2 * d:66 + 2 * d, :] = pp

    return pl.pallas_call(
        body, name=name, in_specs=[VMEM_SPEC] * 4, out_specs=VMEM_SPEC,
        out_shape=jax.ShapeDtypeStruct((FIN_ROWS, D), F32),
        scratch_shapes=[pltpu.VMEM((SMALL_ROWS, D), F32)],
    )(g3, g4, c_ctx, lbp)


def _pack_rows(arrs):
    flat = jnp.concatenate([a.reshape(-1) for a in arrs])
    n = -(-flat.shape[0] // (8 * D)) * 8 * D
    return jnp.pad(flat, (0, n - flat.shape[0])).reshape(n // D, D)


def _unpack_rows(packed, shapes):
    flat = packed.reshape(-1)
    outs, off = [], 0
    for s in shapes:
        size = 1
        for k in s:
            size *= k
        outs.append(flat[off:off + size].reshape(s))
        off += size
    return outs


def _pad8(a):
    return jnp.pad(a, ((0, 8 - a.shape[0]), (0, 0)))


def kernel(x, c, ctx, c_ctx, ada_w, ada_b, norm1, norm2, norm_f, mlp_w1, mlp_w2, hgrn_w_in, hgrn_lb, hgrn_gnorm, hgrn_w_out, conv_w_in, conv_w, conv_b, conv_w_out, loss_target, m_c_ctx, m_ada_w, m_ada_b, m_norm1, m_norm2, m_norm_f, m_mlp_w1, m_mlp_w2, m_hgrn_w_in, m_hgrn_lb, m_hgrn_gnorm, m_hgrn_w_out, m_conv_w_in, m_conv_w, m_conv_b, m_conv_w_out, v_c_ctx, v_ada_w, v_ada_b, v_norm1, v_norm2, v_norm_f, v_mlp_w1, v_mlp_w2, v_hgrn_w_in, v_hgrn_lb, v_hgrn_gnorm, v_hgrn_w_out, v_conv_w_in, v_conv_w, v_conv_b, v_conv_w_out):
    xi, yi, ci = _place()
    me = 4 * xi + 2 * yi + ci
    chip = 2 * xi + yi
    seq = x.shape[1]
    assert ctx.shape[1] == TM and seq % TM == 0 and (seq + TM) % TMW == 0
    n_lat = seq // TM
    sd = D // 4
    nca = ada_w.shape[2]
    xs = jnp.concatenate([x[0], ctx[0]], axis=0)

    sh_rows = jnp.concatenate([hgrn_lb.reshape(4, sd), conv_w.reshape(6, sd), conv_b.reshape(2, sd)], axis=0)
    buf1 = jnp.concatenate([c, jnp.pad(sh_rows, ((0, 0), (0, D - sd))), jnp.zeros((3, D), F32)], axis=0)
    g1 = small_allgather(buf1, "gather_small_in")
    cvec = jnp.concatenate([g1[:, 0, :], jnp.broadcast_to(c_ctx[None], (8, D))], axis=0)
    shf = g1[0::2, 1:13, :sd].transpose(1, 0, 2).reshape(12, D)
    lb_p = jax.nn.softmax(shf[0:4].reshape(2, 2, D), axis=1)
    lower = jnp.cumsum(lb_p, axis=1) - lb_p[:, :1]
    lbs = [lower[:, 0], lower[:, 1]]
    cw8 = [_pad8(shf[4:7]), _pad8(shf[7:10])]
    cb = [shf[10:11], shf[11:12]]

    bias = lax.dynamic_slice_in_dim(ada_b, chip * nca, nca, axis=1).reshape(DEPTH, 1, nca)
    ada_part = ada_fwd(cvec, ada_w, bias, "ada_fwd")
    g2 = small_allgather(ada_part.reshape(DEPTH * 16, nca), "gather_ada")
    ada_full = g2[0::2].reshape(4, DEPTH, 16, nca).transpose(1, 2, 0, 3).reshape(DEPTH, 16, 4 * nca)
    lat = lax.dynamic_slice_in_dim(ada_full, me, 1, axis=1)[:, 0]
    mods = [jnp.stack([_pad8(lat[i].reshape(6, D)), _pad8(ada_full[i, 8].reshape(6, D))]) for i in range(DEPTH)]

    big = [(mlp_w1, m_mlp_w1, v_mlp_w1), (mlp_w2, m_mlp_w2, v_mlp_w2), (hgrn_w_in, m_hgrn_w_in, v_hgrn_w_in),
           (hgrn_w_out, m_hgrn_w_out, v_hgrn_w_out), (conv_w_in, m_conv_w_in, v_conv_w_in), (conv_w_out, m_conv_w_out, v_conv_w_out)]
    big_names = ["w1", "w2", "hin", "hout", "cin", "cout"]
    flat2 = lambda a: a.reshape(a.shape[0] * a.shape[1], a.shape[2])
    tensors = dict(zip(big_names, big))
    chip1 = jnp.reshape(chip, (1,)).astype(jnp.int32)
    order = []
    for i in range(DEPTH):
        order += [("hin", i // 2), ("hout", i // 2)] if i % 2 == 0 else [("cin", i // 2), ("cout", i // 2)]
        order += [("w1", i), ("w2", i)]
    lands = [cast_to_slot(flat2(tensors[n][0]), idx, tensors[n][0].shape[1], chip1, f"cast_{n}_{idx}") for n, idx in order]
    w_sems, lands, _ = split_start(lands, None, "gather_start")
    unit = {key: u for u, key in enumerate(order)}

    def wts(n, idx, after):
        u = unit[(n, idx)]
        (w,), _ = split_wait([lands[u]], None, w_sems[2 * u:2 * u + 2], after, f"gather_wait_{n}_{idx}")
        return w.reshape(w.shape[0] * w.shape[1], w.shape[2]) if n in ("w2", "hout", "cout") else w

    started = []

    def on_grads(i, tag, g):
        names = sorted(g)
        gs = [g[n].reshape(4, g[n].shape[0] * g[n].shape[1] // 4, g[n].shape[2]) for n in names]
        sems, zones, srcs = split_start([lax.empty(a.shape, BF16) for a in gs], gs, f"grad_start_{tag}_{i}")
        started.append(([(n, i if n in ("w1", "w2") else i // 2) for n in names], sems, zones, srcs))

    lane, dx, small = local_step(xs, loss_target[0], mods, norm1, norm2, norm_f[None], lbs, hgrn_gnorm, cw8, cb, wts, n_lat, on_grads)
    grad_x = dx[:seq][None]

    rows3 = jnp.concatenate(small["dmod"] + small["norm1"] + small["norm2"] + [small["norm_f"]] + small["gnorm"]
                            + [small["lb"][1]] + small["cw"] + small["cb"] + [lane, jnp.zeros((2, D), F32)], axis=0)
    g3 = small_allgather(rows3, "gather_small_out")
    dmat = g3[:, :64].reshape(8, DEPTH, 2, 8, D)[:, :, :, :6].transpose(1, 2, 0, 3, 4).reshape(DEPTH, 16, 6 * D)
    dcols = lax.dynamic_slice_in_dim(dmat, chip * nca, nca, axis=2)
    g_ada_w, d_ada_w, nm_ada_w, nv_ada_w, acc4 = ada_bwd(cvec, dcols, ada_w, m_ada_w, v_ada_w, "ada_bwd")
    g4 = small_allgather(acc4, "gather_cctx")
    fin = small_finish(g3, g4, c_ctx[None], _pad8(lb_p.reshape(4, D)), "small_finish")
    loss = 0.5 * jnp.sum(fin[61]) / D
    cols = lambda a: lax.dynamic_slice_in_dim(a, chip * sd, sd, axis=a.ndim - 1)
    small_g = [fin[32], fin[0:32].reshape(DEPTH, 8, D)[:, :6].reshape(DEPTH, 6 * D), fin[40:44], fin[44:48], fin[48], fin[49:51],
               cols(fin[64:68].reshape(2, 2, D)), cols(fin[53:59].reshape(2, 3, D)), cols(fin[59:61])]
    small_w = [c_ctx, ada_b, norm1, norm2, norm_f, hgrn_gnorm, hgrn_lb, conv_w, conv_b]
    small_m = [m_c_ctx, m_ada_b, m_norm1, m_norm2, m_norm_f, m_hgrn_gnorm, m_hgrn_lb, m_conv_w, m_conv_b]
    small_v = [v_c_ctx, v_ada_b, v_norm1, v_norm2, v_norm_f, v_hgrn_gnorm, v_hgrn_lb, v_conv_w, v_conv_b]
    shapes = [w.shape for w in small_w]
    packed = adamw([_pack_rows(small_g)], _pack_rows(small_w), _pack_rows(small_m), _pack_rows(small_v), "adamw_small")
    s_g, s_d, s_m, s_v = [_unpack_rows(p, shapes) for p in packed]

    units = []
    for ks, sems, zones, srcs in started:
        units += [(key, sems[2 * u:2 * u + 2], zones[u], srcs[u]) for u, key in enumerate(ks)]
    late_keys = (("hin", 0), ("hout", 0))
    acc = {n: lax.empty(flat2(w).shape, F32) for n, (w, _, _) in tensors.items()}

    def finish_units(group, after, name):
        zones, srcs = split_wait([u[2] for u in group], [u[3] for u in group], [s for u in group for s in u[1]], after, name)
        for (key, _, _, _), zone, own in zip(group, zones, srcs):
            acc[key[0]] = sum_slots(own, zone, acc[key[0]], key[1], chip1, f"sum_{key[0]}_{key[1]}")

    finish_units([u for u in units if u[0] not in late_keys], fin, "grad_wait_early")
    early_names = ["w1", "w2", "cin", "cout"]
    late_names = ["hin", "hout"]
    sib_early = sibling_start([acc[n] for n in early_names], "sibling_start_early")
    finish_units([u for u in units if u[0] in late_keys], sib_early[0][-1], "grad_wait_late")
    sib_late = sibling_start([acc[n] for n in late_names], "sibling_start_late")
    results = {}

    def finish_tensors(names, sib, after, name):
        mine, other = sibling_wait(*sib, after, name)
        for n, pm, po in zip(names, mine, other):
            w, m, v = tensors[n]
            results[n] = [a.reshape(w.shape) for a in adamw([pm, po], flat2(w), flat2(m), flat2(v), f"adamw_{n}")]

    finish_tensors(early_names, sib_early, sib_late[0][-1], "sibling_wait_early")
    finish_tensors(late_names, sib_late, results["cout"][0], "sibling_wait_late")
    b_g, b_d, b_m, b_v = [[results[n][k] for n in big_names] for k in range(4)]

    def ordered(s, a, b):
        return [s[0], a, s[1], s[2], s[3], s[4], b[0], b[1], b[2], s[6], s[5], b[3], b[4], s[7], s[8], b[5]]

    return (loss, grad_x, *ordered(s_g, g_ada_w, b_g), *ordered(s_d, d_ada_w, b_d), *ordered(s_m, nm_ada_w, b_m),
            *ordered(s_v, nv_ada_w, b_v))
```

```python
import functools

import jax
import jax.numpy as jnp
from jax import lax
from jax.experimental import pallas as pl
from jax.experimental.pallas import tpu as pltpu

F32 = jnp.float32
BF16 = jnp.bfloat16
MESH = pl.DeviceIdType.MESH

D = 1024
HD = 128
NH = D // HD
CH = 64
TM = 256
TMW = 768
EPS = 1e-6
DEPTH = 4
VMEM_LIMIT = 56 * 1024 * 1024

ADAM_LR = 0.001
ADAM_B1 = 0.9
ADAM_B2 = 0.999
ADAM_EPS = 1e-08
ADAM_WD = 0.01
ADAM_STEP = 10


def _cp(n_grid):
    return pltpu.CompilerParams(dimension_semantics=("arbitrary",) * n_grid, vmem_limit_bytes=VMEM_LIMIT)


def _dot(a, b):
    return jnp.dot(a, b, preferred_element_type=F32)


def _dot_nt(a, b):
    return lax.dot_general(a, b, (((1,), (1,)), ((), ())), preferred_element_type=F32)


def _dot_tn(a, b):
    return lax.dot_general(a, b, (((0,), (0,)), ((), ())), preferred_element_type=F32)


def _sigmoid(z):
    return 1.0 / (1.0 + jnp.exp(-z))


def _norm_mod(x, gain, sh, sc):
    r = lax.rsqrt(jnp.mean(x * x, axis=-1, keepdims=True) + EPS)
    xn = x * r
    yn = xn * gain
    return r, xn, yn, yn * (1.0 + sc) + sh


def _row_spec(width):
    return pl.BlockSpec((TM, width), lambda i: (i, 0))


def _col_spec(col):
    return pl.BlockSpec((TM, D), lambda i: (i, col))


def _full_spec(shape):
    nd = len(shape)
    return pl.BlockSpec(shape, lambda i: (0,) * nd)


def _mod_spec(n_lat):
    return pl.BlockSpec((1, 8, D), lambda i: (i // n_lat, 0, 0))


def proj_fwd(x, gain, mod, m0, w4, n_lat, name):
    t = x.shape[0]
    nb, _, ns = w4.shape

    def body(x_ref, gain_ref, mod_ref, w_ref, p_ref):
        _, _, _, h = _norm_mod(x_ref[...], gain_ref[...], mod_ref[0, m0:m0 + 1, :], mod_ref[0, m0 + 1:m0 + 2, :])
        hb = h.astype(BF16)
        for c in range(nb):
            p_ref[:, c * ns:(c + 1) * ns] = _dot(hb, w_ref[c])

    return pl.pallas_call(
        body, name=name, grid=(t // TM,),
        in_specs=[_row_spec(D), _full_spec((1, D)), _mod_spec(n_lat), _full_spec(w4.shape)],
        out_specs=_row_spec(nb * ns),
        out_shape=jax.ShapeDtypeStruct((t, nb * ns), F32),
        compiler_params=_cp(1),
    )(x, gain, mod, w4)


def proj_bwd(parts, w4, x, gain, mod, m0, dx_in, n_lat, name):
    t = x.shape[0]
    nb, _, ns = w4.shape
    n = nb * ns
    n_parts = len(parts)
    widths = [p.shape[1] for p in parts]
    offs = [sum(widths[:k]) for k in range(n_parts)]
    assert sum(widths) == n
    single = n_parts == 1

    def body(*refs):
        part_refs = refs[:n_parts]
        w_ref, x_ref, gain_ref, mod_ref, dxin_ref = refs[n_parts:n_parts + 5]
        rest = refs[n_parts + 5:]
        if single:
            dx_ref, hb_ref, acc_ref = rest
            src = part_refs[0]
        else:
            dx_ref, hb_ref, acc_ref, dpb_ref = rest
            for p_ref, off, w in zip(part_refs, offs, widths):
                dpb_ref[:, off:off + w] = p_ref[...]
            src = dpb_ref
        i = pl.program_id(0)

        @pl.when(i == 0)
        def _():
            acc_ref[...] = jnp.zeros_like(acc_ref)

        gain = gain_ref[...]
        sc = mod_ref[0, m0 + 1:m0 + 2, :]
        r, xn, yn, h = _norm_mod(x_ref[...], gain, mod_ref[0, m0:m0 + 1, :], sc)
        hb_ref[...] = h.astype(BF16)
        dh = _dot_nt(src[:, 0:ns], w_ref[0])
        for c in range(1, nb):
            dh = dh + _dot_nt(src[:, c * ns:(c + 1) * ns], w_ref[c])
        dsh = jnp.sum(dh, axis=0, keepdims=True)
        dsc = jnp.sum(dh * yn, axis=0, keepdims=True)
        dyn = dh * (1.0 + sc)
        dgain = jnp.sum(dyn * xn, axis=0, keepdims=True)
        dxn = dyn * gain
        dx = r * (dxn - xn * jnp.mean(dxn * xn, axis=-1, keepdims=True))
        dx_ref[...] = dxin_ref[...] + dx
        latf = (i < n_lat).astype(F32)
        ctxf = 1.0 - latf
        acc_ref[0:1, :] += dgain
        acc_ref[1:2, :] += dsh * latf
        acc_ref[2:3, :] += dsc * latf
        acc_ref[3:4, :] += dsh * ctxf
        acc_ref[4:5, :] += dsc * ctxf

    out_specs = [_row_spec(D), _row_spec(D), _full_spec((8, D))]
    out_shape = [jax.ShapeDtypeStruct((t, D), F32), jax.ShapeDtypeStruct((t, D), BF16), jax.ShapeDtypeStruct((8, D), F32)]
    if not single:
        out_specs.append(_row_spec(n))
        out_shape.append(jax.ShapeDtypeStruct((t, n), BF16))
    outs = pl.pallas_call(
        body, name=name, grid=(t // TM,),
        in_specs=[_row_spec(w) for w in widths]
        + [_full_spec(w4.shape), _row_spec(D), _full_spec((1, D)), _mod_spec(n_lat), _row_spec(D)],
        out_specs=out_specs, out_shape=out_shape, compiler_params=_cp(1),
    )(*parts, w4, x, gain, mod, dx_in)
    if single:
        return outs[0], outs[1], parts[0], outs[2]
    return outs[0], outs[1], outs[3], outs[2]


def dw_tn(a, b, nb, a_blocked, name):
    t = a.shape[0]
    ka = a.shape[1] // nb if a_blocked else a.shape[1]
    kb = b.shape[1] if a_blocked else b.shape[1] // nb
    n_k = t // TMW

    def body(a_ref, b_ref, o_ref, acc):
        k = pl.program_id(1)

        @pl.when(k == 0)
        def _():
            acc[...] = jnp.zeros_like(acc)

        acc[...] += _dot_tn(a_ref[...], b_ref[...])

        @pl.when(k == n_k - 1)
        def _():
            o_ref[0] = acc[...].astype(BF16)

    a_spec = pl.BlockSpec((TMW, ka), (lambda j, k: (k, j)) if a_blocked else (lambda j, k: (k, 0)))
    b_spec = pl.BlockSpec((TMW, kb), (lambda j, k: (k, 0)) if a_blocked else (lambda j, k: (k, j)))
    return pl.pallas_call(
        body, name=name, grid=(nb, n_k),
        in_specs=[a_spec, b_spec],
        out_specs=pl.BlockSpec((1, ka, kb), lambda j, k: (j, 0, 0)),
        out_shape=jax.ShapeDtypeStruct((nb, ka, kb), BF16),
        scratch_shapes=[pltpu.VMEM((ka, kb), F32)],
        compiler_params=_cp(2),
    )(a, b)


def outproj_fwd(prologue, extras, extra_specs, w, x, mod, m0, n_lat, name):
    t = x.shape[0]
    k = w.shape[0]
    n_extra = len(extras)

    def body(*refs):
        ex = refs[:n_extra]
        w_ref, x_ref, mod_ref, xo_ref, y_ref, ab_ref = refs[n_extra:]
        ab = prologue(pl.program_id(0), *ex).astype(BF16)
        ab_ref[...] = ab
        y = _dot(ab, w_ref[...])
        y_ref[...] = y
        xo_ref[...] = x_ref[...] + mod_ref[0, m0 + 2:m0 + 3, :] * y

    return pl.pallas_call(
        body, name=name, grid=(t // TM,),
        in_specs=list(extra_specs) + [_full_spec(w.shape), _row_spec(D), _mod_spec(n_lat)],
        out_specs=[_row_spec(D), _row_spec(D), _row_spec(k)],
        out_shape=[jax.ShapeDtypeStruct((t, D), F32), jax.ShapeDtypeStruct((t, D), F32), jax.ShapeDtypeStruct((t, k), BF16)],
        compiler_params=_cp(1),
    )(*extras, w, x, mod)


def outproj_bwd(epilogue, extras, extra_specs, ep_out_specs, ep_out_shapes, w, dxn, y, mod, m0, n_lat, name):
    t = dxn.shape[0]
    n_extra = len(extras)

    def body(*refs):
        ex = refs[:n_extra]
        w_ref, dxn_ref, y_ref, mod_ref, dyb_ref, acc_ref = refs[n_extra:n_extra + 6]
        ep_outs = refs[n_extra + 6:]
        i = pl.program_id(0)

        @pl.when(i == 0)
        def _():
            acc_ref[...] = jnp.zeros_like(acc_ref)

        dxv = dxn_ref[...]
        dyb = (dxv * mod_ref[0, m0 + 2:m0 + 3, :]).astype(BF16)
        dyb_ref[...] = dyb
        dg = jnp.sum(dxv * y_ref[...], axis=0, keepdims=True)
        latf = (i < n_lat).astype(F32)
        acc_ref[0:1, :] += dg * latf
        acc_ref[1:2, :] += dg * (1.0 - latf)
        epilogue(i, _dot_nt(dyb, w_ref[...]), ex, ep_outs, acc_ref)

    outs = pl.pallas_call(
        body, name=name, grid=(t // TM,),
        in_specs=list(extra_specs) + [_full_spec(w.shape), _row_spec(D), _row_spec(D), _mod_spec(n_lat)],
        out_specs=[_row_spec(D), _full_spec((8, D))] + list(ep_out_specs),
        out_shape=[jax.ShapeDtypeStruct((t, D), BF16), jax.ShapeDtypeStruct((8, D), F32)] + list(ep_out_shapes),
        compiler_params=_cp(1),
    )(*extras, w, dxn, y, mod)
    return outs[0], outs[1], outs[2:]


def mlp_fwd(x, gain, mod, w1, w2, n_lat, name):
    t = x.shape[0]
    nb, _, ns = w1.shape

    def body(x_ref, gain_ref, mod_ref, w1_ref, w2_ref, xo_ref, y_ref, ab_ref):
        x = x_ref[...]
        _, _, _, h = _norm_mod(x, gain_ref[...], mod_ref[0, 3:4, :], mod_ref[0, 4:5, :])
        hb = h.astype(BF16)
        y = None
        for c in range(nb):
            a = jnp.square(jnp.maximum(_dot(hb, w1_ref[c]), 0.0)).astype(BF16)
            ab_ref[:, c * ns:(c + 1) * ns] = a
            yc = _dot(a, w2_ref[c * ns:(c + 1) * ns, :])
            y = yc if y is None else y + yc
        y_ref[...] = y
        xo_ref[...] = x + mod_ref[0, 5:6, :] * y

    return pl.pallas_call(
        body, name=name, grid=(t // TM,),
        in_specs=[_row_spec(D), _full_spec((1, D)), _mod_spec(n_lat), _full_spec(w1.shape), _full_spec(w2.shape)],
        out_specs=[_row_spec(D), _row_spec(D), _row_spec(nb * ns)],
        out_shape=[jax.ShapeDtypeStruct((t, D), F32), jax.ShapeDtypeStruct((t, D), F32), jax.ShapeDtypeStruct((t, nb * ns), BF16)],
        compiler_params=_cp(1),
    )(x, gain, mod, w1, w2)


def mlp_bwd(dxn, y, ab, x, gain, mod, w1, w2, n_lat, name):
    t = x.shape[0]
    nb, _, ns = w1.shape

    def body(dxn_ref, y_ref, ab_ref, x_ref, gain_ref, mod_ref, w1_ref, w2_ref, dx_ref, dyb_ref, dp_ref, hb_ref, acc_ref):
        i = pl.program_id(0)

        @pl.when(i == 0)
        def _():
            acc_ref[...] = jnp.zeros_like(acc_ref)

        dxv = dxn_ref[...]
        dyb = (dxv * mod_ref[0, 5:6, :]).astype(BF16)
        dyb_ref[...] = dyb
        dg = jnp.sum(dxv * y_ref[...], axis=0, keepdims=True)
        gain = gain_ref[...]
        sc = mod_ref[0, 4:5, :]
        r, xn, yn, h = _norm_mod(x_ref[...], gain, mod_ref[0, 3:4, :], sc)
        hb_ref[...] = h.astype(BF16)
        dh = None
        for c in range(nb):
            cols = slice(c * ns, (c + 1) * ns)
            da = _dot_nt(dyb, w2_ref[cols, :])
            dp = (da * (2.0 * jnp.sqrt(ab_ref[:, cols].astype(F32)))).astype(BF16)
            dp_ref[:, cols] = dp
            d = _dot_nt(dp, w1_ref[c])
            dh = d if dh is None else dh + d
        dsh = jnp.sum(dh, axis=0, keepdims=True)
        dsc = jnp.sum(dh * yn, axis=0, keepdims=True)
        dyn = dh * (1.0 + sc)
        dgain = jnp.sum(dyn * xn, axis=0, keepdims=True)
        dxn_ = dyn * gain
        dx_ref[...] = dxv + r * (dxn_ - xn * jnp.mean(dxn_ * xn, axis=-1, keepdims=True))
        latf = (i < n_lat).astype(F32)
        ctxf = 1.0 - latf
        acc_ref[0:1, :] += dgain
        acc_ref[1:2, :] += dsh * latf
        acc_ref[2:3, :] += dsc * latf
        acc_ref[3:4, :] += dsh * ctxf
        acc_ref[4:5, :] += dsc * ctxf
        acc_ref[5:6, :] += dg * latf
        acc_ref[6:7, :] += dg * ctxf

    return pl.pallas_call(
        body, name=name, grid=(t // TM,),
        in_specs=[_row_spec(D), _row_spec(D), _row_spec(nb * ns), _row_spec(D), _full_spec((1, D)), _mod_spec(n_lat),
                  _full_spec(w1.shape), _full_spec(w2.shape)],
        out_specs=[_row_spec(D), _row_spec(D), _row_spec(nb * ns), _row_spec(D), _full_spec((8, D))],
        out_shape=[jax.ShapeDtypeStruct((t, D), F32), jax.ShapeDtypeStruct((t, D), BF16), jax.ShapeDtypeStruct((t, nb * ns), BF16),
                   jax.ShapeDtypeStruct((t, D), BF16), jax.ShapeDtypeStruct((8, D), F32)],
        compiler_params=_cp(1),
    )(dxn, y, ab, x, gain, mod, w1, w2)


def readout_prologue(i, o0_ref, o1_ref, gate_ref, gn_ref):
    o = o0_ref[...] + o1_ref[...]
    gate = gate_ref[...]
    w = gn_ref[...] * (gate * _sigmoid(gate))
    pieces = []
    for h in range(NH):
        sl = slice(h * HD, (h + 1) * HD)
        oh = o[:, sl]
        pieces.append(oh * lax.rsqrt(jnp.mean(oh * oh, axis=-1, keepdims=True) + EPS) * w[:, sl])
    return jnp.concatenate(pieces, axis=1)


def readout_epilogue(i, da, ex, outs, acc_ref):
    o0_ref, o1_ref, gate_ref, gn_ref = ex
    do_ref, dgate_ref = outs
    o = o0_ref[...] + o1_ref[...]
    gate = gate_ref[...]
    gn = gn_ref[...]
    sg = _sigmoid(gate)
    silu = gate * sg
    dsilu = sg * (1.0 + gate * (1.0 - sg))
    for h in range(NH):
        sl = slice(h * HD, (h + 1) * HD)
        oh = o[:, sl]
        r = lax.rsqrt(jnp.mean(oh * oh, axis=-1, keepdims=True) + EPS)
        nh = oh * r
        dah = da[:, sl]
        acc_ref[2:3, sl] += jnp.sum(dah * nh * silu[:, sl], axis=0, keepdims=True)
        dgate_ref[:, sl] = (dah * nh * gn[:, sl] * dsilu[:, sl]).astype(BF16)
        dn = dah * gn[:, sl] * silu[:, sl]
        do_ref[:, sl] = r * (dn - nh * jnp.mean(dn * nh, axis=-1, keepdims=True))


def _seg_masks(i, n_lat):
    rows = lax.broadcasted_iota(jnp.int32, (TM, 1), 0)
    latf = (i < n_lat).astype(F32)
    ctxf = 1.0 - latf
    prev_ok = (rows % CH != 0).astype(F32) * latf + (rows != 0).astype(F32) * ctxf
    next_ok = (rows % CH != CH - 1).astype(F32) * latf + (rows != TM - 1).astype(F32) * ctxf
    return prev_ok, next_ok


def _shifts(i, n_lat, sft, cur, halo_prev, halo_next):
    if sft == 1:
        prev_ok, next_ok = _seg_masks(i, n_lat)
        return pltpu.roll(cur, 1, 0) * prev_ok, pltpu.roll(cur, TM - 1, 0) * next_ok
    has_prev = jnp.logical_and(i > 0, i < n_lat).astype(F32)
    has_next = (i < n_lat - 1).astype(F32)
    prev = jnp.concatenate([halo_prev * has_prev, cur[:TM - CH]], axis=0)
    nxt = jnp.concatenate([cur[CH:], halo_next * has_next], axis=0)
    return prev, nxt


def _conv_u(sft, ex):
    if sft == 1:
        gb_ref, gc_ref, xi_ref, cw_ref, cb_ref = ex
        return gb_ref, gc_ref[...] * xi_ref[...], None, None, cw_ref, cb_ref
    gb_ref, gc_ref, xi_ref, gcp_ref, xip_ref, gcn_ref, xin_ref, cw_ref, cb_ref = ex
    return gb_ref, gc_ref[...] * xi_ref[...], gcp_ref[...] * xip_ref[...], gcn_ref[...] * xin_ref[...], cw_ref, cb_ref


def _conv_value(i, n_lat, sft, ex):
    gb_ref, u, up, un, cw_ref, cb_ref = _conv_u(sft, ex)
    u_prev, u_next = _shifts(i, n_lat, sft, u, up, un)
    return gb_ref, cb_ref[...] + cw_ref[0:1, :] * u_prev + cw_ref[1:2, :] * u + cw_ref[2:3, :] * u_next


def make_conv_prologue(n_lat, sft):
    def prologue(i, *ex):
        gb_ref, conv = _conv_value(i, n_lat, sft, ex)
        return gb_ref[...] * conv
    return prologue


def make_conv_epilogue(n_lat, sft):
    def epilogue(i, da, ex, outs, acc_ref):
        gb_ref, conv = _conv_value(i, n_lat, sft, ex)
        outs[0][...] = da * gb_ref[...]
        outs[1][...] = (da * conv).astype(BF16)
    return epilogue


def _conv_specs(sft, t):
    specs = [_col_spec(0), _col_spec(1), _col_spec(2)]
    if sft != 1:
        per = TM // CH
        last = t // CH - 1
        for fn in (lambda i: jnp.maximum(i * per - 1, 0), lambda i: jnp.minimum(i * per + per, last)):
            for col in (1, 2):
                specs.append(pl.BlockSpec((CH, D), functools.partial(lambda i, f, c: (f(i), c), f=fn, c=col)))
    return specs + [_full_spec((8, D)), _full_spec((1, D))]


def _conv_args(sft, p, cw8, cb):
    return [p] * (3 if sft == 1 else 7) + [cw8, cb]


def conv_bwd(dconv, p, cw8, sft, n_lat, name):
    t = dconv.shape[0]
    halo = sft != 1

    def body(*refs):
        if halo:
            dc_ref, dcp_ref, dcn_ref, gc_ref, xi_ref, gcp_ref, xip_ref, gcn_ref, xin_ref, cw_ref, dgc_ref, dxi_ref, acc_ref = refs
            up, un = gcp_ref[...] * xip_ref[...], gcn_ref[...] * xin_ref[...]
            dcp, dcn = dcp_ref[...], dcn_ref[...]
        else:
            dc_ref, gc_ref, xi_ref, cw_ref, dgc_ref, dxi_ref, acc_ref = refs
            up = un = dcp = dcn = None
        i = pl.program_id(0)

        @pl.when(i == 0)
        def _():
            acc_ref[...] = jnp.zeros_like(acc_ref)

        gc = gc_ref[...]
        xi = xi_ref[...]
        u = gc * xi
        dc = dc_ref[...]
        u_prev, u_next = _shifts(i, n_lat, sft, u, up, un)
        dc_prev, dc_next = _shifts(i, n_lat, sft, dc, dcp, dcn)
        acc_ref[0:1, :] += jnp.sum(dc * u_prev, axis=0, keepdims=True)
        acc_ref[1:2, :] += jnp.sum(dc * u, axis=0, keepdims=True)
        acc_ref[2:3, :] += jnp.sum(dc * u_next, axis=0, keepdims=True)
        acc_ref[3:4, :] += jnp.sum(dc, axis=0, keepdims=True)
        du = cw_ref[0:1, :] * dc_next + cw_ref[1:2, :] * dc + cw_ref[2:3, :] * dc_prev
        dgc_ref[...] = (du * xi).astype(BF16)
        dxi_ref[...] = (du * gc).astype(BF16)

    per = TM // CH
    last = t // CH - 1
    prev_i = lambda i: jnp.maximum(i * per - 1, 0)
    next_i = lambda i: jnp.minimum(i * per + per, last)
    if halo:
        in_specs = [_row_spec(D), pl.BlockSpec((CH, D), lambda i: (prev_i(i), 0)), pl.BlockSpec((CH, D), lambda i: (next_i(i), 0)),
                    _col_spec(1), _col_spec(2),
                    pl.BlockSpec((CH, D), lambda i: (prev_i(i), 1)), pl.BlockSpec((CH, D), lambda i: (prev_i(i), 2)),
                    pl.BlockSpec((CH, D), lambda i: (next_i(i), 1)), pl.BlockSpec((CH, D), lambda i: (next_i(i), 2)),
                    _full_spec((8, D))]
        args = [dconv, dconv, dconv, p, p, p, p, p, p, cw8]
    else:
        in_specs = [_row_spec(D), _col_spec(1), _col_spec(2), _full_spec((8, D))]
        args = [dconv, p, p, cw8]
    return pl.pallas_call(
        body, name=name, grid=(t // TM,), in_specs=in_specs,
        out_specs=[_row_spec(D), _row_spec(D), _full_spec((8, D))],
        out_shape=[jax.ShapeDtypeStruct((t, D), BF16), jax.ShapeDtypeStruct((t, D), BF16), jax.ShapeDtypeStruct((8, D), F32)],
        compiler_params=_cp(1),
    )(*args)


LOG2E = 1.4426950408889634


def _cumsum_matrix(reverse):
    r = lax.broadcasted_iota(jnp.int32, (CH, CH), 0)
    c = lax.broadcasted_iota(jnp.int32, (CH, CH), 1)
    return (r <= c if reverse else r >= c).astype(BF16)


def _chunk_cumsum(g, tri):
    hi = g.astype(BF16)
    lo = (g - hi.astype(F32)).astype(BF16)
    return _dot(tri, hi) + _dot(tri, lo)


def _gate_values(z, lb):
    sig = _sigmoid(z)
    f = lb + (1.0 - lb) * sig
    return sig, f


def _tri(direction, transposed):
    r = lax.broadcasted_iota(jnp.int32, (CH, CH), 0)
    c = lax.broadcasted_iota(jnp.int32, (CH, CH), 1)
    lower = (direction == 0) != transposed
    return r >= c if lower else r <= c


def _gla_rows(direction):
    return (CH // 2 - 1, CH - 1) if direction == 0 else (CH // 2, 0)


def gla_fwd(p, lb2, direction, name):
    t = p.shape[0]
    nt = t // TM
    per = TM // CH
    ref_row, last_row = _gla_rows(direction)
    tile = (lambda i: (i + nt - 1) % nt) if direction == 0 else (lambda i: nt - 1 - i)

    def body(z_ref, v_ref, qr_ref, lb_ref, o_ref, s_ref, st, q_s, k_s, c_s):
        @pl.when(pl.program_id(0) == 0)
        def _():
            st[...] = jnp.zeros_like(st)

        mask = _tri(direction, False)
        tri = _cumsum_matrix(direction == 1)
        lb = lb_ref[direction:direction + 1, :]
        for ci in range(per):
            rows = slice(ci * CH, (ci + 1) * CH)
            _, f = _gate_values(z_ref[rows, :], lb)
            k_s[rows, :] = 1.0 - f
            c_s[rows, :] = _chunk_cumsum(jnp.log(f) * LOG2E, tri)
            qr = qr_ref[rows, :]
            q_s[rows, :] = qr * _sigmoid(qr)
        for it in range(per):
            ci = it if direction == 0 else per - 1 - it
            r0 = ci * CH
            rows = slice(r0, r0 + CH)
            cum = c_s[rows, :]
            ref = c_s[r0 + ref_row:r0 + ref_row + 1, :]
            last = c_s[r0 + last_row:r0 + last_row + 1, :]
            q = q_s[rows, :]
            k = k_s[rows, :]
            qh = (q * jnp.exp2(cum)).astype(BF16)
            qt = (q * jnp.exp2(cum - ref)).astype(BF16)
            kt = (k * jnp.exp2(ref - cum)).astype(BF16)
            kb = (k * jnp.exp2(last - cum)).astype(BF16)
            el = jnp.exp2(last)
            vb = v_ref[rows, :].astype(BF16)
            for h in range(NH):
                sl = slice(h * HD, (h + 1) * HD)
                s_t = st[h]
                s_ref[ci, h] = s_t
                sc = jnp.where(mask, _dot_nt(qt[:, sl], kt[:, sl]), 0.0)
                o_ref[rows, sl] = _dot_nt(qh[:, sl], s_t.astype(BF16)) + _dot(sc.astype(BF16), vb[:, sl])
                st[h] = s_t * el[:, sl] + _dot_tn(vb[:, sl], kb[:, sl])

    tspec = lambda col: pl.BlockSpec((TM, D), lambda i: (tile(i), col))
    return pl.pallas_call(
        body, name=name, grid=(nt,),
        in_specs=[tspec(direction), tspec(2), tspec(3), _full_spec((2, D))],
        out_specs=[pl.BlockSpec((TM, D), lambda i: (tile(i), 0)), pl.BlockSpec((per, NH, HD, HD), lambda i: (tile(i), 0, 0, 0))],
        out_shape=[jax.ShapeDtypeStruct((t, D), F32), jax.ShapeDtypeStruct((t // CH, NH, HD, HD), F32)],
        scratch_shapes=[pltpu.VMEM((NH, HD, HD), F32)] + [pltpu.VMEM((TM, D), F32)] * 3,
        compiler_params=_cp(1),
    )(p, p, p, lb2)


def gla_bwd(p, lb2, do, states, direction, prev, name):
    t = p.shape[0]
    nt = t // TM
    per = TM // CH
    ref_row, last_row = _gla_rows(direction)
    tile = (lambda i: (2 * nt - 2 - i) % nt) if direction == 0 else (lambda i: i)
    final = prev is not None
    n_in = 8 if final else 6

    def body(*refs):
        z_ref, v_ref, qr_ref, lb_ref, do_ref, s_ref = refs[:6]
        dz_ref, dv_ref, dq_ref, acc_ref, dst, q_s, k_s, c_s, dq_s, dk_s, dl_s = refs[n_in:]

        @pl.when(pl.program_id(0) == 0)
        def _():
            dst[...] = jnp.zeros_like(dst)
            acc_ref[...] = jnp.zeros_like(acc_ref)

        mask = _tri(direction, False)
        mask_t = _tri(direction, True)
        tri = _cumsum_matrix(direction == 1)
        tri_t = _cumsum_matrix(direction == 0)
        is_last = lax.broadcasted_iota(jnp.int32, (CH, 1), 0) == last_row
        lb = lb_ref[direction:direction + 1, :]
        for ci in range(per):
            rows = slice(ci * CH, (ci + 1) * CH)
            _, f = _gate_values(z_ref[rows, :], lb)
            k_s[rows, :] = 1.0 - f
            c_s[rows, :] = _chunk_cumsum(jnp.log(f) * LOG2E, tri)
            qr = qr_ref[rows, :]
            q_s[rows, :] = qr * _sigmoid(qr)
        for it in range(per):
            ci = per - 1 - it if direction == 0 else it
            r0 = ci * CH
            rows = slice(r0, r0 + CH)
            cum = c_s[rows, :]
            ref = c_s[r0 + ref_row:r0 + ref_row + 1, :]
            last = c_s[r0 + last_row:r0 + last_row + 1, :]
            q = q_s[rows, :]
            k = k_s[rows, :]
            e_h = jnp.exp2(cum)
            e_t = jnp.exp2(cum - ref)
            e_kt = jnp.exp2(ref - cum)
            e_kb = jnp.exp2(last - cum)
            el = jnp.exp2(last)
            qh = (q * e_h).astype(BF16)
            qt = (q * e_t).astype(BF16)
            kt = (k * e_kt).astype(BF16)
            kbf = k * e_kb
            kb = kbf.astype(BF16)
            vb = v_ref[rows, :].astype(BF16)
            dob = do_ref[rows, :].astype(BF16)
            for h in range(NH):
                sl = slice(h * HD, (h + 1) * HD)
                s_t = s_ref[ci, h]
                ds_t = dst[h]
                ds_b = ds_t.astype(BF16)
                d_a = jnp.where(mask, _dot_nt(dob[:, sl], vb[:, sl]), 0.0).astype(BF16)
                a_t = jnp.where(mask_t, _dot_nt(kt[:, sl], qt[:, sl]), 0.0).astype(BF16)
                d_at = jnp.where(mask_t, _dot_nt(vb[:, sl], dob[:, sl]), 0.0).astype(BF16)
                dv = _dot(a_t, dob[:, sl]) + _dot_nt(kb[:, sl], ds_b)
                dkb = _dot(vb[:, sl], ds_b)
                dl_s[it:it + 1, sl] = (el[:, sl] * jnp.sum(ds_t * s_t, axis=0, keepdims=True)
                                       + jnp.sum(dkb * kbf[:, sl], axis=0, keepdims=True))
                dst[h] = ds_t * el[:, sl] + _dot_tn(dob[:, sl], qh[:, sl])
                dq_s[rows, sl] = _dot(dob[:, sl], s_t.astype(BF16)) * e_h[:, sl] + _dot(d_a, kt[:, sl]) * e_t[:, sl]
                dk_s[rows, sl] = _dot(d_at, qt[:, sl]) * e_kt[:, sl] + dkb * e_kb[:, sl]
                if final:
                    dv_ref[rows, sl] = (refs[6][rows, sl] + dv).astype(BF16)
                else:
                    dv_ref[rows, sl] = dv
        for it in range(per):
            ci = per - 1 - it if direction == 0 else it
            rows = slice(ci * CH, (ci + 1) * CH)
            dq = dq_s[rows, :]
            dk = dk_s[rows, :]
            dg = _chunk_cumsum(dq * q_s[rows, :] - dk * k_s[rows, :] + jnp.where(is_last, dl_s[it:it + 1, :], 0.0), tri_t)
            sig, f = _gate_values(z_ref[rows, :], lb)
            df = dg / f - dk
            acc_ref[0:1, :] += jnp.sum(df * (1.0 - sig), axis=0, keepdims=True)
            dz_ref[rows, :] = (df * (1.0 - lb) * sig * (1.0 - sig)).astype(BF16)
            if final:
                qr = qr_ref[rows, :]
                sq = _sigmoid(qr)
                dq_ref[rows, :] = ((refs[7][rows, :] + dq) * (sq * (1.0 + qr * (1.0 - sq)))).astype(BF16)
            else:
                dq_ref[rows, :] = dq

    tspec = lambda col: pl.BlockSpec((TM, D), lambda i: (tile(i), col))
    sspec = pl.BlockSpec((per, NH, HD, HD), lambda i: (tile(i), 0, 0, 0))
    in_specs = [tspec(direction), tspec(2), tspec(3), _full_spec((2, D)), tspec(0), sspec]
    args = [p, p, p, lb2, do, states]
    if final:
        in_specs += [tspec(0), tspec(0)]
        args += list(prev)
    odt = BF16 if final else F32
    return pl.pallas_call(
        body, name=name, grid=(nt,), in_specs=in_specs,
        out_specs=[tspec(0), tspec(0), tspec(0), _full_spec((8, D))],
        out_shape=[jax.ShapeDtypeStruct((t, D), BF16), jax.ShapeDtypeStruct((t, D), odt), jax.ShapeDtypeStruct((t, D), odt),
                   jax.ShapeDtypeStruct((8, D), F32)],
        scratch_shapes=[pltpu.VMEM((NH, HD, HD), F32)] + [pltpu.VMEM((TM, D), F32)] * 5 + [pltpu.VMEM((8, D), F32)],
        compiler_params=_cp(1),
    )(*args)


def loss_bwd(x, gain, target, n_lat, name):
    t = x.shape[0]

    def body(x_ref, gain_ref, tg_ref, dx_ref, acc_ref):
        i = pl.program_id(0)

        @pl.when(i == 0)
        def _():
            acc_ref[...] = jnp.zeros_like(acc_ref)

        latf = (i < n_lat).astype(F32)
        x = x_ref[...]
        gain = gain_ref[...]
        r = lax.rsqrt(jnp.mean(x * x, axis=-1, keepdims=True) + EPS)
        xn = x * r
        err = (xn * gain - tg_ref[...]) * latf
        dy = err * (1.0 / D)
        dxn = dy * gain
        dx_ref[...] = r * (dxn - xn * jnp.mean(dxn * xn, axis=-1, keepdims=True))
        acc_ref[0:1, :] += jnp.sum(dy * xn, axis=0, keepdims=True)
        acc_ref[1:2, :] += jnp.sum(err * err, axis=0, keepdims=True)

    return pl.pallas_call(
        body, name=name, grid=(t // TM,),
        in_specs=[_row_spec(D), _full_spec((1, D)), pl.BlockSpec((TM, D), lambda i: (jnp.minimum(i, n_lat - 1), 0))],
        out_specs=[_row_spec(D), _full_spec((8, D))],
        out_shape=[jax.ShapeDtypeStruct((t, D), F32), jax.ShapeDtypeStruct((8, D), F32)],
        compiler_params=_cp(1),
    )(x, gain, target)


def local_step(xs, target, mods, norm1, norm2, norm_f, lbs, gnorm, cw8, cb, wts, n_lat, on_grads):
    t = xs.shape[0]
    saved = []
    cache = {}

    def W(name, idx, after=None):
        if (name, idx) not in cache:
            cache[(name, idx)] = wts(name, idx, after)
        return cache[(name, idx)]

    x = xs
    for i in range(DEPTH):
        j = i // 2
        rec = i % 2 == 0
        n1 = norm1[i:i + 1]
        n2 = norm2[i:i + 1]
        s = {"x_in": x}
        if rec:
            p = proj_fwd(x, n1, mods[i], 0, W("hin", j, x), n_lat, f"hin_fwd_{i}")
            o0, st0 = gla_fwd(p, lbs[j], 0, f"gla_fwd0_{i}")
            o1, st1 = gla_fwd(p, lbs[j], 1, f"gla_fwd1_{i}")
            ex = [o0, o1, p, gnorm[j:j + 1]]
            ex_specs = [_row_spec(D), _row_spec(D), _col_spec(4), _full_spec((1, D))]
            xm, y, ab = outproj_fwd(readout_prologue, ex, ex_specs, W("hout", j, o1), x, mods[i], 0, n_lat, f"hout_fwd_{i}")
            s.update(st0=st0, st1=st1)
        else:
            sft = 1 if j % 2 == 0 else CH
            p = proj_fwd(x, n1, mods[i], 0, W("cin", j, x), n_lat, f"cin_fwd_{i}")
            ex = _conv_args(sft, p, cw8[j], cb[j])
            ex_specs = _conv_specs(sft, t)
            xm, y, ab = outproj_fwd(make_conv_prologue(n_lat, sft), ex, ex_specs, W("cout", j, p), x, mods[i], 0, n_lat, f"cout_fwd_{i}")
        s.update(p=p, ex=ex, ex_specs=ex_specs, y_mix=y, ab_mix=ab, x_mid=xm)
        x, y2, ab2 = mlp_fwd(xm, n2, mods[i], W("w1", i, xm), W("w2", i, xm), n_lat, f"mlp_fwd_{i}")
        s.update(y_mlp=y2, ab_mlp=ab2)
        saved.append(s)

    dx, acc_loss = loss_bwd(x, norm_f, target, n_lat, "loss")
    small = {"norm_f": acc_loss[0:1], "norm1": [None] * DEPTH, "norm2": [None] * DEPTH, "dmod": [None] * DEPTH,
             "gnorm": [None] * 2, "lb": [None] * 2, "cw": [None] * 2, "cb": [None] * 2}
    bshape = lambda w: jax.ShapeDtypeStruct((t, w), BF16)
    for i in reversed(range(DEPTH)):
        j = i // 2
        rec = i % 2 == 0
        s = saved[i]
        n1 = norm1[i:i + 1]
        n2 = norm2[i:i + 1]
        dx, dyb, dp1, hb, acc_n2 = mlp_bwd(dx, s["y_mlp"], s["ab_mlp"], s["x_mid"], n2, mods[i], W("w1", i), W("w2", i), n_lat, f"mlp_bwd_{i}")
        on_grads(i, "mlp", {"w2": dw_tn(s["ab_mlp"], dyb, 4, True, f"w2_dw_{i}"), "w1": dw_tn(hb, dp1, 4, False, f"w1_dw_{i}")})
        g = {}
        if rec:
            dyb, acc_g1, (do, dgate) = outproj_bwd(
                readout_epilogue, s["ex"], s["ex_specs"], [_row_spec(D), _row_spec(D)],
                [jax.ShapeDtypeStruct((t, D), F32), bshape(D)], W("hout", j), dx, s["y_mix"], mods[i], 0, n_lat, f"hout_bwd_{i}")
            g["hout"] = dw_tn(s["ab_mix"], dyb, 1, False, f"hout_dw_{i}")
            dz0, dv0, dq0, acc_l0 = gla_bwd(s["p"], lbs[j], do, s["st0"], 0, None, f"gla_bwd0_{i}")
            dz1, dv, dq, acc_l1 = gla_bwd(s["p"], lbs[j], do, s["st1"], 1, (dv0, dq0), f"gla_bwd1_{i}")
            dx, hb, dpb, acc_n1 = proj_bwd([dz0, dz1, dv, dq, dgate], W("hin", j), s["x_in"], n1, mods[i], 0, dx, n_lat, f"hin_bwd_{i}")
            g["hin"] = dw_tn(hb, dpb, 4, False, f"hin_dw_{i}")
            small["gnorm"][j] = acc_g1[2:3]
            small["lb"][j] = jnp.concatenate([acc_l0[0:1], acc_l1[0:1]], axis=0)
        else:
            sft = 1 if j % 2 == 0 else CH
            dyb, acc_g1, (dconv, dgb) = outproj_bwd(
                make_conv_epilogue(n_lat, sft), s["ex"], s["ex_specs"], [_row_spec(D), _row_spec(D)],
                [jax.ShapeDtypeStruct((t, D), F32), bshape(D)], W("cout", j), dx, s["y_mix"], mods[i], 0, n_lat, f"cout_bwd_{i}")
            g["cout"] = dw_tn(s["ab_mix"], dyb, 1, False, f"cout_dw_{i}")
            dgc, dxi, acc_c = conv_bwd(dconv, s["p"], cw8[j], sft, n_lat, f"conv_bwd_{i}")
            dx, hb, dpb, acc_n1 = proj_bwd([dgb, dgc, dxi], W("cin", j), s["x_in"], n1, mods[i], 0, dx, n_lat, f"cin_bwd_{i}")
            g["cin"] = dw_tn(hb, dpb, 4, False, f"cin_dw_{i}")
            small["cw"][j] = acc_c[0:3]
            small["cb"][j] = acc_c[3:4]
        small["norm1"][i] = acc_n1[0:1]
        small["norm2"][i] = acc_n2[0:1]
        z2 = jnp.zeros((2, D), F32)
        small["dmod"][i] = jnp.concatenate([acc_n1[1:3], acc_g1[0:1], acc_n2[1:3], acc_n2[5:6], z2,
                                            acc_n1[3:5], acc_g1[1:2], acc_n2[3:5], acc_n2[6:7], z2], axis=0)
        on_grads(i, "mix", g)
    return acc_loss[1:2], dx, small


RB = 256


def cast_to_slot(w2d, layer, k, chip1, name):
    c = w2d.shape[1]
    nblk = k // RB

    def body(chip_ref, w_ref, o_ref):
        o_ref[0] = w_ref[...].astype(BF16)

    return pl.pallas_call(
        body, name=name,
        grid_spec=pltpu.PrefetchScalarGridSpec(
            num_scalar_prefetch=1, grid=(nblk,),
            in_specs=[pl.BlockSpec((RB, c), lambda i, ch: (layer * nblk + i, 0))],
            out_specs=pl.BlockSpec((1, RB, c), lambda i, ch: (ch[0], i, 0))),
        out_shape=jax.ShapeDtypeStruct((4, k, c), BF16), compiler_params=_cp(1))(chip1, w2d)


def sum_slots(own, land, acc, layer, chip1, name):
    _, k, c = own.shape
    nblk = k // RB

    def body(chip_ref, own_ref, l1_ref, l2_ref, l3_ref, acc_ref, o_ref):
        o_ref[...] = ((own_ref[0].astype(F32) + l1_ref[0].astype(F32)) + l2_ref[0].astype(F32)) + l3_ref[0].astype(F32)

    slot = lambda d: pl.BlockSpec((1, RB, c), lambda i, ch: ((ch[0] + d) % 4, i, 0))
    return pl.pallas_call(
        body, name=name,
        grid_spec=pltpu.PrefetchScalarGridSpec(
            num_scalar_prefetch=1, grid=(nblk,),
            in_specs=[slot(0), slot(1), slot(2), slot(3), ANY],
            out_specs=pl.BlockSpec((RB, c), lambda i, ch: (layer * nblk + i, 0))),
        out_shape=jax.ShapeDtypeStruct(acc.shape, F32), input_output_aliases={5: 0}, compiler_params=_cp(1),
    )(chip1, own, land, land, land, acc)


def _adamw_math(w, g, m, v):
    m = ADAM_B1 * m + (1.0 - ADAM_B1) * g
    v = ADAM_B2 * v + (1.0 - ADAM_B2) * jnp.square(g)
    m_hat = m / (1.0 - ADAM_B1 ** ADAM_STEP)
    v_hat = v / (1.0 - ADAM_B2 ** ADAM_STEP)
    delta = -ADAM_LR * (m_hat / (jnp.sqrt(v_hat) + ADAM_EPS) + ADAM_WD * w)
    return delta, m, v


def adamw(gsrcs, w, m, v, name):
    r, c = w.shape
    rb = RB if r % RB == 0 else r
    n_g = len(gsrcs)

    def body(*refs):
        g = refs[0][...]
        for k in range(1, n_g):
            g = g + refs[k][...]
        w_ref, m_ref, v_ref, g_ref, d_ref, mo_ref, vo_ref = refs[n_g:]
        delta, mo, vo = _adamw_math(w_ref[...], g, m_ref[...], v_ref[...])
        g_ref[...] = g
        d_ref[...] = delta
        mo_ref[...] = mo
        vo_ref[...] = vo

    spec = pl.BlockSpec((rb, c), lambda i: (i, 0))
    shp = jax.ShapeDtypeStruct((r, c), F32)
    return pl.pallas_call(body, name=name, grid=(r // rb,), in_specs=[spec] * (n_g + 3), out_specs=[spec] * 4,
                          out_shape=[shp] * 4, compiler_params=_cp(1))(*gsrcs, w, m, v)


ADA_CB = 512


def ada_fwd(cvec, ada_w, bias, name):
    _, _, nc = ada_w.shape

    def body(c_ref, w_ref, b_ref, o_ref):
        cv = c_ref[...]
        a = (cv * _sigmoid(cv)).astype(BF16)
        o_ref[0] = _dot(a, w_ref[0].astype(BF16)) + b_ref[0]

    return pl.pallas_call(
        body, name=name, grid=(DEPTH, nc // ADA_CB),
        in_specs=[pl.BlockSpec((16, D), lambda i, j: (0, 0)), pl.BlockSpec((1, D, ADA_CB), lambda i, j: (i, 0, j)),
                  pl.BlockSpec((1, 1, ADA_CB), lambda i, j: (i, 0, j))],
        out_specs=pl.BlockSpec((1, 16, ADA_CB), lambda i, j: (i, 0, j)),
        out_shape=jax.ShapeDtypeStruct((DEPTH, 16, nc), F32), compiler_params=_cp(2),
    )(cvec, ada_w, bias)


def ada_bwd(cvec, dcols, ada_w, m, v, name):
    _, _, nc = ada_w.shape

    def body(c_ref, d_ref, w_ref, m_ref, v_ref, g_ref, dl_ref, mo_ref, vo_ref, acc_ref):
        @pl.when(jnp.logical_and(pl.program_id(0) == 0, pl.program_id(1) == 0))
        def _():
            acc_ref[...] = jnp.zeros_like(acc_ref)

        cv = c_ref[...]
        a = (cv * _sigmoid(cv)).astype(BF16)
        db = d_ref[0].astype(BF16)
        w = w_ref[0]
        g = _dot_tn(a, db)
        delta, mo, vo = _adamw_math(w, g, m_ref[0], v_ref[0])
        g_ref[0] = g
        dl_ref[0] = delta
        mo_ref[0] = mo
        vo_ref[0] = vo
        acc_ref[...] += _dot_nt(db[8:16, :], w.astype(BF16))

    wspec = pl.BlockSpec((1, D, ADA_CB), lambda i, j: (i, 0, j))
    wshape = jax.ShapeDtypeStruct(ada_w.shape, F32)
    return pl.pallas_call(
        body, name=name, grid=(DEPTH, nc // ADA_CB),
        in_specs=[pl.BlockSpec((16, D), lambda i, j: (0, 0)), pl.BlockSpec((1, 16, ADA_CB), lambda i, j: (i, 0, j)), wspec, wspec, wspec],
        out_specs=[wspec, wspec, wspec, wspec, pl.BlockSpec((8, D), lambda i, j: (0, 0))],
        out_shape=[wshape, wshape, wshape, wshape, jax.ShapeDtypeStruct((8, D), F32)], compiler_params=_cp(2),
    )(cvec, dcols, ada_w, m, v)


def _place():
    return lax.axis_index("x"), lax.axis_index("y"), lax.axis_index("c")


ANY = pl.BlockSpec(memory_space=pl.ANY)
VMEM_SPEC = pl.BlockSpec(memory_space=pltpu.VMEM)


def small_allgather(buf, name):
    r, c = buf.shape

    def body(in_ref, out_ref, send_sems, recv_sems, loc_sem):
        x, y, cc = _place()
        me = 4 * x + 2 * y + cc
        loc = pltpu.make_async_copy(in_ref, out_ref.at[me], loc_sem)
        loc.start()
        peers = []
        for k in range(1, 8):
            px = 1 - x if k & 4 else x
            py = 1 - y if k & 2 else y
            pc = 1 - cc if k & 1 else cc
            peers.append((px, py, pc))
        sends = []
        for k, peer in enumerate(peers):
            cp = pltpu.make_async_remote_copy(src_ref=in_ref, dst_ref=out_ref.at[me], send_sem=send_sems.at[k],
                                              recv_sem=recv_sems.at[k], device_id=peer, device_id_type=MESH)
            cp.start()
            sends.append(cp)
        for k, (px, py, pc) in enumerate(peers):
            pltpu.make_async_remote_copy(src_ref=in_ref, dst_ref=out_ref.at[4 * px + 2 * py + pc], send_sem=send_sems.at[k],
                                         recv_sem=recv_sems.at[k], device_id=(px, py, pc), device_id_type=MESH).wait_recv()
        for cp in sends:
            cp.wait_send()
        loc.wait()

    return pl.pallas_call(
        body, name=name, in_specs=[VMEM_SPEC], out_specs=VMEM_SPEC,
        out_shape=jax.ShapeDtypeStruct((8, r, c), buf.dtype),
        scratch_shapes=[pltpu.SemaphoreType.DMA((7,)), pltpu.SemaphoreType.DMA((7,)), pltpu.SemaphoreType.DMA],
    )(buf)


def _chip_peers(x, y):
    return [(1 - x, y), (x, 1 - y), (1 - x, 1 - y)]


HBM_SPEC = pl.BlockSpec(memory_space=pltpu.HBM)
SEM_SPEC = pl.BlockSpec(memory_space=pltpu.SEMAPHORE)
EFFECT = pltpu.SideEffectType.DATAFLOW_SIDE_EFFECTING


def _hbm(a):
    return pltpu.with_memory_space_constraint(a, pltpu.HBM)


def _split_copy(u, p, peer, dst_slot, chip, land_refs, src_refs, sem_refs, cc):
    px, py = peer
    src = land_refs[u].at[chip] if src_refs is None else src_refs[u].at[2 * px + py]
    return pltpu.make_async_remote_copy(src_ref=src, dst_ref=land_refs[u].at[dst_slot], send_sem=sem_refs[2 * u].at[p],
                                        recv_sem=sem_refs[2 * u + 1].at[p], device_id=(px, py, cc), device_id_type=MESH)


def split_start(lands, srcs, name):
    n = len(lands)
    ops = list(lands) + (list(srcs) if srcs is not None else [])
    n_ops = len(ops)

    def body(*refs):
        land_refs = refs[:n]
        src_refs = refs[n:n_ops] if srcs is not None else None
        sem_refs = refs[n_ops:n_ops + 2 * n]
        x, y, cc = _place()
        chip = 2 * x + y
        for u in range(n):
            for p, peer in enumerate(_chip_peers(x, y)):
                _split_copy(u, p, peer, chip, chip, land_refs, src_refs, sem_refs, cc).start()

    outs = pl.pallas_call(
        body, name=name, in_specs=[HBM_SPEC] * n_ops,
        out_specs=[SEM_SPEC] * (2 * n) + [HBM_SPEC] * n_ops,
        out_shape=[pltpu.SemaphoreType.DMA((3,))] * (2 * n) + [pltpu.HBM(a.shape, a.dtype) for a in ops],
        input_output_aliases={k: 2 * n + k for k in range(n_ops)},
        compiler_params=pltpu.CompilerParams(has_side_effects=EFFECT),
    )(*[_hbm(a) for a in ops])
    sems = list(outs[:2 * n])
    thru = list(outs[2 * n:])
    return sems, thru[:n], thru[n:]


def split_wait(lands, srcs, sems, after, name):
    n = len(lands)
    ops = list(lands) + (list(srcs) if srcs is not None else [])
    n_ops = len(ops)

    def body(*refs):
        land_refs = refs[:n]
        src_refs = refs[n:n_ops] if srcs is not None else None
        sem_refs = refs[n_ops:n_ops + 2 * n]
        x, y, cc = _place()
        chip = 2 * x + y
        for u in range(n):
            for p, peer in enumerate(_chip_peers(x, y)):
                cp = _split_copy(u, p, peer, 2 * peer[0] + peer[1], chip, land_refs, src_refs, sem_refs, cc)
                cp.wait_send()
                cp.wait_recv()

    outs = pl.pallas_call(
        body, name=name, in_specs=[HBM_SPEC] * n_ops + [SEM_SPEC] * (2 * n) + [ANY],
        out_specs=[HBM_SPEC] * n_ops, out_shape=[pltpu.HBM(a.shape, a.dtype) for a in ops],
        input_output_aliases={k: k for k in range(n_ops)},
        compiler_params=pltpu.CompilerParams(has_side_effects=EFFECT),
    )(*ops, *sems, after)
    return list(outs[:n]), list(outs[n:])


def _sibling_copy(k, src_refs, zone_refs, sem_refs):
    x, y, cc = _place()
    return pltpu.make_async_remote_copy(src_ref=src_refs[k], dst_ref=zone_refs[k], send_sem=sem_refs[2 * k], recv_sem=sem_refs[2 * k + 1],
                                        device_id=(x, y, 1 - cc), device_id_type=MESH)


def sibling_start(parts, name):
    n = len(parts)
    ops = list(parts) + [lax.empty(p.shape, p.dtype) for p in parts]

    def body(*refs):
        for k in range(n):
            _sibling_copy(k, refs[:n], refs[n:2 * n], refs[2 * n:4 * n]).start()

    outs = pl.pallas_call(
        body, name=name, in_specs=[HBM_SPEC] * (2 * n),
        out_specs=[SEM_SPEC] * (2 * n) + [HBM_SPEC] * (2 * n),
        out_shape=[pltpu.SemaphoreType.DMA(())] * (2 * n) + [pltpu.HBM(a.shape, a.dtype) for a in ops],
        input_output_aliases={k: 2 * n + k for k in range(2 * n)},
        compiler_params=pltpu.CompilerParams(has_side_effects=EFFECT),
    )(*[_hbm(a) for a in ops])
    return list(outs[2 * n:3 * n]), list(outs[3 * n:]), list(outs[:2 * n])


def sibling_wait(parts, zones, sems, after, name):
    n = len(parts)

    def body(*refs):
        for k in range(n):
            cp = _sibling_copy(k, refs[:n], refs[n:2 * n], refs[2 * n:4 * n])
            cp.wait_send()
            cp.wait_recv()

    outs = pl.pallas_call(
        body, name=name, in_specs=[HBM_SPEC] * (2 * n) + [SEM_SPEC] * (2 * n) + [ANY],
        out_specs=[HBM_SPEC] * (2 * n), out_shape=[pltpu.HBM(a.shape, a.dtype) for a in list(parts) + list(zones)],
        input_output_aliases={k: k for k in range(2 * n)},
        compiler_params=pltpu.CompilerParams(has_side_effects=EFFECT),
    )(*parts, *zones, *sems, after)
    return list(outs[:n]), list(outs[n:])


SMALL_ROWS = 88
FIN_ROWS = 72


def small_finish(g3, g4, c_ctx, lbp, name):
    def body(g3_ref, g4_ref, cc_ref, lbp_ref, o_ref, s_ref):
        s = g3_ref[0]
        for k in range(1, 8):
            s = s + g3_ref[k]
        s_ref[...] = s
        for i in range(DEPTH):
            o_ref[8 * i:8 * i + 8, :] = s_ref[16 * i:16 * i + 8, :] + s_ref[16 * i + 8:16 * i + 16, :]
        acc = g4_ref[0]
        for k in (2, 4, 6):
            acc = acc + g4_ref[k]
        cc = cc_ref[...]
        sg = _sigmoid(cc)
        row = jnp.sum(acc, axis=0, keepdims=True) * (sg * (1.0 + cc * (1.0 - sg)))
        o_ref[32:40, :] = jnp.broadcast_to(row, (8, D))
        o_ref[40:64, :] = s_ref[64:88, :]
        o_ref[64:72, :] = jnp.zeros((8, D), F32)
        for d in range(2):
            pp = lbp_ref[2 * d:2 * d + 1, :] * lbp_ref[2 * d + 1:2 * d + 2, :] * s_ref[75 + d:76 + d, :]
            o_ref[64 + 2 * d:65 + 2 * d, :] = -pp
            o_ref[65 + 2 * d:66 + 2 * d, :] = pp

    return pl.pallas_call(
        body, name=name, in_specs=[VMEM_SPEC] * 4, out_specs=VMEM_SPEC,
        out_shape=jax.ShapeDtypeStruct((FIN_ROWS, D), F32),
        scratch_shapes=[pltpu.VMEM((SMALL_ROWS, D), F32)],
    )(g3, g4, c_ctx, lbp)


def _pack_rows(arrs):
    flat = jnp.concatenate([a.reshape(-1) for a in arrs])
    n = -(-flat.shape[0] // (8 * D)) * 8 * D
    return jnp.pad(flat, (0, n - flat.shape[0])).reshape(n // D, D)


def _unpack_rows(packed, shapes):
    flat = packed.reshape(-1)
    outs, off = [], 0
    for s in shapes:
        size = 1
        for k in s:
            size *= k
        outs.append(flat[off:off + size].reshape(s))
        off += size
    return outs


def _pad8(a):
    return jnp.pad(a, ((0, 8 - a.shape[0]), (0, 0)))


def kernel(x, c, ctx, c_ctx, ada_w, ada_b, norm1, norm2, norm_f, mlp_w1, mlp_w2, hgrn_w_in, hgrn_lb, hgrn_gnorm, hgrn_w_out, conv_w_in, conv_w, conv_b, conv_w_out, loss_target, m_c_ctx, m_ada_w, m_ada_b, m_norm1, m_norm2, m_norm_f, m_mlp_w1, m_mlp_w2, m_hgrn_w_in, m_hgrn_lb, m_hgrn_gnorm, m_hgrn_w_out, m_conv_w_in, m_conv_w, m_conv_b, m_conv_w_out, v_c_ctx, v_ada_w, v_ada_b, v_norm1, v_norm2, v_norm_f, v_mlp_w1, v_mlp_w2, v_hgrn_w_in, v_hgrn_lb, v_hgrn_gnorm, v_hgrn_w_out, v_conv_w_in, v_conv_w, v_conv_b, v_conv_w_out):
    xi, yi, ci = _place()
    me = 4 * xi + 2 * yi + ci
    chip = 2 * xi + yi
    seq = x.shape[1]
    assert ctx.shape[1] == TM and seq % TM == 0 and (seq + TM) % TMW == 0
    n_lat = seq // TM
    sd = D // 4
    nca = ada_w.shape[2]
    xs = jnp.concatenate([x[0], ctx[0]], axis=0)

    sh_rows = jnp.concatenate([hgrn_lb.reshape(4, sd), conv_w.reshape(6, sd), conv_b.reshape(2, sd)], axis=0)
    buf1 = jnp.concatenate([c, jnp.pad(sh_rows, ((0, 0), (0, D - sd))), jnp.zeros((3, D), F32)], axis=0)
    g1 = small_allgather(buf1, "gather_small_in")
    cvec = jnp.concatenate([g1[:, 0, :], jnp.broadcast_to(c_ctx[None], (8, D))], axis=0)
    shf = g1[0::2, 1:13, :sd].transpose(1, 0, 2).reshape(12, D)
    lb_p = jax.nn.softmax(shf[0:4].reshape(2, 2, D), axis=1)
    lower = jnp.cumsum(lb_p, axis=1) - lb_p[:, :1]
    lbs = [lower[:, 0], lower[:, 1]]
    cw8 = [_pad8(shf[4:7]), _pad8(shf[7:10])]
    cb = [shf[10:11], shf[11:12]]

    bias = lax.dynamic_slice_in_dim(ada_b, chip * nca, nca, axis=1).reshape(DEPTH, 1, nca)
    ada_part = ada_fwd(cvec, ada_w, bias, "ada_fwd")
    g2 = small_allgather(ada_part.reshape(DEPTH * 16, nca), "gather_ada")
    ada_full = g2[0::2].reshape(4, DEPTH, 16, nca).transpose(1, 2, 0, 3).reshape(DEPTH, 16, 4 * nca)
    lat = lax.dynamic_slice_in_dim(ada_full, me, 1, axis=1)[:, 0]
    mods = [jnp.stack([_pad8(lat[i].reshape(6, D)), _pad8(ada_full[i, 8].reshape(6, D))]) for i in range(DEPTH)]

    big = [(mlp_w1, m_mlp_w1, v_mlp_w1), (mlp_w2, m_mlp_w2, v_mlp_w2), (hgrn_w_in, m_hgrn_w_in, v_hgrn_w_in),
           (hgrn_w_out, m_hgrn_w_out, v_hgrn_w_out), (conv_w_in, m_conv_w_in, v_conv_w_in), (conv_w_out, m_conv_w_out, v_conv_w_out)]
    big_names = ["w1", "w2", "hin", "hout", "cin", "cout"]
    flat2 = lambda a: a.reshape(a.shape[0] * a.shape[1], a.shape[2])
    tensors = dict(zip(big_names, big))
    chip1 = jnp.reshape(chip, (1,)).astype(jnp.int32)
    order = []
    for i in range(DEPTH):
        order += [("hin", i // 2), ("hout", i // 2)] if i % 2 == 0 else [("cin", i // 2), ("cout", i // 2)]
        order += [("w1", i), ("w2", i)]
    lands = [cast_to_slot(flat2(tensors[n][0]), idx, tensors[n][0].shape[1], chip1, f"cast_{n}_{idx}") for n, idx in order]
    w_sems, lands, _ = split_start(lands, None, "gather_start")
    unit = {key: u for u, key in enumerate(order)}

    def wts(n, idx, after):
        u = unit[(n, idx)]
        (w,), _ = split_wait([lands[u]], None, w_sems[2 * u:2 * u + 2], after, f"gather_wait_{n}_{idx}")
        return w.reshape(w.shape[0] * w.shape[1], w.shape[2]) if n in ("w2", "hout", "cout") else w

    started = []

    def on_grads(i, tag, g):
        names = sorted(g)
        gs = [g[n].reshape(4, g[n].shape[0] * g[n].shape[1] // 4, g[n].shape[2]) for n in names]
        sems, zones, srcs = split_start([lax.empty(a.shape, BF16) for a in gs], gs, f"grad_start_{tag}_{i}")
        started.append(([(n, i if n in ("w1", "w2") else i // 2) for n in names], sems, zones, srcs))

    lane, dx, small = local_step(xs, loss_target[0], mods, norm1, norm2, norm_f[None], lbs, hgrn_gnorm, cw8, cb, wts, n_lat, on_grads)
    loss = lax.psum(0.5 * jnp.sum(lane) / D, ("x", "y", "c"))
    grad_x = dx[:seq][None]

    rows3 = jnp.concatenate(small["dmod"] + small["norm1"] + small["norm2"] + [small["norm_f"]] + small["gnorm"]
                            + [small["lb"][1]] + small["cw"] + small["cb"] + [jnp.zeros((3, D), F32)], axis=0)
    g3 = small_allgather(rows3, "gather_small_out")
    dmat = g3[:, :64].reshape(8, DEPTH, 2, 8, D)[:, :, :, :6].transpose(1, 2, 0, 3, 4).reshape(DEPTH, 16, 6 * D)
    dcols = lax.dynamic_slice_in_dim(dmat, chip * nca, nca, axis=2)
    g_ada_w, d_ada_w, nm_ada_w, nv_ada_w, acc4 = ada_bwd(cvec, dcols, ada_w, m_ada_w, v_ada_w, "ada_bwd")
    g4 = small_allgather(acc4, "gather_cctx")
    fin = small_finish(g3, g4, c_ctx[None], _pad8(lb_p.reshape(4, D)), "small_finish")
    cols = lambda a: lax.dynamic_slice_in_dim(a, chip * sd, sd, axis=a.ndim - 1)
    small_g = [fin[32], fin[0:32].reshape(DEPTH, 8, D)[:, :6].reshape(DEPTH, 6 * D), fin[40:44], fin[44:48], fin[48], fin[49:51],
               cols(fin[64:68].reshape(2, 2, D)), cols(fin[53:59].reshape(2, 3, D)), cols(fin[59:61])]
    small_w = [c_ctx, ada_b, norm1, norm2, norm_f, hgrn_gnorm, hgrn_lb, conv_w, conv_b]
    small_m = [m_c_ctx, m_ada_b, m_norm1, m_norm2, m_norm_f, m_hgrn_gnorm, m_hgrn_lb, m_conv_w, m_conv_b]
    small_v = [v_c_ctx, v_ada_b, v_norm1, v_norm2, v_norm_f, v_hgrn_gnorm, v_hgrn_lb, v_conv_w, v_conv_b]
    shapes = [w.shape for w in small_w]
    packed = adamw([_pack_rows(small_g)], _pack_rows(small_w), _pack_rows(small_m), _pack_rows(small_v), "adamw_small")
    s_g, s_d, s_m, s_v = [_unpack_rows(p, shapes) for p in packed]

    units = []
    for ks, sems, zones, srcs in started:
        units += [(key, sems[2 * u:2 * u + 2], zones[u], srcs[u]) for u, key in enumerate(ks)]
    late_keys = (("hin", 0), ("hout", 0))
    acc = {n: lax.empty(flat2(w).shape, F32) for n, (w, _, _) in tensors.items()}

    def finish_units(group, after, name):
        zones, srcs = split_wait([u[2] for u in group], [u[3] for u in group], [s for u in group for s in u[1]], after, name)
        for (key, _, _, _), zone, own in zip(group, zones, srcs):
            acc[key[0]] = sum_slots(own, zone, acc[key[0]], key[1], chip1, f"sum_{key[0]}_{key[1]}")

    finish_units([u for u in units if u[0] not in late_keys], fin, "grad_wait_early")
    early_names = ["w1", "w2", "cin", "cout"]
    late_names = ["hin", "hout"]
    sib_early = sibling_start([acc[n] for n in early_names], "sibling_start_early")
    finish_units([u for u in units if u[0] in late_keys], sib_early[0][-1], "grad_wait_late")
    sib_late = sibling_start([acc[n] for n in late_names], "sibling_start_late")
    results = {}

    def finish_tensors(names, sib, after, name):
        mine, other = sibling_wait(*sib, after, name)
        for n, pm, po in zip(names, mine, other):
            w, m, v = tensors[n]
            results[n] = [a.reshape(w.shape) for a in adamw([pm, po], flat2(w), flat2(m), flat2(v), f"adamw_{n}")]

    finish_tensors(early_names, sib_early, sib_late[0][-1], "sibling_wait_early")
    finish_tensors(late_names, sib_late, results["cout"][0], "sibling_wait_late")
    b_g, b_d, b_m, b_v = [[results[n][k] for n in big_names] for k in range(4)]

    def ordered(s, a, b):
        return [s[0], a, s[1], s[2], s[3], s[4], b[0], b[1], b[2], s[6], s[5], b[3], b[4], s[7], s[8], b[5]]

    return (loss, grad_x, *ordered(s_g, g_ada_w, b_g), *ordered(s_d, d_ada_w, b_d), *ordered(s_m, nm_ada_w, b_m),
            *ordered(s_v, nv_ada_w, b_v))
```

```python
import functools

import jax
import jax.numpy as jnp
from jax import lax
from jax.experimental import pallas as pl
from jax.experimental.pallas import tpu as pltpu

F32 = jnp.float32
BF16 = jnp.bfloat16
MESH = pl.DeviceIdType.MESH

D = 1024
HD = 128
NH = D // HD
CH = 64
TM = 256
TMW = 768
EPS = 1e-6
DEPTH = 4
VMEM_LIMIT = 56 * 1024 * 1024

ADAM_LR = 0.001
ADAM_B1 = 0.9
ADAM_B2 = 0.999
ADAM_EPS = 1e-08
ADAM_WD = 0.01
ADAM_STEP = 10


def _cp(n_grid):
    return pltpu.CompilerParams(dimension_semantics=("arbitrary",) * n_grid, vmem_limit_bytes=VMEM_LIMIT)


def _dot(a, b):
    return jnp.dot(a, b, preferred_element_type=F32)


def _dot_nt(a, b):
    return lax.dot_general(a, b, (((1,), (1,)), ((), ())), preferred_element_type=F32)


def _dot_tn(a, b):
    return lax.dot_general(a, b, (((0,), (0,)), ((), ())), preferred_element_type=F32)


def _sigmoid(z):
    return 1.0 / (1.0 + jnp.exp(-z))


def _norm_mod(x, gain, sh, sc):
    r = lax.rsqrt(jnp.mean(x * x, axis=-1, keepdims=True) + EPS)
    xn = x * r
    yn = xn * gain
    return r, xn, yn, yn * (1.0 + sc) + sh


def _row_spec(width):
    return pl.BlockSpec((TM, width), lambda i: (i, 0))


def _col_spec(col):
    return pl.BlockSpec((TM, D), lambda i: (i, col))


def _full_spec(shape):
    nd = len(shape)
    return pl.BlockSpec(shape, lambda i: (0,) * nd)


def _mod_spec(n_lat):
    return pl.BlockSpec((1, 8, D), lambda i: (i // n_lat, 0, 0))


def proj_fwd(x, gain, mod, m0, w4, n_lat, name):
    t = x.shape[0]
    nb, _, ns = w4.shape

    def body(x_ref, gain_ref, mod_ref, w_ref, p_ref):
        _, _, _, h = _norm_mod(x_ref[...], gain_ref[...], mod_ref[0, m0:m0 + 1, :], mod_ref[0, m0 + 1:m0 + 2, :])
        hb = h.astype(BF16)
        for c in range(nb):
            p_ref[:, c * ns:(c + 1) * ns] = _dot(hb, w_ref[c])

    return pl.pallas_call(
        body, name=name, grid=(t // TM,),
        in_specs=[_row_spec(D), _full_spec((1, D)), _mod_spec(n_lat), _full_spec(w4.shape)],
        out_specs=_row_spec(nb * ns),
        out_shape=jax.ShapeDtypeStruct((t, nb * ns), F32),
        compiler_params=_cp(1),
    )(x, gain, mod, w4)


def proj_bwd(parts, w4, x, gain, mod, m0, dx_in, n_lat, name):
    t = x.shape[0]
    nb, _, ns = w4.shape
    n = nb * ns
    n_parts = len(parts)
    widths = [p.shape[1] for p in parts]
    offs = [sum(widths[:k]) for k in range(n_parts)]
    assert sum(widths) == n
    single = n_parts == 1

    def body(*refs):
        part_refs = refs[:n_parts]
        w_ref, x_ref, gain_ref, mod_ref, dxin_ref = refs[n_parts:n_parts + 5]
        rest = refs[n_parts + 5:]
        if single:
            dx_ref, hb_ref, acc_ref = rest
            src = part_refs[0]
        else:
            dx_ref, hb_ref, acc_ref, dpb_ref = rest
            for p_ref, off, w in zip(part_refs, offs, widths):
                dpb_ref[:, off:off + w] = p_ref[...]
            src = dpb_ref
        i = pl.program_id(0)

        @pl.when(i == 0)
        def _():
            acc_ref[...] = jnp.zeros_like(acc_ref)

        gain = gain_ref[...]
        sc = mod_ref[0, m0 + 1:m0 + 2, :]
        r, xn, yn, h = _norm_mod(x_ref[...], gain, mod_ref[0, m0:m0 + 1, :], sc)
        hb_ref[...] = h.astype(BF16)
        dh = _dot_nt(src[:, 0:ns], w_ref[0])
        for c in range(1, nb):
            dh = dh + _dot_nt(src[:, c * ns:(c + 1) * ns], w_ref[c])
        dsh = jnp.sum(dh, axis=0, keepdims=True)
        dsc = jnp.sum(dh * yn, axis=0, keepdims=True)
        dyn = dh * (1.0 + sc)
        dgain = jnp.sum(dyn * xn, axis=0, keepdims=True)
        dxn = dyn * gain
        dx = r * (dxn - xn * jnp.mean(dxn * xn, axis=-1, keepdims=True))
        dx_ref[...] = dxin_ref[...] + dx
        latf = (i < n_lat).astype(F32)
        ctxf = 1.0 - latf
        acc_ref[0:1, :] += dgain
        acc_ref[1:2, :] += dsh * latf
        acc_ref[2:3, :] += dsc * latf
        acc_ref[3:4, :] += dsh * ctxf
        acc_ref[4:5, :] += dsc * ctxf

    out_specs = [_row_spec(D), _row_spec(D), _full_spec((8, D))]
    out_shape = [jax.ShapeDtypeStruct((t, D), F32), jax.ShapeDtypeStruct((t, D), BF16), jax.ShapeDtypeStruct((8, D), F32)]
    if not single:
        out_specs.append(_row_spec(n))
        out_shape.append(jax.ShapeDtypeStruct((t, n), BF16))
    outs = pl.pallas_call(
        body, name=name, grid=(t // TM,),
        in_specs=[_row_spec(w) for w in widths]
        + [_full_spec(w4.shape), _row_spec(D), _full_spec((1, D)), _mod_spec(n_lat), _row_spec(D)],
        out_specs=out_specs, out_shape=out_shape, compiler_params=_cp(1),
    )(*parts, w4, x, gain, mod, dx_in)
    if single:
        return outs[0], outs[1], parts[0], outs[2]
    return outs[0], outs[1], outs[3], outs[2]


def dw_tn(a, b, nb, a_blocked, name):
    t = a.shape[0]
    ka = a.shape[1] // nb if a_blocked else a.shape[1]
    kb = b.shape[1] if a_blocked else b.shape[1] // nb
    n_k = t // TMW

    def body(a_ref, b_ref, o_ref, acc):
        k = pl.program_id(1)

        @pl.when(k == 0)
        def _():
            acc[...] = jnp.zeros_like(acc)

        acc[...] += _dot_tn(a_ref[...], b_ref[...])

        @pl.when(k == n_k - 1)
        def _():
            o_ref[0] = acc[...].astype(BF16)

    a_spec = pl.BlockSpec((TMW, ka), (lambda j, k: (k, j)) if a_blocked else (lambda j, k: (k, 0)))
    b_spec = pl.BlockSpec((TMW, kb), (lambda j, k: (k, 0)) if a_blocked else (lambda j, k: (k, j)))
    return pl.pallas_call(
        body, name=name, grid=(nb, n_k),
        in_specs=[a_spec, b_spec],
        out_specs=pl.BlockSpec((1, ka, kb), lambda j, k: (j, 0, 0)),
        out_shape=jax.ShapeDtypeStruct((nb, ka, kb), BF16),
        scratch_shapes=[pltpu.VMEM((ka, kb), F32)],
        compiler_params=_cp(2),
    )(a, b)


def outproj_fwd(prologue, extras, extra_specs, w, x, mod, m0, n_lat, name):
    t = x.shape[0]
    k = w.shape[0]
    n_extra = len(extras)

    def body(*refs):
        ex = refs[:n_extra]
        w_ref, x_ref, mod_ref, xo_ref, y_ref, ab_ref = refs[n_extra:]
        ab = prologue(pl.program_id(0), *ex).astype(BF16)
        ab_ref[...] = ab
        y = _dot(ab, w_ref[...])
        y_ref[...] = y
        xo_ref[...] = x_ref[...] + mod_ref[0, m0 + 2:m0 + 3, :] * y

    return pl.pallas_call(
        body, name=name, grid=(t // TM,),
        in_specs=list(extra_specs) + [_full_spec(w.shape), _row_spec(D), _mod_spec(n_lat)],
        out_specs=[_row_spec(D), _row_spec(D), _row_spec(k)],
        out_shape=[jax.ShapeDtypeStruct((t, D), F32), jax.ShapeDtypeStruct((t, D), F32), jax.ShapeDtypeStruct((t, k), BF16)],
        compiler_params=_cp(1),
    )(*extras, w, x, mod)


def outproj_bwd(epilogue, extras, extra_specs, ep_out_specs, ep_out_shapes, w, dxn, y, mod, m0, n_lat, dep, name):
    t = dxn.shape[0]
    n_extra = len(extras)

    def body(*refs):
        ex = refs[:n_extra]
        w_ref, dxn_ref, y_ref, mod_ref, _, dyb_ref, acc_ref = refs[n_extra:n_extra + 7]
        ep_outs = refs[n_extra + 7:]
        i = pl.program_id(0)

        @pl.when(i == 0)
        def _():
            acc_ref[...] = jnp.zeros_like(acc_ref)

        dxv = dxn_ref[...]
        dyb = (dxv * mod_ref[0, m0 + 2:m0 + 3, :]).astype(BF16)
        dyb_ref[...] = dyb
        dg = jnp.sum(dxv * y_ref[...], axis=0, keepdims=True)
        latf = (i < n_lat).astype(F32)
        acc_ref[0:1, :] += dg * latf
        acc_ref[1:2, :] += dg * (1.0 - latf)
        epilogue(i, _dot_nt(dyb, w_ref[...]), ex, ep_outs, acc_ref)

    outs = pl.pallas_call(
        body, name=name, grid=(t // TM,),
        in_specs=list(extra_specs) + [_full_spec(w.shape), _row_spec(D), _row_spec(D), _mod_spec(n_lat), ANY],
        out_specs=[_row_spec(D), _full_spec((8, D))] + list(ep_out_specs),
        out_shape=[jax.ShapeDtypeStruct((t, D), BF16), jax.ShapeDtypeStruct((8, D), F32)] + list(ep_out_shapes),
        compiler_params=_cp(1),
    )(*extras, w, dxn, y, mod, dep)
    return outs[0], outs[1], outs[2:]


def mlp_fwd(x, gain, mod, w1, w2, n_lat, name):
    t = x.shape[0]
    nb, _, ns = w1.shape

    def body(x_ref, gain_ref, mod_ref, w1_ref, w2_ref, xo_ref, y_ref, ab_ref):
        x = x_ref[...]
        _, _, _, h = _norm_mod(x, gain_ref[...], mod_ref[0, 3:4, :], mod_ref[0, 4:5, :])
        hb = h.astype(BF16)
        y = None
        for c in range(nb):
            a = jnp.square(jnp.maximum(_dot(hb, w1_ref[c]), 0.0)).astype(BF16)
            ab_ref[:, c * ns:(c + 1) * ns] = a
            yc = _dot(a, w2_ref[c * ns:(c + 1) * ns, :])
            y = yc if y is None else y + yc
        y_ref[...] = y
        xo_ref[...] = x + mod_ref[0, 5:6, :] * y

    return pl.pallas_call(
        body, name=name, grid=(t // TM,),
        in_specs=[_row_spec(D), _full_spec((1, D)), _mod_spec(n_lat), _full_spec(w1.shape), _full_spec(w2.shape)],
        out_specs=[_row_spec(D), _row_spec(D), _row_spec(nb * ns)],
        out_shape=[jax.ShapeDtypeStruct((t, D), F32), jax.ShapeDtypeStruct((t, D), F32), jax.ShapeDtypeStruct((t, nb * ns), BF16)],
        compiler_params=_cp(1),
    )(x, gain, mod, w1, w2)


def mlp_bwd(dxn, y, ab, x, gain, mod, w1, w2, n_lat, dep, name):
    t = x.shape[0]
    nb, _, ns = w1.shape

    def body(dxn_ref, y_ref, ab_ref, x_ref, gain_ref, mod_ref, w1_ref, w2_ref, _, dx_ref, dyb_ref, dp_ref, hb_ref, acc_ref):
        i = pl.program_id(0)

        @pl.when(i == 0)
        def _():
            acc_ref[...] = jnp.zeros_like(acc_ref)

        dxv = dxn_ref[...]
        dyb = (dxv * mod_ref[0, 5:6, :]).astype(BF16)
        dyb_ref[...] = dyb
        dg = jnp.sum(dxv * y_ref[...], axis=0, keepdims=True)
        gain = gain_ref[...]
        sc = mod_ref[0, 4:5, :]
        r, xn, yn, h = _norm_mod(x_ref[...], gain, mod_ref[0, 3:4, :], sc)
        hb_ref[...] = h.astype(BF16)
        dh = None
        for c in range(nb):
            cols = slice(c * ns, (c + 1) * ns)
            da = _dot_nt(dyb, w2_ref[cols, :])
            dp = (da * (2.0 * jnp.sqrt(ab_ref[:, cols].astype(F32)))).astype(BF16)
            dp_ref[:, cols] = dp
            d = _dot_nt(dp, w1_ref[c])
            dh = d if dh is None else dh + d
        dsh = jnp.sum(dh, axis=0, keepdims=True)
        dsc = jnp.sum(dh * yn, axis=0, keepdims=True)
        dyn = dh * (1.0 + sc)
        dgain = jnp.sum(dyn * xn, axis=0, keepdims=True)
        dxn_ = dyn * gain
        dx_ref[...] = dxv + r * (dxn_ - xn * jnp.mean(dxn_ * xn, axis=-1, keepdims=True))
        latf = (i < n_lat).astype(F32)
        ctxf = 1.0 - latf
        acc_ref[0:1, :] += dgain
        acc_ref[1:2, :] += dsh * latf
        acc_ref[2:3, :] += dsc * latf
        acc_ref[3:4, :] += dsh * ctxf
        acc_ref[4:5, :] += dsc * ctxf
        acc_ref[5:6, :] += dg * latf
        acc_ref[6:7, :] += dg * ctxf

    return pl.pallas_call(
        body, name=name, grid=(t // TM,),
        in_specs=[_row_spec(D), _row_spec(D), _row_spec(nb * ns), _row_spec(D), _full_spec((1, D)), _mod_spec(n_lat),
                  _full_spec(w1.shape), _full_spec(w2.shape), ANY],
        out_specs=[_row_spec(D), _row_spec(D), _row_spec(nb * ns), _row_spec(D), _full_spec((8, D))],
        out_shape=[jax.ShapeDtypeStruct((t, D), F32), jax.ShapeDtypeStruct((t, D), BF16), jax.ShapeDtypeStruct((t, nb * ns), BF16),
                   jax.ShapeDtypeStruct((t, D), BF16), jax.ShapeDtypeStruct((8, D), F32)],
        compiler_params=_cp(1),
    )(dxn, y, ab, x, gain, mod, w1, w2, dep)


def readout_prologue(i, o0_ref, o1_ref, gate_ref, gn_ref):
    o = o0_ref[...] + o1_ref[...]
    gate = gate_ref[...]
    w = gn_ref[...] * (gate * _sigmoid(gate))
    pieces = []
    for h in range(NH):
        sl = slice(h * HD, (h + 1) * HD)
        oh = o[:, sl]
        pieces.append(oh * lax.rsqrt(jnp.mean(oh * oh, axis=-1, keepdims=True) + EPS) * w[:, sl])
    return jnp.concatenate(pieces, axis=1)


def readout_epilogue(i, da, ex, outs, acc_ref):
    o0_ref, o1_ref, gate_ref, gn_ref = ex
    do_ref, dgate_ref = outs
    o = o0_ref[...] + o1_ref[...]
    gate = gate_ref[...]
    gn = gn_ref[...]
    sg = _sigmoid(gate)
    silu = gate * sg
    dsilu = sg * (1.0 + gate * (1.0 - sg))
    for h in range(NH):
        sl = slice(h * HD, (h + 1) * HD)
        oh = o[:, sl]
        r = lax.rsqrt(jnp.mean(oh * oh, axis=-1, keepdims=True) + EPS)
        nh = oh * r
        dah = da[:, sl]
        acc_ref[2:3, sl] += jnp.sum(dah * nh * silu[:, sl], axis=0, keepdims=True)
        dgate_ref[:, sl] = (dah * nh * gn[:, sl] * dsilu[:, sl]).astype(BF16)
        dn = dah * gn[:, sl] * silu[:, sl]
        do_ref[:, sl] = r * (dn - nh * jnp.mean(dn * nh, axis=-1, keepdims=True))


def _seg_masks(i, n_lat):
    rows = lax.broadcasted_iota(jnp.int32, (TM, 1), 0)
    latf = (i < n_lat).astype(F32)
    ctxf = 1.0 - latf
    prev_ok = (rows % CH != 0).astype(F32) * latf + (rows != 0).astype(F32) * ctxf
    next_ok = (rows % CH != CH - 1).astype(F32) * latf + (rows != TM - 1).astype(F32) * ctxf
    return prev_ok, next_ok


def _shifts(i, n_lat, sft, cur, halo_prev, halo_next):
    if sft == 1:
        prev_ok, next_ok = _seg_masks(i, n_lat)
        return pltpu.roll(cur, 1, 0) * prev_ok, pltpu.roll(cur, TM - 1, 0) * next_ok
    has_prev = jnp.logical_and(i > 0, i < n_lat).astype(F32)
    has_next = (i < n_lat - 1).astype(F32)
    prev = jnp.concatenate([halo_prev * has_prev, cur[:TM - CH]], axis=0)
    nxt = jnp.concatenate([cur[CH:], halo_next * has_next], axis=0)
    return prev, nxt


def _conv_u(sft, ex):
    if sft == 1:
        gb_ref, gc_ref, xi_ref, cw_ref, cb_ref = ex
        return gb_ref, gc_ref[...] * xi_ref[...], None, None, cw_ref, cb_ref
    gb_ref, gc_ref, xi_ref, gcp_ref, xip_ref, gcn_ref, xin_ref, cw_ref, cb_ref = ex
    return gb_ref, gc_ref[...] * xi_ref[...], gcp_ref[...] * xip_ref[...], gcn_ref[...] * xin_ref[...], cw_ref, cb_ref


def _conv_value(i, n_lat, sft, ex):
    gb_ref, u, up, un, cw_ref, cb_ref = _conv_u(sft, ex)
    u_prev, u_next = _shifts(i, n_lat, sft, u, up, un)
    return gb_ref, cb_ref[...] + cw_ref[0:1, :] * u_prev + cw_ref[1:2, :] * u + cw_ref[2:3, :] * u_next


def make_conv_prologue(n_lat, sft):
    def prologue(i, *ex):
        gb_ref, conv = _conv_value(i, n_lat, sft, ex)
        return gb_ref[...] * conv
    return prologue


def make_conv_epilogue(n_lat, sft):
    def epilogue(i, da, ex, outs, acc_ref):
        gb_ref, conv = _conv_value(i, n_lat, sft, ex)
        outs[0][...] = da * gb_ref[...]
        outs[1][...] = (da * conv).astype(BF16)
    return epilogue


def _conv_specs(sft, t):
    specs = [_col_spec(0), _col_spec(1), _col_spec(2)]
    if sft != 1:
        per = TM // CH
        last = t // CH - 1
        for fn in (lambda i: jnp.maximum(i * per - 1, 0), lambda i: jnp.minimum(i * per + per, last)):
            for col in (1, 2):
                specs.append(pl.BlockSpec((CH, D), functools.partial(lambda i, f, c: (f(i), c), f=fn, c=col)))
    return specs + [_full_spec((8, D)), _full_spec((1, D))]


def _conv_args(sft, p, cw8, cb):
    return [p] * (3 if sft == 1 else 7) + [cw8, cb]


def conv_bwd(dconv, p, cw8, sft, n_lat, name):
    t = dconv.shape[0]
    halo = sft != 1

    def body(*refs):
        if halo:
            dc_ref, dcp_ref, dcn_ref, gc_ref, xi_ref, gcp_ref, xip_ref, gcn_ref, xin_ref, cw_ref, dgc_ref, dxi_ref, acc_ref = refs
            up, un = gcp_ref[...] * xip_ref[...], gcn_ref[...] * xin_ref[...]
            dcp, dcn = dcp_ref[...], dcn_ref[...]
        else:
            dc_ref, gc_ref, xi_ref, cw_ref, dgc_ref, dxi_ref, acc_ref = refs
            up = un = dcp = dcn = None
        i = pl.program_id(0)

        @pl.when(i == 0)
        def _():
            acc_ref[...] = jnp.zeros_like(acc_ref)

        gc = gc_ref[...]
        xi = xi_ref[...]
        u = gc * xi
        dc = dc_ref[...]
        u_prev, u_next = _shifts(i, n_lat, sft, u, up, un)
        dc_prev, dc_next = _shifts(i, n_lat, sft, dc, dcp, dcn)
        acc_ref[0:1, :] += jnp.sum(dc * u_prev, axis=0, keepdims=True)
        acc_ref[1:2, :] += jnp.sum(dc * u, axis=0, keepdims=True)
        acc_ref[2:3, :] += jnp.sum(dc * u_next, axis=0, keepdims=True)
        acc_ref[3:4, :] += jnp.sum(dc, axis=0, keepdims=True)
        du = cw_ref[0:1, :] * dc_next + cw_ref[1:2, :] * dc + cw_ref[2:3, :] * dc_prev
        dgc_ref[...] = (du * xi).astype(BF16)
        dxi_ref[...] = (du * gc).astype(BF16)

    per = TM // CH
    last = t // CH - 1
    prev_i = lambda i: jnp.maximum(i * per - 1, 0)
    next_i = lambda i: jnp.minimum(i * per + per, last)
    if halo:
        in_specs = [_row_spec(D), pl.BlockSpec((CH, D), lambda i: (prev_i(i), 0)), pl.BlockSpec((CH, D), lambda i: (next_i(i), 0)),
                    _col_spec(1), _col_spec(2),
                    pl.BlockSpec((CH, D), lambda i: (prev_i(i), 1)), pl.BlockSpec((CH, D), lambda i: (prev_i(i), 2)),
                    pl.BlockSpec((CH, D), lambda i: (next_i(i), 1)), pl.BlockSpec((CH, D), lambda i: (next_i(i), 2)),
                    _full_spec((8, D))]
        args = [dconv, dconv, dconv, p, p, p, p, p, p, cw8]
    else:
        in_specs = [_row_spec(D), _col_spec(1), _col_spec(2), _full_spec((8, D))]
        args = [dconv, p, p, cw8]
    return pl.pallas_call(
        body, name=name, grid=(t // TM,), in_specs=in_specs,
        out_specs=[_row_spec(D), _row_spec(D), _full_spec((8, D))],
        out_shape=[jax.ShapeDtypeStruct((t, D), BF16), jax.ShapeDtypeStruct((t, D), BF16), jax.ShapeDtypeStruct((8, D), F32)],
        compiler_params=_cp(1),
    )(*args)


LOG2E = 1.4426950408889634


def _cumsum_matrix(reverse):
    r = lax.broadcasted_iota(jnp.int32, (CH, CH), 0)
    c = lax.broadcasted_iota(jnp.int32, (CH, CH), 1)
    return (r <= c if reverse else r >= c).astype(BF16)


def _chunk_cumsum(g, tri):
    hi = g.astype(BF16)
    lo = (g - hi.astype(F32)).astype(BF16)
    return _dot(tri, hi) + _dot(tri, lo)


def _gate_values(z, lb):
    sig = _sigmoid(z)
    f = lb + (1.0 - lb) * sig
    return sig, f


def _tri(direction, transposed):
    r = lax.broadcasted_iota(jnp.int32, (CH, CH), 0)
    c = lax.broadcasted_iota(jnp.int32, (CH, CH), 1)
    lower = (direction == 0) != transposed
    return r >= c if lower else r <= c


def _gla_rows(direction):
    return (CH // 2 - 1, CH - 1) if direction == 0 else (CH // 2, 0)


def gla_fwd(p, lb2, direction, name):
    t = p.shape[0]
    nt = t // TM
    per = TM // CH
    ref_row, last_row = _gla_rows(direction)
    tile = (lambda i: (i + nt - 1) % nt) if direction == 0 else (lambda i: nt - 1 - i)

    def body(z_ref, v_ref, qr_ref, lb_ref, o_ref, s_ref, st, q_s, k_s, c_s):
        @pl.when(pl.program_id(0) == 0)
        def _():
            st[...] = jnp.zeros_like(st)

        mask = _tri(direction, False)
        tri = _cumsum_matrix(direction == 1)
        lb = lb_ref[direction:direction + 1, :]
        for ci in range(per):
            rows = slice(ci * CH, (ci + 1) * CH)
            _, f = _gate_values(z_ref[rows, :], lb)
            k_s[rows, :] = 1.0 - f
            c_s[rows, :] = _chunk_cumsum(jnp.log(f) * LOG2E, tri)
            qr = qr_ref[rows, :]
            q_s[rows, :] = qr * _sigmoid(qr)
        for it in range(per):
            ci = it if direction == 0 else per - 1 - it
            r0 = ci * CH
            rows = slice(r0, r0 + CH)
            cum = c_s[rows, :]
            ref = c_s[r0 + ref_row:r0 + ref_row + 1, :]
            last = c_s[r0 + last_row:r0 + last_row + 1, :]
            q = q_s[rows, :]
            k = k_s[rows, :]
            qh = (q * jnp.exp2(cum)).astype(BF16)
            qt = (q * jnp.exp2(cum - ref)).astype(BF16)
            kt = (k * jnp.exp2(ref - cum)).astype(BF16)
            kb = (k * jnp.exp2(last - cum)).astype(BF16)
            el = jnp.exp2(last)
            vb = v_ref[rows, :].astype(BF16)
            for h in range(NH):
                sl = slice(h * HD, (h + 1) * HD)
                s_t = st[h]
                s_ref[ci, h] = s_t
                sc = jnp.where(mask, _dot_nt(qt[:, sl], kt[:, sl]), 0.0)
                o_ref[rows, sl] = _dot_nt(qh[:, sl], s_t.astype(BF16)) + _dot(sc.astype(BF16), vb[:, sl])
                st[h] = s_t * el[:, sl] + _dot_tn(vb[:, sl], kb[:, sl])

    tspec = lambda col: pl.BlockSpec((TM, D), lambda i: (tile(i), col))
    return pl.pallas_call(
        body, name=name, grid=(nt,),
        in_specs=[tspec(direction), tspec(2), tspec(3), _full_spec((2, D))],
        out_specs=[pl.BlockSpec((TM, D), lambda i: (tile(i), 0)), pl.BlockSpec((per, NH, HD, HD), lambda i: (tile(i), 0, 0, 0))],
        out_shape=[jax.ShapeDtypeStruct((t, D), F32), jax.ShapeDtypeStruct((t // CH, NH, HD, HD), F32)],
        scratch_shapes=[pltpu.VMEM((NH, HD, HD), F32)] + [pltpu.VMEM((TM, D), F32)] * 3,
        compiler_params=_cp(1),
    )(p, p, p, lb2)


def gla_bwd(p, lb2, do, states, direction, prev, name):
    t = p.shape[0]
    nt = t // TM
    per = TM // CH
    ref_row, last_row = _gla_rows(direction)
    tile = (lambda i: (2 * nt - 2 - i) % nt) if direction == 0 else (lambda i: i)
    final = prev is not None
    n_in = 8 if final else 6

    def body(*refs):
        z_ref, v_ref, qr_ref, lb_ref, do_ref, s_ref = refs[:6]
        dz_ref, dv_ref, dq_ref, acc_ref, dst, q_s, k_s, c_s, dq_s, dk_s, dl_s = refs[n_in:]

        @pl.when(pl.program_id(0) == 0)
        def _():
            dst[...] = jnp.zeros_like(dst)
            acc_ref[...] = jnp.zeros_like(acc_ref)

        mask = _tri(direction, False)
        mask_t = _tri(direction, True)
        tri = _cumsum_matrix(direction == 1)
        tri_t = _cumsum_matrix(direction == 0)
        is_last = lax.broadcasted_iota(jnp.int32, (CH, 1), 0) == last_row
        lb = lb_ref[direction:direction + 1, :]
        for ci in range(per):
            rows = slice(ci * CH, (ci + 1) * CH)
            _, f = _gate_values(z_ref[rows, :], lb)
            k_s[rows, :] = 1.0 - f
            c_s[rows, :] = _chunk_cumsum(jnp.log(f) * LOG2E, tri)
            qr = qr_ref[rows, :]
            q_s[rows, :] = qr * _sigmoid(qr)
        for it in range(per):
            ci = per - 1 - it if direction == 0 else it
            r0 = ci * CH
            rows = slice(r0, r0 + CH)
            cum = c_s[rows, :]
            ref = c_s[r0 + ref_row:r0 + ref_row + 1, :]
            last = c_s[r0 + last_row:r0 + last_row + 1, :]
            q = q_s[rows, :]
            k = k_s[rows, :]
            e_h = jnp.exp2(cum)
            e_t = jnp.exp2(cum - ref)
            e_kt = jnp.exp2(ref - cum)
            e_kb = jnp.exp2(last - cum)
            el = jnp.exp2(last)
            qh = (q * e_h).astype(BF16)
            qt = (q * e_t).astype(BF16)
            kt = (k * e_kt).astype(BF16)
            kbf = k * e_kb
            kb = kbf.astype(BF16)
            vb = v_ref[rows, :].astype(BF16)
            dob = do_ref[rows, :].astype(BF16)
            for h in range(NH):
                sl = slice(h * HD, (h + 1) * HD)
                s_t = s_ref[ci, h]
                ds_t = dst[h]
                ds_b = ds_t.astype(BF16)
                d_a = jnp.where(mask, _dot_nt(dob[:, sl], vb[:, sl]), 0.0).astype(BF16)
                a_t = jnp.where(mask_t, _dot_nt(kt[:, sl], qt[:, sl]), 0.0).astype(BF16)
                d_at = jnp.where(mask_t, _dot_nt(vb[:, sl], dob[:, sl]), 0.0).astype(BF16)
                dv = _dot(a_t, dob[:, sl]) + _dot_nt(kb[:, sl], ds_b)
                dkb = _dot(vb[:, sl], ds_b)
                dl_s[it:it + 1, sl] = (el[:, sl] * jnp.sum(ds_t * s_t, axis=0, keepdims=True)
                                       + jnp.sum(dkb * kbf[:, sl], axis=0, keepdims=True))
                dst[h] = ds_t * el[:, sl] + _dot_tn(dob[:, sl], qh[:, sl])
                dq_s[rows, sl] = _dot(dob[:, sl], s_t.astype(BF16)) * e_h[:, sl] + _dot(d_a, kt[:, sl]) * e_t[:, sl]
                dk_s[rows, sl] = _dot(d_at, qt[:, sl]) * e_kt[:, sl] + dkb * e_kb[:, sl]
                if final:
                    dv_ref[rows, sl] = (refs[6][rows, sl] + dv).astype(BF16)
                else:
                    dv_ref[rows, sl] = dv
        for it in range(per):
            ci = per - 1 - it if direction == 0 else it
            rows = slice(ci * CH, (ci + 1) * CH)
            dq = dq_s[rows, :]
            dk = dk_s[rows, :]
            dg = _chunk_cumsum(dq * q_s[rows, :] - dk * k_s[rows, :] + jnp.where(is_last, dl_s[it:it + 1, :], 0.0), tri_t)
            sig, f = _gate_values(z_ref[rows, :], lb)
            df = dg / f - dk
            acc_ref[0:1, :] += jnp.sum(df * (1.0 - sig), axis=0, keepdims=True)
            dz_ref[rows, :] = (df * (1.0 - lb) * sig * (1.0 - sig)).astype(BF16)
            if final:
                qr = qr_ref[rows, :]
                sq = _sigmoid(qr)
                dq_ref[rows, :] = ((refs[7][rows, :] + dq) * (sq * (1.0 + qr * (1.0 - sq)))).astype(BF16)
            else:
                dq_ref[rows, :] = dq

    tspec = lambda col: pl.BlockSpec((TM, D), lambda i: (tile(i), col))
    sspec = pl.BlockSpec((per, NH, HD, HD), lambda i: (tile(i), 0, 0, 0))
    in_specs = [tspec(direction), tspec(2), tspec(3), _full_spec((2, D)), tspec(0), sspec]
    args = [p, p, p, lb2, do, states]
    if final:
        in_specs += [tspec(0), tspec(0)]
        args += list(prev)
    odt = BF16 if final else F32
    return pl.pallas_call(
        body, name=name, grid=(nt,), in_specs=in_specs,
        out_specs=[tspec(0), tspec(0), tspec(0), _full_spec((8, D))],
        out_shape=[jax.ShapeDtypeStruct((t, D), BF16), jax.ShapeDtypeStruct((t, D), odt), jax.ShapeDtypeStruct((t, D), odt),
                   jax.ShapeDtypeStruct((8, D), F32)],
        scratch_shapes=[pltpu.VMEM((NH, HD, HD), F32)] + [pltpu.VMEM((TM, D), F32)] * 5 + [pltpu.VMEM((8, D), F32)],
        compiler_params=_cp(1),
    )(*args)


def loss_bwd(x, gain, target, n_lat, name):
    t = x.shape[0]

    def body(x_ref, gain_ref, tg_ref, dx_ref, acc_ref):
        i = pl.program_id(0)

        @pl.when(i == 0)
        def _():
            acc_ref[...] = jnp.zeros_like(acc_ref)

        latf = (i < n_lat).astype(F32)
        x = x_ref[...]
        gain = gain_ref[...]
        r = lax.rsqrt(jnp.mean(x * x, axis=-1, keepdims=True) + EPS)
        xn = x * r
        err = (xn * gain - tg_ref[...]) * latf
        dy = err * (1.0 / D)
        dxn = dy * gain
        dx_ref[...] = r * (dxn - xn * jnp.mean(dxn * xn, axis=-1, keepdims=True))
        acc_ref[0:1, :] += jnp.sum(dy * xn, axis=0, keepdims=True)
        acc_ref[1:2, :] += jnp.sum(err * err, axis=0, keepdims=True)

    return pl.pallas_call(
        body, name=name, grid=(t // TM,),
        in_specs=[_row_spec(D), _full_spec((1, D)), pl.BlockSpec((TM, D), lambda i: (jnp.minimum(i, n_lat - 1), 0))],
        out_specs=[_row_spec(D), _full_spec((8, D))],
        out_shape=[jax.ShapeDtypeStruct((t, D), F32), jax.ShapeDtypeStruct((8, D), F32)],
        compiler_params=_cp(1),
    )(x, gain, target)


def local_step(xs, target, mods, norm1, norm2, norm_f, lbs, gnorm, cw8, cb, wts, n_lat, on_grads):
    t = xs.shape[0]
    saved = []
    cache = {}

    def W(name, idx, after=None):
        if (name, idx) not in cache:
            cache[(name, idx)] = wts(name, idx, after)
        return cache[(name, idx)]

    x = xs
    for i in range(DEPTH):
        j = i // 2
        rec = i % 2 == 0
        n1 = norm1[i:i + 1]
        n2 = norm2[i:i + 1]
        s = {"x_in": x}
        if rec:
            p = proj_fwd(x, n1, mods[i], 0, W("hin", j, x), n_lat, f"hin_fwd_{i}")
            o0, st0 = gla_fwd(p, lbs[j], 0, f"gla_fwd0_{i}")
            o1, st1 = gla_fwd(p, lbs[j], 1, f"gla_fwd1_{i}")
            ex = [o0, o1, p, gnorm[j:j + 1]]
            ex_specs = [_row_spec(D), _row_spec(D), _col_spec(4), _full_spec((1, D))]
            xm, y, ab = outproj_fwd(readout_prologue, ex, ex_specs, W("hout", j, o1), x, mods[i], 0, n_lat, f"hout_fwd_{i}")
            s.update(st0=st0, st1=st1)
        else:
            sft = 1 if j % 2 == 0 else CH
            p = proj_fwd(x, n1, mods[i], 0, W("cin", j, x), n_lat, f"cin_fwd_{i}")
            ex = _conv_args(sft, p, cw8[j], cb[j])
            ex_specs = _conv_specs(sft, t)
            xm, y, ab = outproj_fwd(make_conv_prologue(n_lat, sft), ex, ex_specs, W("cout", j, p), x, mods[i], 0, n_lat, f"cout_fwd_{i}")
        s.update(p=p, ex=ex, ex_specs=ex_specs, y_mix=y, ab_mix=ab, x_mid=xm)
        x, y2, ab2 = mlp_fwd(xm, n2, mods[i], W("w1", i, xm), W("w2", i, xm), n_lat, f"mlp_fwd_{i}")
        s.update(y_mlp=y2, ab_mlp=ab2)
        saved.append(s)

    dx, acc_loss = loss_bwd(x, norm_f, target, n_lat, "loss")
    small = {"norm_f": acc_loss[0:1], "norm1": [None] * DEPTH, "norm2": [None] * DEPTH, "dmod": [None] * DEPTH,
             "gnorm": [None] * 2, "lb": [None] * 2, "cw": [None] * 2, "cb": [None] * 2}
    bshape = lambda w: jax.ShapeDtypeStruct((t, w), BF16)
    token = jnp.zeros((8, 128), F32)
    for i in reversed(range(DEPTH)):
        j = i // 2
        rec = i % 2 == 0
        s = saved[i]
        n1 = norm1[i:i + 1]
        n2 = norm2[i:i + 1]
        dx, dyb, dp1, hb, acc_n2 = mlp_bwd(dx, s["y_mlp"], s["ab_mlp"], s["x_mid"], n2, mods[i], W("w1", i), W("w2", i), n_lat, token,
                                           f"mlp_bwd_{i}")
        token = on_grads(i, "mlp", {"w2": dw_tn(s["ab_mlp"], dyb, 4, True, f"w2_dw_{i}"), "w1": dw_tn(hb, dp1, 4, False, f"w1_dw_{i}")})
        g = {}
        if rec:
            dyb, acc_g1, (do, dgate) = outproj_bwd(
                readout_epilogue, s["ex"], s["ex_specs"], [_row_spec(D), _row_spec(D)],
                [jax.ShapeDtypeStruct((t, D), F32), bshape(D)], W("hout", j), dx, s["y_mix"], mods[i], 0, n_lat, token, f"hout_bwd_{i}")
            g["hout"] = dw_tn(s["ab_mix"], dyb, 1, False, f"hout_dw_{i}")
            dz0, dv0, dq0, acc_l0 = gla_bwd(s["p"], lbs[j], do, s["st0"], 0, None, f"gla_bwd0_{i}")
            dz1, dv, dq, acc_l1 = gla_bwd(s["p"], lbs[j], do, s["st1"], 1, (dv0, dq0), f"gla_bwd1_{i}")
            dx, hb, dpb, acc_n1 = proj_bwd([dz0, dz1, dv, dq, dgate], W("hin", j), s["x_in"], n1, mods[i], 0, dx, n_lat, f"hin_bwd_{i}")
            g["hin"] = dw_tn(hb, dpb, 4, False, f"hin_dw_{i}")
            small["gnorm"][j] = acc_g1[2:3]
            small["lb"][j] = jnp.concatenate([acc_l0[0:1], acc_l1[0:1]], axis=0)
        else:
            sft = 1 if j % 2 == 0 else CH
            dyb, acc_g1, (dconv, dgb) = outproj_bwd(
                make_conv_epilogue(n_lat, sft), s["ex"], s["ex_specs"], [_row_spec(D), _row_spec(D)],
                [jax.ShapeDtypeStruct((t, D), F32), bshape(D)], W("cout", j), dx, s["y_mix"], mods[i], 0, n_lat, token, f"cout_bwd_{i}")
            g["cout"] = dw_tn(s["ab_mix"], dyb, 1, False, f"cout_dw_{i}")
            dgc, dxi, acc_c = conv_bwd(dconv, s["p"], cw8[j], sft, n_lat, f"conv_bwd_{i}")
            dx, hb, dpb, acc_n1 = proj_bwd([dgb, dgc, dxi], W("cin", j), s["x_in"], n1, mods[i], 0, dx, n_lat, f"cin_bwd_{i}")
            g["cin"] = dw_tn(hb, dpb, 4, False, f"cin_dw_{i}")
            small["cw"][j] = acc_c[0:3]
            small["cb"][j] = acc_c[3:4]
        small["norm1"][i] = acc_n1[0:1]
        small["norm2"][i] = acc_n2[0:1]
        z2 = jnp.zeros((2, D), F32)
        small["dmod"][i] = jnp.concatenate([acc_n1[1:3], acc_g1[0:1], acc_n2[1:3], acc_n2[5:6], z2,
                                            acc_n1[3:5], acc_g1[1:2], acc_n2[3:5], acc_n2[6:7], z2], axis=0)
        token = on_grads(i, "mix", g)
    return acc_loss[1:2], dx, small, token


RB = 256


def cast_to_slot(w2d, layer, k, chip1, name):
    c = w2d.shape[1]
    nblk = k // RB

    def body(chip_ref, w_ref, o_ref):
        o_ref[0] = w_ref[...].astype(BF16)

    return pl.pallas_call(
        body, name=name,
        grid_spec=pltpu.PrefetchScalarGridSpec(
            num_scalar_prefetch=1, grid=(nblk,),
            in_specs=[pl.BlockSpec((RB, c), lambda i, ch: (layer * nblk + i, 0))],
            out_specs=pl.BlockSpec((1, RB, c), lambda i, ch: (ch[0], i, 0))),
        out_shape=jax.ShapeDtypeStruct((4, k, c), BF16), compiler_params=_cp(1))(chip1, w2d)


def sum_slots(own, land, acc, layer, chip1, name):
    _, k, c = own.shape
    nblk = k // RB

    def body(chip_ref, own_ref, l1_ref, l2_ref, l3_ref, acc_ref, o_ref):
        o_ref[...] = ((own_ref[0].astype(F32) + l1_ref[0].astype(F32)) + l2_ref[0].astype(F32)) + l3_ref[0].astype(F32)

    slot = lambda d: pl.BlockSpec((1, RB, c), lambda i, ch: ((ch[0] + d) % 4, i, 0))
    return pl.pallas_call(
        body, name=name,
        grid_spec=pltpu.PrefetchScalarGridSpec(
            num_scalar_prefetch=1, grid=(nblk,),
            in_specs=[slot(0), slot(1), slot(2), slot(3), ANY],
            out_specs=pl.BlockSpec((RB, c), lambda i, ch: (layer * nblk + i, 0))),
        out_shape=jax.ShapeDtypeStruct(acc.shape, F32), input_output_aliases={5: 0}, compiler_params=_cp(1),
    )(chip1, own, land, land, land, acc)


def _adamw_math(w, g, m, v):
    m = ADAM_B1 * m + (1.0 - ADAM_B1) * g
    v = ADAM_B2 * v + (1.0 - ADAM_B2) * jnp.square(g)
    m_hat = m / (1.0 - ADAM_B1 ** ADAM_STEP)
    v_hat = v / (1.0 - ADAM_B2 ** ADAM_STEP)
    delta = -ADAM_LR * (m_hat / (jnp.sqrt(v_hat) + ADAM_EPS) + ADAM_WD * w)
    return delta, m, v


def adamw(gsrcs, w, m, v, name):
    r, c = w.shape
    rb = RB if r % RB == 0 else r
    n_g = len(gsrcs)

    def body(*refs):
        g = refs[0][...]
        for k in range(1, n_g):
            g = g + refs[k][...]
        w_ref, m_ref, v_ref, g_ref, d_ref, mo_ref, vo_ref = refs[n_g:]
        delta, mo, vo = _adamw_math(w_ref[...], g, m_ref[...], v_ref[...])
        g_ref[...] = g
        d_ref[...] = delta
        mo_ref[...] = mo
        vo_ref[...] = vo

    spec = pl.BlockSpec((rb, c), lambda i: (i, 0))
    shp = jax.ShapeDtypeStruct((r, c), F32)
    return pl.pallas_call(body, name=name, grid=(r // rb,), in_specs=[spec] * (n_g + 3), out_specs=[spec] * 4,
                          out_shape=[shp] * 4, compiler_params=_cp(1))(*gsrcs, w, m, v)


ADA_CB = 512


def ada_fwd(cvec, ada_w, bias, name):
    _, _, nc = ada_w.shape

    def body(c_ref, w_ref, b_ref, o_ref):
        cv = c_ref[...]
        a = (cv * _sigmoid(cv)).astype(BF16)
        o_ref[0] = _dot(a, w_ref[0].astype(BF16)) + b_ref[0]

    return pl.pallas_call(
        body, name=name, grid=(DEPTH, nc // ADA_CB),
        in_specs=[pl.BlockSpec((16, D), lambda i, j: (0, 0)), pl.BlockSpec((1, D, ADA_CB), lambda i, j: (i, 0, j)),
                  pl.BlockSpec((1, 1, ADA_CB), lambda i, j: (i, 0, j))],
        out_specs=pl.BlockSpec((1, 16, ADA_CB), lambda i, j: (i, 0, j)),
        out_shape=jax.ShapeDtypeStruct((DEPTH, 16, nc), F32), compiler_params=_cp(2),
    )(cvec, ada_w, bias)


def ada_bwd(cvec, dcols, ada_w, m, v, name):
    _, _, nc = ada_w.shape

    def body(c_ref, d_ref, w_ref, m_ref, v_ref, g_ref, dl_ref, mo_ref, vo_ref, acc_ref):
        @pl.when(jnp.logical_and(pl.program_id(0) == 0, pl.program_id(1) == 0))
        def _():
            acc_ref[...] = jnp.zeros_like(acc_ref)

        cv = c_ref[...]
        a = (cv * _sigmoid(cv)).astype(BF16)
        db = d_ref[0].astype(BF16)
        w = w_ref[0]
        g = _dot_tn(a, db)
        delta, mo, vo = _adamw_math(w, g, m_ref[0], v_ref[0])
        g_ref[0] = g
        dl_ref[0] = delta
        mo_ref[0] = mo
        vo_ref[0] = vo
        acc_ref[...] += _dot_nt(db[8:16, :], w.astype(BF16))

    wspec = pl.BlockSpec((1, D, ADA_CB), lambda i, j: (i, 0, j))
    wshape = jax.ShapeDtypeStruct(ada_w.shape, F32)
    return pl.pallas_call(
        body, name=name, grid=(DEPTH, nc // ADA_CB),
        in_specs=[pl.BlockSpec((16, D), lambda i, j: (0, 0)), pl.BlockSpec((1, 16, ADA_CB), lambda i, j: (i, 0, j)), wspec, wspec, wspec],
        out_specs=[wspec, wspec, wspec, wspec, pl.BlockSpec((8, D), lambda i, j: (0, 0))],
        out_shape=[wshape, wshape, wshape, wshape, jax.ShapeDtypeStruct((8, D), F32)], compiler_params=_cp(2),
    )(cvec, dcols, ada_w, m, v)


def _place():
    return lax.axis_index("x"), lax.axis_index("y"), lax.axis_index("c")


ANY = pl.BlockSpec(memory_space=pl.ANY)
VMEM_SPEC = pl.BlockSpec(memory_space=pltpu.VMEM)


def small_allgather(buf, name):
    r, c = buf.shape

    def body(in_ref, out_ref, send_sems, recv_sems, loc_sem):
        x, y, cc = _place()
        me = 4 * x + 2 * y + cc
        loc = pltpu.make_async_copy(in_ref, out_ref.at[me], loc_sem)
        loc.start()
        peers = []
        for k in range(1, 8):
            px = 1 - x if k & 4 else x
            py = 1 - y if k & 2 else y
            pc = 1 - cc if k & 1 else cc
            peers.append((px, py, pc))
        sends = []
        for k, peer in enumerate(peers):
            cp = pltpu.make_async_remote_copy(src_ref=in_ref, dst_ref=out_ref.at[me], send_sem=send_sems.at[k],
                                              recv_sem=recv_sems.at[k], device_id=peer, device_id_type=MESH)
            cp.start()
            sends.append(cp)
        for k, (px, py, pc) in enumerate(peers):
            pltpu.make_async_remote_copy(src_ref=in_ref, dst_ref=out_ref.at[4 * px + 2 * py + pc], send_sem=send_sems.at[k],
                                         recv_sem=recv_sems.at[k], device_id=(px, py, pc), device_id_type=MESH).wait_recv()
        for cp in sends:
            cp.wait_send()
        loc.wait()

    return pl.pallas_call(
        body, name=name, in_specs=[VMEM_SPEC], out_specs=VMEM_SPEC,
        out_shape=jax.ShapeDtypeStruct((8, r, c), buf.dtype),
        scratch_shapes=[pltpu.SemaphoreType.DMA((7,)), pltpu.SemaphoreType.DMA((7,)), pltpu.SemaphoreType.DMA],
    )(buf)


def _chip_peers(x, y):
    return [(1 - x, y), (x, 1 - y), (1 - x, 1 - y)]


HBM_SPEC = pl.BlockSpec(memory_space=pltpu.HBM)
SEM_SPEC = pl.BlockSpec(memory_space=pltpu.SEMAPHORE)
EFFECT = pltpu.SideEffectType.DATAFLOW_SIDE_EFFECTING


def _hbm(a):
    return pltpu.with_memory_space_constraint(a, pltpu.HBM)


def _split_copy(u, p, peer, dst_slot, chip, land_refs, src_refs, sem_refs, cc):
    px, py = peer
    src = land_refs[u].at[chip] if src_refs is None else src_refs[u].at[2 * px + py]
    return pltpu.make_async_remote_copy(src_ref=src, dst_ref=land_refs[u].at[dst_slot], send_sem=sem_refs[2 * u].at[p],
                                        recv_sem=sem_refs[2 * u + 1].at[p], device_id=(px, py, cc), device_id_type=MESH)


def split_start(lands, srcs, name):
    n = len(lands)
    ops = list(lands) + (list(srcs) if srcs is not None else [])
    n_ops = len(ops)

    def body(*refs):
        land_refs = refs[:n]
        src_refs = refs[n:n_ops] if srcs is not None else None
        sem_refs = refs[n_ops:n_ops + 2 * n]
        x, y, cc = _place()
        chip = 2 * x + y
        for u in range(n):
            for p, peer in enumerate(_chip_peers(x, y)):
                _split_copy(u, p, peer, chip, chip, land_refs, src_refs, sem_refs, cc).start()
        refs[-1][...] = jnp.zeros((8, 128), F32)

    outs = pl.pallas_call(
        body, name=name, in_specs=[HBM_SPEC] * n_ops,
        out_specs=[SEM_SPEC] * (2 * n) + [HBM_SPEC] * n_ops + [VMEM_SPEC],
        out_shape=[pltpu.SemaphoreType.DMA((3,))] * (2 * n) + [pltpu.HBM(a.shape, a.dtype) for a in ops]
        + [jax.ShapeDtypeStruct((8, 128), F32)],
        input_output_aliases={k: 2 * n + k for k in range(n_ops)},
        compiler_params=pltpu.CompilerParams(has_side_effects=EFFECT),
    )(*[_hbm(a) for a in ops])
    sems = list(outs[:2 * n])
    thru = list(outs[2 * n:2 * n + n_ops])
    return sems, thru[:n], thru[n:], outs[-1]


def split_wait(lands, srcs, sems, after, name):
    n = len(lands)
    ops = list(lands) + (list(srcs) if srcs is not None else [])
    n_ops = len(ops)

    def body(*refs):
        land_refs = refs[:n]
        src_refs = refs[n:n_ops] if srcs is not None else None
        sem_refs = refs[n_ops:n_ops + 2 * n]
        x, y, cc = _place()
        chip = 2 * x + y
        for u in range(n):
            for p, peer in enumerate(_chip_peers(x, y)):
                cp = _split_copy(u, p, peer, 2 * peer[0] + peer[1], chip, land_refs, src_refs, sem_refs, cc)
                cp.wait_send()
                cp.wait_recv()

    outs = pl.pallas_call(
        body, name=name, in_specs=[HBM_SPEC] * n_ops + [SEM_SPEC] * (2 * n) + [ANY],
        out_specs=[HBM_SPEC] * n_ops, out_shape=[pltpu.HBM(a.shape, a.dtype) for a in ops],
        input_output_aliases={k: k for k in range(n_ops)},
        compiler_params=pltpu.CompilerParams(has_side_effects=EFFECT),
    )(*ops, *sems, after)
    return list(outs[:n]), list(outs[n:])


def _sibling_copy(k, src_refs, zone_refs, sem_refs):
    x, y, cc = _place()
    return pltpu.make_async_remote_copy(src_ref=src_refs[k], dst_ref=zone_refs[k], send_sem=sem_refs[2 * k], recv_sem=sem_refs[2 * k + 1],
                                        device_id=(x, y, 1 - cc), device_id_type=MESH)


def sibling_start(parts, name):
    n = len(parts)
    ops = list(parts) + [lax.empty(p.shape, p.dtype) for p in parts]

    def body(*refs):
        for k in range(n):
            _sibling_copy(k, refs[:n], refs[n:2 * n], refs[2 * n:4 * n]).start()

    outs = pl.pallas_call(
        body, name=name, in_specs=[HBM_SPEC] * (2 * n),
        out_specs=[SEM_SPEC] * (2 * n) + [HBM_SPEC] * (2 * n),
        out_shape=[pltpu.SemaphoreType.DMA(())] * (2 * n) + [pltpu.HBM(a.shape, a.dtype) for a in ops],
        input_output_aliases={k: 2 * n + k for k in range(2 * n)},
        compiler_params=pltpu.CompilerParams(has_side_effects=EFFECT),
    )(*[_hbm(a) for a in ops])
    return list(outs[2 * n:3 * n]), list(outs[3 * n:]), list(outs[:2 * n])


def sibling_wait(parts, zones, sems, after, name):
    n = len(parts)

    def body(*refs):
        for k in range(n):
            cp = _sibling_copy(k, refs[:n], refs[n:2 * n], refs[2 * n:4 * n])
            cp.wait_send()
            cp.wait_recv()

    outs = pl.pallas_call(
        body, name=name, in_specs=[HBM_SPEC] * (2 * n) + [SEM_SPEC] * (2 * n) + [ANY],
        out_specs=[HBM_SPEC] * (2 * n), out_shape=[pltpu.HBM(a.shape, a.dtype) for a in list(parts) + list(zones)],
        input_output_aliases={k: k for k in range(2 * n)},
        compiler_params=pltpu.CompilerParams(has_side_effects=EFFECT),
    )(*parts, *zones, *sems, after)
    return list(outs[:n]), list(outs[n:])


SMALL_ROWS = 88
FIN_ROWS = 72


def small_finish(g3, g4, c_ctx, lbp, name):
    def body(g3_ref, g4_ref, cc_ref, lbp_ref, o_ref, s_ref):
        s = g3_ref[0]
        for k in range(1, 8):
            s = s + g3_ref[k]
        s_ref[...] = s
        for i in range(DEPTH):
            o_ref[8 * i:8 * i + 8, :] = s_ref[16 * i:16 * i + 8, :] + s_ref[16 * i + 8:16 * i + 16, :]
        acc = g4_ref[0]
        for k in (2, 4, 6):
            acc = acc + g4_ref[k]
        cc = cc_ref[...]
        sg = _sigmoid(cc)
        row = jnp.sum(acc, axis=0, keepdims=True) * (sg * (1.0 + cc * (1.0 - sg)))
        o_ref[32:40, :] = jnp.broadcast_to(row, (8, D))
        o_ref[40:64, :] = s_ref[64:88, :]
        o_ref[64:72, :] = jnp.zeros((8, D), F32)
        for d in range(2):
            pp = lbp_ref[2 * d:2 * d + 1, :] * lbp_ref[2 * d + 1:2 * d + 2, :] * s_ref[75 + d:76 + d, :]
            o_ref[64 + 2 * d:65 + 2 * d, :] = -pp
            o_ref[65 + 2 * d:66 + 2 * d, :] = pp

    return pl.pallas_call(
        body, name=name, in_specs=[VMEM_SPEC] * 4, out_specs=VMEM_SPEC,
        out_shape=jax.ShapeDtypeStruct((FIN_ROWS, D), F32),
        scratch_shapes=[pltpu.VMEM((SMALL_ROWS, D), F32)],
    )(g3, g4, c_ctx, lbp)


def _pack_rows(arrs):
    flat = jnp.concatenate([a.reshape(-1) for a in arrs])
    n = -(-flat.shape[0] // (8 * D)) * 8 * D
    return jnp.pad(flat, (0, n - flat.shape[0])).reshape(n // D, D)


def _unpack_rows(packed, shapes):
    flat = packed.reshape(-1)
    outs, off = [], 0
    for s in shapes:
        size = 1
        for k in s:
            size *= k
        outs.append(flat[off:off + size].reshape(s))
        off += size
    return outs


def _pad8(a):
    return jnp.pad(a, ((0, 8 - a.shape[0]), (0, 0)))


def kernel(x, c, ctx, c_ctx, ada_w, ada_b, norm1, norm2, norm_f, mlp_w1, mlp_w2, hgrn_w_in, hgrn_lb, hgrn_gnorm, hgrn_w_out, conv_w_in, conv_w, conv_b, conv_w_out, loss_target, m_c_ctx, m_ada_w, m_ada_b, m_norm1, m_norm2, m_norm_f, m_mlp_w1, m_mlp_w2, m_hgrn_w_in, m_hgrn_lb, m_hgrn_gnorm, m_hgrn_w_out, m_conv_w_in, m_conv_w, m_conv_b, m_conv_w_out, v_c_ctx, v_ada_w, v_ada_b, v_norm1, v_norm2, v_norm_f, v_mlp_w1, v_mlp_w2, v_hgrn_w_in, v_hgrn_lb, v_hgrn_gnorm, v_hgrn_w_out, v_conv_w_in, v_conv_w, v_conv_b, v_conv_w_out):
    xi, yi, ci = _place()
    me = 4 * xi + 2 * yi + ci
    chip = 2 * xi + yi
    seq = x.shape[1]
    assert ctx.shape[1] == TM and seq % TM == 0 and (seq + TM) % TMW == 0
    n_lat = seq // TM
    sd = D // 4
    nca = ada_w.shape[2]
    xs = jnp.concatenate([x[0], ctx[0]], axis=0)

    sh_rows = jnp.concatenate([hgrn_lb.reshape(4, sd), conv_w.reshape(6, sd), conv_b.reshape(2, sd)], axis=0)
    buf1 = jnp.concatenate([c, jnp.pad(sh_rows, ((0, 0), (0, D - sd))), jnp.zeros((3, D), F32)], axis=0)
    g1 = small_allgather(buf1, "gather_small_in")
    cvec = jnp.concatenate([g1[:, 0, :], jnp.broadcast_to(c_ctx[None], (8, D))], axis=0)
    shf = g1[0::2, 1:13, :sd].transpose(1, 0, 2).reshape(12, D)
    lb_p = jax.nn.softmax(shf[0:4].reshape(2, 2, D), axis=1)
    lower = jnp.cumsum(lb_p, axis=1) - lb_p[:, :1]
    lbs = [lower[:, 0], lower[:, 1]]
    cw8 = [_pad8(shf[4:7]), _pad8(shf[7:10])]
    cb = [shf[10:11], shf[11:12]]

    bias = lax.dynamic_slice_in_dim(ada_b, chip * nca, nca, axis=1).reshape(DEPTH, 1, nca)
    ada_part = ada_fwd(cvec, ada_w, bias, "ada_fwd")
    g2 = small_allgather(ada_part.reshape(DEPTH * 16, nca), "gather_ada")
    ada_full = g2[0::2].reshape(4, DEPTH, 16, nca).transpose(1, 2, 0, 3).reshape(DEPTH, 16, 4 * nca)
    lat = lax.dynamic_slice_in_dim(ada_full, me, 1, axis=1)[:, 0]
    mods = [jnp.stack([_pad8(lat[i].reshape(6, D)), _pad8(ada_full[i, 8].reshape(6, D))]) for i in range(DEPTH)]

    big = [(mlp_w1, m_mlp_w1, v_mlp_w1), (mlp_w2, m_mlp_w2, v_mlp_w2), (hgrn_w_in, m_hgrn_w_in, v_hgrn_w_in),
           (hgrn_w_out, m_hgrn_w_out, v_hgrn_w_out), (conv_w_in, m_conv_w_in, v_conv_w_in), (conv_w_out, m_conv_w_out, v_conv_w_out)]
    big_names = ["w1", "w2", "hin", "hout", "cin", "cout"]
    flat2 = lambda a: a.reshape(a.shape[0] * a.shape[1], a.shape[2])
    tensors = dict(zip(big_names, big))
    chip1 = jnp.reshape(chip, (1,)).astype(jnp.int32)
    order = []
    for i in range(DEPTH):
        order += [("hin", i // 2), ("hout", i // 2)] if i % 2 == 0 else [("cin", i // 2), ("cout", i // 2)]
        order += [("w1", i), ("w2", i)]
    lands = [cast_to_slot(flat2(tensors[n][0]), idx, tensors[n][0].shape[1], chip1, f"cast_{n}_{idx}") for n, idx in order]
    w_sems, lands, _, _ = split_start(lands, None, "gather_start")
    unit = {key: u for u, key in enumerate(order)}

    def wts(n, idx, after):
        u = unit[(n, idx)]
        (w,), _ = split_wait([lands[u]], None, w_sems[2 * u:2 * u + 2], after, f"gather_wait_{n}_{idx}")
        return w.reshape(w.shape[0] * w.shape[1], w.shape[2]) if n in ("w2", "hout", "cout") else w

    started = []

    def on_grads(i, tag, g):
        names = sorted(g)
        gs = [g[n].reshape(4, g[n].shape[0] * g[n].shape[1] // 4, g[n].shape[2]) for n in names]
        sems, zones, srcs, token = split_start([lax.empty(a.shape, BF16) for a in gs], gs, f"grad_start_{tag}_{i}")
        started.append(([(n, i if n in ("w1", "w2") else i // 2) for n in names], sems, zones, srcs))
        return token

    lane, dx, small, last_token = local_step(xs, loss_target[0], mods, norm1, norm2, norm_f[None], lbs, hgrn_gnorm, cw8, cb, wts, n_lat,
                                             on_grads)
    loss = lax.psum(0.5 * jnp.sum(lane) / D, ("x", "y", "c"))
    grad_x = dx[:seq][None]

    rows3 = jnp.concatenate(small["dmod"] + small["norm1"] + small["norm2"] + [small["norm_f"]] + small["gnorm"]
                            + [small["lb"][1]] + small["cw"] + small["cb"] + [jnp.tile(last_token[0:3], (1, D // 128))], axis=0)
    g3 = small_allgather(rows3, "gather_small_out")
    dmat = g3[:, :64].reshape(8, DEPTH, 2, 8, D)[:, :, :, :6].transpose(1, 2, 0, 3, 4).reshape(DEPTH, 16, 6 * D)
    dcols = lax.dynamic_slice_in_dim(dmat, chip * nca, nca, axis=2)
    g_ada_w, d_ada_w, nm_ada_w, nv_ada_w, acc4 = ada_bwd(cvec, dcols, ada_w, m_ada_w, v_ada_w, "ada_bwd")
    g4 = small_allgather(acc4, "gather_cctx")
    fin = small_finish(g3, g4, c_ctx[None], _pad8(lb_p.reshape(4, D)), "small_finish")
    cols = lambda a: lax.dynamic_slice_in_dim(a, chip * sd, sd, axis=a.ndim - 1)
    small_g = [fin[32], fin[0:32].reshape(DEPTH, 8, D)[:, :6].reshape(DEPTH, 6 * D), fin[40:44], fin[44:48], fin[48], fin[49:51],
               cols(fin[64:68].reshape(2, 2, D)), cols(fin[53:59].reshape(2, 3, D)), cols(fin[59:61])]
    small_w = [c_ctx, ada_b, norm1, norm2, norm_f, hgrn_gnorm, hgrn_lb, conv_w, conv_b]
    small_m = [m_c_ctx, m_ada_b, m_norm1, m_norm2, m_norm_f, m_hgrn_gnorm, m_hgrn_lb, m_conv_w, m_conv_b]
    small_v = [v_c_ctx, v_ada_b, v_norm1, v_norm2, v_norm_f, v_hgrn_gnorm, v_hgrn_lb, v_conv_w, v_conv_b]
    shapes = [w.shape for w in small_w]
    packed = adamw([_pack_rows(small_g)], _pack_rows(small_w), _pack_rows(small_m), _pack_rows(small_v), "adamw_small")
    s_g, s_d, s_m, s_v = [_unpack_rows(p, shapes) for p in packed]

    units = []
    for ks, sems, zones, srcs in started:
        units += [(key, sems[2 * u:2 * u + 2], zones[u], srcs[u]) for u, key in enumerate(ks)]
    late_keys = (("hin", 0), ("hout", 0))
    acc = {n: lax.empty(flat2(w).shape, F32) for n, (w, _, _) in tensors.items()}

    def finish_units(group, after, name):
        zones, srcs = split_wait([u[2] for u in group], [u[3] for u in group], [s for u in group for s in u[1]], after, name)
        for (key, _, _, _), zone, own in zip(group, zones, srcs):
            acc[key[0]] = sum_slots(own, zone, acc[key[0]], key[1], chip1, f"sum_{key[0]}_{key[1]}")

    finish_units([u for u in units if u[0] not in late_keys], fin, "grad_wait_early")
    early_names = ["w1", "w2", "cin", "cout"]
    late_names = ["hin", "hout"]
    sib_early = sibling_start([acc[n] for n in early_names], "sibling_start_early")
    finish_units([u for u in units if u[0] in late_keys], sib_early[0][-1], "grad_wait_late")
    sib_late = sibling_start([acc[n] for n in late_names], "sibling_start_late")
    results = {}

    def finish_tensors(names, sib, after, name):
        mine, other = sibling_wait(*sib, after, name)
        for n, pm, po in zip(names, mine, other):
            w, m, v = tensors[n]
            results[n] = [a.reshape(w.shape) for a in adamw([pm, po], flat2(w), flat2(m), flat2(v), f"adamw_{n}")]

    finish_tensors(early_names, sib_early, sib_late[0][-1], "sibling_wait_early")
    finish_tensors(late_names, sib_late, results["cout"][0], "sibling_wait_late")
    b_g, b_d, b_m, b_v = [[results[n][k] for n in big_names] for k in range(4)]

    def ordered(s, a, b):
        return [s[0], a, s[1], s[2], s[3], s[4], b[0], b[1], b[2], s[6], s[5], b[3], b[4], s[7], s[8], b[5]]

    return (loss, grad_x, *ordered(s_g, g_ada_w, b_g), *ordered(s_d, d_ada_w, b_d), *ordered(s_m, nm_ada_w, b_m),
            *ordered(s_v, nv_ada_w, b_v))
```

```python
import functools

import jax
import jax.numpy as jnp
from jax import lax
from jax.experimental import pallas as pl
from jax.experimental.pallas import tpu as pltpu

F32 = jnp.float32
BF16 = jnp.bfloat16
MESH = pl.DeviceIdType.MESH

D = 1024
HD = 128
NH = D // HD
CH = 64
TM = 256
TMW = 768
EPS = 1e-6
DEPTH = 4
VMEM_LIMIT = 56 * 1024 * 1024

ADAM_LR = 0.001
ADAM_B1 = 0.9
ADAM_B2 = 0.999
ADAM_EPS = 1e-08
ADAM_WD = 0.01
ADAM_STEP = 10


def _cp(n_grid):
    return pltpu.CompilerParams(dimension_semantics=("arbitrary",) * n_grid, vmem_limit_bytes=VMEM_LIMIT)


def _dot(a, b):
    return jnp.dot(a, b, preferred_element_type=F32)


def _dot_nt(a, b):
    return lax.dot_general(a, b, (((1,), (1,)), ((), ())), preferred_element_type=F32)


def _dot_tn(a, b):
    return lax.dot_general(a, b, (((0,), (0,)), ((), ())), preferred_element_type=F32)


def _sigmoid(z):
    return 1.0 / (1.0 + jnp.exp(-z))


def _norm_mod(x, gain, sh, sc):
    r = lax.rsqrt(jnp.mean(x * x, axis=-1, keepdims=True) + EPS)
    xn = x * r
    yn = xn * gain
    return r, xn, yn, yn * (1.0 + sc) + sh


def _row_spec(width):
    return pl.BlockSpec((TM, width), lambda i: (i, 0))


def _col_spec(col):
    return pl.BlockSpec((TM, D), lambda i: (i, col))


def _full_spec(shape):
    nd = len(shape)
    return pl.BlockSpec(shape, lambda i: (0,) * nd)


def _mod_spec(n_lat):
    return pl.BlockSpec((1, 8, D), lambda i: (i // n_lat, 0, 0))


def proj_fwd(x, gain, mod, m0, w4, n_lat, name):
    t = x.shape[0]
    nb, _, ns = w4.shape

    def body(x_ref, gain_ref, mod_ref, w_ref, p_ref):
        _, _, _, h = _norm_mod(x_ref[...], gain_ref[...], mod_ref[0, m0:m0 + 1, :], mod_ref[0, m0 + 1:m0 + 2, :])
        hb = h.astype(BF16)
        for c in range(nb):
            p_ref[:, c * ns:(c + 1) * ns] = _dot(hb, w_ref[c])

    return pl.pallas_call(
        body, name=name, grid=(t // TM,),
        in_specs=[_row_spec(D), _full_spec((1, D)), _mod_spec(n_lat), _full_spec(w4.shape)],
        out_specs=_row_spec(nb * ns),
        out_shape=jax.ShapeDtypeStruct((t, nb * ns), F32),
        compiler_params=_cp(1),
    )(x, gain, mod, w4)


def proj_bwd(parts, w4, x, gain, mod, m0, dx_in, n_lat, name):
    t = x.shape[0]
    nb, _, ns = w4.shape
    n = nb * ns
    n_parts = len(parts)
    widths = [p.shape[1] for p in parts]
    offs = [sum(widths[:k]) for k in range(n_parts)]
    assert sum(widths) == n
    single = n_parts == 1

    def body(*refs):
        part_refs = refs[:n_parts]
        w_ref, x_ref, gain_ref, mod_ref, dxin_ref = refs[n_parts:n_parts + 5]
        rest = refs[n_parts + 5:]
        if single:
            dx_ref, hb_ref, acc_ref = rest
            src = part_refs[0]
        else:
            dx_ref, hb_ref, acc_ref, dpb_ref = rest
            for p_ref, off, w in zip(part_refs, offs, widths):
                dpb_ref[:, off:off + w] = p_ref[...]
            src = dpb_ref
        i = pl.program_id(0)

        @pl.when(i == 0)
        def _():
            acc_ref[...] = jnp.zeros_like(acc_ref)

        gain = gain_ref[...]
        sc = mod_ref[0, m0 + 1:m0 + 2, :]
        r, xn, yn, h = _norm_mod(x_ref[...], gain, mod_ref[0, m0:m0 + 1, :], sc)
        hb_ref[...] = h.astype(BF16)
        dh = _dot_nt(src[:, 0:ns], w_ref[0])
        for c in range(1, nb):
            dh = dh + _dot_nt(src[:, c * ns:(c + 1) * ns], w_ref[c])
        dsh = jnp.sum(dh, axis=0, keepdims=True)
        dsc = jnp.sum(dh * yn, axis=0, keepdims=True)
        dyn = dh * (1.0 + sc)
        dgain = jnp.sum(dyn * xn, axis=0, keepdims=True)
        dxn = dyn * gain
        dx = r * (dxn - xn * jnp.mean(dxn * xn, axis=-1, keepdims=True))
        dx_ref[...] = dxin_ref[...] + dx
        latf = (i < n_lat).astype(F32)
        ctxf = 1.0 - latf
        acc_ref[0:1, :] += dgain
        acc_ref[1:2, :] += dsh * latf
        acc_ref[2:3, :] += dsc * latf
        acc_ref[3:4, :] += dsh * ctxf
        acc_ref[4:5, :] += dsc * ctxf

    out_specs = [_row_spec(D), _row_spec(D), _full_spec((8, D))]
    out_shape = [jax.ShapeDtypeStruct((t, D), F32), jax.ShapeDtypeStruct((t, D), BF16), jax.ShapeDtypeStruct((8, D), F32)]
    if not single:
        out_specs.append(_row_spec(n))
        out_shape.append(jax.ShapeDtypeStruct((t, n), BF16))
    outs = pl.pallas_call(
        body, name=name, grid=(t // TM,),
        in_specs=[_row_spec(w) for w in widths]
        + [_full_spec(w4.shape), _row_spec(D), _full_spec((1, D)), _mod_spec(n_lat), _row_spec(D)],
        out_specs=out_specs, out_shape=out_shape, compiler_params=_cp(1),
    )(*parts, w4, x, gain, mod, dx_in)
    if single:
        return outs[0], outs[1], parts[0], outs[2]
    return outs[0], outs[1], outs[3], outs[2]


def dw_tn(a, b, nb, a_blocked, dep, name):
    t = a.shape[0]
    ka = a.shape[1] // nb if a_blocked else a.shape[1]
    kb = b.shape[1] if a_blocked else b.shape[1] // nb
    n_k = t // TMW

    def body(a_ref, b_ref, _, o_ref, acc):
        k = pl.program_id(1)

        @pl.when(k == 0)
        def _():
            acc[...] = jnp.zeros_like(acc)

        acc[...] += _dot_tn(a_ref[...], b_ref[...])

        @pl.when(k == n_k - 1)
        def _():
            o_ref[0] = acc[...].astype(BF16)

    a_spec = pl.BlockSpec((TMW, ka), (lambda j, k: (k, j)) if a_blocked else (lambda j, k: (k, 0)))
    b_spec = pl.BlockSpec((TMW, kb), (lambda j, k: (k, 0)) if a_blocked else (lambda j, k: (k, j)))
    return pl.pallas_call(
        body, name=name, grid=(nb, n_k),
        in_specs=[a_spec, b_spec, ANY],
        out_specs=pl.BlockSpec((1, ka, kb), lambda j, k: (j, 0, 0)),
        out_shape=jax.ShapeDtypeStruct((nb, ka, kb), BF16),
        scratch_shapes=[pltpu.VMEM((ka, kb), F32)],
        compiler_params=_cp(2),
    )(a, b, dep)


def outproj_fwd(prologue, extras, extra_specs, w, x, mod, m0, n_lat, name):
    t = x.shape[0]
    k = w.shape[0]
    n_extra = len(extras)

    def body(*refs):
        ex = refs[:n_extra]
        w_ref, x_ref, mod_ref, xo_ref, y_ref, ab_ref = refs[n_extra:]
        ab = prologue(pl.program_id(0), *ex).astype(BF16)
        ab_ref[...] = ab
        y = _dot(ab, w_ref[...])
        y_ref[...] = y
        xo_ref[...] = x_ref[...] + mod_ref[0, m0 + 2:m0 + 3, :] * y

    return pl.pallas_call(
        body, name=name, grid=(t // TM,),
        in_specs=list(extra_specs) + [_full_spec(w.shape), _row_spec(D), _mod_spec(n_lat)],
        out_specs=[_row_spec(D), _row_spec(D), _row_spec(k)],
        out_shape=[jax.ShapeDtypeStruct((t, D), F32), jax.ShapeDtypeStruct((t, D), F32), jax.ShapeDtypeStruct((t, k), BF16)],
        compiler_params=_cp(1),
    )(*extras, w, x, mod)


def outproj_bwd(epilogue, extras, extra_specs, ep_out_specs, ep_out_shapes, w, dxn, y, mod, m0, n_lat, dep, name):
    t = dxn.shape[0]
    n_extra = len(extras)

    def body(*refs):
        ex = refs[:n_extra]
        w_ref, dxn_ref, y_ref, mod_ref, _, dyb_ref, acc_ref = refs[n_extra:n_extra + 7]
        ep_outs = refs[n_extra + 7:]
        i = pl.program_id(0)

        @pl.when(i == 0)
        def _():
            acc_ref[...] = jnp.zeros_like(acc_ref)

        dxv = dxn_ref[...]
        dyb = (dxv * mod_ref[0, m0 + 2:m0 + 3, :]).astype(BF16)
        dyb_ref[...] = dyb
        dg = jnp.sum(dxv * y_ref[...], axis=0, keepdims=True)
        latf = (i < n_lat).astype(F32)
        acc_ref[0:1, :] += dg * latf
        acc_ref[1:2, :] += dg * (1.0 - latf)
        epilogue(i, _dot_nt(dyb, w_ref[...]), ex, ep_outs, acc_ref)

    outs = pl.pallas_call(
        body, name=name, grid=(t // TM,),
        in_specs=list(extra_specs) + [_full_spec(w.shape), _row_spec(D), _row_spec(D), _mod_spec(n_lat), ANY],
        out_specs=[_row_spec(D), _full_spec((8, D))] + list(ep_out_specs),
        out_shape=[jax.ShapeDtypeStruct((t, D), BF16), jax.ShapeDtypeStruct((8, D), F32)] + list(ep_out_shapes),
        compiler_params=_cp(1),
    )(*extras, w, dxn, y, mod, dep)
    return outs[0], outs[1], outs[2:]


def mlp_fwd(x, gain, mod, w1, w2, n_lat, name):
    t = x.shape[0]
    nb, _, ns = w1.shape

    def body(x_ref, gain_ref, mod_ref, w1_ref, w2_ref, xo_ref, y_ref, ab_ref):
        x = x_ref[...]
        _, _, _, h = _norm_mod(x, gain_ref[...], mod_ref[0, 3:4, :], mod_ref[0, 4:5, :])
        hb = h.astype(BF16)
        y = None
        for c in range(nb):
            a = jnp.square(jnp.maximum(_dot(hb, w1_ref[c]), 0.0)).astype(BF16)
            ab_ref[:, c * ns:(c + 1) * ns] = a
            yc = _dot(a, w2_ref[c * ns:(c + 1) * ns, :])
            y = yc if y is None else y + yc
        y_ref[...] = y
        xo_ref[...] = x + mod_ref[0, 5:6, :] * y

    return pl.pallas_call(
        body, name=name, grid=(t // TM,),
        in_specs=[_row_spec(D), _full_spec((1, D)), _mod_spec(n_lat), _full_spec(w1.shape), _full_spec(w2.shape)],
        out_specs=[_row_spec(D), _row_spec(D), _row_spec(nb * ns)],
        out_shape=[jax.ShapeDtypeStruct((t, D), F32), jax.ShapeDtypeStruct((t, D), F32), jax.ShapeDtypeStruct((t, nb * ns), BF16)],
        compiler_params=_cp(1),
    )(x, gain, mod, w1, w2)


def mlp_bwd(dxn, y, ab, x, gain, mod, w1, w2, n_lat, dep, name):
    t = x.shape[0]
    nb, _, ns = w1.shape

    def body(dxn_ref, y_ref, ab_ref, x_ref, gain_ref, mod_ref, w1_ref, w2_ref, _, dx_ref, dyb_ref, dp_ref, hb_ref, acc_ref):
        i = pl.program_id(0)

        @pl.when(i == 0)
        def _():
            acc_ref[...] = jnp.zeros_like(acc_ref)

        dxv = dxn_ref[...]
        dyb = (dxv * mod_ref[0, 5:6, :]).astype(BF16)
        dyb_ref[...] = dyb
        dg = jnp.sum(dxv * y_ref[...], axis=0, keepdims=True)
        gain = gain_ref[...]
        sc = mod_ref[0, 4:5, :]
        r, xn, yn, h = _norm_mod(x_ref[...], gain, mod_ref[0, 3:4, :], sc)
        hb_ref[...] = h.astype(BF16)
        dh = None
        for c in range(nb):
            cols = slice(c * ns, (c + 1) * ns)
            da = _dot_nt(dyb, w2_ref[cols, :])
            dp = (da * (2.0 * jnp.sqrt(ab_ref[:, cols].astype(F32)))).astype(BF16)
            dp_ref[:, cols] = dp
            d = _dot_nt(dp, w1_ref[c])
            dh = d if dh is None else dh + d
        dsh = jnp.sum(dh, axis=0, keepdims=True)
        dsc = jnp.sum(dh * yn, axis=0, keepdims=True)
        dyn = dh * (1.0 + sc)
        dgain = jnp.sum(dyn * xn, axis=0, keepdims=True)
        dxn_ = dyn * gain
        dx_ref[...] = dxv + r * (dxn_ - xn * jnp.mean(dxn_ * xn, axis=-1, keepdims=True))
        latf = (i < n_lat).astype(F32)
        ctxf = 1.0 - latf
        acc_ref[0:1, :] += dgain
        acc_ref[1:2, :] += dsh * latf
        acc_ref[2:3, :] += dsc * latf
        acc_ref[3:4, :] += dsh * ctxf
        acc_ref[4:5, :] += dsc * ctxf
        acc_ref[5:6, :] += dg * latf
        acc_ref[6:7, :] += dg * ctxf

    return pl.pallas_call(
        body, name=name, grid=(t // TM,),
        in_specs=[_row_spec(D), _row_spec(D), _row_spec(nb * ns), _row_spec(D), _full_spec((1, D)), _mod_spec(n_lat),
                  _full_spec(w1.shape), _full_spec(w2.shape), ANY],
        out_specs=[_row_spec(D), _row_spec(D), _row_spec(nb * ns), _row_spec(D), _full_spec((8, D))],
        out_shape=[jax.ShapeDtypeStruct((t, D), F32), jax.ShapeDtypeStruct((t, D), BF16), jax.ShapeDtypeStruct((t, nb * ns), BF16),
                   jax.ShapeDtypeStruct((t, D), BF16), jax.ShapeDtypeStruct((8, D), F32)],
        compiler_params=_cp(1),
    )(dxn, y, ab, x, gain, mod, w1, w2, dep)


def readout_prologue(i, o0_ref, o1_ref, gate_ref, gn_ref):
    o = o0_ref[...] + o1_ref[...]
    gate = gate_ref[...]
    w = gn_ref[...] * (gate * _sigmoid(gate))
    pieces = []
    for h in range(NH):
        sl = slice(h * HD, (h + 1) * HD)
        oh = o[:, sl]
        pieces.append(oh * lax.rsqrt(jnp.mean(oh * oh, axis=-1, keepdims=True) + EPS) * w[:, sl])
    return jnp.concatenate(pieces, axis=1)


def readout_epilogue(i, da, ex, outs, acc_ref):
    o0_ref, o1_ref, gate_ref, gn_ref = ex
    do_ref, dgate_ref = outs
    o = o0_ref[...] + o1_ref[...]
    gate = gate_ref[...]
    gn = gn_ref[...]
    sg = _sigmoid(gate)
    silu = gate * sg
    dsilu = sg * (1.0 + gate * (1.0 - sg))
    for h in range(NH):
        sl = slice(h * HD, (h + 1) * HD)
        oh = o[:, sl]
        r = lax.rsqrt(jnp.mean(oh * oh, axis=-1, keepdims=True) + EPS)
        nh = oh * r
        dah = da[:, sl]
        acc_ref[2:3, sl] += jnp.sum(dah * nh * silu[:, sl], axis=0, keepdims=True)
        dgate_ref[:, sl] = (dah * nh * gn[:, sl] * dsilu[:, sl]).astype(BF16)
        dn = dah * gn[:, sl] * silu[:, sl]
        do_ref[:, sl] = r * (dn - nh * jnp.mean(dn * nh, axis=-1, keepdims=True))


def _seg_masks(i, n_lat):
    rows = lax.broadcasted_iota(jnp.int32, (TM, 1), 0)
    latf = (i < n_lat).astype(F32)
    ctxf = 1.0 - latf
    prev_ok = (rows % CH != 0).astype(F32) * latf + (rows != 0).astype(F32) * ctxf
    next_ok = (rows % CH != CH - 1).astype(F32) * latf + (rows != TM - 1).astype(F32) * ctxf
    return prev_ok, next_ok


def _shifts(i, n_lat, sft, cur, halo_prev, halo_next):
    if sft == 1:
        prev_ok, next_ok = _seg_masks(i, n_lat)
        return pltpu.roll(cur, 1, 0) * prev_ok, pltpu.roll(cur, TM - 1, 0) * next_ok
    has_prev = jnp.logical_and(i > 0, i < n_lat).astype(F32)
    has_next = (i < n_lat - 1).astype(F32)
    prev = jnp.concatenate([halo_prev * has_prev, cur[:TM - CH]], axis=0)
    nxt = jnp.concatenate([cur[CH:], halo_next * has_next], axis=0)
    return prev, nxt


def _conv_u(sft, ex):
    if sft == 1:
        gb_ref, gc_ref, xi_ref, cw_ref, cb_ref = ex
        return gb_ref, gc_ref[...] * xi_ref[...], None, None, cw_ref, cb_ref
    gb_ref, gc_ref, xi_ref, gcp_ref, xip_ref, gcn_ref, xin_ref, cw_ref, cb_ref = ex
    return gb_ref, gc_ref[...] * xi_ref[...], gcp_ref[...] * xip_ref[...], gcn_ref[...] * xin_ref[...], cw_ref, cb_ref


def _conv_value(i, n_lat, sft, ex):
    gb_ref, u, up, un, cw_ref, cb_ref = _conv_u(sft, ex)
    u_prev, u_next = _shifts(i, n_lat, sft, u, up, un)
    return gb_ref, cb_ref[...] + cw_ref[0:1, :] * u_prev + cw_ref[1:2, :] * u + cw_ref[2:3, :] * u_next


def make_conv_prologue(n_lat, sft):
    def prologue(i, *ex):
        gb_ref, conv = _conv_value(i, n_lat, sft, ex)
        return gb_ref[...] * conv
    return prologue


def make_conv_epilogue(n_lat, sft):
    def epilogue(i, da, ex, outs, acc_ref):
        gb_ref, conv = _conv_value(i, n_lat, sft, ex)
        outs[0][...] = da * gb_ref[...]
        outs[1][...] = (da * conv).astype(BF16)
    return epilogue


def _conv_specs(sft, t):
    specs = [_col_spec(0), _col_spec(1), _col_spec(2)]
    if sft != 1:
        per = TM // CH
        last = t // CH - 1
        for fn in (lambda i: jnp.maximum(i * per - 1, 0), lambda i: jnp.minimum(i * per + per, last)):
            for col in (1, 2):
                specs.append(pl.BlockSpec((CH, D), functools.partial(lambda i, f, c: (f(i), c), f=fn, c=col)))
    return specs + [_full_spec((8, D)), _full_spec((1, D))]


def _conv_args(sft, p, cw8, cb):
    return [p] * (3 if sft == 1 else 7) + [cw8, cb]


def conv_bwd(dconv, p, cw8, sft, n_lat, name):
    t = dconv.shape[0]
    halo = sft != 1

    def body(*refs):
        if halo:
            dc_ref, dcp_ref, dcn_ref, gc_ref, xi_ref, gcp_ref, xip_ref, gcn_ref, xin_ref, cw_ref, dgc_ref, dxi_ref, acc_ref = refs
            up, un = gcp_ref[...] * xip_ref[...], gcn_ref[...] * xin_ref[...]
            dcp, dcn = dcp_ref[...], dcn_ref[...]
        else:
            dc_ref, gc_ref, xi_ref, cw_ref, dgc_ref, dxi_ref, acc_ref = refs
            up = un = dcp = dcn = None
        i = pl.program_id(0)

        @pl.when(i == 0)
        def _():
            acc_ref[...] = jnp.zeros_like(acc_ref)

        gc = gc_ref[...]
        xi = xi_ref[...]
        u = gc * xi
        dc = dc_ref[...]
        u_prev, u_next = _shifts(i, n_lat, sft, u, up, un)
        dc_prev, dc_next = _shifts(i, n_lat, sft, dc, dcp, dcn)
        acc_ref[0:1, :] += jnp.sum(dc * u_prev, axis=0, keepdims=True)
        acc_ref[1:2, :] += jnp.sum(dc * u, axis=0, keepdims=True)
        acc_ref[2:3, :] += jnp.sum(dc * u_next, axis=0, keepdims=True)
        acc_ref[3:4, :] += jnp.sum(dc, axis=0, keepdims=True)
        du = cw_ref[0:1, :] * dc_next + cw_ref[1:2, :] * dc + cw_ref[2:3, :] * dc_prev
        dgc_ref[...] = (du * xi).astype(BF16)
        dxi_ref[...] = (du * gc).astype(BF16)

    per = TM // CH
    last = t // CH - 1
    prev_i = lambda i: jnp.maximum(i * per - 1, 0)
    next_i = lambda i: jnp.minimum(i * per + per, last)
    if halo:
        in_specs = [_row_spec(D), pl.BlockSpec((CH, D), lambda i: (prev_i(i), 0)), pl.BlockSpec((CH, D), lambda i: (next_i(i), 0)),
                    _col_spec(1), _col_spec(2),
                    pl.BlockSpec((CH, D), lambda i: (prev_i(i), 1)), pl.BlockSpec((CH, D), lambda i: (prev_i(i), 2)),
                    pl.BlockSpec((CH, D), lambda i: (next_i(i), 1)), pl.BlockSpec((CH, D), lambda i: (next_i(i), 2)),
                    _full_spec((8, D))]
        args = [dconv, dconv, dconv, p, p, p, p, p, p, cw8]
    else:
        in_specs = [_row_spec(D), _col_spec(1), _col_spec(2), _full_spec((8, D))]
        args = [dconv, p, p, cw8]
    return pl.pallas_call(
        body, name=name, grid=(t // TM,), in_specs=in_specs,
        out_specs=[_row_spec(D), _row_spec(D), _full_spec((8, D))],
        out_shape=[jax.ShapeDtypeStruct((t, D), BF16), jax.ShapeDtypeStruct((t, D), BF16), jax.ShapeDtypeStruct((8, D), F32)],
        compiler_params=_cp(1),
    )(*args)


LOG2E = 1.4426950408889634


def _cumsum_matrix(reverse):
    r = lax.broadcasted_iota(jnp.int32, (CH, CH), 0)
    c = lax.broadcasted_iota(jnp.int32, (CH, CH), 1)
    return (r <= c if reverse else r >= c).astype(BF16)


def _chunk_cumsum(g, tri):
    hi = g.astype(BF16)
    lo = (g - hi.astype(F32)).astype(BF16)
    return _dot(tri, hi) + _dot(tri, lo)


def _gate_values(z, lb):
    sig = _sigmoid(z)
    f = lb + (1.0 - lb) * sig
    return sig, f


def _tri(direction, transposed):
    r = lax.broadcasted_iota(jnp.int32, (CH, CH), 0)
    c = lax.broadcasted_iota(jnp.int32, (CH, CH), 1)
    lower = (direction == 0) != transposed
    return r >= c if lower else r <= c


def _gla_rows(direction):
    return (CH // 2 - 1, CH - 1) if direction == 0 else (CH // 2, 0)


def gla_fwd(p, lb2, direction, name):
    t = p.shape[0]
    nt = t // TM
    per = TM // CH
    ref_row, last_row = _gla_rows(direction)
    tile = (lambda i: (i + nt - 1) % nt) if direction == 0 else (lambda i: nt - 1 - i)

    def body(z_ref, v_ref, qr_ref, lb_ref, o_ref, s_ref, st, q_s, k_s, c_s):
        @pl.when(pl.program_id(0) == 0)
        def _():
            st[...] = jnp.zeros_like(st)

        mask = _tri(direction, False)
        tri = _cumsum_matrix(direction == 1)
        lb = lb_ref[direction:direction + 1, :]
        for ci in range(per):
            rows = slice(ci * CH, (ci + 1) * CH)
            _, f = _gate_values(z_ref[rows, :], lb)
            k_s[rows, :] = 1.0 - f
            c_s[rows, :] = _chunk_cumsum(jnp.log(f) * LOG2E, tri)
            qr = qr_ref[rows, :]
            q_s[rows, :] = qr * _sigmoid(qr)
        for it in range(per):
            ci = it if direction == 0 else per - 1 - it
            r0 = ci * CH
            rows = slice(r0, r0 + CH)
            cum = c_s[rows, :]
            ref = c_s[r0 + ref_row:r0 + ref_row + 1, :]
            last = c_s[r0 + last_row:r0 + last_row + 1, :]
            q = q_s[rows, :]
            k = k_s[rows, :]
            qh = (q * jnp.exp2(cum)).astype(BF16)
            qt = (q * jnp.exp2(cum - ref)).astype(BF16)
            kt = (k * jnp.exp2(ref - cum)).astype(BF16)
            kb = (k * jnp.exp2(last - cum)).astype(BF16)
            el = jnp.exp2(last)
            vb = v_ref[rows, :].astype(BF16)
            for h in range(NH):
                sl = slice(h * HD, (h + 1) * HD)
                s_t = st[h]
                s_ref[ci, h] = s_t
                sc = jnp.where(mask, _dot_nt(qt[:, sl], kt[:, sl]), 0.0)
                o_ref[rows, sl] = _dot_nt(qh[:, sl], s_t.astype(BF16)) + _dot(sc.astype(BF16), vb[:, sl])
                st[h] = s_t * el[:, sl] + _dot_tn(vb[:, sl], kb[:, sl])

    tspec = lambda col: pl.BlockSpec((TM, D), lambda i: (tile(i), col))
    return pl.pallas_call(
        body, name=name, grid=(nt,),
        in_specs=[tspec(direction), tspec(2), tspec(3), _full_spec((2, D))],
        out_specs=[pl.BlockSpec((TM, D), lambda i: (tile(i), 0)), pl.BlockSpec((per, NH, HD, HD), lambda i: (tile(i), 0, 0, 0))],
        out_shape=[jax.ShapeDtypeStruct((t, D), F32), jax.ShapeDtypeStruct((t // CH, NH, HD, HD), F32)],
        scratch_shapes=[pltpu.VMEM((NH, HD, HD), F32)] + [pltpu.VMEM((TM, D), F32)] * 3,
        compiler_params=_cp(1),
    )(p, p, p, lb2)


def gla_bwd(p, lb2, do, states, direction, prev, name):
    t = p.shape[0]
    nt = t // TM
    per = TM // CH
    ref_row, last_row = _gla_rows(direction)
    tile = (lambda i: (2 * nt - 2 - i) % nt) if direction == 0 else (lambda i: i)
    final = prev is not None
    n_in = 8 if final else 6

    def body(*refs):
        z_ref, v_ref, qr_ref, lb_ref, do_ref, s_ref = refs[:6]
        dz_ref, dv_ref, dq_ref, acc_ref, dst, q_s, k_s, c_s, dq_s, dk_s, dl_s = refs[n_in:]

        @pl.when(pl.program_id(0) == 0)
        def _():
            dst[...] = jnp.zeros_like(dst)
            acc_ref[...] = jnp.zeros_like(acc_ref)

        mask = _tri(direction, False)
        mask_t = _tri(direction, True)
        tri = _cumsum_matrix(direction == 1)
        tri_t = _cumsum_matrix(direction == 0)
        is_last = lax.broadcasted_iota(jnp.int32, (CH, 1), 0) == last_row
        lb = lb_ref[direction:direction + 1, :]
        for ci in range(per):
            rows = slice(ci * CH, (ci + 1) * CH)
            _, f = _gate_values(z_ref[rows, :], lb)
            k_s[rows, :] = 1.0 - f
            c_s[rows, :] = _chunk_cumsum(jnp.log(f) * LOG2E, tri)
            qr = qr_ref[rows, :]
            q_s[rows, :] = qr * _sigmoid(qr)
        for it in range(per):
            ci = per - 1 - it if direction == 0 else it
            r0 = ci * CH
            rows = slice(r0, r0 + CH)
            cum = c_s[rows, :]
            ref = c_s[r0 + ref_row:r0 + ref_row + 1, :]
            last = c_s[r0 + last_row:r0 + last_row + 1, :]
            q = q_s[rows, :]
            k = k_s[rows, :]
            e_h = jnp.exp2(cum)
            e_t = jnp.exp2(cum - ref)
            e_kt = jnp.exp2(ref - cum)
            e_kb = jnp.exp2(last - cum)
            el = jnp.exp2(last)
            qh = (q * e_h).astype(BF16)
            qt = (q * e_t).astype(BF16)
            kt = (k * e_kt).astype(BF16)
            kbf = k * e_kb
            kb = kbf.astype(BF16)
            vb = v_ref[rows, :].astype(BF16)
            dob = do_ref[rows, :].astype(BF16)
            for h in range(NH):
                sl = slice(h * HD, (h + 1) * HD)
                s_t = s_ref[ci, h]
                ds_t = dst[h]
                ds_b = ds_t.astype(BF16)
                d_a = jnp.where(mask, _dot_nt(dob[:, sl], vb[:, sl]), 0.0).astype(BF16)
                a_t = jnp.where(mask_t, _dot_nt(kt[:, sl], qt[:, sl]), 0.0).astype(BF16)
                d_at = jnp.where(mask_t, _dot_nt(vb[:, sl], dob[:, sl]), 0.0).astype(BF16)
                dv = _dot(a_t, dob[:, sl]) + _dot_nt(kb[:, sl], ds_b)
                dkb = _dot(vb[:, sl], ds_b)
                dl_s[it:it + 1, sl] = (el[:, sl] * jnp.sum(ds_t * s_t, axis=0, keepdims=True)
                                       + jnp.sum(dkb * kbf[:, sl], axis=0, keepdims=True))
                dst[h] = ds_t * el[:, sl] + _dot_tn(dob[:, sl], qh[:, sl])
                dq_s[rows, sl] = _dot(dob[:, sl], s_t.astype(BF16)) * e_h[:, sl] + _dot(d_a, kt[:, sl]) * e_t[:, sl]
                dk_s[rows, sl] = _dot(d_at, qt[:, sl]) * e_kt[:, sl] + dkb * e_kb[:, sl]
                if final:
                    dv_ref[rows, sl] = (refs[6][rows, sl] + dv).astype(BF16)
                else:
                    dv_ref[rows, sl] = dv
        for it in range(per):
            ci = per - 1 - it if direction == 0 else it
            rows = slice(ci * CH, (ci + 1) * CH)
            dq = dq_s[rows, :]
            dk = dk_s[rows, :]
            dg = _chunk_cumsum(dq * q_s[rows, :] - dk * k_s[rows, :] + jnp.where(is_last, dl_s[it:it + 1, :], 0.0), tri_t)
            sig, f = _gate_values(z_ref[rows, :], lb)
            df = dg / f - dk
            acc_ref[0:1, :] += jnp.sum(df * (1.0 - sig), axis=0, keepdims=True)
            dz_ref[rows, :] = (df * (1.0 - lb) * sig * (1.0 - sig)).astype(BF16)
            if final:
                qr = qr_ref[rows, :]
                sq = _sigmoid(qr)
                dq_ref[rows, :] = ((refs[7][rows, :] + dq) * (sq * (1.0 + qr * (1.0 - sq)))).astype(BF16)
            else:
                dq_ref[rows, :] = dq

    tspec = lambda col: pl.BlockSpec((TM, D), lambda i: (tile(i), col))
    sspec = pl.BlockSpec((per, NH, HD, HD), lambda i: (tile(i), 0, 0, 0))
    in_specs = [tspec(direction), tspec(2), tspec(3), _full_spec((2, D)), tspec(0), sspec]
    args = [p, p, p, lb2, do, states]
    if final:
        in_specs += [tspec(0), tspec(0)]
        args += list(prev)
    odt = BF16 if final else F32
    return pl.pallas_call(
        body, name=name, grid=(nt,), in_specs=in_specs,
        out_specs=[tspec(0), tspec(0), tspec(0), _full_spec((8, D))],
        out_shape=[jax.ShapeDtypeStruct((t, D), BF16), jax.ShapeDtypeStruct((t, D), odt), jax.ShapeDtypeStruct((t, D), odt),
                   jax.ShapeDtypeStruct((8, D), F32)],
        scratch_shapes=[pltpu.VMEM((NH, HD, HD), F32)] + [pltpu.VMEM((TM, D), F32)] * 5 + [pltpu.VMEM((8, D), F32)],
        compiler_params=_cp(1),
    )(*args)


def loss_bwd(x, gain, target, n_lat, name):
    t = x.shape[0]

    def body(x_ref, gain_ref, tg_ref, dx_ref, acc_ref):
        i = pl.program_id(0)

        @pl.when(i == 0)
        def _():
            acc_ref[...] = jnp.zeros_like(acc_ref)

        latf = (i < n_lat).astype(F32)
        x = x_ref[...]
        gain = gain_ref[...]
        r = lax.rsqrt(jnp.mean(x * x, axis=-1, keepdims=True) + EPS)
        xn = x * r
        err = (xn * gain - tg_ref[...]) * latf
        dy = err * (1.0 / D)
        dxn = dy * gain
        dx_ref[...] = r * (dxn - xn * jnp.mean(dxn * xn, axis=-1, keepdims=True))
        acc_ref[0:1, :] += jnp.sum(dy * xn, axis=0, keepdims=True)
        acc_ref[1:2, :] += jnp.sum(err * err, axis=0, keepdims=True)

    return pl.pallas_call(
        body, name=name, grid=(t // TM,),
        in_specs=[_row_spec(D), _full_spec((1, D)), pl.BlockSpec((TM, D), lambda i: (jnp.minimum(i, n_lat - 1), 0))],
        out_specs=[_row_spec(D), _full_spec((8, D))],
        out_shape=[jax.ShapeDtypeStruct((t, D), F32), jax.ShapeDtypeStruct((8, D), F32)],
        compiler_params=_cp(1),
    )(x, gain, target)


def local_step(xs, target, mods, norm1, norm2, norm_f, lbs, gnorm, cw8, cb, wts, n_lat, on_grads, after_backward):
    t = xs.shape[0]
    saved = []
    cache = {}

    def W(name, idx, after=None):
        if (name, idx) not in cache:
            cache[(name, idx)] = wts(name, idx, after)
        return cache[(name, idx)]

    x = xs
    for i in range(DEPTH):
        j = i // 2
        rec = i % 2 == 0
        n1 = norm1[i:i + 1]
        n2 = norm2[i:i + 1]
        s = {"x_in": x}
        if rec:
            p = proj_fwd(x, n1, mods[i], 0, W("hin", j, x), n_lat, f"hin_fwd_{i}")
            o0, st0 = gla_fwd(p, lbs[j], 0, f"gla_fwd0_{i}")
            o1, st1 = gla_fwd(p, lbs[j], 1, f"gla_fwd1_{i}")
            ex = [o0, o1, p, gnorm[j:j + 1]]
            ex_specs = [_row_spec(D), _row_spec(D), _col_spec(4), _full_spec((1, D))]
            xm, y, ab = outproj_fwd(readout_prologue, ex, ex_specs, W("hout", j, o1), x, mods[i], 0, n_lat, f"hout_fwd_{i}")
            s.update(st0=st0, st1=st1)
        else:
            sft = 1 if j % 2 == 0 else CH
            p = proj_fwd(x, n1, mods[i], 0, W("cin", j, x), n_lat, f"cin_fwd_{i}")
            ex = _conv_args(sft, p, cw8[j], cb[j])
            ex_specs = _conv_specs(sft, t)
            xm, y, ab = outproj_fwd(make_conv_prologue(n_lat, sft), ex, ex_specs, W("cout", j, p), x, mods[i], 0, n_lat, f"cout_fwd_{i}")
        s.update(p=p, ex=ex, ex_specs=ex_specs, y_mix=y, ab_mix=ab, x_mid=xm)
        x, y2, ab2 = mlp_fwd(xm, n2, mods[i], W("w1", i, xm), W("w2", i, xm), n_lat, f"mlp_fwd_{i}")
        s.update(y_mlp=y2, ab_mlp=ab2)
        saved.append(s)

    dx, acc_loss = loss_bwd(x, norm_f, target, n_lat, "loss")
    small = {"norm_f": acc_loss[0:1], "norm1": [None] * DEPTH, "norm2": [None] * DEPTH, "dmod": [None] * DEPTH,
             "gnorm": [None] * 2, "lb": [None] * 2, "cw": [None] * 2, "cb": [None] * 2}
    bshape = lambda w: jax.ShapeDtypeStruct((t, w), BF16)
    token = jnp.zeros((8, 128), F32)
    for i in reversed(range(DEPTH)):
        j = i // 2
        rec = i % 2 == 0
        s = saved[i]
        n1 = norm1[i:i + 1]
        n2 = norm2[i:i + 1]
        dx, dyb, dp1, hb, acc_n2 = mlp_bwd(dx, s["y_mlp"], s["ab_mlp"], s["x_mid"], n2, mods[i], W("w1", i), W("w2", i), n_lat, token,
                                           f"mlp_bwd_{i}")
        token = on_grads(i, "mlp", {"w2": dw_tn(s["ab_mlp"], dyb, 4, True, token, f"w2_dw_{i}"),
                                    "w1": dw_tn(hb, dp1, 4, False, token, f"w1_dw_{i}")})
        if rec:
            dyb, acc_g1, (do, dgate) = outproj_bwd(
                readout_epilogue, s["ex"], s["ex_specs"], [_row_spec(D), _row_spec(D)],
                [jax.ShapeDtypeStruct((t, D), F32), bshape(D)], W("hout", j), dx, s["y_mix"], mods[i], 0, n_lat, token, f"hout_bwd_{i}")
            dz0, dv0, dq0, acc_l0 = gla_bwd(s["p"], lbs[j], do, s["st0"], 0, None, f"gla_bwd0_{i}")
            dz1, dv, dq, acc_l1 = gla_bwd(s["p"], lbs[j], do, s["st1"], 1, (dv0, dq0), f"gla_bwd1_{i}")
            dx, hb, dpb, acc_n1 = proj_bwd([dz0, dz1, dv, dq, dgate], W("hin", j), s["x_in"], n1, mods[i], 0, dx, n_lat, f"hin_bwd_{i}")
            small["gnorm"][j] = acc_g1[2:3]
            small["lb"][j] = jnp.concatenate([acc_l0[0:1], acc_l1[0:1]], axis=0)
            mix = ("hout", "hin")
        else:
            sft = 1 if j % 2 == 0 else CH
            dyb, acc_g1, (dconv, dgb) = outproj_bwd(
                make_conv_epilogue(n_lat, sft), s["ex"], s["ex_specs"], [_row_spec(D), _row_spec(D)],
                [jax.ShapeDtypeStruct((t, D), F32), bshape(D)], W("cout", j), dx, s["y_mix"], mods[i], 0, n_lat, token, f"cout_bwd_{i}")
            dgc, dxi, acc_c = conv_bwd(dconv, s["p"], cw8[j], sft, n_lat, f"conv_bwd_{i}")
            dx, hb, dpb, acc_n1 = proj_bwd([dgb, dgc, dxi], W("cin", j), s["x_in"], n1, mods[i], 0, dx, n_lat, f"cin_bwd_{i}")
            small["cw"][j] = acc_c[0:3]
            small["cb"][j] = acc_c[3:4]
            mix = ("cout", "cin")
        small["norm1"][i] = acc_n1[0:1]
        small["norm2"][i] = acc_n2[0:1]
        z2 = jnp.zeros((2, D), F32)
        small["dmod"][i] = jnp.concatenate([acc_n1[1:3], acc_g1[0:1], acc_n2[1:3], acc_n2[5:6], z2,
                                            acc_n1[3:5], acc_g1[1:2], acc_n2[3:5], acc_n2[6:7], z2], axis=0)
        if i == 0:
            token = after_backward(small, token)
        token = on_grads(i, "mix", {mix[0]: dw_tn(s["ab_mix"], dyb, 1, False, token, f"{mix[0]}_dw_{i}"),
                                    mix[1]: dw_tn(hb, dpb, 4, False, token, f"{mix[1]}_dw_{i}")})
    return acc_loss[1:2], dx, token


RB = 256


def cast_to_slot(w2d, layer, k, chip1, name):
    c = w2d.shape[1]
    nblk = k // RB

    def body(chip_ref, w_ref, o_ref):
        o_ref[0] = w_ref[...].astype(BF16)

    return pl.pallas_call(
        body, name=name,
        grid_spec=pltpu.PrefetchScalarGridSpec(
            num_scalar_prefetch=1, grid=(nblk,),
            in_specs=[pl.BlockSpec((RB, c), lambda i, ch: (layer * nblk + i, 0))],
            out_specs=pl.BlockSpec((1, RB, c), lambda i, ch: (ch[0], i, 0))),
        out_shape=jax.ShapeDtypeStruct((4, k, c), BF16), compiler_params=_cp(1))(chip1, w2d)


def sum_slots(own, land, acc, layer, chip1, name):
    _, k, c = own.shape
    nblk = k // RB

    def body(chip_ref, own_ref, l1_ref, l2_ref, l3_ref, acc_ref, o_ref):
        o_ref[...] = ((own_ref[0].astype(F32) + l1_ref[0].astype(F32)) + l2_ref[0].astype(F32)) + l3_ref[0].astype(F32)

    slot = lambda d: pl.BlockSpec((1, RB, c), lambda i, ch: ((ch[0] + d) % 4, i, 0))
    return pl.pallas_call(
        body, name=name,
        grid_spec=pltpu.PrefetchScalarGridSpec(
            num_scalar_prefetch=1, grid=(nblk,),
            in_specs=[slot(0), slot(1), slot(2), slot(3), ANY],
            out_specs=pl.BlockSpec((RB, c), lambda i, ch: (layer * nblk + i, 0))),
        out_shape=jax.ShapeDtypeStruct(acc.shape, F32), input_output_aliases={5: 0}, compiler_params=_cp(1),
    )(chip1, own, land, land, land, acc)


def _adamw_math(w, g, m, v):
    m = ADAM_B1 * m + (1.0 - ADAM_B1) * g
    v = ADAM_B2 * v + (1.0 - ADAM_B2) * jnp.square(g)
    m_hat = m / (1.0 - ADAM_B1 ** ADAM_STEP)
    v_hat = v / (1.0 - ADAM_B2 ** ADAM_STEP)
    delta = -ADAM_LR * (m_hat / (jnp.sqrt(v_hat) + ADAM_EPS) + ADAM_WD * w)
    return delta, m, v


def adamw(gsrcs, w, m, v, name):
    r, c = w.shape
    rb = RB if r % RB == 0 else r
    n_g = len(gsrcs)

    def body(*refs):
        g = refs[0][...]
        for k in range(1, n_g):
            g = g + refs[k][...]
        w_ref, m_ref, v_ref, g_ref, d_ref, mo_ref, vo_ref = refs[n_g:]
        delta, mo, vo = _adamw_math(w_ref[...], g, m_ref[...], v_ref[...])
        g_ref[...] = g
        d_ref[...] = delta
        mo_ref[...] = mo
        vo_ref[...] = vo

    spec = pl.BlockSpec((rb, c), lambda i: (i, 0))
    shp = jax.ShapeDtypeStruct((r, c), F32)
    return pl.pallas_call(body, name=name, grid=(r // rb,), in_specs=[spec] * (n_g + 3), out_specs=[spec] * 4,
                          out_shape=[shp] * 4, compiler_params=_cp(1))(*gsrcs, w, m, v)


ADA_CB = 512


def ada_fwd(cvec, ada_w, bias, name):
    _, _, nc = ada_w.shape

    def body(c_ref, w_ref, b_ref, o_ref):
        cv = c_ref[...]
        a = (cv * _sigmoid(cv)).astype(BF16)
        o_ref[0] = _dot(a, w_ref[0].astype(BF16)) + b_ref[0]

    return pl.pallas_call(
        body, name=name, grid=(DEPTH, nc // ADA_CB),
        in_specs=[pl.BlockSpec((16, D), lambda i, j: (0, 0)), pl.BlockSpec((1, D, ADA_CB), lambda i, j: (i, 0, j)),
                  pl.BlockSpec((1, 1, ADA_CB), lambda i, j: (i, 0, j))],
        out_specs=pl.BlockSpec((1, 16, ADA_CB), lambda i, j: (i, 0, j)),
        out_shape=jax.ShapeDtypeStruct((DEPTH, 16, nc), F32), compiler_params=_cp(2),
    )(cvec, ada_w, bias)


def ada_bwd(cvec, dcols, ada_w, m, v, name):
    _, _, nc = ada_w.shape

    def body(c_ref, d_ref, w_ref, m_ref, v_ref, g_ref, dl_ref, mo_ref, vo_ref, acc_ref):
        @pl.when(jnp.logical_and(pl.program_id(0) == 0, pl.program_id(1) == 0))
        def _():
            acc_ref[...] = jnp.zeros_like(acc_ref)

        cv = c_ref[...]
        a = (cv * _sigmoid(cv)).astype(BF16)
        db = d_ref[0].astype(BF16)
        w = w_ref[0]
        g = _dot_tn(a, db)
        delta, mo, vo = _adamw_math(w, g, m_ref[0], v_ref[0])
        g_ref[0] = g
        dl_ref[0] = delta
        mo_ref[0] = mo
        vo_ref[0] = vo
        acc_ref[...] += _dot_nt(db[8:16, :], w.astype(BF16))

    wspec = pl.BlockSpec((1, D, ADA_CB), lambda i, j: (i, 0, j))
    wshape = jax.ShapeDtypeStruct(ada_w.shape, F32)
    return pl.pallas_call(
        body, name=name, grid=(DEPTH, nc // ADA_CB),
        in_specs=[pl.BlockSpec((16, D), lambda i, j: (0, 0)), pl.BlockSpec((1, 16, ADA_CB), lambda i, j: (i, 0, j)), wspec, wspec, wspec],
        out_specs=[wspec, wspec, wspec, wspec, pl.BlockSpec((8, D), lambda i, j: (0, 0))],
        out_shape=[wshape, wshape, wshape, wshape, jax.ShapeDtypeStruct((8, D), F32)], compiler_params=_cp(2),
    )(cvec, dcols, ada_w, m, v)


def _place():
    return lax.axis_index("x"), lax.axis_index("y"), lax.axis_index("c")


ANY = pl.BlockSpec(memory_space=pl.ANY)
VMEM_SPEC = pl.BlockSpec(memory_space=pltpu.VMEM)


def small_allgather(buf, name):
    r, c = buf.shape

    def body(in_ref, out_ref, send_sems, recv_sems, loc_sem):
        x, y, cc = _place()
        me = 4 * x + 2 * y + cc
        loc = pltpu.make_async_copy(in_ref, out_ref.at[me], loc_sem)
        loc.start()
        peers = []
        for k in range(1, 8):
            px = 1 - x if k & 4 else x
            py = 1 - y if k & 2 else y
            pc = 1 - cc if k & 1 else cc
            peers.append((px, py, pc))
        sends = []
        for k, peer in enumerate(peers):
            cp = pltpu.make_async_remote_copy(src_ref=in_ref, dst_ref=out_ref.at[me], send_sem=send_sems.at[k],
                                              recv_sem=recv_sems.at[k], device_id=peer, device_id_type=MESH)
            cp.start()
            sends.append(cp)
        for k, (px, py, pc) in enumerate(peers):
            pltpu.make_async_remote_copy(src_ref=in_ref, dst_ref=out_ref.at[4 * px + 2 * py + pc], send_sem=send_sems.at[k],
                                         recv_sem=recv_sems.at[k], device_id=(px, py, pc), device_id_type=MESH).wait_recv()
        for cp in sends:
            cp.wait_send()
        loc.wait()

    return pl.pallas_call(
        body, name=name, in_specs=[VMEM_SPEC], out_specs=VMEM_SPEC,
        out_shape=jax.ShapeDtypeStruct((8, r, c), buf.dtype),
        scratch_shapes=[pltpu.SemaphoreType.DMA((7,)), pltpu.SemaphoreType.DMA((7,)), pltpu.SemaphoreType.DMA],
    )(buf)


def _chip_peers(x, y):
    return [(1 - x, y), (x, 1 - y), (1 - x, 1 - y)]


HBM_SPEC = pl.BlockSpec(memory_space=pltpu.HBM)
SEM_SPEC = pl.BlockSpec(memory_space=pltpu.SEMAPHORE)
EFFECT = pltpu.SideEffectType.DATAFLOW_SIDE_EFFECTING


def _hbm(a):
    return pltpu.with_memory_space_constraint(a, pltpu.HBM)


def _split_copy(u, p, peer, dst_slot, chip, land_refs, src_refs, sem_refs, cc):
    px, py = peer
    src = land_refs[u].at[chip] if src_refs is None else src_refs[u].at[2 * px + py]
    return pltpu.make_async_remote_copy(src_ref=src, dst_ref=land_refs[u].at[dst_slot], send_sem=sem_refs[2 * u].at[p],
                                        recv_sem=sem_refs[2 * u + 1].at[p], device_id=(px, py, cc), device_id_type=MESH)


def split_start(lands, srcs, name):
    n = len(lands)
    ops = list(lands) + (list(srcs) if srcs is not None else [])
    n_ops = len(ops)

    def body(*refs):
        land_refs = refs[:n]
        src_refs = refs[n:n_ops] if srcs is not None else None
        sem_refs = refs[n_ops:n_ops + 2 * n]
        x, y, cc = _place()
        chip = 2 * x + y
        for u in range(n):
            for p, peer in enumerate(_chip_peers(x, y)):
                _split_copy(u, p, peer, chip, chip, land_refs, src_refs, sem_refs, cc).start()
        refs[-1][...] = jnp.zeros((8, 128), F32)

    outs = pl.pallas_call(
        body, name=name, in_specs=[HBM_SPEC] * n_ops,
        out_specs=[SEM_SPEC] * (2 * n) + [HBM_SPEC] * n_ops + [VMEM_SPEC],
        out_shape=[pltpu.SemaphoreType.DMA((3,))] * (2 * n) + [pltpu.HBM(a.shape, a.dtype) for a in ops]
        + [jax.ShapeDtypeStruct((8, 128), F32)],
        input_output_aliases={k: 2 * n + k for k in range(n_ops)},
        compiler_params=pltpu.CompilerParams(has_side_effects=EFFECT),
    )(*[_hbm(a) for a in ops])
    sems = list(outs[:2 * n])
    thru = list(outs[2 * n:2 * n + n_ops])
    return sems, thru[:n], thru[n:], outs[-1]


def split_wait(lands, srcs, sems, after, name):
    n = len(lands)
    ops = list(lands) + (list(srcs) if srcs is not None else [])
    n_ops = len(ops)

    def body(*refs):
        land_refs = refs[:n]
        src_refs = refs[n:n_ops] if srcs is not None else None
        sem_refs = refs[n_ops:n_ops + 2 * n]
        x, y, cc = _place()
        chip = 2 * x + y
        for u in range(n):
            for p, peer in enumerate(_chip_peers(x, y)):
                cp = _split_copy(u, p, peer, 2 * peer[0] + peer[1], chip, land_refs, src_refs, sem_refs, cc)
                cp.wait_send()
                cp.wait_recv()

    outs = pl.pallas_call(
        body, name=name, in_specs=[HBM_SPEC] * n_ops + [SEM_SPEC] * (2 * n) + [ANY],
        out_specs=[HBM_SPEC] * n_ops, out_shape=[pltpu.HBM(a.shape, a.dtype) for a in ops],
        input_output_aliases={k: k for k in range(n_ops)},
        compiler_params=pltpu.CompilerParams(has_side_effects=EFFECT),
    )(*ops, *sems, after)
    return list(outs[:n]), list(outs[n:])


def _sibling_copy(k, src_refs, zone_refs, sem_refs):
    x, y, cc = _place()
    return pltpu.make_async_remote_copy(src_ref=src_refs[k], dst_ref=zone_refs[k], send_sem=sem_refs[2 * k], recv_sem=sem_refs[2 * k + 1],
                                        device_id=(x, y, 1 - cc), device_id_type=MESH)


def sibling_start(parts, name):
    n = len(parts)
    ops = list(parts) + [lax.empty(p.shape, p.dtype) for p in parts]

    def body(*refs):
        for k in range(n):
            _sibling_copy(k, refs[:n], refs[n:2 * n], refs[2 * n:4 * n]).start()

    outs = pl.pallas_call(
        body, name=name, in_specs=[HBM_SPEC] * (2 * n),
        out_specs=[SEM_SPEC] * (2 * n) + [HBM_SPEC] * (2 * n),
        out_shape=[pltpu.SemaphoreType.DMA(())] * (2 * n) + [pltpu.HBM(a.shape, a.dtype) for a in ops],
        input_output_aliases={k: 2 * n + k for k in range(2 * n)},
        compiler_params=pltpu.CompilerParams(has_side_effects=EFFECT),
    )(*[_hbm(a) for a in ops])
    return list(outs[2 * n:3 * n]), list(outs[3 * n:]), list(outs[:2 * n])


def sibling_wait(parts, zones, sems, after, name):
    n = len(parts)

    def body(*refs):
        for k in range(n):
            cp = _sibling_copy(k, refs[:n], refs[n:2 * n], refs[2 * n:4 * n])
            cp.wait_send()
            cp.wait_recv()

    outs = pl.pallas_call(
        body, name=name, in_specs=[HBM_SPEC] * (2 * n) + [SEM_SPEC] * (2 * n) + [ANY],
        out_specs=[HBM_SPEC] * (2 * n), out_shape=[pltpu.HBM(a.shape, a.dtype) for a in list(parts) + list(zones)],
        input_output_aliases={k: k for k in range(2 * n)},
        compiler_params=pltpu.CompilerParams(has_side_effects=EFFECT),
    )(*parts, *zones, *sems, after)
    return list(outs[:n]), list(outs[n:])


SMALL_ROWS = 88
FIN_ROWS = 72


def small_finish(g3, g4, c_ctx, lbp, name):
    def body(g3_ref, g4_ref, cc_ref, lbp_ref, o_ref, s_ref):
        s = g3_ref[0]
        for k in range(1, 8):
            s = s + g3_ref[k]
        s_ref[...] = s
        for i in range(DEPTH):
            o_ref[8 * i:8 * i + 8, :] = s_ref[16 * i:16 * i + 8, :] + s_ref[16 * i + 8:16 * i + 16, :]
        acc = g4_ref[0]
        for k in (2, 4, 6):
            acc = acc + g4_ref[k]
        cc = cc_ref[...]
        sg = _sigmoid(cc)
        row = jnp.sum(acc, axis=0, keepdims=True) * (sg * (1.0 + cc * (1.0 - sg)))
        o_ref[32:40, :] = jnp.broadcast_to(row, (8, D))
        o_ref[40:64, :] = s_ref[64:88, :]
        o_ref[64:72, :] = jnp.zeros((8, D), F32)
        for d in range(2):
            pp = lbp_ref[2 * d:2 * d + 1, :] * lbp_ref[2 * d + 1:2 * d + 2, :] * s_ref[75 + d:76 + d, :]
            o_ref[64 + 2 * d:65 + 2 * d, :] = -pp
            o_ref[65 + 2 * d:66 + 2 * d, :] = pp

    return pl.pallas_call(
        body, name=name, in_specs=[VMEM_SPEC] * 4, out_specs=VMEM_SPEC,
        out_shape=jax.ShapeDtypeStruct((FIN_ROWS, D), F32),
        scratch_shapes=[pltpu.VMEM((SMALL_ROWS, D), F32)],
    )(g3, g4, c_ctx, lbp)


def _pack_rows(arrs):
    flat = jnp.concatenate([a.reshape(-1) for a in arrs])
    n = -(-flat.shape[0] // (8 * D)) * 8 * D
    return jnp.pad(flat, (0, n - flat.shape[0])).reshape(n // D, D)


def _unpack_rows(packed, shapes):
    flat = packed.reshape(-1)
    outs, off = [], 0
    for s in shapes:
        size = 1
        for k in s:
            size *= k
        outs.append(flat[off:off + size].reshape(s))
        off += size
    return outs


def _pad8(a):
    return jnp.pad(a, ((0, 8 - a.shape[0]), (0, 0)))


def kernel(x, c, ctx, c_ctx, ada_w, ada_b, norm1, norm2, norm_f, mlp_w1, mlp_w2, hgrn_w_in, hgrn_lb, hgrn_gnorm, hgrn_w_out, conv_w_in, conv_w, conv_b, conv_w_out, loss_target, m_c_ctx, m_ada_w, m_ada_b, m_norm1, m_norm2, m_norm_f, m_mlp_w1, m_mlp_w2, m_hgrn_w_in, m_hgrn_lb, m_hgrn_gnorm, m_hgrn_w_out, m_conv_w_in, m_conv_w, m_conv_b, m_conv_w_out, v_c_ctx, v_ada_w, v_ada_b, v_norm1, v_norm2, v_norm_f, v_mlp_w1, v_mlp_w2, v_hgrn_w_in, v_hgrn_lb, v_hgrn_gnorm, v_hgrn_w_out, v_conv_w_in, v_conv_w, v_conv_b, v_conv_w_out):
    xi, yi, ci = _place()
    me = 4 * xi + 2 * yi + ci
    chip = 2 * xi + yi
    seq = x.shape[1]
    assert ctx.shape[1] == TM and seq % TM == 0 and (seq + TM) % TMW == 0
    n_lat = seq // TM
    sd = D // 4
    nca = ada_w.shape[2]
    xs = jnp.concatenate([x[0], ctx[0]], axis=0)

    sh_rows = jnp.concatenate([hgrn_lb.reshape(4, sd), conv_w.reshape(6, sd), conv_b.reshape(2, sd)], axis=0)
    buf1 = jnp.concatenate([c, jnp.pad(sh_rows, ((0, 0), (0, D - sd))), jnp.zeros((3, D), F32)], axis=0)
    g1 = small_allgather(buf1, "gather_small_in")
    cvec = jnp.concatenate([g1[:, 0, :], jnp.broadcast_to(c_ctx[None], (8, D))], axis=0)
    shf = g1[0::2, 1:13, :sd].transpose(1, 0, 2).reshape(12, D)
    lb_p = jax.nn.softmax(shf[0:4].reshape(2, 2, D), axis=1)
    lower = jnp.cumsum(lb_p, axis=1) - lb_p[:, :1]
    lbs = [lower[:, 0], lower[:, 1]]
    cw8 = [_pad8(shf[4:7]), _pad8(shf[7:10])]
    cb = [shf[10:11], shf[11:12]]

    bias = lax.dynamic_slice_in_dim(ada_b, chip * nca, nca, axis=1).reshape(DEPTH, 1, nca)
    ada_part = ada_fwd(cvec, ada_w, bias, "ada_fwd")
    g2 = small_allgather(ada_part.reshape(DEPTH * 16, nca), "gather_ada")
    ada_full = g2[0::2].reshape(4, DEPTH, 16, nca).transpose(1, 2, 0, 3).reshape(DEPTH, 16, 4 * nca)
    lat = lax.dynamic_slice_in_dim(ada_full, me, 1, axis=1)[:, 0]
    mods = [jnp.stack([_pad8(lat[i].reshape(6, D)), _pad8(ada_full[i, 8].reshape(6, D))]) for i in range(DEPTH)]

    big = [(mlp_w1, m_mlp_w1, v_mlp_w1), (mlp_w2, m_mlp_w2, v_mlp_w2), (hgrn_w_in, m_hgrn_w_in, v_hgrn_w_in),
           (hgrn_w_out, m_hgrn_w_out, v_hgrn_w_out), (conv_w_in, m_conv_w_in, v_conv_w_in), (conv_w_out, m_conv_w_out, v_conv_w_out)]
    big_names = ["w1", "w2", "hin", "hout", "cin", "cout"]
    flat2 = lambda a: a.reshape(a.shape[0] * a.shape[1], a.shape[2])
    tensors = dict(zip(big_names, big))
    chip1 = jnp.reshape(chip, (1,)).astype(jnp.int32)
    order = []
    for i in range(DEPTH):
        order += [("hin", i // 2), ("hout", i // 2)] if i % 2 == 0 else [("cin", i // 2), ("cout", i // 2)]
        order += [("w1", i), ("w2", i)]
    lands = [cast_to_slot(flat2(tensors[n][0]), idx, tensors[n][0].shape[1], chip1, f"cast_{n}_{idx}") for n, idx in order]
    w_sems, lands, _, _ = split_start(lands, None, "gather_start")
    unit = {key: u for u, key in enumerate(order)}

    def wts(n, idx, after):
        u = unit[(n, idx)]
        (w,), _ = split_wait([lands[u]], None, w_sems[2 * u:2 * u + 2], after, f"gather_wait_{n}_{idx}")
        return w.reshape(w.shape[0] * w.shape[1], w.shape[2]) if n in ("w2", "hout", "cout") else w

    started = []

    def on_grads(i, tag, g):
        names = sorted(g)
        gs = [g[n].reshape(4, g[n].shape[0] * g[n].shape[1] // 4, g[n].shape[2]) for n in names]
        sems, zones, srcs, token = split_start([lax.empty(a.shape, BF16) for a in gs], gs, f"grad_start_{tag}_{i}")
        started.append(([(n, i if n in ("w1", "w2") else i // 2) for n in names], sems, zones, srcs))
        return token

    done = {}

    def after_backward(small, token):
        rows3 = jnp.concatenate(small["dmod"] + small["norm1"] + small["norm2"] + [small["norm_f"]] + small["gnorm"]
                                + [small["lb"][1]] + small["cw"] + small["cb"] + [jnp.tile(token[0:3], (1, D // 128))], axis=0)
        g3 = small_allgather(rows3, "gather_small_out")
        dmat = g3[:, :64].reshape(8, DEPTH, 2, 8, D)[:, :, :, :6].transpose(1, 2, 0, 3, 4).reshape(DEPTH, 16, 6 * D)
        dcols = lax.dynamic_slice_in_dim(dmat, chip * nca, nca, axis=2)
        *done["ada"], acc4 = ada_bwd(cvec, dcols, ada_w, m_ada_w, v_ada_w, "ada_bwd")
        g4 = small_allgather(acc4, "gather_cctx")
        done["fin"] = small_finish(g3, g4, c_ctx[None], _pad8(lb_p.reshape(4, D)), "small_finish")
        return done["fin"]

    lane, dx, last_token = local_step(xs, loss_target[0], mods, norm1, norm2, norm_f[None], lbs, hgrn_gnorm, cw8, cb, wts, n_lat,
                                      on_grads, after_backward)
    loss = lax.psum(0.5 * jnp.sum(lane) / D, ("x", "y", "c"))
    grad_x = dx[:seq][None]
    g_ada_w, d_ada_w, nm_ada_w, nv_ada_w = done["ada"]
    fin = done["fin"]
    cols = lambda a: lax.dynamic_slice_in_dim(a, chip * sd, sd, axis=a.ndim - 1)
    small_g = [fin[32], fin[0:32].reshape(DEPTH, 8, D)[:, :6].reshape(DEPTH, 6 * D), fin[40:44], fin[44:48], fin[48], fin[49:51],
               cols(fin[64:68].reshape(2, 2, D)), cols(fin[53:59].reshape(2, 3, D)), cols(fin[59:61])]
    small_w = [c_ctx, ada_b, norm1, norm2, norm_f, hgrn_gnorm, hgrn_lb, conv_w, conv_b]
    small_m = [m_c_ctx, m_ada_b, m_norm1, m_norm2, m_norm_f, m_hgrn_gnorm, m_hgrn_lb, m_conv_w, m_conv_b]
    small_v = [v_c_ctx, v_ada_b, v_norm1, v_norm2, v_norm_f, v_hgrn_gnorm, v_hgrn_lb, v_conv_w, v_conv_b]
    shapes = [w.shape for w in small_w]
    packed = adamw([_pack_rows(small_g)], _pack_rows(small_w), _pack_rows(small_m), _pack_rows(small_v), "adamw_small")
    s_g, s_d, s_m, s_v = [_unpack_rows(p, shapes) for p in packed]

    units = []
    for ks, sems, zones, srcs in started:
        units += [(key, sems[2 * u:2 * u + 2], zones[u], srcs[u]) for u, key in enumerate(ks)]
    late_keys = (("hin", 0), ("hout", 0))
    acc = {n: lax.empty(flat2(w).shape, F32) for n, (w, _, _) in tensors.items()}

    def finish_units(group, after, name):
        zones, srcs = split_wait([u[2] for u in group], [u[3] for u in group], [s for u in group for s in u[1]], after, name)
        for (key, _, _, _), zone, own in zip(group, zones, srcs):
            acc[key[0]] = sum_slots(own, zone, acc[key[0]], key[1], chip1, f"sum_{key[0]}_{key[1]}")

    finish_units([u for u in units if u[0] not in late_keys], last_token, "grad_wait_early")
    early_names = ["w1", "w2", "cin", "cout"]
    late_names = ["hin", "hout"]
    sib_early = sibling_start([acc[n] for n in early_names], "sibling_start_early")
    finish_units([u for u in units if u[0] in late_keys], sib_early[0][-1], "grad_wait_late")
    sib_late = sibling_start([acc[n] for n in late_names], "sibling_start_late")
    results = {}

    def finish_tensors(names, sib, after, name):
        mine, other = sibling_wait(*sib, after, name)
        for n, pm, po in zip(names, mine, other):
            w, m, v = tensors[n]
            results[n] = [a.reshape(w.shape) for a in adamw([pm, po], flat2(w), flat2(m), flat2(v), f"adamw_{n}")]

    finish_tensors(early_names, sib_early, sib_late[0][-1], "sibling_wait_early")
    finish_tensors(late_names, sib_late, results["cout"][0], "sibling_wait_late")
    b_g, b_d, b_m, b_v = [[results[n][k] for n in big_names] for k in range(4)]

    def ordered(s, a, b):
        return [s[0], a, s[1], s[2], s[3], s[4], b[0], b[1], b[2], s[6], s[5], b[3], b[4], s[7], s[8], b[5]]

    return (loss, grad_x, *ordered(s_g, g_ada_w, b_g), *ordered(s_d, d_ada_w, b_d), *ordered(s_m, nm_ada_w, b_m),
            *ordered(s_v, nv_ada_w, b_v))
```

```python
import functools

import jax
import jax.numpy as jnp
from jax import lax
from jax.experimental import pallas as pl
from jax.experimental.pallas import tpu as pltpu

F32 = jnp.float32
BF16 = jnp.bfloat16
MESH = pl.DeviceIdType.MESH

D = 1024
HD = 128
NH = D // HD
CH = 64
TM = 256
TMW = 768
EPS = 1e-6
DEPTH = 4
VMEM_LIMIT = 56 * 1024 * 1024

ADAM_LR = 0.001
ADAM_B1 = 0.9
ADAM_B2 = 0.999
ADAM_EPS = 1e-08
ADAM_WD = 0.01
ADAM_STEP = 10


def _cp(n_grid):
    return pltpu.CompilerParams(dimension_semantics=("arbitrary",) * n_grid, vmem_limit_bytes=VMEM_LIMIT)


def _dot(a, b):
    return jnp.dot(a, b, preferred_element_type=F32)


def _dot_nt(a, b):
    return lax.dot_general(a, b, (((1,), (1,)), ((), ())), preferred_element_type=F32)


def _dot_tn(a, b):
    return lax.dot_general(a, b, (((0,), (0,)), ((), ())), preferred_element_type=F32)


def _sigmoid(z):
    return 1.0 / (1.0 + jnp.exp(-z))


def _norm_mod(x, gain, sh, sc):
    r = lax.rsqrt(jnp.mean(x * x, axis=-1, keepdims=True) + EPS)
    xn = x * r
    yn = xn * gain
    return r, xn, yn, yn * (1.0 + sc) + sh


def _row_spec(width):
    return pl.BlockSpec((TM, width), lambda i: (i, 0))


def _col_spec(col):
    return pl.BlockSpec((TM, D), lambda i: (i, col))


def _full_spec(shape):
    nd = len(shape)
    return pl.BlockSpec(shape, lambda i: (0,) * nd)


def _mod_spec(n_lat):
    return pl.BlockSpec((1, 8, D), lambda i: (i // n_lat, 0, 0))


def _f32(ref):
    return ref[...].astype(F32)


def proj_fwd(x, gain, mod, m0, w4, n_lat, dtype, name):
    t = x.shape[0]
    nb, _, ns = w4.shape

    def body(x_ref, gain_ref, mod_ref, w_ref, p_ref):
        _, _, _, h = _norm_mod(x_ref[...], gain_ref[...], mod_ref[0, m0:m0 + 1, :], mod_ref[0, m0 + 1:m0 + 2, :])
        hb = h.astype(BF16)
        for c in range(nb):
            p_ref[:, c * ns:(c + 1) * ns] = _dot(hb, w_ref[c]).astype(dtype)

    return pl.pallas_call(
        body, name=name, grid=(t // TM,),
        in_specs=[_row_spec(D), _full_spec((1, D)), _mod_spec(n_lat), _full_spec(w4.shape)],
        out_specs=_row_spec(nb * ns),
        out_shape=jax.ShapeDtypeStruct((t, nb * ns), dtype),
        compiler_params=_cp(1),
    )(x, gain, mod, w4)


def proj_bwd(parts, w4, x, gain, mod, m0, dx_in, n_lat, lat_only, name):
    t = x.shape[0]
    nb, _, ns = w4.shape
    n = nb * ns
    n_parts = len(parts)
    widths = [p.shape[1] for p in parts]
    offs = [sum(widths[:k]) for k in range(n_parts)]
    assert sum(widths) == n
    single = n_parts == 1

    def body(*refs):
        part_refs = refs[:n_parts]
        w_ref, x_ref, gain_ref, mod_ref, dxin_ref = refs[n_parts:n_parts + 5]
        rest = refs[n_parts + 5:]
        if single:
            dx_ref, hb_ref, acc_ref = rest
            src = part_refs[0]
        else:
            dx_ref, hb_ref, acc_ref, dpb_ref = rest
            for p_ref, off, w in zip(part_refs, offs, widths):
                dpb_ref[:, off:off + w] = p_ref[...]
            src = dpb_ref
        i = pl.program_id(0)

        @pl.when(i == 0)
        def _():
            acc_ref[...] = jnp.zeros_like(acc_ref)

        gain = gain_ref[...]
        sc = mod_ref[0, m0 + 1:m0 + 2, :]
        r, xn, yn, h = _norm_mod(x_ref[...], gain, mod_ref[0, m0:m0 + 1, :], sc)
        hb_ref[...] = h.astype(BF16)
        dh = _dot_nt(src[:, 0:ns], w_ref[0])
        for c in range(1, nb):
            dh = dh + _dot_nt(src[:, c * ns:(c + 1) * ns], w_ref[c])
        dsh = jnp.sum(dh, axis=0, keepdims=True)
        dsc = jnp.sum(dh * yn, axis=0, keepdims=True)
        dyn = dh * (1.0 + sc)
        dgain = jnp.sum(dyn * xn, axis=0, keepdims=True)
        dxn = dyn * gain
        dx = dxin_ref[...] + r * (dxn - xn * jnp.mean(dxn * xn, axis=-1, keepdims=True))
        if lat_only:
            @pl.when(i < n_lat)
            def _():
                dx_ref[...] = dx
        else:
            dx_ref[...] = dx
        latf = (i < n_lat).astype(F32)
        ctxf = 1.0 - latf
        acc_ref[0:1, :] += dgain
        acc_ref[1:2, :] += dsh * latf
        acc_ref[2:3, :] += dsc * latf
        acc_ref[3:4, :] += dsh * ctxf
        acc_ref[4:5, :] += dsc * ctxf

    dx_rows = n_lat * TM if lat_only else t
    dx_spec = pl.BlockSpec((TM, D), lambda i: (jnp.minimum(i, n_lat - 1), 0)) if lat_only else _row_spec(D)
    out_specs = [dx_spec, _row_spec(D), _full_spec((8, D))]
    out_shape = [jax.ShapeDtypeStruct((dx_rows, D), F32), jax.ShapeDtypeStruct((t, D), BF16), jax.ShapeDtypeStruct((8, D), F32)]
    if not single:
        out_specs.append(_row_spec(n))
        out_shape.append(jax.ShapeDtypeStruct((t, n), BF16))
    outs = pl.pallas_call(
        body, name=name, grid=(t // TM,),
        in_specs=[_row_spec(w) for w in widths]
        + [_full_spec(w4.shape), _row_spec(D), _full_spec((1, D)), _mod_spec(n_lat), _row_spec(D)],
        out_specs=out_specs, out_shape=out_shape, compiler_params=_cp(1),
    )(*parts, w4, x, gain, mod, dx_in)
    if single:
        return outs[0], outs[1], parts[0], outs[2]
    return outs[0], outs[1], outs[3], outs[2]


def dw_tn(a, b, nb, a_blocked, dep, name):
    t = a.shape[0]
    ka = a.shape[1] // nb if a_blocked else a.shape[1]
    kb = b.shape[1] if a_blocked else b.shape[1] // nb
    n_k = t // TMW

    def body(a_ref, b_ref, _, o_ref, acc):
        k = pl.program_id(1)

        @pl.when(k == 0)
        def _():
            acc[...] = jnp.zeros_like(acc)

        acc[...] += _dot_tn(a_ref[...], b_ref[...])

        @pl.when(k == n_k - 1)
        def _():
            o_ref[0] = acc[...].astype(BF16)

    a_spec = pl.BlockSpec((TMW, ka), (lambda j, k: (k, j)) if a_blocked else (lambda j, k: (k, 0)))
    b_spec = pl.BlockSpec((TMW, kb), (lambda j, k: (k, 0)) if a_blocked else (lambda j, k: (k, j)))
    return pl.pallas_call(
        body, name=name, grid=(nb, n_k),
        in_specs=[a_spec, b_spec, ANY],
        out_specs=pl.BlockSpec((1, ka, kb), lambda j, k: (j, 0, 0)),
        out_shape=jax.ShapeDtypeStruct((nb, ka, kb), BF16),
        scratch_shapes=[pltpu.VMEM((ka, kb), F32)],
        compiler_params=_cp(2),
    )(a, b, dep)


def outproj_fwd(prologue, extras, extra_specs, w, x, mod, m0, n_lat, name):
    t = x.shape[0]
    k = w.shape[0]
    n_extra = len(extras)

    def body(*refs):
        ex = refs[:n_extra]
        w_ref, x_ref, mod_ref, xo_ref, y_ref, ab_ref = refs[n_extra:]
        ab = prologue(pl.program_id(0), *ex).astype(BF16)
        ab_ref[...] = ab
        y = _dot(ab, w_ref[...])
        y_ref[...] = y
        xo_ref[...] = x_ref[...] + mod_ref[0, m0 + 2:m0 + 3, :] * y

    return pl.pallas_call(
        body, name=name, grid=(t // TM,),
        in_specs=list(extra_specs) + [_full_spec(w.shape), _row_spec(D), _mod_spec(n_lat)],
        out_specs=[_row_spec(D), _row_spec(D), _row_spec(k)],
        out_shape=[jax.ShapeDtypeStruct((t, D), F32), jax.ShapeDtypeStruct((t, D), F32), jax.ShapeDtypeStruct((t, k), BF16)],
        compiler_params=_cp(1),
    )(*extras, w, x, mod)


def outproj_bwd(epilogue, extras, extra_specs, ep_out_specs, ep_out_shapes, w, dxn, y, mod, m0, n_lat, dep, name):
    t = dxn.shape[0]
    n_extra = len(extras)

    def body(*refs):
        ex = refs[:n_extra]
        w_ref, dxn_ref, y_ref, mod_ref, _, dyb_ref, acc_ref = refs[n_extra:n_extra + 7]
        ep_outs = refs[n_extra + 7:]
        i = pl.program_id(0)

        @pl.when(i == 0)
        def _():
            acc_ref[...] = jnp.zeros_like(acc_ref)

        dxv = dxn_ref[...]
        dyb = (dxv * mod_ref[0, m0 + 2:m0 + 3, :]).astype(BF16)
        dyb_ref[...] = dyb
        dg = jnp.sum(dxv * y_ref[...], axis=0, keepdims=True)
        latf = (i < n_lat).astype(F32)
        acc_ref[0:1, :] += dg * latf
        acc_ref[1:2, :] += dg * (1.0 - latf)
        epilogue(i, _dot_nt(dyb, w_ref[...]), ex, ep_outs, acc_ref)

    outs = pl.pallas_call(
        body, name=name, grid=(t // TM,),
        in_specs=list(extra_specs) + [_full_spec(w.shape), _row_spec(D), _row_spec(D), _mod_spec(n_lat), ANY],
        out_specs=[_row_spec(D), _full_spec((8, D))] + list(ep_out_specs),
        out_shape=[jax.ShapeDtypeStruct((t, D), BF16), jax.ShapeDtypeStruct((8, D), F32)] + list(ep_out_shapes),
        compiler_params=_cp(1),
    )(*extras, w, dxn, y, mod, dep)
    return outs[0], outs[1], outs[2:]


def mlp_fwd(x, gain, mod, w1, w2, n_lat, name):
    t = x.shape[0]
    nb, _, ns = w1.shape

    def body(x_ref, gain_ref, mod_ref, w1_ref, w2_ref, xo_ref, y_ref, ab_ref):
        x = x_ref[...]
        _, _, _, h = _norm_mod(x, gain_ref[...], mod_ref[0, 3:4, :], mod_ref[0, 4:5, :])
        hb = h.astype(BF16)
        y = None
        for c in range(nb):
            a = jnp.square(jnp.maximum(_dot(hb, w1_ref[c]), 0.0)).astype(BF16)
            ab_ref[:, c * ns:(c + 1) * ns] = a
            yc = _dot(a, w2_ref[c * ns:(c + 1) * ns, :])
            y = yc if y is None else y + yc
        y_ref[...] = y
        xo_ref[...] = x + mod_ref[0, 5:6, :] * y

    return pl.pallas_call(
        body, name=name, grid=(t // TM,),
        in_specs=[_row_spec(D), _full_spec((1, D)), _mod_spec(n_lat), _full_spec(w1.shape), _full_spec(w2.shape)],
        out_specs=[_row_spec(D), _row_spec(D), _row_spec(nb * ns)],
        out_shape=[jax.ShapeDtypeStruct((t, D), F32), jax.ShapeDtypeStruct((t, D), F32), jax.ShapeDtypeStruct((t, nb * ns), BF16)],
        compiler_params=_cp(1),
    )(x, gain, mod, w1, w2)


def mlp_bwd(dxn, y, ab, x, gain, mod, w1, w2, n_lat, dep, name):
    t = x.shape[0]
    nb, _, ns = w1.shape

    def body(dxn_ref, y_ref, ab_ref, x_ref, gain_ref, mod_ref, w1_ref, w2_ref, _, dx_ref, dyb_ref, dp_ref, hb_ref, acc_ref):
        i = pl.program_id(0)

        @pl.when(i == 0)
        def _():
            acc_ref[...] = jnp.zeros_like(acc_ref)

        dxv = dxn_ref[...]
        dyb = (dxv * mod_ref[0, 5:6, :]).astype(BF16)
        dyb_ref[...] = dyb
        dg = jnp.sum(dxv * y_ref[...], axis=0, keepdims=True)
        gain = gain_ref[...]
        sc = mod_ref[0, 4:5, :]
        r, xn, yn, h = _norm_mod(x_ref[...], gain, mod_ref[0, 3:4, :], sc)
        hb_ref[...] = h.astype(BF16)
        dh = None
        for c in range(nb):
            cols = slice(c * ns, (c + 1) * ns)
            da = _dot_nt(dyb, w2_ref[cols, :])
            dp = (da * (2.0 * jnp.sqrt(ab_ref[:, cols].astype(F32)))).astype(BF16)
            dp_ref[:, cols] = dp
            d = _dot_nt(dp, w1_ref[c])
            dh = d if dh is None else dh + d
        dsh = jnp.sum(dh, axis=0, keepdims=True)
        dsc = jnp.sum(dh * yn, axis=0, keepdims=True)
        dyn = dh * (1.0 + sc)
        dgain = jnp.sum(dyn * xn, axis=0, keepdims=True)
        dxn_ = dyn * gain
        dx_ref[...] = dxv + r * (dxn_ - xn * jnp.mean(dxn_ * xn, axis=-1, keepdims=True))
        latf = (i < n_lat).astype(F32)
        ctxf = 1.0 - latf
        acc_ref[0:1, :] += dgain
        acc_ref[1:2, :] += dsh * latf
        acc_ref[2:3, :] += dsc * latf
        acc_ref[3:4, :] += dsh * ctxf
        acc_ref[4:5, :] += dsc * ctxf
        acc_ref[5:6, :] += dg * latf
        acc_ref[6:7, :] += dg * ctxf

    return pl.pallas_call(
        body, name=name, grid=(t // TM,),
        in_specs=[_row_spec(D), _row_spec(D), _row_spec(nb * ns), _row_spec(D), _full_spec((1, D)), _mod_spec(n_lat),
                  _full_spec(w1.shape), _full_spec(w2.shape), ANY],
        out_specs=[_row_spec(D), _row_spec(D), _row_spec(nb * ns), _row_spec(D), _full_spec((8, D))],
        out_shape=[jax.ShapeDtypeStruct((t, D), F32), jax.ShapeDtypeStruct((t, D), BF16), jax.ShapeDtypeStruct((t, nb * ns), BF16),
                   jax.ShapeDtypeStruct((t, D), BF16), jax.ShapeDtypeStruct((8, D), F32)],
        compiler_params=_cp(1),
    )(dxn, y, ab, x, gain, mod, w1, w2, dep)


def readout_prologue(i, o0_ref, o1_ref, gate_ref, gn_ref):
    o = o0_ref[...] + o1_ref[...]
    gate = gate_ref[...]
    w = gn_ref[...] * (gate * _sigmoid(gate))
    pieces = []
    for h in range(NH):
        sl = slice(h * HD, (h + 1) * HD)
        oh = o[:, sl]
        pieces.append(oh * lax.rsqrt(jnp.mean(oh * oh, axis=-1, keepdims=True) + EPS) * w[:, sl])
    return jnp.concatenate(pieces, axis=1)


def readout_epilogue(i, da, ex, outs, acc_ref):
    o0_ref, o1_ref, gate_ref, gn_ref = ex
    do_ref, dgate_ref = outs
    o = o0_ref[...] + o1_ref[...]
    gate = gate_ref[...]
    gn = gn_ref[...]
    sg = _sigmoid(gate)
    silu = gate * sg
    dsilu = sg * (1.0 + gate * (1.0 - sg))
    for h in range(NH):
        sl = slice(h * HD, (h + 1) * HD)
        oh = o[:, sl]
        r = lax.rsqrt(jnp.mean(oh * oh, axis=-1, keepdims=True) + EPS)
        nh = oh * r
        dah = da[:, sl]
        acc_ref[2:3, sl] += jnp.sum(dah * nh * silu[:, sl], axis=0, keepdims=True)
        dgate_ref[:, sl] = (dah * nh * gn[:, sl] * dsilu[:, sl]).astype(BF16)
        dn = dah * gn[:, sl] * silu[:, sl]
        do_ref[:, sl] = r * (dn - nh * jnp.mean(dn * nh, axis=-1, keepdims=True))


def _seg_masks(i, n_lat):
    rows = lax.broadcasted_iota(jnp.int32, (TM, 1), 0)
    latf = (i < n_lat).astype(F32)
    ctxf = 1.0 - latf
    prev_ok = (rows % CH != 0).astype(F32) * latf + (rows != 0).astype(F32) * ctxf
    next_ok = (rows % CH != CH - 1).astype(F32) * latf + (rows != TM - 1).astype(F32) * ctxf
    return prev_ok, next_ok


def _shifts(i, n_lat, sft, cur, halo_prev, halo_next):
    if sft == 1:
        prev_ok, next_ok = _seg_masks(i, n_lat)
        return pltpu.roll(cur, 1, 0) * prev_ok, pltpu.roll(cur, TM - 1, 0) * next_ok
    has_prev = jnp.logical_and(i > 0, i < n_lat).astype(F32)
    has_next = (i < n_lat - 1).astype(F32)
    prev = jnp.concatenate([halo_prev * has_prev, cur[:TM - CH]], axis=0)
    nxt = jnp.concatenate([cur[CH:], halo_next * has_next], axis=0)
    return prev, nxt


def _conv_u(sft, ex):
    if sft == 1:
        gb_ref, gc_ref, xi_ref, cw_ref, cb_ref = ex
        return gb_ref, _f32(gc_ref) * _f32(xi_ref), None, None, cw_ref, cb_ref
    gb_ref, gc_ref, xi_ref, gcp_ref, xip_ref, gcn_ref, xin_ref, cw_ref, cb_ref = ex
    return gb_ref, _f32(gc_ref) * _f32(xi_ref), _f32(gcp_ref) * _f32(xip_ref), _f32(gcn_ref) * _f32(xin_ref), cw_ref, cb_ref


def _conv_value(i, n_lat, sft, ex):
    gb_ref, u, up, un, cw_ref, cb_ref = _conv_u(sft, ex)
    u_prev, u_next = _shifts(i, n_lat, sft, u, up, un)
    return gb_ref, cb_ref[...] + cw_ref[0:1, :] * u_prev + cw_ref[1:2, :] * u + cw_ref[2:3, :] * u_next


def make_conv_prologue(n_lat, sft):
    def prologue(i, *ex):
        gb_ref, conv = _conv_value(i, n_lat, sft, ex)
        return _f32(gb_ref) * conv
    return prologue


def make_conv_epilogue(n_lat, sft):
    def epilogue(i, da, ex, outs, acc_ref):
        gb_ref, conv = _conv_value(i, n_lat, sft, ex)
        outs[0][...] = da * _f32(gb_ref)
        outs[1][...] = (da * conv).astype(BF16)
    return epilogue


def _conv_specs(sft, t):
    specs = [_col_spec(0), _col_spec(1), _col_spec(2)]
    if sft != 1:
        per = TM // CH
        last = t // CH - 1
        for fn in (lambda i: jnp.maximum(i * per - 1, 0), lambda i: jnp.minimum(i * per + per, last)):
            for col in (1, 2):
                specs.append(pl.BlockSpec((CH, D), functools.partial(lambda i, f, c: (f(i), c), f=fn, c=col)))
    return specs + [_full_spec((8, D)), _full_spec((1, D))]


def _conv_args(sft, p, cw8, cb):
    return [p] * (3 if sft == 1 else 7) + [cw8, cb]


def conv_bwd(dconv, p, cw8, sft, n_lat, name):
    t = dconv.shape[0]
    halo = sft != 1

    def body(*refs):
        if halo:
            dc_ref, dcp_ref, dcn_ref, gc_ref, xi_ref, gcp_ref, xip_ref, gcn_ref, xin_ref, cw_ref, dgc_ref, dxi_ref, acc_ref = refs
            up, un = _f32(gcp_ref) * _f32(xip_ref), _f32(gcn_ref) * _f32(xin_ref)
            dcp, dcn = dcp_ref[...], dcn_ref[...]
        else:
            dc_ref, gc_ref, xi_ref, cw_ref, dgc_ref, dxi_ref, acc_ref = refs
            up = un = dcp = dcn = None
        i = pl.program_id(0)

        @pl.when(i == 0)
        def _():
            acc_ref[...] = jnp.zeros_like(acc_ref)

        gc = _f32(gc_ref)
        xi = _f32(xi_ref)
        u = gc * xi
        dc = dc_ref[...]
        u_prev, u_next = _shifts(i, n_lat, sft, u, up, un)
        dc_prev, dc_next = _shifts(i, n_lat, sft, dc, dcp, dcn)
        acc_ref[0:1, :] += jnp.sum(dc * u_prev, axis=0, keepdims=True)
        acc_ref[1:2, :] += jnp.sum(dc * u, axis=0, keepdims=True)
        acc_ref[2:3, :] += jnp.sum(dc * u_next, axis=0, keepdims=True)
        acc_ref[3:4, :] += jnp.sum(dc, axis=0, keepdims=True)
        du = cw_ref[0:1, :] * dc_next + cw_ref[1:2, :] * dc + cw_ref[2:3, :] * dc_prev
        dgc_ref[...] = (du * xi).astype(BF16)
        dxi_ref[...] = (du * gc).astype(BF16)

    per = TM // CH
    last = t // CH - 1
    prev_i = lambda i: jnp.maximum(i * per - 1, 0)
    next_i = lambda i: jnp.minimum(i * per + per, last)
    if halo:
        in_specs = [_row_spec(D), pl.BlockSpec((CH, D), lambda i: (prev_i(i), 0)), pl.BlockSpec((CH, D), lambda i: (next_i(i), 0)),
                    _col_spec(1), _col_spec(2),
                    pl.BlockSpec((CH, D), lambda i: (prev_i(i), 1)), pl.BlockSpec((CH, D), lambda i: (prev_i(i), 2)),
                    pl.BlockSpec((CH, D), lambda i: (next_i(i), 1)), pl.BlockSpec((CH, D), lambda i: (next_i(i), 2)),
                    _full_spec((8, D))]
        args = [dconv, dconv, dconv, p, p, p, p, p, p, cw8]
    else:
        in_specs = [_row_spec(D), _col_spec(1), _col_spec(2), _full_spec((8, D))]
        args = [dconv, p, p, cw8]
    return pl.pallas_call(
        body, name=name, grid=(t // TM,), in_specs=in_specs,
        out_specs=[_row_spec(D), _row_spec(D), _full_spec((8, D))],
        out_shape=[jax.ShapeDtypeStruct((t, D), BF16), jax.ShapeDtypeStruct((t, D), BF16), jax.ShapeDtypeStruct((8, D), F32)],
        compiler_params=_cp(1),
    )(*args)


LOG2E = 1.4426950408889634


def _cumsum_matrix(reverse):
    r = lax.broadcasted_iota(jnp.int32, (CH, CH), 0)
    c = lax.broadcasted_iota(jnp.int32, (CH, CH), 1)
    return (r <= c if reverse else r >= c).astype(BF16)


def _chunk_cumsum(g, tri):
    hi = g.astype(BF16)
    lo = (g - hi.astype(F32)).astype(BF16)
    return _dot(tri, hi) + _dot(tri, lo)


def _gate_values(z, lb):
    sig = _sigmoid(z)
    f = lb + (1.0 - lb) * sig
    return sig, f


def _tri(direction, transposed):
    r = lax.broadcasted_iota(jnp.int32, (CH, CH), 0)
    c = lax.broadcasted_iota(jnp.int32, (CH, CH), 1)
    lower = (direction == 0) != transposed
    return r >= c if lower else r <= c


def _gla_rows(direction):
    return (CH // 2 - 1, CH - 1) if direction == 0 else (CH // 2, 0)


def _fwd_tile(direction, nt):
    return (lambda i: (i + nt - 1) % nt) if direction == 0 else (lambda i: nt - 1 - i)


def gla_fwd(p, lb2, name):
    t = p.shape[0]
    nt = t // TM
    per = TM // CH

    def body(z0_ref, v0_ref, q0_ref, z1_ref, v1_ref, q1_ref, lb_ref, o0_ref, s0_ref, o1_ref, s1_ref, st, q_s, k_s, c_s):
        @pl.when(pl.program_id(0) == 0)
        def _():
            st[...] = jnp.zeros_like(st)

        ins = ((z0_ref, v0_ref, q0_ref, o0_ref, s0_ref), (z1_ref, v1_ref, q1_ref, o1_ref, s1_ref))
        for d in range(2):
            z_ref, _, qr_ref, _, _ = ins[d]
            tri = _cumsum_matrix(d == 1)
            lb = lb_ref[d:d + 1, :]
            for ci in range(per):
                rows = slice(ci * CH, (ci + 1) * CH)
                _, f = _gate_values(z_ref[rows, :], lb)
                k_s[d, rows, :] = 1.0 - f
                c_s[d, rows, :] = _chunk_cumsum(jnp.log(f) * LOG2E, tri)
                qr = qr_ref[rows, :]
                q_s[d, rows, :] = qr * _sigmoid(qr)
        masks = (_tri(0, False), _tri(1, False))
        state = [[st[d, h] for h in range(NH)] for d in range(2)]
        for it in range(per):
            chunk = []
            for d in range(2):
                ref_row, last_row = _gla_rows(d)
                ci = it if d == 0 else per - 1 - it
                r0 = ci * CH
                rows = slice(r0, r0 + CH)
                cum = c_s[d, rows, :]
                ref = c_s[d, r0 + ref_row:r0 + ref_row + 1, :]
                last = c_s[d, r0 + last_row:r0 + last_row + 1, :]
                q = q_s[d, rows, :]
                k = k_s[d, rows, :]
                chunk.append(dict(
                    ci=ci, rows=rows, qh=(q * jnp.exp2(cum)).astype(BF16), qt=(q * jnp.exp2(cum - ref)).astype(BF16),
                    kt=(k * jnp.exp2(ref - cum)).astype(BF16), kb=(k * jnp.exp2(last - cum)).astype(BF16),
                    el=jnp.exp2(last), vb=ins[d][1][rows, :].astype(BF16)))
            for h in range(NH):
                sl = slice(h * HD, (h + 1) * HD)
                for d in range(2):
                    c = chunk[d]
                    o_ref, s_ref = ins[d][3], ins[d][4]
                    s_t = state[d][h]
                    s_ref[c["ci"], h] = s_t
                    sc = jnp.where(masks[d], _dot_nt(c["qt"][:, sl], c["kt"][:, sl]), 0.0)
                    o_ref[c["rows"], sl] = _dot_nt(c["qh"][:, sl], s_t.astype(BF16)) + _dot(sc.astype(BF16), c["vb"][:, sl])
                    state[d][h] = s_t * c["el"][:, sl] + _dot_tn(c["vb"][:, sl], c["kb"][:, sl])
        for d in range(2):
            for h in range(NH):
                st[d, h] = state[d][h]

    tiles = (_fwd_tile(0, nt), _fwd_tile(1, nt))
    tspec = lambda d, col: pl.BlockSpec((TM, D), lambda i: (tiles[d](i), col))
    sspec = lambda d: pl.BlockSpec((per, NH, HD, HD), lambda i: (tiles[d](i), 0, 0, 0))
    o_shape = jax.ShapeDtypeStruct((t, D), F32)
    s_shape = jax.ShapeDtypeStruct((t // CH, NH, HD, HD), F32)
    return pl.pallas_call(
        body, name=name, grid=(nt,),
        in_specs=[tspec(0, 0), tspec(0, 2), tspec(0, 3), tspec(1, 1), tspec(1, 2), tspec(1, 3), _full_spec((2, D))],
        out_specs=[tspec(0, 0), sspec(0), tspec(1, 0), sspec(1)],
        out_shape=[o_shape, s_shape, o_shape, s_shape],
        scratch_shapes=[pltpu.VMEM((2, NH, HD, HD), F32)] + [pltpu.VMEM((2, TM, D), F32)] * 3,
        compiler_params=_cp(1),
    )(p, p, p, p, p, p, lb2)


def gla_bwd(p, lb2, do, states, direction, prev, name):
    t = p.shape[0]
    nt = t // TM
    per = TM // CH
    ref_row, last_row = _gla_rows(direction)
    tile = (lambda i: (2 * nt - 2 - i) % nt) if direction == 0 else (lambda i: i)
    final = prev is not None
    n_in = 8 if final else 6

    def body(*refs):
        z_ref, v_ref, qr_ref, lb_ref, do_ref, s_ref = refs[:6]
        dz_ref, dv_ref, dq_ref, acc_ref, dst, q_s, k_s, c_s, dq_s, dk_s, dl_s = refs[n_in:]

        @pl.when(pl.program_id(0) == 0)
        def _():
            dst[...] = jnp.zeros_like(dst)
            acc_ref[...] = jnp.zeros_like(acc_ref)

        mask = _tri(direction, False)
        mask_t = _tri(direction, True)
        tri = _cumsum_matrix(direction == 1)
        tri_t = _cumsum_matrix(direction == 0)
        is_last = lax.broadcasted_iota(jnp.int32, (CH, 1), 0) == last_row
        lb = lb_ref[direction:direction + 1, :]
        for ci in range(per):
            rows = slice(ci * CH, (ci + 1) * CH)
            _, f = _gate_values(z_ref[rows, :], lb)
            k_s[rows, :] = 1.0 - f
            c_s[rows, :] = _chunk_cumsum(jnp.log(f) * LOG2E, tri)
            qr = qr_ref[rows, :]
            q_s[rows, :] = qr * _sigmoid(qr)
        for it in range(per):
            ci = per - 1 - it if direction == 0 else it
            r0 = ci * CH
            rows = slice(r0, r0 + CH)
            cum = c_s[rows, :]
            ref = c_s[r0 + ref_row:r0 + ref_row + 1, :]
            last = c_s[r0 + last_row:r0 + last_row + 1, :]
            q = q_s[rows, :]
            k = k_s[rows, :]
            e_h = jnp.exp2(cum)
            e_t = jnp.exp2(cum - ref)
            e_kt = jnp.exp2(ref - cum)
            e_kb = jnp.exp2(last - cum)
            el = jnp.exp2(last)
            qh = (q * e_h).astype(BF16)
            qt = (q * e_t).astype(BF16)
            kt = (k * e_kt).astype(BF16)
            kbf = k * e_kb
            kb = kbf.astype(BF16)
            vb = v_ref[rows, :].astype(BF16)
            dob = do_ref[rows, :].astype(BF16)
            for h in range(NH):
                sl = slice(h * HD, (h + 1) * HD)
                s_t = s_ref[ci, h]
                ds_t = dst[h]
                ds_b = ds_t.astype(BF16)
                d_a = jnp.where(mask, _dot_nt(dob[:, sl], vb[:, sl]), 0.0).astype(BF16)
                a_t = jnp.where(mask_t, _dot_nt(kt[:, sl], qt[:, sl]), 0.0).astype(BF16)
                d_at = jnp.where(mask_t, _dot_nt(vb[:, sl], dob[:, sl]), 0.0).astype(BF16)
                dv = _dot(a_t, dob[:, sl]) + _dot_nt(kb[:, sl], ds_b)
                dkb = _dot(vb[:, sl], ds_b)
                dl_s[it:it + 1, sl] = (el[:, sl] * jnp.sum(ds_t * s_t, axis=0, keepdims=True)
                                       + jnp.sum(dkb * kbf[:, sl], axis=0, keepdims=True))
                dst[h] = ds_t * el[:, sl] + _dot_tn(dob[:, sl], qh[:, sl])
                dq_s[rows, sl] = _dot(dob[:, sl], s_t.astype(BF16)) * e_h[:, sl] + _dot(d_a, kt[:, sl]) * e_t[:, sl]
                dk_s[rows, sl] = _dot(d_at, qt[:, sl]) * e_kt[:, sl] + dkb * e_kb[:, sl]
                if final:
                    dv_ref[rows, sl] = (refs[6][rows, sl] + dv).astype(BF16)
                else:
                    dv_ref[rows, sl] = dv
        for it in range(per):
            ci = per - 1 - it if direction == 0 else it
            rows = slice(ci * CH, (ci + 1) * CH)
            dq = dq_s[rows, :]
            dk = dk_s[rows, :]
            dg = _chunk_cumsum(dq * q_s[rows, :] - dk * k_s[rows, :] + jnp.where(is_last, dl_s[it:it + 1, :], 0.0), tri_t)
            sig, f = _gate_values(z_ref[rows, :], lb)
            df = dg / f - dk
            acc_ref[0:1, :] += jnp.sum(df * (1.0 - sig), axis=0, keepdims=True)
            dz_ref[rows, :] = (df * (1.0 - lb) * sig * (1.0 - sig)).astype(BF16)
            if final:
                qr = qr_ref[rows, :]
                sq = _sigmoid(qr)
                dq_ref[rows, :] = ((refs[7][rows, :] + dq) * (sq * (1.0 + qr * (1.0 - sq)))).astype(BF16)
            else:
                dq_ref[rows, :] = dq

    tspec = lambda col: pl.BlockSpec((TM, D), lambda i: (tile(i), col))
    sspec = pl.BlockSpec((per, NH, HD, HD), lambda i: (tile(i), 0, 0, 0))
    in_specs = [tspec(direction), tspec(2), tspec(3), _full_spec((2, D)), tspec(0), sspec]
    args = [p, p, p, lb2, do, states]
    if final:
        in_specs += [tspec(0), tspec(0)]
        args += list(prev)
    odt = BF16 if final else F32
    return pl.pallas_call(
        body, name=name, grid=(nt,), in_specs=in_specs,
        out_specs=[tspec(0), tspec(0), tspec(0), _full_spec((8, D))],
        out_shape=[jax.ShapeDtypeStruct((t, D), BF16), jax.ShapeDtypeStruct((t, D), odt), jax.ShapeDtypeStruct((t, D), odt),
                   jax.ShapeDtypeStruct((8, D), F32)],
        scratch_shapes=[pltpu.VMEM((NH, HD, HD), F32)] + [pltpu.VMEM((TM, D), F32)] * 5 + [pltpu.VMEM((8, D), F32)],
        compiler_params=_cp(1),
    )(*args)


def loss_bwd(x, gain, target, n_lat, name):
    t = x.shape[0]

    def body(x_ref, gain_ref, tg_ref, dx_ref, acc_ref):
        i = pl.program_id(0)

        @pl.when(i == 0)
        def _():
            acc_ref[...] = jnp.zeros_like(acc_ref)

        latf = (i < n_lat).astype(F32)
        x = x_ref[...]
        gain = gain_ref[...]
        r = lax.rsqrt(jnp.mean(x * x, axis=-1, keepdims=True) + EPS)
        xn = x * r
        err = (xn * gain - tg_ref[...]) * latf
        dy = err * (1.0 / D)
        dxn = dy * gain
        dx_ref[...] = r * (dxn - xn * jnp.mean(dxn * xn, axis=-1, keepdims=True))
        acc_ref[0:1, :] += jnp.sum(dy * xn, axis=0, keepdims=True)
        acc_ref[1:2, :] += jnp.sum(err * err, axis=0, keepdims=True)

    return pl.pallas_call(
        body, name=name, grid=(t // TM,),
        in_specs=[_row_spec(D), _full_spec((1, D)), pl.BlockSpec((TM, D), lambda i: (jnp.minimum(i, n_lat - 1), 0))],
        out_specs=[_row_spec(D), _full_spec((8, D))],
        out_shape=[jax.ShapeDtypeStruct((t, D), F32), jax.ShapeDtypeStruct((8, D), F32)],
        compiler_params=_cp(1),
    )(x, gain, target)


def local_step(xs, target, mods, norm1, norm2, norm_f, lbs, gnorm, cw8, cb, wts, n_lat, on_grads, after_backward):
    t = xs.shape[0]
    saved = []
    cache = {}

    def W(name, idx, after=None):
        if (name, idx) not in cache:
            cache[(name, idx)] = wts(name, idx, after)
        return cache[(name, idx)]

    x = xs
    for i in range(DEPTH):
        j = i // 2
        rec = i % 2 == 0
        n1 = norm1[i:i + 1]
        n2 = norm2[i:i + 1]
        s = {"x_in": x}
        if rec:
            p = proj_fwd(x, n1, mods[i], 0, W("hin", j, x), n_lat, F32, f"hin_fwd_{i}")
            o0, st0, o1, st1 = gla_fwd(p, lbs[j], f"gla_fwd_{i}")
            ex = [o0, o1, p, gnorm[j:j + 1]]
            ex_specs = [_row_spec(D), _row_spec(D), _col_spec(4), _full_spec((1, D))]
            xm, y, ab = outproj_fwd(readout_prologue, ex, ex_specs, W("hout", j, o1), x, mods[i], 0, n_lat, f"hout_fwd_{i}")
            s.update(st0=st0, st1=st1)
        else:
            sft = 1 if j % 2 == 0 else CH
            p = proj_fwd(x, n1, mods[i], 0, W("cin", j, x), n_lat, BF16, f"cin_fwd_{i}")
            ex = _conv_args(sft, p, cw8[j], cb[j])
            ex_specs = _conv_specs(sft, t)
            xm, y, ab = outproj_fwd(make_conv_prologue(n_lat, sft), ex, ex_specs, W("cout", j, p), x, mods[i], 0, n_lat, f"cout_fwd_{i}")
        s.update(p=p, ex=ex, ex_specs=ex_specs, y_mix=y, ab_mix=ab, x_mid=xm)
        x, y2, ab2 = mlp_fwd(xm, n2, mods[i], W("w1", i, xm), W("w2", i, xm), n_lat, f"mlp_fwd_{i}")
        s.update(y_mlp=y2, ab_mlp=ab2)
        saved.append(s)

    dx, acc_loss = loss_bwd(x, norm_f, target, n_lat, "loss")
    small = {"norm_f": acc_loss[0:1], "norm1": [None] * DEPTH, "norm2": [None] * DEPTH, "dmod": [None] * DEPTH,
             "gnorm": [None] * 2, "lb": [None] * 2, "cw": [None] * 2, "cb": [None] * 2}
    bshape = lambda w: jax.ShapeDtypeStruct((t, w), BF16)
    token = jnp.zeros((8, 128), F32)
    for i in reversed(range(DEPTH)):
        j = i // 2
        rec = i % 2 == 0
        s = saved[i]
        n1 = norm1[i:i + 1]
        n2 = norm2[i:i + 1]
        dx, dyb, dp1, hb, acc_n2 = mlp_bwd(dx, s["y_mlp"], s["ab_mlp"], s["x_mid"], n2, mods[i], W("w1", i), W("w2", i), n_lat, token,
                                           f"mlp_bwd_{i}")
        token = on_grads(i, "mlp", {"w2": dw_tn(s["ab_mlp"], dyb, 4, True, token, f"w2_dw_{i}"),
                                    "w1": dw_tn(hb, dp1, 4, False, token, f"w1_dw_{i}")})
        if rec:
            dyb, acc_g1, (do, dgate) = outproj_bwd(
                readout_epilogue, s["ex"], s["ex_specs"], [_row_spec(D), _row_spec(D)],
                [jax.ShapeDtypeStruct((t, D), F32), bshape(D)], W("hout", j), dx, s["y_mix"], mods[i], 0, n_lat, token, f"hout_bwd_{i}")
            dz0, dv0, dq0, acc_l0 = gla_bwd(s["p"], lbs[j], do, s["st0"], 0, None, f"gla_bwd0_{i}")
            dz1, dv, dq, acc_l1 = gla_bwd(s["p"], lbs[j], do, s["st1"], 1, (dv0, dq0), f"gla_bwd1_{i}")
            dx, hb, dpb, acc_n1 = proj_bwd([dz0, dz1, dv, dq, dgate], W("hin", j), s["x_in"], n1, mods[i], 0, dx, n_lat, i == 0,
                                           f"hin_bwd_{i}")
            small["gnorm"][j] = acc_g1[2:3]
            small["lb"][j] = jnp.concatenate([acc_l0[0:1], acc_l1[0:1]], axis=0)
            mix = ("hout", "hin")
        else:
            sft = 1 if j % 2 == 0 else CH
            dyb, acc_g1, (dconv, dgb) = outproj_bwd(
                make_conv_epilogue(n_lat, sft), s["ex"], s["ex_specs"], [_row_spec(D), _row_spec(D)],
                [jax.ShapeDtypeStruct((t, D), F32), bshape(D)], W("cout", j), dx, s["y_mix"], mods[i], 0, n_lat, token, f"cout_bwd_{i}")
            dgc, dxi, acc_c = conv_bwd(dconv, s["p"], cw8[j], sft, n_lat, f"conv_bwd_{i}")
            dx, hb, dpb, acc_n1 = proj_bwd([dgb, dgc, dxi], W("cin", j), s["x_in"], n1, mods[i], 0, dx, n_lat, False, f"cin_bwd_{i}")
            small["cw"][j] = acc_c[0:3]
            small["cb"][j] = acc_c[3:4]
            mix = ("cout", "cin")
        small["norm1"][i] = acc_n1[0:1]
        small["norm2"][i] = acc_n2[0:1]
        z2 = jnp.zeros((2, D), F32)
        small["dmod"][i] = jnp.concatenate([acc_n1[1:3], acc_g1[0:1], acc_n2[1:3], acc_n2[5:6], z2,
                                            acc_n1[3:5], acc_g1[1:2], acc_n2[3:5], acc_n2[6:7], z2], axis=0)
        if i == 0:
            token = after_backward(small, token)
        token = on_grads(i, "mix", {mix[0]: dw_tn(s["ab_mix"], dyb, 1, False, token, f"{mix[0]}_dw_{i}"),
                                    mix[1]: dw_tn(hb, dpb, 4, False, token, f"{mix[1]}_dw_{i}")})
    return acc_loss[1:2], dx, token


RB = 256


def cast_to_slot(w2d, layer, k, chip1, name):
    c = w2d.shape[1]
    nblk = k // RB

    def body(chip_ref, w_ref, o_ref):
        o_ref[0] = w_ref[...].astype(BF16)

    return pl.pallas_call(
        body, name=name,
        grid_spec=pltpu.PrefetchScalarGridSpec(
            num_scalar_prefetch=1, grid=(nblk,),
            in_specs=[pl.BlockSpec((RB, c), lambda i, ch: (layer * nblk + i, 0))],
            out_specs=pl.BlockSpec((1, RB, c), lambda i, ch: (ch[0], i, 0))),
        out_shape=jax.ShapeDtypeStruct((4, k, c), BF16), compiler_params=_cp(1))(chip1, w2d)


def sum_slots(own, land, acc, layer, chip1, name):
    _, k, c = own.shape
    nblk = k // RB

    def body(chip_ref, own_ref, l1_ref, l2_ref, l3_ref, acc_ref, o_ref):
        o_ref[...] = ((own_ref[0].astype(F32) + l1_ref[0].astype(F32)) + l2_ref[0].astype(F32)) + l3_ref[0].astype(F32)

    slot = lambda d: pl.BlockSpec((1, RB, c), lambda i, ch: ((ch[0] + d) % 4, i, 0))
    return pl.pallas_call(
        body, name=name,
        grid_spec=pltpu.PrefetchScalarGridSpec(
            num_scalar_prefetch=1, grid=(nblk,),
            in_specs=[slot(0), slot(1), slot(2), slot(3), ANY],
            out_specs=pl.BlockSpec((RB, c), lambda i, ch: (layer * nblk + i, 0))),
        out_shape=jax.ShapeDtypeStruct(acc.shape, F32), input_output_aliases={5: 0}, compiler_params=_cp(1),
    )(chip1, own, land, land, land, acc)


def _adamw_math(w, g, m, v):
    m = ADAM_B1 * m + (1.0 - ADAM_B1) * g
    v = ADAM_B2 * v + (1.0 - ADAM_B2) * jnp.square(g)
    m_hat = m / (1.0 - ADAM_B1 ** ADAM_STEP)
    v_hat = v / (1.0 - ADAM_B2 ** ADAM_STEP)
    delta = -ADAM_LR * (m_hat / (jnp.sqrt(v_hat) + ADAM_EPS) + ADAM_WD * w)
    return delta, m, v


def adamw(gsrcs, w, m, v, name):
    r, c = w.shape
    rb = RB if r % RB == 0 else r
    n_g = len(gsrcs)

    def body(*refs):
        g = refs[0][...]
        for k in range(1, n_g):
            g = g + refs[k][...]
        w_ref, m_ref, v_ref, g_ref, d_ref, mo_ref, vo_ref = refs[n_g:]
        delta, mo, vo = _adamw_math(w_ref[...], g, m_ref[...], v_ref[...])
        g_ref[...] = g
        d_ref[...] = delta
        mo_ref[...] = mo
        vo_ref[...] = vo

    spec = pl.BlockSpec((rb, c), lambda i: (i, 0))
    shp = jax.ShapeDtypeStruct((r, c), F32)
    return pl.pallas_call(body, name=name, grid=(r // rb,), in_specs=[spec] * (n_g + 3), out_specs=[spec] * 4,
                          out_shape=[shp] * 4, compiler_params=_cp(1))(*gsrcs, w, m, v)


ADA_CB = 512


def ada_fwd(cvec, ada_w, bias, name):
    _, _, nc = ada_w.shape

    def body(c_ref, w_ref, b_ref, o_ref):
        cv = c_ref[...]
        a = (cv * _sigmoid(cv)).astype(BF16)
        o_ref[0] = _dot(a, w_ref[0].astype(BF16)) + b_ref[0]

    return pl.pallas_call(
        body, name=name, grid=(DEPTH, nc // ADA_CB),
        in_specs=[pl.BlockSpec((16, D), lambda i, j: (0, 0)), pl.BlockSpec((1, D, ADA_CB), lambda i, j: (i, 0, j)),
                  pl.BlockSpec((1, 1, ADA_CB), lambda i, j: (i, 0, j))],
        out_specs=pl.BlockSpec((1, 16, ADA_CB), lambda i, j: (i, 0, j)),
        out_shape=jax.ShapeDtypeStruct((DEPTH, 16, nc), F32), compiler_params=_cp(2),
    )(cvec, ada_w, bias)


def ada_bwd(cvec, dcols, ada_w, m, v, name):
    _, _, nc = ada_w.shape

    def body(c_ref, d_ref, w_ref, m_ref, v_ref, g_ref, dl_ref, mo_ref, vo_ref, acc_ref):
        @pl.when(jnp.logical_and(pl.program_id(0) == 0, pl.program_id(1) == 0))
        def _():
            acc_ref[...] = jnp.zeros_like(acc_ref)

        cv = c_ref[...]
        a = (cv * _sigmoid(cv)).astype(BF16)
        db = d_ref[0].astype(BF16)
        w = w_ref[0]
        g = _dot_tn(a, db)
        delta, mo, vo = _adamw_math(w, g, m_ref[0], v_ref[0])
        g_ref[0] = g
        dl_ref[0] = delta
        mo_ref[0] = mo
        vo_ref[0] = vo
        acc_ref[...] += _dot_nt(db[8:16, :], w.astype(BF16))

    wspec = pl.BlockSpec((1, D, ADA_CB), lambda i, j: (i, 0, j))
    wshape = jax.ShapeDtypeStruct(ada_w.shape, F32)
    return pl.pallas_call(
        body, name=name, grid=(DEPTH, nc // ADA_CB),
        in_specs=[pl.BlockSpec((16, D), lambda i, j: (0, 0)), pl.BlockSpec((1, 16, ADA_CB), lambda i, j: (i, 0, j)), wspec, wspec, wspec],
        out_specs=[wspec, wspec, wspec, wspec, pl.BlockSpec((8, D), lambda i, j: (0, 0))],
        out_shape=[wshape, wshape, wshape, wshape, jax.ShapeDtypeStruct((8, D), F32)], compiler_params=_cp(2),
    )(cvec, dcols, ada_w, m, v)


def _place():
    return lax.axis_index("x"), lax.axis_index("y"), lax.axis_index("c")


ANY = pl.BlockSpec(memory_space=pl.ANY)
VMEM_SPEC = pl.BlockSpec(memory_space=pltpu.VMEM)


def small_allgather(buf, deps, name):
    r, c = buf.shape
    n_dep = len(deps)

    def body(*refs):
        in_ref = refs[0]
        out_ref, send_sems, recv_sems, loc_sem = refs[1 + n_dep:]
        x, y, cc = _place()
        me = 4 * x + 2 * y + cc
        loc = pltpu.make_async_copy(in_ref, out_ref.at[me], loc_sem)
        loc.start()
        peers = []
        for k in range(1, 8):
            px = 1 - x if k & 4 else x
            py = 1 - y if k & 2 else y
            pc = 1 - cc if k & 1 else cc
            peers.append((px, py, pc))
        sends = []
        for k, peer in enumerate(peers):
            cp = pltpu.make_async_remote_copy(src_ref=in_ref, dst_ref=out_ref.at[me], send_sem=send_sems.at[k],
                                              recv_sem=recv_sems.at[k], device_id=peer, device_id_type=MESH)
            cp.start()
            sends.append(cp)
        for k, (px, py, pc) in enumerate(peers):
            pltpu.make_async_remote_copy(src_ref=in_ref, dst_ref=out_ref.at[4 * px + 2 * py + pc], send_sem=send_sems.at[k],
                                         recv_sem=recv_sems.at[k], device_id=(px, py, pc), device_id_type=MESH).wait_recv()
        for cp in sends:
            cp.wait_send()
        loc.wait()

    return pl.pallas_call(
        body, name=name, in_specs=[VMEM_SPEC] + [ANY] * n_dep, out_specs=VMEM_SPEC,
        out_shape=jax.ShapeDtypeStruct((8, r, c), buf.dtype),
        scratch_shapes=[pltpu.SemaphoreType.DMA((7,)), pltpu.SemaphoreType.DMA((7,)), pltpu.SemaphoreType.DMA],
    )(buf, *deps)


def _chip_peers(x, y):
    return [(1 - x, y), (x, 1 - y), (1 - x, 1 - y)]


HBM_SPEC = pl.BlockSpec(memory_space=pltpu.HBM)
SEM_SPEC = pl.BlockSpec(memory_space=pltpu.SEMAPHORE)
EFFECT = pltpu.SideEffectType.DATAFLOW_SIDE_EFFECTING


def _hbm(a):
    return pltpu.with_memory_space_constraint(a, pltpu.HBM)


def _split_copy(u, p, peer, dst_slot, chip, land_refs, src_refs, sem_refs, cc):
    px, py = peer
    src = land_refs[u].at[chip] if src_refs is None else src_refs[u].at[2 * px + py]
    return pltpu.make_async_remote_copy(src_ref=src, dst_ref=land_refs[u].at[dst_slot], send_sem=sem_refs[2 * u].at[p],
                                        recv_sem=sem_refs[2 * u + 1].at[p], device_id=(px, py, cc), device_id_type=MESH)


def split_start(lands, srcs, after, name):
    n = len(lands)
    ops = list(lands) + (list(srcs) if srcs is not None else [])
    n_ops = len(ops)

    def body(*refs):
        land_refs = refs[:n]
        src_refs = refs[n:n_ops] if srcs is not None else None
        sem_refs = refs[n_ops + 1:n_ops + 1 + 2 * n]
        x, y, cc = _place()
        chip = 2 * x + y
        for u in range(n):
            for p, peer in enumerate(_chip_peers(x, y)):
                _split_copy(u, p, peer, chip, chip, land_refs, src_refs, sem_refs, cc).start()
        refs[-1][...] = jnp.zeros((8, 128), F32)

    outs = pl.pallas_call(
        body, name=name, in_specs=[HBM_SPEC] * n_ops + [ANY],
        out_specs=[SEM_SPEC] * (2 * n) + [HBM_SPEC] * n_ops + [VMEM_SPEC],
        out_shape=[pltpu.SemaphoreType.DMA((3,))] * (2 * n) + [pltpu.HBM(a.shape, a.dtype) for a in ops]
        + [jax.ShapeDtypeStruct((8, 128), F32)],
        input_output_aliases={k: 2 * n + k for k in range(n_ops)},
        compiler_params=pltpu.CompilerParams(has_side_effects=EFFECT),
    )(*[_hbm(a) for a in ops], after)
    sems = list(outs[:2 * n])
    thru = list(outs[2 * n:2 * n + n_ops])
    return sems, thru[:n], thru[n:], outs[-1]


def split_wait(lands, srcs, sems, after, name):
    n = len(lands)
    ops = list(lands) + (list(srcs) if srcs is not None else [])
    n_ops = len(ops)

    def body(*refs):
        land_refs = refs[:n]
        src_refs = refs[n:n_ops] if srcs is not None else None
        sem_refs = refs[n_ops:n_ops + 2 * n]
        x, y, cc = _place()
        chip = 2 * x + y
        for u in range(n):
            for p, peer in enumerate(_chip_peers(x, y)):
                cp = _split_copy(u, p, peer, 2 * peer[0] + peer[1], chip, land_refs, src_refs, sem_refs, cc)
                cp.wait_send()
                cp.wait_recv()

    outs = pl.pallas_call(
        body, name=name, in_specs=[HBM_SPEC] * n_ops + [SEM_SPEC] * (2 * n) + [ANY],
        out_specs=[HBM_SPEC] * n_ops, out_shape=[pltpu.HBM(a.shape, a.dtype) for a in ops],
        input_output_aliases={k: k for k in range(n_ops)},
        compiler_params=pltpu.CompilerParams(has_side_effects=EFFECT),
    )(*ops, *sems, after)
    return list(outs[:n]), list(outs[n:])


def _sibling_copy(k, src_refs, zone_refs, sem_refs):
    x, y, cc = _place()
    return pltpu.make_async_remote_copy(src_ref=src_refs[k], dst_ref=zone_refs[k], send_sem=sem_refs[2 * k], recv_sem=sem_refs[2 * k + 1],
                                        device_id=(x, y, 1 - cc), device_id_type=MESH)


def sibling_start(parts, name):
    n = len(parts)
    ops = list(parts) + [lax.empty(p.shape, p.dtype) for p in parts]

    def body(*refs):
        for k in range(n):
            _sibling_copy(k, refs[:n], refs[n:2 * n], refs[2 * n:4 * n]).start()

    outs = pl.pallas_call(
        body, name=name, in_specs=[HBM_SPEC] * (2 * n),
        out_specs=[SEM_SPEC] * (2 * n) + [HBM_SPEC] * (2 * n),
        out_shape=[pltpu.SemaphoreType.DMA(())] * (2 * n) + [pltpu.HBM(a.shape, a.dtype) for a in ops],
        input_output_aliases={k: 2 * n + k for k in range(2 * n)},
        compiler_params=pltpu.CompilerParams(has_side_effects=EFFECT),
    )(*[_hbm(a) for a in ops])
    return list(outs[2 * n:3 * n]), list(outs[3 * n:]), list(outs[:2 * n])


def sibling_wait(parts, zones, sems, after, name):
    n = len(parts)

    def body(*refs):
        for k in range(n):
            cp = _sibling_copy(k, refs[:n], refs[n:2 * n], refs[2 * n:4 * n])
            cp.wait_send()
            cp.wait_recv()

    outs = pl.pallas_call(
        body, name=name, in_specs=[HBM_SPEC] * (2 * n) + [SEM_SPEC] * (2 * n) + [ANY],
        out_specs=[HBM_SPEC] * (2 * n), out_shape=[pltpu.HBM(a.shape, a.dtype) for a in list(parts) + list(zones)],
        input_output_aliases={k: k for k in range(2 * n)},
        compiler_params=pltpu.CompilerParams(has_side_effects=EFFECT),
    )(*parts, *zones, *sems, after)
    return list(outs[:n]), list(outs[n:])


SMALL_ROWS = 88
FIN_ROWS = 72


def small_finish(g3, g4, c_ctx, lbp, name):
    def body(g3_ref, g4_ref, cc_ref, lbp_ref, o_ref, s_ref):
        s = g3_ref[0]
        for k in range(1, 8):
            s = s + g3_ref[k]
        s_ref[...] = s
        for i in range(DEPTH):
            o_ref[8 * i:8 * i + 8, :] = s_ref[16 * i:16 * i + 8, :] + s_ref[16 * i + 8:16 * i + 16, :]
        acc = g4_ref[0]
        for k in (2, 4, 6):
            acc = acc + g4_ref[k]
        cc = cc_ref[...]
        sg = _sigmoid(cc)
        row = jnp.sum(acc, axis=0, keepdims=True) * (sg * (1.0 + cc * (1.0 - sg)))
        o_ref[32:40, :] = jnp.broadcast_to(row, (8, D))
        o_ref[40:64, :] = s_ref[64:88, :]
        o_ref[64:72, :] = jnp.zeros((8, D), F32)
        for d in range(2):
            pp = lbp_ref[2 * d:2 * d + 1, :] * lbp_ref[2 * d + 1:2 * d + 2, :] * s_ref[75 + d:76 + d, :]
            o_ref[64 + 2 * d:65 + 2 * d, :] = -pp
            o_ref[65 + 2 * d:66 + 2 * d, :] = pp

    return pl.pallas_call(
        body, name=name, in_specs=[VMEM_SPEC] * 4, out_specs=VMEM_SPEC,
        out_shape=jax.ShapeDtypeStruct((FIN_ROWS, D), F32),
        scratch_shapes=[pltpu.VMEM((SMALL_ROWS, D), F32)],
    )(g3, g4, c_ctx, lbp)


def _pack_rows(arrs):
    flat = jnp.concatenate([a.reshape(-1) for a in arrs])
    n = -(-flat.shape[0] // (8 * D)) * 8 * D
    return jnp.pad(flat, (0, n - flat.shape[0])).reshape(n // D, D)


def _unpack_rows(packed, shapes):
    flat = packed.reshape(-1)
    outs, off = [], 0
    for s in shapes:
        size = 1
        for k in s:
            size *= k
        outs.append(flat[off:off + size].reshape(s))
        off += size
    return outs


def _pad8(a):
    return jnp.pad(a, ((0, 8 - a.shape[0]), (0, 0)))


def kernel(x, c, ctx, c_ctx, ada_w, ada_b, norm1, norm2, norm_f, mlp_w1, mlp_w2, hgrn_w_in, hgrn_lb, hgrn_gnorm, hgrn_w_out, conv_w_in, conv_w, conv_b, conv_w_out, loss_target, m_c_ctx, m_ada_w, m_ada_b, m_norm1, m_norm2, m_norm_f, m_mlp_w1, m_mlp_w2, m_hgrn_w_in, m_hgrn_lb, m_hgrn_gnorm, m_hgrn_w_out, m_conv_w_in, m_conv_w, m_conv_b, m_conv_w_out, v_c_ctx, v_ada_w, v_ada_b, v_norm1, v_norm2, v_norm_f, v_mlp_w1, v_mlp_w2, v_hgrn_w_in, v_hgrn_lb, v_hgrn_gnorm, v_hgrn_w_out, v_conv_w_in, v_conv_w, v_conv_b, v_conv_w_out):
    xi, yi, ci = _place()
    me = 4 * xi + 2 * yi + ci
    chip = 2 * xi + yi
    seq = x.shape[1]
    assert ctx.shape[1] == TM and seq % TM == 0 and (seq + TM) % TMW == 0
    n_lat = seq // TM
    sd = D // 4
    nca = ada_w.shape[2]
    xs = jnp.concatenate([x[0], ctx[0]], axis=0)

    big = [(mlp_w1, m_mlp_w1, v_mlp_w1), (mlp_w2, m_mlp_w2, v_mlp_w2), (hgrn_w_in, m_hgrn_w_in, v_hgrn_w_in),
           (hgrn_w_out, m_hgrn_w_out, v_hgrn_w_out), (conv_w_in, m_conv_w_in, v_conv_w_in), (conv_w_out, m_conv_w_out, v_conv_w_out)]
    big_names = ["w1", "w2", "hin", "hout", "cin", "cout"]
    flat2 = lambda a: a.reshape(a.shape[0] * a.shape[1], a.shape[2])
    tensors = dict(zip(big_names, big))
    chip1 = jnp.reshape(chip, (1,)).astype(jnp.int32)
    order = []
    for i in range(DEPTH):
        order += [("hin", i // 2), ("hout", i // 2)] if i % 2 == 0 else [("cin", i // 2), ("cout", i // 2)]
        order += [("w1", i), ("w2", i)]
    lands = [cast_to_slot(flat2(tensors[n][0]), idx, tensors[n][0].shape[1], chip1, f"cast_{n}_{idx}") for n, idx in order]
    first_sems, first_lands, _, first_token = split_start(lands[:1], None, chip1, "gather_start_first")

    sh_rows = jnp.concatenate([hgrn_lb.reshape(4, sd), conv_w.reshape(6, sd), conv_b.reshape(2, sd)], axis=0)
    buf1 = jnp.concatenate([c, jnp.pad(sh_rows, ((0, 0), (0, D - sd))), jnp.zeros((3, D), F32)], axis=0)
    g1 = small_allgather(buf1, [first_token] + lands[1:], "gather_small_in")
    cvec = jnp.concatenate([g1[:, 0, :], jnp.broadcast_to(c_ctx[None], (8, D))], axis=0)
    shf = g1[0::2, 1:13, :sd].transpose(1, 0, 2).reshape(12, D)
    lb_p = jax.nn.softmax(shf[0:4].reshape(2, 2, D), axis=1)
    lower = jnp.cumsum(lb_p, axis=1) - lb_p[:, :1]
    lbs = [lower[:, 0], lower[:, 1]]
    cw8 = [_pad8(shf[4:7]), _pad8(shf[7:10])]
    cb = [shf[10:11], shf[11:12]]

    bias = lax.dynamic_slice_in_dim(ada_b, chip * nca, nca, axis=1).reshape(DEPTH, 1, nca)
    ada_part = ada_fwd(cvec, ada_w, bias, "ada_fwd")
    g2 = small_allgather(ada_part.reshape(DEPTH * 16, nca), [], "gather_ada")
    ada_full = g2[0::2].reshape(4, DEPTH, 16, nca).transpose(1, 2, 0, 3).reshape(DEPTH, 16, 4 * nca)
    lat = lax.dynamic_slice_in_dim(ada_full, me, 1, axis=1)[:, 0]
    mods = [jnp.stack([_pad8(lat[i].reshape(6, D)), _pad8(ada_full[i, 8].reshape(6, D))]) for i in range(DEPTH)]

    rest_sems, rest_lands, _, _ = split_start(lands[1:], None, g2, "gather_start")
    w_sems = first_sems + rest_sems
    lands = first_lands + rest_lands
    unit = {key: u for u, key in enumerate(order)}

    def wts(n, idx, after):
        u = unit[(n, idx)]
        (w,), _ = split_wait([lands[u]], None, w_sems[2 * u:2 * u + 2], after, f"gather_wait_{n}_{idx}")
        return w.reshape(w.shape[0] * w.shape[1], w.shape[2]) if n in ("w2", "hout", "cout") else w

    started = []

    def on_grads(i, tag, g):
        names = sorted(g)
        gs = [g[n].reshape(4, g[n].shape[0] * g[n].shape[1] // 4, g[n].shape[2]) for n in names]
        sems, zones, srcs, token = split_start([lax.empty(a.shape, BF16) for a in gs], gs, chip1, f"grad_start_{tag}_{i}")
        started.append(([(n, i if n in ("w1", "w2") else i // 2) for n in names], sems, zones, srcs))
        return token

    done = {}

    def after_backward(small, token):
        rows3 = jnp.concatenate(small["dmod"] + small["norm1"] + small["norm2"] + [small["norm_f"]] + small["gnorm"]
                                + [small["lb"][1]] + small["cw"] + small["cb"] + [jnp.tile(token[0:3], (1, D // 128))], axis=0)
        g3 = small_allgather(rows3, [], "gather_small_out")
        dmat = g3[:, :64].reshape(8, DEPTH, 2, 8, D)[:, :, :, :6].transpose(1, 2, 0, 3, 4).reshape(DEPTH, 16, 6 * D)
        dcols = lax.dynamic_slice_in_dim(dmat, chip * nca, nca, axis=2)
        *done["ada"], acc4 = ada_bwd(cvec, dcols, ada_w, m_ada_w, v_ada_w, "ada_bwd")
        g4 = small_allgather(acc4, [], "gather_cctx")
        done["fin"] = small_finish(g3, g4, c_ctx[None], _pad8(lb_p.reshape(4, D)), "small_finish")
        return done["fin"]

    lane, dx, last_token = local_step(xs, loss_target[0], mods, norm1, norm2, norm_f[None], lbs, hgrn_gnorm, cw8, cb, wts, n_lat,
                                      on_grads, after_backward)
    loss = lax.psum(0.5 * jnp.sum(lane) / D, ("x", "y", "c"))
    grad_x = dx[None]
    g_ada_w, d_ada_w, nm_ada_w, nv_ada_w = done["ada"]
    fin = done["fin"]
    cols = lambda a: lax.dynamic_slice_in_dim(a, chip * sd, sd, axis=a.ndim - 1)
    small_g = [fin[32], fin[0:32].reshape(DEPTH, 8, D)[:, :6].reshape(DEPTH, 6 * D), fin[40:44], fin[44:48], fin[48], fin[49:51],
               cols(fin[64:68].reshape(2, 2, D)), cols(fin[53:59].reshape(2, 3, D)), cols(fin[59:61])]
    small_w = [c_ctx, ada_b, norm1, norm2, norm_f, hgrn_gnorm, hgrn_lb, conv_w, conv_b]
    small_m = [m_c_ctx, m_ada_b, m_norm1, m_norm2, m_norm_f, m_hgrn_gnorm, m_hgrn_lb, m_conv_w, m_conv_b]
    small_v = [v_c_ctx, v_ada_b, v_norm1, v_norm2, v_norm_f, v_hgrn_gnorm, v_hgrn_lb, v_conv_w, v_conv_b]
    shapes = [w.shape for w in small_w]
    packed = adamw([_pack_rows(small_g)], _pack_rows(small_w), _pack_rows(small_m), _pack_rows(small_v), "adamw_small")
    s_g, s_d, s_m, s_v = [_unpack_rows(p, shapes) for p in packed]

    units = []
    for ks, sems, zones, srcs in started:
        units += [(key, sems[2 * u:2 * u + 2], zones[u], srcs[u]) for u, key in enumerate(ks)]
    late_keys = (("hin", 0), ("hout", 0))
    acc = {n: lax.empty(flat2(w).shape, F32) for n, (w, _, _) in tensors.items()}

    def finish_units(group, after, name):
        zones, srcs = split_wait([u[2] for u in group], [u[3] for u in group], [s for u in group for s in u[1]], after, name)
        for (key, _, _, _), zone, own in zip(group, zones, srcs):
            acc[key[0]] = sum_slots(own, zone, acc[key[0]], key[1], chip1, f"sum_{key[0]}_{key[1]}")

    finish_units([u for u in units if u[0] not in late_keys], last_token, "grad_wait_early")
    early_names = ["w1", "w2", "cin", "cout"]
    late_names = ["hin", "hout"]
    sib_early = sibling_start([acc[n] for n in early_names], "sibling_start_early")
    finish_units([u for u in units if u[0] in late_keys], sib_early[0][-1], "grad_wait_late")
    sib_late = sibling_start([acc[n] for n in late_names], "sibling_start_late")
    results = {}

    def finish_tensors(names, sib, after, name):
        mine, other = sibling_wait(*sib, after, name)
        for n, pm, po in zip(names, mine, other):
            w, m, v = tensors[n]
            results[n] = [a.reshape(w.shape) for a in adamw([pm, po], flat2(w), flat2(m), flat2(v), f"adamw_{n}")]

    finish_tensors(early_names, sib_early, sib_late[0][-1], "sibling_wait_early")
    finish_tensors(late_names, sib_late, results["cout"][0], "sibling_wait_late")
    b_g, b_d, b_m, b_v = [[results[n][k] for n in big_names] for k in range(4)]

    def ordered(s, a, b):
        return [s[0], a, s[1], s[2], s[3], s[4], b[0], b[1], b[2], s[6], s[5], b[3], b[4], s[7], s[8], b[5]]

    return (loss, grad_x, *ordered(s_g, g_ada_w, b_g), *ordered(s_d, d_ada_w, b_d), *ordered(s_m, nm_ada_w, b_m),
            *ordered(s_v, nv_ada_w, b_v))
```

```python
import functools

import jax
import jax.numpy as jnp
from jax import lax
from jax.experimental import pallas as pl
from jax.experimental.pallas import tpu as pltpu

F32 = jnp.float32
BF16 = jnp.bfloat16
MESH = pl.DeviceIdType.MESH

D = 1024
HD = 128
NH = D // HD
CH = 64
TM = 256
TMW = 768
EPS = 1e-6
DEPTH = 4
VMEM_LIMIT = 56 * 1024 * 1024

ADAM_LR = 0.001
ADAM_B1 = 0.9
ADAM_B2 = 0.999
ADAM_EPS = 1e-08
ADAM_WD = 0.01
ADAM_STEP = 10


def _cp(n_grid):
    return pltpu.CompilerParams(dimension_semantics=("arbitrary",) * n_grid, vmem_limit_bytes=VMEM_LIMIT)


def _dot(a, b):
    return jnp.dot(a, b, preferred_element_type=F32)


def _dot_nt(a, b):
    return lax.dot_general(a, b, (((1,), (1,)), ((), ())), preferred_element_type=F32)


def _dot_tn(a, b):
    return lax.dot_general(a, b, (((0,), (0,)), ((), ())), preferred_element_type=F32)


def _sigmoid(z):
    return 1.0 / (1.0 + jnp.exp(-z))


def _norm_mod(x, gain, sh, sc):
    r = lax.rsqrt(jnp.mean(x * x, axis=-1, keepdims=True) + EPS)
    xn = x * r
    yn = xn * gain
    return r, xn, yn, yn * (1.0 + sc) + sh


def _row_spec(width):
    return pl.BlockSpec((TM, width), lambda i: (i, 0))


def _col_spec(col):
    return pl.BlockSpec((TM, D), lambda i: (i, col))


def _full_spec(shape):
    nd = len(shape)
    return pl.BlockSpec(shape, lambda i: (0,) * nd)


def _mod_spec(n_lat):
    return pl.BlockSpec((1, 8, D), lambda i: (i // n_lat, 0, 0))


def _f32(ref):
    return ref[...].astype(F32)


def proj_fwd(x, gain, mod, m0, w4, n_lat, dtype, name):
    t = x.shape[0]
    nb, _, ns = w4.shape

    def body(x_ref, gain_ref, mod_ref, w_ref, p_ref):
        _, _, _, h = _norm_mod(x_ref[...], gain_ref[...], mod_ref[0, m0:m0 + 1, :], mod_ref[0, m0 + 1:m0 + 2, :])
        hb = h.astype(BF16)
        for c in range(nb):
            p_ref[:, c * ns:(c + 1) * ns] = _dot(hb, w_ref[c]).astype(dtype)

    return pl.pallas_call(
        body, name=name, grid=(t // TM,),
        in_specs=[_row_spec(D), _full_spec((1, D)), _mod_spec(n_lat), _full_spec(w4.shape)],
        out_specs=_row_spec(nb * ns),
        out_shape=jax.ShapeDtypeStruct((t, nb * ns), dtype),
        compiler_params=_cp(1),
    )(x, gain, mod, w4)


def proj_bwd(parts, w4, x, gain, mod, m0, dx_in, n_lat, lat_only, name):
    t = x.shape[0]
    nb, _, ns = w4.shape
    n = nb * ns
    n_parts = len(parts)
    widths = [p.shape[1] for p in parts]
    offs = [sum(widths[:k]) for k in range(n_parts)]
    assert sum(widths) == n
    single = n_parts == 1

    def body(*refs):
        part_refs = refs[:n_parts]
        w_ref, x_ref, gain_ref, mod_ref, dxin_ref = refs[n_parts:n_parts + 5]
        rest = refs[n_parts + 5:]
        if single:
            dx_ref, hb_ref, acc_ref = rest
            src = part_refs[0]
        else:
            dx_ref, hb_ref, acc_ref, dpb_ref = rest
            for p_ref, off, w in zip(part_refs, offs, widths):
                dpb_ref[:, off:off + w] = p_ref[...]
            src = dpb_ref
        i = pl.program_id(0)

        @pl.when(i == 0)
        def _():
            acc_ref[...] = jnp.zeros_like(acc_ref)

        gain = gain_ref[...]
        sc = mod_ref[0, m0 + 1:m0 + 2, :]
        r, xn, yn, h = _norm_mod(x_ref[...], gain, mod_ref[0, m0:m0 + 1, :], sc)
        hb_ref[...] = h.astype(BF16)
        dh = _dot_nt(src[:, 0:ns], w_ref[0])
        for c in range(1, nb):
            dh = dh + _dot_nt(src[:, c * ns:(c + 1) * ns], w_ref[c])
        dsh = jnp.sum(dh, axis=0, keepdims=True)
        dsc = jnp.sum(dh * yn, axis=0, keepdims=True)
        dyn = dh * (1.0 + sc)
        dgain = jnp.sum(dyn * xn, axis=0, keepdims=True)
        dxn = dyn * gain
        dx = dxin_ref[...] + r * (dxn - xn * jnp.mean(dxn * xn, axis=-1, keepdims=True))
        if lat_only:
            @pl.when(i < n_lat)
            def _():
                dx_ref[...] = dx
        else:
            dx_ref[...] = dx
        latf = (i < n_lat).astype(F32)
        ctxf = 1.0 - latf
        acc_ref[0:1, :] += dgain
        acc_ref[1:2, :] += dsh * latf
        acc_ref[2:3, :] += dsc * latf
        acc_ref[3:4, :] += dsh * ctxf
        acc_ref[4:5, :] += dsc * ctxf

    dx_rows = n_lat * TM if lat_only else t
    dx_spec = pl.BlockSpec((TM, D), lambda i: (jnp.minimum(i, n_lat - 1), 0)) if lat_only else _row_spec(D)
    out_specs = [dx_spec, _row_spec(D), _full_spec((8, D))]
    out_shape = [jax.ShapeDtypeStruct((dx_rows, D), F32), jax.ShapeDtypeStruct((t, D), BF16), jax.ShapeDtypeStruct((8, D), F32)]
    if not single:
        out_specs.append(_row_spec(n))
        out_shape.append(jax.ShapeDtypeStruct((t, n), BF16))
    outs = pl.pallas_call(
        body, name=name, grid=(t // TM,),
        in_specs=[_row_spec(w) for w in widths]
        + [_full_spec(w4.shape), _row_spec(D), _full_spec((1, D)), _mod_spec(n_lat), _row_spec(D)],
        out_specs=out_specs, out_shape=out_shape, compiler_params=_cp(1),
    )(*parts, w4, x, gain, mod, dx_in)
    if single:
        return outs[0], outs[1], parts[0], outs[2]
    return outs[0], outs[1], outs[3], outs[2]


def dw_tn(a, b, nb, a_blocked, dep, name):
    t = a.shape[0]
    ka = a.shape[1] // nb if a_blocked else a.shape[1]
    kb = b.shape[1] if a_blocked else b.shape[1] // nb
    n_k = t // TMW

    def body(a_ref, b_ref, _, o_ref, acc):
        k = pl.program_id(1)

        @pl.when(k == 0)
        def _():
            acc[...] = jnp.zeros_like(acc)

        acc[...] += _dot_tn(a_ref[...], b_ref[...])

        @pl.when(k == n_k - 1)
        def _():
            o_ref[0] = acc[...].astype(BF16)

    a_spec = pl.BlockSpec((TMW, ka), (lambda j, k: (k, j)) if a_blocked else (lambda j, k: (k, 0)))
    b_spec = pl.BlockSpec((TMW, kb), (lambda j, k: (k, 0)) if a_blocked else (lambda j, k: (k, j)))
    return pl.pallas_call(
        body, name=name, grid=(nb, n_k),
        in_specs=[a_spec, b_spec, ANY],
        out_specs=pl.BlockSpec((1, ka, kb), lambda j, k: (j, 0, 0)),
        out_shape=jax.ShapeDtypeStruct((nb, ka, kb), BF16),
        scratch_shapes=[pltpu.VMEM((ka, kb), F32)],
        compiler_params=_cp(2),
    )(a, b, dep)


def outproj_fwd(prologue, extras, extra_specs, w, x, mod, m0, n_lat, name):
    t = x.shape[0]
    k = w.shape[0]
    n_extra = len(extras)

    def body(*refs):
        ex = refs[:n_extra]
        w_ref, x_ref, mod_ref, xo_ref, y_ref, ab_ref = refs[n_extra:]
        ab = prologue(pl.program_id(0), *ex).astype(BF16)
        ab_ref[...] = ab
        y = _dot(ab, w_ref[...])
        y_ref[...] = y
        xo_ref[...] = x_ref[...] + mod_ref[0, m0 + 2:m0 + 3, :] * y

    return pl.pallas_call(
        body, name=name, grid=(t // TM,),
        in_specs=list(extra_specs) + [_full_spec(w.shape), _row_spec(D), _mod_spec(n_lat)],
        out_specs=[_row_spec(D), _row_spec(D), _row_spec(k)],
        out_shape=[jax.ShapeDtypeStruct((t, D), F32), jax.ShapeDtypeStruct((t, D), F32), jax.ShapeDtypeStruct((t, k), BF16)],
        compiler_params=_cp(1),
    )(*extras, w, x, mod)


def outproj_bwd(epilogue, extras, extra_specs, ep_out_specs, ep_out_shapes, w, dxn, y, mod, m0, n_lat, dep, name):
    t = dxn.shape[0]
    n_extra = len(extras)

    def body(*refs):
        ex = refs[:n_extra]
        w_ref, dxn_ref, y_ref, mod_ref, _, dyb_ref, acc_ref = refs[n_extra:n_extra + 7]
        ep_outs = refs[n_extra + 7:]
        i = pl.program_id(0)

        @pl.when(i == 0)
        def _():
            acc_ref[...] = jnp.zeros_like(acc_ref)

        dxv = dxn_ref[...]
        dyb = (dxv * mod_ref[0, m0 + 2:m0 + 3, :]).astype(BF16)
        dyb_ref[...] = dyb
        dg = jnp.sum(dxv * y_ref[...], axis=0, keepdims=True)
        latf = (i < n_lat).astype(F32)
        acc_ref[0:1, :] += dg * latf
        acc_ref[1:2, :] += dg * (1.0 - latf)
        epilogue(i, _dot_nt(dyb, w_ref[...]), ex, ep_outs, acc_ref)

    outs = pl.pallas_call(
        body, name=name, grid=(t // TM,),
        in_specs=list(extra_specs) + [_full_spec(w.shape), _row_spec(D), _row_spec(D), _mod_spec(n_lat), ANY],
        out_specs=[_row_spec(D), _full_spec((8, D))] + list(ep_out_specs),
        out_shape=[jax.ShapeDtypeStruct((t, D), BF16), jax.ShapeDtypeStruct((8, D), F32)] + list(ep_out_shapes),
        compiler_params=_cp(1),
    )(*extras, w, dxn, y, mod, dep)
    return outs[0], outs[1], outs[2:]


def mlp_fwd(x, gain, mod, w1, w2, n_lat, name):
    t = x.shape[0]
    nb, _, ns = w1.shape

    def body(x_ref, gain_ref, mod_ref, w1_ref, w2_ref, xo_ref, y_ref, ab_ref):
        x = x_ref[...]
        _, _, _, h = _norm_mod(x, gain_ref[...], mod_ref[0, 3:4, :], mod_ref[0, 4:5, :])
        hb = h.astype(BF16)
        y = None
        for c in range(nb):
            a = jnp.square(jnp.maximum(_dot(hb, w1_ref[c]), 0.0)).astype(BF16)
            ab_ref[:, c * ns:(c + 1) * ns] = a
            yc = _dot(a, w2_ref[c * ns:(c + 1) * ns, :])
            y = yc if y is None else y + yc
        y_ref[...] = y
        xo_ref[...] = x + mod_ref[0, 5:6, :] * y

    return pl.pallas_call(
        body, name=name, grid=(t // TM,),
        in_specs=[_row_spec(D), _full_spec((1, D)), _mod_spec(n_lat), _full_spec(w1.shape), _full_spec(w2.shape)],
        out_specs=[_row_spec(D), _row_spec(D), _row_spec(nb * ns)],
        out_shape=[jax.ShapeDtypeStruct((t, D), F32), jax.ShapeDtypeStruct((t, D), F32), jax.ShapeDtypeStruct((t, nb * ns), BF16)],
        compiler_params=_cp(1),
    )(x, gain, mod, w1, w2)


def mlp_bwd(dxn, y, ab, x, gain, mod, w1, w2, n_lat, dep, name):
    t = x.shape[0]
    nb, _, ns = w1.shape

    def body(dxn_ref, y_ref, ab_ref, x_ref, gain_ref, mod_ref, w1_ref, w2_ref, _, dx_ref, dyb_ref, dp_ref, hb_ref, acc_ref):
        i = pl.program_id(0)

        @pl.when(i == 0)
        def _():
            acc_ref[...] = jnp.zeros_like(acc_ref)

        dxv = dxn_ref[...]
        dyb = (dxv * mod_ref[0, 5:6, :]).astype(BF16)
        dyb_ref[...] = dyb
        dg = jnp.sum(dxv * y_ref[...], axis=0, keepdims=True)
        gain = gain_ref[...]
        sc = mod_ref[0, 4:5, :]
        r, xn, yn, h = _norm_mod(x_ref[...], gain, mod_ref[0, 3:4, :], sc)
        hb_ref[...] = h.astype(BF16)
        dh = None
        for c in range(nb):
            cols = slice(c * ns, (c + 1) * ns)
            da = _dot_nt(dyb, w2_ref[cols, :])
            dp = (da * (2.0 * jnp.sqrt(ab_ref[:, cols].astype(F32)))).astype(BF16)
            dp_ref[:, cols] = dp
            d = _dot_nt(dp, w1_ref[c])
            dh = d if dh is None else dh + d
        dsh = jnp.sum(dh, axis=0, keepdims=True)
        dsc = jnp.sum(dh * yn, axis=0, keepdims=True)
        dyn = dh * (1.0 + sc)
        dgain = jnp.sum(dyn * xn, axis=0, keepdims=True)
        dxn_ = dyn * gain
        dx_ref[...] = dxv + r * (dxn_ - xn * jnp.mean(dxn_ * xn, axis=-1, keepdims=True))
        latf = (i < n_lat).astype(F32)
        ctxf = 1.0 - latf
        acc_ref[0:1, :] += dgain
        acc_ref[1:2, :] += dsh * latf
        acc_ref[2:3, :] += dsc * latf
        acc_ref[3:4, :] += dsh * ctxf
        acc_ref[4:5, :] += dsc * ctxf
        acc_ref[5:6, :] += dg * latf
        acc_ref[6:7, :] += dg * ctxf

    return pl.pallas_call(
        body, name=name, grid=(t // TM,),
        in_specs=[_row_spec(D), _row_spec(D), _row_spec(nb * ns), _row_spec(D), _full_spec((1, D)), _mod_spec(n_lat),
                  _full_spec(w1.shape), _full_spec(w2.shape), ANY],
        out_specs=[_row_spec(D), _row_spec(D), _row_spec(nb * ns), _row_spec(D), _full_spec((8, D))],
        out_shape=[jax.ShapeDtypeStruct((t, D), F32), jax.ShapeDtypeStruct((t, D), BF16), jax.ShapeDtypeStruct((t, nb * ns), BF16),
                   jax.ShapeDtypeStruct((t, D), BF16), jax.ShapeDtypeStruct((8, D), F32)],
        compiler_params=_cp(1),
    )(dxn, y, ab, x, gain, mod, w1, w2, dep)


def readout_prologue(i, o0_ref, o1_ref, gate_ref, gn_ref):
    o = o0_ref[...] + o1_ref[...]
    gate = gate_ref[...]
    w = gn_ref[...] * (gate * _sigmoid(gate))
    pieces = []
    for h in range(NH):
        sl = slice(h * HD, (h + 1) * HD)
        oh = o[:, sl]
        pieces.append(oh * lax.rsqrt(jnp.mean(oh * oh, axis=-1, keepdims=True) + EPS) * w[:, sl])
    return jnp.concatenate(pieces, axis=1)


def readout_epilogue(i, da, ex, outs, acc_ref):
    o0_ref, o1_ref, gate_ref, gn_ref = ex
    do_ref, dgate_ref = outs
    o = o0_ref[...] + o1_ref[...]
    gate = gate_ref[...]
    gn = gn_ref[...]
    sg = _sigmoid(gate)
    silu = gate * sg
    dsilu = sg * (1.0 + gate * (1.0 - sg))
    for h in range(NH):
        sl = slice(h * HD, (h + 1) * HD)
        oh = o[:, sl]
        r = lax.rsqrt(jnp.mean(oh * oh, axis=-1, keepdims=True) + EPS)
        nh = oh * r
        dah = da[:, sl]
        acc_ref[2:3, sl] += jnp.sum(dah * nh * silu[:, sl], axis=0, keepdims=True)
        dgate_ref[:, sl] = (dah * nh * gn[:, sl] * dsilu[:, sl]).astype(BF16)
        dn = dah * gn[:, sl] * silu[:, sl]
        do_ref[:, sl] = r * (dn - nh * jnp.mean(dn * nh, axis=-1, keepdims=True))


def _seg_masks(i, n_lat):
    rows = lax.broadcasted_iota(jnp.int32, (TM, 1), 0)
    latf = (i < n_lat).astype(F32)
    ctxf = 1.0 - latf
    prev_ok = (rows % CH != 0).astype(F32) * latf + (rows != 0).astype(F32) * ctxf
    next_ok = (rows % CH != CH - 1).astype(F32) * latf + (rows != TM - 1).astype(F32) * ctxf
    return prev_ok, next_ok


def _shifts(i, n_lat, sft, cur, halo_prev, halo_next):
    if sft == 1:
        prev_ok, next_ok = _seg_masks(i, n_lat)
        return pltpu.roll(cur, 1, 0) * prev_ok, pltpu.roll(cur, TM - 1, 0) * next_ok
    has_prev = jnp.logical_and(i > 0, i < n_lat).astype(F32)
    has_next = (i < n_lat - 1).astype(F32)
    prev = jnp.concatenate([halo_prev * has_prev, cur[:TM - CH]], axis=0)
    nxt = jnp.concatenate([cur[CH:], halo_next * has_next], axis=0)
    return prev, nxt


def _conv_u(sft, ex):
    if sft == 1:
        gb_ref, gc_ref, xi_ref, cw_ref, cb_ref = ex
        return gb_ref, _f32(gc_ref) * _f32(xi_ref), None, None, cw_ref, cb_ref
    gb_ref, gc_ref, xi_ref, gcp_ref, xip_ref, gcn_ref, xin_ref, cw_ref, cb_ref = ex
    return gb_ref, _f32(gc_ref) * _f32(xi_ref), _f32(gcp_ref) * _f32(xip_ref), _f32(gcn_ref) * _f32(xin_ref), cw_ref, cb_ref


def _conv_value(i, n_lat, sft, ex):
    gb_ref, u, up, un, cw_ref, cb_ref = _conv_u(sft, ex)
    u_prev, u_next = _shifts(i, n_lat, sft, u, up, un)
    return gb_ref, cb_ref[...] + cw_ref[0:1, :] * u_prev + cw_ref[1:2, :] * u + cw_ref[2:3, :] * u_next


def make_conv_prologue(n_lat, sft):
    def prologue(i, *ex):
        gb_ref, conv = _conv_value(i, n_lat, sft, ex)
        return _f32(gb_ref) * conv
    return prologue


def make_conv_epilogue(n_lat, sft):
    def epilogue(i, da, ex, outs, acc_ref):
        gb_ref, conv = _conv_value(i, n_lat, sft, ex)
        outs[0][...] = da * _f32(gb_ref)
        outs[1][...] = (da * conv).astype(BF16)
    return epilogue


def _conv_specs(sft, t):
    specs = [_col_spec(0), _col_spec(1), _col_spec(2)]
    if sft != 1:
        per = TM // CH
        last = t // CH - 1
        for fn in (lambda i: jnp.maximum(i * per - 1, 0), lambda i: jnp.minimum(i * per + per, last)):
            for col in (1, 2):
                specs.append(pl.BlockSpec((CH, D), functools.partial(lambda i, f, c: (f(i), c), f=fn, c=col)))
    return specs + [_full_spec((8, D)), _full_spec((1, D))]


def _conv_args(sft, p, cw8, cb):
    return [p] * (3 if sft == 1 else 7) + [cw8, cb]


def conv_bwd(dconv, p, cw8, sft, n_lat, name):
    t = dconv.shape[0]
    halo = sft != 1

    def body(*refs):
        if halo:
            dc_ref, dcp_ref, dcn_ref, gc_ref, xi_ref, gcp_ref, xip_ref, gcn_ref, xin_ref, cw_ref, dgc_ref, dxi_ref, acc_ref = refs
            up, un = _f32(gcp_ref) * _f32(xip_ref), _f32(gcn_ref) * _f32(xin_ref)
            dcp, dcn = dcp_ref[...], dcn_ref[...]
        else:
            dc_ref, gc_ref, xi_ref, cw_ref, dgc_ref, dxi_ref, acc_ref = refs
            up = un = dcp = dcn = None
        i = pl.program_id(0)

        @pl.when(i == 0)
        def _():
            acc_ref[...] = jnp.zeros_like(acc_ref)

        gc = _f32(gc_ref)
        xi = _f32(xi_ref)
        u = gc * xi
        dc = dc_ref[...]
        u_prev, u_next = _shifts(i, n_lat, sft, u, up, un)
        dc_prev, dc_next = _shifts(i, n_lat, sft, dc, dcp, dcn)
        acc_ref[0:1, :] += jnp.sum(dc * u_prev, axis=0, keepdims=True)
        acc_ref[1:2, :] += jnp.sum(dc * u, axis=0, keepdims=True)
        acc_ref[2:3, :] += jnp.sum(dc * u_next, axis=0, keepdims=True)
        acc_ref[3:4, :] += jnp.sum(dc, axis=0, keepdims=True)
        du = cw_ref[0:1, :] * dc_next + cw_ref[1:2, :] * dc + cw_ref[2:3, :] * dc_prev
        dgc_ref[...] = (du * xi).astype(BF16)
        dxi_ref[...] = (du * gc).astype(BF16)

    per = TM // CH
    last = t // CH - 1
    prev_i = lambda i: jnp.maximum(i * per - 1, 0)
    next_i = lambda i: jnp.minimum(i * per + per, last)
    if halo:
        in_specs = [_row_spec(D), pl.BlockSpec((CH, D), lambda i: (prev_i(i), 0)), pl.BlockSpec((CH, D), lambda i: (next_i(i), 0)),
                    _col_spec(1), _col_spec(2),
                    pl.BlockSpec((CH, D), lambda i: (prev_i(i), 1)), pl.BlockSpec((CH, D), lambda i: (prev_i(i), 2)),
                    pl.BlockSpec((CH, D), lambda i: (next_i(i), 1)), pl.BlockSpec((CH, D), lambda i: (next_i(i), 2)),
                    _full_spec((8, D))]
        args = [dconv, dconv, dconv, p, p, p, p, p, p, cw8]
    else:
        in_specs = [_row_spec(D), _col_spec(1), _col_spec(2), _full_spec((8, D))]
        args = [dconv, p, p, cw8]
    return pl.pallas_call(
        body, name=name, grid=(t // TM,), in_specs=in_specs,
        out_specs=[_row_spec(D), _row_spec(D), _full_spec((8, D))],
        out_shape=[jax.ShapeDtypeStruct((t, D), BF16), jax.ShapeDtypeStruct((t, D), BF16), jax.ShapeDtypeStruct((8, D), F32)],
        compiler_params=_cp(1),
    )(*args)


LOG2E = 1.4426950408889634


def _cumsum_matrix(reverse):
    r = lax.broadcasted_iota(jnp.int32, (CH, CH), 0)
    c = lax.broadcasted_iota(jnp.int32, (CH, CH), 1)
    return (r <= c if reverse else r >= c).astype(BF16)


def _chunk_cumsum(g, tri):
    hi = g.astype(BF16)
    lo = (g - hi.astype(F32)).astype(BF16)
    return _dot(tri, hi) + _dot(tri, lo)


def _gate_values(z, lb):
    sig = _sigmoid(z)
    f = lb + (1.0 - lb) * sig
    return sig, f


def _tri(direction, transposed):
    r = lax.broadcasted_iota(jnp.int32, (CH, CH), 0)
    c = lax.broadcasted_iota(jnp.int32, (CH, CH), 1)
    lower = (direction == 0) != transposed
    return r >= c if lower else r <= c


def _gla_rows(direction):
    return (CH // 2 - 1, CH - 1) if direction == 0 else (CH // 2, 0)


def _fwd_tile(direction, nt):
    return (lambda i: (i + nt - 1) % nt) if direction == 0 else (lambda i: nt - 1 - i)


def gla_fwd(p, lb2, name):
    t = p.shape[0]
    nt = t // TM
    per = TM // CH

    def body(z0_ref, v0_ref, q0_ref, z1_ref, v1_ref, q1_ref, lb_ref, o0_ref, s0_ref, o1_ref, s1_ref, st, q_s, k_s, c_s):
        @pl.when(pl.program_id(0) == 0)
        def _():
            st[...] = jnp.zeros_like(st)

        ins = ((z0_ref, v0_ref, q0_ref, o0_ref, s0_ref), (z1_ref, v1_ref, q1_ref, o1_ref, s1_ref))
        for d in range(2):
            z_ref, _, qr_ref, _, _ = ins[d]
            tri = _cumsum_matrix(d == 1)
            lb = lb_ref[d:d + 1, :]
            for ci in range(per):
                rows = slice(ci * CH, (ci + 1) * CH)
                _, f = _gate_values(z_ref[rows, :], lb)
                k_s[d, rows, :] = 1.0 - f
                c_s[d, rows, :] = _chunk_cumsum(jnp.log(f) * LOG2E, tri)
                qr = qr_ref[rows, :]
                q_s[d, rows, :] = qr * _sigmoid(qr)
        masks = (_tri(0, False), _tri(1, False))
        state = [[st[d, h] for h in range(NH)] for d in range(2)]
        for it in range(per):
            chunk = []
            for d in range(2):
                ref_row, last_row = _gla_rows(d)
                ci = it if d == 0 else per - 1 - it
                r0 = ci * CH
                rows = slice(r0, r0 + CH)
                cum = c_s[d, rows, :]
                ref = c_s[d, r0 + ref_row:r0 + ref_row + 1, :]
                last = c_s[d, r0 + last_row:r0 + last_row + 1, :]
                q = q_s[d, rows, :]
                k = k_s[d, rows, :]
                chunk.append(dict(
                    ci=ci, rows=rows, qh=(q * jnp.exp2(cum)).astype(BF16), qt=(q * jnp.exp2(cum - ref)).astype(BF16),
                    kt=(k * jnp.exp2(ref - cum)).astype(BF16), kb=(k * jnp.exp2(last - cum)).astype(BF16),
                    el=jnp.exp2(last), vb=ins[d][1][rows, :].astype(BF16)))
            for h in range(NH):
                sl = slice(h * HD, (h + 1) * HD)
                for d in range(2):
                    c = chunk[d]
                    o_ref, s_ref = ins[d][3], ins[d][4]
                    s_t = state[d][h]
                    s_ref[c["ci"], h] = s_t
                    sc = jnp.where(masks[d], _dot_nt(c["qt"][:, sl], c["kt"][:, sl]), 0.0)
                    o_ref[c["rows"], sl] = _dot_nt(c["qh"][:, sl], s_t.astype(BF16)) + _dot(sc.astype(BF16), c["vb"][:, sl])
                    state[d][h] = s_t * c["el"][:, sl] + _dot_tn(c["vb"][:, sl], c["kb"][:, sl])
        for d in range(2):
            for h in range(NH):
                st[d, h] = state[d][h]

    tiles = (_fwd_tile(0, nt), _fwd_tile(1, nt))
    tspec = lambda d, col: pl.BlockSpec((TM, D), lambda i: (tiles[d](i), col))
    sspec = lambda d: pl.BlockSpec((per, NH, HD, HD), lambda i: (tiles[d](i), 0, 0, 0))
    o_shape = jax.ShapeDtypeStruct((t, D), F32)
    s_shape = jax.ShapeDtypeStruct((t // CH, NH, HD, HD), F32)
    return pl.pallas_call(
        body, name=name, grid=(nt,),
        in_specs=[tspec(0, 0), tspec(0, 2), tspec(0, 3), tspec(1, 1), tspec(1, 2), tspec(1, 3), _full_spec((2, D))],
        out_specs=[tspec(0, 0), sspec(0), tspec(1, 0), sspec(1)],
        out_shape=[o_shape, s_shape, o_shape, s_shape],
        scratch_shapes=[pltpu.VMEM((2, NH, HD, HD), F32)] + [pltpu.VMEM((2, TM, D), F32)] * 3,
        compiler_params=_cp(1),
    )(p, p, p, p, p, p, lb2)


def gla_bwd(p, lb2, do, states, direction, prev, name):
    t = p.shape[0]
    nt = t // TM
    per = TM // CH
    ref_row, last_row = _gla_rows(direction)
    tile = (lambda i: (2 * nt - 2 - i) % nt) if direction == 0 else (lambda i: i)
    final = prev is not None
    n_in = 8 if final else 6

    def body(*refs):
        z_ref, v_ref, qr_ref, lb_ref, do_ref, s_ref = refs[:6]
        dz_ref, dv_ref, dq_ref, acc_ref, dst, q_s, k_s, c_s, dq_s, dk_s, dl_s = refs[n_in:]

        @pl.when(pl.program_id(0) == 0)
        def _():
            dst[...] = jnp.zeros_like(dst)
            acc_ref[...] = jnp.zeros_like(acc_ref)

        mask = _tri(direction, False)
        mask_t = _tri(direction, True)
        tri = _cumsum_matrix(direction == 1)
        tri_t = _cumsum_matrix(direction == 0)
        is_last = lax.broadcasted_iota(jnp.int32, (CH, 1), 0) == last_row
        lb = lb_ref[direction:direction + 1, :]
        for ci in range(per):
            rows = slice(ci * CH, (ci + 1) * CH)
            _, f = _gate_values(z_ref[rows, :], lb)
            k_s[rows, :] = 1.0 - f
            c_s[rows, :] = _chunk_cumsum(jnp.log(f) * LOG2E, tri)
            qr = qr_ref[rows, :]
            q_s[rows, :] = qr * _sigmoid(qr)
        for it in range(per):
            ci = per - 1 - it if direction == 0 else it
            r0 = ci * CH
            rows = slice(r0, r0 + CH)
            cum = c_s[rows, :]
            ref = c_s[r0 + ref_row:r0 + ref_row + 1, :]
            last = c_s[r0 + last_row:r0 + last_row + 1, :]
            q = q_s[rows, :]
            k = k_s[rows, :]
            e_h = jnp.exp2(cum)
            e_t = jnp.exp2(cum - ref)
            e_kt = jnp.exp2(ref - cum)
            e_kb = jnp.exp2(last - cum)
            el = jnp.exp2(last)
            qh = (q * e_h).astype(BF16)
            qt = (q * e_t).astype(BF16)
            kt = (k * e_kt).astype(BF16)
            kbf = k * e_kb
            kb = kbf.astype(BF16)
            vb = v_ref[rows, :].astype(BF16)
            dob = do_ref[rows, :].astype(BF16)
            for h in range(NH):
                sl = slice(h * HD, (h + 1) * HD)
                s_t = s_ref[ci, h]
                ds_t = dst[h]
                ds_b = ds_t.astype(BF16)
                d_a = jnp.where(mask, _dot_nt(dob[:, sl], vb[:, sl]), 0.0).astype(BF16)
                a_t = jnp.where(mask_t, _dot_nt(kt[:, sl], qt[:, sl]), 0.0).astype(BF16)
                d_at = jnp.where(mask_t, _dot_nt(vb[:, sl], dob[:, sl]), 0.0).astype(BF16)
                dv = _dot(a_t, dob[:, sl]) + _dot_nt(kb[:, sl], ds_b)
                dkb = _dot(vb[:, sl], ds_b)
                dl_s[it:it + 1, sl] = (el[:, sl] * jnp.sum(ds_t * s_t, axis=0, keepdims=True)
                                       + jnp.sum(dkb * kbf[:, sl], axis=0, keepdims=True))
                dst[h] = ds_t * el[:, sl] + _dot_tn(dob[:, sl], qh[:, sl])
                dq_s[rows, sl] = _dot(dob[:, sl], s_t.astype(BF16)) * e_h[:, sl] + _dot(d_a, kt[:, sl]) * e_t[:, sl]
                dk_s[rows, sl] = _dot(d_at, qt[:, sl]) * e_kt[:, sl] + dkb * e_kb[:, sl]
                if final:
                    dv_ref[rows, sl] = (refs[6][rows, sl] + dv).astype(BF16)
                else:
                    dv_ref[rows, sl] = dv
        for it in range(per):
            ci = per - 1 - it if direction == 0 else it
            rows = slice(ci * CH, (ci + 1) * CH)
            dq = dq_s[rows, :]
            dk = dk_s[rows, :]
            dg = _chunk_cumsum(dq * q_s[rows, :] - dk * k_s[rows, :] + jnp.where(is_last, dl_s[it:it + 1, :], 0.0), tri_t)
            sig, f = _gate_values(z_ref[rows, :], lb)
            df = dg / f - dk
            acc_ref[0:1, :] += jnp.sum(df * (1.0 - sig), axis=0, keepdims=True)
            dz_ref[rows, :] = (df * (1.0 - lb) * sig * (1.0 - sig)).astype(BF16)
            if final:
                qr = qr_ref[rows, :]
                sq = _sigmoid(qr)
                dq_ref[rows, :] = ((refs[7][rows, :] + dq) * (sq * (1.0 + qr * (1.0 - sq)))).astype(BF16)
            else:
                dq_ref[rows, :] = dq

    tspec = lambda col: pl.BlockSpec((TM, D), lambda i: (tile(i), col))
    sspec = pl.BlockSpec((per, NH, HD, HD), lambda i: (tile(i), 0, 0, 0))
    in_specs = [tspec(direction), tspec(2), tspec(3), _full_spec((2, D)), tspec(0), sspec]
    args = [p, p, p, lb2, do, states]
    if final:
        in_specs += [tspec(0), tspec(0)]
        args += list(prev)
    odt = BF16 if final else F32
    return pl.pallas_call(
        body, name=name, grid=(nt,), in_specs=in_specs,
        out_specs=[tspec(0), tspec(0), tspec(0), _full_spec((8, D))],
        out_shape=[jax.ShapeDtypeStruct((t, D), BF16), jax.ShapeDtypeStruct((t, D), odt), jax.ShapeDtypeStruct((t, D), odt),
                   jax.ShapeDtypeStruct((8, D), F32)],
        scratch_shapes=[pltpu.VMEM((NH, HD, HD), F32)] + [pltpu.VMEM((TM, D), F32)] * 5 + [pltpu.VMEM((8, D), F32)],
        compiler_params=_cp(1),
    )(*args)


def loss_bwd(x, gain, target, n_lat, name):
    t = x.shape[0]

    def body(x_ref, gain_ref, tg_ref, dx_ref, acc_ref):
        i = pl.program_id(0)

        @pl.when(i == 0)
        def _():
            acc_ref[...] = jnp.zeros_like(acc_ref)

        latf = (i < n_lat).astype(F32)
        x = x_ref[...]
        gain = gain_ref[...]
        r = lax.rsqrt(jnp.mean(x * x, axis=-1, keepdims=True) + EPS)
        xn = x * r
        err = (xn * gain - tg_ref[...]) * latf
        dy = err * (1.0 / D)
        dxn = dy * gain
        dx_ref[...] = r * (dxn - xn * jnp.mean(dxn * xn, axis=-1, keepdims=True))
        acc_ref[0:1, :] += jnp.sum(dy * xn, axis=0, keepdims=True)
        acc_ref[1:2, :] += jnp.sum(err * err, axis=0, keepdims=True)

    return pl.pallas_call(
        body, name=name, grid=(t // TM,),
        in_specs=[_row_spec(D), _full_spec((1, D)), pl.BlockSpec((TM, D), lambda i: (jnp.minimum(i, n_lat - 1), 0))],
        out_specs=[_row_spec(D), _full_spec((8, D))],
        out_shape=[jax.ShapeDtypeStruct((t, D), F32), jax.ShapeDtypeStruct((8, D), F32)],
        compiler_params=_cp(1),
    )(x, gain, target)


def local_step(xs, target, mods, norm1, norm2, norm_f, lbs, gnorm, cw8, cb, wts, n_lat, on_grads, after_backward):
    t = xs.shape[0]
    saved = []
    cache = {}

    def W(name, idx, after=None):
        if (name, idx) not in cache:
            cache[(name, idx)] = wts(name, idx, after)
        return cache[(name, idx)]

    x = xs
    for i in range(DEPTH):
        j = i // 2
        rec = i % 2 == 0
        n1 = norm1[i:i + 1]
        n2 = norm2[i:i + 1]
        s = {"x_in": x}
        if rec:
            p = proj_fwd(x, n1, mods[i], 0, W("hin", j, x), n_lat, F32, f"hin_fwd_{i}")
            o0, st0, o1, st1 = gla_fwd(p, lbs[j], f"gla_fwd_{i}")
            ex = [o0, o1, p, gnorm[j:j + 1]]
            ex_specs = [_row_spec(D), _row_spec(D), _col_spec(4), _full_spec((1, D))]
            xm, y, ab = outproj_fwd(readout_prologue, ex, ex_specs, W("hout", j, o1), x, mods[i], 0, n_lat, f"hout_fwd_{i}")
            s.update(st0=st0, st1=st1)
        else:
            sft = 1 if j % 2 == 0 else CH
            p = proj_fwd(x, n1, mods[i], 0, W("cin", j, x), n_lat, BF16, f"cin_fwd_{i}")
            ex = _conv_args(sft, p, cw8[j], cb[j])
            ex_specs = _conv_specs(sft, t)
            xm, y, ab = outproj_fwd(make_conv_prologue(n_lat, sft), ex, ex_specs, W("cout", j, p), x, mods[i], 0, n_lat, f"cout_fwd_{i}")
        s.update(p=p, ex=ex, ex_specs=ex_specs, y_mix=y, ab_mix=ab, x_mid=xm)
        x, y2, ab2 = mlp_fwd(xm, n2, mods[i], W("w1", i, xm), W("w2", i, xm), n_lat, f"mlp_fwd_{i}")
        s.update(y_mlp=y2, ab_mlp=ab2)
        saved.append(s)

    dx, acc_loss = loss_bwd(x, norm_f, target, n_lat, "loss")
    small = {"norm_f": acc_loss[0:1], "norm1": [None] * DEPTH, "norm2": [None] * DEPTH, "dmod": [None] * DEPTH,
             "gnorm": [None] * 2, "lb": [None] * 2, "cw": [None] * 2, "cb": [None] * 2}
    bshape = lambda w: jax.ShapeDtypeStruct((t, w), BF16)
    token = jnp.zeros((8, 128), F32)
    for i in reversed(range(DEPTH)):
        j = i // 2
        rec = i % 2 == 0
        s = saved[i]
        n1 = norm1[i:i + 1]
        n2 = norm2[i:i + 1]
        dx, dyb, dp1, hb, acc_n2 = mlp_bwd(dx, s["y_mlp"], s["ab_mlp"], s["x_mid"], n2, mods[i], W("w1", i), W("w2", i), n_lat, token,
                                           f"mlp_bwd_{i}")
        token = on_grads(i, "mlp", {"w2": dw_tn(s["ab_mlp"], dyb, 4, True, token, f"w2_dw_{i}"),
                                    "w1": dw_tn(hb, dp1, 4, False, token, f"w1_dw_{i}")})
        if rec:
            dyb, acc_g1, (do, dgate) = outproj_bwd(
                readout_epilogue, s["ex"], s["ex_specs"], [_row_spec(D), _row_spec(D)],
                [jax.ShapeDtypeStruct((t, D), F32), bshape(D)], W("hout", j), dx, s["y_mix"], mods[i], 0, n_lat, token, f"hout_bwd_{i}")
            dz0, dv0, dq0, acc_l0 = gla_bwd(s["p"], lbs[j], do, s["st0"], 0, None, f"gla_bwd0_{i}")
            dz1, dv, dq, acc_l1 = gla_bwd(s["p"], lbs[j], do, s["st1"], 1, (dv0, dq0), f"gla_bwd1_{i}")
            dx, hb, dpb, acc_n1 = proj_bwd([dz0, dz1, dv, dq, dgate], W("hin", j), s["x_in"], n1, mods[i], 0, dx, n_lat, i == 0,
                                           f"hin_bwd_{i}")
            small["gnorm"][j] = acc_g1[2:3]
            small["lb"][j] = jnp.concatenate([acc_l0[0:1], acc_l1[0:1]], axis=0)
            mix = ("hout", "hin")
        else:
            sft = 1 if j % 2 == 0 else CH
            dyb, acc_g1, (dconv, dgb) = outproj_bwd(
                make_conv_epilogue(n_lat, sft), s["ex"], s["ex_specs"], [_row_spec(D), _row_spec(D)],
                [jax.ShapeDtypeStruct((t, D), F32), bshape(D)], W("cout", j), dx, s["y_mix"], mods[i], 0, n_lat, token, f"cout_bwd_{i}")
            dgc, dxi, acc_c = conv_bwd(dconv, s["p"], cw8[j], sft, n_lat, f"conv_bwd_{i}")
            dx, hb, dpb, acc_n1 = proj_bwd([dgb, dgc, dxi], W("cin", j), s["x_in"], n1, mods[i], 0, dx, n_lat, False, f"cin_bwd_{i}")
            small["cw"][j] = acc_c[0:3]
            small["cb"][j] = acc_c[3:4]
            mix = ("cout", "cin")
        small["norm1"][i] = acc_n1[0:1]
        small["norm2"][i] = acc_n2[0:1]
        z2 = jnp.zeros((2, D), F32)
        small["dmod"][i] = jnp.concatenate([acc_n1[1:3], acc_g1[0:1], acc_n2[1:3], acc_n2[5:6], z2,
                                            acc_n1[3:5], acc_g1[1:2], acc_n2[3:5], acc_n2[6:7], z2], axis=0)
        if i == 0:
            token = after_backward(small, token)
        token = on_grads(i, "mix", {mix[0]: dw_tn(s["ab_mix"], dyb, 1, False, token, f"{mix[0]}_dw_{i}"),
                                    mix[1]: dw_tn(hb, dpb, 4, False, token, f"{mix[1]}_dw_{i}")})
    return acc_loss[1:2], dx, token


RB = 256


def cast_to_slot(w2d, layer, k, chip1, name):
    c = w2d.shape[1]
    nblk = k // RB

    def body(chip_ref, w_ref, o_ref):
        o_ref[0] = w_ref[...].astype(BF16)

    return pl.pallas_call(
        body, name=name,
        grid_spec=pltpu.PrefetchScalarGridSpec(
            num_scalar_prefetch=1, grid=(nblk,),
            in_specs=[pl.BlockSpec((RB, c), lambda i, ch: (layer * nblk + i, 0))],
            out_specs=pl.BlockSpec((1, RB, c), lambda i, ch: (ch[0], i, 0))),
        out_shape=jax.ShapeDtypeStruct((4, k, c), BF16), compiler_params=_cp(1))(chip1, w2d)


def sum_slots(own, land, acc, layer, chip1, name):
    _, k, c = own.shape
    nblk = k // RB

    def body(chip_ref, own_ref, l1_ref, l2_ref, l3_ref, acc_ref, o_ref):
        o_ref[...] = ((own_ref[0].astype(F32) + l1_ref[0].astype(F32)) + l2_ref[0].astype(F32)) + l3_ref[0].astype(F32)

    slot = lambda d: pl.BlockSpec((1, RB, c), lambda i, ch: ((ch[0] + d) % 4, i, 0))
    return pl.pallas_call(
        body, name=name,
        grid_spec=pltpu.PrefetchScalarGridSpec(
            num_scalar_prefetch=1, grid=(nblk,),
            in_specs=[slot(0), slot(1), slot(2), slot(3), ANY],
            out_specs=pl.BlockSpec((RB, c), lambda i, ch: (layer * nblk + i, 0))),
        out_shape=jax.ShapeDtypeStruct(acc.shape, F32), input_output_aliases={5: 0}, compiler_params=_cp(1),
    )(chip1, own, land, land, land, acc)


def _adamw_math(w, g, m, v):
    m = ADAM_B1 * m + (1.0 - ADAM_B1) * g
    v = ADAM_B2 * v + (1.0 - ADAM_B2) * jnp.square(g)
    m_hat = m / (1.0 - ADAM_B1 ** ADAM_STEP)
    v_hat = v / (1.0 - ADAM_B2 ** ADAM_STEP)
    delta = -ADAM_LR * (m_hat / (jnp.sqrt(v_hat) + ADAM_EPS) + ADAM_WD * w)
    return delta, m, v


def adamw(gsrcs, w, m, v, name):
    r, c = w.shape
    rb = RB if r % RB == 0 else r
    n_g = len(gsrcs)

    def body(*refs):
        g = refs[0][...]
        for k in range(1, n_g):
            g = g + refs[k][...]
        w_ref, m_ref, v_ref, g_ref, d_ref, mo_ref, vo_ref = refs[n_g:]
        delta, mo, vo = _adamw_math(w_ref[...], g, m_ref[...], v_ref[...])
        g_ref[...] = g
        d_ref[...] = delta
        mo_ref[...] = mo
        vo_ref[...] = vo

    spec = pl.BlockSpec((rb, c), lambda i: (i, 0))
    shp = jax.ShapeDtypeStruct((r, c), F32)
    return pl.pallas_call(body, name=name, grid=(r // rb,), in_specs=[spec] * (n_g + 3), out_specs=[spec] * 4,
                          out_shape=[shp] * 4, compiler_params=_cp(1))(*gsrcs, w, m, v)


ADA_CB = 512


def ada_fwd(cvec, ada_w, bias, name):
    _, _, nc = ada_w.shape

    def body(c_ref, w_ref, b_ref, o_ref):
        cv = c_ref[...]
        a = (cv * _sigmoid(cv)).astype(BF16)
        o_ref[0] = _dot(a, w_ref[0].astype(BF16)) + b_ref[0]

    return pl.pallas_call(
        body, name=name, grid=(DEPTH, nc // ADA_CB),
        in_specs=[pl.BlockSpec((16, D), lambda i, j: (0, 0)), pl.BlockSpec((1, D, ADA_CB), lambda i, j: (i, 0, j)),
                  pl.BlockSpec((1, 1, ADA_CB), lambda i, j: (i, 0, j))],
        out_specs=pl.BlockSpec((1, 16, ADA_CB), lambda i, j: (i, 0, j)),
        out_shape=jax.ShapeDtypeStruct((DEPTH, 16, nc), F32), compiler_params=_cp(2),
    )(cvec, ada_w, bias)


def ada_bwd(cvec, dcols, ada_w, m, v, name):
    _, _, nc = ada_w.shape

    def body(c_ref, d_ref, w_ref, m_ref, v_ref, g_ref, dl_ref, mo_ref, vo_ref, acc_ref):
        @pl.when(jnp.logical_and(pl.program_id(0) == 0, pl.program_id(1) == 0))
        def _():
            acc_ref[...] = jnp.zeros_like(acc_ref)

        cv = c_ref[...]
        a = (cv * _sigmoid(cv)).astype(BF16)
        db = d_ref[0].astype(BF16)
        w = w_ref[0]
        g = _dot_tn(a, db)
        delta, mo, vo = _adamw_math(w, g, m_ref[0], v_ref[0])
        g_ref[0] = g
        dl_ref[0] = delta
        mo_ref[0] = mo
        vo_ref[0] = vo
        acc_ref[...] += _dot_nt(db[8:16, :], w.astype(BF16))

    wspec = pl.BlockSpec((1, D, ADA_CB), lambda i, j: (i, 0, j))
    wshape = jax.ShapeDtypeStruct(ada_w.shape, F32)
    return pl.pallas_call(
        body, name=name, grid=(DEPTH, nc // ADA_CB),
        in_specs=[pl.BlockSpec((16, D), lambda i, j: (0, 0)), pl.BlockSpec((1, 16, ADA_CB), lambda i, j: (i, 0, j)), wspec, wspec, wspec],
        out_specs=[wspec, wspec, wspec, wspec, pl.BlockSpec((8, D), lambda i, j: (0, 0))],
        out_shape=[wshape, wshape, wshape, wshape, jax.ShapeDtypeStruct((8, D), F32)], compiler_params=_cp(2),
    )(cvec, dcols, ada_w, m, v)


def _place():
    return lax.axis_index("x"), lax.axis_index("y"), lax.axis_index("c")


ANY = pl.BlockSpec(memory_space=pl.ANY)
VMEM_SPEC = pl.BlockSpec(memory_space=pltpu.VMEM)


def small_allgather(buf, deps, name):
    r, c = buf.shape
    n_dep = len(deps)

    def body(*refs):
        in_ref = refs[0]
        out_ref, send_sems, recv_sems, loc_sem = refs[1 + n_dep:]
        x, y, cc = _place()
        me = 4 * x + 2 * y + cc
        loc = pltpu.make_async_copy(in_ref, out_ref.at[me], loc_sem)
        loc.start()
        peers = []
        for k in range(1, 8):
            px = 1 - x if k & 4 else x
            py = 1 - y if k & 2 else y
            pc = 1 - cc if k & 1 else cc
            peers.append((px, py, pc))
        sends = []
        for k, peer in enumerate(peers):
            cp = pltpu.make_async_remote_copy(src_ref=in_ref, dst_ref=out_ref.at[me], send_sem=send_sems.at[k],
                                              recv_sem=recv_sems.at[k], device_id=peer, device_id_type=MESH)
            cp.start()
            sends.append(cp)
        for k, (px, py, pc) in enumerate(peers):
            pltpu.make_async_remote_copy(src_ref=in_ref, dst_ref=out_ref.at[4 * px + 2 * py + pc], send_sem=send_sems.at[k],
                                         recv_sem=recv_sems.at[k], device_id=(px, py, pc), device_id_type=MESH).wait_recv()
        for cp in sends:
            cp.wait_send()
        loc.wait()

    return pl.pallas_call(
        body, name=name, in_specs=[VMEM_SPEC] + [ANY] * n_dep, out_specs=VMEM_SPEC,
        out_shape=jax.ShapeDtypeStruct((8, r, c), buf.dtype),
        scratch_shapes=[pltpu.SemaphoreType.DMA((7,)), pltpu.SemaphoreType.DMA((7,)), pltpu.SemaphoreType.DMA],
    )(buf, *deps)


def _chip_peers(x, y):
    return [(1 - x, y), (x, 1 - y), (1 - x, 1 - y)]


HBM_SPEC = pl.BlockSpec(memory_space=pltpu.HBM)
SEM_SPEC = pl.BlockSpec(memory_space=pltpu.SEMAPHORE)
EFFECT = pltpu.SideEffectType.DATAFLOW_SIDE_EFFECTING


def _hbm(a):
    return pltpu.with_memory_space_constraint(a, pltpu.HBM)


def _split_copy(u, p, peer, dst_slot, chip, land_refs, src_refs, sem_refs, cc):
    px, py = peer
    src = land_refs[u].at[chip] if src_refs is None else src_refs[u].at[2 * px + py]
    return pltpu.make_async_remote_copy(src_ref=src, dst_ref=land_refs[u].at[dst_slot], send_sem=sem_refs[2 * u].at[p],
                                        recv_sem=sem_refs[2 * u + 1].at[p], device_id=(px, py, cc), device_id_type=MESH)


def split_start(lands, srcs, after, name):
    n = len(lands)
    ops = list(lands) + (list(srcs) if srcs is not None else [])
    n_ops = len(ops)

    def body(*refs):
        land_refs = refs[:n]
        src_refs = refs[n:n_ops] if srcs is not None else None
        sem_refs = refs[n_ops + 1:n_ops + 1 + 2 * n]
        x, y, cc = _place()
        chip = 2 * x + y
        for u in range(n):
            for p, peer in enumerate(_chip_peers(x, y)):
                _split_copy(u, p, peer, chip, chip, land_refs, src_refs, sem_refs, cc).start()
        refs[-1][...] = jnp.zeros((8, 128), F32)

    outs = pl.pallas_call(
        body, name=name, in_specs=[HBM_SPEC] * n_ops + [ANY],
        out_specs=[SEM_SPEC] * (2 * n) + [HBM_SPEC] * n_ops + [VMEM_SPEC],
        out_shape=[pltpu.SemaphoreType.DMA((3,))] * (2 * n) + [pltpu.HBM(a.shape, a.dtype) for a in ops]
        + [jax.ShapeDtypeStruct((8, 128), F32)],
        input_output_aliases={k: 2 * n + k for k in range(n_ops)},
        compiler_params=pltpu.CompilerParams(has_side_effects=EFFECT),
    )(*[_hbm(a) for a in ops], after)
    sems = list(outs[:2 * n])
    thru = list(outs[2 * n:2 * n + n_ops])
    return sems, thru[:n], thru[n:], outs[-1]


def split_wait(lands, srcs, sems, after, name):
    n = len(lands)
    ops = list(lands) + (list(srcs) if srcs is not None else [])
    n_ops = len(ops)

    def body(*refs):
        land_refs = refs[:n]
        src_refs = refs[n:n_ops] if srcs is not None else None
        sem_refs = refs[n_ops:n_ops + 2 * n]
        x, y, cc = _place()
        chip = 2 * x + y
        for u in range(n):
            for p, peer in enumerate(_chip_peers(x, y)):
                cp = _split_copy(u, p, peer, 2 * peer[0] + peer[1], chip, land_refs, src_refs, sem_refs, cc)
                cp.wait_send()
                cp.wait_recv()

    outs = pl.pallas_call(
        body, name=name, in_specs=[HBM_SPEC] * n_ops + [SEM_SPEC] * (2 * n) + [ANY],
        out_specs=[HBM_SPEC] * n_ops, out_shape=[pltpu.HBM(a.shape, a.dtype) for a in ops],
        input_output_aliases={k: k for k in range(n_ops)},
        compiler_params=pltpu.CompilerParams(has_side_effects=EFFECT),
    )(*ops, *sems, after)
    return list(outs[:n]), list(outs[n:])


def _sibling_copy(k, src_refs, zone_refs, sem_refs):
    x, y, cc = _place()
    return pltpu.make_async_remote_copy(src_ref=src_refs[k], dst_ref=zone_refs[k], send_sem=sem_refs[2 * k], recv_sem=sem_refs[2 * k + 1],
                                        device_id=(x, y, 1 - cc), device_id_type=MESH)


def sibling_start(parts, name):
    n = len(parts)
    ops = list(parts) + [lax.empty(p.shape, p.dtype) for p in parts]

    def body(*refs):
        for k in range(n):
            _sibling_copy(k, refs[:n], refs[n:2 * n], refs[2 * n:4 * n]).start()

    outs = pl.pallas_call(
        body, name=name, in_specs=[HBM_SPEC] * (2 * n),
        out_specs=[SEM_SPEC] * (2 * n) + [HBM_SPEC] * (2 * n),
        out_shape=[pltpu.SemaphoreType.DMA(())] * (2 * n) + [pltpu.HBM(a.shape, a.dtype) for a in ops],
        input_output_aliases={k: 2 * n + k for k in range(2 * n)},
        compiler_params=pltpu.CompilerParams(has_side_effects=EFFECT),
    )(*[_hbm(a) for a in ops])
    return list(outs[2 * n:3 * n]), list(outs[3 * n:]), list(outs[:2 * n])


def sibling_wait(parts, zones, sems, after, name):
    n = len(parts)

    def body(*refs):
        for k in range(n):
            cp = _sibling_copy(k, refs[:n], refs[n:2 * n], refs[2 * n:4 * n])
            cp.wait_send()
            cp.wait_recv()

    outs = pl.pallas_call(
        body, name=name, in_specs=[HBM_SPEC] * (2 * n) + [SEM_SPEC] * (2 * n) + [ANY],
        out_specs=[HBM_SPEC] * (2 * n), out_shape=[pltpu.HBM(a.shape, a.dtype) for a in list(parts) + list(zones)],
        input_output_aliases={k: k for k in range(2 * n)},
        compiler_params=pltpu.CompilerParams(has_side_effects=EFFECT),
    )(*parts, *zones, *sems, after)
    return list(outs[:n]), list(outs[n:])


SMALL_ROWS = 88
FIN_ROWS = 72


def small_finish(g3, g4, c_ctx, lbp, name):
    def body(g3_ref, g4_ref, cc_ref, lbp_ref, o_ref, s_ref):
        s = g3_ref[0]
        for k in range(1, 8):
            s = s + g3_ref[k]
        s_ref[...] = s
        for i in range(DEPTH):
            o_ref[8 * i:8 * i + 8, :] = s_ref[16 * i:16 * i + 8, :] + s_ref[16 * i + 8:16 * i + 16, :]
        acc = g4_ref[0]
        for k in (2, 4, 6):
            acc = acc + g4_ref[k]
        cc = cc_ref[...]
        sg = _sigmoid(cc)
        row = jnp.sum(acc, axis=0, keepdims=True) * (sg * (1.0 + cc * (1.0 - sg)))
        o_ref[32:40, :] = jnp.broadcast_to(row, (8, D))
        o_ref[40:64, :] = s_ref[64:88, :]
        o_ref[64:72, :] = jnp.zeros((8, D), F32)
        for d in range(2):
            pp = lbp_ref[2 * d:2 * d + 1, :] * lbp_ref[2 * d + 1:2 * d + 2, :] * s_ref[75 + d:76 + d, :]
            o_ref[64 + 2 * d:65 + 2 * d, :] = -pp
            o_ref[65 + 2 * d:66 + 2 * d, :] = pp

    return pl.pallas_call(
        body, name=name, in_specs=[VMEM_SPEC] * 4, out_specs=VMEM_SPEC,
        out_shape=jax.ShapeDtypeStruct((FIN_ROWS, D), F32),
        scratch_shapes=[pltpu.VMEM((SMALL_ROWS, D), F32)],
    )(g3, g4, c_ctx, lbp)


def _pack_rows(arrs):
    flat = jnp.concatenate([a.reshape(-1) for a in arrs])
    n = -(-flat.shape[0] // (8 * D)) * 8 * D
    return jnp.pad(flat, (0, n - flat.shape[0])).reshape(n // D, D)


def _unpack_rows(packed, shapes):
    flat = packed.reshape(-1)
    outs, off = [], 0
    for s in shapes:
        size = 1
        for k in s:
            size *= k
        outs.append(flat[off:off + size].reshape(s))
        off += size
    return outs


def _pad8(a):
    return jnp.pad(a, ((0, 8 - a.shape[0]), (0, 0)))


def kernel(x, c, ctx, c_ctx, ada_w, ada_b, norm1, norm2, norm_f, mlp_w1, mlp_w2, hgrn_w_in, hgrn_lb, hgrn_gnorm, hgrn_w_out, conv_w_in, conv_w, conv_b, conv_w_out, loss_target, m_c_ctx, m_ada_w, m_ada_b, m_norm1, m_norm2, m_norm_f, m_mlp_w1, m_mlp_w2, m_hgrn_w_in, m_hgrn_lb, m_hgrn_gnorm, m_hgrn_w_out, m_conv_w_in, m_conv_w, m_conv_b, m_conv_w_out, v_c_ctx, v_ada_w, v_ada_b, v_norm1, v_norm2, v_norm_f, v_mlp_w1, v_mlp_w2, v_hgrn_w_in, v_hgrn_lb, v_hgrn_gnorm, v_hgrn_w_out, v_conv_w_in, v_conv_w, v_conv_b, v_conv_w_out):
    xi, yi, ci = _place()
    me = 4 * xi + 2 * yi + ci
    chip = 2 * xi + yi
    seq = x.shape[1]
    assert ctx.shape[1] == TM and seq % TM == 0 and (seq + TM) % TMW == 0
    n_lat = seq // TM
    sd = D // 4
    nca = ada_w.shape[2]
    xs = jnp.concatenate([x[0], ctx[0]], axis=0)

    big = [(mlp_w1, m_mlp_w1, v_mlp_w1), (mlp_w2, m_mlp_w2, v_mlp_w2), (hgrn_w_in, m_hgrn_w_in, v_hgrn_w_in),
           (hgrn_w_out, m_hgrn_w_out, v_hgrn_w_out), (conv_w_in, m_conv_w_in, v_conv_w_in), (conv_w_out, m_conv_w_out, v_conv_w_out)]
    big_names = ["w1", "w2", "hin", "hout", "cin", "cout"]
    flat2 = lambda a: a.reshape(a.shape[0] * a.shape[1], a.shape[2])
    tensors = dict(zip(big_names, big))
    chip1 = jnp.reshape(chip, (1,)).astype(jnp.int32)
    order = []
    for i in range(DEPTH):
        order += [("hin", i // 2), ("hout", i // 2)] if i % 2 == 0 else [("cin", i // 2), ("cout", i // 2)]
        order += [("w1", i), ("w2", i)]
    lands = [cast_to_slot(flat2(tensors[n][0]), idx, tensors[n][0].shape[1], chip1, f"cast_{n}_{idx}") for n, idx in order]
    sh_rows = jnp.concatenate([hgrn_lb.reshape(4, sd), conv_w.reshape(6, sd), conv_b.reshape(2, sd)], axis=0)
    buf1 = jnp.concatenate([c, jnp.pad(sh_rows, ((0, 0), (0, D - sd))), jnp.zeros((3, D), F32)], axis=0)
    g1 = small_allgather(buf1, [], "gather_small_in")
    first_sems, first_lands, _, first_token = split_start(lands[:1], None, g1, "gather_start_first")
    cvec = jnp.concatenate([g1[:, 0, :], jnp.broadcast_to(c_ctx[None], (8, D))], axis=0)
    shf = g1[0::2, 1:13, :sd].transpose(1, 0, 2).reshape(12, D)
    lb_p = jax.nn.softmax(shf[0:4].reshape(2, 2, D), axis=1)
    lower = jnp.cumsum(lb_p, axis=1) - lb_p[:, :1]
    lbs = [lower[:, 0], lower[:, 1]]
    cw8 = [_pad8(shf[4:7]), _pad8(shf[7:10])]
    cb = [shf[10:11], shf[11:12]]

    bias = lax.dynamic_slice_in_dim(ada_b, chip * nca, nca, axis=1).reshape(DEPTH, 1, nca)
    ada_part = ada_fwd(cvec, ada_w, bias, "ada_fwd")
    g2 = small_allgather(ada_part.reshape(DEPTH * 16, nca), [first_token] + lands[1:], "gather_ada")
    ada_full = g2[0::2].reshape(4, DEPTH, 16, nca).transpose(1, 2, 0, 3).reshape(DEPTH, 16, 4 * nca)
    lat = lax.dynamic_slice_in_dim(ada_full, me, 1, axis=1)[:, 0]
    mods = [jnp.stack([_pad8(lat[i].reshape(6, D)), _pad8(ada_full[i, 8].reshape(6, D))]) for i in range(DEPTH)]

    rest_sems, rest_lands, _, rest_token = split_start(lands[1:], None, g2, "gather_start")
    w_sems = first_sems + rest_sems
    lands = first_lands + rest_lands
    unit = {key: u for u, key in enumerate(order)}

    def wts(n, idx, after):
        u = unit[(n, idx)]
        if u == 0:
            after = rest_token
        (w,), _ = split_wait([lands[u]], None, w_sems[2 * u:2 * u + 2], after, f"gather_wait_{n}_{idx}")
        return w.reshape(w.shape[0] * w.shape[1], w.shape[2]) if n in ("w2", "hout", "cout") else w

    started = []

    def on_grads(i, tag, g):
        names = sorted(g)
        gs = [g[n].reshape(4, g[n].shape[0] * g[n].shape[1] // 4, g[n].shape[2]) for n in names]
        sems, zones, srcs, token = split_start([lax.empty(a.shape, BF16) for a in gs], gs, chip1, f"grad_start_{tag}_{i}")
        started.append(([(n, i if n in ("w1", "w2") else i // 2) for n in names], sems, zones, srcs))
        return token

    done = {}
    acc = {n: lax.empty(flat2(w).shape, F32) for n, (w, _, _) in tensors.items()}
    early_names = ["w1", "w2", "cin", "cout"]
    late_names = ["hin", "hout"]

    def finish_units(group, after, name):
        units = [(key, sems[2 * u:2 * u + 2], zones[u], srcs[u]) for ks, sems, zones, srcs in group for u, key in enumerate(ks)]
        zones, srcs = split_wait([u[2] for u in units], [u[3] for u in units], [s for u in units for s in u[1]], after, name)
        for (key, _, _, _), zone, own in zip(units, zones, srcs):
            acc[key[0]] = sum_slots(own, zone, acc[key[0]], key[1], chip1, f"sum_{key[0]}_{key[1]}")

    def after_backward(small, token):
        rows3 = jnp.concatenate(small["dmod"] + small["norm1"] + small["norm2"] + [small["norm_f"]] + small["gnorm"]
                                + [small["lb"][1]] + small["cw"] + small["cb"] + [jnp.tile(token[0:3], (1, D // 128))], axis=0)
        g3 = small_allgather(rows3, [], "gather_small_out")
        dmat = g3[:, :64].reshape(8, DEPTH, 2, 8, D)[:, :, :, :6].transpose(1, 2, 0, 3, 4).reshape(DEPTH, 16, 6 * D)
        dcols = lax.dynamic_slice_in_dim(dmat, chip * nca, nca, axis=2)
        *done["ada"], acc4 = ada_bwd(cvec, dcols, ada_w, m_ada_w, v_ada_w, "ada_bwd")
        g4 = small_allgather(acc4, [], "gather_cctx")
        done["fin"] = small_finish(g3, g4, c_ctx[None], _pad8(lb_p.reshape(4, D)), "small_finish")
        finish_units(list(started), done["fin"], "grad_wait_early")
        done["sib_early"] = sibling_start([acc[n] for n in early_names], "sibling_start_early")
        return done["sib_early"][0][-1]

    lane, dx, last_token = local_step(xs, loss_target[0], mods, norm1, norm2, norm_f[None], lbs, hgrn_gnorm, cw8, cb, wts, n_lat,
                                      on_grads, after_backward)
    loss = lax.psum(0.5 * jnp.sum(lane) / D, ("x", "y", "c"))
    grad_x = dx[None]
    g_ada_w, d_ada_w, nm_ada_w, nv_ada_w = done["ada"]
    fin = done["fin"]
    cols = lambda a: lax.dynamic_slice_in_dim(a, chip * sd, sd, axis=a.ndim - 1)
    small_g = [fin[32], fin[0:32].reshape(DEPTH, 8, D)[:, :6].reshape(DEPTH, 6 * D), fin[40:44], fin[44:48], fin[48], fin[49:51],
               cols(fin[64:68].reshape(2, 2, D)), cols(fin[53:59].reshape(2, 3, D)), cols(fin[59:61])]
    small_w = [c_ctx, ada_b, norm1, norm2, norm_f, hgrn_gnorm, hgrn_lb, conv_w, conv_b]
    small_m = [m_c_ctx, m_ada_b, m_norm1, m_norm2, m_norm_f, m_hgrn_gnorm, m_hgrn_lb, m_conv_w, m_conv_b]
    small_v = [v_c_ctx, v_ada_b, v_norm1, v_norm2, v_norm_f, v_hgrn_gnorm, v_hgrn_lb, v_conv_w, v_conv_b]
    shapes = [w.shape for w in small_w]
    packed = adamw([_pack_rows(small_g)], _pack_rows(small_w), _pack_rows(small_m), _pack_rows(small_v), "adamw_small")
    s_g, s_d, s_m, s_v = [_unpack_rows(p, shapes) for p in packed]

    results = {}

    def finish_tensors(names, sib, after, name):
        mine, other = sibling_wait(*sib, after, name)
        for n, pm, po in zip(names, mine, other):
            w, m, v = tensors[n]
            results[n] = [a.reshape(w.shape) for a in adamw([pm, po], flat2(w), flat2(m), flat2(v), f"adamw_{n}")]

    finish_tensors(early_names, done["sib_early"], last_token, "sibling_wait_early")
    finish_units(started[-1:], results["cout"][0], "grad_wait_late")
    sib_late = sibling_start([acc[n] for n in late_names], "sibling_start_late")
    finish_tensors(late_names, sib_late, chip1, "sibling_wait_late")
    b_g, b_d, b_m, b_v = [[results[n][k] for n in big_names] for k in range(4)]

    def ordered(s, a, b):
        return [s[0], a, s[1], s[2], s[3], s[4], b[0], b[1], b[2], s[6], s[5], b[3], b[4], s[7], s[8], b[5]]

    return (loss, grad_x, *ordered(s_g, g_ada_w, b_g), *ordered(s_d, d_ada_w, b_d), *ordered(s_m, nm_ada_w, b_m),
            *ordered(s_v, nv_ada_w, b_v))
```

```python
import functools

import jax
import jax.numpy as jnp
from jax import lax
from jax.experimental import pallas as pl
from jax.experimental.pallas import tpu as pltpu

F32 = jnp.float32
BF16 = jnp.bfloat16
MESH = pl.DeviceIdType.MESH

D = 1024
HD = 128
NH = D // HD
CH = 64
TM = 256
TMW = 768
EPS = 1e-6
DEPTH = 4
VMEM_LIMIT = 56 * 1024 * 1024

ADAM_LR = 0.001
ADAM_B1 = 0.9
ADAM_B2 = 0.999
ADAM_EPS = 1e-08
ADAM_WD = 0.01
ADAM_STEP = 10


def _cp(n_grid):
    return pltpu.CompilerParams(dimension_semantics=("arbitrary",) * n_grid, vmem_limit_bytes=VMEM_LIMIT)


def _dot(a, b):
    return jnp.dot(a, b, preferred_element_type=F32)


def _dot_nt(a, b):
    return lax.dot_general(a, b, (((1,), (1,)), ((), ())), preferred_element_type=F32)


def _dot_tn(a, b):
    return lax.dot_general(a, b, (((0,), (0,)), ((), ())), preferred_element_type=F32)


def _sigmoid(z):
    return 1.0 / (1.0 + jnp.exp(-z))


def _norm_mod(x, gain, sh, sc):
    r = lax.rsqrt(jnp.mean(x * x, axis=-1, keepdims=True) + EPS)
    xn = x * r
    yn = xn * gain
    return r, xn, yn, yn * (1.0 + sc) + sh


def _row_spec(width):
    return pl.BlockSpec((TM, width), lambda i: (i, 0))


def _col_spec(col):
    return pl.BlockSpec((TM, D), lambda i: (i, col))


def _full_spec(shape):
    nd = len(shape)
    return pl.BlockSpec(shape, lambda i: (0,) * nd)


def _mod_spec(n_lat):
    return pl.BlockSpec((1, 8, D), lambda i: (i // n_lat, 0, 0))


def _f32(ref):
    return ref[...].astype(F32)


def proj_fwd(x, gain, mod, m0, w4, n_lat, dtype, name):
    t = x.shape[0]
    nb, _, ns = w4.shape

    def body(x_ref, gain_ref, mod_ref, w_ref, p_ref):
        _, _, _, h = _norm_mod(x_ref[...], gain_ref[...], mod_ref[0, m0:m0 + 1, :], mod_ref[0, m0 + 1:m0 + 2, :])
        hb = h.astype(BF16)
        for c in range(nb):
            p_ref[:, c * ns:(c + 1) * ns] = _dot(hb, w_ref[c]).astype(dtype)

    return pl.pallas_call(
        body, name=name, grid=(t // TM,),
        in_specs=[_row_spec(D), _full_spec((1, D)), _mod_spec(n_lat), _full_spec(w4.shape)],
        out_specs=_row_spec(nb * ns),
        out_shape=jax.ShapeDtypeStruct((t, nb * ns), dtype),
        compiler_params=_cp(1),
    )(x, gain, mod, w4)


def proj_bwd(parts, w4, x, gain, mod, m0, dx_in, n_lat, lat_only, name):
    t = x.shape[0]
    nb, _, ns = w4.shape
    n = nb * ns
    n_parts = len(parts)
    widths = [p.shape[1] for p in parts]
    offs = [sum(widths[:k]) for k in range(n_parts)]
    assert sum(widths) == n
    single = n_parts == 1

    def body(*refs):
        part_refs = refs[:n_parts]
        w_ref, x_ref, gain_ref, mod_ref, dxin_ref = refs[n_parts:n_parts + 5]
        rest = refs[n_parts + 5:]
        if single:
            dx_ref, hb_ref, acc_ref = rest
            src = part_refs[0]
        else:
            dx_ref, hb_ref, acc_ref, dpb_ref = rest
            for p_ref, off, w in zip(part_refs, offs, widths):
                dpb_ref[:, off:off + w] = p_ref[...]
            src = dpb_ref
        i = pl.program_id(0)

        @pl.when(i == 0)
        def _():
            acc_ref[...] = jnp.zeros_like(acc_ref)

        gain = gain_ref[...]
        sc = mod_ref[0, m0 + 1:m0 + 2, :]
        r, xn, yn, h = _norm_mod(x_ref[...], gain, mod_ref[0, m0:m0 + 1, :], sc)
        hb_ref[...] = h.astype(BF16)
        dh = _dot_nt(src[:, 0:ns], w_ref[0])
        for c in range(1, nb):
            dh = dh + _dot_nt(src[:, c * ns:(c + 1) * ns], w_ref[c])
        dsh = jnp.sum(dh, axis=0, keepdims=True)
        dsc = jnp.sum(dh * yn, axis=0, keepdims=True)
        dyn = dh * (1.0 + sc)
        dgain = jnp.sum(dyn * xn, axis=0, keepdims=True)
        dxn = dyn * gain
        dx = dxin_ref[...] + r * (dxn - xn * jnp.mean(dxn * xn, axis=-1, keepdims=True))
        if lat_only:
            @pl.when(i < n_lat)
            def _():
                dx_ref[...] = dx
        else:
            dx_ref[...] = dx
        latf = (i < n_lat).astype(F32)
        ctxf = 1.0 - latf
        acc_ref[0:1, :] += dgain
        acc_ref[1:2, :] += dsh * latf
        acc_ref[2:3, :] += dsc * latf
        acc_ref[3:4, :] += dsh * ctxf
        acc_ref[4:5, :] += dsc * ctxf

    dx_rows = n_lat * TM if lat_only else t
    dx_spec = pl.BlockSpec((TM, D), lambda i: (jnp.minimum(i, n_lat - 1), 0)) if lat_only else _row_spec(D)
    out_specs = [dx_spec, _row_spec(D), _full_spec((8, D))]
    out_shape = [jax.ShapeDtypeStruct((dx_rows, D), F32), jax.ShapeDtypeStruct((t, D), BF16), jax.ShapeDtypeStruct((8, D), F32)]
    if not single:
        out_specs.append(_row_spec(n))
        out_shape.append(jax.ShapeDtypeStruct((t, n), BF16))
    outs = pl.pallas_call(
        body, name=name, grid=(t // TM,),
        in_specs=[_row_spec(w) for w in widths]
        + [_full_spec(w4.shape), _row_spec(D), _full_spec((1, D)), _mod_spec(n_lat), _row_spec(D)],
        out_specs=out_specs, out_shape=out_shape, compiler_params=_cp(1),
    )(*parts, w4, x, gain, mod, dx_in)
    if single:
        return outs[0], outs[1], parts[0], outs[2]
    return outs[0], outs[1], outs[3], outs[2]


def dw_tn(a, b, nb, a_blocked, dep, name):
    t = a.shape[0]
    ka = a.shape[1] // nb if a_blocked else a.shape[1]
    kb = b.shape[1] if a_blocked else b.shape[1] // nb
    n_k = t // TMW

    def body(a_ref, b_ref, _, o_ref, acc):
        k = pl.program_id(1)

        @pl.when(k == 0)
        def _():
            acc[...] = jnp.zeros_like(acc)

        acc[...] += _dot_tn(a_ref[...], b_ref[...])

        @pl.when(k == n_k - 1)
        def _():
            o_ref[0] = acc[...].astype(BF16)

    a_spec = pl.BlockSpec((TMW, ka), (lambda j, k: (k, j)) if a_blocked else (lambda j, k: (k, 0)))
    b_spec = pl.BlockSpec((TMW, kb), (lambda j, k: (k, 0)) if a_blocked else (lambda j, k: (k, j)))
    return pl.pallas_call(
        body, name=name, grid=(nb, n_k),
        in_specs=[a_spec, b_spec, ANY],
        out_specs=pl.BlockSpec((1, ka, kb), lambda j, k: (j, 0, 0)),
        out_shape=jax.ShapeDtypeStruct((nb, ka, kb), BF16),
        scratch_shapes=[pltpu.VMEM((ka, kb), F32)],
        compiler_params=_cp(2),
    )(a, b, dep)


def outproj_fwd(prologue, extras, extra_specs, w, x, mod, m0, n_lat, name):
    t = x.shape[0]
    k = w.shape[0]
    n_extra = len(extras)

    def body(*refs):
        ex = refs[:n_extra]
        w_ref, x_ref, mod_ref, xo_ref, y_ref, ab_ref = refs[n_extra:]
        ab = prologue(pl.program_id(0), *ex).astype(BF16)
        ab_ref[...] = ab
        y = _dot(ab, w_ref[...])
        y_ref[...] = y.astype(BF16)
        xo_ref[...] = x_ref[...] + mod_ref[0, m0 + 2:m0 + 3, :] * y

    return pl.pallas_call(
        body, name=name, grid=(t // TM,),
        in_specs=list(extra_specs) + [_full_spec(w.shape), _row_spec(D), _mod_spec(n_lat)],
        out_specs=[_row_spec(D), _row_spec(D), _row_spec(k)],
        out_shape=[jax.ShapeDtypeStruct((t, D), F32), jax.ShapeDtypeStruct((t, D), BF16), jax.ShapeDtypeStruct((t, k), BF16)],
        compiler_params=_cp(1),
    )(*extras, w, x, mod)


def outproj_bwd(epilogue, extras, extra_specs, ep_out_specs, ep_out_shapes, w, dxn, y, mod, m0, n_lat, dep, name):
    t = dxn.shape[0]
    n_extra = len(extras)

    def body(*refs):
        ex = refs[:n_extra]
        w_ref, dxn_ref, y_ref, mod_ref, _, dyb_ref, acc_ref = refs[n_extra:n_extra + 7]
        ep_outs = refs[n_extra + 7:]
        i = pl.program_id(0)

        @pl.when(i == 0)
        def _():
            acc_ref[...] = jnp.zeros_like(acc_ref)

        dxv = dxn_ref[...]
        dyb = (dxv * mod_ref[0, m0 + 2:m0 + 3, :]).astype(BF16)
        dyb_ref[...] = dyb
        dg = jnp.sum(dxv * _f32(y_ref), axis=0, keepdims=True)
        latf = (i < n_lat).astype(F32)
        acc_ref[0:1, :] += dg * latf
        acc_ref[1:2, :] += dg * (1.0 - latf)
        epilogue(i, _dot_nt(dyb, w_ref[...]), ex, ep_outs, acc_ref)

    outs = pl.pallas_call(
        body, name=name, grid=(t // TM,),
        in_specs=list(extra_specs) + [_full_spec(w.shape), _row_spec(D), _row_spec(D), _mod_spec(n_lat), ANY],
        out_specs=[_row_spec(D), _full_spec((8, D))] + list(ep_out_specs),
        out_shape=[jax.ShapeDtypeStruct((t, D), BF16), jax.ShapeDtypeStruct((8, D), F32)] + list(ep_out_shapes),
        compiler_params=_cp(1),
    )(*extras, w, dxn, y, mod, dep)
    return outs[0], outs[1], outs[2:]


def mlp_fwd(x, gain, mod, w1, w2, n_lat, name):
    t = x.shape[0]
    nb, _, ns = w1.shape

    def body(x_ref, gain_ref, mod_ref, w1_ref, w2_ref, xo_ref, y_ref, ab_ref):
        x = x_ref[...]
        _, _, _, h = _norm_mod(x, gain_ref[...], mod_ref[0, 3:4, :], mod_ref[0, 4:5, :])
        hb = h.astype(BF16)
        y = None
        for c in range(nb):
            a = jnp.square(jnp.maximum(_dot(hb, w1_ref[c]), 0.0)).astype(BF16)
            ab_ref[:, c * ns:(c + 1) * ns] = a
            yc = _dot(a, w2_ref[c * ns:(c + 1) * ns, :])
            y = yc if y is None else y + yc
        y_ref[...] = y.astype(BF16)
        xo_ref[...] = x + mod_ref[0, 5:6, :] * y

    return pl.pallas_call(
        body, name=name, grid=(t // TM,),
        in_specs=[_row_spec(D), _full_spec((1, D)), _mod_spec(n_lat), _full_spec(w1.shape), _full_spec(w2.shape)],
        out_specs=[_row_spec(D), _row_spec(D), _row_spec(nb * ns)],
        out_shape=[jax.ShapeDtypeStruct((t, D), F32), jax.ShapeDtypeStruct((t, D), BF16), jax.ShapeDtypeStruct((t, nb * ns), BF16)],
        compiler_params=_cp(1),
    )(x, gain, mod, w1, w2)


def mlp_bwd(dxn, y, ab, x, gain, mod, w1, w2, n_lat, dep, name):
    t = x.shape[0]
    nb, _, ns = w1.shape

    def body(dxn_ref, y_ref, ab_ref, x_ref, gain_ref, mod_ref, w1_ref, w2_ref, _, dx_ref, dyb_ref, dp_ref, hb_ref, acc_ref):
        i = pl.program_id(0)

        @pl.when(i == 0)
        def _():
            acc_ref[...] = jnp.zeros_like(acc_ref)

        dxv = dxn_ref[...]
        dyb = (dxv * mod_ref[0, 5:6, :]).astype(BF16)
        dyb_ref[...] = dyb
        dg = jnp.sum(dxv * _f32(y_ref), axis=0, keepdims=True)
        gain = gain_ref[...]
        sc = mod_ref[0, 4:5, :]
        r, xn, yn, h = _norm_mod(x_ref[...], gain, mod_ref[0, 3:4, :], sc)
        hb_ref[...] = h.astype(BF16)
        dh = None
        for c in range(nb):
            cols = slice(c * ns, (c + 1) * ns)
            da = _dot_nt(dyb, w2_ref[cols, :])
            dp = (da * (2.0 * jnp.sqrt(ab_ref[:, cols].astype(F32)))).astype(BF16)
            dp_ref[:, cols] = dp
            d = _dot_nt(dp, w1_ref[c])
            dh = d if dh is None else dh + d
        dsh = jnp.sum(dh, axis=0, keepdims=True)
        dsc = jnp.sum(dh * yn, axis=0, keepdims=True)
        dyn = dh * (1.0 + sc)
        dgain = jnp.sum(dyn * xn, axis=0, keepdims=True)
        dxn_ = dyn * gain
        dx_ref[...] = dxv + r * (dxn_ - xn * jnp.mean(dxn_ * xn, axis=-1, keepdims=True))
        latf = (i < n_lat).astype(F32)
        ctxf = 1.0 - latf
        acc_ref[0:1, :] += dgain
        acc_ref[1:2, :] += dsh * latf
        acc_ref[2:3, :] += dsc * latf
        acc_ref[3:4, :] += dsh * ctxf
        acc_ref[4:5, :] += dsc * ctxf
        acc_ref[5:6, :] += dg * latf
        acc_ref[6:7, :] += dg * ctxf

    return pl.pallas_call(
        body, name=name, grid=(t // TM,),
        in_specs=[_row_spec(D), _row_spec(D), _row_spec(nb * ns), _row_spec(D), _full_spec((1, D)), _mod_spec(n_lat),
                  _full_spec(w1.shape), _full_spec(w2.shape), ANY],
        out_specs=[_row_spec(D), _row_spec(D), _row_spec(nb * ns), _row_spec(D), _full_spec((8, D))],
        out_shape=[jax.ShapeDtypeStruct((t, D), F32), jax.ShapeDtypeStruct((t, D), BF16), jax.ShapeDtypeStruct((t, nb * ns), BF16),
                   jax.ShapeDtypeStruct((t, D), BF16), jax.ShapeDtypeStruct((8, D), F32)],
        compiler_params=_cp(1),
    )(dxn, y, ab, x, gain, mod, w1, w2, dep)


def readout_prologue(i, o0_ref, o1_ref, gate_ref, gn_ref):
    o = _f32(o0_ref) + _f32(o1_ref)
    gate = gate_ref[...]
    w = gn_ref[...] * (gate * _sigmoid(gate))
    pieces = []
    for h in range(NH):
        sl = slice(h * HD, (h + 1) * HD)
        oh = o[:, sl]
        pieces.append(oh * lax.rsqrt(jnp.mean(oh * oh, axis=-1, keepdims=True) + EPS) * w[:, sl])
    return jnp.concatenate(pieces, axis=1)


def readout_epilogue(i, da, ex, outs, acc_ref):
    o0_ref, o1_ref, gate_ref, gn_ref = ex
    do_ref, dgate_ref = outs
    o = _f32(o0_ref) + _f32(o1_ref)
    gate = gate_ref[...]
    gn = gn_ref[...]
    sg = _sigmoid(gate)
    silu = gate * sg
    dsilu = sg * (1.0 + gate * (1.0 - sg))
    for h in range(NH):
        sl = slice(h * HD, (h + 1) * HD)
        oh = o[:, sl]
        r = lax.rsqrt(jnp.mean(oh * oh, axis=-1, keepdims=True) + EPS)
        nh = oh * r
        dah = da[:, sl]
        acc_ref[2:3, sl] += jnp.sum(dah * nh * silu[:, sl], axis=0, keepdims=True)
        dgate_ref[:, sl] = (dah * nh * gn[:, sl] * dsilu[:, sl]).astype(BF16)
        dn = dah * gn[:, sl] * silu[:, sl]
        do_ref[:, sl] = r * (dn - nh * jnp.mean(dn * nh, axis=-1, keepdims=True))


def _seg_masks(i, n_lat):
    rows = lax.broadcasted_iota(jnp.int32, (TM, 1), 0)
    latf = (i < n_lat).astype(F32)
    ctxf = 1.0 - latf
    prev_ok = (rows % CH != 0).astype(F32) * latf + (rows != 0).astype(F32) * ctxf
    next_ok = (rows % CH != CH - 1).astype(F32) * latf + (rows != TM - 1).astype(F32) * ctxf
    return prev_ok, next_ok


def _shifts(i, n_lat, sft, cur, halo_prev, halo_next):
    if sft == 1:
        prev_ok, next_ok = _seg_masks(i, n_lat)
        return pltpu.roll(cur, 1, 0) * prev_ok, pltpu.roll(cur, TM - 1, 0) * next_ok
    has_prev = jnp.logical_and(i > 0, i < n_lat).astype(F32)
    has_next = (i < n_lat - 1).astype(F32)
    prev = jnp.concatenate([halo_prev * has_prev, cur[:TM - CH]], axis=0)
    nxt = jnp.concatenate([cur[CH:], halo_next * has_next], axis=0)
    return prev, nxt


def _conv_u(sft, ex):
    if sft == 1:
        gb_ref, gc_ref, xi_ref, cw_ref, cb_ref = ex
        return gb_ref, _f32(gc_ref) * _f32(xi_ref), None, None, cw_ref, cb_ref
    gb_ref, gc_ref, xi_ref, gcp_ref, xip_ref, gcn_ref, xin_ref, cw_ref, cb_ref = ex
    return gb_ref, _f32(gc_ref) * _f32(xi_ref), _f32(gcp_ref) * _f32(xip_ref), _f32(gcn_ref) * _f32(xin_ref), cw_ref, cb_ref


def _conv_value(i, n_lat, sft, ex):
    gb_ref, u, up, un, cw_ref, cb_ref = _conv_u(sft, ex)
    u_prev, u_next = _shifts(i, n_lat, sft, u, up, un)
    return gb_ref, cb_ref[...] + cw_ref[0:1, :] * u_prev + cw_ref[1:2, :] * u + cw_ref[2:3, :] * u_next


def make_conv_prologue(n_lat, sft):
    def prologue(i, *ex):
        gb_ref, conv = _conv_value(i, n_lat, sft, ex)
        return _f32(gb_ref) * conv
    return prologue


def make_conv_epilogue(n_lat, sft):
    def epilogue(i, da, ex, outs, acc_ref):
        gb_ref, conv = _conv_value(i, n_lat, sft, ex)
        outs[0][...] = da * _f32(gb_ref)
        outs[1][...] = (da * conv).astype(BF16)
    return epilogue


def _conv_specs(sft, t):
    specs = [_col_spec(0), _col_spec(1), _col_spec(2)]
    if sft != 1:
        per = TM // CH
        last = t // CH - 1
        for fn in (lambda i: jnp.maximum(i * per - 1, 0), lambda i: jnp.minimum(i * per + per, last)):
            for col in (1, 2):
                specs.append(pl.BlockSpec((CH, D), functools.partial(lambda i, f, c: (f(i), c), f=fn, c=col)))
    return specs + [_full_spec((8, D)), _full_spec((1, D))]


def _conv_args(sft, p, cw8, cb):
    return [p] * (3 if sft == 1 else 7) + [cw8, cb]


def conv_bwd(dconv, p, cw8, sft, n_lat, name):
    t = dconv.shape[0]
    halo = sft != 1

    def body(*refs):
        if halo:
            dc_ref, dcp_ref, dcn_ref, gc_ref, xi_ref, gcp_ref, xip_ref, gcn_ref, xin_ref, cw_ref, dgc_ref, dxi_ref, acc_ref = refs
            up, un = _f32(gcp_ref) * _f32(xip_ref), _f32(gcn_ref) * _f32(xin_ref)
            dcp, dcn = dcp_ref[...], dcn_ref[...]
        else:
            dc_ref, gc_ref, xi_ref, cw_ref, dgc_ref, dxi_ref, acc_ref = refs
            up = un = dcp = dcn = None
        i = pl.program_id(0)

        @pl.when(i == 0)
        def _():
            acc_ref[...] = jnp.zeros_like(acc_ref)

        gc = _f32(gc_ref)
        xi = _f32(xi_ref)
        u = gc * xi
        dc = dc_ref[...]
        u_prev, u_next = _shifts(i, n_lat, sft, u, up, un)
        dc_prev, dc_next = _shifts(i, n_lat, sft, dc, dcp, dcn)
        acc_ref[0:1, :] += jnp.sum(dc * u_prev, axis=0, keepdims=True)
        acc_ref[1:2, :] += jnp.sum(dc * u, axis=0, keepdims=True)
        acc_ref[2:3, :] += jnp.sum(dc * u_next, axis=0, keepdims=True)
        acc_ref[3:4, :] += jnp.sum(dc, axis=0, keepdims=True)
        du = cw_ref[0:1, :] * dc_next + cw_ref[1:2, :] * dc + cw_ref[2:3, :] * dc_prev
        dgc_ref[...] = (du * xi).astype(BF16)
        dxi_ref[...] = (du * gc).astype(BF16)

    per = TM // CH
    last = t // CH - 1
    prev_i = lambda i: jnp.maximum(i * per - 1, 0)
    next_i = lambda i: jnp.minimum(i * per + per, last)
    if halo:
        in_specs = [_row_spec(D), pl.BlockSpec((CH, D), lambda i: (prev_i(i), 0)), pl.BlockSpec((CH, D), lambda i: (next_i(i), 0)),
                    _col_spec(1), _col_spec(2),
                    pl.BlockSpec((CH, D), lambda i: (prev_i(i), 1)), pl.BlockSpec((CH, D), lambda i: (prev_i(i), 2)),
                    pl.BlockSpec((CH, D), lambda i: (next_i(i), 1)), pl.BlockSpec((CH, D), lambda i: (next_i(i), 2)),
                    _full_spec((8, D))]
        args = [dconv, dconv, dconv, p, p, p, p, p, p, cw8]
    else:
        in_specs = [_row_spec(D), _col_spec(1), _col_spec(2), _full_spec((8, D))]
        args = [dconv, p, p, cw8]
    return pl.pallas_call(
        body, name=name, grid=(t // TM,), in_specs=in_specs,
        out_specs=[_row_spec(D), _row_spec(D), _full_spec((8, D))],
        out_shape=[jax.ShapeDtypeStruct((t, D), BF16), jax.ShapeDtypeStruct((t, D), BF16), jax.ShapeDtypeStruct((8, D), F32)],
        compiler_params=_cp(1),
    )(*args)


LOG2E = 1.4426950408889634


def _cumsum_matrix(reverse):
    r = lax.broadcasted_iota(jnp.int32, (CH, CH), 0)
    c = lax.broadcasted_iota(jnp.int32, (CH, CH), 1)
    return (r <= c if reverse else r >= c).astype(BF16)


def _chunk_cumsum(g, tri):
    hi = g.astype(BF16)
    lo = (g - hi.astype(F32)).astype(BF16)
    return _dot(tri, hi) + _dot(tri, lo)


def _gate_values(z, lb):
    sig = _sigmoid(z)
    f = lb + (1.0 - lb) * sig
    return sig, f


def _tri(direction, transposed):
    r = lax.broadcasted_iota(jnp.int32, (CH, CH), 0)
    c = lax.broadcasted_iota(jnp.int32, (CH, CH), 1)
    lower = (direction == 0) != transposed
    return r >= c if lower else r <= c


def _gla_rows(direction):
    return (CH // 2 - 1, CH - 1) if direction == 0 else (CH // 2, 0)


def _fwd_tile(direction, nt):
    return (lambda i: (i + nt - 1) % nt) if direction == 0 else (lambda i: nt - 1 - i)


def gla_fwd(p, lb2, name):
    t = p.shape[0]
    nt = t // TM
    per = TM // CH

    def body(z0_ref, v0_ref, q0_ref, z1_ref, v1_ref, q1_ref, lb_ref, o0_ref, s0_ref, o1_ref, s1_ref, st, q_s, k_s, c_s):
        @pl.when(pl.program_id(0) == 0)
        def _():
            st[...] = jnp.zeros_like(st)

        ins = ((z0_ref, v0_ref, q0_ref, o0_ref, s0_ref), (z1_ref, v1_ref, q1_ref, o1_ref, s1_ref))
        for d in range(2):
            z_ref, _, qr_ref, _, _ = ins[d]
            tri = _cumsum_matrix(d == 1)
            lb = lb_ref[d:d + 1, :]
            for ci in range(per):
                rows = slice(ci * CH, (ci + 1) * CH)
                _, f = _gate_values(z_ref[rows, :], lb)
                k_s[d, rows, :] = 1.0 - f
                c_s[d, rows, :] = _chunk_cumsum(jnp.log(f) * LOG2E, tri)
                qr = qr_ref[rows, :]
                q_s[d, rows, :] = qr * _sigmoid(qr)
        masks = (_tri(0, False), _tri(1, False))
        state = [[st[d, h] for h in range(NH)] for d in range(2)]
        for it in range(per):
            chunk = []
            for d in range(2):
                ref_row, last_row = _gla_rows(d)
                ci = it if d == 0 else per - 1 - it
                r0 = ci * CH
                rows = slice(r0, r0 + CH)
                cum = c_s[d, rows, :]
                ref = c_s[d, r0 + ref_row:r0 + ref_row + 1, :]
                last = c_s[d, r0 + last_row:r0 + last_row + 1, :]
                q = q_s[d, rows, :]
                k = k_s[d, rows, :]
                chunk.append(dict(
                    ci=ci, rows=rows, qh=(q * jnp.exp2(cum)).astype(BF16), qt=(q * jnp.exp2(cum - ref)).astype(BF16),
                    kt=(k * jnp.exp2(ref - cum)).astype(BF16), kb=(k * jnp.exp2(last - cum)).astype(BF16),
                    el=jnp.exp2(last), vb=ins[d][1][rows, :].astype(BF16)))
            for h in range(NH):
                sl = slice(h * HD, (h + 1) * HD)
                for d in range(2):
                    c = chunk[d]
                    o_ref, s_ref = ins[d][3], ins[d][4]
                    s_t = state[d][h]
                    s_ref[c["ci"], h] = s_t
                    sc = jnp.where(masks[d], _dot_nt(c["qt"][:, sl], c["kt"][:, sl]), 0.0)
                    o_ref[c["rows"], sl] = (_dot_nt(c["qh"][:, sl], s_t.astype(BF16))
                                            + _dot(sc.astype(BF16), c["vb"][:, sl])).astype(BF16)
                    state[d][h] = s_t * c["el"][:, sl] + _dot_tn(c["vb"][:, sl], c["kb"][:, sl])
        for d in range(2):
            for h in range(NH):
                st[d, h] = state[d][h]

    tiles = (_fwd_tile(0, nt), _fwd_tile(1, nt))
    tspec = lambda d, col: pl.BlockSpec((TM, D), lambda i: (tiles[d](i), col))
    sspec = lambda d: pl.BlockSpec((per, NH, HD, HD), lambda i: (tiles[d](i), 0, 0, 0))
    o_shape = jax.ShapeDtypeStruct((t, D), BF16)
    s_shape = jax.ShapeDtypeStruct((t // CH, NH, HD, HD), F32)
    return pl.pallas_call(
        body, name=name, grid=(nt,),
        in_specs=[tspec(0, 0), tspec(0, 2), tspec(0, 3), tspec(1, 1), tspec(1, 2), tspec(1, 3), _full_spec((2, D))],
        out_specs=[tspec(0, 0), sspec(0), tspec(1, 0), sspec(1)],
        out_shape=[o_shape, s_shape, o_shape, s_shape],
        scratch_shapes=[pltpu.VMEM((2, NH, HD, HD), F32)] + [pltpu.VMEM((2, TM, D), F32)] * 3,
        compiler_params=_cp(1),
    )(p, p, p, p, p, p, lb2)


def gla_bwd(p, lb2, do, states, direction, prev, name):
    t = p.shape[0]
    nt = t // TM
    per = TM // CH
    ref_row, last_row = _gla_rows(direction)
    tile = (lambda i: (2 * nt - 2 - i) % nt) if direction == 0 else (lambda i: i)
    final = prev is not None
    n_in = 8 if final else 6

    def body(*refs):
        z_ref, v_ref, qr_ref, lb_ref, do_ref, s_ref = refs[:6]
        dz_ref, dv_ref, dq_ref, acc_ref, dst, q_s, k_s, c_s, dq_s, dk_s, dl_s = refs[n_in:]

        @pl.when(pl.program_id(0) == 0)
        def _():
            dst[...] = jnp.zeros_like(dst)
            acc_ref[...] = jnp.zeros_like(acc_ref)

        mask = _tri(direction, False)
        mask_t = _tri(direction, True)
        tri = _cumsum_matrix(direction == 1)
        tri_t = _cumsum_matrix(direction == 0)
        is_last = lax.broadcasted_iota(jnp.int32, (CH, 1), 0) == last_row
        lb = lb_ref[direction:direction + 1, :]
        for ci in range(per):
            rows = slice(ci * CH, (ci + 1) * CH)
            _, f = _gate_values(z_ref[rows, :], lb)
            k_s[rows, :] = 1.0 - f
            c_s[rows, :] = _chunk_cumsum(jnp.log(f) * LOG2E, tri)
            qr = qr_ref[rows, :]
            q_s[rows, :] = qr * _sigmoid(qr)
        for it in range(per):
            ci = per - 1 - it if direction == 0 else it
            r0 = ci * CH
            rows = slice(r0, r0 + CH)
            cum = c_s[rows, :]
            ref = c_s[r0 + ref_row:r0 + ref_row + 1, :]
            last = c_s[r0 + last_row:r0 + last_row + 1, :]
            q = q_s[rows, :]
            k = k_s[rows, :]
            e_h = jnp.exp2(cum)
            e_t = jnp.exp2(cum - ref)
            e_kt = jnp.exp2(ref - cum)
            e_kb = jnp.exp2(last - cum)
            el = jnp.exp2(last)
            qh = (q * e_h).astype(BF16)
            qt = (q * e_t).astype(BF16)
            kt = (k * e_kt).astype(BF16)
            kbf = k * e_kb
            kb = kbf.astype(BF16)
            vb = v_ref[rows, :].astype(BF16)
            dob = do_ref[rows, :].astype(BF16)
            for h in range(NH):
                sl = slice(h * HD, (h + 1) * HD)
                s_t = s_ref[ci, h]
                ds_t = dst[h]
                ds_b = ds_t.astype(BF16)
                d_a = jnp.where(mask, _dot_nt(dob[:, sl], vb[:, sl]), 0.0).astype(BF16)
                a_t = jnp.where(mask_t, _dot_nt(kt[:, sl], qt[:, sl]), 0.0).astype(BF16)
                d_at = jnp.where(mask_t, _dot_nt(vb[:, sl], dob[:, sl]), 0.0).astype(BF16)
                dv = _dot(a_t, dob[:, sl]) + _dot_nt(kb[:, sl], ds_b)
                dkb = _dot(vb[:, sl], ds_b)
                dl_s[it:it + 1, sl] = (el[:, sl] * jnp.sum(ds_t * s_t, axis=0, keepdims=True)
                                       + jnp.sum(dkb * kbf[:, sl], axis=0, keepdims=True))
                dst[h] = ds_t * el[:, sl] + _dot_tn(dob[:, sl], qh[:, sl])
                dq_s[rows, sl] = _dot(dob[:, sl], s_t.astype(BF16)) * e_h[:, sl] + _dot(d_a, kt[:, sl]) * e_t[:, sl]
                dk_s[rows, sl] = _dot(d_at, qt[:, sl]) * e_kt[:, sl] + dkb * e_kb[:, sl]
                if final:
                    dv_ref[rows, sl] = (refs[6][rows, sl] + dv).astype(BF16)
                else:
                    dv_ref[rows, sl] = dv
        for it in range(per):
            ci = per - 1 - it if direction == 0 else it
            rows = slice(ci * CH, (ci + 1) * CH)
            dq = dq_s[rows, :]
            dk = dk_s[rows, :]
            dg = _chunk_cumsum(dq * q_s[rows, :] - dk * k_s[rows, :] + jnp.where(is_last, dl_s[it:it + 1, :], 0.0), tri_t)
            sig, f = _gate_values(z_ref[rows, :], lb)
            df = dg / f - dk
            acc_ref[0:1, :] += jnp.sum(df * (1.0 - sig), axis=0, keepdims=True)
            dz_ref[rows, :] = (df * (1.0 - lb) * sig * (1.0 - sig)).astype(BF16)
            if final:
                qr = qr_ref[rows, :]
                sq = _sigmoid(qr)
                dq_ref[rows, :] = ((refs[7][rows, :] + dq) * (sq * (1.0 + qr * (1.0 - sq)))).astype(BF16)
            else:
                dq_ref[rows, :] = dq

    tspec = lambda col: pl.BlockSpec((TM, D), lambda i: (tile(i), col))
    sspec = pl.BlockSpec((per, NH, HD, HD), lambda i: (tile(i), 0, 0, 0))
    in_specs = [tspec(direction), tspec(2), tspec(3), _full_spec((2, D)), tspec(0), sspec]
    args = [p, p, p, lb2, do, states]
    if final:
        in_specs += [tspec(0), tspec(0)]
        args += list(prev)
    odt = BF16 if final else F32
    return pl.pallas_call(
        body, name=name, grid=(nt,), in_specs=in_specs,
        out_specs=[tspec(0), tspec(0), tspec(0), _full_spec((8, D))],
        out_shape=[jax.ShapeDtypeStruct((t, D), BF16), jax.ShapeDtypeStruct((t, D), odt), jax.ShapeDtypeStruct((t, D), odt),
                   jax.ShapeDtypeStruct((8, D), F32)],
        scratch_shapes=[pltpu.VMEM((NH, HD, HD), F32)] + [pltpu.VMEM((TM, D), F32)] * 5 + [pltpu.VMEM((8, D), F32)],
        compiler_params=_cp(1),
    )(*args)


def loss_bwd(x, gain, target, n_lat, name):
    t = x.shape[0]

    def body(x_ref, gain_ref, tg_ref, dx_ref, acc_ref):
        i = pl.program_id(0)

        @pl.when(i == 0)
        def _():
            acc_ref[...] = jnp.zeros_like(acc_ref)

        latf = (i < n_lat).astype(F32)
        x = x_ref[...]
        gain = gain_ref[...]
        r = lax.rsqrt(jnp.mean(x * x, axis=-1, keepdims=True) + EPS)
        xn = x * r
        err = (xn * gain - tg_ref[...]) * latf
        dy = err * (1.0 / D)
        dxn = dy * gain
        dx_ref[...] = r * (dxn - xn * jnp.mean(dxn * xn, axis=-1, keepdims=True))
        acc_ref[0:1, :] += jnp.sum(dy * xn, axis=0, keepdims=True)
        acc_ref[1:2, :] += jnp.sum(err * err, axis=0, keepdims=True)

    return pl.pallas_call(
        body, name=name, grid=(t // TM,),
        in_specs=[_row_spec(D), _full_spec((1, D)), pl.BlockSpec((TM, D), lambda i: (jnp.minimum(i, n_lat - 1), 0))],
        out_specs=[_row_spec(D), _full_spec((8, D))],
        out_shape=[jax.ShapeDtypeStruct((t, D), F32), jax.ShapeDtypeStruct((8, D), F32)],
        compiler_params=_cp(1),
    )(x, gain, target)


def local_step(xs, target, mods, norm1, norm2, norm_f, lbs, gnorm, cw8, cb, wts, n_lat, on_grads, after_backward):
    t = xs.shape[0]
    saved = []
    cache = {}

    def W(name, idx, after=None):
        if (name, idx) not in cache:
            cache[(name, idx)] = wts(name, idx, after)
        return cache[(name, idx)]

    x = xs
    for i in range(DEPTH):
        j = i // 2
        rec = i % 2 == 0
        n1 = norm1[i:i + 1]
        n2 = norm2[i:i + 1]
        s = {"x_in": x}
        if rec:
            p = proj_fwd(x, n1, mods[i], 0, W("hin", j, x), n_lat, F32, f"hin_fwd_{i}")
            o0, st0, o1, st1 = gla_fwd(p, lbs[j], f"gla_fwd_{i}")
            ex = [o0, o1, p, gnorm[j:j + 1]]
            ex_specs = [_row_spec(D), _row_spec(D), _col_spec(4), _full_spec((1, D))]
            xm, y, ab = outproj_fwd(readout_prologue, ex, ex_specs, W("hout", j, o1), x, mods[i], 0, n_lat, f"hout_fwd_{i}")
            s.update(st0=st0, st1=st1)
        else:
            sft = 1 if j % 2 == 0 else CH
            p = proj_fwd(x, n1, mods[i], 0, W("cin", j, x), n_lat, BF16, f"cin_fwd_{i}")
            ex = _conv_args(sft, p, cw8[j], cb[j])
            ex_specs = _conv_specs(sft, t)
            xm, y, ab = outproj_fwd(make_conv_prologue(n_lat, sft), ex, ex_specs, W("cout", j, p), x, mods[i], 0, n_lat, f"cout_fwd_{i}")
        s.update(p=p, ex=ex, ex_specs=ex_specs, y_mix=y, ab_mix=ab, x_mid=xm)
        x, y2, ab2 = mlp_fwd(xm, n2, mods[i], W("w1", i, xm), W("w2", i, xm), n_lat, f"mlp_fwd_{i}")
        s.update(y_mlp=y2, ab_mlp=ab2)
        saved.append(s)

    dx, acc_loss = loss_bwd(x, norm_f, target, n_lat, "loss")
    small = {"norm_f": acc_loss[0:1], "norm1": [None] * DEPTH, "norm2": [None] * DEPTH, "dmod": [None] * DEPTH,
             "gnorm": [None] * 2, "lb": [None] * 2, "cw": [None] * 2, "cb": [None] * 2}
    bshape = lambda w: jax.ShapeDtypeStruct((t, w), BF16)
    token = jnp.zeros((8, 128), F32)
    for i in reversed(range(DEPTH)):
        j = i // 2
        rec = i % 2 == 0
        s = saved[i]
        n1 = norm1[i:i + 1]
        n2 = norm2[i:i + 1]
        dx, dyb, dp1, hb, acc_n2 = mlp_bwd(dx, s["y_mlp"], s["ab_mlp"], s["x_mid"], n2, mods[i], W("w1", i), W("w2", i), n_lat, token,
                                           f"mlp_bwd_{i}")
        token = on_grads(i, "mlp", {"w2": dw_tn(s["ab_mlp"], dyb, 4, True, token, f"w2_dw_{i}"),
                                    "w1": dw_tn(hb, dp1, 4, False, token, f"w1_dw_{i}")})
        if rec:
            dyb, acc_g1, (do, dgate) = outproj_bwd(
                readout_epilogue, s["ex"], s["ex_specs"], [_row_spec(D), _row_spec(D)],
                [jax.ShapeDtypeStruct((t, D), F32), bshape(D)], W("hout", j), dx, s["y_mix"], mods[i], 0, n_lat, token, f"hout_bwd_{i}")
            dz0, dv0, dq0, acc_l0 = gla_bwd(s["p"], lbs[j], do, s["st0"], 0, None, f"gla_bwd0_{i}")
            dz1, dv, dq, acc_l1 = gla_bwd(s["p"], lbs[j], do, s["st1"], 1, (dv0, dq0), f"gla_bwd1_{i}")
            dx, hb, dpb, acc_n1 = proj_bwd([dz0, dz1, dv, dq, dgate], W("hin", j), s["x_in"], n1, mods[i], 0, dx, n_lat, i == 0,
                                           f"hin_bwd_{i}")
            small["gnorm"][j] = acc_g1[2:3]
            small["lb"][j] = jnp.concatenate([acc_l0[0:1], acc_l1[0:1]], axis=0)
            mix = ("hout", "hin")
        else:
            sft = 1 if j % 2 == 0 else CH
            dyb, acc_g1, (dconv, dgb) = outproj_bwd(
                make_conv_epilogue(n_lat, sft), s["ex"], s["ex_specs"], [_row_spec(D), _row_spec(D)],
                [jax.ShapeDtypeStruct((t, D), F32), bshape(D)], W("cout", j), dx, s["y_mix"], mods[i], 0, n_lat, token, f"cout_bwd_{i}")
            dgc, dxi, acc_c = conv_bwd(dconv, s["p"], cw8[j], sft, n_lat, f"conv_bwd_{i}")
            dx, hb, dpb, acc_n1 = proj_bwd([dgb, dgc, dxi], W("cin", j), s["x_in"], n1, mods[i], 0, dx, n_lat, False, f"cin_bwd_{i}")
            small["cw"][j] = acc_c[0:3]
            small["cb"][j] = acc_c[3:4]
            mix = ("cout", "cin")
        small["norm1"][i] = acc_n1[0:1]
        small["norm2"][i] = acc_n2[0:1]
        z2 = jnp.zeros((2, D), F32)
        small["dmod"][i] = jnp.concatenate([acc_n1[1:3], acc_g1[0:1], acc_n2[1:3], acc_n2[5:6], z2,
                                            acc_n1[3:5], acc_g1[1:2], acc_n2[3:5], acc_n2[6:7], z2], axis=0)
        if i == 0:
            token = after_backward(small, token)
        token = on_grads(i, "mix", {mix[0]: dw_tn(s["ab_mix"], dyb, 1, False, token, f"{mix[0]}_dw_{i}"),
                                    mix[1]: dw_tn(hb, dpb, 4, False, token, f"{mix[1]}_dw_{i}")})
    return acc_loss[1:2], dx, token


RB = 256


def cast_to_slot(w2d, layer, k, chip1, name):
    c = w2d.shape[1]
    nblk = k // RB

    def body(chip_ref, w_ref, o_ref):
        o_ref[0] = w_ref[...].astype(BF16)

    return pl.pallas_call(
        body, name=name,
        grid_spec=pltpu.PrefetchScalarGridSpec(
            num_scalar_prefetch=1, grid=(nblk,),
            in_specs=[pl.BlockSpec((RB, c), lambda i, ch: (layer * nblk + i, 0))],
            out_specs=pl.BlockSpec((1, RB, c), lambda i, ch: (ch[0], i, 0))),
        out_shape=jax.ShapeDtypeStruct((4, k, c), BF16), compiler_params=_cp(1))(chip1, w2d)


def sum_slots(own, land, acc, layer, chip1, name):
    _, k, c = own.shape
    nblk = k // RB

    def body(chip_ref, own_ref, l1_ref, l2_ref, l3_ref, acc_ref, o_ref):
        o_ref[...] = ((own_ref[0].astype(F32) + l1_ref[0].astype(F32)) + l2_ref[0].astype(F32)) + l3_ref[0].astype(F32)

    slot = lambda d: pl.BlockSpec((1, RB, c), lambda i, ch: ((ch[0] + d) % 4, i, 0))
    return pl.pallas_call(
        body, name=name,
        grid_spec=pltpu.PrefetchScalarGridSpec(
            num_scalar_prefetch=1, grid=(nblk,),
            in_specs=[slot(0), slot(1), slot(2), slot(3), ANY],
            out_specs=pl.BlockSpec((RB, c), lambda i, ch: (layer * nblk + i, 0))),
        out_shape=jax.ShapeDtypeStruct(acc.shape, F32), input_output_aliases={5: 0}, compiler_params=_cp(1),
    )(chip1, own, land, land, land, acc)


def _adamw_math(w, g, m, v):
    m = ADAM_B1 * m + (1.0 - ADAM_B1) * g
    v = ADAM_B2 * v + (1.0 - ADAM_B2) * jnp.square(g)
    m_hat = m / (1.0 - ADAM_B1 ** ADAM_STEP)
    v_hat = v / (1.0 - ADAM_B2 ** ADAM_STEP)
    delta = -ADAM_LR * (m_hat / (jnp.sqrt(v_hat) + ADAM_EPS) + ADAM_WD * w)
    return delta, m, v


def adamw(gsrcs, w, m, v, name):
    r, c = w.shape
    rb = RB if r % RB == 0 else r
    n_g = len(gsrcs)

    def body(*refs):
        g = refs[0][...]
        for k in range(1, n_g):
            g = g + refs[k][...]
        w_ref, m_ref, v_ref, g_ref, d_ref, mo_ref, vo_ref = refs[n_g:]
        delta, mo, vo = _adamw_math(w_ref[...], g, m_ref[...], v_ref[...])
        g_ref[...] = g
        d_ref[...] = delta
        mo_ref[...] = mo
        vo_ref[...] = vo

    spec = pl.BlockSpec((rb, c), lambda i: (i, 0))
    shp = jax.ShapeDtypeStruct((r, c), F32)
    return pl.pallas_call(body, name=name, grid=(r // rb,), in_specs=[spec] * (n_g + 3), out_specs=[spec] * 4,
                          out_shape=[shp] * 4, compiler_params=_cp(1))(*gsrcs, w, m, v)


ADA_CB = 512


def ada_fwd(cvec, ada_w, bias, name):
    _, _, nc = ada_w.shape

    def body(c_ref, w_ref, b_ref, o_ref):
        cv = c_ref[...]
        a = (cv * _sigmoid(cv)).astype(BF16)
        o_ref[0] = _dot(a, w_ref[0].astype(BF16)) + b_ref[0]

    return pl.pallas_call(
        body, name=name, grid=(DEPTH, nc // ADA_CB),
        in_specs=[pl.BlockSpec((16, D), lambda i, j: (0, 0)), pl.BlockSpec((1, D, ADA_CB), lambda i, j: (i, 0, j)),
                  pl.BlockSpec((1, 1, ADA_CB), lambda i, j: (i, 0, j))],
        out_specs=pl.BlockSpec((1, 16, ADA_CB), lambda i, j: (i, 0, j)),
        out_shape=jax.ShapeDtypeStruct((DEPTH, 16, nc), F32), compiler_params=_cp(2),
    )(cvec, ada_w, bias)


def ada_bwd(cvec, dcols, ada_w, m, v, name):
    _, _, nc = ada_w.shape

    def body(c_ref, d_ref, w_ref, m_ref, v_ref, g_ref, dl_ref, mo_ref, vo_ref, acc_ref):
        @pl.when(jnp.logical_and(pl.program_id(0) == 0, pl.program_id(1) == 0))
        def _():
            acc_ref[...] = jnp.zeros_like(acc_ref)

        cv = c_ref[...]
        a = (cv * _sigmoid(cv)).astype(BF16)
        db = d_ref[0].astype(BF16)
        w = w_ref[0]
        g = _dot_tn(a, db)
        delta, mo, vo = _adamw_math(w, g, m_ref[0], v_ref[0])
        g_ref[0] = g
        dl_ref[0] = delta
        mo_ref[0] = mo
        vo_ref[0] = vo
        acc_ref[...] += _dot_nt(db[8:16, :], w.astype(BF16))

    wspec = pl.BlockSpec((1, D, ADA_CB), lambda i, j: (i, 0, j))
    wshape = jax.ShapeDtypeStruct(ada_w.shape, F32)
    return pl.pallas_call(
        body, name=name, grid=(DEPTH, nc // ADA_CB),
        in_specs=[pl.BlockSpec((16, D), lambda i, j: (0, 0)), pl.BlockSpec((1, 16, ADA_CB), lambda i, j: (i, 0, j)), wspec, wspec, wspec],
        out_specs=[wspec, wspec, wspec, wspec, pl.BlockSpec((8, D), lambda i, j: (0, 0))],
        out_shape=[wshape, wshape, wshape, wshape, jax.ShapeDtypeStruct((8, D), F32)], compiler_params=_cp(2),
    )(cvec, dcols, ada_w, m, v)


def _place():
    return lax.axis_index("x"), lax.axis_index("y"), lax.axis_index("c")


ANY = pl.BlockSpec(memory_space=pl.ANY)
VMEM_SPEC = pl.BlockSpec(memory_space=pltpu.VMEM)


def small_allgather(buf, deps, name):
    r, c = buf.shape
    n_dep = len(deps)

    def body(*refs):
        in_ref = refs[0]
        out_ref, send_sems, recv_sems, loc_sem = refs[1 + n_dep:]
        x, y, cc = _place()
        me = 4 * x + 2 * y + cc
        loc = pltpu.make_async_copy(in_ref, out_ref.at[me], loc_sem)
        loc.start()
        peers = []
        for k in range(1, 8):
            px = 1 - x if k & 4 else x
            py = 1 - y if k & 2 else y
            pc = 1 - cc if k & 1 else cc
            peers.append((px, py, pc))
        sends = []
        for k, peer in enumerate(peers):
            cp = pltpu.make_async_remote_copy(src_ref=in_ref, dst_ref=out_ref.at[me], send_sem=send_sems.at[k],
                                              recv_sem=recv_sems.at[k], device_id=peer, device_id_type=MESH)
            cp.start()
            sends.append(cp)
        for k, (px, py, pc) in enumerate(peers):
            pltpu.make_async_remote_copy(src_ref=in_ref, dst_ref=out_ref.at[4 * px + 2 * py + pc], send_sem=send_sems.at[k],
                                         recv_sem=recv_sems.at[k], device_id=(px, py, pc), device_id_type=MESH).wait_recv()
        for cp in sends:
            cp.wait_send()
        loc.wait()

    return pl.pallas_call(
        body, name=name, in_specs=[VMEM_SPEC] + [ANY] * n_dep, out_specs=VMEM_SPEC,
        out_shape=jax.ShapeDtypeStruct((8, r, c), buf.dtype),
        scratch_shapes=[pltpu.SemaphoreType.DMA((7,)), pltpu.SemaphoreType.DMA((7,)), pltpu.SemaphoreType.DMA],
    )(buf, *deps)


def _chip_peers(x, y):
    return [(1 - x, y), (x, 1 - y), (1 - x, 1 - y)]


HBM_SPEC = pl.BlockSpec(memory_space=pltpu.HBM)
SEM_SPEC = pl.BlockSpec(memory_space=pltpu.SEMAPHORE)
EFFECT = pltpu.SideEffectType.DATAFLOW_SIDE_EFFECTING


def _hbm(a):
    return pltpu.with_memory_space_constraint(a, pltpu.HBM)


def _split_copy(u, p, peer, dst_slot, chip, land_refs, src_refs, sem_refs, cc):
    px, py = peer
    src = land_refs[u].at[chip] if src_refs is None else src_refs[u].at[2 * px + py]
    return pltpu.make_async_remote_copy(src_ref=src, dst_ref=land_refs[u].at[dst_slot], send_sem=sem_refs[2 * u].at[p],
                                        recv_sem=sem_refs[2 * u + 1].at[p], device_id=(px, py, cc), device_id_type=MESH)


def split_start(lands, srcs, after, name):
    n = len(lands)
    ops = list(lands) + (list(srcs) if srcs is not None else [])
    n_ops = len(ops)

    def body(*refs):
        land_refs = refs[:n]
        src_refs = refs[n:n_ops] if srcs is not None else None
        sem_refs = refs[n_ops + 1:n_ops + 1 + 2 * n]
        x, y, cc = _place()
        chip = 2 * x + y
        for u in range(n):
            for p, peer in enumerate(_chip_peers(x, y)):
                _split_copy(u, p, peer, chip, chip, land_refs, src_refs, sem_refs, cc).start()
        refs[-1][...] = jnp.zeros((8, 128), F32)

    outs = pl.pallas_call(
        body, name=name, in_specs=[HBM_SPEC] * n_ops + [ANY],
        out_specs=[SEM_SPEC] * (2 * n) + [HBM_SPEC] * n_ops + [VMEM_SPEC],
        out_shape=[pltpu.SemaphoreType.DMA((3,))] * (2 * n) + [pltpu.HBM(a.shape, a.dtype) for a in ops]
        + [jax.ShapeDtypeStruct((8, 128), F32)],
        input_output_aliases={k: 2 * n + k for k in range(n_ops)},
        compiler_params=pltpu.CompilerParams(has_side_effects=EFFECT),
    )(*[_hbm(a) for a in ops], after)
    sems = list(outs[:2 * n])
    thru = list(outs[2 * n:2 * n + n_ops])
    return sems, thru[:n], thru[n:], outs[-1]


def split_wait(lands, srcs, sems, after, name):
    n = len(lands)
    ops = list(lands) + (list(srcs) if srcs is not None else [])
    n_ops = len(ops)

    def body(*refs):
        land_refs = refs[:n]
        src_refs = refs[n:n_ops] if srcs is not None else None
        sem_refs = refs[n_ops:n_ops + 2 * n]
        x, y, cc = _place()
        chip = 2 * x + y
        for u in range(n):
            for p, peer in enumerate(_chip_peers(x, y)):
                cp = _split_copy(u, p, peer, 2 * peer[0] + peer[1], chip, land_refs, src_refs, sem_refs, cc)
                cp.wait_send()
                cp.wait_recv()

    outs = pl.pallas_call(
        body, name=name, in_specs=[HBM_SPEC] * n_ops + [SEM_SPEC] * (2 * n) + [ANY],
        out_specs=[HBM_SPEC] * n_ops, out_shape=[pltpu.HBM(a.shape, a.dtype) for a in ops],
        input_output_aliases={k: k for k in range(n_ops)},
        compiler_params=pltpu.CompilerParams(has_side_effects=EFFECT),
    )(*ops, *sems, after)
    return list(outs[:n]), list(outs[n:])


def _sibling_copy(k, src_refs, zone_refs, sem_refs):
    x, y, cc = _place()
    return pltpu.make_async_remote_copy(src_ref=src_refs[k], dst_ref=zone_refs[k], send_sem=sem_refs[2 * k], recv_sem=sem_refs[2 * k + 1],
                                        device_id=(x, y, 1 - cc), device_id_type=MESH)


def sibling_start(parts, name):
    n = len(parts)
    ops = list(parts) + [lax.empty(p.shape, p.dtype) for p in parts]

    def body(*refs):
        for k in range(n):
            _sibling_copy(k, refs[:n], refs[n:2 * n], refs[2 * n:4 * n]).start()

    outs = pl.pallas_call(
        body, name=name, in_specs=[HBM_SPEC] * (2 * n),
        out_specs=[SEM_SPEC] * (2 * n) + [HBM_SPEC] * (2 * n),
        out_shape=[pltpu.SemaphoreType.DMA(())] * (2 * n) + [pltpu.HBM(a.shape, a.dtype) for a in ops],
        input_output_aliases={k: 2 * n + k for k in range(2 * n)},
        compiler_params=pltpu.CompilerParams(has_side_effects=EFFECT),
    )(*[_hbm(a) for a in ops])
    return list(outs[2 * n:3 * n]), list(outs[3 * n:]), list(outs[:2 * n])


def sibling_wait(parts, zones, sems, after, name):
    n = len(parts)

    def body(*refs):
        for k in range(n):
            cp = _sibling_copy(k, refs[:n], refs[n:2 * n], refs[2 * n:4 * n])
            cp.wait_send()
            cp.wait_recv()

    outs = pl.pallas_call(
        body, name=name, in_specs=[HBM_SPEC] * (2 * n) + [SEM_SPEC] * (2 * n) + [ANY],
        out_specs=[HBM_SPEC] * (2 * n), out_shape=[pltpu.HBM(a.shape, a.dtype) for a in list(parts) + list(zones)],
        input_output_aliases={k: k for k in range(2 * n)},
        compiler_params=pltpu.CompilerParams(has_side_effects=EFFECT),
    )(*parts, *zones, *sems, after)
    return list(outs[:n]), list(outs[n:])


SMALL_ROWS = 88
FIN_ROWS = 72


def small_finish(g3, g4, c_ctx, lbp, name):
    def body(g3_ref, g4_ref, cc_ref, lbp_ref, o_ref, s_ref):
        s = g3_ref[0]
        for k in range(1, 8):
            s = s + g3_ref[k]
        s_ref[...] = s
        for i in range(DEPTH):
            o_ref[8 * i:8 * i + 8, :] = s_ref[16 * i:16 * i + 8, :] + s_ref[16 * i + 8:16 * i + 16, :]
        acc = g4_ref[0]
        for k in (2, 4, 6):
            acc = acc + g4_ref[k]
        cc = cc_ref[...]
        sg = _sigmoid(cc)
        row = jnp.sum(acc, axis=0, keepdims=True) * (sg * (1.0 + cc * (1.0 - sg)))
        o_ref[32:40, :] = jnp.broadcast_to(row, (8, D))
        o_ref[40:64, :] = s_ref[64:88, :]
        o_ref[64:72, :] = jnp.zeros((8, D), F32)
        for d in range(2):
            pp = lbp_ref[2 * d:2 * d + 1, :] * lbp_ref[2 * d + 1:2 * d + 2, :] * s_ref[75 + d:76 + d, :]
            o_ref[64 + 2 * d:65 + 2 * d, :] = -pp
            o_ref[65 + 2 * d:66 + 2 * d, :] = pp

    return pl.pallas_call(
        body, name=name, in_specs=[VMEM_SPEC] * 4, out_specs=VMEM_SPEC,
        out_shape=jax.ShapeDtypeStruct((FIN_ROWS, D), F32),
        scratch_shapes=[pltpu.VMEM((SMALL_ROWS, D), F32)],
    )(g3, g4, c_ctx, lbp)


def _pack_rows(arrs):
    flat = jnp.concatenate([a.reshape(-1) for a in arrs])
    n = -(-flat.shape[0] // (8 * D)) * 8 * D
    return jnp.pad(flat, (0, n - flat.shape[0])).reshape(n // D, D)


def _unpack_rows(packed, shapes):
    flat = packed.reshape(-1)
    outs, off = [], 0
    for s in shapes:
        size = 1
        for k in s:
            size *= k
        outs.append(flat[off:off + size].reshape(s))
        off += size
    return outs


def _pad8(a):
    return jnp.pad(a, ((0, 8 - a.shape[0]), (0, 0)))


def kernel(x, c, ctx, c_ctx, ada_w, ada_b, norm1, norm2, norm_f, mlp_w1, mlp_w2, hgrn_w_in, hgrn_lb, hgrn_gnorm, hgrn_w_out, conv_w_in, conv_w, conv_b, conv_w_out, loss_target, m_c_ctx, m_ada_w, m_ada_b, m_norm1, m_norm2, m_norm_f, m_mlp_w1, m_mlp_w2, m_hgrn_w_in, m_hgrn_lb, m_hgrn_gnorm, m_hgrn_w_out, m_conv_w_in, m_conv_w, m_conv_b, m_conv_w_out, v_c_ctx, v_ada_w, v_ada_b, v_norm1, v_norm2, v_norm_f, v_mlp_w1, v_mlp_w2, v_hgrn_w_in, v_hgrn_lb, v_hgrn_gnorm, v_hgrn_w_out, v_conv_w_in, v_conv_w, v_conv_b, v_conv_w_out):
    xi, yi, ci = _place()
    me = 4 * xi + 2 * yi + ci
    chip = 2 * xi + yi
    seq = x.shape[1]
    assert ctx.shape[1] == TM and seq % TM == 0 and (seq + TM) % TMW == 0
    n_lat = seq // TM
    sd = D // 4
    nca = ada_w.shape[2]
    xs = jnp.concatenate([x[0], ctx[0]], axis=0)

    big = [(mlp_w1, m_mlp_w1, v_mlp_w1), (mlp_w2, m_mlp_w2, v_mlp_w2), (hgrn_w_in, m_hgrn_w_in, v_hgrn_w_in),
           (hgrn_w_out, m_hgrn_w_out, v_hgrn_w_out), (conv_w_in, m_conv_w_in, v_conv_w_in), (conv_w_out, m_conv_w_out, v_conv_w_out)]
    big_names = ["w1", "w2", "hin", "hout", "cin", "cout"]
    flat2 = lambda a: a.reshape(a.shape[0] * a.shape[1], a.shape[2])
    tensors = dict(zip(big_names, big))
    chip1 = jnp.reshape(chip, (1,)).astype(jnp.int32)
    order = []
    for i in range(DEPTH):
        order += [("hin", i // 2), ("hout", i // 2)] if i % 2 == 0 else [("cin", i // 2), ("cout", i // 2)]
        order += [("w1", i), ("w2", i)]
    lands = [cast_to_slot(flat2(tensors[n][0]), idx, tensors[n][0].shape[1], chip1, f"cast_{n}_{idx}") for n, idx in order]
    sh_rows = jnp.concatenate([hgrn_lb.reshape(4, sd), conv_w.reshape(6, sd), conv_b.reshape(2, sd)], axis=0)
    buf1 = jnp.concatenate([c, jnp.pad(sh_rows, ((0, 0), (0, D - sd))), jnp.zeros((3, D), F32)], axis=0)
    g1 = small_allgather(buf1, [], "gather_small_in")
    first_sems, first_lands, _, first_token = split_start(lands[:1], None, g1, "gather_start_first")
    cvec = jnp.concatenate([g1[:, 0, :], jnp.broadcast_to(c_ctx[None], (8, D))], axis=0)
    shf = g1[0::2, 1:13, :sd].transpose(1, 0, 2).reshape(12, D)
    lb_p = jax.nn.softmax(shf[0:4].reshape(2, 2, D), axis=1)
    lower = jnp.cumsum(lb_p, axis=1) - lb_p[:, :1]
    lbs = [lower[:, 0], lower[:, 1]]
    cw8 = [_pad8(shf[4:7]), _pad8(shf[7:10])]
    cb = [shf[10:11], shf[11:12]]

    bias = lax.dynamic_slice_in_dim(ada_b, chip * nca, nca, axis=1).reshape(DEPTH, 1, nca)
    ada_part = ada_fwd(cvec, ada_w, bias, "ada_fwd")
    g2 = small_allgather(ada_part.reshape(DEPTH * 16, nca), [first_token] + lands[1:], "gather_ada")
    ada_full = g2[0::2].reshape(4, DEPTH, 16, nca).transpose(1, 2, 0, 3).reshape(DEPTH, 16, 4 * nca)
    lat = lax.dynamic_slice_in_dim(ada_full, me, 1, axis=1)[:, 0]
    mods = [jnp.stack([_pad8(lat[i].reshape(6, D)), _pad8(ada_full[i, 8].reshape(6, D))]) for i in range(DEPTH)]

    rest_sems, rest_lands, _, rest_token = split_start(lands[1:], None, g2, "gather_start")
    w_sems = first_sems + rest_sems
    lands = first_lands + rest_lands
    unit = {key: u for u, key in enumerate(order)}

    def wts(n, idx, after):
        u = unit[(n, idx)]
        if u == 0:
            after = rest_token
        (w,), _ = split_wait([lands[u]], None, w_sems[2 * u:2 * u + 2], after, f"gather_wait_{n}_{idx}")
        return w.reshape(w.shape[0] * w.shape[1], w.shape[2]) if n in ("w2", "hout", "cout") else w

    started = []

    def on_grads(i, tag, g):
        names = sorted(g)
        gs = [g[n].reshape(4, g[n].shape[0] * g[n].shape[1] // 4, g[n].shape[2]) for n in names]
        sems, zones, srcs, token = split_start([lax.empty(a.shape, BF16) for a in gs], gs, chip1, f"grad_start_{tag}_{i}")
        started.append(([(n, i if n in ("w1", "w2") else i // 2) for n in names], sems, zones, srcs))
        return token

    done = {}
    acc = {n: lax.empty(flat2(w).shape, F32) for n, (w, _, _) in tensors.items()}
    early_names = ["w1", "w2", "cin", "cout"]
    late_names = ["hin", "hout"]

    def finish_units(group, after, name):
        units = [(key, sems[2 * u:2 * u + 2], zones[u], srcs[u]) for ks, sems, zones, srcs in group for u, key in enumerate(ks)]
        zones, srcs = split_wait([u[2] for u in units], [u[3] for u in units], [s for u in units for s in u[1]], after, name)
        for (key, _, _, _), zone, own in zip(units, zones, srcs):
            acc[key[0]] = sum_slots(own, zone, acc[key[0]], key[1], chip1, f"sum_{key[0]}_{key[1]}")

    def after_backward(small, token):
        rows3 = jnp.concatenate(small["dmod"] + small["norm1"] + small["norm2"] + [small["norm_f"]] + small["gnorm"]
                                + [small["lb"][1]] + small["cw"] + small["cb"] + [jnp.tile(token[0:3], (1, D // 128))], axis=0)
        g3 = small_allgather(rows3, [], "gather_small_out")
        dmat = g3[:, :64].reshape(8, DEPTH, 2, 8, D)[:, :, :, :6].transpose(1, 2, 0, 3, 4).reshape(DEPTH, 16, 6 * D)
        dcols = lax.dynamic_slice_in_dim(dmat, chip * nca, nca, axis=2)
        *done["ada"], acc4 = ada_bwd(cvec, dcols, ada_w, m_ada_w, v_ada_w, "ada_bwd")
        g4 = small_allgather(acc4, [], "gather_cctx")
        done["fin"] = small_finish(g3, g4, c_ctx[None], _pad8(lb_p.reshape(4, D)), "small_finish")
        finish_units(list(started), done["fin"], "grad_wait_early")
        done["sib_early"] = sibling_start([acc[n] for n in early_names], "sibling_start_early")
        return done["sib_early"][0][-1]

    lane, dx, last_token = local_step(xs, loss_target[0], mods, norm1, norm2, norm_f[None], lbs, hgrn_gnorm, cw8, cb, wts, n_lat,
                                      on_grads, after_backward)
    loss = lax.psum(0.5 * jnp.sum(lane) / D, ("x", "y", "c"))
    grad_x = dx[None]
    g_ada_w, d_ada_w, nm_ada_w, nv_ada_w = done["ada"]
    fin = done["fin"]
    cols = lambda a: lax.dynamic_slice_in_dim(a, chip * sd, sd, axis=a.ndim - 1)
    small_g = [fin[32], fin[0:32].reshape(DEPTH, 8, D)[:, :6].reshape(DEPTH, 6 * D), fin[40:44], fin[44:48], fin[48], fin[49:51],
               cols(fin[64:68].reshape(2, 2, D)), cols(fin[53:59].reshape(2, 3, D)), cols(fin[59:61])]
    small_w = [c_ctx, ada_b, norm1, norm2, norm_f, hgrn_gnorm, hgrn_lb, conv_w, conv_b]
    small_m = [m_c_ctx, m_ada_b, m_norm1, m_norm2, m_norm_f, m_hgrn_gnorm, m_hgrn_lb, m_conv_w, m_conv_b]
    small_v = [v_c_ctx, v_ada_b, v_norm1, v_norm2, v_norm_f, v_hgrn_gnorm, v_hgrn_lb, v_conv_w, v_conv_b]
    shapes = [w.shape for w in small_w]
    packed = adamw([_pack_rows(small_g)], _pack_rows(small_w), _pack_rows(small_m), _pack_rows(small_v), "adamw_small")
    s_g, s_d, s_m, s_v = [_unpack_rows(p, shapes) for p in packed]

    results = {}

    def finish_tensors(names, sib, after, name):
        mine, other = sibling_wait(*sib, after, name)
        for n, pm, po in zip(names, mine, other):
            w, m, v = tensors[n]
            results[n] = [a.reshape(w.shape) for a in adamw([pm, po], flat2(w), flat2(m), flat2(v), f"adamw_{n}")]

    finish_tensors(early_names, done["sib_early"], last_token, "sibling_wait_early")
    finish_units(started[-1:], results["cout"][0], "grad_wait_late")
    sib_late = sibling_start([acc[n] for n in late_names], "sibling_start_late")
    finish_tensors(late_names, sib_late, results["cin"][0], "sibling_wait_late")
    b_g, b_d, b_m, b_v = [[results[n][k] for n in big_names] for k in range(4)]

    def ordered(s, a, b):
        return [s[0], a, s[1], s[2], s[3], s[4], b[0], b[1], b[2], s[6], s[5], b[3], b[4], s[7], s[8], b[5]]

    return (loss, grad_x, *ordered(s_g, g_ada_w, b_g), *ordered(s_d, d_ada_w, b_d), *ordered(s_m, nm_ada_w, b_m),
            *ordered(s_v, nv_ada_w, b_v))
```

```python
import functools

import jax
import jax.numpy as jnp
from jax import lax
from jax.experimental import pallas as pl
from jax.experimental.pallas import tpu as pltpu

F32 = jnp.float32
BF16 = jnp.bfloat16
MESH = pl.DeviceIdType.MESH

D = 1024
HD = 128
NH = D // HD
CH = 64
TM = 256
TMW = 768
EPS = 1e-6
DEPTH = 4
VMEM_LIMIT = 56 * 1024 * 1024

ADAM_LR = 0.001
ADAM_B1 = 0.9
ADAM_B2 = 0.999
ADAM_EPS = 1e-08
ADAM_WD = 0.01
ADAM_STEP = 10


def _cp(n_grid):
    return pltpu.CompilerParams(dimension_semantics=("arbitrary",) * n_grid, vmem_limit_bytes=VMEM_LIMIT)


def _dot(a, b):
    return jnp.dot(a, b, preferred_element_type=F32)


def _dot_nt(a, b):
    return lax.dot_general(a, b, (((1,), (1,)), ((), ())), preferred_element_type=F32)


def _dot_tn(a, b):
    return lax.dot_general(a, b, (((0,), (0,)), ((), ())), preferred_element_type=F32)


def _sigmoid(z):
    return 1.0 / (1.0 + jnp.exp(-z))


def _norm_mod(x, gain, sh, sc):
    r = lax.rsqrt(jnp.mean(x * x, axis=-1, keepdims=True) + EPS)
    xn = x * r
    yn = xn * gain
    return r, xn, yn, yn * (1.0 + sc) + sh


def _row_spec(width):
    return pl.BlockSpec((TM, width), lambda i: (i, 0))


def _col_spec(col):
    return pl.BlockSpec((TM, D), lambda i: (i, col))


def _full_spec(shape):
    nd = len(shape)
    return pl.BlockSpec(shape, lambda i: (0,) * nd)


def _mod_spec(n_lat):
    return pl.BlockSpec((1, 8, D), lambda i: (i // n_lat, 0, 0))


def _f32(ref):
    return ref[...].astype(F32)


def proj_fwd(x, gain, mod, m0, w4, n_lat, dtype, name):
    t = x.shape[0]
    nb, _, ns = w4.shape

    def body(x_ref, gain_ref, mod_ref, w_ref, p_ref):
        _, _, _, h = _norm_mod(x_ref[...], gain_ref[...], mod_ref[0, m0:m0 + 1, :], mod_ref[0, m0 + 1:m0 + 2, :])
        hb = h.astype(BF16)
        for c in range(nb):
            p_ref[:, c * ns:(c + 1) * ns] = _dot(hb, w_ref[c]).astype(dtype)

    return pl.pallas_call(
        body, name=name, grid=(t // TM,),
        in_specs=[_row_spec(D), _full_spec((1, D)), _mod_spec(n_lat), _full_spec(w4.shape)],
        out_specs=_row_spec(nb * ns),
        out_shape=jax.ShapeDtypeStruct((t, nb * ns), dtype),
        compiler_params=_cp(1),
    )(x, gain, mod, w4)


def proj_bwd(parts, w4, x, gain, mod, m0, dx_in, n_lat, lat_only, name):
    t = x.shape[0]
    nb, _, ns = w4.shape
    n = nb * ns
    n_parts = len(parts)
    widths = [p.shape[1] for p in parts]
    offs = [sum(widths[:k]) for k in range(n_parts)]
    assert sum(widths) == n
    single = n_parts == 1

    def body(*refs):
        part_refs = refs[:n_parts]
        w_ref, x_ref, gain_ref, mod_ref, dxin_ref = refs[n_parts:n_parts + 5]
        rest = refs[n_parts + 5:]
        if single:
            dx_ref, hb_ref, acc_ref = rest
            src = part_refs[0]
        else:
            dx_ref, hb_ref, acc_ref, dpb_ref = rest
            for p_ref, off, w in zip(part_refs, offs, widths):
                dpb_ref[:, off:off + w] = p_ref[...]
            src = dpb_ref
        i = pl.program_id(0)

        @pl.when(i == 0)
        def _():
            acc_ref[...] = jnp.zeros_like(acc_ref)

        gain = gain_ref[...]
        sc = mod_ref[0, m0 + 1:m0 + 2, :]
        r, xn, yn, h = _norm_mod(x_ref[...], gain, mod_ref[0, m0:m0 + 1, :], sc)
        hb_ref[...] = h.astype(BF16)
        dh = _dot_nt(src[:, 0:ns], w_ref[0])
        for c in range(1, nb):
            dh = dh + _dot_nt(src[:, c * ns:(c + 1) * ns], w_ref[c])
        dsh = jnp.sum(dh, axis=0, keepdims=True)
        dsc = jnp.sum(dh * yn, axis=0, keepdims=True)
        dyn = dh * (1.0 + sc)
        dgain = jnp.sum(dyn * xn, axis=0, keepdims=True)
        dxn = dyn * gain
        dx = dxin_ref[...] + r * (dxn - xn * jnp.mean(dxn * xn, axis=-1, keepdims=True))
        if lat_only:
            @pl.when(i < n_lat)
            def _():
                dx_ref[...] = dx
        else:
            dx_ref[...] = dx
        latf = (i < n_lat).astype(F32)
        ctxf = 1.0 - latf
        acc_ref[0:1, :] += dgain
        acc_ref[1:2, :] += dsh * latf
        acc_ref[2:3, :] += dsc * latf
        acc_ref[3:4, :] += dsh * ctxf
        acc_ref[4:5, :] += dsc * ctxf

    dx_rows = n_lat * TM if lat_only else t
    dx_spec = pl.BlockSpec((TM, D), lambda i: (jnp.minimum(i, n_lat - 1), 0)) if lat_only else _row_spec(D)
    out_specs = [dx_spec, _row_spec(D), _full_spec((8, D))]
    out_shape = [jax.ShapeDtypeStruct((dx_rows, D), F32), jax.ShapeDtypeStruct((t, D), BF16), jax.ShapeDtypeStruct((8, D), F32)]
    if not single:
        out_specs.append(_row_spec(n))
        out_shape.append(jax.ShapeDtypeStruct((t, n), BF16))
    outs = pl.pallas_call(
        body, name=name, grid=(t // TM,),
        in_specs=[_row_spec(w) for w in widths]
        + [_full_spec(w4.shape), _row_spec(D), _full_spec((1, D)), _mod_spec(n_lat), _row_spec(D)],
        out_specs=out_specs, out_shape=out_shape, compiler_params=_cp(1),
    )(*parts, w4, x, gain, mod, dx_in)
    if single:
        return outs[0], outs[1], parts[0], outs[2]
    return outs[0], outs[1], outs[3], outs[2]


def dw_tn(a, b, nb, a_blocked, square_a, dep, name):
    t = a.shape[0]
    ka = a.shape[1] // nb if a_blocked else a.shape[1]
    kb = b.shape[1] if a_blocked else b.shape[1] // nb
    n_k = t // TMW

    def body(a_ref, b_ref, _, o_ref, acc):
        k = pl.program_id(1)

        @pl.when(k == 0)
        def _():
            acc[...] = jnp.zeros_like(acc)

        a = a_ref[...]
        acc[...] += _dot_tn(a * a if square_a else a, b_ref[...])

        @pl.when(k == n_k - 1)
        def _():
            o_ref[0] = acc[...].astype(BF16)

    a_spec = pl.BlockSpec((TMW, ka), (lambda j, k: (k, j)) if a_blocked else (lambda j, k: (k, 0)))
    b_spec = pl.BlockSpec((TMW, kb), (lambda j, k: (k, 0)) if a_blocked else (lambda j, k: (k, j)))
    return pl.pallas_call(
        body, name=name, grid=(nb, n_k),
        in_specs=[a_spec, b_spec, ANY],
        out_specs=pl.BlockSpec((1, ka, kb), lambda j, k: (j, 0, 0)),
        out_shape=jax.ShapeDtypeStruct((nb, ka, kb), BF16),
        scratch_shapes=[pltpu.VMEM((ka, kb), F32)],
        compiler_params=_cp(2),
    )(a, b, dep)


def outproj_fwd(prologue, extras, extra_specs, w, x, mod, m0, n_lat, name):
    t = x.shape[0]
    k = w.shape[0]
    n_extra = len(extras)

    def body(*refs):
        ex = refs[:n_extra]
        w_ref, x_ref, mod_ref, xo_ref, y_ref, ab_ref = refs[n_extra:]
        ab = prologue(pl.program_id(0), *ex).astype(BF16)
        ab_ref[...] = ab
        y = _dot(ab, w_ref[...])
        y_ref[...] = y.astype(BF16)
        xo_ref[...] = x_ref[...] + mod_ref[0, m0 + 2:m0 + 3, :] * y

    return pl.pallas_call(
        body, name=name, grid=(t // TM,),
        in_specs=list(extra_specs) + [_full_spec(w.shape), _row_spec(D), _mod_spec(n_lat)],
        out_specs=[_row_spec(D), _row_spec(D), _row_spec(k)],
        out_shape=[jax.ShapeDtypeStruct((t, D), F32), jax.ShapeDtypeStruct((t, D), BF16), jax.ShapeDtypeStruct((t, k), BF16)],
        compiler_params=_cp(1),
    )(*extras, w, x, mod)


def outproj_bwd(epilogue, extras, extra_specs, ep_out_specs, ep_out_shapes, w, dxn, y, mod, m0, n_lat, dep, name):
    t = dxn.shape[0]
    n_extra = len(extras)

    def body(*refs):
        ex = refs[:n_extra]
        w_ref, dxn_ref, y_ref, mod_ref, _, dyb_ref, acc_ref = refs[n_extra:n_extra + 7]
        ep_outs = refs[n_extra + 7:]
        i = pl.program_id(0)

        @pl.when(i == 0)
        def _():
            acc_ref[...] = jnp.zeros_like(acc_ref)

        dxv = dxn_ref[...]
        dyb = (dxv * mod_ref[0, m0 + 2:m0 + 3, :]).astype(BF16)
        dyb_ref[...] = dyb
        dg = jnp.sum(dxv * _f32(y_ref), axis=0, keepdims=True)
        latf = (i < n_lat).astype(F32)
        acc_ref[0:1, :] += dg * latf
        acc_ref[1:2, :] += dg * (1.0 - latf)
        epilogue(i, _dot_nt(dyb, w_ref[...]), ex, ep_outs, acc_ref)

    outs = pl.pallas_call(
        body, name=name, grid=(t // TM,),
        in_specs=list(extra_specs) + [_full_spec(w.shape), _row_spec(D), _row_spec(D), _mod_spec(n_lat), ANY],
        out_specs=[_row_spec(D), _full_spec((8, D))] + list(ep_out_specs),
        out_shape=[jax.ShapeDtypeStruct((t, D), BF16), jax.ShapeDtypeStruct((8, D), F32)] + list(ep_out_shapes),
        compiler_params=_cp(1),
    )(*extras, w, dxn, y, mod, dep)
    return outs[0], outs[1], outs[2:]


def mlp_fwd(x, gain, mod, w1, w2, n_lat, name):
    t = x.shape[0]
    nb, _, ns = w1.shape

    def body(x_ref, gain_ref, mod_ref, w1_ref, w2_ref, xo_ref, y_ref, rb_ref):
        x = x_ref[...]
        _, _, _, h = _norm_mod(x, gain_ref[...], mod_ref[0, 3:4, :], mod_ref[0, 4:5, :])
        hb = h.astype(BF16)
        y = None
        for c in range(nb):
            r = jnp.maximum(_dot(hb, w1_ref[c]), 0.0)
            rb_ref[:, c * ns:(c + 1) * ns] = r.astype(BF16)
            yc = _dot((r * r).astype(BF16), w2_ref[c * ns:(c + 1) * ns, :])
            y = yc if y is None else y + yc
        y_ref[...] = y.astype(BF16)
        xo_ref[...] = x + mod_ref[0, 5:6, :] * y

    return pl.pallas_call(
        body, name=name, grid=(t // TM,),
        in_specs=[_row_spec(D), _full_spec((1, D)), _mod_spec(n_lat), _full_spec(w1.shape), _full_spec(w2.shape)],
        out_specs=[_row_spec(D), _row_spec(D), _row_spec(nb * ns)],
        out_shape=[jax.ShapeDtypeStruct((t, D), F32), jax.ShapeDtypeStruct((t, D), BF16), jax.ShapeDtypeStruct((t, nb * ns), BF16)],
        compiler_params=_cp(1),
    )(x, gain, mod, w1, w2)


def mlp_bwd(dxn, y, ab, x, gain, mod, w1, w2, n_lat, dep, name):
    t = x.shape[0]
    nb, _, ns = w1.shape

    def body(dxn_ref, y_ref, rb_ref, x_ref, gain_ref, mod_ref, w1_ref, w2_ref, _, dx_ref, dyb_ref, dp_ref, hb_ref, acc_ref):
        i = pl.program_id(0)

        @pl.when(i == 0)
        def _():
            acc_ref[...] = jnp.zeros_like(acc_ref)

        dxv = dxn_ref[...]
        dyb = (dxv * mod_ref[0, 5:6, :]).astype(BF16)
        dyb_ref[...] = dyb
        dg = jnp.sum(dxv * _f32(y_ref), axis=0, keepdims=True)
        gain = gain_ref[...]
        sc = mod_ref[0, 4:5, :]
        r, xn, yn, h = _norm_mod(x_ref[...], gain, mod_ref[0, 3:4, :], sc)
        hb_ref[...] = h.astype(BF16)
        dh = None
        for c in range(nb):
            cols = slice(c * ns, (c + 1) * ns)
            da = _dot_nt(dyb, w2_ref[cols, :])
            dp = (da * (2.0 * rb_ref[:, cols].astype(F32))).astype(BF16)
            dp_ref[:, cols] = dp
            d = _dot_nt(dp, w1_ref[c])
            dh = d if dh is None else dh + d
        dsh = jnp.sum(dh, axis=0, keepdims=True)
        dsc = jnp.sum(dh * yn, axis=0, keepdims=True)
        dyn = dh * (1.0 + sc)
        dgain = jnp.sum(dyn * xn, axis=0, keepdims=True)
        dxn_ = dyn * gain
        dx_ref[...] = dxv + r * (dxn_ - xn * jnp.mean(dxn_ * xn, axis=-1, keepdims=True))
        latf = (i < n_lat).astype(F32)
        ctxf = 1.0 - latf
        acc_ref[0:1, :] += dgain
        acc_ref[1:2, :] += dsh * latf
        acc_ref[2:3, :] += dsc * latf
        acc_ref[3:4, :] += dsh * ctxf
        acc_ref[4:5, :] += dsc * ctxf
        acc_ref[5:6, :] += dg * latf
        acc_ref[6:7, :] += dg * ctxf

    return pl.pallas_call(
        body, name=name, grid=(t // TM,),
        in_specs=[_row_spec(D), _row_spec(D), _row_spec(nb * ns), _row_spec(D), _full_spec((1, D)), _mod_spec(n_lat),
                  _full_spec(w1.shape), _full_spec(w2.shape), ANY],
        out_specs=[_row_spec(D), _row_spec(D), _row_spec(nb * ns), _row_spec(D), _full_spec((8, D))],
        out_shape=[jax.ShapeDtypeStruct((t, D), F32), jax.ShapeDtypeStruct((t, D), BF16), jax.ShapeDtypeStruct((t, nb * ns), BF16),
                   jax.ShapeDtypeStruct((t, D), BF16), jax.ShapeDtypeStruct((8, D), F32)],
        compiler_params=_cp(1),
    )(dxn, y, ab, x, gain, mod, w1, w2, dep)


def readout_prologue(i, o0_ref, o1_ref, gate_ref, gn_ref):
    o = _f32(o0_ref) + _f32(o1_ref)
    gate = gate_ref[...]
    w = gn_ref[...] * (gate * _sigmoid(gate))
    pieces = []
    for h in range(NH):
        sl = slice(h * HD, (h + 1) * HD)
        oh = o[:, sl]
        pieces.append(oh * lax.rsqrt(jnp.mean(oh * oh, axis=-1, keepdims=True) + EPS) * w[:, sl])
    return jnp.concatenate(pieces, axis=1)


def readout_epilogue(i, da, ex, outs, acc_ref):
    o0_ref, o1_ref, gate_ref, gn_ref = ex
    do_ref, dgate_ref = outs
    o = _f32(o0_ref) + _f32(o1_ref)
    gate = gate_ref[...]
    gn = gn_ref[...]
    sg = _sigmoid(gate)
    silu = gate * sg
    dsilu = sg * (1.0 + gate * (1.0 - sg))
    for h in range(NH):
        sl = slice(h * HD, (h + 1) * HD)
        oh = o[:, sl]
        r = lax.rsqrt(jnp.mean(oh * oh, axis=-1, keepdims=True) + EPS)
        nh = oh * r
        dah = da[:, sl]
        acc_ref[2:3, sl] += jnp.sum(dah * nh * silu[:, sl], axis=0, keepdims=True)
        dgate_ref[:, sl] = (dah * nh * gn[:, sl] * dsilu[:, sl]).astype(BF16)
        dn = dah * gn[:, sl] * silu[:, sl]
        do_ref[:, sl] = r * (dn - nh * jnp.mean(dn * nh, axis=-1, keepdims=True))


def _seg_masks(i, n_lat):
    rows = lax.broadcasted_iota(jnp.int32, (TM, 1), 0)
    latf = (i < n_lat).astype(F32)
    ctxf = 1.0 - latf
    prev_ok = (rows % CH != 0).astype(F32) * latf + (rows != 0).astype(F32) * ctxf
    next_ok = (rows % CH != CH - 1).astype(F32) * latf + (rows != TM - 1).astype(F32) * ctxf
    return prev_ok, next_ok


def _shifts(i, n_lat, sft, cur, halo_prev, halo_next):
    if sft == 1:
        prev_ok, next_ok = _seg_masks(i, n_lat)
        return pltpu.roll(cur, 1, 0) * prev_ok, pltpu.roll(cur, TM - 1, 0) * next_ok
    has_prev = jnp.logical_and(i > 0, i < n_lat).astype(F32)
    has_next = (i < n_lat - 1).astype(F32)
    prev = jnp.concatenate([halo_prev * has_prev, cur[:TM - CH]], axis=0)
    nxt = jnp.concatenate([cur[CH:], halo_next * has_next], axis=0)
    return prev, nxt


def _conv_u(sft, ex):
    if sft == 1:
        gb_ref, gc_ref, xi_ref, cw_ref, cb_ref = ex
        return gb_ref, _f32(gc_ref) * _f32(xi_ref), None, None, cw_ref, cb_ref
    gb_ref, gc_ref, xi_ref, gcp_ref, xip_ref, gcn_ref, xin_ref, cw_ref, cb_ref = ex
    return gb_ref, _f32(gc_ref) * _f32(xi_ref), _f32(gcp_ref) * _f32(xip_ref), _f32(gcn_ref) * _f32(xin_ref), cw_ref, cb_ref


def _conv_value(i, n_lat, sft, ex):
    gb_ref, u, up, un, cw_ref, cb_ref = _conv_u(sft, ex)
    u_prev, u_next = _shifts(i, n_lat, sft, u, up, un)
    return gb_ref, cb_ref[...] + cw_ref[0:1, :] * u_prev + cw_ref[1:2, :] * u + cw_ref[2:3, :] * u_next


def make_conv_prologue(n_lat, sft):
    def prologue(i, *ex):
        gb_ref, conv = _conv_value(i, n_lat, sft, ex)
        return _f32(gb_ref) * conv
    return prologue


def make_conv_epilogue(n_lat, sft):
    def epilogue(i, da, ex, outs, acc_ref):
        gb_ref, conv = _conv_value(i, n_lat, sft, ex)
        outs[0][...] = (da * _f32(gb_ref)).astype(BF16)
        outs[1][...] = (da * conv).astype(BF16)
    return epilogue


def _conv_specs(sft, t):
    specs = [_col_spec(0), _col_spec(1), _col_spec(2)]
    if sft != 1:
        per = TM // CH
        last = t // CH - 1
        for fn in (lambda i: jnp.maximum(i * per - 1, 0), lambda i: jnp.minimum(i * per + per, last)):
            for col in (1, 2):
                specs.append(pl.BlockSpec((CH, D), functools.partial(lambda i, f, c: (f(i), c), f=fn, c=col)))
    return specs + [_full_spec((8, D)), _full_spec((1, D))]


def _conv_args(sft, p, cw8, cb):
    return [p] * (3 if sft == 1 else 7) + [cw8, cb]


def conv_bwd(dconv, p, cw8, sft, n_lat, name):
    t = dconv.shape[0]
    halo = sft != 1

    def body(*refs):
        if halo:
            dc_ref, dcp_ref, dcn_ref, gc_ref, xi_ref, gcp_ref, xip_ref, gcn_ref, xin_ref, cw_ref, dgc_ref, dxi_ref, acc_ref = refs
            up, un = _f32(gcp_ref) * _f32(xip_ref), _f32(gcn_ref) * _f32(xin_ref)
            dcp, dcn = _f32(dcp_ref), _f32(dcn_ref)
        else:
            dc_ref, gc_ref, xi_ref, cw_ref, dgc_ref, dxi_ref, acc_ref = refs
            up = un = dcp = dcn = None
        i = pl.program_id(0)

        @pl.when(i == 0)
        def _():
            acc_ref[...] = jnp.zeros_like(acc_ref)

        gc = _f32(gc_ref)
        xi = _f32(xi_ref)
        u = gc * xi
        dc = _f32(dc_ref)
        u_prev, u_next = _shifts(i, n_lat, sft, u, up, un)
        dc_prev, dc_next = _shifts(i, n_lat, sft, dc, dcp, dcn)
        acc_ref[0:1, :] += jnp.sum(dc * u_prev, axis=0, keepdims=True)
        acc_ref[1:2, :] += jnp.sum(dc * u, axis=0, keepdims=True)
        acc_ref[2:3, :] += jnp.sum(dc * u_next, axis=0, keepdims=True)
        acc_ref[3:4, :] += jnp.sum(dc, axis=0, keepdims=True)
        du = cw_ref[0:1, :] * dc_next + cw_ref[1:2, :] * dc + cw_ref[2:3, :] * dc_prev
        dgc_ref[...] = (du * xi).astype(BF16)
        dxi_ref[...] = (du * gc).astype(BF16)

    per = TM // CH
    last = t // CH - 1
    prev_i = lambda i: jnp.maximum(i * per - 1, 0)
    next_i = lambda i: jnp.minimum(i * per + per, last)
    if halo:
        in_specs = [_row_spec(D), pl.BlockSpec((CH, D), lambda i: (prev_i(i), 0)), pl.BlockSpec((CH, D), lambda i: (next_i(i), 0)),
                    _col_spec(1), _col_spec(2),
                    pl.BlockSpec((CH, D), lambda i: (prev_i(i), 1)), pl.BlockSpec((CH, D), lambda i: (prev_i(i), 2)),
                    pl.BlockSpec((CH, D), lambda i: (next_i(i), 1)), pl.BlockSpec((CH, D), lambda i: (next_i(i), 2)),
                    _full_spec((8, D))]
        args = [dconv, dconv, dconv, p, p, p, p, p, p, cw8]
    else:
        in_specs = [_row_spec(D), _col_spec(1), _col_spec(2), _full_spec((8, D))]
        args = [dconv, p, p, cw8]
    return pl.pallas_call(
        body, name=name, grid=(t // TM,), in_specs=in_specs,
        out_specs=[_row_spec(D), _row_spec(D), _full_spec((8, D))],
        out_shape=[jax.ShapeDtypeStruct((t, D), BF16), jax.ShapeDtypeStruct((t, D), BF16), jax.ShapeDtypeStruct((8, D), F32)],
        compiler_params=_cp(1),
    )(*args)


LOG2E = 1.4426950408889634


def _cumsum_matrix(reverse):
    r = lax.broadcasted_iota(jnp.int32, (CH, CH), 0)
    c = lax.broadcasted_iota(jnp.int32, (CH, CH), 1)
    return (r <= c if reverse else r >= c).astype(BF16)


def _chunk_cumsum(g, tri):
    hi = g.astype(BF16)
    lo = (g - hi.astype(F32)).astype(BF16)
    return _dot(tri, hi) + _dot(tri, lo)


def _gate_values(z, lb):
    sig = _sigmoid(z)
    f = lb + (1.0 - lb) * sig
    return sig, f


def _tri(direction, transposed):
    r = lax.broadcasted_iota(jnp.int32, (CH, CH), 0)
    c = lax.broadcasted_iota(jnp.int32, (CH, CH), 1)
    lower = (direction == 0) != transposed
    return r >= c if lower else r <= c


def _gla_rows(direction):
    return (CH // 2 - 1, CH - 1) if direction == 0 else (CH // 2, 0)


def _fwd_tile(direction, nt):
    return (lambda i: (i + nt - 1) % nt) if direction == 0 else (lambda i: nt - 1 - i)


def gla_fwd(p, lb2, name):
    t = p.shape[0]
    nt = t // TM
    per = TM // CH

    def body(z0_ref, v0_ref, q0_ref, z1_ref, v1_ref, q1_ref, lb_ref, o0_ref, s0_ref, o1_ref, s1_ref, st, q_s, k_s, c_s):
        @pl.when(pl.program_id(0) == 0)
        def _():
            st[...] = jnp.zeros_like(st)

        ins = ((z0_ref, v0_ref, q0_ref, o0_ref, s0_ref), (z1_ref, v1_ref, q1_ref, o1_ref, s1_ref))
        for d in range(2):
            z_ref, _, qr_ref, _, _ = ins[d]
            tri = _cumsum_matrix(d == 1)
            lb = lb_ref[d:d + 1, :]
            for ci in range(per):
                rows = slice(ci * CH, (ci + 1) * CH)
                _, f = _gate_values(z_ref[rows, :], lb)
                k_s[d, rows, :] = 1.0 - f
                c_s[d, rows, :] = _chunk_cumsum(jnp.log(f) * LOG2E, tri)
                qr = qr_ref[rows, :]
                q_s[d, rows, :] = qr * _sigmoid(qr)
        masks = (_tri(0, False), _tri(1, False))
        state = [[st[d, h] for h in range(NH)] for d in range(2)]
        for it in range(per):
            chunk = []
            for d in range(2):
                ref_row, last_row = _gla_rows(d)
                ci = it if d == 0 else per - 1 - it
                r0 = ci * CH
                rows = slice(r0, r0 + CH)
                cum = c_s[d, rows, :]
                ref = c_s[d, r0 + ref_row:r0 + ref_row + 1, :]
                last = c_s[d, r0 + last_row:r0 + last_row + 1, :]
                q = q_s[d, rows, :]
                k = k_s[d, rows, :]
                chunk.append(dict(
                    ci=ci, rows=rows, qh=(q * jnp.exp2(cum)).astype(BF16), qt=(q * jnp.exp2(cum - ref)).astype(BF16),
                    kt=(k * jnp.exp2(ref - cum)).astype(BF16), kb=(k * jnp.exp2(last - cum)).astype(BF16),
                    el=jnp.exp2(last), vb=ins[d][1][rows, :].astype(BF16)))
            for h in range(NH):
                sl = slice(h * HD, (h + 1) * HD)
                for d in range(2):
                    c = chunk[d]
                    o_ref, s_ref = ins[d][3], ins[d][4]
                    s_t = state[d][h]
                    s_ref[c["ci"], h] = s_t
                    sc = jnp.where(masks[d], _dot_nt(c["qt"][:, sl], c["kt"][:, sl]), 0.0)
                    o_ref[c["rows"], sl] = (_dot_nt(c["qh"][:, sl], s_t.astype(BF16))
                                            + _dot(sc.astype(BF16), c["vb"][:, sl])).astype(BF16)
                    state[d][h] = s_t * c["el"][:, sl] + _dot_tn(c["vb"][:, sl], c["kb"][:, sl])
        for d in range(2):
            for h in range(NH):
                st[d, h] = state[d][h]

    tiles = (_fwd_tile(0, nt), _fwd_tile(1, nt))
    tspec = lambda d, col: pl.BlockSpec((TM, D), lambda i: (tiles[d](i), col))
    sspec = lambda d: pl.BlockSpec((per, NH, HD, HD), lambda i: (tiles[d](i), 0, 0, 0))
    o_shape = jax.ShapeDtypeStruct((t, D), BF16)
    s_shape = jax.ShapeDtypeStruct((t // CH, NH, HD, HD), F32)
    return pl.pallas_call(
        body, name=name, grid=(nt,),
        in_specs=[tspec(0, 0), tspec(0, 2), tspec(0, 3), tspec(1, 1), tspec(1, 2), tspec(1, 3), _full_spec((2, D))],
        out_specs=[tspec(0, 0), sspec(0), tspec(1, 0), sspec(1)],
        out_shape=[o_shape, s_shape, o_shape, s_shape],
        scratch_shapes=[pltpu.VMEM((2, NH, HD, HD), F32)] + [pltpu.VMEM((2, TM, D), F32)] * 3,
        compiler_params=_cp(1),
    )(p, p, p, p, p, p, lb2)


def gla_bwd(p, lb2, do, states, direction, prev, name):
    t = p.shape[0]
    nt = t // TM
    per = TM // CH
    ref_row, last_row = _gla_rows(direction)
    tile = (lambda i: (2 * nt - 2 - i) % nt) if direction == 0 else (lambda i: i)
    final = prev is not None
    n_in = 8 if final else 6

    def body(*refs):
        z_ref, v_ref, qr_ref, lb_ref, do_ref, s_ref = refs[:6]
        dz_ref, dv_ref, dq_ref, acc_ref, dst, q_s, k_s, c_s, dq_s, dk_s, dl_s = refs[n_in:]

        @pl.when(pl.program_id(0) == 0)
        def _():
            dst[...] = jnp.zeros_like(dst)
            acc_ref[...] = jnp.zeros_like(acc_ref)

        mask = _tri(direction, False)
        mask_t = _tri(direction, True)
        tri = _cumsum_matrix(direction == 1)
        tri_t = _cumsum_matrix(direction == 0)
        is_last = lax.broadcasted_iota(jnp.int32, (CH, 1), 0) == last_row
        lb = lb_ref[direction:direction + 1, :]
        for ci in range(per):
            rows = slice(ci * CH, (ci + 1) * CH)
            _, f = _gate_values(z_ref[rows, :], lb)
            k_s[rows, :] = 1.0 - f
            c_s[rows, :] = _chunk_cumsum(jnp.log(f) * LOG2E, tri)
            qr = qr_ref[rows, :]
            q_s[rows, :] = qr * _sigmoid(qr)
        for it in range(per):
            ci = per - 1 - it if direction == 0 else it
            r0 = ci * CH
            rows = slice(r0, r0 + CH)
            cum = c_s[rows, :]
            ref = c_s[r0 + ref_row:r0 + ref_row + 1, :]
            last = c_s[r0 + last_row:r0 + last_row + 1, :]
            q = q_s[rows, :]
            k = k_s[rows, :]
            e_h = jnp.exp2(cum)
            e_t = jnp.exp2(cum - ref)
            e_kt = jnp.exp2(ref - cum)
            e_kb = jnp.exp2(last - cum)
            el = jnp.exp2(last)
            qh = (q * e_h).astype(BF16)
            qt = (q * e_t).astype(BF16)
            kt = (k * e_kt).astype(BF16)
            kbf = k * e_kb
            kb = kbf.astype(BF16)
            vb = v_ref[rows, :].astype(BF16)
            dob = do_ref[rows, :].astype(BF16)
            for h in range(NH):
                sl = slice(h * HD, (h + 1) * HD)
                s_t = s_ref[ci, h]
                ds_t = dst[h]
                ds_b = ds_t.astype(BF16)
                d_a = jnp.where(mask, _dot_nt(dob[:, sl], vb[:, sl]), 0.0).astype(BF16)
                a_t = jnp.where(mask_t, _dot_nt(kt[:, sl], qt[:, sl]), 0.0).astype(BF16)
                d_at = jnp.where(mask_t, _dot_nt(vb[:, sl], dob[:, sl]), 0.0).astype(BF16)
                dv = _dot(a_t, dob[:, sl]) + _dot_nt(kb[:, sl], ds_b)
                dkb = _dot(vb[:, sl], ds_b)
                dl_s[it:it + 1, sl] = (el[:, sl] * jnp.sum(ds_t * s_t, axis=0, keepdims=True)
                                       + jnp.sum(dkb * kbf[:, sl], axis=0, keepdims=True))
                dst[h] = ds_t * el[:, sl] + _dot_tn(dob[:, sl], qh[:, sl])
                dq_s[rows, sl] = _dot(dob[:, sl], s_t.astype(BF16)) * e_h[:, sl] + _dot(d_a, kt[:, sl]) * e_t[:, sl]
                dk_s[rows, sl] = _dot(d_at, qt[:, sl]) * e_kt[:, sl] + dkb * e_kb[:, sl]
                if final:
                    dv_ref[rows, sl] = (refs[6][rows, sl] + dv).astype(BF16)
                else:
                    dv_ref[rows, sl] = dv
        for it in range(per):
            ci = per - 1 - it if direction == 0 else it
            rows = slice(ci * CH, (ci + 1) * CH)
            dq = dq_s[rows, :]
            dk = dk_s[rows, :]
            dg = _chunk_cumsum(dq * q_s[rows, :] - dk * k_s[rows, :] + jnp.where(is_last, dl_s[it:it + 1, :], 0.0), tri_t)
            sig, f = _gate_values(z_ref[rows, :], lb)
            df = dg / f - dk
            acc_ref[0:1, :] += jnp.sum(df * (1.0 - sig), axis=0, keepdims=True)
            dz_ref[rows, :] = (df * (1.0 - lb) * sig * (1.0 - sig)).astype(BF16)
            if final:
                qr = qr_ref[rows, :]
                sq = _sigmoid(qr)
                dq_ref[rows, :] = ((refs[7][rows, :] + dq) * (sq * (1.0 + qr * (1.0 - sq)))).astype(BF16)
            else:
                dq_ref[rows, :] = dq

    tspec = lambda col: pl.BlockSpec((TM, D), lambda i: (tile(i), col))
    sspec = pl.BlockSpec((per, NH, HD, HD), lambda i: (tile(i), 0, 0, 0))
    in_specs = [tspec(direction), tspec(2), tspec(3), _full_spec((2, D)), tspec(0), sspec]
    args = [p, p, p, lb2, do, states]
    if final:
        in_specs += [tspec(0), tspec(0)]
        args += list(prev)
    odt = BF16 if final else F32
    return pl.pallas_call(
        body, name=name, grid=(nt,), in_specs=in_specs,
        out_specs=[tspec(0), tspec(0), tspec(0), _full_spec((8, D))],
        out_shape=[jax.ShapeDtypeStruct((t, D), BF16), jax.ShapeDtypeStruct((t, D), odt), jax.ShapeDtypeStruct((t, D), odt),
                   jax.ShapeDtypeStruct((8, D), F32)],
        scratch_shapes=[pltpu.VMEM((NH, HD, HD), F32)] + [pltpu.VMEM((TM, D), F32)] * 5 + [pltpu.VMEM((8, D), F32)],
        compiler_params=_cp(1),
    )(*args)


def loss_bwd(x, gain, target, n_lat, name):
    t = x.shape[0]

    def body(x_ref, gain_ref, tg_ref, dx_ref, acc_ref):
        i = pl.program_id(0)

        @pl.when(i == 0)
        def _():
            acc_ref[...] = jnp.zeros_like(acc_ref)

        latf = (i < n_lat).astype(F32)
        x = x_ref[...]
        gain = gain_ref[...]
        r = lax.rsqrt(jnp.mean(x * x, axis=-1, keepdims=True) + EPS)
        xn = x * r
        err = (xn * gain - tg_ref[...]) * latf
        dy = err * (1.0 / D)
        dxn = dy * gain
        dx_ref[...] = r * (dxn - xn * jnp.mean(dxn * xn, axis=-1, keepdims=True))
        acc_ref[0:1, :] += jnp.sum(dy * xn, axis=0, keepdims=True)
        acc_ref[1:2, :] += jnp.sum(err * err, axis=0, keepdims=True)

    return pl.pallas_call(
        body, name=name, grid=(t // TM,),
        in_specs=[_row_spec(D), _full_spec((1, D)), pl.BlockSpec((TM, D), lambda i: (jnp.minimum(i, n_lat - 1), 0))],
        out_specs=[_row_spec(D), _full_spec((8, D))],
        out_shape=[jax.ShapeDtypeStruct((t, D), F32), jax.ShapeDtypeStruct((8, D), F32)],
        compiler_params=_cp(1),
    )(x, gain, target)


def local_step(xs, target, mods, norm1, norm2, norm_f, lbs, gnorm, cw8, cb, wts, n_lat, on_grads, after_backward):
    t = xs.shape[0]
    saved = []
    cache = {}

    def W(name, idx, after=None):
        if (name, idx) not in cache:
            cache[(name, idx)] = wts(name, idx, after)
        return cache[(name, idx)]

    x = xs
    for i in range(DEPTH):
        j = i // 2
        rec = i % 2 == 0
        n1 = norm1[i:i + 1]
        n2 = norm2[i:i + 1]
        s = {"x_in": x}
        if rec:
            p = proj_fwd(x, n1, mods[i], 0, W("hin", j, x), n_lat, F32, f"hin_fwd_{i}")
            o0, st0, o1, st1 = gla_fwd(p, lbs[j], f"gla_fwd_{i}")
            ex = [o0, o1, p, gnorm[j:j + 1]]
            ex_specs = [_row_spec(D), _row_spec(D), _col_spec(4), _full_spec((1, D))]
            xm, y, ab = outproj_fwd(readout_prologue, ex, ex_specs, W("hout", j, o1), x, mods[i], 0, n_lat, f"hout_fwd_{i}")
            s.update(st0=st0, st1=st1)
        else:
            sft = 1 if j % 2 == 0 else CH
            p = proj_fwd(x, n1, mods[i], 0, W("cin", j, x), n_lat, BF16, f"cin_fwd_{i}")
            ex = _conv_args(sft, p, cw8[j], cb[j])
            ex_specs = _conv_specs(sft, t)
            xm, y, ab = outproj_fwd(make_conv_prologue(n_lat, sft), ex, ex_specs, W("cout", j, p), x, mods[i], 0, n_lat, f"cout_fwd_{i}")
        s.update(p=p, ex=ex, ex_specs=ex_specs, y_mix=y, ab_mix=ab, x_mid=xm)
        x, y2, ab2 = mlp_fwd(xm, n2, mods[i], W("w1", i, xm), W("w2", i, xm), n_lat, f"mlp_fwd_{i}")
        s.update(y_mlp=y2, ab_mlp=ab2)
        saved.append(s)

    dx, acc_loss = loss_bwd(x, norm_f, target, n_lat, "loss")
    small = {"norm_f": acc_loss[0:1], "norm1": [None] * DEPTH, "norm2": [None] * DEPTH, "dmod": [None] * DEPTH,
             "gnorm": [None] * 2, "lb": [None] * 2, "cw": [None] * 2, "cb": [None] * 2}
    bshape = lambda w: jax.ShapeDtypeStruct((t, w), BF16)
    token = jnp.zeros((8, 128), F32)
    for i in reversed(range(DEPTH)):
        j = i // 2
        rec = i % 2 == 0
        s = saved[i]
        n1 = norm1[i:i + 1]
        n2 = norm2[i:i + 1]
        dx, dyb, dp1, hb, acc_n2 = mlp_bwd(dx, s["y_mlp"], s["ab_mlp"], s["x_mid"], n2, mods[i], W("w1", i), W("w2", i), n_lat, token,
                                           f"mlp_bwd_{i}")
        token = on_grads(i, "mlp", {"w2": dw_tn(s["ab_mlp"], dyb, 4, True, True, token, f"w2_dw_{i}"),
                                    "w1": dw_tn(hb, dp1, 4, False, False, token, f"w1_dw_{i}")})
        if rec:
            dyb, acc_g1, (do, dgate) = outproj_bwd(
                readout_epilogue, s["ex"], s["ex_specs"], [_row_spec(D), _row_spec(D)],
                [jax.ShapeDtypeStruct((t, D), F32), bshape(D)], W("hout", j), dx, s["y_mix"], mods[i], 0, n_lat, token, f"hout_bwd_{i}")
            dz0, dv0, dq0, acc_l0 = gla_bwd(s["p"], lbs[j], do, s["st0"], 0, None, f"gla_bwd0_{i}")
            dz1, dv, dq, acc_l1 = gla_bwd(s["p"], lbs[j], do, s["st1"], 1, (dv0, dq0), f"gla_bwd1_{i}")
            dx, hb, dpb, acc_n1 = proj_bwd([dz0, dz1, dv, dq, dgate], W("hin", j), s["x_in"], n1, mods[i], 0, dx, n_lat, i == 0,
                                           f"hin_bwd_{i}")
            small["gnorm"][j] = acc_g1[2:3]
            small["lb"][j] = jnp.concatenate([acc_l0[0:1], acc_l1[0:1]], axis=0)
            mix = ("hout", "hin")
        else:
            sft = 1 if j % 2 == 0 else CH
            dyb, acc_g1, (dconv, dgb) = outproj_bwd(
                make_conv_epilogue(n_lat, sft), s["ex"], s["ex_specs"], [_row_spec(D), _row_spec(D)],
                [bshape(D), bshape(D)], W("cout", j), dx, s["y_mix"], mods[i], 0, n_lat, token, f"cout_bwd_{i}")
            dgc, dxi, acc_c = conv_bwd(dconv, s["p"], cw8[j], sft, n_lat, f"conv_bwd_{i}")
            dx, hb, dpb, acc_n1 = proj_bwd([dgb, dgc, dxi], W("cin", j), s["x_in"], n1, mods[i], 0, dx, n_lat, False, f"cin_bwd_{i}")
            small["cw"][j] = acc_c[0:3]
            small["cb"][j] = acc_c[3:4]
            mix = ("cout", "cin")
        small["norm1"][i] = acc_n1[0:1]
        small["norm2"][i] = acc_n2[0:1]
        z2 = jnp.zeros((2, D), F32)
        small["dmod"][i] = jnp.concatenate([acc_n1[1:3], acc_g1[0:1], acc_n2[1:3], acc_n2[5:6], z2,
                                            acc_n1[3:5], acc_g1[1:2], acc_n2[3:5], acc_n2[6:7], z2], axis=0)
        if i == 0:
            token = after_backward(small, token)
        token = on_grads(i, "mix", {mix[0]: dw_tn(s["ab_mix"], dyb, 1, False, False, token, f"{mix[0]}_dw_{i}"),
                                    mix[1]: dw_tn(hb, dpb, 4, False, False, token, f"{mix[1]}_dw_{i}")})
    return acc_loss[1:2], dx, token


RB = 256


def cast_to_slot(w2d, layer, k, chip1, name):
    c = w2d.shape[1]
    nblk = k // RB

    def body(chip_ref, w_ref, o_ref):
        o_ref[0] = w_ref[...].astype(BF16)

    return pl.pallas_call(
        body, name=name,
        grid_spec=pltpu.PrefetchScalarGridSpec(
            num_scalar_prefetch=1, grid=(nblk,),
            in_specs=[pl.BlockSpec((RB, c), lambda i, ch: (layer * nblk + i, 0))],
            out_specs=pl.BlockSpec((1, RB, c), lambda i, ch: (ch[0], i, 0))),
        out_shape=jax.ShapeDtypeStruct((4, k, c), BF16), compiler_params=_cp(1))(chip1, w2d)


def sum_slots(own, land, acc, layer, chip1, name):
    _, k, c = own.shape
    nblk = k // RB

    def body(chip_ref, own_ref, l1_ref, l2_ref, l3_ref, acc_ref, o_ref):
        o_ref[...] = ((own_ref[0].astype(F32) + l1_ref[0].astype(F32)) + l2_ref[0].astype(F32)) + l3_ref[0].astype(F32)

    slot = lambda d: pl.BlockSpec((1, RB, c), lambda i, ch: ((ch[0] + d) % 4, i, 0))
    return pl.pallas_call(
        body, name=name,
        grid_spec=pltpu.PrefetchScalarGridSpec(
            num_scalar_prefetch=1, grid=(nblk,),
            in_specs=[slot(0), slot(1), slot(2), slot(3), ANY],
            out_specs=pl.BlockSpec((RB, c), lambda i, ch: (layer * nblk + i, 0))),
        out_shape=jax.ShapeDtypeStruct(acc.shape, F32), input_output_aliases={5: 0}, compiler_params=_cp(1),
    )(chip1, own, land, land, land, acc)


def _adamw_math(w, g, m, v):
    m = ADAM_B1 * m + (1.0 - ADAM_B1) * g
    v = ADAM_B2 * v + (1.0 - ADAM_B2) * jnp.square(g)
    m_hat = m / (1.0 - ADAM_B1 ** ADAM_STEP)
    v_hat = v / (1.0 - ADAM_B2 ** ADAM_STEP)
    delta = -ADAM_LR * (m_hat / (jnp.sqrt(v_hat) + ADAM_EPS) + ADAM_WD * w)
    return delta, m, v


def adamw(gsrcs, w, m, v, name):
    r, c = w.shape
    rb = RB if r % RB == 0 else r
    n_g = len(gsrcs)

    def body(*refs):
        g = refs[0][...]
        for k in range(1, n_g):
            g = g + refs[k][...]
        w_ref, m_ref, v_ref, g_ref, d_ref, mo_ref, vo_ref = refs[n_g:]
        delta, mo, vo = _adamw_math(w_ref[...], g, m_ref[...], v_ref[...])
        g_ref[...] = g
        d_ref[...] = delta
        mo_ref[...] = mo
        vo_ref[...] = vo

    spec = pl.BlockSpec((rb, c), lambda i: (i, 0))
    shp = jax.ShapeDtypeStruct((r, c), F32)
    return pl.pallas_call(body, name=name, grid=(r // rb,), in_specs=[spec] * (n_g + 3), out_specs=[spec] * 4,
                          out_shape=[shp] * 4, compiler_params=_cp(1))(*gsrcs, w, m, v)


ADA_CB = 512


def ada_fwd(cvec, ada_w, bias, name):
    _, _, nc = ada_w.shape

    def body(c_ref, w_ref, b_ref, o_ref):
        cv = c_ref[...]
        a = (cv * _sigmoid(cv)).astype(BF16)
        o_ref[0] = _dot(a, w_ref[0].astype(BF16)) + b_ref[0]

    return pl.pallas_call(
        body, name=name, grid=(DEPTH, nc // ADA_CB),
        in_specs=[pl.BlockSpec((16, D), lambda i, j: (0, 0)), pl.BlockSpec((1, D, ADA_CB), lambda i, j: (i, 0, j)),
                  pl.BlockSpec((1, 1, ADA_CB), lambda i, j: (i, 0, j))],
        out_specs=pl.BlockSpec((1, 16, ADA_CB), lambda i, j: (i, 0, j)),
        out_shape=jax.ShapeDtypeStruct((DEPTH, 16, nc), F32), compiler_params=_cp(2),
    )(cvec, ada_w, bias)


def ada_bwd(cvec, dcols, ada_w, m, v, name):
    _, _, nc = ada_w.shape

    def body(c_ref, d_ref, w_ref, m_ref, v_ref, g_ref, dl_ref, mo_ref, vo_ref, acc_ref):
        @pl.when(jnp.logical_and(pl.program_id(0) == 0, pl.program_id(1) == 0))
        def _():
            acc_ref[...] = jnp.zeros_like(acc_ref)

        cv = c_ref[...]
        a = (cv * _sigmoid(cv)).astype(BF16)
        db = d_ref[0].astype(BF16)
        w = w_ref[0]
        g = _dot_tn(a, db)
        delta, mo, vo = _adamw_math(w, g, m_ref[0], v_ref[0])
        g_ref[0] = g
        dl_ref[0] = delta
        mo_ref[0] = mo
        vo_ref[0] = vo
        acc_ref[...] += _dot_nt(db[8:16, :], w.astype(BF16))

    wspec = pl.BlockSpec((1, D, ADA_CB), lambda i, j: (i, 0, j))
    wshape = jax.ShapeDtypeStruct(ada_w.shape, F32)
    return pl.pallas_call(
        body, name=name, grid=(DEPTH, nc // ADA_CB),
        in_specs=[pl.BlockSpec((16, D), lambda i, j: (0, 0)), pl.BlockSpec((1, 16, ADA_CB), lambda i, j: (i, 0, j)), wspec, wspec, wspec],
        out_specs=[wspec, wspec, wspec, wspec, pl.BlockSpec((8, D), lambda i, j: (0, 0))],
        out_shape=[wshape, wshape, wshape, wshape, jax.ShapeDtypeStruct((8, D), F32)], compiler_params=_cp(2),
    )(cvec, dcols, ada_w, m, v)


def _place():
    return lax.axis_index("x"), lax.axis_index("y"), lax.axis_index("c")


ANY = pl.BlockSpec(memory_space=pl.ANY)
VMEM_SPEC = pl.BlockSpec(memory_space=pltpu.VMEM)


def small_allgather(buf, deps, name):
    r, c = buf.shape
    n_dep = len(deps)

    def body(*refs):
        in_ref = refs[0]
        out_ref, send_sems, recv_sems, loc_sem = refs[1 + n_dep:]
        x, y, cc = _place()
        me = 4 * x + 2 * y + cc
        loc = pltpu.make_async_copy(in_ref, out_ref.at[me], loc_sem)
        loc.start()
        peers = []
        for k in range(1, 8):
            px = 1 - x if k & 4 else x
            py = 1 - y if k & 2 else y
            pc = 1 - cc if k & 1 else cc
            peers.append((px, py, pc))
        sends = []
        for k, peer in enumerate(peers):
            cp = pltpu.make_async_remote_copy(src_ref=in_ref, dst_ref=out_ref.at[me], send_sem=send_sems.at[k],
                                              recv_sem=recv_sems.at[k], device_id=peer, device_id_type=MESH)
            cp.start()
            sends.append(cp)
        for k, (px, py, pc) in enumerate(peers):
            pltpu.make_async_remote_copy(src_ref=in_ref, dst_ref=out_ref.at[4 * px + 2 * py + pc], send_sem=send_sems.at[k],
                                         recv_sem=recv_sems.at[k], device_id=(px, py, pc), device_id_type=MESH).wait_recv()
        for cp in sends:
            cp.wait_send()
        loc.wait()

    return pl.pallas_call(
        body, name=name, in_specs=[VMEM_SPEC] + [ANY] * n_dep, out_specs=VMEM_SPEC,
        out_shape=jax.ShapeDtypeStruct((8, r, c), buf.dtype),
        scratch_shapes=[pltpu.SemaphoreType.DMA((7,)), pltpu.SemaphoreType.DMA((7,)), pltpu.SemaphoreType.DMA],
    )(buf, *deps)


def _chip_peers(x, y):
    return [(1 - x, y), (x, 1 - y), (1 - x, 1 - y)]


HBM_SPEC = pl.BlockSpec(memory_space=pltpu.HBM)
SEM_SPEC = pl.BlockSpec(memory_space=pltpu.SEMAPHORE)
EFFECT = pltpu.SideEffectType.DATAFLOW_SIDE_EFFECTING


def _hbm(a):
    return pltpu.with_memory_space_constraint(a, pltpu.HBM)


def _split_copy(u, p, peer, dst_slot, chip, land_refs, src_refs, sem_refs, cc):
    px, py = peer
    src = land_refs[u].at[chip] if src_refs is None else src_refs[u].at[2 * px + py]
    return pltpu.make_async_remote_copy(src_ref=src, dst_ref=land_refs[u].at[dst_slot], send_sem=sem_refs[2 * u].at[p],
                                        recv_sem=sem_refs[2 * u + 1].at[p], device_id=(px, py, cc), device_id_type=MESH)


def split_start(lands, srcs, after, name):
    n = len(lands)
    ops = list(lands) + (list(srcs) if srcs is not None else [])
    n_ops = len(ops)

    def body(*refs):
        land_refs = refs[:n]
        src_refs = refs[n:n_ops] if srcs is not None else None
        sem_refs = refs[n_ops + 1:n_ops + 1 + 2 * n]
        x, y, cc = _place()
        chip = 2 * x + y
        for u in range(n):
            for p, peer in enumerate(_chip_peers(x, y)):
                _split_copy(u, p, peer, chip, chip, land_refs, src_refs, sem_refs, cc).start()
        refs[-1][...] = jnp.zeros((8, 128), F32)

    outs = pl.pallas_call(
        body, name=name, in_specs=[HBM_SPEC] * n_ops + [ANY],
        out_specs=[SEM_SPEC] * (2 * n) + [HBM_SPEC] * n_ops + [VMEM_SPEC],
        out_shape=[pltpu.SemaphoreType.DMA((3,))] * (2 * n) + [pltpu.HBM(a.shape, a.dtype) for a in ops]
        + [jax.ShapeDtypeStruct((8, 128), F32)],
        input_output_aliases={k: 2 * n + k for k in range(n_ops)},
        compiler_params=pltpu.CompilerParams(has_side_effects=EFFECT),
    )(*[_hbm(a) for a in ops], after)
    sems = list(outs[:2 * n])
    thru = list(outs[2 * n:2 * n + n_ops])
    return sems, thru[:n], thru[n:], outs[-1]


def split_wait(lands, srcs, sems, after, name):
    n = len(lands)
    ops = list(lands) + (list(srcs) if srcs is not None else [])
    n_ops = len(ops)

    def body(*refs):
        land_refs = refs[:n]
        src_refs = refs[n:n_ops] if srcs is not None else None
        sem_refs = refs[n_ops:n_ops + 2 * n]
        x, y, cc = _place()
        chip = 2 * x + y
        for u in range(n):
            for p, peer in enumerate(_chip_peers(x, y)):
                cp = _split_copy(u, p, peer, 2 * peer[0] + peer[1], chip, land_refs, src_refs, sem_refs, cc)
                cp.wait_send()
                cp.wait_recv()

    outs = pl.pallas_call(
        body, name=name, in_specs=[HBM_SPEC] * n_ops + [SEM_SPEC] * (2 * n) + [ANY],
        out_specs=[HBM_SPEC] * n_ops, out_shape=[pltpu.HBM(a.shape, a.dtype) for a in ops],
        input_output_aliases={k: k for k in range(n_ops)},
        compiler_params=pltpu.CompilerParams(has_side_effects=EFFECT),
    )(*ops, *sems, after)
    return list(outs[:n]), list(outs[n:])


def _sibling_copy(k, src_refs, zone_refs, sem_refs):
    x, y, cc = _place()
    return pltpu.make_async_remote_copy(src_ref=src_refs[k], dst_ref=zone_refs[k], send_sem=sem_refs[2 * k], recv_sem=sem_refs[2 * k + 1],
                                        device_id=(x, y, 1 - cc), device_id_type=MESH)


def sibling_start(parts, name):
    n = len(parts)
    ops = list(parts) + [lax.empty(p.shape, p.dtype) for p in parts]

    def body(*refs):
        for k in range(n):
            _sibling_copy(k, refs[:n], refs[n:2 * n], refs[2 * n:4 * n]).start()

    outs = pl.pallas_call(
        body, name=name, in_specs=[HBM_SPEC] * (2 * n),
        out_specs=[SEM_SPEC] * (2 * n) + [HBM_SPEC] * (2 * n),
        out_shape=[pltpu.SemaphoreType.DMA(())] * (2 * n) + [pltpu.HBM(a.shape, a.dtype) for a in ops],
        input_output_aliases={k: 2 * n + k for k in range(2 * n)},
        compiler_params=pltpu.CompilerParams(has_side_effects=EFFECT),
    )(*[_hbm(a) for a in ops])
    return list(outs[2 * n:3 * n]), list(outs[3 * n:]), list(outs[:2 * n])


def sibling_wait(parts, zones, sems, after, name):
    n = len(parts)

    def body(*refs):
        for k in range(n):
            cp = _sibling_copy(k, refs[:n], refs[n:2 * n], refs[2 * n:4 * n])
            cp.wait_send()
            cp.wait_recv()

    outs = pl.pallas_call(
        body, name=name, in_specs=[HBM_SPEC] * (2 * n) + [SEM_SPEC] * (2 * n) + [ANY],
        out_specs=[HBM_SPEC] * (2 * n), out_shape=[pltpu.HBM(a.shape, a.dtype) for a in list(parts) + list(zones)],
        input_output_aliases={k: k for k in range(2 * n)},
        compiler_params=pltpu.CompilerParams(has_side_effects=EFFECT),
    )(*parts, *zones, *sems, after)
    return list(outs[:n]), list(outs[n:])


SMALL_ROWS = 88
FIN_ROWS = 72


def small_finish(g3, g4, c_ctx, lbp, name):
    def body(g3_ref, g4_ref, cc_ref, lbp_ref, o_ref, s_ref):
        s = g3_ref[0]
        for k in range(1, 8):
            s = s + g3_ref[k]
        s_ref[...] = s
        for i in range(DEPTH):
            o_ref[8 * i:8 * i + 8, :] = s_ref[16 * i:16 * i + 8, :] + s_ref[16 * i + 8:16 * i + 16, :]
        acc = g4_ref[0]
        for k in (2, 4, 6):
            acc = acc + g4_ref[k]
        cc = cc_ref[...]
        sg = _sigmoid(cc)
        row = jnp.sum(acc, axis=0, keepdims=True) * (sg * (1.0 + cc * (1.0 - sg)))
        o_ref[32:40, :] = jnp.broadcast_to(row, (8, D))
        o_ref[40:64, :] = s_ref[64:88, :]
        o_ref[64:72, :] = jnp.zeros((8, D), F32)
        for d in range(2):
            pp = lbp_ref[2 * d:2 * d + 1, :] * lbp_ref[2 * d + 1:2 * d + 2, :] * s_ref[75 + d:76 + d, :]
            o_ref[64 + 2 * d:65 + 2 * d, :] = -pp
            o_ref[65 + 2 * d:66 + 2 * d, :] = pp

    return pl.pallas_call(
        body, name=name, in_specs=[VMEM_SPEC] * 4, out_specs=VMEM_SPEC,
        out_shape=jax.ShapeDtypeStruct((FIN_ROWS, D), F32),
        scratch_shapes=[pltpu.VMEM((SMALL_ROWS, D), F32)],
    )(g3, g4, c_ctx, lbp)


def _pack_rows(arrs):
    flat = jnp.concatenate([a.reshape(-1) for a in arrs])
    n = -(-flat.shape[0] // (8 * D)) * 8 * D
    return jnp.pad(flat, (0, n - flat.shape[0])).reshape(n // D, D)


def _unpack_rows(packed, shapes):
    flat = packed.reshape(-1)
    outs, off = [], 0
    for s in shapes:
        size = 1
        for k in s:
            size *= k
        outs.append(flat[off:off + size].reshape(s))
        off += size
    return outs


def _pad8(a):
    return jnp.pad(a, ((0, 8 - a.shape[0]), (0, 0)))


def kernel(x, c, ctx, c_ctx, ada_w, ada_b, norm1, norm2, norm_f, mlp_w1, mlp_w2, hgrn_w_in, hgrn_lb, hgrn_gnorm, hgrn_w_out, conv_w_in, conv_w, conv_b, conv_w_out, loss_target, m_c_ctx, m_ada_w, m_ada_b, m_norm1, m_norm2, m_norm_f, m_mlp_w1, m_mlp_w2, m_hgrn_w_in, m_hgrn_lb, m_hgrn_gnorm, m_hgrn_w_out, m_conv_w_in, m_conv_w, m_conv_b, m_conv_w_out, v_c_ctx, v_ada_w, v_ada_b, v_norm1, v_norm2, v_norm_f, v_mlp_w1, v_mlp_w2, v_hgrn_w_in, v_hgrn_lb, v_hgrn_gnorm, v_hgrn_w_out, v_conv_w_in, v_conv_w, v_conv_b, v_conv_w_out):
    xi, yi, ci = _place()
    me = 4 * xi + 2 * yi + ci
    chip = 2 * xi + yi
    seq = x.shape[1]
    assert ctx.shape[1] == TM and seq % TM == 0 and (seq + TM) % TMW == 0
    n_lat = seq // TM
    sd = D // 4
    nca = ada_w.shape[2]
    xs = jnp.concatenate([x[0], ctx[0]], axis=0)

    big = [(mlp_w1, m_mlp_w1, v_mlp_w1), (mlp_w2, m_mlp_w2, v_mlp_w2), (hgrn_w_in, m_hgrn_w_in, v_hgrn_w_in),
           (hgrn_w_out, m_hgrn_w_out, v_hgrn_w_out), (conv_w_in, m_conv_w_in, v_conv_w_in), (conv_w_out, m_conv_w_out, v_conv_w_out)]
    big_names = ["w1", "w2", "hin", "hout", "cin", "cout"]
    flat2 = lambda a: a.reshape(a.shape[0] * a.shape[1], a.shape[2])
    tensors = dict(zip(big_names, big))
    chip1 = jnp.reshape(chip, (1,)).astype(jnp.int32)
    order = []
    for i in range(DEPTH):
        order += [("hin", i // 2), ("hout", i // 2)] if i % 2 == 0 else [("cin", i // 2), ("cout", i // 2)]
        order += [("w1", i), ("w2", i)]
    lands = [cast_to_slot(flat2(tensors[n][0]), idx, tensors[n][0].shape[1], chip1, f"cast_{n}_{idx}") for n, idx in order]
    sh_rows = jnp.concatenate([hgrn_lb.reshape(4, sd), conv_w.reshape(6, sd), conv_b.reshape(2, sd)], axis=0)
    buf1 = jnp.concatenate([c, jnp.pad(sh_rows, ((0, 0), (0, D - sd))), jnp.zeros((3, D), F32)], axis=0)
    g1 = small_allgather(buf1, [], "gather_small_in")
    first_sems, first_lands, _, first_token = split_start(lands[:1], None, g1, "gather_start_first")
    cvec = jnp.concatenate([g1[:, 0, :], jnp.broadcast_to(c_ctx[None], (8, D))], axis=0)
    shf = g1[0::2, 1:13, :sd].transpose(1, 0, 2).reshape(12, D)
    lb_p = jax.nn.softmax(shf[0:4].reshape(2, 2, D), axis=1)
    lower = jnp.cumsum(lb_p, axis=1) - lb_p[:, :1]
    lbs = [lower[:, 0], lower[:, 1]]
    cw8 = [_pad8(shf[4:7]), _pad8(shf[7:10])]
    cb = [shf[10:11], shf[11:12]]

    bias = lax.dynamic_slice_in_dim(ada_b, chip * nca, nca, axis=1).reshape(DEPTH, 1, nca)
    ada_part = ada_fwd(cvec, ada_w, bias, "ada_fwd")
    g2 = small_allgather(ada_part.reshape(DEPTH * 16, nca), [first_token] + lands[1:], "gather_ada")
    ada_full = g2[0::2].reshape(4, DEPTH, 16, nca).transpose(1, 2, 0, 3).reshape(DEPTH, 16, 4 * nca)
    lat = lax.dynamic_slice_in_dim(ada_full, me, 1, axis=1)[:, 0]
    mods = [jnp.stack([_pad8(lat[i].reshape(6, D)), _pad8(ada_full[i, 8].reshape(6, D))]) for i in range(DEPTH)]

    rest_sems, rest_lands, _, rest_token = split_start(lands[1:], None, g2, "gather_start")
    w_sems = first_sems + rest_sems
    lands = first_lands + rest_lands
    unit = {key: u for u, key in enumerate(order)}

    def wts(n, idx, after):
        u = unit[(n, idx)]
        if u == 0:
            after = rest_token
        (w,), _ = split_wait([lands[u]], None, w_sems[2 * u:2 * u + 2], after, f"gather_wait_{n}_{idx}")
        return w.reshape(w.shape[0] * w.shape[1], w.shape[2]) if n in ("w2", "hout", "cout") else w

    started = []

    def on_grads(i, tag, g):
        names = sorted(g)
        gs = [g[n].reshape(4, g[n].shape[0] * g[n].shape[1] // 4, g[n].shape[2]) for n in names]
        sems, zones, srcs, token = split_start([lax.empty(a.shape, BF16) for a in gs], gs, chip1, f"grad_start_{tag}_{i}")
        started.append(([(n, i if n in ("w1", "w2") else i // 2) for n in names], sems, zones, srcs))
        return token

    done = {}
    acc = {n: lax.empty(flat2(w).shape, F32) for n, (w, _, _) in tensors.items()}
    early_names = ["w1", "w2", "cin", "cout"]
    late_names = ["hin", "hout"]

    def finish_units(group, after, name):
        units = [(key, sems[2 * u:2 * u + 2], zones[u], srcs[u]) for ks, sems, zones, srcs in group for u, key in enumerate(ks)]
        zones, srcs = split_wait([u[2] for u in units], [u[3] for u in units], [s for u in units for s in u[1]], after, name)
        for (key, _, _, _), zone, own in zip(units, zones, srcs):
            acc[key[0]] = sum_slots(own, zone, acc[key[0]], key[1], chip1, f"sum_{key[0]}_{key[1]}")

    def after_backward(small, token):
        rows3 = jnp.concatenate(small["dmod"] + small["norm1"] + small["norm2"] + [small["norm_f"]] + small["gnorm"]
                                + [small["lb"][1]] + small["cw"] + small["cb"] + [jnp.tile(token[0:3], (1, D // 128))], axis=0)
        g3 = small_allgather(rows3, [], "gather_small_out")
        dmat = g3[:, :64].reshape(8, DEPTH, 2, 8, D)[:, :, :, :6].transpose(1, 2, 0, 3, 4).reshape(DEPTH, 16, 6 * D)
        dcols = lax.dynamic_slice_in_dim(dmat, chip * nca, nca, axis=2)
        *done["ada"], acc4 = ada_bwd(cvec, dcols, ada_w, m_ada_w, v_ada_w, "ada_bwd")
        g4 = small_allgather(acc4, [], "gather_cctx")
        done["fin"] = small_finish(g3, g4, c_ctx[None], _pad8(lb_p.reshape(4, D)), "small_finish")
        finish_units(list(started), done["fin"], "grad_wait_early")
        done["sib_early"] = sibling_start([acc[n] for n in early_names], "sibling_start_early")
        return done["sib_early"][0][-1]

    lane, dx, last_token = local_step(xs, loss_target[0], mods, norm1, norm2, norm_f[None], lbs, hgrn_gnorm, cw8, cb, wts, n_lat,
                                      on_grads, after_backward)
    loss = lax.psum(0.5 * jnp.sum(lane) / D, ("x", "y", "c"))
    grad_x = dx[None]
    g_ada_w, d_ada_w, nm_ada_w, nv_ada_w = done["ada"]
    fin = done["fin"]
    cols = lambda a: lax.dynamic_slice_in_dim(a, chip * sd, sd, axis=a.ndim - 1)
    small_g = [fin[32], fin[0:32].reshape(DEPTH, 8, D)[:, :6].reshape(DEPTH, 6 * D), fin[40:44], fin[44:48], fin[48], fin[49:51],
               cols(fin[64:68].reshape(2, 2, D)), cols(fin[53:59].reshape(2, 3, D)), cols(fin[59:61])]
    small_w = [c_ctx, ada_b, norm1, norm2, norm_f, hgrn_gnorm, hgrn_lb, conv_w, conv_b]
    small_m = [m_c_ctx, m_ada_b, m_norm1, m_norm2, m_norm_f, m_hgrn_gnorm, m_hgrn_lb, m_conv_w, m_conv_b]
    small_v = [v_c_ctx, v_ada_b, v_norm1, v_norm2, v_norm_f, v_hgrn_gnorm, v_hgrn_lb, v_conv_w, v_conv_b]
    shapes = [w.shape for w in small_w]
    packed = adamw([_pack_rows(small_g)], _pack_rows(small_w), _pack_rows(small_m), _pack_rows(small_v), "adamw_small")
    s_g, s_d, s_m, s_v = [_unpack_rows(p, shapes) for p in packed]

    results = {}

    def finish_tensors(names, sib, after, name):
        mine, other = sibling_wait(*sib, after, name)
        for n, pm, po in zip(names, mine, other):
            w, m, v = tensors[n]
            results[n] = [a.reshape(w.shape) for a in adamw([pm, po], flat2(w), flat2(m), flat2(v), f"adamw_{n}")]

    finish_tensors(early_names, done["sib_early"], last_token, "sibling_wait_early")
    finish_units(started[-1:], results["cout"][0], "grad_wait_late")
    sib_late = sibling_start([acc[n] for n in late_names], "sibling_start_late")
    finish_tensors(late_names, sib_late, results["cin"][0], "sibling_wait_late")
    b_g, b_d, b_m, b_v = [[results[n][k] for n in big_names] for k in range(4)]

    def ordered(s, a, b):
        return [s[0], a, s[1], s[2], s[3], s[4], b[0], b[1], b[2], s[6], s[5], b[3], b[4], s[7], s[8], b[5]]

    return (loss, grad_x, *ordered(s_g, g_ada_w, b_g), *ordered(s_d, d_ada_w, b_d), *ordered(s_m, nm_ada_w, b_m),
            *ordered(s_v, nv_ada_w, b_v))
```

```python
import functools

import jax
import jax.numpy as jnp
from jax import lax
from jax.experimental import pallas as pl
from jax.experimental.pallas import tpu as pltpu

F32 = jnp.float32
BF16 = jnp.bfloat16
MESH = pl.DeviceIdType.MESH

D = 1024
HD = 128
NH = D // HD
CH = 64
TM = 256
TMW = 2816
EPS = 1e-6
DEPTH = 4
VMEM_LIMIT = 56 * 1024 * 1024

ADAM_LR = 0.001
ADAM_B1 = 0.9
ADAM_B2 = 0.999
ADAM_EPS = 1e-08
ADAM_WD = 0.01
ADAM_STEP = 10


def _cp(n_grid):
    return pltpu.CompilerParams(dimension_semantics=("arbitrary",) * n_grid, vmem_limit_bytes=VMEM_LIMIT)


def _dot(a, b):
    return jnp.dot(a, b, preferred_element_type=F32)


def _dot_nt(a, b):
    return lax.dot_general(a, b, (((1,), (1,)), ((), ())), preferred_element_type=F32)


def _dot_tn(a, b):
    return lax.dot_general(a, b, (((0,), (0,)), ((), ())), preferred_element_type=F32)


def _sigmoid(z):
    return 1.0 / (1.0 + jnp.exp(-z))


def _norm_mod(x, gain, sh, sc):
    r = lax.rsqrt(jnp.mean(x * x, axis=-1, keepdims=True) + EPS)
    xn = x * r
    yn = xn * gain
    return r, xn, yn, yn * (1.0 + sc) + sh


def _row_spec(width):
    return pl.BlockSpec((TM, width), lambda i: (i, 0))


def _col_spec(col):
    return pl.BlockSpec((TM, D), lambda i: (i, col))


def _full_spec(shape):
    nd = len(shape)
    return pl.BlockSpec(shape, lambda i: (0,) * nd)


def _mod_spec(n_lat):
    return pl.BlockSpec((1, 8, D), lambda i: (i // n_lat, 0, 0))


def _f32(ref):
    return ref[...].astype(F32)


def proj_fwd(x, gain, mod, m0, w4, n_lat, dtype, name):
    t = x.shape[0]
    nb, _, ns = w4.shape

    def body(x_ref, gain_ref, mod_ref, w_ref, p_ref):
        _, _, _, h = _norm_mod(x_ref[...], gain_ref[...], mod_ref[0, m0:m0 + 1, :], mod_ref[0, m0 + 1:m0 + 2, :])
        hb = h.astype(BF16)
        for c in range(nb):
            p_ref[:, c * ns:(c + 1) * ns] = _dot(hb, w_ref[c]).astype(dtype)

    return pl.pallas_call(
        body, name=name, grid=(t // TM,),
        in_specs=[_row_spec(D), _full_spec((1, D)), _mod_spec(n_lat), _full_spec(w4.shape)],
        out_specs=_row_spec(nb * ns),
        out_shape=jax.ShapeDtypeStruct((t, nb * ns), dtype),
        compiler_params=_cp(1),
    )(x, gain, mod, w4)


def proj_bwd(parts, w4, x, gain, mod, m0, dx_in, n_lat, lat_only, name):
    t = x.shape[0]
    nb, _, ns = w4.shape
    n = nb * ns
    n_parts = len(parts)
    widths = [p.shape[1] for p in parts]
    offs = [sum(widths[:k]) for k in range(n_parts)]
    assert sum(widths) == n
    single = n_parts == 1

    def body(*refs):
        part_refs = refs[:n_parts]
        w_ref, x_ref, gain_ref, mod_ref, dxin_ref = refs[n_parts:n_parts + 5]
        rest = refs[n_parts + 5:]
        if single:
            dx_ref, hb_ref, acc_ref = rest
            src = part_refs[0]
        else:
            dx_ref, hb_ref, acc_ref, dpb_ref = rest
            for p_ref, off, w in zip(part_refs, offs, widths):
                dpb_ref[:, off:off + w] = p_ref[...]
            src = dpb_ref
        i = pl.program_id(0)

        @pl.when(i == 0)
        def _():
            acc_ref[...] = jnp.zeros_like(acc_ref)

        gain = gain_ref[...]
        sc = mod_ref[0, m0 + 1:m0 + 2, :]
        r, xn, yn, h = _norm_mod(x_ref[...], gain, mod_ref[0, m0:m0 + 1, :], sc)
        hb_ref[...] = h.astype(BF16)
        dh = _dot_nt(src[:, 0:ns], w_ref[0])
        for c in range(1, nb):
            dh = dh + _dot_nt(src[:, c * ns:(c + 1) * ns], w_ref[c])
        dsh = jnp.sum(dh, axis=0, keepdims=True)
        dsc = jnp.sum(dh * yn, axis=0, keepdims=True)
        dyn = dh * (1.0 + sc)
        dgain = jnp.sum(dyn * xn, axis=0, keepdims=True)
        dxn = dyn * gain
        dx = dxin_ref[...] + r * (dxn - xn * jnp.mean(dxn * xn, axis=-1, keepdims=True))
        if lat_only:
            @pl.when(i < n_lat)
            def _():
                dx_ref[...] = dx
        else:
            dx_ref[...] = dx
        latf = (i < n_lat).astype(F32)
        ctxf = 1.0 - latf
        acc_ref[0:1, :] += dgain
        acc_ref[1:2, :] += dsh * latf
        acc_ref[2:3, :] += dsc * latf
        acc_ref[3:4, :] += dsh * ctxf
        acc_ref[4:5, :] += dsc * ctxf

    dx_rows = n_lat * TM if lat_only else t
    dx_spec = pl.BlockSpec((TM, D), lambda i: (jnp.minimum(i, n_lat - 1), 0)) if lat_only else _row_spec(D)
    out_specs = [dx_spec, _row_spec(D), _full_spec((8, D))]
    out_shape = [jax.ShapeDtypeStruct((dx_rows, D), F32), jax.ShapeDtypeStruct((t, D), BF16), jax.ShapeDtypeStruct((8, D), F32)]
    if not single:
        out_specs.append(_row_spec(n))
        out_shape.append(jax.ShapeDtypeStruct((t, n), BF16))
    outs = pl.pallas_call(
        body, name=name, grid=(t // TM,),
        in_specs=[_row_spec(w) for w in widths]
        + [_full_spec(w4.shape), _row_spec(D), _full_spec((1, D)), _mod_spec(n_lat), _row_spec(D)],
        out_specs=out_specs, out_shape=out_shape, compiler_params=_cp(1),
    )(*parts, w4, x, gain, mod, dx_in)
    if single:
        return outs[0], outs[1], parts[0], outs[2]
    return outs[0], outs[1], outs[3], outs[2]


def dw_tn(a, b, nb, a_blocked, square_a, dep, name):
    t = a.shape[0]
    ka = a.shape[1] // nb if a_blocked else a.shape[1]
    kb = b.shape[1] if a_blocked else b.shape[1] // nb
    n_k = t // TMW

    def body(a_ref, b_ref, _, o_ref, acc):
        k = pl.program_id(1)

        @pl.when(k == 0)
        def _():
            acc[...] = jnp.zeros_like(acc)

        a = a_ref[...]
        acc[...] += _dot_tn(a * a if square_a else a, b_ref[...])

        @pl.when(k == n_k - 1)
        def _():
            o_ref[0] = acc[...].astype(BF16)

    a_spec = pl.BlockSpec((TMW, ka), (lambda j, k: (k, j)) if a_blocked else (lambda j, k: (k, 0)))
    b_spec = pl.BlockSpec((TMW, kb), (lambda j, k: (k, 0)) if a_blocked else (lambda j, k: (k, j)))
    return pl.pallas_call(
        body, name=name, grid=(nb, n_k),
        in_specs=[a_spec, b_spec, ANY],
        out_specs=pl.BlockSpec((1, ka, kb), lambda j, k: (j, 0, 0)),
        out_shape=jax.ShapeDtypeStruct((nb, ka, kb), BF16),
        scratch_shapes=[pltpu.VMEM((ka, kb), F32)],
        compiler_params=_cp(2),
    )(a, b, dep)


def outproj_fwd(prologue, extras, extra_specs, w, x, mod, m0, n_lat, name):
    t = x.shape[0]
    k = w.shape[0]
    n_extra = len(extras)

    def body(*refs):
        ex = refs[:n_extra]
        w_ref, x_ref, mod_ref, xo_ref, y_ref, ab_ref = refs[n_extra:]
        ab = prologue(pl.program_id(0), *ex).astype(BF16)
        ab_ref[...] = ab
        y = _dot(ab, w_ref[...])
        y_ref[...] = y.astype(BF16)
        xo_ref[...] = x_ref[...] + mod_ref[0, m0 + 2:m0 + 3, :] * y

    return pl.pallas_call(
        body, name=name, grid=(t // TM,),
        in_specs=list(extra_specs) + [_full_spec(w.shape), _row_spec(D), _mod_spec(n_lat)],
        out_specs=[_row_spec(D), _row_spec(D), _row_spec(k)],
        out_shape=[jax.ShapeDtypeStruct((t, D), F32), jax.ShapeDtypeStruct((t, D), BF16), jax.ShapeDtypeStruct((t, k), BF16)],
        compiler_params=_cp(1),
    )(*extras, w, x, mod)


def outproj_bwd(epilogue, extras, extra_specs, ep_out_specs, ep_out_shapes, w, dxn, y, mod, m0, n_lat, dep, name):
    t = dxn.shape[0]
    n_extra = len(extras)

    def body(*refs):
        ex = refs[:n_extra]
        w_ref, dxn_ref, y_ref, mod_ref, _, dyb_ref, acc_ref = refs[n_extra:n_extra + 7]
        ep_outs = refs[n_extra + 7:]
        i = pl.program_id(0)

        @pl.when(i == 0)
        def _():
            acc_ref[...] = jnp.zeros_like(acc_ref)

        dxv = dxn_ref[...]
        dyb = (dxv * mod_ref[0, m0 + 2:m0 + 3, :]).astype(BF16)
        dyb_ref[...] = dyb
        dg = jnp.sum(dxv * _f32(y_ref), axis=0, keepdims=True)
        latf = (i < n_lat).astype(F32)
        acc_ref[0:1, :] += dg * latf
        acc_ref[1:2, :] += dg * (1.0 - latf)
        epilogue(i, _dot_nt(dyb, w_ref[...]), ex, ep_outs, acc_ref)

    outs = pl.pallas_call(
        body, name=name, grid=(t // TM,),
        in_specs=list(extra_specs) + [_full_spec(w.shape), _row_spec(D), _row_spec(D), _mod_spec(n_lat), ANY],
        out_specs=[_row_spec(D), _full_spec((8, D))] + list(ep_out_specs),
        out_shape=[jax.ShapeDtypeStruct((t, D), BF16), jax.ShapeDtypeStruct((8, D), F32)] + list(ep_out_shapes),
        compiler_params=_cp(1),
    )(*extras, w, dxn, y, mod, dep)
    return outs[0], outs[1], outs[2:]


def mlp_fwd(x, gain, mod, w1, w2, n_lat, name):
    t = x.shape[0]
    nb, _, ns = w1.shape

    def body(x_ref, gain_ref, mod_ref, w1_ref, w2_ref, xo_ref, y_ref, rb_ref):
        x = x_ref[...]
        _, _, _, h = _norm_mod(x, gain_ref[...], mod_ref[0, 3:4, :], mod_ref[0, 4:5, :])
        hb = h.astype(BF16)
        y = None
        for c in range(nb):
            r = jnp.maximum(_dot(hb, w1_ref[c]), 0.0)
            rb_ref[:, c * ns:(c + 1) * ns] = r.astype(BF16)
            yc = _dot((r * r).astype(BF16), w2_ref[c * ns:(c + 1) * ns, :])
            y = yc if y is None else y + yc
        y_ref[...] = y.astype(BF16)
        xo_ref[...] = x + mod_ref[0, 5:6, :] * y

    return pl.pallas_call(
        body, name=name, grid=(t // TM,),
        in_specs=[_row_spec(D), _full_spec((1, D)), _mod_spec(n_lat), _full_spec(w1.shape), _full_spec(w2.shape)],
        out_specs=[_row_spec(D), _row_spec(D), _row_spec(nb * ns)],
        out_shape=[jax.ShapeDtypeStruct((t, D), F32), jax.ShapeDtypeStruct((t, D), BF16), jax.ShapeDtypeStruct((t, nb * ns), BF16)],
        compiler_params=_cp(1),
    )(x, gain, mod, w1, w2)


def mlp_bwd(dxn, y, ab, x, gain, mod, w1, w2, n_lat, dep, name):
    t = x.shape[0]
    nb, _, ns = w1.shape

    def body(dxn_ref, y_ref, rb_ref, x_ref, gain_ref, mod_ref, w1_ref, w2_ref, _, dx_ref, dyb_ref, dp_ref, hb_ref, acc_ref):
        i = pl.program_id(0)

        @pl.when(i == 0)
        def _():
            acc_ref[...] = jnp.zeros_like(acc_ref)

        dxv = dxn_ref[...]
        dyb = (dxv * mod_ref[0, 5:6, :]).astype(BF16)
        dyb_ref[...] = dyb
        dg = jnp.sum(dxv * _f32(y_ref), axis=0, keepdims=True)
        gain = gain_ref[...]
        sc = mod_ref[0, 4:5, :]
        r, xn, yn, h = _norm_mod(x_ref[...], gain, mod_ref[0, 3:4, :], sc)
        hb_ref[...] = h.astype(BF16)
        dh = None
        for c in range(nb):
            cols = slice(c * ns, (c + 1) * ns)
            da = _dot_nt(dyb, w2_ref[cols, :])
            dp = (da * (2.0 * rb_ref[:, cols].astype(F32))).astype(BF16)
            dp_ref[:, cols] = dp
            d = _dot_nt(dp, w1_ref[c])
            dh = d if dh is None else dh + d
        dsh = jnp.sum(dh, axis=0, keepdims=True)
        dsc = jnp.sum(dh * yn, axis=0, keepdims=True)
        dyn = dh * (1.0 + sc)
        dgain = jnp.sum(dyn * xn, axis=0, keepdims=True)
        dxn_ = dyn * gain
        dx_ref[...] = dxv + r * (dxn_ - xn * jnp.mean(dxn_ * xn, axis=-1, keepdims=True))
        latf = (i < n_lat).astype(F32)
        ctxf = 1.0 - latf
        acc_ref[0:1, :] += dgain
        acc_ref[1:2, :] += dsh * latf
        acc_ref[2:3, :] += dsc * latf
        acc_ref[3:4, :] += dsh * ctxf
        acc_ref[4:5, :] += dsc * ctxf
        acc_ref[5:6, :] += dg * latf
        acc_ref[6:7, :] += dg * ctxf

    return pl.pallas_call(
        body, name=name, grid=(t // TM,),
        in_specs=[_row_spec(D), _row_spec(D), _row_spec(nb * ns), _row_spec(D), _full_spec((1, D)), _mod_spec(n_lat),
                  _full_spec(w1.shape), _full_spec(w2.shape), ANY],
        out_specs=[_row_spec(D), _row_spec(D), _row_spec(nb * ns), _row_spec(D), _full_spec((8, D))],
        out_shape=[jax.ShapeDtypeStruct((t, D), F32), jax.ShapeDtypeStruct((t, D), BF16), jax.ShapeDtypeStruct((t, nb * ns), BF16),
                   jax.ShapeDtypeStruct((t, D), BF16), jax.ShapeDtypeStruct((8, D), F32)],
        compiler_params=_cp(1),
    )(dxn, y, ab, x, gain, mod, w1, w2, dep)


def readout_prologue(i, o0_ref, o1_ref, gate_ref, gn_ref):
    o = _f32(o0_ref) + _f32(o1_ref)
    gate = gate_ref[...]
    w = gn_ref[...] * (gate * _sigmoid(gate))
    pieces = []
    for h in range(NH):
        sl = slice(h * HD, (h + 1) * HD)
        oh = o[:, sl]
        pieces.append(oh * lax.rsqrt(jnp.mean(oh * oh, axis=-1, keepdims=True) + EPS) * w[:, sl])
    return jnp.concatenate(pieces, axis=1)


def readout_epilogue(i, da, ex, outs, acc_ref):
    o0_ref, o1_ref, gate_ref, gn_ref = ex
    do_ref, dgate_ref = outs
    o = _f32(o0_ref) + _f32(o1_ref)
    gate = gate_ref[...]
    gn = gn_ref[...]
    sg = _sigmoid(gate)
    silu = gate * sg
    dsilu = sg * (1.0 + gate * (1.0 - sg))
    for h in range(NH):
        sl = slice(h * HD, (h + 1) * HD)
        oh = o[:, sl]
        r = lax.rsqrt(jnp.mean(oh * oh, axis=-1, keepdims=True) + EPS)
        nh = oh * r
        dah = da[:, sl]
        acc_ref[2:3, sl] += jnp.sum(dah * nh * silu[:, sl], axis=0, keepdims=True)
        dgate_ref[:, sl] = (dah * nh * gn[:, sl] * dsilu[:, sl]).astype(BF16)
        dn = dah * gn[:, sl] * silu[:, sl]
        do_ref[:, sl] = r * (dn - nh * jnp.mean(dn * nh, axis=-1, keepdims=True))


def _seg_masks(i, n_lat):
    rows = lax.broadcasted_iota(jnp.int32, (TM, 1), 0)
    latf = (i < n_lat).astype(F32)
    ctxf = 1.0 - latf
    prev_ok = (rows % CH != 0).astype(F32) * latf + (rows != 0).astype(F32) * ctxf
    next_ok = (rows % CH != CH - 1).astype(F32) * latf + (rows != TM - 1).astype(F32) * ctxf
    return prev_ok, next_ok


def _shifts(i, n_lat, sft, cur, halo_prev, halo_next):
    if sft == 1:
        prev_ok, next_ok = _seg_masks(i, n_lat)
        return pltpu.roll(cur, 1, 0) * prev_ok, pltpu.roll(cur, TM - 1, 0) * next_ok
    has_prev = jnp.logical_and(i > 0, i < n_lat).astype(F32)
    has_next = (i < n_lat - 1).astype(F32)
    prev = jnp.concatenate([halo_prev * has_prev, cur[:TM - CH]], axis=0)
    nxt = jnp.concatenate([cur[CH:], halo_next * has_next], axis=0)
    return prev, nxt


def _conv_u(sft, ex):
    if sft == 1:
        gb_ref, gc_ref, xi_ref, cw_ref, cb_ref = ex
        return gb_ref, _f32(gc_ref) * _f32(xi_ref), None, None, cw_ref, cb_ref
    gb_ref, gc_ref, xi_ref, gcp_ref, xip_ref, gcn_ref, xin_ref, cw_ref, cb_ref = ex
    return gb_ref, _f32(gc_ref) * _f32(xi_ref), _f32(gcp_ref) * _f32(xip_ref), _f32(gcn_ref) * _f32(xin_ref), cw_ref, cb_ref


def _conv_value(i, n_lat, sft, ex):
    gb_ref, u, up, un, cw_ref, cb_ref = _conv_u(sft, ex)
    u_prev, u_next = _shifts(i, n_lat, sft, u, up, un)
    return gb_ref, cb_ref[...] + cw_ref[0:1, :] * u_prev + cw_ref[1:2, :] * u + cw_ref[2:3, :] * u_next


def make_conv_prologue(n_lat, sft):
    def prologue(i, *ex):
        gb_ref, conv = _conv_value(i, n_lat, sft, ex)
        return _f32(gb_ref) * conv
    return prologue


def make_conv_epilogue(n_lat, sft):
    def epilogue(i, da, ex, outs, acc_ref):
        gb_ref, conv = _conv_value(i, n_lat, sft, ex)
        outs[0][...] = (da * _f32(gb_ref)).astype(BF16)
        outs[1][...] = (da * conv).astype(BF16)
    return epilogue


def _conv_specs(sft, t):
    specs = [_col_spec(0), _col_spec(1), _col_spec(2)]
    if sft != 1:
        per = TM // CH
        last = t // CH - 1
        for fn in (lambda i: jnp.maximum(i * per - 1, 0), lambda i: jnp.minimum(i * per + per, last)):
            for col in (1, 2):
                specs.append(pl.BlockSpec((CH, D), functools.partial(lambda i, f, c: (f(i), c), f=fn, c=col)))
    return specs + [_full_spec((8, D)), _full_spec((1, D))]


def _conv_args(sft, p, cw8, cb):
    return [p] * (3 if sft == 1 else 7) + [cw8, cb]


def conv_bwd(dconv, p, cw8, sft, n_lat, name):
    t = dconv.shape[0]
    halo = sft != 1

    def body(*refs):
        if halo:
            dc_ref, dcp_ref, dcn_ref, gc_ref, xi_ref, gcp_ref, xip_ref, gcn_ref, xin_ref, cw_ref, dgc_ref, dxi_ref, acc_ref = refs
            up, un = _f32(gcp_ref) * _f32(xip_ref), _f32(gcn_ref) * _f32(xin_ref)
            dcp, dcn = _f32(dcp_ref), _f32(dcn_ref)
        else:
            dc_ref, gc_ref, xi_ref, cw_ref, dgc_ref, dxi_ref, acc_ref = refs
            up = un = dcp = dcn = None
        i = pl.program_id(0)

        @pl.when(i == 0)
        def _():
            acc_ref[...] = jnp.zeros_like(acc_ref)

        gc = _f32(gc_ref)
        xi = _f32(xi_ref)
        u = gc * xi
        dc = _f32(dc_ref)
        u_prev, u_next = _shifts(i, n_lat, sft, u, up, un)
        dc_prev, dc_next = _shifts(i, n_lat, sft, dc, dcp, dcn)
        acc_ref[0:1, :] += jnp.sum(dc * u_prev, axis=0, keepdims=True)
        acc_ref[1:2, :] += jnp.sum(dc * u, axis=0, keepdims=True)
        acc_ref[2:3, :] += jnp.sum(dc * u_next, axis=0, keepdims=True)
        acc_ref[3:4, :] += jnp.sum(dc, axis=0, keepdims=True)
        du = cw_ref[0:1, :] * dc_next + cw_ref[1:2, :] * dc + cw_ref[2:3, :] * dc_prev
        dgc_ref[...] = (du * xi).astype(BF16)
        dxi_ref[...] = (du * gc).astype(BF16)

    per = TM // CH
    last = t // CH - 1
    prev_i = lambda i: jnp.maximum(i * per - 1, 0)
    next_i = lambda i: jnp.minimum(i * per + per, last)
    if halo:
        in_specs = [_row_spec(D), pl.BlockSpec((CH, D), lambda i: (prev_i(i), 0)), pl.BlockSpec((CH, D), lambda i: (next_i(i), 0)),
                    _col_spec(1), _col_spec(2),
                    pl.BlockSpec((CH, D), lambda i: (prev_i(i), 1)), pl.BlockSpec((CH, D), lambda i: (prev_i(i), 2)),
                    pl.BlockSpec((CH, D), lambda i: (next_i(i), 1)), pl.BlockSpec((CH, D), lambda i: (next_i(i), 2)),
                    _full_spec((8, D))]
        args = [dconv, dconv, dconv, p, p, p, p, p, p, cw8]
    else:
        in_specs = [_row_spec(D), _col_spec(1), _col_spec(2), _full_spec((8, D))]
        args = [dconv, p, p, cw8]
    return pl.pallas_call(
        body, name=name, grid=(t // TM,), in_specs=in_specs,
        out_specs=[_row_spec(D), _row_spec(D), _full_spec((8, D))],
        out_shape=[jax.ShapeDtypeStruct((t, D), BF16), jax.ShapeDtypeStruct((t, D), BF16), jax.ShapeDtypeStruct((8, D), F32)],
        compiler_params=_cp(1),
    )(*args)


LOG2E = 1.4426950408889634


def _cumsum_matrix(reverse):
    r = lax.broadcasted_iota(jnp.int32, (CH, CH), 0)
    c = lax.broadcasted_iota(jnp.int32, (CH, CH), 1)
    return (r <= c if reverse else r >= c).astype(BF16)


def _chunk_cumsum(g, tri):
    hi = g.astype(BF16)
    lo = (g - hi.astype(F32)).astype(BF16)
    return _dot(tri, hi) + _dot(tri, lo)


def _gate_values(z, lb):
    sig = _sigmoid(z)
    f = lb + (1.0 - lb) * sig
    return sig, f


def _tri(direction, transposed):
    r = lax.broadcasted_iota(jnp.int32, (CH, CH), 0)
    c = lax.broadcasted_iota(jnp.int32, (CH, CH), 1)
    lower = (direction == 0) != transposed
    return r >= c if lower else r <= c


def _gla_rows(direction):
    return (CH // 2 - 1, CH - 1) if direction == 0 else (CH // 2, 0)


def _fwd_tile(direction, nt):
    return (lambda i: (i + nt - 1) % nt) if direction == 0 else (lambda i: nt - 1 - i)


def gla_fwd(p, lb2, name):
    t = p.shape[0]
    nt = t // TM
    per = TM // CH

    def body(z0_ref, v0_ref, q0_ref, z1_ref, v1_ref, q1_ref, lb_ref, o0_ref, s0_ref, o1_ref, s1_ref, st, q_s, k_s, c_s):
        @pl.when(pl.program_id(0) == 0)
        def _():
            st[...] = jnp.zeros_like(st)

        ins = ((z0_ref, v0_ref, q0_ref, o0_ref, s0_ref), (z1_ref, v1_ref, q1_ref, o1_ref, s1_ref))
        for d in range(2):
            z_ref, _, qr_ref, _, _ = ins[d]
            tri = _cumsum_matrix(d == 1)
            lb = lb_ref[d:d + 1, :]
            for ci in range(per):
                rows = slice(ci * CH, (ci + 1) * CH)
                _, f = _gate_values(z_ref[rows, :], lb)
                k_s[d, rows, :] = 1.0 - f
                c_s[d, rows, :] = _chunk_cumsum(jnp.log(f) * LOG2E, tri)
                qr = qr_ref[rows, :]
                q_s[d, rows, :] = qr * _sigmoid(qr)
        masks = (_tri(0, False), _tri(1, False))
        state = [[st[d, h] for h in range(NH)] for d in range(2)]
        for it in range(per):
            chunk = []
            for d in range(2):
                ref_row, last_row = _gla_rows(d)
                ci = it if d == 0 else per - 1 - it
                r0 = ci * CH
                rows = slice(r0, r0 + CH)
                cum = c_s[d, rows, :]
                ref = c_s[d, r0 + ref_row:r0 + ref_row + 1, :]
                last = c_s[d, r0 + last_row:r0 + last_row + 1, :]
                q = q_s[d, rows, :]
                k = k_s[d, rows, :]
                chunk.append(dict(
                    ci=ci, rows=rows, qh=(q * jnp.exp2(cum)).astype(BF16), qt=(q * jnp.exp2(cum - ref)).astype(BF16),
                    kt=(k * jnp.exp2(ref - cum)).astype(BF16), kb=(k * jnp.exp2(last - cum)).astype(BF16),
                    el=jnp.exp2(last), vb=ins[d][1][rows, :].astype(BF16)))
            for h in range(NH):
                sl = slice(h * HD, (h + 1) * HD)
                for d in range(2):
                    c = chunk[d]
                    o_ref, s_ref = ins[d][3], ins[d][4]
                    s_t = state[d][h]
                    s_ref[c["ci"], h] = s_t
                    sc = jnp.where(masks[d], _dot_nt(c["qt"][:, sl], c["kt"][:, sl]), 0.0)
                    o_ref[c["rows"], sl] = (_dot_nt(c["qh"][:, sl], s_t.astype(BF16))
                                            + _dot(sc.astype(BF16), c["vb"][:, sl])).astype(BF16)
                    state[d][h] = s_t * c["el"][:, sl] + _dot_tn(c["vb"][:, sl], c["kb"][:, sl])
        for d in range(2):
            for h in range(NH):
                st[d, h] = state[d][h]

    tiles = (_fwd_tile(0, nt), _fwd_tile(1, nt))
    tspec = lambda d, col: pl.BlockSpec((TM, D), lambda i: (tiles[d](i), col))
    sspec = lambda d: pl.BlockSpec((per, NH, HD, HD), lambda i: (tiles[d](i), 0, 0, 0))
    o_shape = jax.ShapeDtypeStruct((t, D), BF16)
    s_shape = jax.ShapeDtypeStruct((t // CH, NH, HD, HD), F32)
    return pl.pallas_call(
        body, name=name, grid=(nt,),
        in_specs=[tspec(0, 0), tspec(0, 2), tspec(0, 3), tspec(1, 1), tspec(1, 2), tspec(1, 3), _full_spec((2, D))],
        out_specs=[tspec(0, 0), sspec(0), tspec(1, 0), sspec(1)],
        out_shape=[o_shape, s_shape, o_shape, s_shape],
        scratch_shapes=[pltpu.VMEM((2, NH, HD, HD), F32)] + [pltpu.VMEM((2, TM, D), F32)] * 3,
        compiler_params=_cp(1),
    )(p, p, p, p, p, p, lb2)


def gla_bwd(p, lb2, do, states, direction, prev, name):
    t = p.shape[0]
    nt = t // TM
    per = TM // CH
    ref_row, last_row = _gla_rows(direction)
    tile = (lambda i: (2 * nt - 2 - i) % nt) if direction == 0 else (lambda i: i)
    final = prev is not None
    n_in = 8 if final else 6

    def body(*refs):
        z_ref, v_ref, qr_ref, lb_ref, do_ref, s_ref = refs[:6]
        dz_ref, dv_ref, dq_ref, acc_ref, dst, q_s, k_s, c_s, dq_s, dk_s, dl_s = refs[n_in:]

        @pl.when(pl.program_id(0) == 0)
        def _():
            dst[...] = jnp.zeros_like(dst)
            acc_ref[...] = jnp.zeros_like(acc_ref)

        mask = _tri(direction, False)
        mask_t = _tri(direction, True)
        tri = _cumsum_matrix(direction == 1)
        tri_t = _cumsum_matrix(direction == 0)
        is_last = lax.broadcasted_iota(jnp.int32, (CH, 1), 0) == last_row
        lb = lb_ref[direction:direction + 1, :]
        for ci in range(per):
            rows = slice(ci * CH, (ci + 1) * CH)
            _, f = _gate_values(z_ref[rows, :], lb)
            k_s[rows, :] = 1.0 - f
            c_s[rows, :] = _chunk_cumsum(jnp.log(f) * LOG2E, tri)
            qr = qr_ref[rows, :]
            q_s[rows, :] = qr * _sigmoid(qr)
        for it in range(per):
            ci = per - 1 - it if direction == 0 else it
            r0 = ci * CH
            rows = slice(r0, r0 + CH)
            cum = c_s[rows, :]
            ref = c_s[r0 + ref_row:r0 + ref_row + 1, :]
            last = c_s[r0 + last_row:r0 + last_row + 1, :]
            q = q_s[rows, :]
            k = k_s[rows, :]
            e_h = jnp.exp2(cum)
            e_t = jnp.exp2(cum - ref)
            e_kt = jnp.exp2(ref - cum)
            e_kb = jnp.exp2(last - cum)
            el = jnp.exp2(last)
            qh = (q * e_h).astype(BF16)
            qt = (q * e_t).astype(BF16)
            kt = (k * e_kt).astype(BF16)
            kbf = k * e_kb
            kb = kbf.astype(BF16)
            vb = v_ref[rows, :].astype(BF16)
            dob = do_ref[rows, :].astype(BF16)
            for h in range(NH):
                sl = slice(h * HD, (h + 1) * HD)
                s_t = s_ref[ci, h]
                ds_t = dst[h]
                ds_b = ds_t.astype(BF16)
                d_a = jnp.where(mask, _dot_nt(dob[:, sl], vb[:, sl]), 0.0).astype(BF16)
                a_t = jnp.where(mask_t, _dot_nt(kt[:, sl], qt[:, sl]), 0.0).astype(BF16)
                d_at = jnp.where(mask_t, _dot_nt(vb[:, sl], dob[:, sl]), 0.0).astype(BF16)
                dv = _dot(a_t, dob[:, sl]) + _dot_nt(kb[:, sl], ds_b)
                dkb = _dot(vb[:, sl], ds_b)
                dl_s[it:it + 1, sl] = (el[:, sl] * jnp.sum(ds_t * s_t, axis=0, keepdims=True)
                                       + jnp.sum(dkb * kbf[:, sl], axis=0, keepdims=True))
                dst[h] = ds_t * el[:, sl] + _dot_tn(dob[:, sl], qh[:, sl])
                dq_s[rows, sl] = _dot(dob[:, sl], s_t.astype(BF16)) * e_h[:, sl] + _dot(d_a, kt[:, sl]) * e_t[:, sl]
                dk_s[rows, sl] = _dot(d_at, qt[:, sl]) * e_kt[:, sl] + dkb * e_kb[:, sl]
                if final:
                    dv_ref[rows, sl] = (refs[6][rows, sl] + dv).astype(BF16)
                else:
                    dv_ref[rows, sl] = dv
        for it in range(per):
            ci = per - 1 - it if direction == 0 else it
            rows = slice(ci * CH, (ci + 1) * CH)
            dq = dq_s[rows, :]
            dk = dk_s[rows, :]
            dg = _chunk_cumsum(dq * q_s[rows, :] - dk * k_s[rows, :] + jnp.where(is_last, dl_s[it:it + 1, :], 0.0), tri_t)
            sig, f = _gate_values(z_ref[rows, :], lb)
            df = dg / f - dk
            acc_ref[0:1, :] += jnp.sum(df * (1.0 - sig), axis=0, keepdims=True)
            dz_ref[rows, :] = (df * (1.0 - lb) * sig * (1.0 - sig)).astype(BF16)
            if final:
                qr = qr_ref[rows, :]
                sq = _sigmoid(qr)
                dq_ref[rows, :] = ((refs[7][rows, :] + dq) * (sq * (1.0 + qr * (1.0 - sq)))).astype(BF16)
            else:
                dq_ref[rows, :] = dq

    tspec = lambda col: pl.BlockSpec((TM, D), lambda i: (tile(i), col))
    sspec = pl.BlockSpec((per, NH, HD, HD), lambda i: (tile(i), 0, 0, 0))
    in_specs = [tspec(direction), tspec(2), tspec(3), _full_spec((2, D)), tspec(0), sspec]
    args = [p, p, p, lb2, do, states]
    if final:
        in_specs += [tspec(0), tspec(0)]
        args += list(prev)
    odt = BF16 if final else F32
    return pl.pallas_call(
        body, name=name, grid=(nt,), in_specs=in_specs,
        out_specs=[tspec(0), tspec(0), tspec(0), _full_spec((8, D))],
        out_shape=[jax.ShapeDtypeStruct((t, D), BF16), jax.ShapeDtypeStruct((t, D), odt), jax.ShapeDtypeStruct((t, D), odt),
                   jax.ShapeDtypeStruct((8, D), F32)],
        scratch_shapes=[pltpu.VMEM((NH, HD, HD), F32)] + [pltpu.VMEM((TM, D), F32)] * 5 + [pltpu.VMEM((8, D), F32)],
        compiler_params=_cp(1),
    )(*args)


def loss_bwd(x, gain, target, n_lat, name):
    t = x.shape[0]

    def body(x_ref, gain_ref, tg_ref, dx_ref, acc_ref):
        i = pl.program_id(0)

        @pl.when(i == 0)
        def _():
            acc_ref[...] = jnp.zeros_like(acc_ref)

        latf = (i < n_lat).astype(F32)
        x = x_ref[...]
        gain = gain_ref[...]
        r = lax.rsqrt(jnp.mean(x * x, axis=-1, keepdims=True) + EPS)
        xn = x * r
        err = (xn * gain - tg_ref[...]) * latf
        dy = err * (1.0 / D)
        dxn = dy * gain
        dx_ref[...] = r * (dxn - xn * jnp.mean(dxn * xn, axis=-1, keepdims=True))
        acc_ref[0:1, :] += jnp.sum(dy * xn, axis=0, keepdims=True)
        acc_ref[1:2, :] += jnp.sum(err * err, axis=0, keepdims=True)

    return pl.pallas_call(
        body, name=name, grid=(t // TM,),
        in_specs=[_row_spec(D), _full_spec((1, D)), pl.BlockSpec((TM, D), lambda i: (jnp.minimum(i, n_lat - 1), 0))],
        out_specs=[_row_spec(D), _full_spec((8, D))],
        out_shape=[jax.ShapeDtypeStruct((t, D), F32), jax.ShapeDtypeStruct((8, D), F32)],
        compiler_params=_cp(1),
    )(x, gain, target)


def local_step(xs, target, mods, norm1, norm2, norm_f, lbs, gnorm, cw8, cb, wts, n_lat, on_grads, after_backward):
    t = xs.shape[0]
    saved = []
    cache = {}

    def W(name, idx, after=None):
        if (name, idx) not in cache:
            cache[(name, idx)] = wts(name, idx, after)
        return cache[(name, idx)]

    x = xs
    for i in range(DEPTH):
        j = i // 2
        rec = i % 2 == 0
        n1 = norm1[i:i + 1]
        n2 = norm2[i:i + 1]
        s = {"x_in": x}
        if rec:
            p = proj_fwd(x, n1, mods[i], 0, W("hin", j, x), n_lat, F32, f"hin_fwd_{i}")
            o0, st0, o1, st1 = gla_fwd(p, lbs[j], f"gla_fwd_{i}")
            ex = [o0, o1, p, gnorm[j:j + 1]]
            ex_specs = [_row_spec(D), _row_spec(D), _col_spec(4), _full_spec((1, D))]
            xm, y, ab = outproj_fwd(readout_prologue, ex, ex_specs, W("hout", j, o1), x, mods[i], 0, n_lat, f"hout_fwd_{i}")
            s.update(st0=st0, st1=st1)
        else:
            sft = 1 if j % 2 == 0 else CH
            p = proj_fwd(x, n1, mods[i], 0, W("cin", j, x), n_lat, BF16, f"cin_fwd_{i}")
            ex = _conv_args(sft, p, cw8[j], cb[j])
            ex_specs = _conv_specs(sft, t)
            xm, y, ab = outproj_fwd(make_conv_prologue(n_lat, sft), ex, ex_specs, W("cout", j, p), x, mods[i], 0, n_lat, f"cout_fwd_{i}")
        s.update(p=p, ex=ex, ex_specs=ex_specs, y_mix=y, ab_mix=ab, x_mid=xm)
        x, y2, ab2 = mlp_fwd(xm, n2, mods[i], W("w1", i, xm), W("w2", i, xm), n_lat, f"mlp_fwd_{i}")
        s.update(y_mlp=y2, ab_mlp=ab2)
        saved.append(s)

    dx, acc_loss = loss_bwd(x, norm_f, target, n_lat, "loss")
    small = {"norm_f": acc_loss[0:1], "norm1": [None] * DEPTH, "norm2": [None] * DEPTH, "dmod": [None] * DEPTH,
             "gnorm": [None] * 2, "lb": [None] * 2, "cw": [None] * 2, "cb": [None] * 2}
    bshape = lambda w: jax.ShapeDtypeStruct((t, w), BF16)
    token = jnp.zeros((8, 128), F32)
    for i in reversed(range(DEPTH)):
        j = i // 2
        rec = i % 2 == 0
        s = saved[i]
        n1 = norm1[i:i + 1]
        n2 = norm2[i:i + 1]
        dx, dyb, dp1, hb, acc_n2 = mlp_bwd(dx, s["y_mlp"], s["ab_mlp"], s["x_mid"], n2, mods[i], W("w1", i), W("w2", i), n_lat, token,
                                           f"mlp_bwd_{i}")
        token = on_grads(i, "mlp", {"w2": dw_tn(s["ab_mlp"], dyb, 4, True, True, token, f"w2_dw_{i}"),
                                    "w1": dw_tn(hb, dp1, 4, False, False, token, f"w1_dw_{i}")})
        if rec:
            dyb, acc_g1, (do, dgate) = outproj_bwd(
                readout_epilogue, s["ex"], s["ex_specs"], [_row_spec(D), _row_spec(D)],
                [jax.ShapeDtypeStruct((t, D), F32), bshape(D)], W("hout", j), dx, s["y_mix"], mods[i], 0, n_lat, token, f"hout_bwd_{i}")
            dz0, dv0, dq0, acc_l0 = gla_bwd(s["p"], lbs[j], do, s["st0"], 0, None, f"gla_bwd0_{i}")
            dz1, dv, dq, acc_l1 = gla_bwd(s["p"], lbs[j], do, s["st1"], 1, (dv0, dq0), f"gla_bwd1_{i}")
            dx, hb, dpb, acc_n1 = proj_bwd([dz0, dz1, dv, dq, dgate], W("hin", j), s["x_in"], n1, mods[i], 0, dx, n_lat, i == 0,
                                           f"hin_bwd_{i}")
            small["gnorm"][j] = acc_g1[2:3]
            small["lb"][j] = jnp.concatenate([acc_l0[0:1], acc_l1[0:1]], axis=0)
            mix = ("hout", "hin")
        else:
            sft = 1 if j % 2 == 0 else CH
            dyb, acc_g1, (dconv, dgb) = outproj_bwd(
                make_conv_epilogue(n_lat, sft), s["ex"], s["ex_specs"], [_row_spec(D), _row_spec(D)],
                [bshape(D), bshape(D)], W("cout", j), dx, s["y_mix"], mods[i], 0, n_lat, token, f"cout_bwd_{i}")
            dgc, dxi, acc_c = conv_bwd(dconv, s["p"], cw8[j], sft, n_lat, f"conv_bwd_{i}")
            dx, hb, dpb, acc_n1 = proj_bwd([dgb, dgc, dxi], W("cin", j), s["x_in"], n1, mods[i], 0, dx, n_lat, False, f"cin_bwd_{i}")
            small["cw"][j] = acc_c[0:3]
            small["cb"][j] = acc_c[3:4]
            mix = ("cout", "cin")
        small["norm1"][i] = acc_n1[0:1]
        small["norm2"][i] = acc_n2[0:1]
        z2 = jnp.zeros((2, D), F32)
        small["dmod"][i] = jnp.concatenate([acc_n1[1:3], acc_g1[0:1], acc_n2[1:3], acc_n2[5:6], z2,
                                            acc_n1[3:5], acc_g1[1:2], acc_n2[3:5], acc_n2[6:7], z2], axis=0)
        if i == 0:
            token = after_backward(small, token)
        token = on_grads(i, "mix", {mix[0]: dw_tn(s["ab_mix"], dyb, 1, False, False, token, f"{mix[0]}_dw_{i}"),
                                    mix[1]: dw_tn(hb, dpb, 4, False, False, token, f"{mix[1]}_dw_{i}")})
    return acc_loss[1:2], dx, token


RB = 256


def cast_to_slot(w2d, layer, k, chip1, name):
    c = w2d.shape[1]
    nblk = k // RB

    def body(chip_ref, w_ref, o_ref):
        o_ref[0] = w_ref[...].astype(BF16)

    return pl.pallas_call(
        body, name=name,
        grid_spec=pltpu.PrefetchScalarGridSpec(
            num_scalar_prefetch=1, grid=(nblk,),
            in_specs=[pl.BlockSpec((RB, c), lambda i, ch: (layer * nblk + i, 0))],
            out_specs=pl.BlockSpec((1, RB, c), lambda i, ch: (ch[0], i, 0))),
        out_shape=jax.ShapeDtypeStruct((4, k, c), BF16), compiler_params=_cp(1))(chip1, w2d)


def sum_slots(own, land, acc, layer, chip1, name):
    _, k, c = own.shape
    nblk = k // RB

    def body(chip_ref, own_ref, l1_ref, l2_ref, l3_ref, acc_ref, o_ref):
        o_ref[...] = ((own_ref[0].astype(F32) + l1_ref[0].astype(F32)) + l2_ref[0].astype(F32)) + l3_ref[0].astype(F32)

    slot = lambda d: pl.BlockSpec((1, RB, c), lambda i, ch: ((ch[0] + d) % 4, i, 0))
    return pl.pallas_call(
        body, name=name,
        grid_spec=pltpu.PrefetchScalarGridSpec(
            num_scalar_prefetch=1, grid=(nblk,),
            in_specs=[slot(0), slot(1), slot(2), slot(3), ANY],
            out_specs=pl.BlockSpec((RB, c), lambda i, ch: (layer * nblk + i, 0))),
        out_shape=jax.ShapeDtypeStruct(acc.shape, F32), input_output_aliases={5: 0}, compiler_params=_cp(1),
    )(chip1, own, land, land, land, acc)


def _adamw_math(w, g, m, v):
    m = ADAM_B1 * m + (1.0 - ADAM_B1) * g
    v = ADAM_B2 * v + (1.0 - ADAM_B2) * jnp.square(g)
    m_hat = m / (1.0 - ADAM_B1 ** ADAM_STEP)
    v_hat = v / (1.0 - ADAM_B2 ** ADAM_STEP)
    delta = -ADAM_LR * (m_hat / (jnp.sqrt(v_hat) + ADAM_EPS) + ADAM_WD * w)
    return delta, m, v


def adamw(gsrcs, w, m, v, name):
    r, c = w.shape
    rb = RB if r % RB == 0 else r
    n_g = len(gsrcs)

    def body(*refs):
        g = refs[0][...]
        for k in range(1, n_g):
            g = g + refs[k][...]
        w_ref, m_ref, v_ref, g_ref, d_ref, mo_ref, vo_ref = refs[n_g:]
        delta, mo, vo = _adamw_math(w_ref[...], g, m_ref[...], v_ref[...])
        g_ref[...] = g
        d_ref[...] = delta
        mo_ref[...] = mo
        vo_ref[...] = vo

    spec = pl.BlockSpec((rb, c), lambda i: (i, 0))
    shp = jax.ShapeDtypeStruct((r, c), F32)
    return pl.pallas_call(body, name=name, grid=(r // rb,), in_specs=[spec] * (n_g + 3), out_specs=[spec] * 4,
                          out_shape=[shp] * 4, compiler_params=_cp(1))(*gsrcs, w, m, v)


ADA_CB = 512


def ada_fwd(cvec, ada_w, bias, name):
    _, _, nc = ada_w.shape

    def body(c_ref, w_ref, b_ref, o_ref):
        cv = c_ref[...]
        a = (cv * _sigmoid(cv)).astype(BF16)
        o_ref[0] = _dot(a, w_ref[0].astype(BF16)) + b_ref[0]

    return pl.pallas_call(
        body, name=name, grid=(DEPTH, nc // ADA_CB),
        in_specs=[pl.BlockSpec((16, D), lambda i, j: (0, 0)), pl.BlockSpec((1, D, ADA_CB), lambda i, j: (i, 0, j)),
                  pl.BlockSpec((1, 1, ADA_CB), lambda i, j: (i, 0, j))],
        out_specs=pl.BlockSpec((1, 16, ADA_CB), lambda i, j: (i, 0, j)),
        out_shape=jax.ShapeDtypeStruct((DEPTH, 16, nc), F32), compiler_params=_cp(2),
    )(cvec, ada_w, bias)


def ada_bwd(cvec, dcols, ada_w, m, v, name):
    _, _, nc = ada_w.shape

    def body(c_ref, d_ref, w_ref, m_ref, v_ref, g_ref, dl_ref, mo_ref, vo_ref, acc_ref):
        @pl.when(jnp.logical_and(pl.program_id(0) == 0, pl.program_id(1) == 0))
        def _():
            acc_ref[...] = jnp.zeros_like(acc_ref)

        cv = c_ref[...]
        a = (cv * _sigmoid(cv)).astype(BF16)
        db = d_ref[0].astype(BF16)
        w = w_ref[0]
        g = _dot_tn(a, db)
        delta, mo, vo = _adamw_math(w, g, m_ref[0], v_ref[0])
        g_ref[0] = g
        dl_ref[0] = delta
        mo_ref[0] = mo
        vo_ref[0] = vo
        acc_ref[...] += _dot_nt(db[8:16, :], w.astype(BF16))

    wspec = pl.BlockSpec((1, D, ADA_CB), lambda i, j: (i, 0, j))
    wshape = jax.ShapeDtypeStruct(ada_w.shape, F32)
    return pl.pallas_call(
        body, name=name, grid=(DEPTH, nc // ADA_CB),
        in_specs=[pl.BlockSpec((16, D), lambda i, j: (0, 0)), pl.BlockSpec((1, 16, ADA_CB), lambda i, j: (i, 0, j)), wspec, wspec, wspec],
        out_specs=[wspec, wspec, wspec, wspec, pl.BlockSpec((8, D), lambda i, j: (0, 0))],
        out_shape=[wshape, wshape, wshape, wshape, jax.ShapeDtypeStruct((8, D), F32)], compiler_params=_cp(2),
    )(cvec, dcols, ada_w, m, v)


def _place():
    return lax.axis_index("x"), lax.axis_index("y"), lax.axis_index("c")


ANY = pl.BlockSpec(memory_space=pl.ANY)
VMEM_SPEC = pl.BlockSpec(memory_space=pltpu.VMEM)


def small_allgather(buf, deps, name):
    r, c = buf.shape
    n_dep = len(deps)

    def body(*refs):
        in_ref = refs[0]
        out_ref, send_sems, recv_sems, loc_sem = refs[1 + n_dep:]
        x, y, cc = _place()
        me = 4 * x + 2 * y + cc
        loc = pltpu.make_async_copy(in_ref, out_ref.at[me], loc_sem)
        loc.start()
        peers = []
        for k in range(1, 8):
            px = 1 - x if k & 4 else x
            py = 1 - y if k & 2 else y
            pc = 1 - cc if k & 1 else cc
            peers.append((px, py, pc))
        sends = []
        for k, peer in enumerate(peers):
            cp = pltpu.make_async_remote_copy(src_ref=in_ref, dst_ref=out_ref.at[me], send_sem=send_sems.at[k],
                                              recv_sem=recv_sems.at[k], device_id=peer, device_id_type=MESH)
            cp.start()
            sends.append(cp)
        for k, (px, py, pc) in enumerate(peers):
            pltpu.make_async_remote_copy(src_ref=in_ref, dst_ref=out_ref.at[4 * px + 2 * py + pc], send_sem=send_sems.at[k],
                                         recv_sem=recv_sems.at[k], device_id=(px, py, pc), device_id_type=MESH).wait_recv()
        for cp in sends:
            cp.wait_send()
        loc.wait()

    return pl.pallas_call(
        body, name=name, in_specs=[VMEM_SPEC] + [ANY] * n_dep, out_specs=VMEM_SPEC,
        out_shape=jax.ShapeDtypeStruct((8, r, c), buf.dtype),
        scratch_shapes=[pltpu.SemaphoreType.DMA((7,)), pltpu.SemaphoreType.DMA((7,)), pltpu.SemaphoreType.DMA],
    )(buf, *deps)


def _chip_peers(x, y):
    return [(1 - x, y), (x, 1 - y), (1 - x, 1 - y)]


HBM_SPEC = pl.BlockSpec(memory_space=pltpu.HBM)
SEM_SPEC = pl.BlockSpec(memory_space=pltpu.SEMAPHORE)
EFFECT = pltpu.SideEffectType.DATAFLOW_SIDE_EFFECTING


def _hbm(a):
    return pltpu.with_memory_space_constraint(a, pltpu.HBM)


def _split_copy(u, p, peer, dst_slot, chip, land_refs, src_refs, sem_refs, cc):
    px, py = peer
    src = land_refs[u].at[chip] if src_refs is None else src_refs[u].at[2 * px + py]
    return pltpu.make_async_remote_copy(src_ref=src, dst_ref=land_refs[u].at[dst_slot], send_sem=sem_refs[2 * u].at[p],
                                        recv_sem=sem_refs[2 * u + 1].at[p], device_id=(px, py, cc), device_id_type=MESH)


def split_start(lands, srcs, after, name):
    n = len(lands)
    ops = list(lands) + (list(srcs) if srcs is not None else [])
    n_ops = len(ops)

    def body(*refs):
        land_refs = refs[:n]
        src_refs = refs[n:n_ops] if srcs is not None else None
        sem_refs = refs[n_ops + 1:n_ops + 1 + 2 * n]
        x, y, cc = _place()
        chip = 2 * x + y
        for u in range(n):
            for p, peer in enumerate(_chip_peers(x, y)):
                _split_copy(u, p, peer, chip, chip, land_refs, src_refs, sem_refs, cc).start()
        refs[-1][...] = jnp.zeros((8, 128), F32)

    outs = pl.pallas_call(
        body, name=name, in_specs=[HBM_SPEC] * n_ops + [ANY],
        out_specs=[SEM_SPEC] * (2 * n) + [HBM_SPEC] * n_ops + [VMEM_SPEC],
        out_shape=[pltpu.SemaphoreType.DMA((3,))] * (2 * n) + [pltpu.HBM(a.shape, a.dtype) for a in ops]
        + [jax.ShapeDtypeStruct((8, 128), F32)],
        input_output_aliases={k: 2 * n + k for k in range(n_ops)},
        compiler_params=pltpu.CompilerParams(has_side_effects=EFFECT),
    )(*[_hbm(a) for a in ops], after)
    sems = list(outs[:2 * n])
    thru = list(outs[2 * n:2 * n + n_ops])
    return sems, thru[:n], thru[n:], outs[-1]


def split_wait(lands, srcs, sems, after, name):
    n = len(lands)
    ops = list(lands) + (list(srcs) if srcs is not None else [])
    n_ops = len(ops)

    def body(*refs):
        land_refs = refs[:n]
        src_refs = refs[n:n_ops] if srcs is not None else None
        sem_refs = refs[n_ops:n_ops + 2 * n]
        x, y, cc = _place()
        chip = 2 * x + y
        for u in range(n):
            for p, peer in enumerate(_chip_peers(x, y)):
                cp = _split_copy(u, p, peer, 2 * peer[0] + peer[1], chip, land_refs, src_refs, sem_refs, cc)
                cp.wait_send()
                cp.wait_recv()

    outs = pl.pallas_call(
        body, name=name, in_specs=[HBM_SPEC] * n_ops + [SEM_SPEC] * (2 * n) + [ANY],
        out_specs=[HBM_SPEC] * n_ops, out_shape=[pltpu.HBM(a.shape, a.dtype) for a in ops],
        input_output_aliases={k: k for k in range(n_ops)},
        compiler_params=pltpu.CompilerParams(has_side_effects=EFFECT),
    )(*ops, *sems, after)
    return list(outs[:n]), list(outs[n:])


def _sibling_copy(k, src_refs, zone_refs, sem_refs):
    x, y, cc = _place()
    return pltpu.make_async_remote_copy(src_ref=src_refs[k], dst_ref=zone_refs[k], send_sem=sem_refs[2 * k], recv_sem=sem_refs[2 * k + 1],
                                        device_id=(x, y, 1 - cc), device_id_type=MESH)


def sibling_start(parts, name):
    n = len(parts)
    ops = list(parts) + [lax.empty(p.shape, p.dtype) for p in parts]

    def body(*refs):
        for k in range(n):
            _sibling_copy(k, refs[:n], refs[n:2 * n], refs[2 * n:4 * n]).start()

    outs = pl.pallas_call(
        body, name=name, in_specs=[HBM_SPEC] * (2 * n),
        out_specs=[SEM_SPEC] * (2 * n) + [HBM_SPEC] * (2 * n),
        out_shape=[pltpu.SemaphoreType.DMA(())] * (2 * n) + [pltpu.HBM(a.shape, a.dtype) for a in ops],
        input_output_aliases={k: 2 * n + k for k in range(2 * n)},
        compiler_params=pltpu.CompilerParams(has_side_effects=EFFECT),
    )(*[_hbm(a) for a in ops])
    return list(outs[2 * n:3 * n]), list(outs[3 * n:]), list(outs[:2 * n])


def sibling_wait(parts, zones, sems, after, name):
    n = len(parts)

    def body(*refs):
        for k in range(n):
            cp = _sibling_copy(k, refs[:n], refs[n:2 * n], refs[2 * n:4 * n])
            cp.wait_send()
            cp.wait_recv()

    outs = pl.pallas_call(
        body, name=name, in_specs=[HBM_SPEC] * (2 * n) + [SEM_SPEC] * (2 * n) + [ANY],
        out_specs=[HBM_SPEC] * (2 * n), out_shape=[pltpu.HBM(a.shape, a.dtype) for a in list(parts) + list(zones)],
        input_output_aliases={k: k for k in range(2 * n)},
        compiler_params=pltpu.CompilerParams(has_side_effects=EFFECT),
    )(*parts, *zones, *sems, after)
    return list(outs[:n]), list(outs[n:])


SMALL_ROWS = 88
FIN_ROWS = 72


def small_finish(g3, g4, c_ctx, lbp, name):
    def body(g3_ref, g4_ref, cc_ref, lbp_ref, o_ref, s_ref):
        s = g3_ref[0]
        for k in range(1, 8):
            s = s + g3_ref[k]
        s_ref[...] = s
        for i in range(DEPTH):
            o_ref[8 * i:8 * i + 8, :] = s_ref[16 * i:16 * i + 8, :] + s_ref[16 * i + 8:16 * i + 16, :]
        acc = g4_ref[0]
        for k in (2, 4, 6):
            acc = acc + g4_ref[k]
        cc = cc_ref[...]
        sg = _sigmoid(cc)
        row = jnp.sum(acc, axis=0, keepdims=True) * (sg * (1.0 + cc * (1.0 - sg)))
        o_ref[32:40, :] = jnp.broadcast_to(row, (8, D))
        o_ref[40:64, :] = s_ref[64:88, :]
        o_ref[64:72, :] = jnp.zeros((8, D), F32)
        for d in range(2):
            pp = lbp_ref[2 * d:2 * d + 1, :] * lbp_ref[2 * d + 1:2 * d + 2, :] * s_ref[75 + d:76 + d, :]
            o_ref[64 + 2 * d:65 + 2 * d, :] = -pp
            o_ref[65 + 2 * d:66 + 2 * d, :] = pp

    return pl.pallas_call(
        body, name=name, in_specs=[VMEM_SPEC] * 4, out_specs=VMEM_SPEC,
        out_shape=jax.ShapeDtypeStruct((FIN_ROWS, D), F32),
        scratch_shapes=[pltpu.VMEM((SMALL_ROWS, D), F32)],
    )(g3, g4, c_ctx, lbp)


def _pack_rows(arrs):
    flat = jnp.concatenate([a.reshape(-1) for a in arrs])
    n = -(-flat.shape[0] // (8 * D)) * 8 * D
    return jnp.pad(flat, (0, n - flat.shape[0])).reshape(n // D, D)


def _unpack_rows(packed, shapes):
    flat = packed.reshape(-1)
    outs, off = [], 0
    for s in shapes:
        size = 1
        for k in s:
            size *= k
        outs.append(flat[off:off + size].reshape(s))
        off += size
    return outs


def _pad8(a):
    return jnp.pad(a, ((0, 8 - a.shape[0]), (0, 0)))


def kernel(x, c, ctx, c_ctx, ada_w, ada_b, norm1, norm2, norm_f, mlp_w1, mlp_w2, hgrn_w_in, hgrn_lb, hgrn_gnorm, hgrn_w_out, conv_w_in, conv_w, conv_b, conv_w_out, loss_target, m_c_ctx, m_ada_w, m_ada_b, m_norm1, m_norm2, m_norm_f, m_mlp_w1, m_mlp_w2, m_hgrn_w_in, m_hgrn_lb, m_hgrn_gnorm, m_hgrn_w_out, m_conv_w_in, m_conv_w, m_conv_b, m_conv_w_out, v_c_ctx, v_ada_w, v_ada_b, v_norm1, v_norm2, v_norm_f, v_mlp_w1, v_mlp_w2, v_hgrn_w_in, v_hgrn_lb, v_hgrn_gnorm, v_hgrn_w_out, v_conv_w_in, v_conv_w, v_conv_b, v_conv_w_out):
    xi, yi, ci = _place()
    me = 4 * xi + 2 * yi + ci
    chip = 2 * xi + yi
    seq = x.shape[1]
    assert ctx.shape[1] == TM and seq % TM == 0 and (seq + TM) % TMW == 0
    n_lat = seq // TM
    sd = D // 4
    nca = ada_w.shape[2]
    xs = jnp.concatenate([x[0], ctx[0]], axis=0)

    big = [(mlp_w1, m_mlp_w1, v_mlp_w1), (mlp_w2, m_mlp_w2, v_mlp_w2), (hgrn_w_in, m_hgrn_w_in, v_hgrn_w_in),
           (hgrn_w_out, m_hgrn_w_out, v_hgrn_w_out), (conv_w_in, m_conv_w_in, v_conv_w_in), (conv_w_out, m_conv_w_out, v_conv_w_out)]
    big_names = ["w1", "w2", "hin", "hout", "cin", "cout"]
    flat2 = lambda a: a.reshape(a.shape[0] * a.shape[1], a.shape[2])
    tensors = dict(zip(big_names, big))
    chip1 = jnp.reshape(chip, (1,)).astype(jnp.int32)
    order = []
    for i in range(DEPTH):
        order += [("hin", i // 2), ("hout", i // 2)] if i % 2 == 0 else [("cin", i // 2), ("cout", i // 2)]
        order += [("w1", i), ("w2", i)]
    lands = [cast_to_slot(flat2(tensors[n][0]), idx, tensors[n][0].shape[1], chip1, f"cast_{n}_{idx}") for n, idx in order]
    sh_rows = jnp.concatenate([hgrn_lb.reshape(4, sd), conv_w.reshape(6, sd), conv_b.reshape(2, sd)], axis=0)
    buf1 = jnp.concatenate([c, jnp.pad(sh_rows, ((0, 0), (0, D - sd))), jnp.zeros((3, D), F32)], axis=0)
    g1 = small_allgather(buf1, [], "gather_small_in")
    first_sems, first_lands, _, first_token = split_start(lands[:1], None, g1, "gather_start_first")
    cvec = jnp.concatenate([g1[:, 0, :], jnp.broadcast_to(c_ctx[None], (8, D))], axis=0)
    shf = g1[0::2, 1:13, :sd].transpose(1, 0, 2).reshape(12, D)
    lb_p = jax.nn.softmax(shf[0:4].reshape(2, 2, D), axis=1)
    lower = jnp.cumsum(lb_p, axis=1) - lb_p[:, :1]
    lbs = [lower[:, 0], lower[:, 1]]
    cw8 = [_pad8(shf[4:7]), _pad8(shf[7:10])]
    cb = [shf[10:11], shf[11:12]]

    bias = lax.dynamic_slice_in_dim(ada_b, chip * nca, nca, axis=1).reshape(DEPTH, 1, nca)
    ada_part = ada_fwd(cvec, ada_w, bias, "ada_fwd")
    g2 = small_allgather(ada_part.reshape(DEPTH * 16, nca), [first_token] + lands[1:], "gather_ada")
    ada_full = g2[0::2].reshape(4, DEPTH, 16, nca).transpose(1, 2, 0, 3).reshape(DEPTH, 16, 4 * nca)
    lat = lax.dynamic_slice_in_dim(ada_full, me, 1, axis=1)[:, 0]
    mods = [jnp.stack([_pad8(lat[i].reshape(6, D)), _pad8(ada_full[i, 8].reshape(6, D))]) for i in range(DEPTH)]

    rest_sems, rest_lands, _, rest_token = split_start(lands[1:], None, g2, "gather_start")
    w_sems = first_sems + rest_sems
    lands = first_lands + rest_lands
    unit = {key: u for u, key in enumerate(order)}

    def wts(n, idx, after):
        u = unit[(n, idx)]
        if u == 0:
            after = rest_token
        (w,), _ = split_wait([lands[u]], None, w_sems[2 * u:2 * u + 2], after, f"gather_wait_{n}_{idx}")
        return w.reshape(w.shape[0] * w.shape[1], w.shape[2]) if n in ("w2", "hout", "cout") else w

    started = []

    def on_grads(i, tag, g):
        names = sorted(g)
        gs = [g[n].reshape(4, g[n].shape[0] * g[n].shape[1] // 4, g[n].shape[2]) for n in names]
        sems, zones, srcs, token = split_start([lax.empty(a.shape, BF16) for a in gs], gs, chip1, f"grad_start_{tag}_{i}")
        started.append(([(n, i if n in ("w1", "w2") else i // 2) for n in names], sems, zones, srcs))
        return token

    done = {}
    acc = {n: lax.empty(flat2(w).shape, F32) for n, (w, _, _) in tensors.items()}
    early_names = ["w1", "w2", "cin", "cout"]
    late_names = ["hin", "hout"]

    def finish_units(group, after, name):
        units = [(key, sems[2 * u:2 * u + 2], zones[u], srcs[u]) for ks, sems, zones, srcs in group for u, key in enumerate(ks)]
        zones, srcs = split_wait([u[2] for u in units], [u[3] for u in units], [s for u in units for s in u[1]], after, name)
        for (key, _, _, _), zone, own in zip(units, zones, srcs):
            acc[key[0]] = sum_slots(own, zone, acc[key[0]], key[1], chip1, f"sum_{key[0]}_{key[1]}")

    def after_backward(small, token):
        rows3 = jnp.concatenate(small["dmod"] + small["norm1"] + small["norm2"] + [small["norm_f"]] + small["gnorm"]
                                + [small["lb"][1]] + small["cw"] + small["cb"] + [jnp.tile(token[0:3], (1, D // 128))], axis=0)
        g3 = small_allgather(rows3, [], "gather_small_out")
        dmat = g3[:, :64].reshape(8, DEPTH, 2, 8, D)[:, :, :, :6].transpose(1, 2, 0, 3, 4).reshape(DEPTH, 16, 6 * D)
        dcols = lax.dynamic_slice_in_dim(dmat, chip * nca, nca, axis=2)
        *done["ada"], acc4 = ada_bwd(cvec, dcols, ada_w, m_ada_w, v_ada_w, "ada_bwd")
        g4 = small_allgather(acc4, [], "gather_cctx")
        done["fin"] = small_finish(g3, g4, c_ctx[None], _pad8(lb_p.reshape(4, D)), "small_finish")
        finish_units(list(started), done["fin"], "grad_wait_early")
        done["sib_early"] = sibling_start([acc[n] for n in early_names], "sibling_start_early")
        return done["sib_early"][0][-1]

    lane, dx, last_token = local_step(xs, loss_target[0], mods, norm1, norm2, norm_f[None], lbs, hgrn_gnorm, cw8, cb, wts, n_lat,
                                      on_grads, after_backward)
    loss = lax.psum(0.5 * jnp.sum(lane) / D, ("x", "y", "c"))
    grad_x = dx[None]
    g_ada_w, d_ada_w, nm_ada_w, nv_ada_w = done["ada"]
    fin = done["fin"]
    cols = lambda a: lax.dynamic_slice_in_dim(a, chip * sd, sd, axis=a.ndim - 1)
    small_g = [fin[32], fin[0:32].reshape(DEPTH, 8, D)[:, :6].reshape(DEPTH, 6 * D), fin[40:44], fin[44:48], fin[48], fin[49:51],
               cols(fin[64:68].reshape(2, 2, D)), cols(fin[53:59].reshape(2, 3, D)), cols(fin[59:61])]
    small_w = [c_ctx, ada_b, norm1, norm2, norm_f, hgrn_gnorm, hgrn_lb, conv_w, conv_b]
    small_m = [m_c_ctx, m_ada_b, m_norm1, m_norm2, m_norm_f, m_hgrn_gnorm, m_hgrn_lb, m_conv_w, m_conv_b]
    small_v = [v_c_ctx, v_ada_b, v_norm1, v_norm2, v_norm_f, v_hgrn_gnorm, v_hgrn_lb, v_conv_w, v_conv_b]
    shapes = [w.shape for w in small_w]
    packed = adamw([_pack_rows(small_g)], _pack_rows(small_w), _pack_rows(small_m), _pack_rows(small_v), "adamw_small")
    s_g, s_d, s_m, s_v = [_unpack_rows(p, shapes) for p in packed]

    results = {}

    def finish_tensors(names, sib, after, name):
        mine, other = sibling_wait(*sib, after, name)
        for n, pm, po in zip(names, mine, other):
            w, m, v = tensors[n]
            results[n] = [a.reshape(w.shape) for a in adamw([pm, po], flat2(w), flat2(m), flat2(v), f"adamw_{n}")]

    finish_tensors(early_names, done["sib_early"], last_token, "sibling_wait_early")
    finish_units(started[-1:], results["cout"][0], "grad_wait_late")
    sib_late = sibling_start([acc[n] for n in late_names], "sibling_start_late")
    finish_tensors(late_names, sib_late, results["cin"][0], "sibling_wait_late")
    b_g, b_d, b_m, b_v = [[results[n][k] for n in big_names] for k in range(4)]

    def ordered(s, a, b):
        return [s[0], a, s[1], s[2], s[3], s[4], b[0], b[1], b[2], s[6], s[5], b[3], b[4], s[7], s[8], b[5]]

    return (loss, grad_x, *ordered(s_g, g_ada_w, b_g), *ordered(s_d, d_ada_w, b_d), *ordered(s_m, nm_ada_w, b_m),
            *ordered(s_v, nv_ada_w, b_v))
```

```python
import functools

import jax
import jax.numpy as jnp
from jax import lax
from jax.experimental import pallas as pl
from jax.experimental.pallas import tpu as pltpu

F32 = jnp.float32
BF16 = jnp.bfloat16
MESH = pl.DeviceIdType.MESH

D = 1024
HD = 128
NH = D // HD
CH = 64
TM = 256
TMW = 2816
EPS = 1e-6
DEPTH = 4
VMEM_LIMIT = 56 * 1024 * 1024

ADAM_LR = 0.001
ADAM_B1 = 0.9
ADAM_B2 = 0.999
ADAM_EPS = 1e-08
ADAM_WD = 0.01
ADAM_STEP = 10


def _cp(n_grid):
    return pltpu.CompilerParams(dimension_semantics=("arbitrary",) * n_grid, vmem_limit_bytes=VMEM_LIMIT)


def _dot(a, b):
    return jnp.dot(a, b, preferred_element_type=F32)


def _dot_nt(a, b):
    return lax.dot_general(a, b, (((1,), (1,)), ((), ())), preferred_element_type=F32)


def _dot_tn(a, b):
    return lax.dot_general(a, b, (((0,), (0,)), ((), ())), preferred_element_type=F32)


def _sigmoid(z):
    return 1.0 / (1.0 + jnp.exp(-z))


def _norm_mod(x, gain, sh, sc):
    r = lax.rsqrt(jnp.mean(x * x, axis=-1, keepdims=True) + EPS)
    xn = x * r
    yn = xn * gain
    return r, xn, yn, yn * (1.0 + sc) + sh


def _row_spec(width):
    return pl.BlockSpec((TM, width), lambda i: (i, 0))


def _col_spec(col):
    return pl.BlockSpec((TM, D), lambda i: (i, col))


def _full_spec(shape):
    nd = len(shape)
    return pl.BlockSpec(shape, lambda i: (0,) * nd)


def _mod_spec(n_lat):
    return pl.BlockSpec((1, 8, D), lambda i: (i // n_lat, 0, 0))


def _f32(ref):
    return ref[...].astype(F32)


def proj_fwd(x, gain, mod, m0, w4, n_lat, dtype, name):
    t = x.shape[0]
    nb, _, ns = w4.shape

    def body(x_ref, gain_ref, mod_ref, w_ref, p_ref):
        _, _, _, h = _norm_mod(x_ref[...], gain_ref[...], mod_ref[0, m0:m0 + 1, :], mod_ref[0, m0 + 1:m0 + 2, :])
        hb = h.astype(BF16)
        for c in range(nb):
            p_ref[:, c * ns:(c + 1) * ns] = _dot(hb, w_ref[c]).astype(dtype)

    return pl.pallas_call(
        body, name=name, grid=(t // TM,),
        in_specs=[_row_spec(D), _full_spec((1, D)), _mod_spec(n_lat), _full_spec(w4.shape)],
        out_specs=_row_spec(nb * ns),
        out_shape=jax.ShapeDtypeStruct((t, nb * ns), dtype),
        compiler_params=_cp(1),
    )(x, gain, mod, w4)


def proj_bwd(parts, w4, x, gain, mod, m0, dx_in, n_lat, lat_only, name):
    t = x.shape[0]
    nb, _, ns = w4.shape
    n = nb * ns
    n_parts = len(parts)
    widths = [p.shape[1] for p in parts]
    offs = [sum(widths[:k]) for k in range(n_parts)]
    assert sum(widths) == n
    single = n_parts == 1

    def body(*refs):
        part_refs = refs[:n_parts]
        w_ref, x_ref, gain_ref, mod_ref, dxin_ref = refs[n_parts:n_parts + 5]
        rest = refs[n_parts + 5:]
        if single:
            dx_ref, hb_ref, acc_ref = rest
            src = part_refs[0]
        else:
            dx_ref, hb_ref, acc_ref, dpb_ref = rest
            for p_ref, off, w in zip(part_refs, offs, widths):
                dpb_ref[:, off:off + w] = p_ref[...]
            src = dpb_ref
        i = pl.program_id(0)

        @pl.when(i == 0)
        def _():
            acc_ref[...] = jnp.zeros_like(acc_ref)

        gain = gain_ref[...]
        sc = mod_ref[0, m0 + 1:m0 + 2, :]
        r, xn, yn, h = _norm_mod(x_ref[...], gain, mod_ref[0, m0:m0 + 1, :], sc)
        hb_ref[...] = h.astype(BF16)
        dh = _dot_nt(src[:, 0:ns], w_ref[0])
        for c in range(1, nb):
            dh = dh + _dot_nt(src[:, c * ns:(c + 1) * ns], w_ref[c])
        dsh = jnp.sum(dh, axis=0, keepdims=True)
        dsc = jnp.sum(dh * yn, axis=0, keepdims=True)
        dyn = dh * (1.0 + sc)
        dgain = jnp.sum(dyn * xn, axis=0, keepdims=True)
        dxn = dyn * gain
        dx = dxin_ref[...] + r * (dxn - xn * jnp.mean(dxn * xn, axis=-1, keepdims=True))
        if lat_only:
            @pl.when(i < n_lat)
            def _():
                dx_ref[...] = dx
        else:
            dx_ref[...] = dx
        latf = (i < n_lat).astype(F32)
        ctxf = 1.0 - latf
        acc_ref[0:1, :] += dgain
        acc_ref[1:2, :] += dsh * latf
        acc_ref[2:3, :] += dsc * latf
        acc_ref[3:4, :] += dsh * ctxf
        acc_ref[4:5, :] += dsc * ctxf

    dx_rows = n_lat * TM if lat_only else t
    dx_spec = pl.BlockSpec((TM, D), lambda i: (jnp.minimum(i, n_lat - 1), 0)) if lat_only else _row_spec(D)
    out_specs = [dx_spec, _row_spec(D), _full_spec((8, D))]
    out_shape = [jax.ShapeDtypeStruct((dx_rows, D), F32), jax.ShapeDtypeStruct((t, D), BF16), jax.ShapeDtypeStruct((8, D), F32)]
    if not single:
        out_specs.append(_row_spec(n))
        out_shape.append(jax.ShapeDtypeStruct((t, n), BF16))
    outs = pl.pallas_call(
        body, name=name, grid=(t // TM,),
        in_specs=[_row_spec(w) for w in widths]
        + [_full_spec(w4.shape), _row_spec(D), _full_spec((1, D)), _mod_spec(n_lat), _row_spec(D)],
        out_specs=out_specs, out_shape=out_shape, compiler_params=_cp(1),
    )(*parts, w4, x, gain, mod, dx_in)
    if single:
        return outs[0], outs[1], parts[0], outs[2]
    return outs[0], outs[1], outs[3], outs[2]


def dw_tn(a, b, nb, a_blocked, square_a, dep, name):
    t = a.shape[0]
    ka = a.shape[1] // nb if a_blocked else a.shape[1]
    kb = b.shape[1] if a_blocked else b.shape[1] // nb
    n_k = t // TMW

    def body(a_ref, b_ref, _, o_ref, acc):
        k = pl.program_id(1)

        @pl.when(k == 0)
        def _():
            acc[...] = jnp.zeros_like(acc)

        a = a_ref[...]
        acc[...] += _dot_tn(a * a if square_a else a, b_ref[...])

        @pl.when(k == n_k - 1)
        def _():
            o_ref[0] = acc[...].astype(BF16)

    a_spec = pl.BlockSpec((TMW, ka), (lambda j, k: (k, j)) if a_blocked else (lambda j, k: (k, 0)))
    b_spec = pl.BlockSpec((TMW, kb), (lambda j, k: (k, 0)) if a_blocked else (lambda j, k: (k, j)))
    return pl.pallas_call(
        body, name=name, grid=(nb, n_k),
        in_specs=[a_spec, b_spec, ANY],
        out_specs=pl.BlockSpec((1, ka, kb), lambda j, k: (j, 0, 0)),
        out_shape=jax.ShapeDtypeStruct((nb, ka, kb), BF16),
        scratch_shapes=[pltpu.VMEM((ka, kb), F32)],
        compiler_params=_cp(2),
    )(a, b, dep)


def outproj_fwd(prologue, extras, extra_specs, w, x, mod, m0, n_lat, name):
    t = x.shape[0]
    k = w.shape[0]
    n_extra = len(extras)

    def body(*refs):
        ex = refs[:n_extra]
        w_ref, x_ref, mod_ref, xo_ref, y_ref, ab_ref = refs[n_extra:]
        ab = prologue(pl.program_id(0), *ex).astype(BF16)
        ab_ref[...] = ab
        y = _dot(ab, w_ref[...])
        y_ref[...] = y.astype(BF16)
        xo_ref[...] = x_ref[...] + mod_ref[0, m0 + 2:m0 + 3, :] * y

    return pl.pallas_call(
        body, name=name, grid=(t // TM,),
        in_specs=list(extra_specs) + [_full_spec(w.shape), _row_spec(D), _mod_spec(n_lat)],
        out_specs=[_row_spec(D), _row_spec(D), _row_spec(k)],
        out_shape=[jax.ShapeDtypeStruct((t, D), F32), jax.ShapeDtypeStruct((t, D), BF16), jax.ShapeDtypeStruct((t, k), BF16)],
        compiler_params=_cp(1),
    )(*extras, w, x, mod)


def outproj_bwd(epilogue, extras, extra_specs, ep_out_specs, ep_out_shapes, w, dxn, y, mod, m0, n_lat, dep, name):
    t = dxn.shape[0]
    n_extra = len(extras)

    def body(*refs):
        ex = refs[:n_extra]
        w_ref, dxn_ref, y_ref, mod_ref, _, dyb_ref, acc_ref = refs[n_extra:n_extra + 7]
        ep_outs = refs[n_extra + 7:]
        i = pl.program_id(0)

        @pl.when(i == 0)
        def _():
            acc_ref[...] = jnp.zeros_like(acc_ref)

        dxv = dxn_ref[...]
        dyb = (dxv * mod_ref[0, m0 + 2:m0 + 3, :]).astype(BF16)
        dyb_ref[...] = dyb
        dg = jnp.sum(dxv * _f32(y_ref), axis=0, keepdims=True)
        latf = (i < n_lat).astype(F32)
        acc_ref[0:1, :] += dg * latf
        acc_ref[1:2, :] += dg * (1.0 - latf)
        epilogue(i, _dot_nt(dyb, w_ref[...]), ex, ep_outs, acc_ref)

    outs = pl.pallas_call(
        body, name=name, grid=(t // TM,),
        in_specs=list(extra_specs) + [_full_spec(w.shape), _row_spec(D), _row_spec(D), _mod_spec(n_lat), ANY],
        out_specs=[_row_spec(D), _full_spec((8, D))] + list(ep_out_specs),
        out_shape=[jax.ShapeDtypeStruct((t, D), BF16), jax.ShapeDtypeStruct((8, D), F32)] + list(ep_out_shapes),
        compiler_params=_cp(1),
    )(*extras, w, dxn, y, mod, dep)
    return outs[0], outs[1], outs[2:]


def mlp_fwd(x, gain, mod, w1, w2, n_lat, name):
    t = x.shape[0]
    nb, _, ns = w1.shape

    def body(x_ref, gain_ref, mod_ref, w1_ref, w2_ref, xo_ref, y_ref, rb_ref):
        x = x_ref[...]
        _, _, _, h = _norm_mod(x, gain_ref[...], mod_ref[0, 3:4, :], mod_ref[0, 4:5, :])
        hb = h.astype(BF16)
        y = None
        for c in range(nb):
            r = jnp.maximum(_dot(hb, w1_ref[c]), 0.0)
            rb_ref[:, c * ns:(c + 1) * ns] = r.astype(BF16)
            yc = _dot((r * r).astype(BF16), w2_ref[c * ns:(c + 1) * ns, :])
            y = yc if y is None else y + yc
        y_ref[...] = y.astype(BF16)
        xo_ref[...] = x + mod_ref[0, 5:6, :] * y

    return pl.pallas_call(
        body, name=name, grid=(t // TM,),
        in_specs=[_row_spec(D), _full_spec((1, D)), _mod_spec(n_lat), _full_spec(w1.shape), _full_spec(w2.shape)],
        out_specs=[_row_spec(D), _row_spec(D), _row_spec(nb * ns)],
        out_shape=[jax.ShapeDtypeStruct((t, D), F32), jax.ShapeDtypeStruct((t, D), BF16), jax.ShapeDtypeStruct((t, nb * ns), BF16)],
        compiler_params=_cp(1),
    )(x, gain, mod, w1, w2)


def mlp_bwd(dxn, y, ab, x, gain, mod, w1, w2, n_lat, dep, name):
    t = x.shape[0]
    nb, _, ns = w1.shape

    def body(dxn_ref, y_ref, rb_ref, x_ref, gain_ref, mod_ref, w1_ref, w2_ref, _, dx_ref, dyb_ref, dp_ref, hb_ref, acc_ref):
        i = pl.program_id(0)

        @pl.when(i == 0)
        def _():
            acc_ref[...] = jnp.zeros_like(acc_ref)

        dxv = dxn_ref[...]
        dyb = (dxv * mod_ref[0, 5:6, :]).astype(BF16)
        dyb_ref[...] = dyb
        dg = jnp.sum(dxv * _f32(y_ref), axis=0, keepdims=True)
        gain = gain_ref[...]
        sc = mod_ref[0, 4:5, :]
        r, xn, yn, h = _norm_mod(x_ref[...], gain, mod_ref[0, 3:4, :], sc)
        hb_ref[...] = h.astype(BF16)
        dh = None
        for c in range(nb):
            cols = slice(c * ns, (c + 1) * ns)
            da = _dot_nt(dyb, w2_ref[cols, :])
            dp = (da * (2.0 * rb_ref[:, cols].astype(F32))).astype(BF16)
            dp_ref[:, cols] = dp
            d = _dot_nt(dp, w1_ref[c])
            dh = d if dh is None else dh + d
        dsh = jnp.sum(dh, axis=0, keepdims=True)
        dsc = jnp.sum(dh * yn, axis=0, keepdims=True)
        dyn = dh * (1.0 + sc)
        dgain = jnp.sum(dyn * xn, axis=0, keepdims=True)
        dxn_ = dyn * gain
        dx_ref[...] = dxv + r * (dxn_ - xn * jnp.mean(dxn_ * xn, axis=-1, keepdims=True))
        latf = (i < n_lat).astype(F32)
        ctxf = 1.0 - latf
        acc_ref[0:1, :] += dgain
        acc_ref[1:2, :] += dsh * latf
        acc_ref[2:3, :] += dsc * latf
        acc_ref[3:4, :] += dsh * ctxf
        acc_ref[4:5, :] += dsc * ctxf
        acc_ref[5:6, :] += dg * latf
        acc_ref[6:7, :] += dg * ctxf

    return pl.pallas_call(
        body, name=name, grid=(t // TM,),
        in_specs=[_row_spec(D), _row_spec(D), _row_spec(nb * ns), _row_spec(D), _full_spec((1, D)), _mod_spec(n_lat),
                  _full_spec(w1.shape), _full_spec(w2.shape), ANY],
        out_specs=[_row_spec(D), _row_spec(D), _row_spec(nb * ns), _row_spec(D), _full_spec((8, D))],
        out_shape=[jax.ShapeDtypeStruct((t, D), F32), jax.ShapeDtypeStruct((t, D), BF16), jax.ShapeDtypeStruct((t, nb * ns), BF16),
                   jax.ShapeDtypeStruct((t, D), BF16), jax.ShapeDtypeStruct((8, D), F32)],
        compiler_params=_cp(1),
    )(dxn, y, ab, x, gain, mod, w1, w2, dep)


def readout_prologue(i, o0_ref, o1_ref, gate_ref, gn_ref):
    o = _f32(o0_ref) + _f32(o1_ref)
    gate = gate_ref[...]
    w = gn_ref[...] * (gate * _sigmoid(gate))
    pieces = []
    for h in range(NH):
        sl = slice(h * HD, (h + 1) * HD)
        oh = o[:, sl]
        pieces.append(oh * lax.rsqrt(jnp.mean(oh * oh, axis=-1, keepdims=True) + EPS) * w[:, sl])
    return jnp.concatenate(pieces, axis=1)


def readout_epilogue(i, da, ex, outs, acc_ref):
    o0_ref, o1_ref, gate_ref, gn_ref = ex
    do_ref, dgate_ref = outs
    o = _f32(o0_ref) + _f32(o1_ref)
    gate = gate_ref[...]
    gn = gn_ref[...]
    sg = _sigmoid(gate)
    silu = gate * sg
    dsilu = sg * (1.0 + gate * (1.0 - sg))
    for h in range(NH):
        sl = slice(h * HD, (h + 1) * HD)
        oh = o[:, sl]
        r = lax.rsqrt(jnp.mean(oh * oh, axis=-1, keepdims=True) + EPS)
        nh = oh * r
        dah = da[:, sl]
        acc_ref[2:3, sl] += jnp.sum(dah * nh * silu[:, sl], axis=0, keepdims=True)
        dgate_ref[:, sl] = (dah * nh * gn[:, sl] * dsilu[:, sl]).astype(BF16)
        dn = dah * gn[:, sl] * silu[:, sl]
        do_ref[:, sl] = r * (dn - nh * jnp.mean(dn * nh, axis=-1, keepdims=True))


def _seg_masks(i, n_lat):
    rows = lax.broadcasted_iota(jnp.int32, (TM, 1), 0)
    latf = (i < n_lat).astype(F32)
    ctxf = 1.0 - latf
    prev_ok = (rows % CH != 0).astype(F32) * latf + (rows != 0).astype(F32) * ctxf
    next_ok = (rows % CH != CH - 1).astype(F32) * latf + (rows != TM - 1).astype(F32) * ctxf
    return prev_ok, next_ok


def _shifts(i, n_lat, sft, cur, halo_prev, halo_next):
    if sft == 1:
        prev_ok, next_ok = _seg_masks(i, n_lat)
        return pltpu.roll(cur, 1, 0) * prev_ok, pltpu.roll(cur, TM - 1, 0) * next_ok
    has_prev = jnp.logical_and(i > 0, i < n_lat).astype(F32)
    has_next = (i < n_lat - 1).astype(F32)
    prev = jnp.concatenate([halo_prev * has_prev, cur[:TM - CH]], axis=0)
    nxt = jnp.concatenate([cur[CH:], halo_next * has_next], axis=0)
    return prev, nxt


def _conv_u(sft, ex):
    if sft == 1:
        gb_ref, gc_ref, xi_ref, cw_ref, cb_ref = ex
        return gb_ref, _f32(gc_ref) * _f32(xi_ref), None, None, cw_ref, cb_ref
    gb_ref, gc_ref, xi_ref, gcp_ref, xip_ref, gcn_ref, xin_ref, cw_ref, cb_ref = ex
    return gb_ref, _f32(gc_ref) * _f32(xi_ref), _f32(gcp_ref) * _f32(xip_ref), _f32(gcn_ref) * _f32(xin_ref), cw_ref, cb_ref


def _conv_value(i, n_lat, sft, ex):
    gb_ref, u, up, un, cw_ref, cb_ref = _conv_u(sft, ex)
    u_prev, u_next = _shifts(i, n_lat, sft, u, up, un)
    return gb_ref, cb_ref[...] + cw_ref[0:1, :] * u_prev + cw_ref[1:2, :] * u + cw_ref[2:3, :] * u_next


def make_conv_prologue(n_lat, sft):
    def prologue(i, *ex):
        gb_ref, conv = _conv_value(i, n_lat, sft, ex)
        return _f32(gb_ref) * conv
    return prologue


def make_conv_epilogue(n_lat, sft):
    def epilogue(i, da, ex, outs, acc_ref):
        gb_ref, conv = _conv_value(i, n_lat, sft, ex)
        outs[0][...] = (da * _f32(gb_ref)).astype(BF16)
        outs[1][...] = (da * conv).astype(BF16)
    return epilogue


def _conv_specs(sft, t):
    specs = [_col_spec(0), _col_spec(1), _col_spec(2)]
    if sft != 1:
        per = TM // CH
        last = t // CH - 1
        for fn in (lambda i: jnp.maximum(i * per - 1, 0), lambda i: jnp.minimum(i * per + per, last)):
            for col in (1, 2):
                specs.append(pl.BlockSpec((CH, D), functools.partial(lambda i, f, c: (f(i), c), f=fn, c=col)))
    return specs + [_full_spec((8, D)), _full_spec((1, D))]


def _conv_args(sft, p, cw8, cb):
    return [p] * (3 if sft == 1 else 7) + [cw8, cb]


def conv_bwd(dconv, p, cw8, sft, n_lat, name):
    t = dconv.shape[0]
    halo = sft != 1

    def body(*refs):
        if halo:
            dc_ref, dcp_ref, dcn_ref, gc_ref, xi_ref, gcp_ref, xip_ref, gcn_ref, xin_ref, cw_ref, dgc_ref, dxi_ref, acc_ref = refs
            up, un = _f32(gcp_ref) * _f32(xip_ref), _f32(gcn_ref) * _f32(xin_ref)
            dcp, dcn = _f32(dcp_ref), _f32(dcn_ref)
        else:
            dc_ref, gc_ref, xi_ref, cw_ref, dgc_ref, dxi_ref, acc_ref = refs
            up = un = dcp = dcn = None
        i = pl.program_id(0)

        @pl.when(i == 0)
        def _():
            acc_ref[...] = jnp.zeros_like(acc_ref)

        gc = _f32(gc_ref)
        xi = _f32(xi_ref)
        u = gc * xi
        dc = _f32(dc_ref)
        u_prev, u_next = _shifts(i, n_lat, sft, u, up, un)
        dc_prev, dc_next = _shifts(i, n_lat, sft, dc, dcp, dcn)
        acc_ref[0:1, :] += jnp.sum(dc * u_prev, axis=0, keepdims=True)
        acc_ref[1:2, :] += jnp.sum(dc * u, axis=0, keepdims=True)
        acc_ref[2:3, :] += jnp.sum(dc * u_next, axis=0, keepdims=True)
        acc_ref[3:4, :] += jnp.sum(dc, axis=0, keepdims=True)
        du = cw_ref[0:1, :] * dc_next + cw_ref[1:2, :] * dc + cw_ref[2:3, :] * dc_prev
        dgc_ref[...] = (du * xi).astype(BF16)
        dxi_ref[...] = (du * gc).astype(BF16)

    per = TM // CH
    last = t // CH - 1
    prev_i = lambda i: jnp.maximum(i * per - 1, 0)
    next_i = lambda i: jnp.minimum(i * per + per, last)
    if halo:
        in_specs = [_row_spec(D), pl.BlockSpec((CH, D), lambda i: (prev_i(i), 0)), pl.BlockSpec((CH, D), lambda i: (next_i(i), 0)),
                    _col_spec(1), _col_spec(2),
                    pl.BlockSpec((CH, D), lambda i: (prev_i(i), 1)), pl.BlockSpec((CH, D), lambda i: (prev_i(i), 2)),
                    pl.BlockSpec((CH, D), lambda i: (next_i(i), 1)), pl.BlockSpec((CH, D), lambda i: (next_i(i), 2)),
                    _full_spec((8, D))]
        args = [dconv, dconv, dconv, p, p, p, p, p, p, cw8]
    else:
        in_specs = [_row_spec(D), _col_spec(1), _col_spec(2), _full_spec((8, D))]
        args = [dconv, p, p, cw8]
    return pl.pallas_call(
        body, name=name, grid=(t // TM,), in_specs=in_specs,
        out_specs=[_row_spec(D), _row_spec(D), _full_spec((8, D))],
        out_shape=[jax.ShapeDtypeStruct((t, D), BF16), jax.ShapeDtypeStruct((t, D), BF16), jax.ShapeDtypeStruct((8, D), F32)],
        compiler_params=_cp(1),
    )(*args)


LOG2E = 1.4426950408889634


def _cumsum_matrix(reverse):
    r = lax.broadcasted_iota(jnp.int32, (CH, CH), 0)
    c = lax.broadcasted_iota(jnp.int32, (CH, CH), 1)
    return (r <= c if reverse else r >= c).astype(BF16)


def _chunk_cumsum(g, tri):
    hi = g.astype(BF16)
    lo = (g - hi.astype(F32)).astype(BF16)
    return _dot(tri, hi) + _dot(tri, lo)


def _gate_values(z, lb):
    sig = _sigmoid(z)
    f = lb + (1.0 - lb) * sig
    return sig, f


def _tri(direction, transposed):
    r = lax.broadcasted_iota(jnp.int32, (CH, CH), 0)
    c = lax.broadcasted_iota(jnp.int32, (CH, CH), 1)
    lower = (direction == 0) != transposed
    return r >= c if lower else r <= c


def _gla_rows(direction):
    return (CH // 2 - 1, CH - 1) if direction == 0 else (CH // 2, 0)


def _fwd_tile(direction, nt):
    return (lambda i: (i + nt - 1) % nt) if direction == 0 else (lambda i: nt - 1 - i)


def gla_fwd(p, lb2, name):
    t = p.shape[0]
    nt = t // TM
    per = TM // CH

    def body(z0_ref, v0_ref, q0_ref, z1_ref, v1_ref, q1_ref, lb_ref, o0_ref, s0_ref, o1_ref, s1_ref, st, q_s, k_s, c_s):
        @pl.when(pl.program_id(0) == 0)
        def _():
            st[...] = jnp.zeros_like(st)

        ins = ((z0_ref, v0_ref, q0_ref, o0_ref, s0_ref), (z1_ref, v1_ref, q1_ref, o1_ref, s1_ref))
        for d in range(2):
            z_ref, _, qr_ref, _, _ = ins[d]
            tri = _cumsum_matrix(d == 1)
            lb = lb_ref[d:d + 1, :]
            for ci in range(per):
                rows = slice(ci * CH, (ci + 1) * CH)
                _, f = _gate_values(z_ref[rows, :], lb)
                k_s[d, rows, :] = 1.0 - f
                c_s[d, rows, :] = _chunk_cumsum(jnp.log(f) * LOG2E, tri)
                qr = qr_ref[rows, :]
                q_s[d, rows, :] = qr * _sigmoid(qr)
        masks = (_tri(0, False), _tri(1, False))
        state = [[st[d, h] for h in range(NH)] for d in range(2)]
        for it in range(per):
            chunk = []
            for d in range(2):
                ref_row, last_row = _gla_rows(d)
                ci = it if d == 0 else per - 1 - it
                r0 = ci * CH
                rows = slice(r0, r0 + CH)
                cum = c_s[d, rows, :]
                ref = c_s[d, r0 + ref_row:r0 + ref_row + 1, :]
                last = c_s[d, r0 + last_row:r0 + last_row + 1, :]
                q = q_s[d, rows, :]
                k = k_s[d, rows, :]
                chunk.append(dict(
                    ci=ci, rows=rows, qh=(q * jnp.exp2(cum)).astype(BF16), qt=(q * jnp.exp2(cum - ref)).astype(BF16),
                    kt=(k * jnp.exp2(ref - cum)).astype(BF16), kb=(k * jnp.exp2(last - cum)).astype(BF16),
                    el=jnp.exp2(last), vb=ins[d][1][rows, :].astype(BF16)))
            heads = [slice(h * HD, (h + 1) * HD) for h in range(NH)]
            first = {}
            for h, sl in enumerate(heads):
                for d in range(2):
                    c = chunk[d]
                    s_t = state[d][h]
                    ins[d][4][c["ci"], h] = s_t
                    first[d, h] = (jnp.where(masks[d], _dot_nt(c["qt"][:, sl], c["kt"][:, sl]), 0.0).astype(BF16),
                                   _dot_nt(c["qh"][:, sl], s_t.astype(BF16)),
                                   s_t * c["el"][:, sl] + _dot_tn(c["vb"][:, sl], c["kb"][:, sl]))
            for h, sl in enumerate(heads):
                for d in range(2):
                    c = chunk[d]
                    sc, o_inter, s_new = first[d, h]
                    ins[d][3][c["rows"], sl] = (o_inter + _dot(sc, c["vb"][:, sl])).astype(BF16)
                    state[d][h] = s_new
        for d in range(2):
            for h in range(NH):
                st[d, h] = state[d][h]

    tiles = (_fwd_tile(0, nt), _fwd_tile(1, nt))
    tspec = lambda d, col: pl.BlockSpec((TM, D), lambda i: (tiles[d](i), col))
    sspec = lambda d: pl.BlockSpec((per, NH, HD, HD), lambda i: (tiles[d](i), 0, 0, 0))
    o_shape = jax.ShapeDtypeStruct((t, D), BF16)
    s_shape = jax.ShapeDtypeStruct((t // CH, NH, HD, HD), F32)
    return pl.pallas_call(
        body, name=name, grid=(nt,),
        in_specs=[tspec(0, 0), tspec(0, 2), tspec(0, 3), tspec(1, 1), tspec(1, 2), tspec(1, 3), _full_spec((2, D))],
        out_specs=[tspec(0, 0), sspec(0), tspec(1, 0), sspec(1)],
        out_shape=[o_shape, s_shape, o_shape, s_shape],
        scratch_shapes=[pltpu.VMEM((2, NH, HD, HD), F32)] + [pltpu.VMEM((2, TM, D), F32)] * 3,
        compiler_params=_cp(1),
    )(p, p, p, p, p, p, lb2)


def gla_bwd(p, lb2, do, states, direction, prev, name):
    t = p.shape[0]
    nt = t // TM
    per = TM // CH
    ref_row, last_row = _gla_rows(direction)
    tile = (lambda i: (2 * nt - 2 - i) % nt) if direction == 0 else (lambda i: i)
    final = prev is not None
    n_in = 8 if final else 6

    def body(*refs):
        z_ref, v_ref, qr_ref, lb_ref, do_ref, s_ref = refs[:6]
        dz_ref, dv_ref, dq_ref, acc_ref, dst, q_s, k_s, c_s, dq_s, dk_s, dl_s = refs[n_in:]

        @pl.when(pl.program_id(0) == 0)
        def _():
            dst[...] = jnp.zeros_like(dst)
            acc_ref[...] = jnp.zeros_like(acc_ref)

        mask = _tri(direction, False)
        mask_t = _tri(direction, True)
        tri = _cumsum_matrix(direction == 1)
        tri_t = _cumsum_matrix(direction == 0)
        is_last = lax.broadcasted_iota(jnp.int32, (CH, 1), 0) == last_row
        lb = lb_ref[direction:direction + 1, :]
        for ci in range(per):
            rows = slice(ci * CH, (ci + 1) * CH)
            _, f = _gate_values(z_ref[rows, :], lb)
            k_s[rows, :] = 1.0 - f
            c_s[rows, :] = _chunk_cumsum(jnp.log(f) * LOG2E, tri)
            qr = qr_ref[rows, :]
            q_s[rows, :] = qr * _sigmoid(qr)
        state = [dst[h] for h in range(NH)]
        for it in range(per):
            ci = per - 1 - it if direction == 0 else it
            r0 = ci * CH
            rows = slice(r0, r0 + CH)
            cum = c_s[rows, :]
            ref = c_s[r0 + ref_row:r0 + ref_row + 1, :]
            last = c_s[r0 + last_row:r0 + last_row + 1, :]
            q = q_s[rows, :]
            k = k_s[rows, :]
            e_h = jnp.exp2(cum)
            e_t = jnp.exp2(cum - ref)
            e_kt = jnp.exp2(ref - cum)
            e_kb = jnp.exp2(last - cum)
            el = jnp.exp2(last)
            qh = (q * e_h).astype(BF16)
            qt = (q * e_t).astype(BF16)
            kt = (k * e_kt).astype(BF16)
            kbf = k * e_kb
            kb = kbf.astype(BF16)
            vb = v_ref[rows, :].astype(BF16)
            dob = do_ref[rows, :].astype(BF16)
            heads = [slice(h * HD, (h + 1) * HD) for h in range(NH)]
            first = []
            for h, sl in enumerate(heads):
                s_t = s_ref[ci, h]
                ds_t = state[h]
                ds_b = ds_t.astype(BF16)
                d_a = jnp.where(mask, _dot_nt(dob[:, sl], vb[:, sl]), 0.0).astype(BF16)
                a_t = jnp.where(mask_t, _dot_nt(kt[:, sl], qt[:, sl]), 0.0).astype(BF16)
                d_at = jnp.where(mask_t, _dot_nt(vb[:, sl], dob[:, sl]), 0.0).astype(BF16)
                dkb = _dot(vb[:, sl], ds_b)
                dl_s[it:it + 1, sl] = (el[:, sl] * jnp.sum(ds_t * s_t, axis=0, keepdims=True)
                                       + jnp.sum(dkb * kbf[:, sl], axis=0, keepdims=True))
                state[h] = ds_t * el[:, sl] + _dot_tn(dob[:, sl], qh[:, sl])
                first.append((d_a, a_t, d_at, dkb, _dot_nt(kb[:, sl], ds_b), _dot(dob[:, sl], s_t.astype(BF16))))
            for h, sl in enumerate(heads):
                d_a, a_t, d_at, dkb, dv_state, dq_state = first[h]
                dv = _dot(a_t, dob[:, sl]) + dv_state
                dq_s[rows, sl] = dq_state * e_h[:, sl] + _dot(d_a, kt[:, sl]) * e_t[:, sl]
                dk_s[rows, sl] = _dot(d_at, qt[:, sl]) * e_kt[:, sl] + dkb * e_kb[:, sl]
                if final:
                    dv_ref[rows, sl] = (refs[6][rows, sl] + dv).astype(BF16)
                else:
                    dv_ref[rows, sl] = dv
        for h in range(NH):
            dst[h] = state[h]
        for it in range(per):
            ci = per - 1 - it if direction == 0 else it
            rows = slice(ci * CH, (ci + 1) * CH)
            dq = dq_s[rows, :]
            dk = dk_s[rows, :]
            dg = _chunk_cumsum(dq * q_s[rows, :] - dk * k_s[rows, :] + jnp.where(is_last, dl_s[it:it + 1, :], 0.0), tri_t)
            sig, f = _gate_values(z_ref[rows, :], lb)
            df = dg / f - dk
            acc_ref[0:1, :] += jnp.sum(df * (1.0 - sig), axis=0, keepdims=True)
            dz_ref[rows, :] = (df * (1.0 - lb) * sig * (1.0 - sig)).astype(BF16)
            if final:
                qr = qr_ref[rows, :]
                sq = _sigmoid(qr)
                dq_ref[rows, :] = ((refs[7][rows, :] + dq) * (sq * (1.0 + qr * (1.0 - sq)))).astype(BF16)
            else:
                dq_ref[rows, :] = dq

    tspec = lambda col: pl.BlockSpec((TM, D), lambda i: (tile(i), col))
    sspec = pl.BlockSpec((per, NH, HD, HD), lambda i: (tile(i), 0, 0, 0))
    in_specs = [tspec(direction), tspec(2), tspec(3), _full_spec((2, D)), tspec(0), sspec]
    args = [p, p, p, lb2, do, states]
    if final:
        in_specs += [tspec(0), tspec(0)]
        args += list(prev)
    odt = BF16 if final else F32
    return pl.pallas_call(
        body, name=name, grid=(nt,), in_specs=in_specs,
        out_specs=[tspec(0), tspec(0), tspec(0), _full_spec((8, D))],
        out_shape=[jax.ShapeDtypeStruct((t, D), BF16), jax.ShapeDtypeStruct((t, D), odt), jax.ShapeDtypeStruct((t, D), odt),
                   jax.ShapeDtypeStruct((8, D), F32)],
        scratch_shapes=[pltpu.VMEM((NH, HD, HD), F32)] + [pltpu.VMEM((TM, D), F32)] * 5 + [pltpu.VMEM((8, D), F32)],
        compiler_params=_cp(1),
    )(*args)


def loss_bwd(x, gain, target, n_lat, name):
    t = x.shape[0]

    def body(x_ref, gain_ref, tg_ref, dx_ref, acc_ref):
        i = pl.program_id(0)

        @pl.when(i == 0)
        def _():
            acc_ref[...] = jnp.zeros_like(acc_ref)

        latf = (i < n_lat).astype(F32)
        x = x_ref[...]
        gain = gain_ref[...]
        r = lax.rsqrt(jnp.mean(x * x, axis=-1, keepdims=True) + EPS)
        xn = x * r
        err = (xn * gain - tg_ref[...]) * latf
        dy = err * (1.0 / D)
        dxn = dy * gain
        dx_ref[...] = r * (dxn - xn * jnp.mean(dxn * xn, axis=-1, keepdims=True))
        acc_ref[0:1, :] += jnp.sum(dy * xn, axis=0, keepdims=True)
        acc_ref[1:2, :] += jnp.sum(err * err, axis=0, keepdims=True)

    return pl.pallas_call(
        body, name=name, grid=(t // TM,),
        in_specs=[_row_spec(D), _full_spec((1, D)), pl.BlockSpec((TM, D), lambda i: (jnp.minimum(i, n_lat - 1), 0))],
        out_specs=[_row_spec(D), _full_spec((8, D))],
        out_shape=[jax.ShapeDtypeStruct((t, D), F32), jax.ShapeDtypeStruct((8, D), F32)],
        compiler_params=_cp(1),
    )(x, gain, target)


def local_step(xs, target, mods, norm1, norm2, norm_f, lbs, gnorm, cw8, cb, wts, n_lat, on_grads, after_backward):
    t = xs.shape[0]
    saved = []
    cache = {}

    def W(name, idx, after=None):
        if (name, idx) not in cache:
            cache[(name, idx)] = wts(name, idx, after)
        return cache[(name, idx)]

    x = xs
    for i in range(DEPTH):
        j = i // 2
        rec = i % 2 == 0
        n1 = norm1[i:i + 1]
        n2 = norm2[i:i + 1]
        s = {"x_in": x}
        if rec:
            p = proj_fwd(x, n1, mods[i], 0, W("hin", j, x), n_lat, F32, f"hin_fwd_{i}")
            o0, st0, o1, st1 = gla_fwd(p, lbs[j], f"gla_fwd_{i}")
            ex = [o0, o1, p, gnorm[j:j + 1]]
            ex_specs = [_row_spec(D), _row_spec(D), _col_spec(4), _full_spec((1, D))]
            xm, y, ab = outproj_fwd(readout_prologue, ex, ex_specs, W("hout", j, o1), x, mods[i], 0, n_lat, f"hout_fwd_{i}")
            s.update(st0=st0, st1=st1)
        else:
            sft = 1 if j % 2 == 0 else CH
            p = proj_fwd(x, n1, mods[i], 0, W("cin", j, x), n_lat, BF16, f"cin_fwd_{i}")
            ex = _conv_args(sft, p, cw8[j], cb[j])
            ex_specs = _conv_specs(sft, t)
            xm, y, ab = outproj_fwd(make_conv_prologue(n_lat, sft), ex, ex_specs, W("cout", j, p), x, mods[i], 0, n_lat, f"cout_fwd_{i}")
        s.update(p=p, ex=ex, ex_specs=ex_specs, y_mix=y, ab_mix=ab, x_mid=xm)
        x, y2, ab2 = mlp_fwd(xm, n2, mods[i], W("w1", i, xm), W("w2", i, xm), n_lat, f"mlp_fwd_{i}")
        s.update(y_mlp=y2, ab_mlp=ab2)
        saved.append(s)

    dx, acc_loss = loss_bwd(x, norm_f, target, n_lat, "loss")
    small = {"norm_f": acc_loss[0:1], "norm1": [None] * DEPTH, "norm2": [None] * DEPTH, "dmod": [None] * DEPTH,
             "gnorm": [None] * 2, "lb": [None] * 2, "cw": [None] * 2, "cb": [None] * 2}
    bshape = lambda w: jax.ShapeDtypeStruct((t, w), BF16)
    token = jnp.zeros((8, 128), F32)
    for i in reversed(range(DEPTH)):
        j = i // 2
        rec = i % 2 == 0
        s = saved[i]
        n1 = norm1[i:i + 1]
        n2 = norm2[i:i + 1]
        dx, dyb, dp1, hb, acc_n2 = mlp_bwd(dx, s["y_mlp"], s["ab_mlp"], s["x_mid"], n2, mods[i], W("w1", i), W("w2", i), n_lat, token,
                                           f"mlp_bwd_{i}")
        token = on_grads(i, "mlp", {"w2": dw_tn(s["ab_mlp"], dyb, 4, True, True, token, f"w2_dw_{i}"),
                                    "w1": dw_tn(hb, dp1, 4, False, False, token, f"w1_dw_{i}")})
        if rec:
            dyb, acc_g1, (do, dgate) = outproj_bwd(
                readout_epilogue, s["ex"], s["ex_specs"], [_row_spec(D), _row_spec(D)],
                [jax.ShapeDtypeStruct((t, D), F32), bshape(D)], W("hout", j), dx, s["y_mix"], mods[i], 0, n_lat, token, f"hout_bwd_{i}")
            dz0, dv0, dq0, acc_l0 = gla_bwd(s["p"], lbs[j], do, s["st0"], 0, None, f"gla_bwd0_{i}")
            dz1, dv, dq, acc_l1 = gla_bwd(s["p"], lbs[j], do, s["st1"], 1, (dv0, dq0), f"gla_bwd1_{i}")
            dx, hb, dpb, acc_n1 = proj_bwd([dz0, dz1, dv, dq, dgate], W("hin", j), s["x_in"], n1, mods[i], 0, dx, n_lat, i == 0,
                                           f"hin_bwd_{i}")
            small["gnorm"][j] = acc_g1[2:3]
            small["lb"][j] = jnp.concatenate([acc_l0[0:1], acc_l1[0:1]], axis=0)
            mix = ("hout", "hin")
        else:
            sft = 1 if j % 2 == 0 else CH
            dyb, acc_g1, (dconv, dgb) = outproj_bwd(
                make_conv_epilogue(n_lat, sft), s["ex"], s["ex_specs"], [_row_spec(D), _row_spec(D)],
                [bshape(D), bshape(D)], W("cout", j), dx, s["y_mix"], mods[i], 0, n_lat, token, f"cout_bwd_{i}")
            dgc, dxi, acc_c = conv_bwd(dconv, s["p"], cw8[j], sft, n_lat, f"conv_bwd_{i}")
            dx, hb, dpb, acc_n1 = proj_bwd([dgb, dgc, dxi], W("cin", j), s["x_in"], n1, mods[i], 0, dx, n_lat, False, f"cin_bwd_{i}")
            small["cw"][j] = acc_c[0:3]
            small["cb"][j] = acc_c[3:4]
            mix = ("cout", "cin")
        small["norm1"][i] = acc_n1[0:1]
        small["norm2"][i] = acc_n2[0:1]
        z2 = jnp.zeros((2, D), F32)
        small["dmod"][i] = jnp.concatenate([acc_n1[1:3], acc_g1[0:1], acc_n2[1:3], acc_n2[5:6], z2,
                                            acc_n1[3:5], acc_g1[1:2], acc_n2[3:5], acc_n2[6:7], z2], axis=0)
        if i == 0:
            token = after_backward(small, token)
        token = on_grads(i, "mix", {mix[0]: dw_tn(s["ab_mix"], dyb, 1, False, False, token, f"{mix[0]}_dw_{i}"),
                                    mix[1]: dw_tn(hb, dpb, 4, False, False, token, f"{mix[1]}_dw_{i}")})
    return acc_loss[1:2], dx, token


RB = 256


def cast_to_slot(w2d, layer, k, chip1, name):
    c = w2d.shape[1]
    nblk = k // RB

    def body(chip_ref, w_ref, o_ref):
        o_ref[0] = w_ref[...].astype(BF16)

    return pl.pallas_call(
        body, name=name,
        grid_spec=pltpu.PrefetchScalarGridSpec(
            num_scalar_prefetch=1, grid=(nblk,),
            in_specs=[pl.BlockSpec((RB, c), lambda i, ch: (layer * nblk + i, 0))],
            out_specs=pl.BlockSpec((1, RB, c), lambda i, ch: (ch[0], i, 0))),
        out_shape=jax.ShapeDtypeStruct((4, k, c), BF16), compiler_params=_cp(1))(chip1, w2d)


def sum_slots(own, land, acc, layer, chip1, name):
    _, k, c = own.shape
    nblk = k // RB

    def body(chip_ref, own_ref, l1_ref, l2_ref, l3_ref, acc_ref, o_ref):
        o_ref[...] = ((own_ref[0].astype(F32) + l1_ref[0].astype(F32)) + l2_ref[0].astype(F32)) + l3_ref[0].astype(F32)

    slot = lambda d: pl.BlockSpec((1, RB, c), lambda i, ch: ((ch[0] + d) % 4, i, 0))
    return pl.pallas_call(
        body, name=name,
        grid_spec=pltpu.PrefetchScalarGridSpec(
            num_scalar_prefetch=1, grid=(nblk,),
            in_specs=[slot(0), slot(1), slot(2), slot(3), ANY],
            out_specs=pl.BlockSpec((RB, c), lambda i, ch: (layer * nblk + i, 0))),
        out_shape=jax.ShapeDtypeStruct(acc.shape, F32), input_output_aliases={5: 0}, compiler_params=_cp(1),
    )(chip1, own, land, land, land, acc)


def _adamw_math(w, g, m, v):
    m = ADAM_B1 * m + (1.0 - ADAM_B1) * g
    v = ADAM_B2 * v + (1.0 - ADAM_B2) * jnp.square(g)
    m_hat = m / (1.0 - ADAM_B1 ** ADAM_STEP)
    v_hat = v / (1.0 - ADAM_B2 ** ADAM_STEP)
    delta = -ADAM_LR * (m_hat / (jnp.sqrt(v_hat) + ADAM_EPS) + ADAM_WD * w)
    return delta, m, v


def adamw(gsrcs, w, m, v, name):
    r, c = w.shape
    rb = RB if r % RB == 0 else r
    n_g = len(gsrcs)

    def body(*refs):
        g = refs[0][...]
        for k in range(1, n_g):
            g = g + refs[k][...]
        w_ref, m_ref, v_ref, g_ref, d_ref, mo_ref, vo_ref = refs[n_g:]
        delta, mo, vo = _adamw_math(w_ref[...], g, m_ref[...], v_ref[...])
        g_ref[...] = g
        d_ref[...] = delta
        mo_ref[...] = mo
        vo_ref[...] = vo

    spec = pl.BlockSpec((rb, c), lambda i: (i, 0))
    shp = jax.ShapeDtypeStruct((r, c), F32)
    return pl.pallas_call(body, name=name, grid=(r // rb,), in_specs=[spec] * (n_g + 3), out_specs=[spec] * 4,
                          out_shape=[shp] * 4, compiler_params=_cp(1))(*gsrcs, w, m, v)


ADA_CB = 512


def ada_fwd(cvec, ada_w, bias, name):
    _, _, nc = ada_w.shape

    def body(c_ref, w_ref, b_ref, o_ref):
        cv = c_ref[...]
        a = (cv * _sigmoid(cv)).astype(BF16)
        o_ref[0] = _dot(a, w_ref[0].astype(BF16)) + b_ref[0]

    return pl.pallas_call(
        body, name=name, grid=(DEPTH, nc // ADA_CB),
        in_specs=[pl.BlockSpec((16, D), lambda i, j: (0, 0)), pl.BlockSpec((1, D, ADA_CB), lambda i, j: (i, 0, j)),
                  pl.BlockSpec((1, 1, ADA_CB), lambda i, j: (i, 0, j))],
        out_specs=pl.BlockSpec((1, 16, ADA_CB), lambda i, j: (i, 0, j)),
        out_shape=jax.ShapeDtypeStruct((DEPTH, 16, nc), F32), compiler_params=_cp(2),
    )(cvec, ada_w, bias)


def ada_bwd(cvec, dcols, ada_w, m, v, name):
    _, _, nc = ada_w.shape

    def body(c_ref, d_ref, w_ref, m_ref, v_ref, g_ref, dl_ref, mo_ref, vo_ref, acc_ref):
        @pl.when(jnp.logical_and(pl.program_id(0) == 0, pl.program_id(1) == 0))
        def _():
            acc_ref[...] = jnp.zeros_like(acc_ref)

        cv = c_ref[...]
        a = (cv * _sigmoid(cv)).astype(BF16)
        db = d_ref[0].astype(BF16)
        w = w_ref[0]
        g = _dot_tn(a, db)
        delta, mo, vo = _adamw_math(w, g, m_ref[0], v_ref[0])
        g_ref[0] = g
        dl_ref[0] = delta
        mo_ref[0] = mo
        vo_ref[0] = vo
        acc_ref[...] += _dot_nt(db[8:16, :], w.astype(BF16))

    wspec = pl.BlockSpec((1, D, ADA_CB), lambda i, j: (i, 0, j))
    wshape = jax.ShapeDtypeStruct(ada_w.shape, F32)
    return pl.pallas_call(
        body, name=name, grid=(DEPTH, nc // ADA_CB),
        in_specs=[pl.BlockSpec((16, D), lambda i, j: (0, 0)), pl.BlockSpec((1, 16, ADA_CB), lambda i, j: (i, 0, j)), wspec, wspec, wspec],
        out_specs=[wspec, wspec, wspec, wspec, pl.BlockSpec((8, D), lambda i, j: (0, 0))],
        out_shape=[wshape, wshape, wshape, wshape, jax.ShapeDtypeStruct((8, D), F32)], compiler_params=_cp(2),
    )(cvec, dcols, ada_w, m, v)


def _place():
    return lax.axis_index("x"), lax.axis_index("y"), lax.axis_index("c")


ANY = pl.BlockSpec(memory_space=pl.ANY)
VMEM_SPEC = pl.BlockSpec(memory_space=pltpu.VMEM)


def small_allgather(buf, deps, name):
    r, c = buf.shape
    n_dep = len(deps)

    def body(*refs):
        in_ref = refs[0]
        out_ref, send_sems, recv_sems, loc_sem = refs[1 + n_dep:]
        x, y, cc = _place()
        me = 4 * x + 2 * y + cc
        loc = pltpu.make_async_copy(in_ref, out_ref.at[me], loc_sem)
        loc.start()
        peers = []
        for k in range(1, 8):
            px = 1 - x if k & 4 else x
            py = 1 - y if k & 2 else y
            pc = 1 - cc if k & 1 else cc
            peers.append((px, py, pc))
        sends = []
        for k, peer in enumerate(peers):
            cp = pltpu.make_async_remote_copy(src_ref=in_ref, dst_ref=out_ref.at[me], send_sem=send_sems.at[k],
                                              recv_sem=recv_sems.at[k], device_id=peer, device_id_type=MESH)
            cp.start()
            sends.append(cp)
        for k, (px, py, pc) in enumerate(peers):
            pltpu.make_async_remote_copy(src_ref=in_ref, dst_ref=out_ref.at[4 * px + 2 * py + pc], send_sem=send_sems.at[k],
                                         recv_sem=recv_sems.at[k], device_id=(px, py, pc), device_id_type=MESH).wait_recv()
        for cp in sends:
            cp.wait_send()
        loc.wait()

    return pl.pallas_call(
        body, name=name, in_specs=[VMEM_SPEC] + [ANY] * n_dep, out_specs=VMEM_SPEC,
        out_shape=jax.ShapeDtypeStruct((8, r, c), buf.dtype),
        scratch_shapes=[pltpu.SemaphoreType.DMA((7,)), pltpu.SemaphoreType.DMA((7,)), pltpu.SemaphoreType.DMA],
    )(buf, *deps)


def _chip_peers(x, y):
    return [(1 - x, y), (x, 1 - y), (1 - x, 1 - y)]


HBM_SPEC = pl.BlockSpec(memory_space=pltpu.HBM)
SEM_SPEC = pl.BlockSpec(memory_space=pltpu.SEMAPHORE)
EFFECT = pltpu.SideEffectType.DATAFLOW_SIDE_EFFECTING


def _hbm(a):
    return pltpu.with_memory_space_constraint(a, pltpu.HBM)


def _split_copy(u, p, peer, dst_slot, chip, land_refs, src_refs, sem_refs, cc):
    px, py = peer
    src = land_refs[u].at[chip] if src_refs is None else src_refs[u].at[2 * px + py]
    return pltpu.make_async_remote_copy(src_ref=src, dst_ref=land_refs[u].at[dst_slot], send_sem=sem_refs[2 * u].at[p],
                                        recv_sem=sem_refs[2 * u + 1].at[p], device_id=(px, py, cc), device_id_type=MESH)


def split_start(lands, srcs, after, name):
    n = len(lands)
    ops = list(lands) + (list(srcs) if srcs is not None else [])
    n_ops = len(ops)

    def body(*refs):
        land_refs = refs[:n]
        src_refs = refs[n:n_ops] if srcs is not None else None
        sem_refs = refs[n_ops + 1:n_ops + 1 + 2 * n]
        x, y, cc = _place()
        chip = 2 * x + y
        for u in range(n):
            for p, peer in enumerate(_chip_peers(x, y)):
                _split_copy(u, p, peer, chip, chip, land_refs, src_refs, sem_refs, cc).start()
        refs[-1][...] = jnp.zeros((8, 128), F32)

    outs = pl.pallas_call(
        body, name=name, in_specs=[HBM_SPEC] * n_ops + [ANY],
        out_specs=[SEM_SPEC] * (2 * n) + [HBM_SPEC] * n_ops + [VMEM_SPEC],
        out_shape=[pltpu.SemaphoreType.DMA((3,))] * (2 * n) + [pltpu.HBM(a.shape, a.dtype) for a in ops]
        + [jax.ShapeDtypeStruct((8, 128), F32)],
        input_output_aliases={k: 2 * n + k for k in range(n_ops)},
        compiler_params=pltpu.CompilerParams(has_side_effects=EFFECT),
    )(*[_hbm(a) for a in ops], after)
    sems = list(outs[:2 * n])
    thru = list(outs[2 * n:2 * n + n_ops])
    return sems, thru[:n], thru[n:], outs[-1]


def split_wait(lands, srcs, sems, after, name):
    n = len(lands)
    ops = list(lands) + (list(srcs) if srcs is not None else [])
    n_ops = len(ops)

    def body(*refs):
        land_refs = refs[:n]
        src_refs = refs[n:n_ops] if srcs is not None else None
        sem_refs = refs[n_ops:n_ops + 2 * n]
        x, y, cc = _place()
        chip = 2 * x + y
        for u in range(n):
            for p, peer in enumerate(_chip_peers(x, y)):
                cp = _split_copy(u, p, peer, 2 * peer[0] + peer[1], chip, land_refs, src_refs, sem_refs, cc)
                cp.wait_send()
                cp.wait_recv()

    outs = pl.pallas_call(
        body, name=name, in_specs=[HBM_SPEC] * n_ops + [SEM_SPEC] * (2 * n) + [ANY],
        out_specs=[HBM_SPEC] * n_ops, out_shape=[pltpu.HBM(a.shape, a.dtype) for a in ops],
        input_output_aliases={k: k for k in range(n_ops)},
        compiler_params=pltpu.CompilerParams(has_side_effects=EFFECT),
    )(*ops, *sems, after)
    return list(outs[:n]), list(outs[n:])


def _sibling_copy(k, src_refs, zone_refs, sem_refs):
    x, y, cc = _place()
    return pltpu.make_async_remote_copy(src_ref=src_refs[k], dst_ref=zone_refs[k], send_sem=sem_refs[2 * k], recv_sem=sem_refs[2 * k + 1],
                                        device_id=(x, y, 1 - cc), device_id_type=MESH)


def sibling_start(parts, name):
    n = len(parts)
    ops = list(parts) + [lax.empty(p.shape, p.dtype) for p in parts]

    def body(*refs):
        for k in range(n):
            _sibling_copy(k, refs[:n], refs[n:2 * n], refs[2 * n:4 * n]).start()

    outs = pl.pallas_call(
        body, name=name, in_specs=[HBM_SPEC] * (2 * n),
        out_specs=[SEM_SPEC] * (2 * n) + [HBM_SPEC] * (2 * n),
        out_shape=[pltpu.SemaphoreType.DMA(())] * (2 * n) + [pltpu.HBM(a.shape, a.dtype) for a in ops],
        input_output_aliases={k: 2 * n + k for k in range(2 * n)},
        compiler_params=pltpu.CompilerParams(has_side_effects=EFFECT),
    )(*[_hbm(a) for a in ops])
    return list(outs[2 * n:3 * n]), list(outs[3 * n:]), list(outs[:2 * n])


def sibling_wait(parts, zones, sems, after, name):
    n = len(parts)

    def body(*refs):
        for k in range(n):
            cp = _sibling_copy(k, refs[:n], refs[n:2 * n], refs[2 * n:4 * n])
            cp.wait_send()
            cp.wait_recv()

    outs = pl.pallas_call(
        body, name=name, in_specs=[HBM_SPEC] * (2 * n) + [SEM_SPEC] * (2 * n) + [ANY],
        out_specs=[HBM_SPEC] * (2 * n), out_shape=[pltpu.HBM(a.shape, a.dtype) for a in list(parts) + list(zones)],
        input_output_aliases={k: k for k in range(2 * n)},
        compiler_params=pltpu.CompilerParams(has_side_effects=EFFECT),
    )(*parts, *zones, *sems, after)
    return list(outs[:n]), list(outs[n:])


SMALL_ROWS = 88
FIN_ROWS = 72


def small_finish(g3, g4, c_ctx, lbp, name):
    def body(g3_ref, g4_ref, cc_ref, lbp_ref, o_ref, s_ref):
        s = g3_ref[0]
        for k in range(1, 8):
            s = s + g3_ref[k]
        s_ref[...] = s
        for i in range(DEPTH):
            o_ref[8 * i:8 * i + 8, :] = s_ref[16 * i:16 * i + 8, :] + s_ref[16 * i + 8:16 * i + 16, :]
        acc = g4_ref[0]
        for k in (2, 4, 6):
            acc = acc + g4_ref[k]
        cc = cc_ref[...]
        sg = _sigmoid(cc)
        row = jnp.sum(acc, axis=0, keepdims=True) * (sg * (1.0 + cc * (1.0 - sg)))
        o_ref[32:40, :] = jnp.broadcast_to(row, (8, D))
        o_ref[40:64, :] = s_ref[64:88, :]
        o_ref[64:72, :] = jnp.zeros((8, D), F32)
        for d in range(2):
            pp = lbp_ref[2 * d:2 * d + 1, :] * lbp_ref[2 * d + 1:2 * d + 2, :] * s_ref[75 + d:76 + d, :]
            o_ref[64 + 2 * d:65 + 2 * d, :] = -pp
            o_ref[65 + 2 * d:66 + 2 * d, :] = pp

    return pl.pallas_call(
        body, name=name, in_specs=[VMEM_SPEC] * 4, out_specs=VMEM_SPEC,
        out_shape=jax.ShapeDtypeStruct((FIN_ROWS, D), F32),
        scratch_shapes=[pltpu.VMEM((SMALL_ROWS, D), F32)],
    )(g3, g4, c_ctx, lbp)


def _pack_rows(arrs):
    flat = jnp.concatenate([a.reshape(-1) for a in arrs])
    n = -(-flat.shape[0] // (8 * D)) * 8 * D
    return jnp.pad(flat, (0, n - flat.shape[0])).reshape(n // D, D)


def _unpack_rows(packed, shapes):
    flat = packed.reshape(-1)
    outs, off = [], 0
    for s in shapes:
        size = 1
        for k in s:
            size *= k
        outs.append(flat[off:off + size].reshape(s))
        off += size
    return outs


def _pad8(a):
    return jnp.pad(a, ((0, 8 - a.shape[0]), (0, 0)))


def kernel(x, c, ctx, c_ctx, ada_w, ada_b, norm1, norm2, norm_f, mlp_w1, mlp_w2, hgrn_w_in, hgrn_lb, hgrn_gnorm, hgrn_w_out, conv_w_in, conv_w, conv_b, conv_w_out, loss_target, m_c_ctx, m_ada_w, m_ada_b, m_norm1, m_norm2, m_norm_f, m_mlp_w1, m_mlp_w2, m_hgrn_w_in, m_hgrn_lb, m_hgrn_gnorm, m_hgrn_w_out, m_conv_w_in, m_conv_w, m_conv_b, m_conv_w_out, v_c_ctx, v_ada_w, v_ada_b, v_norm1, v_norm2, v_norm_f, v_mlp_w1, v_mlp_w2, v_hgrn_w_in, v_hgrn_lb, v_hgrn_gnorm, v_hgrn_w_out, v_conv_w_in, v_conv_w, v_conv_b, v_conv_w_out):
    xi, yi, ci = _place()
    me = 4 * xi + 2 * yi + ci
    chip = 2 * xi + yi
    seq = x.shape[1]
    assert ctx.shape[1] == TM and seq % TM == 0 and (seq + TM) % TMW == 0
    n_lat = seq // TM
    sd = D // 4
    nca = ada_w.shape[2]
    xs = jnp.concatenate([x[0], ctx[0]], axis=0)

    big = [(mlp_w1, m_mlp_w1, v_mlp_w1), (mlp_w2, m_mlp_w2, v_mlp_w2), (hgrn_w_in, m_hgrn_w_in, v_hgrn_w_in),
           (hgrn_w_out, m_hgrn_w_out, v_hgrn_w_out), (conv_w_in, m_conv_w_in, v_conv_w_in), (conv_w_out, m_conv_w_out, v_conv_w_out)]
    big_names = ["w1", "w2", "hin", "hout", "cin", "cout"]
    flat2 = lambda a: a.reshape(a.shape[0] * a.shape[1], a.shape[2])
    tensors = dict(zip(big_names, big))
    chip1 = jnp.reshape(chip, (1,)).astype(jnp.int32)
    order = []
    for i in range(DEPTH):
        order += [("hin", i // 2), ("hout", i // 2)] if i % 2 == 0 else [("cin", i // 2), ("cout", i // 2)]
        order += [("w1", i), ("w2", i)]
    lands = [cast_to_slot(flat2(tensors[n][0]), idx, tensors[n][0].shape[1], chip1, f"cast_{n}_{idx}") for n, idx in order]
    sh_rows = jnp.concatenate([hgrn_lb.reshape(4, sd), conv_w.reshape(6, sd), conv_b.reshape(2, sd)], axis=0)
    buf1 = jnp.concatenate([c, jnp.pad(sh_rows, ((0, 0), (0, D - sd))), jnp.zeros((3, D), F32)], axis=0)
    g1 = small_allgather(buf1, [], "gather_small_in")
    first_sems, first_lands, _, first_token = split_start(lands[:1], None, g1, "gather_start_first")
    cvec = jnp.concatenate([g1[:, 0, :], jnp.broadcast_to(c_ctx[None], (8, D))], axis=0)
    shf = g1[0::2, 1:13, :sd].transpose(1, 0, 2).reshape(12, D)
    lb_p = jax.nn.softmax(shf[0:4].reshape(2, 2, D), axis=1)
    lower = jnp.cumsum(lb_p, axis=1) - lb_p[:, :1]
    lbs = [lower[:, 0], lower[:, 1]]
    cw8 = [_pad8(shf[4:7]), _pad8(shf[7:10])]
    cb = [shf[10:11], shf[11:12]]

    bias = lax.dynamic_slice_in_dim(ada_b, chip * nca, nca, axis=1).reshape(DEPTH, 1, nca)
    ada_part = ada_fwd(cvec, ada_w, bias, "ada_fwd")
    g2 = small_allgather(ada_part.reshape(DEPTH * 16, nca), [first_token] + lands[1:], "gather_ada")
    ada_full = g2[0::2].reshape(4, DEPTH, 16, nca).transpose(1, 2, 0, 3).reshape(DEPTH, 16, 4 * nca)
    lat = lax.dynamic_slice_in_dim(ada_full, me, 1, axis=1)[:, 0]
    mods = [jnp.stack([_pad8(lat[i].reshape(6, D)), _pad8(ada_full[i, 8].reshape(6, D))]) for i in range(DEPTH)]

    rest_sems, rest_lands, _, rest_token = split_start(lands[1:], None, g2, "gather_start")
    w_sems = first_sems + rest_sems
    lands = first_lands + rest_lands
    unit = {key: u for u, key in enumerate(order)}

    def wts(n, idx, after):
        u = unit[(n, idx)]
        if u == 0:
            after = rest_token
        (w,), _ = split_wait([lands[u]], None, w_sems[2 * u:2 * u + 2], after, f"gather_wait_{n}_{idx}")
        return w.reshape(w.shape[0] * w.shape[1], w.shape[2]) if n in ("w2", "hout", "cout") else w

    started = []

    def on_grads(i, tag, g):
        names = sorted(g)
        gs = [g[n].reshape(4, g[n].shape[0] * g[n].shape[1] // 4, g[n].shape[2]) for n in names]
        sems, zones, srcs, token = split_start([lax.empty(a.shape, BF16) for a in gs], gs, chip1, f"grad_start_{tag}_{i}")
        started.append(([(n, i if n in ("w1", "w2") else i // 2) for n in names], sems, zones, srcs))
        return token

    done = {}
    acc = {n: lax.empty(flat2(w).shape, F32) for n, (w, _, _) in tensors.items()}
    early_names = ["w1", "w2", "cin", "cout"]
    late_names = ["hin", "hout"]

    def finish_units(group, after, name):
        units = [(key, sems[2 * u:2 * u + 2], zones[u], srcs[u]) for ks, sems, zones, srcs in group for u, key in enumerate(ks)]
        zones, srcs = split_wait([u[2] for u in units], [u[3] for u in units], [s for u in units for s in u[1]], after, name)
        for (key, _, _, _), zone, own in zip(units, zones, srcs):
            acc[key[0]] = sum_slots(own, zone, acc[key[0]], key[1], chip1, f"sum_{key[0]}_{key[1]}")

    def after_backward(small, token):
        rows3 = jnp.concatenate(small["dmod"] + small["norm1"] + small["norm2"] + [small["norm_f"]] + small["gnorm"]
                                + [small["lb"][1]] + small["cw"] + small["cb"] + [jnp.tile(token[0:3], (1, D // 128))], axis=0)
        g3 = small_allgather(rows3, [], "gather_small_out")
        dmat = g3[:, :64].reshape(8, DEPTH, 2, 8, D)[:, :, :, :6].transpose(1, 2, 0, 3, 4).reshape(DEPTH, 16, 6 * D)
        dcols = lax.dynamic_slice_in_dim(dmat, chip * nca, nca, axis=2)
        *done["ada"], acc4 = ada_bwd(cvec, dcols, ada_w, m_ada_w, v_ada_w, "ada_bwd")
        g4 = small_allgather(acc4, [], "gather_cctx")
        done["fin"] = small_finish(g3, g4, c_ctx[None], _pad8(lb_p.reshape(4, D)), "small_finish")
        finish_units(list(started), done["fin"], "grad_wait_early")
        done["sib_early"] = sibling_start([acc[n] for n in early_names], "sibling_start_early")
        return done["sib_early"][0][-1]

    lane, dx, last_token = local_step(xs, loss_target[0], mods, norm1, norm2, norm_f[None], lbs, hgrn_gnorm, cw8, cb, wts, n_lat,
                                      on_grads, after_backward)
    loss = lax.psum(0.5 * jnp.sum(lane) / D, ("x", "y", "c"))
    grad_x = dx[None]
    g_ada_w, d_ada_w, nm_ada_w, nv_ada_w = done["ada"]
    fin = done["fin"]
    cols = lambda a: lax.dynamic_slice_in_dim(a, chip * sd, sd, axis=a.ndim - 1)
    small_g = [fin[32], fin[0:32].reshape(DEPTH, 8, D)[:, :6].reshape(DEPTH, 6 * D), fin[40:44], fin[44:48], fin[48], fin[49:51],
               cols(fin[64:68].reshape(2, 2, D)), cols(fin[53:59].reshape(2, 3, D)), cols(fin[59:61])]
    small_w = [c_ctx, ada_b, norm1, norm2, norm_f, hgrn_gnorm, hgrn_lb, conv_w, conv_b]
    small_m = [m_c_ctx, m_ada_b, m_norm1, m_norm2, m_norm_f, m_hgrn_gnorm, m_hgrn_lb, m_conv_w, m_conv_b]
    small_v = [v_c_ctx, v_ada_b, v_norm1, v_norm2, v_norm_f, v_hgrn_gnorm, v_hgrn_lb, v_conv_w, v_conv_b]
    shapes = [w.shape for w in small_w]
    packed = adamw([_pack_rows(small_g)], _pack_rows(small_w), _pack_rows(small_m), _pack_rows(small_v), "adamw_small")
    s_g, s_d, s_m, s_v = [_unpack_rows(p, shapes) for p in packed]

    results = {}

    def finish_tensors(names, sib, after, name):
        mine, other = sibling_wait(*sib, after, name)
        for n, pm, po in zip(names, mine, other):
            w, m, v = tensors[n]
            results[n] = [a.reshape(w.shape) for a in adamw([pm, po], flat2(w), flat2(m), flat2(v), f"adamw_{n}")]

    finish_tensors(early_names, done["sib_early"], last_token, "sibling_wait_early")
    finish_units(started[-1:], results["cout"][0], "grad_wait_late")
    sib_late = sibling_start([acc[n] for n in late_names], "sibling_start_late")
    finish_tensors(late_names, sib_late, results["cin"][0], "sibling_wait_late")
    b_g, b_d, b_m, b_v = [[results[n][k] for n in big_names] for k in range(4)]

    def ordered(s, a, b):
        return [s[0], a, s[1], s[2], s[3], s[4], b[0], b[1], b[2], s[6], s[5], b[3], b[4], s[7], s[8], b[5]]

    return (loss, grad_x, *ordered(s_g, g_ada_w, b_g), *ordered(s_d, d_ada_w, b_d), *ordered(s_m, nm_ada_w, b_m),
            *ordered(s_v, nv_ada_w, b_v))
```

```python
import functools

import jax
import jax.numpy as jnp
from jax import lax
from jax.experimental import pallas as pl
from jax.experimental.pallas import tpu as pltpu

F32 = jnp.float32
BF16 = jnp.bfloat16
MESH = pl.DeviceIdType.MESH

D = 1024
HD = 128
NH = D // HD
CH = 64
TM = 256
TMW = 2816
EPS = 1e-6
DEPTH = 4
VMEM_LIMIT = 56 * 1024 * 1024

ADAM_LR = 0.001
ADAM_B1 = 0.9
ADAM_B2 = 0.999
ADAM_EPS = 1e-08
ADAM_WD = 0.01
ADAM_STEP = 10


def _cp(n_grid):
    return pltpu.CompilerParams(dimension_semantics=("arbitrary",) * n_grid, vmem_limit_bytes=VMEM_LIMIT)


def _dot(a, b):
    return jnp.dot(a, b, preferred_element_type=F32)


def _dot_nt(a, b):
    return lax.dot_general(a, b, (((1,), (1,)), ((), ())), preferred_element_type=F32)


def _dot_tn(a, b):
    return lax.dot_general(a, b, (((0,), (0,)), ((), ())), preferred_element_type=F32)


def _sigmoid(z):
    return 1.0 / (1.0 + jnp.exp(-z))


def _norm_mod(x, gain, sh, sc):
    r = lax.rsqrt(jnp.mean(x * x, axis=-1, keepdims=True) + EPS)
    xn = x * r
    yn = xn * gain
    return r, xn, yn, yn * (1.0 + sc) + sh


def _row_spec(width):
    return pl.BlockSpec((TM, width), lambda i: (i, 0))


def _col_spec(col):
    return pl.BlockSpec((TM, D), lambda i: (i, col))


def _full_spec(shape):
    nd = len(shape)
    return pl.BlockSpec(shape, lambda i: (0,) * nd)


def _mod_spec(n_lat):
    return pl.BlockSpec((1, 8, D), lambda i: (i // n_lat, 0, 0))


def _f32(ref):
    return ref[...].astype(F32)


def proj_fwd(x, gain, mod, m0, w4, n_lat, dtype, name):
    t = x.shape[0]
    nb, _, ns = w4.shape

    def body(x_ref, gain_ref, mod_ref, w_ref, p_ref):
        _, _, _, h = _norm_mod(x_ref[...], gain_ref[...], mod_ref[0, m0:m0 + 1, :], mod_ref[0, m0 + 1:m0 + 2, :])
        hb = h.astype(BF16)
        for c in range(nb):
            p_ref[:, c * ns:(c + 1) * ns] = _dot(hb, w_ref[c]).astype(dtype)

    return pl.pallas_call(
        body, name=name, grid=(t // TM,),
        in_specs=[_row_spec(D), _full_spec((1, D)), _mod_spec(n_lat), _full_spec(w4.shape)],
        out_specs=_row_spec(nb * ns),
        out_shape=jax.ShapeDtypeStruct((t, nb * ns), dtype),
        compiler_params=_cp(1),
    )(x, gain, mod, w4)


def proj_bwd(parts, w4, x, gain, mod, m0, dx_in, n_lat, lat_only, name):
    t = x.shape[0]
    nb, _, ns = w4.shape
    n = nb * ns
    n_parts = len(parts)
    widths = [p.shape[1] for p in parts]
    offs = [sum(widths[:k]) for k in range(n_parts)]
    assert sum(widths) == n
    single = n_parts == 1

    def body(*refs):
        part_refs = refs[:n_parts]
        w_ref, x_ref, gain_ref, mod_ref, dxin_ref = refs[n_parts:n_parts + 5]
        rest = refs[n_parts + 5:]
        if single:
            dx_ref, hb_ref, acc_ref = rest
            src = part_refs[0]
        else:
            dx_ref, hb_ref, acc_ref, dpb_ref = rest
            for p_ref, off, w in zip(part_refs, offs, widths):
                dpb_ref[:, off:off + w] = p_ref[...]
            src = dpb_ref
        i = pl.program_id(0)

        @pl.when(i == 0)
        def _():
            acc_ref[...] = jnp.zeros_like(acc_ref)

        gain = gain_ref[...]
        sc = mod_ref[0, m0 + 1:m0 + 2, :]
        r, xn, yn, h = _norm_mod(x_ref[...], gain, mod_ref[0, m0:m0 + 1, :], sc)
        hb_ref[...] = h.astype(BF16)
        dh = _dot_nt(src[:, 0:ns], w_ref[0])
        for c in range(1, nb):
            dh = dh + _dot_nt(src[:, c * ns:(c + 1) * ns], w_ref[c])
        dsh = jnp.sum(dh, axis=0, keepdims=True)
        dsc = jnp.sum(dh * yn, axis=0, keepdims=True)
        dyn = dh * (1.0 + sc)
        dgain = jnp.sum(dyn * xn, axis=0, keepdims=True)
        dxn = dyn * gain
        dx = dxin_ref[...] + r * (dxn - xn * jnp.mean(dxn * xn, axis=-1, keepdims=True))
        if lat_only:
            @pl.when(i < n_lat)
            def _():
                dx_ref[...] = dx
        else:
            dx_ref[...] = dx
        latf = (i < n_lat).astype(F32)
        ctxf = 1.0 - latf
        acc_ref[0:1, :] += dgain
        acc_ref[1:2, :] += dsh * latf
        acc_ref[2:3, :] += dsc * latf
        acc_ref[3:4, :] += dsh * ctxf
        acc_ref[4:5, :] += dsc * ctxf

    dx_rows = n_lat * TM if lat_only else t
    dx_spec = pl.BlockSpec((TM, D), lambda i: (jnp.minimum(i, n_lat - 1), 0)) if lat_only else _row_spec(D)
    out_specs = [dx_spec, _row_spec(D), _full_spec((8, D))]
    out_shape = [jax.ShapeDtypeStruct((dx_rows, D), F32), jax.ShapeDtypeStruct((t, D), BF16), jax.ShapeDtypeStruct((8, D), F32)]
    if not single:
        out_specs.append(_row_spec(n))
        out_shape.append(jax.ShapeDtypeStruct((t, n), BF16))
    outs = pl.pallas_call(
        body, name=name, grid=(t // TM,),
        in_specs=[_row_spec(w) for w in widths]
        + [_full_spec(w4.shape), _row_spec(D), _full_spec((1, D)), _mod_spec(n_lat), _row_spec(D)],
        out_specs=out_specs, out_shape=out_shape, compiler_params=_cp(1),
    )(*parts, w4, x, gain, mod, dx_in)
    if single:
        return outs[0], outs[1], parts[0], outs[2]
    return outs[0], outs[1], outs[3], outs[2]


def dw_tn(a, b, nb, a_blocked, square_a, dep, name):
    t = a.shape[0]
    ka = a.shape[1] // nb if a_blocked else a.shape[1]
    kb = b.shape[1] if a_blocked else b.shape[1] // nb
    n_k = t // TMW

    def body(a_ref, b_ref, _, o_ref, acc):
        k = pl.program_id(1)

        @pl.when(k == 0)
        def _():
            acc[...] = jnp.zeros_like(acc)

        a = a_ref[...]
        acc[...] += _dot_tn(a * a if square_a else a, b_ref[...])

        @pl.when(k == n_k - 1)
        def _():
            o_ref[0] = acc[...].astype(BF16)

    a_spec = pl.BlockSpec((TMW, ka), (lambda j, k: (k, j)) if a_blocked else (lambda j, k: (k, 0)))
    b_spec = pl.BlockSpec((TMW, kb), (lambda j, k: (k, 0)) if a_blocked else (lambda j, k: (k, j)))
    return pl.pallas_call(
        body, name=name, grid=(nb, n_k),
        in_specs=[a_spec, b_spec, ANY],
        out_specs=pl.BlockSpec((1, ka, kb), lambda j, k: (j, 0, 0)),
        out_shape=jax.ShapeDtypeStruct((nb, ka, kb), BF16),
        scratch_shapes=[pltpu.VMEM((ka, kb), F32)],
        compiler_params=_cp(2),
    )(a, b, dep)


def outproj_fwd(prologue, extras, extra_specs, w, x, mod, m0, n_lat, name):
    t = x.shape[0]
    k = w.shape[0]
    n_extra = len(extras)

    def body(*refs):
        ex = refs[:n_extra]
        w_ref, x_ref, mod_ref, xo_ref, y_ref, ab_ref = refs[n_extra:]
        ab = prologue(pl.program_id(0), *ex).astype(BF16)
        ab_ref[...] = ab
        y = _dot(ab, w_ref[...])
        y_ref[...] = y.astype(BF16)
        xo_ref[...] = x_ref[...] + mod_ref[0, m0 + 2:m0 + 3, :] * y

    return pl.pallas_call(
        body, name=name, grid=(t // TM,),
        in_specs=list(extra_specs) + [_full_spec(w.shape), _row_spec(D), _mod_spec(n_lat)],
        out_specs=[_row_spec(D), _row_spec(D), _row_spec(k)],
        out_shape=[jax.ShapeDtypeStruct((t, D), F32), jax.ShapeDtypeStruct((t, D), BF16), jax.ShapeDtypeStruct((t, k), BF16)],
        compiler_params=_cp(1),
    )(*extras, w, x, mod)


def outproj_bwd(epilogue, extras, extra_specs, ep_out_specs, ep_out_shapes, w, dxn, y, mod, m0, n_lat, dep, name):
    t = dxn.shape[0]
    n_extra = len(extras)

    def body(*refs):
        ex = refs[:n_extra]
        w_ref, dxn_ref, y_ref, mod_ref, _, dyb_ref, acc_ref = refs[n_extra:n_extra + 7]
        ep_outs = refs[n_extra + 7:]
        i = pl.program_id(0)

        @pl.when(i == 0)
        def _():
            acc_ref[...] = jnp.zeros_like(acc_ref)

        dxv = dxn_ref[...]
        dyb = (dxv * mod_ref[0, m0 + 2:m0 + 3, :]).astype(BF16)
        dyb_ref[...] = dyb
        dg = jnp.sum(dxv * _f32(y_ref), axis=0, keepdims=True)
        latf = (i < n_lat).astype(F32)
        acc_ref[0:1, :] += dg * latf
        acc_ref[1:2, :] += dg * (1.0 - latf)
        epilogue(i, _dot_nt(dyb, w_ref[...]), ex, ep_outs, acc_ref)

    outs = pl.pallas_call(
        body, name=name, grid=(t // TM,),
        in_specs=list(extra_specs) + [_full_spec(w.shape), _row_spec(D), _row_spec(D), _mod_spec(n_lat), ANY],
        out_specs=[_row_spec(D), _full_spec((8, D))] + list(ep_out_specs),
        out_shape=[jax.ShapeDtypeStruct((t, D), BF16), jax.ShapeDtypeStruct((8, D), F32)] + list(ep_out_shapes),
        compiler_params=_cp(1),
    )(*extras, w, dxn, y, mod, dep)
    return outs[0], outs[1], outs[2:]


def mlp_fwd(x, gain, mod, w1, w2, n_lat, name):
    t = x.shape[0]
    nb, _, ns = w1.shape

    def body(x_ref, gain_ref, mod_ref, w1_ref, w2_ref, xo_ref, y_ref, rb_ref):
        x = x_ref[...]
        _, _, _, h = _norm_mod(x, gain_ref[...], mod_ref[0, 3:4, :], mod_ref[0, 4:5, :])
        hb = h.astype(BF16)
        y = None
        for c in range(nb):
            r = jnp.maximum(_dot(hb, w1_ref[c]), 0.0)
            rb_ref[:, c * ns:(c + 1) * ns] = r.astype(BF16)
            yc = _dot((r * r).astype(BF16), w2_ref[c * ns:(c + 1) * ns, :])
            y = yc if y is None else y + yc
        y_ref[...] = y.astype(BF16)
        xo_ref[...] = x + mod_ref[0, 5:6, :] * y

    return pl.pallas_call(
        body, name=name, grid=(t // TM,),
        in_specs=[_row_spec(D), _full_spec((1, D)), _mod_spec(n_lat), _full_spec(w1.shape), _full_spec(w2.shape)],
        out_specs=[_row_spec(D), _row_spec(D), _row_spec(nb * ns)],
        out_shape=[jax.ShapeDtypeStruct((t, D), F32), jax.ShapeDtypeStruct((t, D), BF16), jax.ShapeDtypeStruct((t, nb * ns), BF16)],
        compiler_params=_cp(1),
    )(x, gain, mod, w1, w2)


def mlp_bwd(dxn, y, ab, x, gain, mod, w1, w2, n_lat, dep, name):
    t = x.shape[0]
    nb, _, ns = w1.shape

    def body(dxn_ref, y_ref, rb_ref, x_ref, gain_ref, mod_ref, w1_ref, w2_ref, _, dx_ref, dyb_ref, dp_ref, hb_ref, acc_ref):
        i = pl.program_id(0)

        @pl.when(i == 0)
        def _():
            acc_ref[...] = jnp.zeros_like(acc_ref)

        dxv = dxn_ref[...]
        dyb = (dxv * mod_ref[0, 5:6, :]).astype(BF16)
        dyb_ref[...] = dyb
        dg = jnp.sum(dxv * _f32(y_ref), axis=0, keepdims=True)
        gain = gain_ref[...]
        sc = mod_ref[0, 4:5, :]
        r, xn, yn, h = _norm_mod(x_ref[...], gain, mod_ref[0, 3:4, :], sc)
        hb_ref[...] = h.astype(BF16)
        dps = []
        for c in range(nb):
            cols = slice(c * ns, (c + 1) * ns)
            dp = (_dot_nt(dyb, w2_ref[cols, :]) * (2.0 * rb_ref[:, cols].astype(F32))).astype(BF16)
            dp_ref[:, cols] = dp
            dps.append(dp)
        dh = _dot_nt(dps[0], w1_ref[0])
        for c in range(1, nb):
            dh = dh + _dot_nt(dps[c], w1_ref[c])
        dsh = jnp.sum(dh, axis=0, keepdims=True)
        dsc = jnp.sum(dh * yn, axis=0, keepdims=True)
        dyn = dh * (1.0 + sc)
        dgain = jnp.sum(dyn * xn, axis=0, keepdims=True)
        dxn_ = dyn * gain
        dx_ref[...] = dxv + r * (dxn_ - xn * jnp.mean(dxn_ * xn, axis=-1, keepdims=True))
        latf = (i < n_lat).astype(F32)
        ctxf = 1.0 - latf
        acc_ref[0:1, :] += dgain
        acc_ref[1:2, :] += dsh * latf
        acc_ref[2:3, :] += dsc * latf
        acc_ref[3:4, :] += dsh * ctxf
        acc_ref[4:5, :] += dsc * ctxf
        acc_ref[5:6, :] += dg * latf
        acc_ref[6:7, :] += dg * ctxf

    return pl.pallas_call(
        body, name=name, grid=(t // TM,),
        in_specs=[_row_spec(D), _row_spec(D), _row_spec(nb * ns), _row_spec(D), _full_spec((1, D)), _mod_spec(n_lat),
                  _full_spec(w1.shape), _full_spec(w2.shape), ANY],
        out_specs=[_row_spec(D), _row_spec(D), _row_spec(nb * ns), _row_spec(D), _full_spec((8, D))],
        out_shape=[jax.ShapeDtypeStruct((t, D), F32), jax.ShapeDtypeStruct((t, D), BF16), jax.ShapeDtypeStruct((t, nb * ns), BF16),
                   jax.ShapeDtypeStruct((t, D), BF16), jax.ShapeDtypeStruct((8, D), F32)],
        compiler_params=_cp(1),
    )(dxn, y, ab, x, gain, mod, w1, w2, dep)


def readout_prologue(i, o0_ref, o1_ref, gate_ref, gn_ref):
    o = _f32(o0_ref) + _f32(o1_ref)
    gate = gate_ref[...]
    w = gn_ref[...] * (gate * _sigmoid(gate))
    pieces = []
    for h in range(NH):
        sl = slice(h * HD, (h + 1) * HD)
        oh = o[:, sl]
        pieces.append(oh * lax.rsqrt(jnp.mean(oh * oh, axis=-1, keepdims=True) + EPS) * w[:, sl])
    return jnp.concatenate(pieces, axis=1)


def readout_epilogue(i, da, ex, outs, acc_ref):
    o0_ref, o1_ref, gate_ref, gn_ref = ex
    do_ref, dgate_ref = outs
    o = _f32(o0_ref) + _f32(o1_ref)
    gate = gate_ref[...]
    gn = gn_ref[...]
    sg = _sigmoid(gate)
    silu = gate * sg
    dsilu = sg * (1.0 + gate * (1.0 - sg))
    for h in range(NH):
        sl = slice(h * HD, (h + 1) * HD)
        oh = o[:, sl]
        r = lax.rsqrt(jnp.mean(oh * oh, axis=-1, keepdims=True) + EPS)
        nh = oh * r
        dah = da[:, sl]
        acc_ref[2:3, sl] += jnp.sum(dah * nh * silu[:, sl], axis=0, keepdims=True)
        dgate_ref[:, sl] = (dah * nh * gn[:, sl] * dsilu[:, sl]).astype(BF16)
        dn = dah * gn[:, sl] * silu[:, sl]
        do_ref[:, sl] = r * (dn - nh * jnp.mean(dn * nh, axis=-1, keepdims=True))


def _seg_masks(i, n_lat):
    rows = lax.broadcasted_iota(jnp.int32, (TM, 1), 0)
    latf = (i < n_lat).astype(F32)
    ctxf = 1.0 - latf
    prev_ok = (rows % CH != 0).astype(F32) * latf + (rows != 0).astype(F32) * ctxf
    next_ok = (rows % CH != CH - 1).astype(F32) * latf + (rows != TM - 1).astype(F32) * ctxf
    return prev_ok, next_ok


def _shifts(i, n_lat, sft, cur, halo_prev, halo_next):
    if sft == 1:
        prev_ok, next_ok = _seg_masks(i, n_lat)
        return pltpu.roll(cur, 1, 0) * prev_ok, pltpu.roll(cur, TM - 1, 0) * next_ok
    has_prev = jnp.logical_and(i > 0, i < n_lat).astype(F32)
    has_next = (i < n_lat - 1).astype(F32)
    prev = jnp.concatenate([halo_prev * has_prev, cur[:TM - CH]], axis=0)
    nxt = jnp.concatenate([cur[CH:], halo_next * has_next], axis=0)
    return prev, nxt


def _conv_u(sft, ex):
    if sft == 1:
        gb_ref, gc_ref, xi_ref, cw_ref, cb_ref = ex
        return gb_ref, _f32(gc_ref) * _f32(xi_ref), None, None, cw_ref, cb_ref
    gb_ref, gc_ref, xi_ref, gcp_ref, xip_ref, gcn_ref, xin_ref, cw_ref, cb_ref = ex
    return gb_ref, _f32(gc_ref) * _f32(xi_ref), _f32(gcp_ref) * _f32(xip_ref), _f32(gcn_ref) * _f32(xin_ref), cw_ref, cb_ref


def _conv_value(i, n_lat, sft, ex):
    gb_ref, u, up, un, cw_ref, cb_ref = _conv_u(sft, ex)
    u_prev, u_next = _shifts(i, n_lat, sft, u, up, un)
    return gb_ref, cb_ref[...] + cw_ref[0:1, :] * u_prev + cw_ref[1:2, :] * u + cw_ref[2:3, :] * u_next


def make_conv_prologue(n_lat, sft):
    def prologue(i, *ex):
        gb_ref, conv = _conv_value(i, n_lat, sft, ex)
        return _f32(gb_ref) * conv
    return prologue


def make_conv_epilogue(n_lat, sft):
    def epilogue(i, da, ex, outs, acc_ref):
        gb_ref, conv = _conv_value(i, n_lat, sft, ex)
        outs[0][...] = (da * _f32(gb_ref)).astype(BF16)
        outs[1][...] = (da * conv).astype(BF16)
    return epilogue


def _conv_specs(sft, t):
    specs = [_col_spec(0), _col_spec(1), _col_spec(2)]
    if sft != 1:
        per = TM // CH
        last = t // CH - 1
        for fn in (lambda i: jnp.maximum(i * per - 1, 0), lambda i: jnp.minimum(i * per + per, last)):
            for col in (1, 2):
                specs.append(pl.BlockSpec((CH, D), functools.partial(lambda i, f, c: (f(i), c), f=fn, c=col)))
    return specs + [_full_spec((8, D)), _full_spec((1, D))]


def _conv_args(sft, p, cw8, cb):
    return [p] * (3 if sft == 1 else 7) + [cw8, cb]


def conv_bwd(dconv, p, cw8, sft, n_lat, name):
    t = dconv.shape[0]
    halo = sft != 1

    def body(*refs):
        if halo:
            dc_ref, dcp_ref, dcn_ref, gc_ref, xi_ref, gcp_ref, xip_ref, gcn_ref, xin_ref, cw_ref, dgc_ref, dxi_ref, acc_ref = refs
            up, un = _f32(gcp_ref) * _f32(xip_ref), _f32(gcn_ref) * _f32(xin_ref)
            dcp, dcn = _f32(dcp_ref), _f32(dcn_ref)
        else:
            dc_ref, gc_ref, xi_ref, cw_ref, dgc_ref, dxi_ref, acc_ref = refs
            up = un = dcp = dcn = None
        i = pl.program_id(0)

        @pl.when(i == 0)
        def _():
            acc_ref[...] = jnp.zeros_like(acc_ref)

        gc = _f32(gc_ref)
        xi = _f32(xi_ref)
        u = gc * xi
        dc = _f32(dc_ref)
        u_prev, u_next = _shifts(i, n_lat, sft, u, up, un)
        dc_prev, dc_next = _shifts(i, n_lat, sft, dc, dcp, dcn)
        acc_ref[0:1, :] += jnp.sum(dc * u_prev, axis=0, keepdims=True)
        acc_ref[1:2, :] += jnp.sum(dc * u, axis=0, keepdims=True)
        acc_ref[2:3, :] += jnp.sum(dc * u_next, axis=0, keepdims=True)
        acc_ref[3:4, :] += jnp.sum(dc, axis=0, keepdims=True)
        du = cw_ref[0:1, :] * dc_next + cw_ref[1:2, :] * dc + cw_ref[2:3, :] * dc_prev
        dgc_ref[...] = (du * xi).astype(BF16)
        dxi_ref[...] = (du * gc).astype(BF16)

    per = TM // CH
    last = t // CH - 1
    prev_i = lambda i: jnp.maximum(i * per - 1, 0)
    next_i = lambda i: jnp.minimum(i * per + per, last)
    if halo:
        in_specs = [_row_spec(D), pl.BlockSpec((CH, D), lambda i: (prev_i(i), 0)), pl.BlockSpec((CH, D), lambda i: (next_i(i), 0)),
                    _col_spec(1), _col_spec(2),
                    pl.BlockSpec((CH, D), lambda i: (prev_i(i), 1)), pl.BlockSpec((CH, D), lambda i: (prev_i(i), 2)),
                    pl.BlockSpec((CH, D), lambda i: (next_i(i), 1)), pl.BlockSpec((CH, D), lambda i: (next_i(i), 2)),
                    _full_spec((8, D))]
        args = [dconv, dconv, dconv, p, p, p, p, p, p, cw8]
    else:
        in_specs = [_row_spec(D), _col_spec(1), _col_spec(2), _full_spec((8, D))]
        args = [dconv, p, p, cw8]
    return pl.pallas_call(
        body, name=name, grid=(t // TM,), in_specs=in_specs,
        out_specs=[_row_spec(D), _row_spec(D), _full_spec((8, D))],
        out_shape=[jax.ShapeDtypeStruct((t, D), BF16), jax.ShapeDtypeStruct((t, D), BF16), jax.ShapeDtypeStruct((8, D), F32)],
        compiler_params=_cp(1),
    )(*args)


LOG2E = 1.4426950408889634


def _cumsum_matrix(reverse):
    r = lax.broadcasted_iota(jnp.int32, (CH, CH), 0)
    c = lax.broadcasted_iota(jnp.int32, (CH, CH), 1)
    return (r <= c if reverse else r >= c).astype(BF16)


def _chunk_cumsum(g, tri):
    hi = g.astype(BF16)
    lo = (g - hi.astype(F32)).astype(BF16)
    return _dot(tri, hi) + _dot(tri, lo)


def _gate_values(z, lb):
    sig = _sigmoid(z)
    f = lb + (1.0 - lb) * sig
    return sig, f


def _tri(direction, transposed):
    r = lax.broadcasted_iota(jnp.int32, (CH, CH), 0)
    c = lax.broadcasted_iota(jnp.int32, (CH, CH), 1)
    lower = (direction == 0) != transposed
    return r >= c if lower else r <= c


def _gla_rows(direction):
    return (CH // 2 - 1, CH - 1) if direction == 0 else (CH // 2, 0)


def _fwd_tile(direction, nt):
    return (lambda i: (i + nt - 1) % nt) if direction == 0 else (lambda i: nt - 1 - i)


def gla_fwd(p, lb2, name):
    t = p.shape[0]
    nt = t // TM
    per = TM // CH

    def body(z0_ref, v0_ref, q0_ref, z1_ref, v1_ref, q1_ref, lb_ref, o0_ref, s0_ref, o1_ref, s1_ref, st, q_s, k_s, c_s):
        @pl.when(pl.program_id(0) == 0)
        def _():
            st[...] = jnp.zeros_like(st)

        ins = ((z0_ref, v0_ref, q0_ref, o0_ref, s0_ref), (z1_ref, v1_ref, q1_ref, o1_ref, s1_ref))
        for d in range(2):
            z_ref, _, qr_ref, _, _ = ins[d]
            tri = _cumsum_matrix(d == 1)
            lb = lb_ref[d:d + 1, :]
            for ci in range(per):
                rows = slice(ci * CH, (ci + 1) * CH)
                _, f = _gate_values(z_ref[rows, :], lb)
                k_s[d, rows, :] = 1.0 - f
                c_s[d, rows, :] = _chunk_cumsum(jnp.log(f) * LOG2E, tri)
                qr = qr_ref[rows, :]
                q_s[d, rows, :] = qr * _sigmoid(qr)
        masks = (_tri(0, False), _tri(1, False))
        state = [[st[d, h] for h in range(NH)] for d in range(2)]
        for it in range(per):
            chunk = []
            for d in range(2):
                ref_row, last_row = _gla_rows(d)
                ci = it if d == 0 else per - 1 - it
                r0 = ci * CH
                rows = slice(r0, r0 + CH)
                cum = c_s[d, rows, :]
                ref = c_s[d, r0 + ref_row:r0 + ref_row + 1, :]
                last = c_s[d, r0 + last_row:r0 + last_row + 1, :]
                q = q_s[d, rows, :]
                k = k_s[d, rows, :]
                chunk.append(dict(
                    ci=ci, rows=rows, qh=(q * jnp.exp2(cum)).astype(BF16), qt=(q * jnp.exp2(cum - ref)).astype(BF16),
                    kt=(k * jnp.exp2(ref - cum)).astype(BF16), kb=(k * jnp.exp2(last - cum)).astype(BF16),
                    el=jnp.exp2(last), vb=ins[d][1][rows, :].astype(BF16)))
            heads = [slice(h * HD, (h + 1) * HD) for h in range(NH)]
            first = {}
            for h, sl in enumerate(heads):
                for d in range(2):
                    c = chunk[d]
                    s_t = state[d][h]
                    ins[d][4][c["ci"], h] = s_t
                    first[d, h] = (jnp.where(masks[d], _dot_nt(c["qt"][:, sl], c["kt"][:, sl]), 0.0).astype(BF16),
                                   _dot_nt(c["qh"][:, sl], s_t.astype(BF16)),
                                   s_t * c["el"][:, sl] + _dot_tn(c["vb"][:, sl], c["kb"][:, sl]))
            for h, sl in enumerate(heads):
                for d in range(2):
                    c = chunk[d]
                    sc, o_inter, s_new = first[d, h]
                    ins[d][3][c["rows"], sl] = (o_inter + _dot(sc, c["vb"][:, sl])).astype(BF16)
                    state[d][h] = s_new
        for d in range(2):
            for h in range(NH):
                st[d, h] = state[d][h]

    tiles = (_fwd_tile(0, nt), _fwd_tile(1, nt))
    tspec = lambda d, col: pl.BlockSpec((TM, D), lambda i: (tiles[d](i), col))
    sspec = lambda d: pl.BlockSpec((per, NH, HD, HD), lambda i: (tiles[d](i), 0, 0, 0))
    o_shape = jax.ShapeDtypeStruct((t, D), BF16)
    s_shape = jax.ShapeDtypeStruct((t // CH, NH, HD, HD), F32)
    return pl.pallas_call(
        body, name=name, grid=(nt,),
        in_specs=[tspec(0, 0), tspec(0, 2), tspec(0, 3), tspec(1, 1), tspec(1, 2), tspec(1, 3), _full_spec((2, D))],
        out_specs=[tspec(0, 0), sspec(0), tspec(1, 0), sspec(1)],
        out_shape=[o_shape, s_shape, o_shape, s_shape],
        scratch_shapes=[pltpu.VMEM((2, NH, HD, HD), F32)] + [pltpu.VMEM((2, TM, D), F32)] * 3,
        compiler_params=_cp(1),
    )(p, p, p, p, p, p, lb2)


def gla_bwd(p, lb2, do, states, direction, prev, name):
    t = p.shape[0]
    nt = t // TM
    per = TM // CH
    ref_row, last_row = _gla_rows(direction)
    tile = (lambda i: (2 * nt - 2 - i) % nt) if direction == 0 else (lambda i: i)
    final = prev is not None
    n_in = 8 if final else 6

    def body(*refs):
        z_ref, v_ref, qr_ref, lb_ref, do_ref, s_ref = refs[:6]
        dz_ref, dv_ref, dq_ref, acc_ref, dst, q_s, k_s, c_s, dq_s, dk_s, dl_s = refs[n_in:]

        @pl.when(pl.program_id(0) == 0)
        def _():
            dst[...] = jnp.zeros_like(dst)
            acc_ref[...] = jnp.zeros_like(acc_ref)

        mask = _tri(direction, False)
        mask_t = _tri(direction, True)
        tri = _cumsum_matrix(direction == 1)
        tri_t = _cumsum_matrix(direction == 0)
        is_last = lax.broadcasted_iota(jnp.int32, (CH, 1), 0) == last_row
        lb = lb_ref[direction:direction + 1, :]
        for ci in range(per):
            rows = slice(ci * CH, (ci + 1) * CH)
            _, f = _gate_values(z_ref[rows, :], lb)
            k_s[rows, :] = 1.0 - f
            c_s[rows, :] = _chunk_cumsum(jnp.log(f) * LOG2E, tri)
            qr = qr_ref[rows, :]
            q_s[rows, :] = qr * _sigmoid(qr)
        state = [dst[h] for h in range(NH)]
        for it in range(per):
            ci = per - 1 - it if direction == 0 else it
            r0 = ci * CH
            rows = slice(r0, r0 + CH)
            cum = c_s[rows, :]
            ref = c_s[r0 + ref_row:r0 + ref_row + 1, :]
            last = c_s[r0 + last_row:r0 + last_row + 1, :]
            q = q_s[rows, :]
            k = k_s[rows, :]
            e_h = jnp.exp2(cum)
            e_t = jnp.exp2(cum - ref)
            e_kt = jnp.exp2(ref - cum)
            e_kb = jnp.exp2(last - cum)
            el = jnp.exp2(last)
            qh = (q * e_h).astype(BF16)
            qt = (q * e_t).astype(BF16)
            kt = (k * e_kt).astype(BF16)
            kbf = k * e_kb
            kb = kbf.astype(BF16)
            vb = v_ref[rows, :].astype(BF16)
            dob = do_ref[rows, :].astype(BF16)
            heads = [slice(h * HD, (h + 1) * HD) for h in range(NH)]
            first = []
            for h, sl in enumerate(heads):
                s_t = s_ref[ci, h]
                ds_t = state[h]
                ds_b = ds_t.astype(BF16)
                d_a = jnp.where(mask, _dot_nt(dob[:, sl], vb[:, sl]), 0.0).astype(BF16)
                a_t = jnp.where(mask_t, _dot_nt(kt[:, sl], qt[:, sl]), 0.0).astype(BF16)
                d_at = jnp.where(mask_t, _dot_nt(vb[:, sl], dob[:, sl]), 0.0).astype(BF16)
                dkb = _dot(vb[:, sl], ds_b)
                dl_s[it:it + 1, sl] = (el[:, sl] * jnp.sum(ds_t * s_t, axis=0, keepdims=True)
                                       + jnp.sum(dkb * kbf[:, sl], axis=0, keepdims=True))
                state[h] = ds_t * el[:, sl] + _dot_tn(dob[:, sl], qh[:, sl])
                first.append((d_a, a_t, d_at, dkb, _dot_nt(kb[:, sl], ds_b), _dot(dob[:, sl], s_t.astype(BF16))))
            for h, sl in enumerate(heads):
                d_a, a_t, d_at, dkb, dv_state, dq_state = first[h]
                dv = _dot(a_t, dob[:, sl]) + dv_state
                dq_s[rows, sl] = dq_state * e_h[:, sl] + _dot(d_a, kt[:, sl]) * e_t[:, sl]
                dk_s[rows, sl] = _dot(d_at, qt[:, sl]) * e_kt[:, sl] + dkb * e_kb[:, sl]
                if final:
                    dv_ref[rows, sl] = (refs[6][rows, sl] + dv).astype(BF16)
                else:
                    dv_ref[rows, sl] = dv
        for h in range(NH):
            dst[h] = state[h]
        for it in range(per):
            ci = per - 1 - it if direction == 0 else it
            rows = slice(ci * CH, (ci + 1) * CH)
            dq = dq_s[rows, :]
            dk = dk_s[rows, :]
            dg = _chunk_cumsum(dq * q_s[rows, :] - dk * k_s[rows, :] + jnp.where(is_last, dl_s[it:it + 1, :], 0.0), tri_t)
            sig, f = _gate_values(z_ref[rows, :], lb)
            df = dg / f - dk
            acc_ref[0:1, :] += jnp.sum(df * (1.0 - sig), axis=0, keepdims=True)
            dz_ref[rows, :] = (df * (1.0 - lb) * sig * (1.0 - sig)).astype(BF16)
            if final:
                qr = qr_ref[rows, :]
                sq = _sigmoid(qr)
                dq_ref[rows, :] = ((refs[7][rows, :] + dq) * (sq * (1.0 + qr * (1.0 - sq)))).astype(BF16)
            else:
                dq_ref[rows, :] = dq

    tspec = lambda col: pl.BlockSpec((TM, D), lambda i: (tile(i), col))
    sspec = pl.BlockSpec((per, NH, HD, HD), lambda i: (tile(i), 0, 0, 0))
    in_specs = [tspec(direction), tspec(2), tspec(3), _full_spec((2, D)), tspec(0), sspec]
    args = [p, p, p, lb2, do, states]
    if final:
        in_specs += [tspec(0), tspec(0)]
        args += list(prev)
    odt = BF16 if final else F32
    return pl.pallas_call(
        body, name=name, grid=(nt,), in_specs=in_specs,
        out_specs=[tspec(0), tspec(0), tspec(0), _full_spec((8, D))],
        out_shape=[jax.ShapeDtypeStruct((t, D), BF16), jax.ShapeDtypeStruct((t, D), odt), jax.ShapeDtypeStruct((t, D), odt),
                   jax.ShapeDtypeStruct((8, D), F32)],
        scratch_shapes=[pltpu.VMEM((NH, HD, HD), F32)] + [pltpu.VMEM((TM, D), F32)] * 5 + [pltpu.VMEM((8, D), F32)],
        compiler_params=_cp(1),
    )(*args)


def loss_bwd(x, gain, target, n_lat, name):
    t = x.shape[0]

    def body(x_ref, gain_ref, tg_ref, dx_ref, acc_ref):
        i = pl.program_id(0)

        @pl.when(i == 0)
        def _():
            acc_ref[...] = jnp.zeros_like(acc_ref)

        latf = (i < n_lat).astype(F32)
        x = x_ref[...]
        gain = gain_ref[...]
        r = lax.rsqrt(jnp.mean(x * x, axis=-1, keepdims=True) + EPS)
        xn = x * r
        err = (xn * gain - tg_ref[...]) * latf
        dy = err * (1.0 / D)
        dxn = dy * gain
        dx_ref[...] = r * (dxn - xn * jnp.mean(dxn * xn, axis=-1, keepdims=True))
        acc_ref[0:1, :] += jnp.sum(dy * xn, axis=0, keepdims=True)
        acc_ref[1:2, :] += jnp.sum(err * err, axis=0, keepdims=True)

    return pl.pallas_call(
        body, name=name, grid=(t // TM,),
        in_specs=[_row_spec(D), _full_spec((1, D)), pl.BlockSpec((TM, D), lambda i: (jnp.minimum(i, n_lat - 1), 0))],
        out_specs=[_row_spec(D), _full_spec((8, D))],
        out_shape=[jax.ShapeDtypeStruct((t, D), F32), jax.ShapeDtypeStruct((8, D), F32)],
        compiler_params=_cp(1),
    )(x, gain, target)


def local_step(xs, target, mods, norm1, norm2, norm_f, lbs, gnorm, cw8, cb, wts, n_lat, on_grads, after_backward):
    t = xs.shape[0]
    saved = []
    cache = {}

    def W(name, idx, after=None):
        if (name, idx) not in cache:
            cache[(name, idx)] = wts(name, idx, after)
        return cache[(name, idx)]

    x = xs
    for i in range(DEPTH):
        j = i // 2
        rec = i % 2 == 0
        n1 = norm1[i:i + 1]
        n2 = norm2[i:i + 1]
        s = {"x_in": x}
        if rec:
            p = proj_fwd(x, n1, mods[i], 0, W("hin", j, x), n_lat, F32, f"hin_fwd_{i}")
            o0, st0, o1, st1 = gla_fwd(p, lbs[j], f"gla_fwd_{i}")
            ex = [o0, o1, p, gnorm[j:j + 1]]
            ex_specs = [_row_spec(D), _row_spec(D), _col_spec(4), _full_spec((1, D))]
            xm, y, ab = outproj_fwd(readout_prologue, ex, ex_specs, W("hout", j, o1), x, mods[i], 0, n_lat, f"hout_fwd_{i}")
            s.update(st0=st0, st1=st1)
        else:
            sft = 1 if j % 2 == 0 else CH
            p = proj_fwd(x, n1, mods[i], 0, W("cin", j, x), n_lat, BF16, f"cin_fwd_{i}")
            ex = _conv_args(sft, p, cw8[j], cb[j])
            ex_specs = _conv_specs(sft, t)
            xm, y, ab = outproj_fwd(make_conv_prologue(n_lat, sft), ex, ex_specs, W("cout", j, p), x, mods[i], 0, n_lat, f"cout_fwd_{i}")
        s.update(p=p, ex=ex, ex_specs=ex_specs, y_mix=y, ab_mix=ab, x_mid=xm)
        x, y2, ab2 = mlp_fwd(xm, n2, mods[i], W("w1", i, xm), W("w2", i, xm), n_lat, f"mlp_fwd_{i}")
        s.update(y_mlp=y2, ab_mlp=ab2)
        saved.append(s)

    dx, acc_loss = loss_bwd(x, norm_f, target, n_lat, "loss")
    small = {"norm_f": acc_loss[0:1], "norm1": [None] * DEPTH, "norm2": [None] * DEPTH, "dmod": [None] * DEPTH,
             "gnorm": [None] * 2, "lb": [None] * 2, "cw": [None] * 2, "cb": [None] * 2}
    bshape = lambda w: jax.ShapeDtypeStruct((t, w), BF16)
    token = jnp.zeros((8, 128), F32)
    for i in reversed(range(DEPTH)):
        j = i // 2
        rec = i % 2 == 0
        s = saved[i]
        n1 = norm1[i:i + 1]
        n2 = norm2[i:i + 1]
        dx, dyb, dp1, hb, acc_n2 = mlp_bwd(dx, s["y_mlp"], s["ab_mlp"], s["x_mid"], n2, mods[i], W("w1", i), W("w2", i), n_lat, token,
                                           f"mlp_bwd_{i}")
        token = on_grads(i, "mlp", {"w2": dw_tn(s["ab_mlp"], dyb, 4, True, True, token, f"w2_dw_{i}"),
                                    "w1": dw_tn(hb, dp1, 4, False, False, token, f"w1_dw_{i}")})
        if rec:
            dyb, acc_g1, (do, dgate) = outproj_bwd(
                readout_epilogue, s["ex"], s["ex_specs"], [_row_spec(D), _row_spec(D)],
                [jax.ShapeDtypeStruct((t, D), F32), bshape(D)], W("hout", j), dx, s["y_mix"], mods[i], 0, n_lat, token, f"hout_bwd_{i}")
            dz0, dv0, dq0, acc_l0 = gla_bwd(s["p"], lbs[j], do, s["st0"], 0, None, f"gla_bwd0_{i}")
            dz1, dv, dq, acc_l1 = gla_bwd(s["p"], lbs[j], do, s["st1"], 1, (dv0, dq0), f"gla_bwd1_{i}")
            dx, hb, dpb, acc_n1 = proj_bwd([dz0, dz1, dv, dq, dgate], W("hin", j), s["x_in"], n1, mods[i], 0, dx, n_lat, i == 0,
                                           f"hin_bwd_{i}")
            small["gnorm"][j] = acc_g1[2:3]
            small["lb"][j] = jnp.concatenate([acc_l0[0:1], acc_l1[0:1]], axis=0)
            mix = ("hout", "hin")
        else:
            sft = 1 if j % 2 == 0 else CH
            dyb, acc_g1, (dconv, dgb) = outproj_bwd(
                make_conv_epilogue(n_lat, sft), s["ex"], s["ex_specs"], [_row_spec(D), _row_spec(D)],
                [bshape(D), bshape(D)], W("cout", j), dx, s["y_mix"], mods[i], 0, n_lat, token, f"cout_bwd_{i}")
            dgc, dxi, acc_c = conv_bwd(dconv, s["p"], cw8[j], sft, n_lat, f"conv_bwd_{i}")
            dx, hb, dpb, acc_n1 = proj_bwd([dgb, dgc, dxi], W("cin", j), s["x_in"], n1, mods[i], 0, dx, n_lat, False, f"cin_bwd_{i}")
            small["cw"][j] = acc_c[0:3]
            small["cb"][j] = acc_c[3:4]
            mix = ("cout", "cin")
        small["norm1"][i] = acc_n1[0:1]
        small["norm2"][i] = acc_n2[0:1]
        z2 = jnp.zeros((2, D), F32)
        small["dmod"][i] = jnp.concatenate([acc_n1[1:3], acc_g1[0:1], acc_n2[1:3], acc_n2[5:6], z2,
                                            acc_n1[3:5], acc_g1[1:2], acc_n2[3:5], acc_n2[6:7], z2], axis=0)
        if i == 0:
            token = after_backward(small, token)
        token = on_grads(i, "mix", {mix[0]: dw_tn(s["ab_mix"], dyb, 1, False, False, token, f"{mix[0]}_dw_{i}"),
                                    mix[1]: dw_tn(hb, dpb, 4, False, False, token, f"{mix[1]}_dw_{i}")})
    return acc_loss[1:2], dx, token


RB = 256


def cast_to_slot(w2d, layer, k, chip1, name):
    c = w2d.shape[1]
    nblk = k // RB

    def body(chip_ref, w_ref, o_ref):
        o_ref[0] = w_ref[...].astype(BF16)

    return pl.pallas_call(
        body, name=name,
        grid_spec=pltpu.PrefetchScalarGridSpec(
            num_scalar_prefetch=1, grid=(nblk,),
            in_specs=[pl.BlockSpec((RB, c), lambda i, ch: (layer * nblk + i, 0))],
            out_specs=pl.BlockSpec((1, RB, c), lambda i, ch: (ch[0], i, 0))),
        out_shape=jax.ShapeDtypeStruct((4, k, c), BF16), compiler_params=_cp(1))(chip1, w2d)


def sum_slots(own, land, acc, layer, chip1, name):
    _, k, c = own.shape
    nblk = k // RB

    def body(chip_ref, own_ref, l1_ref, l2_ref, l3_ref, acc_ref, o_ref):
        o_ref[...] = ((own_ref[0].astype(F32) + l1_ref[0].astype(F32)) + l2_ref[0].astype(F32)) + l3_ref[0].astype(F32)

    slot = lambda d: pl.BlockSpec((1, RB, c), lambda i, ch: ((ch[0] + d) % 4, i, 0))
    return pl.pallas_call(
        body, name=name,
        grid_spec=pltpu.PrefetchScalarGridSpec(
            num_scalar_prefetch=1, grid=(nblk,),
            in_specs=[slot(0), slot(1), slot(2), slot(3), ANY],
            out_specs=pl.BlockSpec((RB, c), lambda i, ch: (layer * nblk + i, 0))),
        out_shape=jax.ShapeDtypeStruct(acc.shape, F32), input_output_aliases={5: 0}, compiler_params=_cp(1),
    )(chip1, own, land, land, land, acc)


def _adamw_math(w, g, m, v):
    m = ADAM_B1 * m + (1.0 - ADAM_B1) * g
    v = ADAM_B2 * v + (1.0 - ADAM_B2) * jnp.square(g)
    m_hat = m / (1.0 - ADAM_B1 ** ADAM_STEP)
    v_hat = v / (1.0 - ADAM_B2 ** ADAM_STEP)
    delta = -ADAM_LR * (m_hat / (jnp.sqrt(v_hat) + ADAM_EPS) + ADAM_WD * w)
    return delta, m, v


def adamw(gsrcs, w, m, v, name):
    r, c = w.shape
    rb = RB if r % RB == 0 else r
    n_g = len(gsrcs)

    def body(*refs):
        g = refs[0][...]
        for k in range(1, n_g):
            g = g + refs[k][...]
        w_ref, m_ref, v_ref, g_ref, d_ref, mo_ref, vo_ref = refs[n_g:]
        delta, mo, vo = _adamw_math(w_ref[...], g, m_ref[...], v_ref[...])
        g_ref[...] = g
        d_ref[...] = delta
        mo_ref[...] = mo
        vo_ref[...] = vo

    spec = pl.BlockSpec((rb, c), lambda i: (i, 0))
    shp = jax.ShapeDtypeStruct((r, c), F32)
    return pl.pallas_call(body, name=name, grid=(r // rb,), in_specs=[spec] * (n_g + 3), out_specs=[spec] * 4,
                          out_shape=[shp] * 4, compiler_params=_cp(1))(*gsrcs, w, m, v)


ADA_CB = 512


def ada_fwd(cvec, ada_w, bias, name):
    _, _, nc = ada_w.shape

    def body(c_ref, w_ref, b_ref, o_ref):
        cv = c_ref[...]
        a = (cv * _sigmoid(cv)).astype(BF16)
        o_ref[0] = _dot(a, w_ref[0].astype(BF16)) + b_ref[0]

    return pl.pallas_call(
        body, name=name, grid=(DEPTH, nc // ADA_CB),
        in_specs=[pl.BlockSpec((16, D), lambda i, j: (0, 0)), pl.BlockSpec((1, D, ADA_CB), lambda i, j: (i, 0, j)),
                  pl.BlockSpec((1, 1, ADA_CB), lambda i, j: (i, 0, j))],
        out_specs=pl.BlockSpec((1, 16, ADA_CB), lambda i, j: (i, 0, j)),
        out_shape=jax.ShapeDtypeStruct((DEPTH, 16, nc), F32), compiler_params=_cp(2),
    )(cvec, ada_w, bias)


def ada_bwd(cvec, dcols, ada_w, m, v, name):
    _, _, nc = ada_w.shape

    def body(c_ref, d_ref, w_ref, m_ref, v_ref, g_ref, dl_ref, mo_ref, vo_ref, acc_ref):
        @pl.when(jnp.logical_and(pl.program_id(0) == 0, pl.program_id(1) == 0))
        def _():
            acc_ref[...] = jnp.zeros_like(acc_ref)

        cv = c_ref[...]
        a = (cv * _sigmoid(cv)).astype(BF16)
        db = d_ref[0].astype(BF16)
        w = w_ref[0]
        g = _dot_tn(a, db)
        delta, mo, vo = _adamw_math(w, g, m_ref[0], v_ref[0])
        g_ref[0] = g
        dl_ref[0] = delta
        mo_ref[0] = mo
        vo_ref[0] = vo
        acc_ref[...] += _dot_nt(db[8:16, :], w.astype(BF16))

    wspec = pl.BlockSpec((1, D, ADA_CB), lambda i, j: (i, 0, j))
    wshape = jax.ShapeDtypeStruct(ada_w.shape, F32)
    return pl.pallas_call(
        body, name=name, grid=(DEPTH, nc // ADA_CB),
        in_specs=[pl.BlockSpec((16, D), lambda i, j: (0, 0)), pl.BlockSpec((1, 16, ADA_CB), lambda i, j: (i, 0, j)), wspec, wspec, wspec],
        out_specs=[wspec, wspec, wspec, wspec, pl.BlockSpec((8, D), lambda i, j: (0, 0))],
        out_shape=[wshape, wshape, wshape, wshape, jax.ShapeDtypeStruct((8, D), F32)], compiler_params=_cp(2),
    )(cvec, dcols, ada_w, m, v)


def _place():
    return lax.axis_index("x"), lax.axis_index("y"), lax.axis_index("c")


ANY = pl.BlockSpec(memory_space=pl.ANY)
VMEM_SPEC = pl.BlockSpec(memory_space=pltpu.VMEM)


def small_allgather(buf, deps, name):
    r, c = buf.shape
    n_dep = len(deps)

    def body(*refs):
        in_ref = refs[0]
        out_ref, send_sems, recv_sems, loc_sem = refs[1 + n_dep:]
        x, y, cc = _place()
        me = 4 * x + 2 * y + cc
        loc = pltpu.make_async_copy(in_ref, out_ref.at[me], loc_sem)
        loc.start()
        peers = []
        for k in range(1, 8):
            px = 1 - x if k & 4 else x
            py = 1 - y if k & 2 else y
            pc = 1 - cc if k & 1 else cc
            peers.append((px, py, pc))
        sends = []
        for k, peer in enumerate(peers):
            cp = pltpu.make_async_remote_copy(src_ref=in_ref, dst_ref=out_ref.at[me], send_sem=send_sems.at[k],
                                              recv_sem=recv_sems.at[k], device_id=peer, device_id_type=MESH)
            cp.start()
            sends.append(cp)
        for k, (px, py, pc) in enumerate(peers):
            pltpu.make_async_remote_copy(src_ref=in_ref, dst_ref=out_ref.at[4 * px + 2 * py + pc], send_sem=send_sems.at[k],
                                         recv_sem=recv_sems.at[k], device_id=(px, py, pc), device_id_type=MESH).wait_recv()
        for cp in sends:
            cp.wait_send()
        loc.wait()

    return pl.pallas_call(
        body, name=name, in_specs=[VMEM_SPEC] + [ANY] * n_dep, out_specs=VMEM_SPEC,
        out_shape=jax.ShapeDtypeStruct((8, r, c), buf.dtype),
        scratch_shapes=[pltpu.SemaphoreType.DMA((7,)), pltpu.SemaphoreType.DMA((7,)), pltpu.SemaphoreType.DMA],
    )(buf, *deps)


def _chip_peers(x, y):
    return [(1 - x, y), (x, 1 - y), (1 - x, 1 - y)]


HBM_SPEC = pl.BlockSpec(memory_space=pltpu.HBM)
SEM_SPEC = pl.BlockSpec(memory_space=pltpu.SEMAPHORE)
EFFECT = pltpu.SideEffectType.DATAFLOW_SIDE_EFFECTING


def _hbm(a):
    return pltpu.with_memory_space_constraint(a, pltpu.HBM)


def _split_copy(u, p, peer, dst_slot, chip, land_refs, src_refs, sem_refs, cc):
    px, py = peer
    src = land_refs[u].at[chip] if src_refs is None else src_refs[u].at[2 * px + py]
    return pltpu.make_async_remote_copy(src_ref=src, dst_ref=land_refs[u].at[dst_slot], send_sem=sem_refs[2 * u].at[p],
                                        recv_sem=sem_refs[2 * u + 1].at[p], device_id=(px, py, cc), device_id_type=MESH)


def split_start(lands, srcs, after, name):
    n = len(lands)
    ops = list(lands) + (list(srcs) if srcs is not None else [])
    n_ops = len(ops)

    def body(*refs):
        land_refs = refs[:n]
        src_refs = refs[n:n_ops] if srcs is not None else None
        sem_refs = refs[n_ops + 1:n_ops + 1 + 2 * n]
        x, y, cc = _place()
        chip = 2 * x + y
        for u in range(n):
            for p, peer in enumerate(_chip_peers(x, y)):
                _split_copy(u, p, peer, chip, chip, land_refs, src_refs, sem_refs, cc).start()
        refs[-1][...] = jnp.zeros((8, 128), F32)

    outs = pl.pallas_call(
        body, name=name, in_specs=[HBM_SPEC] * n_ops + [ANY],
        out_specs=[SEM_SPEC] * (2 * n) + [HBM_SPEC] * n_ops + [VMEM_SPEC],
        out_shape=[pltpu.SemaphoreType.DMA((3,))] * (2 * n) + [pltpu.HBM(a.shape, a.dtype) for a in ops]
        + [jax.ShapeDtypeStruct((8, 128), F32)],
        input_output_aliases={k: 2 * n + k for k in range(n_ops)},
        compiler_params=pltpu.CompilerParams(has_side_effects=EFFECT),
    )(*[_hbm(a) for a in ops], after)
    sems = list(outs[:2 * n])
    thru = list(outs[2 * n:2 * n + n_ops])
    return sems, thru[:n], thru[n:], outs[-1]


def split_wait(lands, srcs, sems, after, name):
    n = len(lands)
    ops = list(lands) + (list(srcs) if srcs is not None else [])
    n_ops = len(ops)

    def body(*refs):
        land_refs = refs[:n]
        src_refs = refs[n:n_ops] if srcs is not None else None
        sem_refs = refs[n_ops:n_ops + 2 * n]
        x, y, cc = _place()
        chip = 2 * x + y
        for u in range(n):
            for p, peer in enumerate(_chip_peers(x, y)):
                cp = _split_copy(u, p, peer, 2 * peer[0] + peer[1], chip, land_refs, src_refs, sem_refs, cc)
                cp.wait_send()
                cp.wait_recv()

    outs = pl.pallas_call(
        body, name=name, in_specs=[HBM_SPEC] * n_ops + [SEM_SPEC] * (2 * n) + [ANY],
        out_specs=[HBM_SPEC] * n_ops, out_shape=[pltpu.HBM(a.shape, a.dtype) for a in ops],
        input_output_aliases={k: k for k in range(n_ops)},
        compiler_params=pltpu.CompilerParams(has_side_effects=EFFECT),
    )(*ops, *sems, after)
    return list(outs[:n]), list(outs[n:])


def _sibling_copy(k, src_refs, zone_refs, sem_refs):
    x, y, cc = _place()
    return pltpu.make_async_remote_copy(src_ref=src_refs[k], dst_ref=zone_refs[k], send_sem=sem_refs[2 * k], recv_sem=sem_refs[2 * k + 1],
                                        device_id=(x, y, 1 - cc), device_id_type=MESH)


def sibling_start(parts, name):
    n = len(parts)
    ops = list(parts) + [lax.empty(p.shape, p.dtype) for p in parts]

    def body(*refs):
        for k in range(n):
            _sibling_copy(k, refs[:n], refs[n:2 * n], refs[2 * n:4 * n]).start()

    outs = pl.pallas_call(
        body, name=name, in_specs=[HBM_SPEC] * (2 * n),
        out_specs=[SEM_SPEC] * (2 * n) + [HBM_SPEC] * (2 * n),
        out_shape=[pltpu.SemaphoreType.DMA(())] * (2 * n) + [pltpu.HBM(a.shape, a.dtype) for a in ops],
        input_output_aliases={k: 2 * n + k for k in range(2 * n)},
        compiler_params=pltpu.CompilerParams(has_side_effects=EFFECT),
    )(*[_hbm(a) for a in ops])
    return list(outs[2 * n:3 * n]), list(outs[3 * n:]), list(outs[:2 * n])


def sibling_wait(parts, zones, sems, after, name):
    n = len(parts)

    def body(*refs):
        for k in range(n):
            cp = _sibling_copy(k, refs[:n], refs[n:2 * n], refs[2 * n:4 * n])
            cp.wait_send()
            cp.wait_recv()

    outs = pl.pallas_call(
        body, name=name, in_specs=[HBM_SPEC] * (2 * n) + [SEM_SPEC] * (2 * n) + [ANY],
        out_specs=[HBM_SPEC] * (2 * n), out_shape=[pltpu.HBM(a.shape, a.dtype) for a in list(parts) + list(zones)],
        input_output_aliases={k: k for k in range(2 * n)},
        compiler_params=pltpu.CompilerParams(has_side_effects=EFFECT),
    )(*parts, *zones, *sems, after)
    return list(outs[:n]), list(outs[n:])


SMALL_ROWS = 88
FIN_ROWS = 72


def small_finish(g3, g4, c_ctx, lbp, name):
    def body(g3_ref, g4_ref, cc_ref, lbp_ref, o_ref, s_ref):
        s = g3_ref[0]
        for k in range(1, 8):
            s = s + g3_ref[k]
        s_ref[...] = s
        for i in range(DEPTH):
            o_ref[8 * i:8 * i + 8, :] = s_ref[16 * i:16 * i + 8, :] + s_ref[16 * i + 8:16 * i + 16, :]
        acc = g4_ref[0]
        for k in (2, 4, 6):
            acc = acc + g4_ref[k]
        cc = cc_ref[...]
        sg = _sigmoid(cc)
        row = jnp.sum(acc, axis=0, keepdims=True) * (sg * (1.0 + cc * (1.0 - sg)))
        o_ref[32:40, :] = jnp.broadcast_to(row, (8, D))
        o_ref[40:64, :] = s_ref[64:88, :]
        o_ref[64:72, :] = jnp.zeros((8, D), F32)
        for d in range(2):
            pp = lbp_ref[2 * d:2 * d + 1, :] * lbp_ref[2 * d + 1:2 * d + 2, :] * s_ref[75 + d:76 + d, :]
            o_ref[64 + 2 * d:65 + 2 * d, :] = -pp
            o_ref[65 + 2 * d:66 + 2 * d, :] = pp

    return pl.pallas_call(
        body, name=name, in_specs=[VMEM_SPEC] * 4, out_specs=VMEM_SPEC,
        out_shape=jax.ShapeDtypeStruct((FIN_ROWS, D), F32),
        scratch_shapes=[pltpu.VMEM((SMALL_ROWS, D), F32)],
    )(g3, g4, c_ctx, lbp)


def _pack_rows(arrs):
    flat = jnp.concatenate([a.reshape(-1) for a in arrs])
    n = -(-flat.shape[0] // (8 * D)) * 8 * D
    return jnp.pad(flat, (0, n - flat.shape[0])).reshape(n // D, D)


def _unpack_rows(packed, shapes):
    flat = packed.reshape(-1)
    outs, off = [], 0
    for s in shapes:
        size = 1
        for k in s:
            size *= k
        outs.append(flat[off:off + size].reshape(s))
        off += size
    return outs


def _pad8(a):
    return jnp.pad(a, ((0, 8 - a.shape[0]), (0, 0)))


def kernel(x, c, ctx, c_ctx, ada_w, ada_b, norm1, norm2, norm_f, mlp_w1, mlp_w2, hgrn_w_in, hgrn_lb, hgrn_gnorm, hgrn_w_out, conv_w_in, conv_w, conv_b, conv_w_out, loss_target, m_c_ctx, m_ada_w, m_ada_b, m_norm1, m_norm2, m_norm_f, m_mlp_w1, m_mlp_w2, m_hgrn_w_in, m_hgrn_lb, m_hgrn_gnorm, m_hgrn_w_out, m_conv_w_in, m_conv_w, m_conv_b, m_conv_w_out, v_c_ctx, v_ada_w, v_ada_b, v_norm1, v_norm2, v_norm_f, v_mlp_w1, v_mlp_w2, v_hgrn_w_in, v_hgrn_lb, v_hgrn_gnorm, v_hgrn_w_out, v_conv_w_in, v_conv_w, v_conv_b, v_conv_w_out):
    xi, yi, ci = _place()
    me = 4 * xi + 2 * yi + ci
    chip = 2 * xi + yi
    seq = x.shape[1]
    assert ctx.shape[1] == TM and seq % TM == 0 and (seq + TM) % TMW == 0
    n_lat = seq // TM
    sd = D // 4
    nca = ada_w.shape[2]
    xs = jnp.concatenate([x[0], ctx[0]], axis=0)

    big = [(mlp_w1, m_mlp_w1, v_mlp_w1), (mlp_w2, m_mlp_w2, v_mlp_w2), (hgrn_w_in, m_hgrn_w_in, v_hgrn_w_in),
           (hgrn_w_out, m_hgrn_w_out, v_hgrn_w_out), (conv_w_in, m_conv_w_in, v_conv_w_in), (conv_w_out, m_conv_w_out, v_conv_w_out)]
    big_names = ["w1", "w2", "hin", "hout", "cin", "cout"]
    flat2 = lambda a: a.reshape(a.shape[0] * a.shape[1], a.shape[2])
    tensors = dict(zip(big_names, big))
    chip1 = jnp.reshape(chip, (1,)).astype(jnp.int32)
    order = []
    for i in range(DEPTH):
        order += [("hin", i // 2), ("hout", i // 2)] if i % 2 == 0 else [("cin", i // 2), ("cout", i // 2)]
        order += [("w1", i), ("w2", i)]
    lands = [cast_to_slot(flat2(tensors[n][0]), idx, tensors[n][0].shape[1], chip1, f"cast_{n}_{idx}") for n, idx in order]
    sh_rows = jnp.concatenate([hgrn_lb.reshape(4, sd), conv_w.reshape(6, sd), conv_b.reshape(2, sd)], axis=0)
    buf1 = jnp.concatenate([c, jnp.pad(sh_rows, ((0, 0), (0, D - sd))), jnp.zeros((3, D), F32)], axis=0)
    g1 = small_allgather(buf1, [], "gather_small_in")
    first_sems, first_lands, _, first_token = split_start(lands[:1], None, g1, "gather_start_first")
    cvec = jnp.concatenate([g1[:, 0, :], jnp.broadcast_to(c_ctx[None], (8, D))], axis=0)
    shf = g1[0::2, 1:13, :sd].transpose(1, 0, 2).reshape(12, D)
    lb_p = jax.nn.softmax(shf[0:4].reshape(2, 2, D), axis=1)
    lower = jnp.cumsum(lb_p, axis=1) - lb_p[:, :1]
    lbs = [lower[:, 0], lower[:, 1]]
    cw8 = [_pad8(shf[4:7]), _pad8(shf[7:10])]
    cb = [shf[10:11], shf[11:12]]

    bias = lax.dynamic_slice_in_dim(ada_b, chip * nca, nca, axis=1).reshape(DEPTH, 1, nca)
    ada_part = ada_fwd(cvec, ada_w, bias, "ada_fwd")
    g2 = small_allgather(ada_part.reshape(DEPTH * 16, nca), [first_token] + lands[1:], "gather_ada")
    ada_full = g2[0::2].reshape(4, DEPTH, 16, nca).transpose(1, 2, 0, 3).reshape(DEPTH, 16, 4 * nca)
    lat = lax.dynamic_slice_in_dim(ada_full, me, 1, axis=1)[:, 0]
    mods = [jnp.stack([_pad8(lat[i].reshape(6, D)), _pad8(ada_full[i, 8].reshape(6, D))]) for i in range(DEPTH)]

    rest_sems, rest_lands, _, rest_token = split_start(lands[1:], None, g2, "gather_start")
    w_sems = first_sems + rest_sems
    lands = first_lands + rest_lands
    unit = {key: u for u, key in enumerate(order)}

    def wts(n, idx, after):
        u = unit[(n, idx)]
        if u == 0:
            after = rest_token
        (w,), _ = split_wait([lands[u]], None, w_sems[2 * u:2 * u + 2], after, f"gather_wait_{n}_{idx}")
        return w.reshape(w.shape[0] * w.shape[1], w.shape[2]) if n in ("w2", "hout", "cout") else w

    started = []

    def on_grads(i, tag, g):
        names = sorted(g)
        gs = [g[n].reshape(4, g[n].shape[0] * g[n].shape[1] // 4, g[n].shape[2]) for n in names]
        sems, zones, srcs, token = split_start([lax.empty(a.shape, BF16) for a in gs], gs, chip1, f"grad_start_{tag}_{i}")
        started.append(([(n, i if n in ("w1", "w2") else i // 2) for n in names], sems, zones, srcs))
        return token

    done = {}
    acc = {n: lax.empty(flat2(w).shape, F32) for n, (w, _, _) in tensors.items()}
    early_names = ["w1", "w2", "cin", "cout"]
    late_names = ["hin", "hout"]

    def finish_units(group, after, name):
        units = [(key, sems[2 * u:2 * u + 2], zones[u], srcs[u]) for ks, sems, zones, srcs in group for u, key in enumerate(ks)]
        zones, srcs = split_wait([u[2] for u in units], [u[3] for u in units], [s for u in units for s in u[1]], after, name)
        for (key, _, _, _), zone, own in zip(units, zones, srcs):
            acc[key[0]] = sum_slots(own, zone, acc[key[0]], key[1], chip1, f"sum_{key[0]}_{key[1]}")

    def after_backward(small, token):
        rows3 = jnp.concatenate(small["dmod"] + small["norm1"] + small["norm2"] + [small["norm_f"]] + small["gnorm"]
                                + [small["lb"][1]] + small["cw"] + small["cb"] + [jnp.tile(token[0:3], (1, D // 128))], axis=0)
        g3 = small_allgather(rows3, [], "gather_small_out")
        dmat = g3[:, :64].reshape(8, DEPTH, 2, 8, D)[:, :, :, :6].transpose(1, 2, 0, 3, 4).reshape(DEPTH, 16, 6 * D)
        dcols = lax.dynamic_slice_in_dim(dmat, chip * nca, nca, axis=2)
        *done["ada"], acc4 = ada_bwd(cvec, dcols, ada_w, m_ada_w, v_ada_w, "ada_bwd")
        g4 = small_allgather(acc4, [], "gather_cctx")
        done["fin"] = small_finish(g3, g4, c_ctx[None], _pad8(lb_p.reshape(4, D)), "small_finish")
        finish_units(list(started), done["fin"], "grad_wait_early")
        done["sib_early"] = sibling_start([acc[n] for n in early_names], "sibling_start_early")
        return done["sib_early"][0][-1]

    lane, dx, last_token = local_step(xs, loss_target[0], mods, norm1, norm2, norm_f[None], lbs, hgrn_gnorm, cw8, cb, wts, n_lat,
                                      on_grads, after_backward)
    loss = lax.psum(0.5 * jnp.sum(lane) / D, ("x", "y", "c"))
    grad_x = dx[None]
    g_ada_w, d_ada_w, nm_ada_w, nv_ada_w = done["ada"]
    fin = done["fin"]
    cols = lambda a: lax.dynamic_slice_in_dim(a, chip * sd, sd, axis=a.ndim - 1)
    small_g = [fin[32], fin[0:32].reshape(DEPTH, 8, D)[:, :6].reshape(DEPTH, 6 * D), fin[40:44], fin[44:48], fin[48], fin[49:51],
               cols(fin[64:68].reshape(2, 2, D)), cols(fin[53:59].reshape(2, 3, D)), cols(fin[59:61])]
    small_w = [c_ctx, ada_b, norm1, norm2, norm_f, hgrn_gnorm, hgrn_lb, conv_w, conv_b]
    small_m = [m_c_ctx, m_ada_b, m_norm1, m_norm2, m_norm_f, m_hgrn_gnorm, m_hgrn_lb, m_conv_w, m_conv_b]
    small_v = [v_c_ctx, v_ada_b, v_norm1, v_norm2, v_norm_f, v_hgrn_gnorm, v_hgrn_lb, v_conv_w, v_conv_b]
    shapes = [w.shape for w in small_w]
    packed = adamw([_pack_rows(small_g)], _pack_rows(small_w), _pack_rows(small_m), _pack_rows(small_v), "adamw_small")
    s_g, s_d, s_m, s_v = [_unpack_rows(p, shapes) for p in packed]

    results = {}

    def finish_tensors(names, sib, after, name):
        mine, other = sibling_wait(*sib, after, name)
        for n, pm, po in zip(names, mine, other):
            w, m, v = tensors[n]
            results[n] = [a.reshape(w.shape) for a in adamw([pm, po], flat2(w), flat2(m), flat2(v), f"adamw_{n}")]

    finish_tensors(early_names, done["sib_early"], last_token, "sibling_wait_early")
    finish_units(started[-1:], results["cout"][0], "grad_wait_late")
    sib_late = sibling_start([acc[n] for n in late_names], "sibling_start_late")
    finish_tensors(late_names, sib_late, results["cin"][0], "sibling_wait_late")
    b_g, b_d, b_m, b_v = [[results[n][k] for n in big_names] for k in range(4)]

    def ordered(s, a, b):
        return [s[0], a, s[1], s[2], s[3], s[4], b[0], b[1], b[2], s[6], s[5], b[3], b[4], s[7], s[8], b[5]]

    return (loss, grad_x, *ordered(s_g, g_ada_w, b_g), *ordered(s_d, d_ada_w, b_d), *ordered(s_m, nm_ada_w, b_m),
            *ordered(s_v, nv_ada_w, b_v))
```

```python
import functools

import jax
import jax.numpy as jnp
from jax import lax
from jax.experimental import pallas as pl
from jax.experimental.pallas import tpu as pltpu

F32 = jnp.float32
BF16 = jnp.bfloat16
MESH = pl.DeviceIdType.MESH

D = 1024
HD = 128
NH = D // HD
CH = 64
TM = 256
TMW = 2816
EPS = 1e-6
DEPTH = 4
V7X_VMEM_BYTES = 64 * 1024 * 1024
VMEM_LIMIT = V7X_VMEM_BYTES - 8 * 1024 * 1024

ADAM_LR = 0.001
ADAM_B1 = 0.9
ADAM_B2 = 0.999
ADAM_EPS = 1e-08
ADAM_WD = 0.01
ADAM_STEP = 10


def _cp(n_grid):
    return pltpu.CompilerParams(dimension_semantics=("arbitrary",) * n_grid, vmem_limit_bytes=VMEM_LIMIT)


def _dot(a, b):
    return jnp.dot(a, b, preferred_element_type=F32)


def _dot_nt(a, b):
    return lax.dot_general(a, b, (((1,), (1,)), ((), ())), preferred_element_type=F32)


def _dot_tn(a, b):
    return lax.dot_general(a, b, (((0,), (0,)), ((), ())), preferred_element_type=F32)


def _sigmoid(z):
    return 1.0 / (1.0 + jnp.exp(-z))


def _norm_mod(x, gain, sh, sc):
    r = lax.rsqrt(jnp.mean(x * x, axis=-1, keepdims=True) + EPS)
    xn = x * r
    yn = xn * gain
    return r, xn, yn, yn * (1.0 + sc) + sh


def _row_spec(width):
    return pl.BlockSpec((TM, width), lambda i: (i, 0))


def _col_spec(col):
    return pl.BlockSpec((TM, D), lambda i: (i, col))


def _full_spec(shape):
    nd = len(shape)
    return pl.BlockSpec(shape, lambda i: (0,) * nd)


def _mod_spec(n_lat):
    return pl.BlockSpec((1, 8, D), lambda i: (i // n_lat, 0, 0))


def _f32(ref):
    return ref[...].astype(F32)


def proj_fwd(x, gain, mod, m0, w4, n_lat, dtype, name):
    t = x.shape[0]
    nb, _, ns = w4.shape

    def body(x_ref, gain_ref, mod_ref, w_ref, p_ref):
        _, _, _, h = _norm_mod(x_ref[...], gain_ref[...], mod_ref[0, m0:m0 + 1, :], mod_ref[0, m0 + 1:m0 + 2, :])
        hb = h.astype(BF16)
        for c in range(nb):
            p_ref[:, c * ns:(c + 1) * ns] = _dot(hb, w_ref[c]).astype(dtype)

    return pl.pallas_call(
        body, name=name, grid=(t // TM,),
        in_specs=[_row_spec(D), _full_spec((1, D)), _mod_spec(n_lat), _full_spec(w4.shape)],
        out_specs=_row_spec(nb * ns),
        out_shape=jax.ShapeDtypeStruct((t, nb * ns), dtype),
        compiler_params=_cp(1),
    )(x, gain, mod, w4)


def proj_bwd(parts, w4, x, gain, mod, m0, dx_in, n_lat, lat_only, name):
    t = x.shape[0]
    nb, _, ns = w4.shape
    n = nb * ns
    n_parts = len(parts)
    widths = [p.shape[1] for p in parts]
    offs = [sum(widths[:k]) for k in range(n_parts)]
    assert sum(widths) == n
    single = n_parts == 1

    def body(*refs):
        part_refs = refs[:n_parts]
        w_ref, x_ref, gain_ref, mod_ref, dxin_ref = refs[n_parts:n_parts + 5]
        rest = refs[n_parts + 5:]
        if single:
            dx_ref, hb_ref, acc_ref = rest
            src = part_refs[0]
        else:
            dx_ref, hb_ref, acc_ref, dpb_ref = rest
            for p_ref, off, w in zip(part_refs, offs, widths):
                dpb_ref[:, off:off + w] = p_ref[...]
            src = dpb_ref
        i = pl.program_id(0)

        @pl.when(i == 0)
        def _():
            acc_ref[...] = jnp.zeros_like(acc_ref)

        gain = gain_ref[...]
        sc = mod_ref[0, m0 + 1:m0 + 2, :]
        r, xn, yn, h = _norm_mod(x_ref[...], gain, mod_ref[0, m0:m0 + 1, :], sc)
        hb_ref[...] = h.astype(BF16)
        dh = _dot_nt(src[:, 0:ns], w_ref[0])
        for c in range(1, nb):
            dh = dh + _dot_nt(src[:, c * ns:(c + 1) * ns], w_ref[c])
        dsh = jnp.sum(dh, axis=0, keepdims=True)
        dsc = jnp.sum(dh * yn, axis=0, keepdims=True)
        dyn = dh * (1.0 + sc)
        dgain = jnp.sum(dyn * xn, axis=0, keepdims=True)
        dxn = dyn * gain
        dx = dxin_ref[...] + r * (dxn - xn * jnp.mean(dxn * xn, axis=-1, keepdims=True))
        if lat_only:
            @pl.when(i < n_lat)
            def _():
                dx_ref[...] = dx
        else:
            dx_ref[...] = dx
        latf = (i < n_lat).astype(F32)
        ctxf = 1.0 - latf
        acc_ref[0:1, :] += dgain
        acc_ref[1:2, :] += dsh * latf
        acc_ref[2:3, :] += dsc * latf
        acc_ref[3:4, :] += dsh * ctxf
        acc_ref[4:5, :] += dsc * ctxf

    dx_rows = n_lat * TM if lat_only else t
    dx_spec = pl.BlockSpec((TM, D), lambda i: (jnp.minimum(i, n_lat - 1), 0)) if lat_only else _row_spec(D)
    out_specs = [dx_spec, _row_spec(D), _full_spec((8, D))]
    out_shape = [jax.ShapeDtypeStruct((dx_rows, D), F32), jax.ShapeDtypeStruct((t, D), BF16), jax.ShapeDtypeStruct((8, D), F32)]
    if not single:
        out_specs.append(_row_spec(n))
        out_shape.append(jax.ShapeDtypeStruct((t, n), BF16))
    outs = pl.pallas_call(
        body, name=name, grid=(t // TM,),
        in_specs=[_row_spec(w) for w in widths]
        + [_full_spec(w4.shape), _row_spec(D), _full_spec((1, D)), _mod_spec(n_lat), _row_spec(D)],
        out_specs=out_specs, out_shape=out_shape, compiler_params=_cp(1),
    )(*parts, w4, x, gain, mod, dx_in)
    if single:
        return outs[0], outs[1], parts[0], outs[2]
    return outs[0], outs[1], outs[3], outs[2]


def dw_tn(a, b, nb, a_blocked, square_a, dep, name):
    t = a.shape[0]
    ka = a.shape[1] // nb if a_blocked else a.shape[1]
    kb = b.shape[1] if a_blocked else b.shape[1] // nb
    n_k = t // TMW

    def body(a_ref, b_ref, _, o_ref, acc):
        k = pl.program_id(1)

        @pl.when(k == 0)
        def _():
            acc[...] = jnp.zeros_like(acc)

        a = a_ref[...]
        acc[...] += _dot_tn(a * a if square_a else a, b_ref[...])

        @pl.when(k == n_k - 1)
        def _():
            o_ref[0] = acc[...].astype(BF16)

    a_spec = pl.BlockSpec((TMW, ka), (lambda j, k: (k, j)) if a_blocked else (lambda j, k: (k, 0)))
    b_spec = pl.BlockSpec((TMW, kb), (lambda j, k: (k, 0)) if a_blocked else (lambda j, k: (k, j)))
    return pl.pallas_call(
        body, name=name, grid=(nb, n_k),
        in_specs=[a_spec, b_spec, ANY],
        out_specs=pl.BlockSpec((1, ka, kb), lambda j, k: (j, 0, 0)),
        out_shape=jax.ShapeDtypeStruct((nb, ka, kb), BF16),
        scratch_shapes=[pltpu.VMEM((ka, kb), F32)],
        compiler_params=_cp(2),
    )(a, b, dep)


def outproj_fwd(prologue, extras, extra_specs, w, x, mod, m0, n_lat, name):
    t = x.shape[0]
    k = w.shape[0]
    n_extra = len(extras)

    def body(*refs):
        ex = refs[:n_extra]
        w_ref, x_ref, mod_ref, xo_ref, y_ref, ab_ref = refs[n_extra:]
        ab = prologue(pl.program_id(0), *ex).astype(BF16)
        ab_ref[...] = ab
        y = _dot(ab, w_ref[...])
        y_ref[...] = y.astype(BF16)
        xo_ref[...] = x_ref[...] + mod_ref[0, m0 + 2:m0 + 3, :] * y

    return pl.pallas_call(
        body, name=name, grid=(t // TM,),
        in_specs=list(extra_specs) + [_full_spec(w.shape), _row_spec(D), _mod_spec(n_lat)],
        out_specs=[_row_spec(D), _row_spec(D), _row_spec(k)],
        out_shape=[jax.ShapeDtypeStruct((t, D), F32), jax.ShapeDtypeStruct((t, D), BF16), jax.ShapeDtypeStruct((t, k), BF16)],
        compiler_params=_cp(1),
    )(*extras, w, x, mod)


def outproj_bwd(epilogue, extras, extra_specs, ep_out_specs, ep_out_shapes, w, dxn, y, mod, m0, n_lat, dep, name):
    t = dxn.shape[0]
    n_extra = len(extras)

    def body(*refs):
        ex = refs[:n_extra]
        w_ref, dxn_ref, y_ref, mod_ref, _, dyb_ref, acc_ref = refs[n_extra:n_extra + 7]
        ep_outs = refs[n_extra + 7:]
        i = pl.program_id(0)

        @pl.when(i == 0)
        def _():
            acc_ref[...] = jnp.zeros_like(acc_ref)

        dxv = dxn_ref[...]
        dyb = (dxv * mod_ref[0, m0 + 2:m0 + 3, :]).astype(BF16)
        dyb_ref[...] = dyb
        dg = jnp.sum(dxv * _f32(y_ref), axis=0, keepdims=True)
        latf = (i < n_lat).astype(F32)
        acc_ref[0:1, :] += dg * latf
        acc_ref[1:2, :] += dg * (1.0 - latf)
        epilogue(i, _dot_nt(dyb, w_ref[...]), ex, ep_outs, acc_ref)

    outs = pl.pallas_call(
        body, name=name, grid=(t // TM,),
        in_specs=list(extra_specs) + [_full_spec(w.shape), _row_spec(D), _row_spec(D), _mod_spec(n_lat), ANY],
        out_specs=[_row_spec(D), _full_spec((8, D))] + list(ep_out_specs),
        out_shape=[jax.ShapeDtypeStruct((t, D), BF16), jax.ShapeDtypeStruct((8, D), F32)] + list(ep_out_shapes),
        compiler_params=_cp(1),
    )(*extras, w, dxn, y, mod, dep)
    return outs[0], outs[1], outs[2:]


def mlp_fwd(x, gain, mod, w1, w2, n_lat, name):
    t = x.shape[0]
    nb, _, ns = w1.shape

    def body(x_ref, gain_ref, mod_ref, w1_ref, w2_ref, xo_ref, y_ref, rb_ref):
        x = x_ref[...]
        _, _, _, h = _norm_mod(x, gain_ref[...], mod_ref[0, 3:4, :], mod_ref[0, 4:5, :])
        hb = h.astype(BF16)
        y = None
        for c in range(nb):
            r = jnp.maximum(_dot(hb, w1_ref[c]), 0.0)
            rb_ref[:, c * ns:(c + 1) * ns] = r.astype(BF16)
            yc = _dot((r * r).astype(BF16), w2_ref[c * ns:(c + 1) * ns, :])
            y = yc if y is None else y + yc
        y_ref[...] = y.astype(BF16)
        xo_ref[...] = x + mod_ref[0, 5:6, :] * y

    return pl.pallas_call(
        body, name=name, grid=(t // TM,),
        in_specs=[_row_spec(D), _full_spec((1, D)), _mod_spec(n_lat), _full_spec(w1.shape), _full_spec(w2.shape)],
        out_specs=[_row_spec(D), _row_spec(D), _row_spec(nb * ns)],
        out_shape=[jax.ShapeDtypeStruct((t, D), F32), jax.ShapeDtypeStruct((t, D), BF16), jax.ShapeDtypeStruct((t, nb * ns), BF16)],
        compiler_params=_cp(1),
    )(x, gain, mod, w1, w2)


def mlp_bwd(dxn, head, y, ab, x, gain, mod, w1, w2, n_lat, dep, name):
    t = x.shape[0]
    nb, _, ns = w1.shape
    n_lead = 1 if head is None else 3

    def body(*refs):
        lead = refs[:n_lead]
        y_ref, rb_ref, x_ref, gain_ref, mod_ref, w1_ref, w2_ref, _ = refs[n_lead:n_lead + 8]
        dx_ref, dyb_ref, dp_ref, hb_ref, acc_ref = refs[n_lead + 8:n_lead + 13]
        i = pl.program_id(0)
        latf = (i < n_lat).astype(F32)
        ctxf = 1.0 - latf

        @pl.when(i == 0)
        def _():
            acc_ref[...] = jnp.zeros_like(acc_ref)

        if head is None:
            dxv = lead[0][...]
        else:
            xf_ref, gf_ref, tg_ref = lead
            head_ref = refs[n_lead + 13]

            @pl.when(i == 0)
            def _():
                head_ref[...] = jnp.zeros_like(head_ref)

            xf = xf_ref[...]
            gf = gf_ref[...]
            rf = lax.rsqrt(jnp.mean(xf * xf, axis=-1, keepdims=True) + EPS)
            xnf = xf * rf
            err = (xnf * gf - tg_ref[...]) * latf
            dyf = err * (1.0 / D)
            dxnf = dyf * gf
            dxv = rf * (dxnf - xnf * jnp.mean(dxnf * xnf, axis=-1, keepdims=True))
            head_ref[0:1, :] += jnp.sum(dyf * xnf, axis=0, keepdims=True)
            head_ref[1:2, :] += jnp.sum(err * err, axis=0, keepdims=True)
        dyb = (dxv * mod_ref[0, 5:6, :]).astype(BF16)
        dyb_ref[...] = dyb
        dg = jnp.sum(dxv * _f32(y_ref), axis=0, keepdims=True)
        gain = gain_ref[...]
        sc = mod_ref[0, 4:5, :]
        r, xn, yn, h = _norm_mod(x_ref[...], gain, mod_ref[0, 3:4, :], sc)
        hb_ref[...] = h.astype(BF16)
        dps = []
        for c in range(nb):
            cols = slice(c * ns, (c + 1) * ns)
            dp = (_dot_nt(dyb, w2_ref[cols, :]) * (2.0 * rb_ref[:, cols].astype(F32))).astype(BF16)
            dp_ref[:, cols] = dp
            dps.append(dp)
        dh = _dot_nt(dps[0], w1_ref[0])
        for c in range(1, nb):
            dh = dh + _dot_nt(dps[c], w1_ref[c])
        dsh = jnp.sum(dh, axis=0, keepdims=True)
        dsc = jnp.sum(dh * yn, axis=0, keepdims=True)
        dyn = dh * (1.0 + sc)
        dgain = jnp.sum(dyn * xn, axis=0, keepdims=True)
        dxn_ = dyn * gain
        dx_ref[...] = dxv + r * (dxn_ - xn * jnp.mean(dxn_ * xn, axis=-1, keepdims=True))
        acc_ref[0:1, :] += dgain
        acc_ref[1:2, :] += dsh * latf
        acc_ref[2:3, :] += dsc * latf
        acc_ref[3:4, :] += dsh * ctxf
        acc_ref[4:5, :] += dsc * ctxf
        acc_ref[5:6, :] += dg * latf
        acc_ref[6:7, :] += dg * ctxf

    if head is None:
        lead, lead_specs = [dxn], [_row_spec(D)]
    else:
        lead = list(head)
        lead_specs = [_row_spec(D), _full_spec((1, D)), pl.BlockSpec((TM, D), lambda i: (jnp.minimum(i, n_lat - 1), 0))]
    n_acc = 1 if head is None else 2
    return pl.pallas_call(
        body, name=name, grid=(t // TM,),
        in_specs=lead_specs + [_row_spec(D), _row_spec(nb * ns), _row_spec(D), _full_spec((1, D)), _mod_spec(n_lat),
                               _full_spec(w1.shape), _full_spec(w2.shape), ANY],
        out_specs=[_row_spec(D), _row_spec(D), _row_spec(nb * ns), _row_spec(D)] + [_full_spec((8, D))] * n_acc,
        out_shape=[jax.ShapeDtypeStruct((t, D), F32), jax.ShapeDtypeStruct((t, D), BF16), jax.ShapeDtypeStruct((t, nb * ns), BF16),
                   jax.ShapeDtypeStruct((t, D), BF16)] + [jax.ShapeDtypeStruct((8, D), F32)] * n_acc,
        compiler_params=_cp(1),
    )(*lead, y, ab, x, gain, mod, w1, w2, dep)


def readout_prologue(i, o0_ref, o1_ref, gate_ref, gn_ref):
    o = _f32(o0_ref) + _f32(o1_ref)
    gate = gate_ref[...]
    w = gn_ref[...] * (gate * _sigmoid(gate))
    pieces = []
    for h in range(NH):
        sl = slice(h * HD, (h + 1) * HD)
        oh = o[:, sl]
        pieces.append(oh * lax.rsqrt(jnp.mean(oh * oh, axis=-1, keepdims=True) + EPS) * w[:, sl])
    return jnp.concatenate(pieces, axis=1)


def readout_epilogue(i, da, ex, outs, acc_ref):
    o0_ref, o1_ref, gate_ref, gn_ref = ex
    do_ref, dgate_ref = outs
    o = _f32(o0_ref) + _f32(o1_ref)
    gate = gate_ref[...]
    gn = gn_ref[...]
    sg = _sigmoid(gate)
    silu = gate * sg
    dsilu = sg * (1.0 + gate * (1.0 - sg))
    for h in range(NH):
        sl = slice(h * HD, (h + 1) * HD)
        oh = o[:, sl]
        r = lax.rsqrt(jnp.mean(oh * oh, axis=-1, keepdims=True) + EPS)
        nh = oh * r
        dah = da[:, sl]
        acc_ref[2:3, sl] += jnp.sum(dah * nh * silu[:, sl], axis=0, keepdims=True)
        dgate_ref[:, sl] = (dah * nh * gn[:, sl] * dsilu[:, sl]).astype(BF16)
        dn = dah * gn[:, sl] * silu[:, sl]
        do_ref[:, sl] = r * (dn - nh * jnp.mean(dn * nh, axis=-1, keepdims=True))


def _seg_masks(i, n_lat):
    rows = lax.broadcasted_iota(jnp.int32, (TM, 1), 0)
    latf = (i < n_lat).astype(F32)
    ctxf = 1.0 - latf
    prev_ok = (rows % CH != 0).astype(F32) * latf + (rows != 0).astype(F32) * ctxf
    next_ok = (rows % CH != CH - 1).astype(F32) * latf + (rows != TM - 1).astype(F32) * ctxf
    return prev_ok, next_ok


def _shifts(i, n_lat, sft, cur, halo_prev, halo_next):
    if sft == 1:
        prev_ok, next_ok = _seg_masks(i, n_lat)
        return pltpu.roll(cur, 1, 0) * prev_ok, pltpu.roll(cur, TM - 1, 0) * next_ok
    has_prev = jnp.logical_and(i > 0, i < n_lat).astype(F32)
    has_next = (i < n_lat - 1).astype(F32)
    prev = jnp.concatenate([halo_prev * has_prev, cur[:TM - CH]], axis=0)
    nxt = jnp.concatenate([cur[CH:], halo_next * has_next], axis=0)
    return prev, nxt


def _conv_u(sft, ex):
    if sft == 1:
        gb_ref, gc_ref, xi_ref, cw_ref, cb_ref = ex
        return gb_ref, _f32(gc_ref) * _f32(xi_ref), None, None, cw_ref, cb_ref
    gb_ref, gc_ref, xi_ref, gcp_ref, xip_ref, gcn_ref, xin_ref, cw_ref, cb_ref = ex
    return gb_ref, _f32(gc_ref) * _f32(xi_ref), _f32(gcp_ref) * _f32(xip_ref), _f32(gcn_ref) * _f32(xin_ref), cw_ref, cb_ref


def _conv_value(i, n_lat, sft, ex):
    gb_ref, u, up, un, cw_ref, cb_ref = _conv_u(sft, ex)
    u_prev, u_next = _shifts(i, n_lat, sft, u, up, un)
    return gb_ref, cb_ref[...] + cw_ref[0:1, :] * u_prev + cw_ref[1:2, :] * u + cw_ref[2:3, :] * u_next


def make_conv_prologue(n_lat, sft):
    def prologue(i, *ex):
        gb_ref, conv = _conv_value(i, n_lat, sft, ex)
        return _f32(gb_ref) * conv
    return prologue


def make_conv_epilogue(n_lat, sft):
    def epilogue(i, da, ex, outs, acc_ref):
        gb_ref, conv = _conv_value(i, n_lat, sft, ex)
        outs[0][...] = (da * _f32(gb_ref)).astype(BF16)
        outs[1][...] = (da * conv).astype(BF16)
    return epilogue


def _conv_specs(sft, t):
    specs = [_col_spec(0), _col_spec(1), _col_spec(2)]
    if sft != 1:
        per = TM // CH
        last = t // CH - 1
        for fn in (lambda i: jnp.maximum(i * per - 1, 0), lambda i: jnp.minimum(i * per + per, last)):
            for col in (1, 2):
                specs.append(pl.BlockSpec((CH, D), functools.partial(lambda i, f, c: (f(i), c), f=fn, c=col)))
    return specs + [_full_spec((8, D)), _full_spec((1, D))]


def _conv_args(sft, p, cw8, cb):
    return [p] * (3 if sft == 1 else 7) + [cw8, cb]


def conv_bwd(dconv, p, cw8, sft, n_lat, name):
    t = dconv.shape[0]
    halo = sft != 1

    def body(*refs):
        if halo:
            dc_ref, dcp_ref, dcn_ref, gc_ref, xi_ref, gcp_ref, xip_ref, gcn_ref, xin_ref, cw_ref, dgc_ref, dxi_ref, acc_ref = refs
            up, un = _f32(gcp_ref) * _f32(xip_ref), _f32(gcn_ref) * _f32(xin_ref)
            dcp, dcn = _f32(dcp_ref), _f32(dcn_ref)
        else:
            dc_ref, gc_ref, xi_ref, cw_ref, dgc_ref, dxi_ref, acc_ref = refs
            up = un = dcp = dcn = None
        i = pl.program_id(0)

        @pl.when(i == 0)
        def _():
            acc_ref[...] = jnp.zeros_like(acc_ref)

        gc = _f32(gc_ref)
        xi = _f32(xi_ref)
        u = gc * xi
        dc = _f32(dc_ref)
        u_prev, u_next = _shifts(i, n_lat, sft, u, up, un)
        dc_prev, dc_next = _shifts(i, n_lat, sft, dc, dcp, dcn)
        acc_ref[0:1, :] += jnp.sum(dc * u_prev, axis=0, keepdims=True)
        acc_ref[1:2, :] += jnp.sum(dc * u, axis=0, keepdims=True)
        acc_ref[2:3, :] += jnp.sum(dc * u_next, axis=0, keepdims=True)
        acc_ref[3:4, :] += jnp.sum(dc, axis=0, keepdims=True)
        du = cw_ref[0:1, :] * dc_next + cw_ref[1:2, :] * dc + cw_ref[2:3, :] * dc_prev
        dgc_ref[...] = (du * xi).astype(BF16)
        dxi_ref[...] = (du * gc).astype(BF16)

    per = TM // CH
    last = t // CH - 1
    prev_i = lambda i: jnp.maximum(i * per - 1, 0)
    next_i = lambda i: jnp.minimum(i * per + per, last)
    if halo:
        in_specs = [_row_spec(D), pl.BlockSpec((CH, D), lambda i: (prev_i(i), 0)), pl.BlockSpec((CH, D), lambda i: (next_i(i), 0)),
                    _col_spec(1), _col_spec(2),
                    pl.BlockSpec((CH, D), lambda i: (prev_i(i), 1)), pl.BlockSpec((CH, D), lambda i: (prev_i(i), 2)),
                    pl.BlockSpec((CH, D), lambda i: (next_i(i), 1)), pl.BlockSpec((CH, D), lambda i: (next_i(i), 2)),
                    _full_spec((8, D))]
        args = [dconv, dconv, dconv, p, p, p, p, p, p, cw8]
    else:
        in_specs = [_row_spec(D), _col_spec(1), _col_spec(2), _full_spec((8, D))]
        args = [dconv, p, p, cw8]
    return pl.pallas_call(
        body, name=name, grid=(t // TM,), in_specs=in_specs,
        out_specs=[_row_spec(D), _row_spec(D), _full_spec((8, D))],
        out_shape=[jax.ShapeDtypeStruct((t, D), BF16), jax.ShapeDtypeStruct((t, D), BF16), jax.ShapeDtypeStruct((8, D), F32)],
        compiler_params=_cp(1),
    )(*args)


LOG2E = 1.4426950408889634


def _cumsum_matrix(reverse):
    r = lax.broadcasted_iota(jnp.int32, (CH, CH), 0)
    c = lax.broadcasted_iota(jnp.int32, (CH, CH), 1)
    return (r <= c if reverse else r >= c).astype(BF16)


def _chunk_cumsum(g, tri):
    hi = g.astype(BF16)
    lo = (g - hi.astype(F32)).astype(BF16)
    return _dot(tri, hi) + _dot(tri, lo)


def _gate_values(z, lb):
    sig = _sigmoid(z)
    f = lb + (1.0 - lb) * sig
    return sig, f


def _tri(direction, transposed):
    r = lax.broadcasted_iota(jnp.int32, (CH, CH), 0)
    c = lax.broadcasted_iota(jnp.int32, (CH, CH), 1)
    lower = (direction == 0) != transposed
    return r >= c if lower else r <= c


def _gla_rows(direction):
    return (CH // 2 - 1, CH - 1) if direction == 0 else (CH // 2, 0)


def _fwd_tile(direction, nt):
    return (lambda i: (i + nt - 1) % nt) if direction == 0 else (lambda i: nt - 1 - i)


def gla_fwd(p, lb2, name):
    t = p.shape[0]
    nt = t // TM
    per = TM // CH

    def body(z0_ref, v0_ref, q0_ref, z1_ref, v1_ref, q1_ref, lb_ref, o0_ref, s0_ref, o1_ref, s1_ref, st, q_s, k_s, c_s):
        @pl.when(pl.program_id(0) == 0)
        def _():
            st[...] = jnp.zeros_like(st)

        ins = ((z0_ref, v0_ref, q0_ref, o0_ref, s0_ref), (z1_ref, v1_ref, q1_ref, o1_ref, s1_ref))
        for d in range(2):
            z_ref, _, qr_ref, _, _ = ins[d]
            tri = _cumsum_matrix(d == 1)
            lb = lb_ref[d:d + 1, :]
            for ci in range(per):
                rows = slice(ci * CH, (ci + 1) * CH)
                _, f = _gate_values(z_ref[rows, :], lb)
                k_s[d, rows, :] = 1.0 - f
                c_s[d, rows, :] = _chunk_cumsum(jnp.log(f) * LOG2E, tri)
                qr = qr_ref[rows, :]
                q_s[d, rows, :] = qr * _sigmoid(qr)
        masks = (_tri(0, False), _tri(1, False))
        state = [[st[d, h] for h in range(NH)] for d in range(2)]
        for it in range(per):
            chunk = []
            for d in range(2):
                ref_row, last_row = _gla_rows(d)
                ci = it if d == 0 else per - 1 - it
                r0 = ci * CH
                rows = slice(r0, r0 + CH)
                cum = c_s[d, rows, :]
                ref = c_s[d, r0 + ref_row:r0 + ref_row + 1, :]
                last = c_s[d, r0 + last_row:r0 + last_row + 1, :]
                q = q_s[d, rows, :]
                k = k_s[d, rows, :]
                chunk.append(dict(
                    ci=ci, rows=rows, qh=(q * jnp.exp2(cum)).astype(BF16), qt=(q * jnp.exp2(cum - ref)).astype(BF16),
                    kt=(k * jnp.exp2(ref - cum)).astype(BF16), kb=(k * jnp.exp2(last - cum)).astype(BF16),
                    el=jnp.exp2(last), vb=ins[d][1][rows, :].astype(BF16)))
            heads = [slice(h * HD, (h + 1) * HD) for h in range(NH)]
            first = {}
            for h, sl in enumerate(heads):
                for d in range(2):
                    c = chunk[d]
                    s_t = state[d][h]
                    ins[d][4][c["ci"], h] = s_t
                    first[d, h] = (jnp.where(masks[d], _dot_nt(c["qt"][:, sl], c["kt"][:, sl]), 0.0).astype(BF16),
                                   _dot_nt(c["qh"][:, sl], s_t.astype(BF16)),
                                   s_t * c["el"][:, sl] + _dot_tn(c["vb"][:, sl], c["kb"][:, sl]))
            for h, sl in enumerate(heads):
                for d in range(2):
                    c = chunk[d]
                    sc, o_inter, s_new = first[d, h]
                    ins[d][3][c["rows"], sl] = (o_inter + _dot(sc, c["vb"][:, sl])).astype(BF16)
                    state[d][h] = s_new
        for d in range(2):
            for h in range(NH):
                st[d, h] = state[d][h]

    tiles = (_fwd_tile(0, nt), _fwd_tile(1, nt))
    tspec = lambda d, col: pl.BlockSpec((TM, D), lambda i: (tiles[d](i), col))
    sspec = lambda d: pl.BlockSpec((per, NH, HD, HD), lambda i: (tiles[d](i), 0, 0, 0))
    o_shape = jax.ShapeDtypeStruct((t, D), BF16)
    s_shape = jax.ShapeDtypeStruct((t // CH, NH, HD, HD), F32)
    return pl.pallas_call(
        body, name=name, grid=(nt,),
        in_specs=[tspec(0, 0), tspec(0, 2), tspec(0, 3), tspec(1, 1), tspec(1, 2), tspec(1, 3), _full_spec((2, D))],
        out_specs=[tspec(0, 0), sspec(0), tspec(1, 0), sspec(1)],
        out_shape=[o_shape, s_shape, o_shape, s_shape],
        scratch_shapes=[pltpu.VMEM((2, NH, HD, HD), F32)] + [pltpu.VMEM((2, TM, D), F32)] * 3,
        compiler_params=_cp(1),
    )(p, p, p, p, p, p, lb2)


def gla_bwd(p, lb2, do, states, direction, prev, name):
    t = p.shape[0]
    nt = t // TM
    per = TM // CH
    ref_row, last_row = _gla_rows(direction)
    tile = (lambda i: (2 * nt - 2 - i) % nt) if direction == 0 else (lambda i: i)
    final = prev is not None
    n_in = 8 if final else 6

    def body(*refs):
        z_ref, v_ref, qr_ref, lb_ref, do_ref, s_ref = refs[:6]
        dz_ref, dv_ref, dq_ref, acc_ref, dst, q_s, k_s, c_s, dq_s, dk_s, dl_s = refs[n_in:]

        @pl.when(pl.program_id(0) == 0)
        def _():
            dst[...] = jnp.zeros_like(dst)
            acc_ref[...] = jnp.zeros_like(acc_ref)

        mask = _tri(direction, False)
        mask_t = _tri(direction, True)
        tri = _cumsum_matrix(direction == 1)
        tri_t = _cumsum_matrix(direction == 0)
        is_last = lax.broadcasted_iota(jnp.int32, (CH, 1), 0) == last_row
        lb = lb_ref[direction:direction + 1, :]
        for ci in range(per):
            rows = slice(ci * CH, (ci + 1) * CH)
            _, f = _gate_values(z_ref[rows, :], lb)
            k_s[rows, :] = 1.0 - f
            c_s[rows, :] = _chunk_cumsum(jnp.log(f) * LOG2E, tri)
            qr = qr_ref[rows, :]
            q_s[rows, :] = qr * _sigmoid(qr)
        state = [dst[h] for h in range(NH)]
        for it in range(per):
            ci = per - 1 - it if direction == 0 else it
            r0 = ci * CH
            rows = slice(r0, r0 + CH)
            cum = c_s[rows, :]
            ref = c_s[r0 + ref_row:r0 + ref_row + 1, :]
            last = c_s[r0 + last_row:r0 + last_row + 1, :]
            q = q_s[rows, :]
            k = k_s[rows, :]
            e_h = jnp.exp2(cum)
            e_t = jnp.exp2(cum - ref)
            e_kt = jnp.exp2(ref - cum)
            e_kb = jnp.exp2(last - cum)
            el = jnp.exp2(last)
            qh = (q * e_h).astype(BF16)
            qt = (q * e_t).astype(BF16)
            kt = (k * e_kt).astype(BF16)
            kbf = k * e_kb
            kb = kbf.astype(BF16)
            vb = v_ref[rows, :].astype(BF16)
            dob = do_ref[rows, :].astype(BF16)
            heads = [slice(h * HD, (h + 1) * HD) for h in range(NH)]
            first = []
            for h, sl in enumerate(heads):
                s_t = s_ref[ci, h]
                ds_t = state[h]
                ds_b = ds_t.astype(BF16)
                d_a = jnp.where(mask, _dot_nt(dob[:, sl], vb[:, sl]), 0.0).astype(BF16)
                a_t = jnp.where(mask_t, _dot_nt(kt[:, sl], qt[:, sl]), 0.0).astype(BF16)
                d_at = jnp.where(mask_t, _dot_nt(vb[:, sl], dob[:, sl]), 0.0).astype(BF16)
                dkb = _dot(vb[:, sl], ds_b)
                dl_s[it:it + 1, sl] = (el[:, sl] * jnp.sum(ds_t * s_t, axis=0, keepdims=True)
                                       + jnp.sum(dkb * kbf[:, sl], axis=0, keepdims=True))
                state[h] = ds_t * el[:, sl] + _dot_tn(dob[:, sl], qh[:, sl])
                first.append((d_a, a_t, d_at, dkb, _dot_nt(kb[:, sl], ds_b), _dot(dob[:, sl], s_t.astype(BF16))))
            for h, sl in enumerate(heads):
                d_a, a_t, d_at, dkb, dv_state, dq_state = first[h]
                dv = _dot(a_t, dob[:, sl]) + dv_state
                dq_s[rows, sl] = dq_state * e_h[:, sl] + _dot(d_a, kt[:, sl]) * e_t[:, sl]
                dk_s[rows, sl] = _dot(d_at, qt[:, sl]) * e_kt[:, sl] + dkb * e_kb[:, sl]
                if final:
                    dv_ref[rows, sl] = (refs[6][rows, sl] + dv).astype(BF16)
                else:
                    dv_ref[rows, sl] = dv
        for h in range(NH):
            dst[h] = state[h]
        for it in range(per):
            ci = per - 1 - it if direction == 0 else it
            rows = slice(ci * CH, (ci + 1) * CH)
            dq = dq_s[rows, :]
            dk = dk_s[rows, :]
            dg = _chunk_cumsum(dq * q_s[rows, :] - dk * k_s[rows, :] + jnp.where(is_last, dl_s[it:it + 1, :], 0.0), tri_t)
            sig, f = _gate_values(z_ref[rows, :], lb)
            df = dg / f - dk
            acc_ref[0:1, :] += jnp.sum(df * (1.0 - sig), axis=0, keepdims=True)
            dz_ref[rows, :] = (df * (1.0 - lb) * sig * (1.0 - sig)).astype(BF16)
            if final:
                qr = qr_ref[rows, :]
                sq = _sigmoid(qr)
                dq_ref[rows, :] = ((refs[7][rows, :] + dq) * (sq * (1.0 + qr * (1.0 - sq)))).astype(BF16)
            else:
                dq_ref[rows, :] = dq

    tspec = lambda col: pl.BlockSpec((TM, D), lambda i: (tile(i), col))
    sspec = pl.BlockSpec((per, NH, HD, HD), lambda i: (tile(i), 0, 0, 0))
    in_specs = [tspec(direction), tspec(2), tspec(3), _full_spec((2, D)), tspec(0), sspec]
    args = [p, p, p, lb2, do, states]
    if final:
        in_specs += [tspec(0), tspec(0)]
        args += list(prev)
    odt = BF16 if final else F32
    return pl.pallas_call(
        body, name=name, grid=(nt,), in_specs=in_specs,
        out_specs=[tspec(0), tspec(0), tspec(0), _full_spec((8, D))],
        out_shape=[jax.ShapeDtypeStruct((t, D), BF16), jax.ShapeDtypeStruct((t, D), odt), jax.ShapeDtypeStruct((t, D), odt),
                   jax.ShapeDtypeStruct((8, D), F32)],
        scratch_shapes=[pltpu.VMEM((NH, HD, HD), F32)] + [pltpu.VMEM((TM, D), F32)] * 5 + [pltpu.VMEM((8, D), F32)],
        compiler_params=_cp(1),
    )(*args)


def local_step(xs, target, mods, norm1, norm2, norm_f, lbs, gnorm, cw8, cb, wts, n_lat, on_grads, after_backward):
    t = xs.shape[0]
    saved = []
    cache = {}

    def W(name, idx, after=None):
        if (name, idx) not in cache:
            cache[(name, idx)] = wts(name, idx, after)
        return cache[(name, idx)]

    x = xs
    for i in range(DEPTH):
        j = i // 2
        rec = i % 2 == 0
        n1 = norm1[i:i + 1]
        n2 = norm2[i:i + 1]
        s = {"x_in": x}
        if rec:
            p = proj_fwd(x, n1, mods[i], 0, W("hin", j, x), n_lat, F32, f"hin_fwd_{i}")
            o0, st0, o1, st1 = gla_fwd(p, lbs[j], f"gla_fwd_{i}")
            ex = [o0, o1, p, gnorm[j:j + 1]]
            ex_specs = [_row_spec(D), _row_spec(D), _col_spec(4), _full_spec((1, D))]
            xm, y, ab = outproj_fwd(readout_prologue, ex, ex_specs, W("hout", j, o1), x, mods[i], 0, n_lat, f"hout_fwd_{i}")
            s.update(st0=st0, st1=st1)
        else:
            sft = 1 if j % 2 == 0 else CH
            p = proj_fwd(x, n1, mods[i], 0, W("cin", j, x), n_lat, BF16, f"cin_fwd_{i}")
            ex = _conv_args(sft, p, cw8[j], cb[j])
            ex_specs = _conv_specs(sft, t)
            xm, y, ab = outproj_fwd(make_conv_prologue(n_lat, sft), ex, ex_specs, W("cout", j, p), x, mods[i], 0, n_lat, f"cout_fwd_{i}")
        s.update(p=p, ex=ex, ex_specs=ex_specs, y_mix=y, ab_mix=ab, x_mid=xm)
        x, y2, ab2 = mlp_fwd(xm, n2, mods[i], W("w1", i, xm), W("w2", i, xm), n_lat, f"mlp_fwd_{i}")
        s.update(y_mlp=y2, ab_mlp=ab2)
        saved.append(s)

    small = {"norm_f": None, "norm1": [None] * DEPTH, "norm2": [None] * DEPTH, "dmod": [None] * DEPTH,
             "gnorm": [None] * 2, "lb": [None] * 2, "cw": [None] * 2, "cb": [None] * 2}
    bshape = lambda w: jax.ShapeDtypeStruct((t, w), BF16)
    token = jnp.zeros((8, 128), F32)
    dx = None
    for i in reversed(range(DEPTH)):
        j = i // 2
        rec = i % 2 == 0
        s = saved[i]
        n1 = norm1[i:i + 1]
        n2 = norm2[i:i + 1]
        head = (x, norm_f, target) if i == DEPTH - 1 else None
        dx, dyb, dp1, hb, acc_n2, *acc_head = mlp_bwd(dx, head, s["y_mlp"], s["ab_mlp"], s["x_mid"], n2, mods[i], W("w1", i), W("w2", i),
                                                      n_lat, token, f"mlp_bwd_{i}")
        if head is not None:
            acc_loss = acc_head[0]
            small["norm_f"] = acc_loss[0:1]
        token = on_grads(i, "mlp", {"w2": dw_tn(s["ab_mlp"], dyb, 4, True, True, token, f"w2_dw_{i}"),
                                    "w1": dw_tn(hb, dp1, 4, False, False, token, f"w1_dw_{i}")})
        if rec:
            dyb, acc_g1, (do, dgate) = outproj_bwd(
                readout_epilogue, s["ex"], s["ex_specs"], [_row_spec(D), _row_spec(D)],
                [jax.ShapeDtypeStruct((t, D), F32), bshape(D)], W("hout", j), dx, s["y_mix"], mods[i], 0, n_lat, token, f"hout_bwd_{i}")
            dz0, dv0, dq0, acc_l0 = gla_bwd(s["p"], lbs[j], do, s["st0"], 0, None, f"gla_bwd0_{i}")
            dz1, dv, dq, acc_l1 = gla_bwd(s["p"], lbs[j], do, s["st1"], 1, (dv0, dq0), f"gla_bwd1_{i}")
            dx, hb, dpb, acc_n1 = proj_bwd([dz0, dz1, dv, dq, dgate], W("hin", j), s["x_in"], n1, mods[i], 0, dx, n_lat, i == 0,
                                           f"hin_bwd_{i}")
            small["gnorm"][j] = acc_g1[2:3]
            small["lb"][j] = jnp.concatenate([acc_l0[0:1], acc_l1[0:1]], axis=0)
            mix = ("hout", "hin")
        else:
            sft = 1 if j % 2 == 0 else CH
            dyb, acc_g1, (dconv, dgb) = outproj_bwd(
                make_conv_epilogue(n_lat, sft), s["ex"], s["ex_specs"], [_row_spec(D), _row_spec(D)],
                [bshape(D), bshape(D)], W("cout", j), dx, s["y_mix"], mods[i], 0, n_lat, token, f"cout_bwd_{i}")
            dgc, dxi, acc_c = conv_bwd(dconv, s["p"], cw8[j], sft, n_lat, f"conv_bwd_{i}")
            dx, hb, dpb, acc_n1 = proj_bwd([dgb, dgc, dxi], W("cin", j), s["x_in"], n1, mods[i], 0, dx, n_lat, False, f"cin_bwd_{i}")
            small["cw"][j] = acc_c[0:3]
            small["cb"][j] = acc_c[3:4]
            mix = ("cout", "cin")
        small["norm1"][i] = acc_n1[0:1]
        small["norm2"][i] = acc_n2[0:1]
        z2 = jnp.zeros((2, D), F32)
        small["dmod"][i] = jnp.concatenate([acc_n1[1:3], acc_g1[0:1], acc_n2[1:3], acc_n2[5:6], z2,
                                            acc_n1[3:5], acc_g1[1:2], acc_n2[3:5], acc_n2[6:7], z2], axis=0)
        if i == 0:
            token = after_backward(small, token)
        token = on_grads(i, "mix", {mix[0]: dw_tn(s["ab_mix"], dyb, 1, False, False, token, f"{mix[0]}_dw_{i}"),
                                    mix[1]: dw_tn(hb, dpb, 4, False, False, token, f"{mix[1]}_dw_{i}")})
    return acc_loss[1:2], dx, token


RB = 256


def cast_to_slot(w2d, layer, k, chip1, name):
    c = w2d.shape[1]
    nblk = k // RB

    def body(chip_ref, w_ref, o_ref):
        o_ref[0] = w_ref[...].astype(BF16)

    return pl.pallas_call(
        body, name=name,
        grid_spec=pltpu.PrefetchScalarGridSpec(
            num_scalar_prefetch=1, grid=(nblk,),
            in_specs=[pl.BlockSpec((RB, c), lambda i, ch: (layer * nblk + i, 0))],
            out_specs=pl.BlockSpec((1, RB, c), lambda i, ch: (ch[0], i, 0))),
        out_shape=jax.ShapeDtypeStruct((4, k, c), BF16), compiler_params=_cp(1))(chip1, w2d)


def sum_slots(own, land, acc, layer, chip1, name):
    _, k, c = own.shape
    nblk = k // RB

    def body(chip_ref, own_ref, l1_ref, l2_ref, l3_ref, acc_ref, o_ref):
        o_ref[...] = ((own_ref[0].astype(F32) + l1_ref[0].astype(F32)) + l2_ref[0].astype(F32)) + l3_ref[0].astype(F32)

    slot = lambda d: pl.BlockSpec((1, RB, c), lambda i, ch: ((ch[0] + d) % 4, i, 0))
    return pl.pallas_call(
        body, name=name,
        grid_spec=pltpu.PrefetchScalarGridSpec(
            num_scalar_prefetch=1, grid=(nblk,),
            in_specs=[slot(0), slot(1), slot(2), slot(3), ANY],
            out_specs=pl.BlockSpec((RB, c), lambda i, ch: (layer * nblk + i, 0))),
        out_shape=jax.ShapeDtypeStruct(acc.shape, F32), input_output_aliases={5: 0}, compiler_params=_cp(1),
    )(chip1, own, land, land, land, acc)


def _adamw_math(w, g, m, v):
    m = ADAM_B1 * m + (1.0 - ADAM_B1) * g
    v = ADAM_B2 * v + (1.0 - ADAM_B2) * jnp.square(g)
    m_hat = m / (1.0 - ADAM_B1 ** ADAM_STEP)
    v_hat = v / (1.0 - ADAM_B2 ** ADAM_STEP)
    delta = -ADAM_LR * (m_hat / (jnp.sqrt(v_hat) + ADAM_EPS) + ADAM_WD * w)
    return delta, m, v


def adamw(gsrcs, w, m, v, name):
    r, c = w.shape
    rb = RB if r % RB == 0 else r
    n_g = len(gsrcs)

    def body(*refs):
        g = refs[0][...]
        for k in range(1, n_g):
            g = g + refs[k][...]
        w_ref, m_ref, v_ref, g_ref, d_ref, mo_ref, vo_ref = refs[n_g:]
        delta, mo, vo = _adamw_math(w_ref[...], g, m_ref[...], v_ref[...])
        g_ref[...] = g
        d_ref[...] = delta
        mo_ref[...] = mo
        vo_ref[...] = vo

    spec = pl.BlockSpec((rb, c), lambda i: (i, 0))
    shp = jax.ShapeDtypeStruct((r, c), F32)
    return pl.pallas_call(body, name=name, grid=(r // rb,), in_specs=[spec] * (n_g + 3), out_specs=[spec] * 4,
                          out_shape=[shp] * 4, compiler_params=_cp(1))(*gsrcs, w, m, v)


ADA_CB = 512


def ada_fwd(cvec, ada_w, bias, name):
    _, _, nc = ada_w.shape

    def body(c_ref, w_ref, b_ref, o_ref):
        cv = c_ref[...]
        a = (cv * _sigmoid(cv)).astype(BF16)
        o_ref[0] = _dot(a, w_ref[0].astype(BF16)) + b_ref[0]

    return pl.pallas_call(
        body, name=name, grid=(DEPTH, nc // ADA_CB),
        in_specs=[pl.BlockSpec((16, D), lambda i, j: (0, 0)), pl.BlockSpec((1, D, ADA_CB), lambda i, j: (i, 0, j)),
                  pl.BlockSpec((1, 1, ADA_CB), lambda i, j: (i, 0, j))],
        out_specs=pl.BlockSpec((1, 16, ADA_CB), lambda i, j: (i, 0, j)),
        out_shape=jax.ShapeDtypeStruct((DEPTH, 16, nc), F32), compiler_params=_cp(2),
    )(cvec, ada_w, bias)


def ada_bwd(cvec, dcols, ada_w, m, v, name):
    _, _, nc = ada_w.shape

    def body(c_ref, d_ref, w_ref, m_ref, v_ref, g_ref, dl_ref, mo_ref, vo_ref, acc_ref):
        @pl.when(jnp.logical_and(pl.program_id(0) == 0, pl.program_id(1) == 0))
        def _():
            acc_ref[...] = jnp.zeros_like(acc_ref)

        cv = c_ref[...]
        a = (cv * _sigmoid(cv)).astype(BF16)
        db = d_ref[0].astype(BF16)
        w = w_ref[0]
        g = _dot_tn(a, db)
        delta, mo, vo = _adamw_math(w, g, m_ref[0], v_ref[0])
        g_ref[0] = g
        dl_ref[0] = delta
        mo_ref[0] = mo
        vo_ref[0] = vo
        acc_ref[...] += _dot_nt(db[8:16, :], w.astype(BF16))

    wspec = pl.BlockSpec((1, D, ADA_CB), lambda i, j: (i, 0, j))
    wshape = jax.ShapeDtypeStruct(ada_w.shape, F32)
    return pl.pallas_call(
        body, name=name, grid=(DEPTH, nc // ADA_CB),
        in_specs=[pl.BlockSpec((16, D), lambda i, j: (0, 0)), pl.BlockSpec((1, 16, ADA_CB), lambda i, j: (i, 0, j)), wspec, wspec, wspec],
        out_specs=[wspec, wspec, wspec, wspec, pl.BlockSpec((8, D), lambda i, j: (0, 0))],
        out_shape=[wshape, wshape, wshape, wshape, jax.ShapeDtypeStruct((8, D), F32)], compiler_params=_cp(2),
    )(cvec, dcols, ada_w, m, v)


def _place():
    return lax.axis_index("x"), lax.axis_index("y"), lax.axis_index("c")


ANY = pl.BlockSpec(memory_space=pl.ANY)
VMEM_SPEC = pl.BlockSpec(memory_space=pltpu.VMEM)


def small_allgather(buf, deps, name):
    r, c = buf.shape
    n_dep = len(deps)

    def body(*refs):
        in_ref = refs[0]
        out_ref, send_sems, recv_sems, loc_sem = refs[1 + n_dep:]
        x, y, cc = _place()
        me = 4 * x + 2 * y + cc
        loc = pltpu.make_async_copy(in_ref, out_ref.at[me], loc_sem)
        loc.start()
        peers = []
        for k in range(1, 8):
            px = 1 - x if k & 4 else x
            py = 1 - y if k & 2 else y
            pc = 1 - cc if k & 1 else cc
            peers.append((px, py, pc))
        sends = []
        for k, peer in enumerate(peers):
            cp = pltpu.make_async_remote_copy(src_ref=in_ref, dst_ref=out_ref.at[me], send_sem=send_sems.at[k],
                                              recv_sem=recv_sems.at[k], device_id=peer, device_id_type=MESH)
            cp.start()
            sends.append(cp)
        for k, (px, py, pc) in enumerate(peers):
            pltpu.make_async_remote_copy(src_ref=in_ref, dst_ref=out_ref.at[4 * px + 2 * py + pc], send_sem=send_sems.at[k],
                                         recv_sem=recv_sems.at[k], device_id=(px, py, pc), device_id_type=MESH).wait_recv()
        for cp in sends:
            cp.wait_send()
        loc.wait()

    return pl.pallas_call(
        body, name=name, in_specs=[VMEM_SPEC] + [ANY] * n_dep, out_specs=VMEM_SPEC,
        out_shape=jax.ShapeDtypeStruct((8, r, c), buf.dtype),
        scratch_shapes=[pltpu.SemaphoreType.DMA((7,)), pltpu.SemaphoreType.DMA((7,)), pltpu.SemaphoreType.DMA],
    )(buf, *deps)


def _chip_peers(x, y):
    return [(1 - x, y), (x, 1 - y), (1 - x, 1 - y)]


HBM_SPEC = pl.BlockSpec(memory_space=pltpu.HBM)
SEM_SPEC = pl.BlockSpec(memory_space=pltpu.SEMAPHORE)
EFFECT = pltpu.SideEffectType.DATAFLOW_SIDE_EFFECTING


def _hbm(a):
    return pltpu.with_memory_space_constraint(a, pltpu.HBM)


def _split_copy(u, p, peer, dst_slot, chip, land_refs, src_refs, sem_refs, cc):
    px, py = peer
    src = land_refs[u].at[chip] if src_refs is None else src_refs[u].at[2 * px + py]
    return pltpu.make_async_remote_copy(src_ref=src, dst_ref=land_refs[u].at[dst_slot], send_sem=sem_refs[2 * u].at[p],
                                        recv_sem=sem_refs[2 * u + 1].at[p], device_id=(px, py, cc), device_id_type=MESH)


def split_start(lands, srcs, after, name):
    n = len(lands)
    ops = list(lands) + (list(srcs) if srcs is not None else [])
    n_ops = len(ops)

    def body(*refs):
        land_refs = refs[:n]
        src_refs = refs[n:n_ops] if srcs is not None else None
        sem_refs = refs[n_ops + 1:n_ops + 1 + 2 * n]
        x, y, cc = _place()
        chip = 2 * x + y
        for u in range(n):
            for p, peer in enumerate(_chip_peers(x, y)):
                _split_copy(u, p, peer, chip, chip, land_refs, src_refs, sem_refs, cc).start()
        refs[-1][...] = jnp.zeros((8, 128), F32)

    outs = pl.pallas_call(
        body, name=name, in_specs=[HBM_SPEC] * n_ops + [ANY],
        out_specs=[SEM_SPEC] * (2 * n) + [HBM_SPEC] * n_ops + [VMEM_SPEC],
        out_shape=[pltpu.SemaphoreType.DMA((3,))] * (2 * n) + [pltpu.HBM(a.shape, a.dtype) for a in ops]
        + [jax.ShapeDtypeStruct((8, 128), F32)],
        input_output_aliases={k: 2 * n + k for k in range(n_ops)},
        compiler_params=pltpu.CompilerParams(has_side_effects=EFFECT),
    )(*[_hbm(a) for a in ops], after)
    sems = list(outs[:2 * n])
    thru = list(outs[2 * n:2 * n + n_ops])
    return sems, thru[:n], thru[n:], outs[-1]


def split_wait(lands, srcs, sems, after, name):
    n = len(lands)
    ops = list(lands) + (list(srcs) if srcs is not None else [])
    n_ops = len(ops)

    def body(*refs):
        land_refs = refs[:n]
        src_refs = refs[n:n_ops] if srcs is not None else None
        sem_refs = refs[n_ops:n_ops + 2 * n]
        x, y, cc = _place()
        chip = 2 * x + y
        for u in range(n):
            for p, peer in enumerate(_chip_peers(x, y)):
                cp = _split_copy(u, p, peer, 2 * peer[0] + peer[1], chip, land_refs, src_refs, sem_refs, cc)
                cp.wait_send()
                cp.wait_recv()

    outs = pl.pallas_call(
        body, name=name, in_specs=[HBM_SPEC] * n_ops + [SEM_SPEC] * (2 * n) + [ANY],
        out_specs=[HBM_SPEC] * n_ops, out_shape=[pltpu.HBM(a.shape, a.dtype) for a in ops],
        input_output_aliases={k: k for k in range(n_ops)},
        compiler_params=pltpu.CompilerParams(has_side_effects=EFFECT),
    )(*ops, *sems, after)
    return list(outs[:n]), list(outs[n:])


def _sibling_copy(k, src_refs, zone_refs, sem_refs):
    x, y, cc = _place()
    return pltpu.make_async_remote_copy(src_ref=src_refs[k], dst_ref=zone_refs[k], send_sem=sem_refs[2 * k], recv_sem=sem_refs[2 * k + 1],
                                        device_id=(x, y, 1 - cc), device_id_type=MESH)


def sibling_start(parts, name):
    n = len(parts)
    ops = list(parts) + [lax.empty(p.shape, p.dtype) for p in parts]

    def body(*refs):
        for k in range(n):
            _sibling_copy(k, refs[:n], refs[n:2 * n], refs[2 * n:4 * n]).start()

    outs = pl.pallas_call(
        body, name=name, in_specs=[HBM_SPEC] * (2 * n),
        out_specs=[SEM_SPEC] * (2 * n) + [HBM_SPEC] * (2 * n),
        out_shape=[pltpu.SemaphoreType.DMA(())] * (2 * n) + [pltpu.HBM(a.shape, a.dtype) for a in ops],
        input_output_aliases={k: 2 * n + k for k in range(2 * n)},
        compiler_params=pltpu.CompilerParams(has_side_effects=EFFECT),
    )(*[_hbm(a) for a in ops])
    return list(outs[2 * n:3 * n]), list(outs[3 * n:]), list(outs[:2 * n])


def sibling_wait(parts, zones, sems, after, name):
    n = len(parts)

    def body(*refs):
        for k in range(n):
            cp = _sibling_copy(k, refs[:n], refs[n:2 * n], refs[2 * n:4 * n])
            cp.wait_send()
            cp.wait_recv()

    outs = pl.pallas_call(
        body, name=name, in_specs=[HBM_SPEC] * (2 * n) + [SEM_SPEC] * (2 * n) + [ANY],
        out_specs=[HBM_SPEC] * (2 * n), out_shape=[pltpu.HBM(a.shape, a.dtype) for a in list(parts) + list(zones)],
        input_output_aliases={k: k for k in range(2 * n)},
        compiler_params=pltpu.CompilerParams(has_side_effects=EFFECT),
    )(*parts, *zones, *sems, after)
    return list(outs[:n]), list(outs[n:])


SMALL_ROWS = 88
FIN_ROWS = 72


def small_finish(g3, g4, c_ctx, lbp, name):
    def body(g3_ref, g4_ref, cc_ref, lbp_ref, o_ref, s_ref):
        s = g3_ref[0]
        for k in range(1, 8):
            s = s + g3_ref[k]
        s_ref[...] = s
        for i in range(DEPTH):
            o_ref[8 * i:8 * i + 8, :] = s_ref[16 * i:16 * i + 8, :] + s_ref[16 * i + 8:16 * i + 16, :]
        acc = g4_ref[0]
        for k in (2, 4, 6):
            acc = acc + g4_ref[k]
        cc = cc_ref[...]
        sg = _sigmoid(cc)
        row = jnp.sum(acc, axis=0, keepdims=True) * (sg * (1.0 + cc * (1.0 - sg)))
        o_ref[32:40, :] = jnp.broadcast_to(row, (8, D))
        o_ref[40:64, :] = s_ref[64:88, :]
        o_ref[64:72, :] = jnp.zeros((8, D), F32)
        for d in range(2):
            pp = lbp_ref[2 * d:2 * d + 1, :] * lbp_ref[2 * d + 1:2 * d + 2, :] * s_ref[75 + d:76 + d, :]
            o_ref[64 + 2 * d:65 + 2 * d, :] = -pp
            o_ref[65 + 2 * d:66 + 2 * d, :] = pp

    return pl.pallas_call(
        body, name=name, in_specs=[VMEM_SPEC] * 4, out_specs=VMEM_SPEC,
        out_shape=jax.ShapeDtypeStruct((FIN_ROWS, D), F32),
        scratch_shapes=[pltpu.VMEM((SMALL_ROWS, D), F32)],
    )(g3, g4, c_ctx, lbp)


def _pack_rows(arrs):
    flat = jnp.concatenate([a.reshape(-1) for a in arrs])
    n = -(-flat.shape[0] // (8 * D)) * 8 * D
    return jnp.pad(flat, (0, n - flat.shape[0])).reshape(n // D, D)


def _unpack_rows(packed, shapes):
    flat = packed.reshape(-1)
    outs, off = [], 0
    for s in shapes:
        size = 1
        for k in s:
            size *= k
        outs.append(flat[off:off + size].reshape(s))
        off += size
    return outs


def _pad8(a):
    return jnp.pad(a, ((0, 8 - a.shape[0]), (0, 0)))


def kernel(x, c, ctx, c_ctx, ada_w, ada_b, norm1, norm2, norm_f, mlp_w1, mlp_w2, hgrn_w_in, hgrn_lb, hgrn_gnorm, hgrn_w_out, conv_w_in, conv_w, conv_b, conv_w_out, loss_target, m_c_ctx, m_ada_w, m_ada_b, m_norm1, m_norm2, m_norm_f, m_mlp_w1, m_mlp_w2, m_hgrn_w_in, m_hgrn_lb, m_hgrn_gnorm, m_hgrn_w_out, m_conv_w_in, m_conv_w, m_conv_b, m_conv_w_out, v_c_ctx, v_ada_w, v_ada_b, v_norm1, v_norm2, v_norm_f, v_mlp_w1, v_mlp_w2, v_hgrn_w_in, v_hgrn_lb, v_hgrn_gnorm, v_hgrn_w_out, v_conv_w_in, v_conv_w, v_conv_b, v_conv_w_out):
    xi, yi, ci = _place()
    me = 4 * xi + 2 * yi + ci
    chip = 2 * xi + yi
    seq = x.shape[1]
    assert ctx.shape[1] == TM and seq % TM == 0 and (seq + TM) % TMW == 0
    n_lat = seq // TM
    sd = D // 4
    nca = ada_w.shape[2]
    xs = jnp.concatenate([x[0], ctx[0]], axis=0)

    big = [(mlp_w1, m_mlp_w1, v_mlp_w1), (mlp_w2, m_mlp_w2, v_mlp_w2), (hgrn_w_in, m_hgrn_w_in, v_hgrn_w_in),
           (hgrn_w_out, m_hgrn_w_out, v_hgrn_w_out), (conv_w_in, m_conv_w_in, v_conv_w_in), (conv_w_out, m_conv_w_out, v_conv_w_out)]
    big_names = ["w1", "w2", "hin", "hout", "cin", "cout"]
    flat2 = lambda a: a.reshape(a.shape[0] * a.shape[1], a.shape[2])
    tensors = dict(zip(big_names, big))
    chip1 = jnp.reshape(chip, (1,)).astype(jnp.int32)
    order = []
    for i in range(DEPTH):
        order += [("hin", i // 2), ("hout", i // 2)] if i % 2 == 0 else [("cin", i // 2), ("cout", i // 2)]
        order += [("w1", i), ("w2", i)]
    lands = [cast_to_slot(flat2(tensors[n][0]), idx, tensors[n][0].shape[1], chip1, f"cast_{n}_{idx}") for n, idx in order]
    sh_rows = jnp.concatenate([hgrn_lb.reshape(4, sd), conv_w.reshape(6, sd), conv_b.reshape(2, sd)], axis=0)
    buf1 = jnp.concatenate([c, jnp.pad(sh_rows, ((0, 0), (0, D - sd))), jnp.zeros((3, D), F32)], axis=0)
    g1 = small_allgather(buf1, [], "gather_small_in")
    first_sems, first_lands, _, first_token = split_start(lands[:1], None, g1, "gather_start_first")
    cvec = jnp.concatenate([g1[:, 0, :], jnp.broadcast_to(c_ctx[None], (8, D))], axis=0)
    shf = g1[0::2, 1:13, :sd].transpose(1, 0, 2).reshape(12, D)
    lb_p = jax.nn.softmax(shf[0:4].reshape(2, 2, D), axis=1)
    lower = jnp.cumsum(lb_p, axis=1) - lb_p[:, :1]
    lbs = [lower[:, 0], lower[:, 1]]
    cw8 = [_pad8(shf[4:7]), _pad8(shf[7:10])]
    cb = [shf[10:11], shf[11:12]]

    bias = lax.dynamic_slice_in_dim(ada_b, chip * nca, nca, axis=1).reshape(DEPTH, 1, nca)
    ada_part = ada_fwd(cvec, ada_w, bias, "ada_fwd")
    g2 = small_allgather(ada_part.reshape(DEPTH * 16, nca), [first_token] + lands[1:], "gather_ada")
    ada_full = g2[0::2].reshape(4, DEPTH, 16, nca).transpose(1, 2, 0, 3).reshape(DEPTH, 16, 4 * nca)
    lat = lax.dynamic_slice_in_dim(ada_full, me, 1, axis=1)[:, 0]
    mods = [jnp.stack([_pad8(lat[i].reshape(6, D)), _pad8(ada_full[i, 8].reshape(6, D))]) for i in range(DEPTH)]

    rest_sems, rest_lands, _, rest_token = split_start(lands[1:], None, g2, "gather_start")
    w_sems = first_sems + rest_sems
    lands = first_lands + rest_lands
    unit = {key: u for u, key in enumerate(order)}

    def wts(n, idx, after):
        u = unit[(n, idx)]
        if u == 0:
            after = rest_token
        (w,), _ = split_wait([lands[u]], None, w_sems[2 * u:2 * u + 2], after, f"gather_wait_{n}_{idx}")
        return w.reshape(w.shape[0] * w.shape[1], w.shape[2]) if n in ("w2", "hout", "cout") else w

    started = []

    def on_grads(i, tag, g):
        names = sorted(g)
        gs = [g[n].reshape(4, g[n].shape[0] * g[n].shape[1] // 4, g[n].shape[2]) for n in names]
        sems, zones, srcs, token = split_start([lax.empty(a.shape, BF16) for a in gs], gs, chip1, f"grad_start_{tag}_{i}")
        started.append(([(n, i if n in ("w1", "w2") else i // 2) for n in names], sems, zones, srcs))
        return token

    done = {}
    acc = {n: lax.empty(flat2(w).shape, F32) for n, (w, _, _) in tensors.items()}
    early_names = ["w1", "w2", "cin", "cout"]
    late_names = ["hin", "hout"]

    def finish_units(group, after, name):
        units = [(key, sems[2 * u:2 * u + 2], zones[u], srcs[u]) for ks, sems, zones, srcs in group for u, key in enumerate(ks)]
        zones, srcs = split_wait([u[2] for u in units], [u[3] for u in units], [s for u in units for s in u[1]], after, name)
        for (key, _, _, _), zone, own in zip(units, zones, srcs):
            acc[key[0]] = sum_slots(own, zone, acc[key[0]], key[1], chip1, f"sum_{key[0]}_{key[1]}")

    def after_backward(small, token):
        rows3 = jnp.concatenate(small["dmod"] + small["norm1"] + small["norm2"] + [small["norm_f"]] + small["gnorm"]
                                + [small["lb"][1]] + small["cw"] + small["cb"] + [jnp.tile(token[0:3], (1, D // 128))], axis=0)
        g3 = small_allgather(rows3, [], "gather_small_out")
        dmat = g3[:, :64].reshape(8, DEPTH, 2, 8, D)[:, :, :, :6].transpose(1, 2, 0, 3, 4).reshape(DEPTH, 16, 6 * D)
        dcols = lax.dynamic_slice_in_dim(dmat, chip * nca, nca, axis=2)
        *done["ada"], acc4 = ada_bwd(cvec, dcols, ada_w, m_ada_w, v_ada_w, "ada_bwd")
        g4 = small_allgather(acc4, [], "gather_cctx")
        done["fin"] = small_finish(g3, g4, c_ctx[None], _pad8(lb_p.reshape(4, D)), "small_finish")
        finish_units(list(started), done["fin"], "grad_wait_early")
        done["sib_early"] = sibling_start([acc[n] for n in early_names], "sibling_start_early")
        return done["sib_early"][0][-1]

    lane, dx, last_token = local_step(xs, loss_target[0], mods, norm1, norm2, norm_f[None], lbs, hgrn_gnorm, cw8, cb, wts, n_lat,
                                      on_grads, after_backward)
    loss = lax.psum(0.5 * jnp.sum(lane) / D, ("x", "y", "c"))
    grad_x = dx[None]
    g_ada_w, d_ada_w, nm_ada_w, nv_ada_w = done["ada"]
    fin = done["fin"]
    cols = lambda a: lax.dynamic_slice_in_dim(a, chip * sd, sd, axis=a.ndim - 1)
    small_g = [fin[32], fin[0:32].reshape(DEPTH, 8, D)[:, :6].reshape(DEPTH, 6 * D), fin[40:44], fin[44:48], fin[48], fin[49:51],
               cols(fin[64:68].reshape(2, 2, D)), cols(fin[53:59].reshape(2, 3, D)), cols(fin[59:61])]
    small_w = [c_ctx, ada_b, norm1, norm2, norm_f, hgrn_gnorm, hgrn_lb, conv_w, conv_b]
    small_m = [m_c_ctx, m_ada_b, m_norm1, m_norm2, m_norm_f, m_hgrn_gnorm, m_hgrn_lb, m_conv_w, m_conv_b]
    small_v = [v_c_ctx, v_ada_b, v_norm1, v_norm2, v_norm_f, v_hgrn_gnorm, v_hgrn_lb, v_conv_w, v_conv_b]
    shapes = [w.shape for w in small_w]
    packed = adamw([_pack_rows(small_g)], _pack_rows(small_w), _pack_rows(small_m), _pack_rows(small_v), "adamw_small")
    s_g, s_d, s_m, s_v = [_unpack_rows(p, shapes) for p in packed]

    results = {}

    def finish_tensors(names, sib, after, name):
        mine, other = sibling_wait(*sib, after, name)
        for n, pm, po in zip(names, mine, other):
            w, m, v = tensors[n]
            results[n] = [a.reshape(w.shape) for a in adamw([pm, po], flat2(w), flat2(m), flat2(v), f"adamw_{n}")]

    finish_tensors(early_names, done["sib_early"], last_token, "sibling_wait_early")
    finish_units(started[-1:], results["cout"][0], "grad_wait_late")
    sib_late = sibling_start([acc[n] for n in late_names], "sibling_start_late")
    finish_tensors(late_names, sib_late, results["cin"][0], "sibling_wait_late")
    b_g, b_d, b_m, b_v = [[results[n][k] for n in big_names] for k in range(4)]

    def ordered(s, a, b):
        return [s[0], a, s[1], s[2], s[3], s[4], b[0], b[1], b[2], s[6], s[5], b[3], b[4], s[7], s[8], b[5]]

    return (loss, grad_x, *ordered(s_g, g_ada_w, b_g), *ordered(s_d, d_ada_w, b_d), *ordered(s_m, nm_ada_w, b_m),
            *ordered(s_v, nv_ada_w, b_v))
```

```python
import functools

import jax
import jax.numpy as jnp
from jax import lax
from jax.experimental import pallas as pl
from jax.experimental.pallas import tpu as pltpu

F32 = jnp.float32
BF16 = jnp.bfloat16
MESH = pl.DeviceIdType.MESH

D = 1024
HD = 128
NH = D // HD
CH = 64
TM = 256
TMW = 2816
EPS = 1e-6
DEPTH = 4
V7X_VMEM_BYTES = 64 * 1024 * 1024
VMEM_LIMIT = V7X_VMEM_BYTES - 8 * 1024 * 1024

ADAM_LR = 0.001
ADAM_B1 = 0.9
ADAM_B2 = 0.999
ADAM_EPS = 1e-08
ADAM_WD = 0.01
ADAM_STEP = 10


def _cp(n_grid):
    return pltpu.CompilerParams(dimension_semantics=("arbitrary",) * n_grid, vmem_limit_bytes=VMEM_LIMIT)


def _dot(a, b):
    return jnp.dot(a, b, preferred_element_type=F32)


def _dot_nt(a, b):
    return lax.dot_general(a, b, (((1,), (1,)), ((), ())), preferred_element_type=F32)


def _dot_tn(a, b):
    return lax.dot_general(a, b, (((0,), (0,)), ((), ())), preferred_element_type=F32)


def _sigmoid(z):
    return 1.0 / (1.0 + jnp.exp(-z))


def _norm_mod(x, gain, sh, sc):
    r = lax.rsqrt(jnp.mean(x * x, axis=-1, keepdims=True) + EPS)
    xn = x * r
    yn = xn * gain
    return r, xn, yn, yn * (1.0 + sc) + sh


def _row_spec(width):
    return pl.BlockSpec((TM, width), lambda i: (i, 0))


def _col_spec(col):
    return pl.BlockSpec((TM, D), lambda i: (i, col))


def _full_spec(shape):
    nd = len(shape)
    return pl.BlockSpec(shape, lambda i: (0,) * nd)


def _mod_spec(n_lat):
    return pl.BlockSpec((1, 8, D), lambda i: (i // n_lat, 0, 0))


def _f32(ref):
    return ref[...].astype(F32)


def proj_fwd(x, gain, mod, m0, w4, n_lat, dtype, name):
    t = x.shape[0]
    nb, _, ns = w4.shape

    def body(x_ref, gain_ref, mod_ref, w_ref, p_ref):
        _, _, _, h = _norm_mod(x_ref[...], gain_ref[...], mod_ref[0, m0:m0 + 1, :], mod_ref[0, m0 + 1:m0 + 2, :])
        hb = h.astype(BF16)
        for c in range(nb):
            p_ref[:, c * ns:(c + 1) * ns] = _dot(hb, w_ref[c]).astype(dtype)

    return pl.pallas_call(
        body, name=name, grid=(t // TM,),
        in_specs=[_row_spec(D), _full_spec((1, D)), _mod_spec(n_lat), _full_spec(w4.shape)],
        out_specs=_row_spec(nb * ns),
        out_shape=jax.ShapeDtypeStruct((t, nb * ns), dtype),
        compiler_params=_cp(1),
    )(x, gain, mod, w4)


def proj_bwd(parts, w4, x, gain, mod, m0, dx_in, n_lat, lat_only, name):
    t = x.shape[0]
    nb, _, ns = w4.shape
    n = nb * ns
    n_parts = len(parts)
    widths = [p.shape[1] for p in parts]
    offs = [sum(widths[:k]) for k in range(n_parts)]
    assert sum(widths) == n
    single = n_parts == 1

    def body(*refs):
        part_refs = refs[:n_parts]
        w_ref, x_ref, gain_ref, mod_ref, dxin_ref = refs[n_parts:n_parts + 5]
        rest = refs[n_parts + 5:]
        if single:
            dx_ref, hb_ref, acc_ref = rest
            src = part_refs[0]
        else:
            dx_ref, hb_ref, acc_ref, dpb_ref = rest
            for p_ref, off, w in zip(part_refs, offs, widths):
                dpb_ref[:, off:off + w] = p_ref[...]
            src = dpb_ref
        i = pl.program_id(0)

        @pl.when(i == 0)
        def _():
            acc_ref[...] = jnp.zeros_like(acc_ref)

        gain = gain_ref[...]
        sc = mod_ref[0, m0 + 1:m0 + 2, :]
        r, xn, yn, h = _norm_mod(x_ref[...], gain, mod_ref[0, m0:m0 + 1, :], sc)
        hb_ref[...] = h.astype(BF16)
        dh = _dot_nt(src[:, 0:ns], w_ref[0])
        for c in range(1, nb):
            dh = dh + _dot_nt(src[:, c * ns:(c + 1) * ns], w_ref[c])
        dsh = jnp.sum(dh, axis=0, keepdims=True)
        dsc = jnp.sum(dh * yn, axis=0, keepdims=True)
        dyn = dh * (1.0 + sc)
        dgain = jnp.sum(dyn * xn, axis=0, keepdims=True)
        dxn = dyn * gain
        dx = dxin_ref[...] + r * (dxn - xn * jnp.mean(dxn * xn, axis=-1, keepdims=True))
        if lat_only:
            @pl.when(i < n_lat)
            def _():
                dx_ref[...] = dx
        else:
            dx_ref[...] = dx
        latf = (i < n_lat).astype(F32)
        ctxf = 1.0 - latf
        acc_ref[0:1, :] += dgain
        acc_ref[1:2, :] += dsh * latf
        acc_ref[2:3, :] += dsc * latf
        acc_ref[3:4, :] += dsh * ctxf
        acc_ref[4:5, :] += dsc * ctxf

    dx_rows = n_lat * TM if lat_only else t
    dx_spec = pl.BlockSpec((TM, D), lambda i: (jnp.minimum(i, n_lat - 1), 0)) if lat_only else _row_spec(D)
    out_specs = [dx_spec, _row_spec(D), _full_spec((8, D))]
    out_shape = [jax.ShapeDtypeStruct((dx_rows, D), F32), jax.ShapeDtypeStruct((t, D), BF16), jax.ShapeDtypeStruct((8, D), F32)]
    if not single:
        out_specs.append(_row_spec(n))
        out_shape.append(jax.ShapeDtypeStruct((t, n), BF16))
    outs = pl.pallas_call(
        body, name=name, grid=(t // TM,),
        in_specs=[_row_spec(w) for w in widths]
        + [_full_spec(w4.shape), _row_spec(D), _full_spec((1, D)), _mod_spec(n_lat), _row_spec(D)],
        out_specs=out_specs, out_shape=out_shape, compiler_params=_cp(1),
    )(*parts, w4, x, gain, mod, dx_in)
    if single:
        return outs[0], outs[1], parts[0], outs[2]
    return outs[0], outs[1], outs[3], outs[2]


def dw_tn(a, b, nb, a_blocked, square_a, dep, name):
    t = a.shape[0]
    ka = a.shape[1] // nb if a_blocked else a.shape[1]
    kb = b.shape[1] if a_blocked else b.shape[1] // nb
    n_k = t // TMW

    def body(a_ref, b_ref, _, o_ref, acc):
        k = pl.program_id(1)

        @pl.when(k == 0)
        def _():
            acc[...] = jnp.zeros_like(acc)

        a = a_ref[...]
        acc[...] += _dot_tn(a * a if square_a else a, b_ref[...])

        @pl.when(k == n_k - 1)
        def _():
            o_ref[0] = acc[...].astype(BF16)

    a_spec = pl.BlockSpec((TMW, ka), (lambda j, k: (k, j)) if a_blocked else (lambda j, k: (k, 0)))
    b_spec = pl.BlockSpec((TMW, kb), (lambda j, k: (k, 0)) if a_blocked else (lambda j, k: (k, j)))
    return pl.pallas_call(
        body, name=name, grid=(nb, n_k),
        in_specs=[a_spec, b_spec, ANY],
        out_specs=pl.BlockSpec((1, ka, kb), lambda j, k: (j, 0, 0)),
        out_shape=jax.ShapeDtypeStruct((nb, ka, kb), BF16),
        scratch_shapes=[pltpu.VMEM((ka, kb), F32)],
        compiler_params=_cp(2),
    )(a, b, dep)


def outproj_fwd(prologue, extras, extra_specs, w, x, mod, m0, n_lat, name):
    t = x.shape[0]
    k = w.shape[0]
    n_extra = len(extras)

    def body(*refs):
        ex = refs[:n_extra]
        w_ref, x_ref, mod_ref, xo_ref, y_ref, ab_ref = refs[n_extra:]
        ab = prologue(pl.program_id(0), *ex).astype(BF16)
        ab_ref[...] = ab
        y = _dot(ab, w_ref[...])
        y_ref[...] = y.astype(BF16)
        xo_ref[...] = x_ref[...] + mod_ref[0, m0 + 2:m0 + 3, :] * y

    return pl.pallas_call(
        body, name=name, grid=(t // TM,),
        in_specs=list(extra_specs) + [_full_spec(w.shape), _row_spec(D), _mod_spec(n_lat)],
        out_specs=[_row_spec(D), _row_spec(D), _row_spec(k)],
        out_shape=[jax.ShapeDtypeStruct((t, D), F32), jax.ShapeDtypeStruct((t, D), BF16), jax.ShapeDtypeStruct((t, k), BF16)],
        compiler_params=_cp(1),
    )(*extras, w, x, mod)


def outproj_bwd(epilogue, extras, extra_specs, ep_out_specs, ep_out_shapes, w, dxn, y, mod, m0, n_lat, dep, name):
    t = dxn.shape[0]
    n_extra = len(extras)

    def body(*refs):
        ex = refs[:n_extra]
        w_ref, dxn_ref, y_ref, mod_ref, _, dyb_ref, acc_ref = refs[n_extra:n_extra + 7]
        ep_outs = refs[n_extra + 7:]
        i = pl.program_id(0)

        @pl.when(i == 0)
        def _():
            acc_ref[...] = jnp.zeros_like(acc_ref)

        dxv = dxn_ref[...]
        dyb = (dxv * mod_ref[0, m0 + 2:m0 + 3, :]).astype(BF16)
        dyb_ref[...] = dyb
        dg = jnp.sum(dxv * _f32(y_ref), axis=0, keepdims=True)
        latf = (i < n_lat).astype(F32)
        acc_ref[0:1, :] += dg * latf
        acc_ref[1:2, :] += dg * (1.0 - latf)
        epilogue(i, _dot_nt(dyb, w_ref[...]), ex, ep_outs, acc_ref)

    outs = pl.pallas_call(
        body, name=name, grid=(t // TM,),
        in_specs=list(extra_specs) + [_full_spec(w.shape), _row_spec(D), _row_spec(D), _mod_spec(n_lat), ANY],
        out_specs=[_row_spec(D), _full_spec((8, D))] + list(ep_out_specs),
        out_shape=[jax.ShapeDtypeStruct((t, D), BF16), jax.ShapeDtypeStruct((8, D), F32)] + list(ep_out_shapes),
        compiler_params=_cp(1),
    )(*extras, w, dxn, y, mod, dep)
    return outs[0], outs[1], outs[2:]


def mlp_fwd(x, gain, mod, w1, w2, n_lat, name):
    t = x.shape[0]
    nb, _, ns = w1.shape

    def body(x_ref, gain_ref, mod_ref, w1_ref, w2_ref, xo_ref, y_ref, rb_ref):
        x = x_ref[...]
        _, _, _, h = _norm_mod(x, gain_ref[...], mod_ref[0, 3:4, :], mod_ref[0, 4:5, :])
        hb = h.astype(BF16)
        y = None
        for c in range(nb):
            r = jnp.maximum(_dot(hb, w1_ref[c]), 0.0)
            rb_ref[:, c * ns:(c + 1) * ns] = r.astype(BF16)
            yc = _dot((r * r).astype(BF16), w2_ref[c * ns:(c + 1) * ns, :])
            y = yc if y is None else y + yc
        y_ref[...] = y.astype(BF16)
        xo_ref[...] = x + mod_ref[0, 5:6, :] * y

    return pl.pallas_call(
        body, name=name, grid=(t // TM,),
        in_specs=[_row_spec(D), _full_spec((1, D)), _mod_spec(n_lat), _full_spec(w1.shape), _full_spec(w2.shape)],
        out_specs=[_row_spec(D), _row_spec(D), _row_spec(nb * ns)],
        out_shape=[jax.ShapeDtypeStruct((t, D), F32), jax.ShapeDtypeStruct((t, D), BF16), jax.ShapeDtypeStruct((t, nb * ns), BF16)],
        compiler_params=_cp(1),
    )(x, gain, mod, w1, w2)


def mlp_bwd(dxn, y, ab, x, gain, mod, w1, w2, n_lat, dep, name):
    t = x.shape[0]
    nb, _, ns = w1.shape

    def body(dxn_ref, y_ref, rb_ref, x_ref, gain_ref, mod_ref, w1_ref, w2_ref, _, dx_ref, dyb_ref, dp_ref, hb_ref, acc_ref):
        i = pl.program_id(0)

        @pl.when(i == 0)
        def _():
            acc_ref[...] = jnp.zeros_like(acc_ref)

        dxv = dxn_ref[...]
        dyb = (dxv * mod_ref[0, 5:6, :]).astype(BF16)
        dyb_ref[...] = dyb
        dg = jnp.sum(dxv * _f32(y_ref), axis=0, keepdims=True)
        gain = gain_ref[...]
        sc = mod_ref[0, 4:5, :]
        r, xn, yn, h = _norm_mod(x_ref[...], gain, mod_ref[0, 3:4, :], sc)
        hb_ref[...] = h.astype(BF16)
        dps = []
        for c in range(nb):
            cols = slice(c * ns, (c + 1) * ns)
            dp = (_dot_nt(dyb, w2_ref[cols, :]) * (2.0 * rb_ref[:, cols].astype(F32))).astype(BF16)
            dp_ref[:, cols] = dp
            dps.append(dp)
        dh = _dot_nt(dps[0], w1_ref[0])
        for c in range(1, nb):
            dh = dh + _dot_nt(dps[c], w1_ref[c])
        dsh = jnp.sum(dh, axis=0, keepdims=True)
        dsc = jnp.sum(dh * yn, axis=0, keepdims=True)
        dyn = dh * (1.0 + sc)
        dgain = jnp.sum(dyn * xn, axis=0, keepdims=True)
        dxn_ = dyn * gain
        dx_ref[...] = dxv + r * (dxn_ - xn * jnp.mean(dxn_ * xn, axis=-1, keepdims=True))
        latf = (i < n_lat).astype(F32)
        ctxf = 1.0 - latf
        acc_ref[0:1, :] += dgain
        acc_ref[1:2, :] += dsh * latf
        acc_ref[2:3, :] += dsc * latf
        acc_ref[3:4, :] += dsh * ctxf
        acc_ref[4:5, :] += dsc * ctxf
        acc_ref[5:6, :] += dg * latf
        acc_ref[6:7, :] += dg * ctxf

    return pl.pallas_call(
        body, name=name, grid=(t // TM,),
        in_specs=[_row_spec(D), _row_spec(D), _row_spec(nb * ns), _row_spec(D), _full_spec((1, D)), _mod_spec(n_lat),
                  _full_spec(w1.shape), _full_spec(w2.shape), ANY],
        out_specs=[_row_spec(D), _row_spec(D), _row_spec(nb * ns), _row_spec(D), _full_spec((8, D))],
        out_shape=[jax.ShapeDtypeStruct((t, D), F32), jax.ShapeDtypeStruct((t, D), BF16), jax.ShapeDtypeStruct((t, nb * ns), BF16),
                   jax.ShapeDtypeStruct((t, D), BF16), jax.ShapeDtypeStruct((8, D), F32)],
        compiler_params=_cp(1),
    )(dxn, y, ab, x, gain, mod, w1, w2, dep)


def readout_prologue(i, o0_ref, o1_ref, gate_ref, gn_ref):
    o = _f32(o0_ref) + _f32(o1_ref)
    gate = gate_ref[...]
    w = gn_ref[...] * (gate * _sigmoid(gate))
    pieces = []
    for h in range(NH):
        sl = slice(h * HD, (h + 1) * HD)
        oh = o[:, sl]
        pieces.append(oh * lax.rsqrt(jnp.mean(oh * oh, axis=-1, keepdims=True) + EPS) * w[:, sl])
    return jnp.concatenate(pieces, axis=1)


def readout_epilogue(i, da, ex, outs, acc_ref):
    o0_ref, o1_ref, gate_ref, gn_ref = ex
    do_ref, dgate_ref = outs
    o = _f32(o0_ref) + _f32(o1_ref)
    gate = gate_ref[...]
    gn = gn_ref[...]
    sg = _sigmoid(gate)
    silu = gate * sg
    dsilu = sg * (1.0 + gate * (1.0 - sg))
    for h in range(NH):
        sl = slice(h * HD, (h + 1) * HD)
        oh = o[:, sl]
        r = lax.rsqrt(jnp.mean(oh * oh, axis=-1, keepdims=True) + EPS)
        nh = oh * r
        dah = da[:, sl]
        acc_ref[2:3, sl] += jnp.sum(dah * nh * silu[:, sl], axis=0, keepdims=True)
        dgate_ref[:, sl] = (dah * nh * gn[:, sl] * dsilu[:, sl]).astype(BF16)
        dn = dah * gn[:, sl] * silu[:, sl]
        do_ref[:, sl] = r * (dn - nh * jnp.mean(dn * nh, axis=-1, keepdims=True))


def _seg_masks(i, n_lat):
    rows = lax.broadcasted_iota(jnp.int32, (TM, 1), 0)
    latf = (i < n_lat).astype(F32)
    ctxf = 1.0 - latf
    prev_ok = (rows % CH != 0).astype(F32) * latf + (rows != 0).astype(F32) * ctxf
    next_ok = (rows % CH != CH - 1).astype(F32) * latf + (rows != TM - 1).astype(F32) * ctxf
    return prev_ok, next_ok


def _shifts(i, n_lat, sft, cur, halo_prev, halo_next):
    if sft == 1:
        prev_ok, next_ok = _seg_masks(i, n_lat)
        return pltpu.roll(cur, 1, 0) * prev_ok, pltpu.roll(cur, TM - 1, 0) * next_ok
    has_prev = jnp.logical_and(i > 0, i < n_lat).astype(F32)
    has_next = (i < n_lat - 1).astype(F32)
    prev = jnp.concatenate([halo_prev * has_prev, cur[:TM - CH]], axis=0)
    nxt = jnp.concatenate([cur[CH:], halo_next * has_next], axis=0)
    return prev, nxt


def _conv_u(sft, ex):
    if sft == 1:
        gb_ref, gc_ref, xi_ref, cw_ref, cb_ref = ex
        return gb_ref, _f32(gc_ref) * _f32(xi_ref), None, None, cw_ref, cb_ref
    gb_ref, gc_ref, xi_ref, gcp_ref, xip_ref, gcn_ref, xin_ref, cw_ref, cb_ref = ex
    return gb_ref, _f32(gc_ref) * _f32(xi_ref), _f32(gcp_ref) * _f32(xip_ref), _f32(gcn_ref) * _f32(xin_ref), cw_ref, cb_ref


def _conv_value(i, n_lat, sft, ex):
    gb_ref, u, up, un, cw_ref, cb_ref = _conv_u(sft, ex)
    u_prev, u_next = _shifts(i, n_lat, sft, u, up, un)
    return gb_ref, cb_ref[...] + cw_ref[0:1, :] * u_prev + cw_ref[1:2, :] * u + cw_ref[2:3, :] * u_next


def make_conv_prologue(n_lat, sft):
    def prologue(i, *ex):
        gb_ref, conv = _conv_value(i, n_lat, sft, ex)
        return _f32(gb_ref) * conv
    return prologue


def make_conv_epilogue(n_lat, sft):
    def epilogue(i, da, ex, outs, acc_ref):
        gb_ref, conv = _conv_value(i, n_lat, sft, ex)
        outs[0][...] = (da * _f32(gb_ref)).astype(BF16)
        outs[1][...] = (da * conv).astype(BF16)
    return epilogue


def _conv_specs(sft, t):
    specs = [_col_spec(0), _col_spec(1), _col_spec(2)]
    if sft != 1:
        per = TM // CH
        last = t // CH - 1
        for fn in (lambda i: jnp.maximum(i * per - 1, 0), lambda i: jnp.minimum(i * per + per, last)):
            for col in (1, 2):
                specs.append(pl.BlockSpec((CH, D), functools.partial(lambda i, f, c: (f(i), c), f=fn, c=col)))
    return specs + [_full_spec((8, D)), _full_spec((1, D))]


def _conv_args(sft, p, cw8, cb):
    return [p] * (3 if sft == 1 else 7) + [cw8, cb]


def conv_bwd(dconv, p, cw8, sft, n_lat, name):
    t = dconv.shape[0]
    halo = sft != 1

    def body(*refs):
        if halo:
            dc_ref, dcp_ref, dcn_ref, gc_ref, xi_ref, gcp_ref, xip_ref, gcn_ref, xin_ref, cw_ref, dgc_ref, dxi_ref, acc_ref = refs
            up, un = _f32(gcp_ref) * _f32(xip_ref), _f32(gcn_ref) * _f32(xin_ref)
            dcp, dcn = _f32(dcp_ref), _f32(dcn_ref)
        else:
            dc_ref, gc_ref, xi_ref, cw_ref, dgc_ref, dxi_ref, acc_ref = refs
            up = un = dcp = dcn = None
        i = pl.program_id(0)

        @pl.when(i == 0)
        def _():
            acc_ref[...] = jnp.zeros_like(acc_ref)

        gc = _f32(gc_ref)
        xi = _f32(xi_ref)
        u = gc * xi
        dc = _f32(dc_ref)
        u_prev, u_next = _shifts(i, n_lat, sft, u, up, un)
        dc_prev, dc_next = _shifts(i, n_lat, sft, dc, dcp, dcn)
        acc_ref[0:1, :] += jnp.sum(dc * u_prev, axis=0, keepdims=True)
        acc_ref[1:2, :] += jnp.sum(dc * u, axis=0, keepdims=True)
        acc_ref[2:3, :] += jnp.sum(dc * u_next, axis=0, keepdims=True)
        acc_ref[3:4, :] += jnp.sum(dc, axis=0, keepdims=True)
        du = cw_ref[0:1, :] * dc_next + cw_ref[1:2, :] * dc + cw_ref[2:3, :] * dc_prev
        dgc_ref[...] = (du * xi).astype(BF16)
        dxi_ref[...] = (du * gc).astype(BF16)

    per = TM // CH
    last = t // CH - 1
    prev_i = lambda i: jnp.maximum(i * per - 1, 0)
    next_i = lambda i: jnp.minimum(i * per + per, last)
    if halo:
        in_specs = [_row_spec(D), pl.BlockSpec((CH, D), lambda i: (prev_i(i), 0)), pl.BlockSpec((CH, D), lambda i: (next_i(i), 0)),
                    _col_spec(1), _col_spec(2),
                    pl.BlockSpec((CH, D), lambda i: (prev_i(i), 1)), pl.BlockSpec((CH, D), lambda i: (prev_i(i), 2)),
                    pl.BlockSpec((CH, D), lambda i: (next_i(i), 1)), pl.BlockSpec((CH, D), lambda i: (next_i(i), 2)),
                    _full_spec((8, D))]
        args = [dconv, dconv, dconv, p, p, p, p, p, p, cw8]
    else:
        in_specs = [_row_spec(D), _col_spec(1), _col_spec(2), _full_spec((8, D))]
        args = [dconv, p, p, cw8]
    return pl.pallas_call(
        body, name=name, grid=(t // TM,), in_specs=in_specs,
        out_specs=[_row_spec(D), _row_spec(D), _full_spec((8, D))],
        out_shape=[jax.ShapeDtypeStruct((t, D), BF16), jax.ShapeDtypeStruct((t, D), BF16), jax.ShapeDtypeStruct((8, D), F32)],
        compiler_params=_cp(1),
    )(*args)


LOG2E = 1.4426950408889634


def _cumsum_matrix(reverse):
    r = lax.broadcasted_iota(jnp.int32, (CH, CH), 0)
    c = lax.broadcasted_iota(jnp.int32, (CH, CH), 1)
    return (r <= c if reverse else r >= c).astype(BF16)


def _chunk_cumsum(g, tri):
    hi = g.astype(BF16)
    lo = (g - hi.astype(F32)).astype(BF16)
    return _dot(tri, hi) + _dot(tri, lo)


def _gate_values(z, lb):
    sig = _sigmoid(z)
    f = lb + (1.0 - lb) * sig
    return sig, f


def _tri(direction, transposed):
    r = lax.broadcasted_iota(jnp.int32, (CH, CH), 0)
    c = lax.broadcasted_iota(jnp.int32, (CH, CH), 1)
    lower = (direction == 0) != transposed
    return r >= c if lower else r <= c


def _gla_rows(direction):
    return (CH // 2 - 1, CH - 1) if direction == 0 else (CH // 2, 0)


def _fwd_tile(direction, nt):
    return (lambda i: (i + nt - 1) % nt) if direction == 0 else (lambda i: nt - 1 - i)


def gla_fwd(p, lb2, name):
    t = p.shape[0]
    nt = t // TM
    per = TM // CH

    def body(z0_ref, v0_ref, q0_ref, z1_ref, v1_ref, q1_ref, lb_ref, o0_ref, s0_ref, o1_ref, s1_ref, st, q_s, k_s, c_s):
        @pl.when(pl.program_id(0) == 0)
        def _():
            st[...] = jnp.zeros_like(st)

        ins = ((z0_ref, v0_ref, q0_ref, o0_ref, s0_ref), (z1_ref, v1_ref, q1_ref, o1_ref, s1_ref))
        for d in range(2):
            z_ref, _, qr_ref, _, _ = ins[d]
            tri = _cumsum_matrix(d == 1)
            lb = lb_ref[d:d + 1, :]
            for ci in range(per):
                rows = slice(ci * CH, (ci + 1) * CH)
                _, f = _gate_values(z_ref[rows, :], lb)
                k_s[d, rows, :] = 1.0 - f
                c_s[d, rows, :] = _chunk_cumsum(jnp.log(f) * LOG2E, tri)
                qr = qr_ref[rows, :]
                q_s[d, rows, :] = qr * _sigmoid(qr)
        masks = (_tri(0, False), _tri(1, False))
        state = [[st[d, h] for h in range(NH)] for d in range(2)]
        for it in range(per):
            chunk = []
            for d in range(2):
                ref_row, last_row = _gla_rows(d)
                ci = it if d == 0 else per - 1 - it
                r0 = ci * CH
                rows = slice(r0, r0 + CH)
                cum = c_s[d, rows, :]
                ref = c_s[d, r0 + ref_row:r0 + ref_row + 1, :]
                last = c_s[d, r0 + last_row:r0 + last_row + 1, :]
                q = q_s[d, rows, :]
                k = k_s[d, rows, :]
                chunk.append(dict(
                    ci=ci, rows=rows, qh=(q * jnp.exp2(cum)).astype(BF16), qt=(q * jnp.exp2(cum - ref)).astype(BF16),
                    kt=(k * jnp.exp2(ref - cum)).astype(BF16), kb=(k * jnp.exp2(last - cum)).astype(BF16),
                    el=jnp.exp2(last), vb=ins[d][1][rows, :].astype(BF16)))
            heads = [slice(h * HD, (h + 1) * HD) for h in range(NH)]
            first = {}
            for h, sl in enumerate(heads):
                for d in range(2):
                    c = chunk[d]
                    s_t = state[d][h]
                    ins[d][4][c["ci"], h] = s_t
                    first[d, h] = (jnp.where(masks[d], _dot_nt(c["qt"][:, sl], c["kt"][:, sl]), 0.0).astype(BF16),
                                   _dot_nt(c["qh"][:, sl], s_t.astype(BF16)),
                                   s_t * c["el"][:, sl] + _dot_tn(c["vb"][:, sl], c["kb"][:, sl]))
            for h, sl in enumerate(heads):
                for d in range(2):
                    c = chunk[d]
                    sc, o_inter, s_new = first[d, h]
                    ins[d][3][c["rows"], sl] = (o_inter + _dot(sc, c["vb"][:, sl])).astype(BF16)
                    state[d][h] = s_new
        for d in range(2):
            for h in range(NH):
                st[d, h] = state[d][h]

    tiles = (_fwd_tile(0, nt), _fwd_tile(1, nt))
    tspec = lambda d, col: pl.BlockSpec((TM, D), lambda i: (tiles[d](i), col))
    sspec = lambda d: pl.BlockSpec((per, NH, HD, HD), lambda i: (tiles[d](i), 0, 0, 0))
    o_shape = jax.ShapeDtypeStruct((t, D), BF16)
    s_shape = jax.ShapeDtypeStruct((t // CH, NH, HD, HD), F32)
    return pl.pallas_call(
        body, name=name, grid=(nt,),
        in_specs=[tspec(0, 0), tspec(0, 2), tspec(0, 3), tspec(1, 1), tspec(1, 2), tspec(1, 3), _full_spec((2, D))],
        out_specs=[tspec(0, 0), sspec(0), tspec(1, 0), sspec(1)],
        out_shape=[o_shape, s_shape, o_shape, s_shape],
        scratch_shapes=[pltpu.VMEM((2, NH, HD, HD), F32)] + [pltpu.VMEM((2, TM, D), F32)] * 3,
        compiler_params=_cp(1),
    )(p, p, p, p, p, p, lb2)


def gla_bwd(p, lb2, do, states, direction, prev, dep, name):
    t = p.shape[0]
    nt = t // TM
    per = TM // CH
    ref_row, last_row = _gla_rows(direction)
    tile = (lambda i: (2 * nt - 2 - i) % nt) if direction == 0 else (lambda i: i)
    final = prev is not None
    n_in = 9 if final else 7

    def body(*refs):
        z_ref, v_ref, qr_ref, lb_ref, do_ref, s_ref = refs[:6]
        dz_ref, dv_ref, dq_ref, acc_ref, dst, q_s, k_s, c_s, dq_s, dk_s, dl_s = refs[n_in:]

        @pl.when(pl.program_id(0) == 0)
        def _():
            dst[...] = jnp.zeros_like(dst)
            acc_ref[...] = jnp.zeros_like(acc_ref)

        mask = _tri(direction, False)
        mask_t = _tri(direction, True)
        tri = _cumsum_matrix(direction == 1)
        tri_t = _cumsum_matrix(direction == 0)
        is_last = lax.broadcasted_iota(jnp.int32, (CH, 1), 0) == last_row
        lb = lb_ref[direction:direction + 1, :]
        for ci in range(per):
            rows = slice(ci * CH, (ci + 1) * CH)
            _, f = _gate_values(z_ref[rows, :], lb)
            k_s[rows, :] = 1.0 - f
            c_s[rows, :] = _chunk_cumsum(jnp.log(f) * LOG2E, tri)
            qr = qr_ref[rows, :]
            q_s[rows, :] = qr * _sigmoid(qr)
        state = [dst[h] for h in range(NH)]
        for it in range(per):
            ci = per - 1 - it if direction == 0 else it
            r0 = ci * CH
            rows = slice(r0, r0 + CH)
            cum = c_s[rows, :]
            ref = c_s[r0 + ref_row:r0 + ref_row + 1, :]
            last = c_s[r0 + last_row:r0 + last_row + 1, :]
            q = q_s[rows, :]
            k = k_s[rows, :]
            e_h = jnp.exp2(cum)
            e_t = jnp.exp2(cum - ref)
            e_kt = jnp.exp2(ref - cum)
            e_kb = jnp.exp2(last - cum)
            el = jnp.exp2(last)
            qh = (q * e_h).astype(BF16)
            qt = (q * e_t).astype(BF16)
            kt = (k * e_kt).astype(BF16)
            kbf = k * e_kb
            kb = kbf.astype(BF16)
            vb = v_ref[rows, :].astype(BF16)
            dob = do_ref[rows, :].astype(BF16)
            heads = [slice(h * HD, (h + 1) * HD) for h in range(NH)]
            first = []
            for h, sl in enumerate(heads):
                s_t = s_ref[ci, h]
                ds_t = state[h]
                ds_b = ds_t.astype(BF16)
                d_a = jnp.where(mask, _dot_nt(dob[:, sl], vb[:, sl]), 0.0).astype(BF16)
                a_t = jnp.where(mask_t, _dot_nt(kt[:, sl], qt[:, sl]), 0.0).astype(BF16)
                d_at = jnp.where(mask_t, _dot_nt(vb[:, sl], dob[:, sl]), 0.0).astype(BF16)
                dkb = _dot(vb[:, sl], ds_b)
                dl_s[it:it + 1, sl] = (el[:, sl] * jnp.sum(ds_t * s_t, axis=0, keepdims=True)
                                       + jnp.sum(dkb * kbf[:, sl], axis=0, keepdims=True))
                state[h] = ds_t * el[:, sl] + _dot_tn(dob[:, sl], qh[:, sl])
                first.append((d_a, a_t, d_at, dkb, _dot_nt(kb[:, sl], ds_b), _dot(dob[:, sl], s_t.astype(BF16))))
            for h, sl in enumerate(heads):
                d_a, a_t, d_at, dkb, dv_state, dq_state = first[h]
                dv = _dot(a_t, dob[:, sl]) + dv_state
                dq_s[rows, sl] = dq_state * e_h[:, sl] + _dot(d_a, kt[:, sl]) * e_t[:, sl]
                dk_s[rows, sl] = _dot(d_at, qt[:, sl]) * e_kt[:, sl] + dkb * e_kb[:, sl]
                if final:
                    dv_ref[rows, sl] = (refs[6][rows, sl] + dv).astype(BF16)
                else:
                    dv_ref[rows, sl] = dv
        for h in range(NH):
            dst[h] = state[h]
        for it in range(per):
            ci = per - 1 - it if direction == 0 else it
            rows = slice(ci * CH, (ci + 1) * CH)
            dq = dq_s[rows, :]
            dk = dk_s[rows, :]
            dg = _chunk_cumsum(dq * q_s[rows, :] - dk * k_s[rows, :] + jnp.where(is_last, dl_s[it:it + 1, :], 0.0), tri_t)
            sig, f = _gate_values(z_ref[rows, :], lb)
            df = dg / f - dk
            acc_ref[0:1, :] += jnp.sum(df * (1.0 - sig), axis=0, keepdims=True)
            dz_ref[rows, :] = (df * (1.0 - lb) * sig * (1.0 - sig)).astype(BF16)
            if final:
                qr = qr_ref[rows, :]
                sq = _sigmoid(qr)
                dq_ref[rows, :] = ((refs[7][rows, :] + dq) * (sq * (1.0 + qr * (1.0 - sq)))).astype(BF16)
            else:
                dq_ref[rows, :] = dq

    tspec = lambda col: pl.BlockSpec((TM, D), lambda i: (tile(i), col))
    sspec = pl.BlockSpec((per, NH, HD, HD), lambda i: (tile(i), 0, 0, 0))
    in_specs = [tspec(direction), tspec(2), tspec(3), _full_spec((2, D)), tspec(0), sspec]
    args = [p, p, p, lb2, do, states]
    if final:
        in_specs += [tspec(0), tspec(0)]
        args += list(prev)
    in_specs.append(ANY)
    args.append(dep)
    odt = BF16 if final else F32
    return pl.pallas_call(
        body, name=name, grid=(nt,), in_specs=in_specs,
        out_specs=[tspec(0), tspec(0), tspec(0), _full_spec((8, D))],
        out_shape=[jax.ShapeDtypeStruct((t, D), BF16), jax.ShapeDtypeStruct((t, D), odt), jax.ShapeDtypeStruct((t, D), odt),
                   jax.ShapeDtypeStruct((8, D), F32)],
        scratch_shapes=[pltpu.VMEM((NH, HD, HD), F32)] + [pltpu.VMEM((TM, D), F32)] * 5 + [pltpu.VMEM((8, D), F32)],
        compiler_params=_cp(1),
    )(*args)


def loss_bwd(x, gain, target, n_lat, name):
    t = x.shape[0]

    def body(x_ref, gain_ref, tg_ref, dx_ref, acc_ref):
        i = pl.program_id(0)

        @pl.when(i == 0)
        def _():
            acc_ref[...] = jnp.zeros_like(acc_ref)

        latf = (i < n_lat).astype(F32)
        x = x_ref[...]
        gain = gain_ref[...]
        r = lax.rsqrt(jnp.mean(x * x, axis=-1, keepdims=True) + EPS)
        xn = x * r
        err = (xn * gain - tg_ref[...]) * latf
        dy = err * (1.0 / D)
        dxn = dy * gain
        dx_ref[...] = r * (dxn - xn * jnp.mean(dxn * xn, axis=-1, keepdims=True))
        acc_ref[0:1, :] += jnp.sum(dy * xn, axis=0, keepdims=True)
        acc_ref[1:2, :] += jnp.sum(err * err, axis=0, keepdims=True)

    return pl.pallas_call(
        body, name=name, grid=(t // TM,),
        in_specs=[_row_spec(D), _full_spec((1, D)), pl.BlockSpec((TM, D), lambda i: (jnp.minimum(i, n_lat - 1), 0))],
        out_specs=[_row_spec(D), _full_spec((8, D))],
        out_shape=[jax.ShapeDtypeStruct((t, D), F32), jax.ShapeDtypeStruct((8, D), F32)],
        compiler_params=_cp(1),
    )(x, gain, target)


def local_step(xs, target, mods, norm1, norm2, norm_f, lbs, gnorm, cw8, cb, wts, n_lat, on_grads, after_backward):
    t = xs.shape[0]
    saved = []
    cache = {}

    def W(name, idx, after=None):
        if (name, idx) not in cache:
            cache[(name, idx)] = wts(name, idx, after)
        return cache[(name, idx)]

    x = xs
    for i in range(DEPTH):
        j = i // 2
        rec = i % 2 == 0
        n1 = norm1[i:i + 1]
        n2 = norm2[i:i + 1]
        s = {"x_in": x}
        if rec:
            p = proj_fwd(x, n1, mods[i], 0, W("hin", j, x), n_lat, F32, f"hin_fwd_{i}")
            o0, st0, o1, st1 = gla_fwd(p, lbs[j], f"gla_fwd_{i}")
            ex = [o0, o1, p, gnorm[j:j + 1]]
            ex_specs = [_row_spec(D), _row_spec(D), _col_spec(4), _full_spec((1, D))]
            xm, y, ab = outproj_fwd(readout_prologue, ex, ex_specs, W("hout", j, o1), x, mods[i], 0, n_lat, f"hout_fwd_{i}")
            s.update(st0=st0, st1=st1)
        else:
            sft = 1 if j % 2 == 0 else CH
            p = proj_fwd(x, n1, mods[i], 0, W("cin", j, x), n_lat, BF16, f"cin_fwd_{i}")
            ex = _conv_args(sft, p, cw8[j], cb[j])
            ex_specs = _conv_specs(sft, t)
            xm, y, ab = outproj_fwd(make_conv_prologue(n_lat, sft), ex, ex_specs, W("cout", j, p), x, mods[i], 0, n_lat, f"cout_fwd_{i}")
        s.update(p=p, ex=ex, ex_specs=ex_specs, y_mix=y, ab_mix=ab, x_mid=xm)
        x, y2, ab2 = mlp_fwd(xm, n2, mods[i], W("w1", i, xm), W("w2", i, xm), n_lat, f"mlp_fwd_{i}")
        s.update(y_mlp=y2, ab_mlp=ab2)
        saved.append(s)

    dx, acc_loss = loss_bwd(x, norm_f, target, n_lat, "loss")
    small = {"norm_f": acc_loss[0:1], "norm1": [None] * DEPTH, "norm2": [None] * DEPTH, "dmod": [None] * DEPTH,
             "gnorm": [None] * 2, "lb": [None] * 2, "cw": [None] * 2, "cb": [None] * 2}
    bshape = lambda w: jax.ShapeDtypeStruct((t, w), BF16)
    token = jnp.zeros((8, 128), F32)
    for i in reversed(range(DEPTH)):
        j = i // 2
        rec = i % 2 == 0
        s = saved[i]
        n1 = norm1[i:i + 1]
        n2 = norm2[i:i + 1]
        dx, dyb, dp1, hb, acc_n2 = mlp_bwd(dx, s["y_mlp"], s["ab_mlp"], s["x_mid"], n2, mods[i], W("w1", i), W("w2", i), n_lat, token,
                                           f"mlp_bwd_{i}")
        token = on_grads(i, "mlp", {"w2": dw_tn(s["ab_mlp"], dyb, 4, True, True, token, f"w2_dw_{i}"),
                                    "w1": dw_tn(hb, dp1, 4, False, False, token, f"w1_dw_{i}")})
        if rec:
            dyb, acc_g1, (do, dgate) = outproj_bwd(
                readout_epilogue, s["ex"], s["ex_specs"], [_row_spec(D), _row_spec(D)],
                [jax.ShapeDtypeStruct((t, D), F32), bshape(D)], W("hout", j), dx, s["y_mix"], mods[i], 0, n_lat, token, f"hout_bwd_{i}")
            token = on_grads(i, "out", {"hout": dw_tn(s["ab_mix"], dyb, 1, False, False, token, f"hout_dw_{i}")})
            dz0, dv0, dq0, acc_l0 = gla_bwd(s["p"], lbs[j], do, s["st0"], 0, None, token, f"gla_bwd0_{i}")
            dz1, dv, dq, acc_l1 = gla_bwd(s["p"], lbs[j], do, s["st1"], 1, (dv0, dq0), token, f"gla_bwd1_{i}")
            dx, hb, dpb, acc_n1 = proj_bwd([dz0, dz1, dv, dq, dgate], W("hin", j), s["x_in"], n1, mods[i], 0, dx, n_lat, i == 0,
                                           f"hin_bwd_{i}")
            small["gnorm"][j] = acc_g1[2:3]
            small["lb"][j] = jnp.concatenate([acc_l0[0:1], acc_l1[0:1]], axis=0)
            mix = ("hout", "hin")
        else:
            sft = 1 if j % 2 == 0 else CH
            dyb, acc_g1, (dconv, dgb) = outproj_bwd(
                make_conv_epilogue(n_lat, sft), s["ex"], s["ex_specs"], [_row_spec(D), _row_spec(D)],
                [bshape(D), bshape(D)], W("cout", j), dx, s["y_mix"], mods[i], 0, n_lat, token, f"cout_bwd_{i}")
            dgc, dxi, acc_c = conv_bwd(dconv, s["p"], cw8[j], sft, n_lat, f"conv_bwd_{i}")
            dx, hb, dpb, acc_n1 = proj_bwd([dgb, dgc, dxi], W("cin", j), s["x_in"], n1, mods[i], 0, dx, n_lat, False, f"cin_bwd_{i}")
            small["cw"][j] = acc_c[0:3]
            small["cb"][j] = acc_c[3:4]
            mix = ("cout", "cin")
        small["norm1"][i] = acc_n1[0:1]
        small["norm2"][i] = acc_n2[0:1]
        z2 = jnp.zeros((2, D), F32)
        small["dmod"][i] = jnp.concatenate([acc_n1[1:3], acc_g1[0:1], acc_n2[1:3], acc_n2[5:6], z2,
                                            acc_n1[3:5], acc_g1[1:2], acc_n2[3:5], acc_n2[6:7], z2], axis=0)
        if i == 0:
            token = after_backward(small, token)
        g = {mix[1]: dw_tn(hb, dpb, 4, False, False, token, f"{mix[1]}_dw_{i}")}
        if not rec:
            g[mix[0]] = dw_tn(s["ab_mix"], dyb, 1, False, False, token, f"{mix[0]}_dw_{i}")
        token = on_grads(i, "mix", g)
    return acc_loss[1:2], dx, token


RB = 256


def cast_to_slot(w2d, layer, k, chip1, name):
    c = w2d.shape[1]
    nblk = k // RB

    def body(chip_ref, w_ref, o_ref):
        o_ref[0] = w_ref[...].astype(BF16)

    return pl.pallas_call(
        body, name=name,
        grid_spec=pltpu.PrefetchScalarGridSpec(
            num_scalar_prefetch=1, grid=(nblk,),
            in_specs=[pl.BlockSpec((RB, c), lambda i, ch: (layer * nblk + i, 0))],
            out_specs=pl.BlockSpec((1, RB, c), lambda i, ch: (ch[0], i, 0))),
        out_shape=jax.ShapeDtypeStruct((4, k, c), BF16), compiler_params=_cp(1))(chip1, w2d)


def sum_slots(own, land, acc, layer, chip1, name):
    _, k, c = own.shape
    nblk = k // RB

    def body(chip_ref, own_ref, l1_ref, l2_ref, l3_ref, acc_ref, o_ref):
        o_ref[...] = ((own_ref[0].astype(F32) + l1_ref[0].astype(F32)) + l2_ref[0].astype(F32)) + l3_ref[0].astype(F32)

    slot = lambda d: pl.BlockSpec((1, RB, c), lambda i, ch: ((ch[0] + d) % 4, i, 0))
    return pl.pallas_call(
        body, name=name,
        grid_spec=pltpu.PrefetchScalarGridSpec(
            num_scalar_prefetch=1, grid=(nblk,),
            in_specs=[slot(0), slot(1), slot(2), slot(3), ANY],
            out_specs=pl.BlockSpec((RB, c), lambda i, ch: (layer * nblk + i, 0))),
        out_shape=jax.ShapeDtypeStruct(acc.shape, F32), input_output_aliases={5: 0}, compiler_params=_cp(1),
    )(chip1, own, land, land, land, acc)


def _adamw_math(w, g, m, v):
    m = ADAM_B1 * m + (1.0 - ADAM_B1) * g
    v = ADAM_B2 * v + (1.0 - ADAM_B2) * jnp.square(g)
    m_hat = m / (1.0 - ADAM_B1 ** ADAM_STEP)
    v_hat = v / (1.0 - ADAM_B2 ** ADAM_STEP)
    delta = -ADAM_LR * (m_hat / (jnp.sqrt(v_hat) + ADAM_EPS) + ADAM_WD * w)
    return delta, m, v


def adamw(gsrcs, w, m, v, name):
    r, c = w.shape
    rb = RB if r % RB == 0 else r
    n_g = len(gsrcs)

    def body(*refs):
        g = refs[0][...]
        for k in range(1, n_g):
            g = g + refs[k][...]
        w_ref, m_ref, v_ref, g_ref, d_ref, mo_ref, vo_ref = refs[n_g:]
        delta, mo, vo = _adamw_math(w_ref[...], g, m_ref[...], v_ref[...])
        g_ref[...] = g
        d_ref[...] = delta
        mo_ref[...] = mo
        vo_ref[...] = vo

    spec = pl.BlockSpec((rb, c), lambda i: (i, 0))
    shp = jax.ShapeDtypeStruct((r, c), F32)
    return pl.pallas_call(body, name=name, grid=(r // rb,), in_specs=[spec] * (n_g + 3), out_specs=[spec] * 4,
                          out_shape=[shp] * 4, compiler_params=_cp(1))(*gsrcs, w, m, v)


ADA_CB = 512


def ada_fwd(cvec, ada_w, bias, name):
    _, _, nc = ada_w.shape

    def body(c_ref, w_ref, b_ref, o_ref):
        cv = c_ref[...]
        a = (cv * _sigmoid(cv)).astype(BF16)
        o_ref[0] = _dot(a, w_ref[0].astype(BF16)) + b_ref[0]

    return pl.pallas_call(
        body, name=name, grid=(DEPTH, nc // ADA_CB),
        in_specs=[pl.BlockSpec((16, D), lambda i, j: (0, 0)), pl.BlockSpec((1, D, ADA_CB), lambda i, j: (i, 0, j)),
                  pl.BlockSpec((1, 1, ADA_CB), lambda i, j: (i, 0, j))],
        out_specs=pl.BlockSpec((1, 16, ADA_CB), lambda i, j: (i, 0, j)),
        out_shape=jax.ShapeDtypeStruct((DEPTH, 16, nc), F32), compiler_params=_cp(2),
    )(cvec, ada_w, bias)


def ada_bwd(cvec, dcols, ada_w, m, v, name):
    _, _, nc = ada_w.shape

    def body(c_ref, d_ref, w_ref, m_ref, v_ref, g_ref, dl_ref, mo_ref, vo_ref, acc_ref):
        @pl.when(jnp.logical_and(pl.program_id(0) == 0, pl.program_id(1) == 0))
        def _():
            acc_ref[...] = jnp.zeros_like(acc_ref)

        cv = c_ref[...]
        a = (cv * _sigmoid(cv)).astype(BF16)
        db = d_ref[0].astype(BF16)
        w = w_ref[0]
        g = _dot_tn(a, db)
        delta, mo, vo = _adamw_math(w, g, m_ref[0], v_ref[0])
        g_ref[0] = g
        dl_ref[0] = delta
        mo_ref[0] = mo
        vo_ref[0] = vo
        acc_ref[...] += _dot_nt(db[8:16, :], w.astype(BF16))

    wspec = pl.BlockSpec((1, D, ADA_CB), lambda i, j: (i, 0, j))
    wshape = jax.ShapeDtypeStruct(ada_w.shape, F32)
    return pl.pallas_call(
        body, name=name, grid=(DEPTH, nc // ADA_CB),
        in_specs=[pl.BlockSpec((16, D), lambda i, j: (0, 0)), pl.BlockSpec((1, 16, ADA_CB), lambda i, j: (i, 0, j)), wspec, wspec, wspec],
        out_specs=[wspec, wspec, wspec, wspec, pl.BlockSpec((8, D), lambda i, j: (0, 0))],
        out_shape=[wshape, wshape, wshape, wshape, jax.ShapeDtypeStruct((8, D), F32)], compiler_params=_cp(2),
    )(cvec, dcols, ada_w, m, v)


def _place():
    return lax.axis_index("x"), lax.axis_index("y"), lax.axis_index("c")


ANY = pl.BlockSpec(memory_space=pl.ANY)
VMEM_SPEC = pl.BlockSpec(memory_space=pltpu.VMEM)


def small_allgather(buf, deps, name):
    r, c = buf.shape
    n_dep = len(deps)

    def body(*refs):
        in_ref = refs[0]
        out_ref, send_sems, recv_sems, loc_sem = refs[1 + n_dep:]
        x, y, cc = _place()
        me = 4 * x + 2 * y + cc
        loc = pltpu.make_async_copy(in_ref, out_ref.at[me], loc_sem)
        loc.start()
        peers = []
        for k in range(1, 8):
            px = 1 - x if k & 4 else x
            py = 1 - y if k & 2 else y
            pc = 1 - cc if k & 1 else cc
            peers.append((px, py, pc))
        sends = []
        for k, peer in enumerate(peers):
            cp = pltpu.make_async_remote_copy(src_ref=in_ref, dst_ref=out_ref.at[me], send_sem=send_sems.at[k],
                                              recv_sem=recv_sems.at[k], device_id=peer, device_id_type=MESH)
            cp.start()
            sends.append(cp)
        for k, (px, py, pc) in enumerate(peers):
            pltpu.make_async_remote_copy(src_ref=in_ref, dst_ref=out_ref.at[4 * px + 2 * py + pc], send_sem=send_sems.at[k],
                                         recv_sem=recv_sems.at[k], device_id=(px, py, pc), device_id_type=MESH).wait_recv()
        for cp in sends:
            cp.wait_send()
        loc.wait()

    return pl.pallas_call(
        body, name=name, in_specs=[VMEM_SPEC] + [ANY] * n_dep, out_specs=VMEM_SPEC,
        out_shape=jax.ShapeDtypeStruct((8, r, c), buf.dtype),
        scratch_shapes=[pltpu.SemaphoreType.DMA((7,)), pltpu.SemaphoreType.DMA((7,)), pltpu.SemaphoreType.DMA],
    )(buf, *deps)


def _chip_peers(x, y):
    return [(1 - x, y), (x, 1 - y), (1 - x, 1 - y)]


HBM_SPEC = pl.BlockSpec(memory_space=pltpu.HBM)
SEM_SPEC = pl.BlockSpec(memory_space=pltpu.SEMAPHORE)
EFFECT = pltpu.SideEffectType.DATAFLOW_SIDE_EFFECTING


def _hbm(a):
    return pltpu.with_memory_space_constraint(a, pltpu.HBM)


def _split_copy(u, p, peer, dst_slot, chip, land_refs, src_refs, sem_refs, cc):
    px, py = peer
    src = land_refs[u].at[chip] if src_refs is None else src_refs[u].at[2 * px + py]
    return pltpu.make_async_remote_copy(src_ref=src, dst_ref=land_refs[u].at[dst_slot], send_sem=sem_refs[2 * u].at[p],
                                        recv_sem=sem_refs[2 * u + 1].at[p], device_id=(px, py, cc), device_id_type=MESH)


def split_start(lands, srcs, after, name):
    n = len(lands)
    ops = list(lands) + (list(srcs) if srcs is not None else [])
    n_ops = len(ops)

    def body(*refs):
        land_refs = refs[:n]
        src_refs = refs[n:n_ops] if srcs is not None else None
        sem_refs = refs[n_ops + 1:n_ops + 1 + 2 * n]
        x, y, cc = _place()
        chip = 2 * x + y
        for u in range(n):
            for p, peer in enumerate(_chip_peers(x, y)):
                _split_copy(u, p, peer, chip, chip, land_refs, src_refs, sem_refs, cc).start()
        refs[-1][...] = jnp.zeros((8, 128), F32)

    outs = pl.pallas_call(
        body, name=name, in_specs=[HBM_SPEC] * n_ops + [ANY],
        out_specs=[SEM_SPEC] * (2 * n) + [HBM_SPEC] * n_ops + [VMEM_SPEC],
        out_shape=[pltpu.SemaphoreType.DMA((3,))] * (2 * n) + [pltpu.HBM(a.shape, a.dtype) for a in ops]
        + [jax.ShapeDtypeStruct((8, 128), F32)],
        input_output_aliases={k: 2 * n + k for k in range(n_ops)},
        compiler_params=pltpu.CompilerParams(has_side_effects=EFFECT),
    )(*[_hbm(a) for a in ops], after)
    sems = list(outs[:2 * n])
    thru = list(outs[2 * n:2 * n + n_ops])
    return sems, thru[:n], thru[n:], outs[-1]


def split_wait(lands, srcs, sems, after, name):
    n = len(lands)
    ops = list(lands) + (list(srcs) if srcs is not None else [])
    n_ops = len(ops)

    def body(*refs):
        land_refs = refs[:n]
        src_refs = refs[n:n_ops] if srcs is not None else None
        sem_refs = refs[n_ops:n_ops + 2 * n]
        x, y, cc = _place()
        chip = 2 * x + y
        for u in range(n):
            for p, peer in enumerate(_chip_peers(x, y)):
                cp = _split_copy(u, p, peer, 2 * peer[0] + peer[1], chip, land_refs, src_refs, sem_refs, cc)
                cp.wait_send()
                cp.wait_recv()

    outs = pl.pallas_call(
        body, name=name, in_specs=[HBM_SPEC] * n_ops + [SEM_SPEC] * (2 * n) + [ANY],
        out_specs=[HBM_SPEC] * n_ops, out_shape=[pltpu.HBM(a.shape, a.dtype) for a in ops],
        input_output_aliases={k: k for k in range(n_ops)},
        compiler_params=pltpu.CompilerParams(has_side_effects=EFFECT),
    )(*ops, *sems, after)
    return list(outs[:n]), list(outs[n:])


def _sibling_copy(k, src_refs, zone_refs, sem_refs):
    x, y, cc = _place()
    return pltpu.make_async_remote_copy(src_ref=src_refs[k], dst_ref=zone_refs[k], send_sem=sem_refs[2 * k], recv_sem=sem_refs[2 * k + 1],
                                        device_id=(x, y, 1 - cc), device_id_type=MESH)


def sibling_start(parts, name):
    n = len(parts)
    ops = list(parts) + [lax.empty(p.shape, p.dtype) for p in parts]

    def body(*refs):
        for k in range(n):
            _sibling_copy(k, refs[:n], refs[n:2 * n], refs[2 * n:4 * n]).start()

    outs = pl.pallas_call(
        body, name=name, in_specs=[HBM_SPEC] * (2 * n),
        out_specs=[SEM_SPEC] * (2 * n) + [HBM_SPEC] * (2 * n),
        out_shape=[pltpu.SemaphoreType.DMA(())] * (2 * n) + [pltpu.HBM(a.shape, a.dtype) for a in ops],
        input_output_aliases={k: 2 * n + k for k in range(2 * n)},
        compiler_params=pltpu.CompilerParams(has_side_effects=EFFECT),
    )(*[_hbm(a) for a in ops])
    return list(outs[2 * n:3 * n]), list(outs[3 * n:]), list(outs[:2 * n])


def sibling_wait(parts, zones, sems, after, name):
    n = len(parts)

    def body(*refs):
        for k in range(n):
            cp = _sibling_copy(k, refs[:n], refs[n:2 * n], refs[2 * n:4 * n])
            cp.wait_send()
            cp.wait_recv()

    outs = pl.pallas_call(
        body, name=name, in_specs=[HBM_SPEC] * (2 * n) + [SEM_SPEC] * (2 * n) + [ANY],
        out_specs=[HBM_SPEC] * (2 * n), out_shape=[pltpu.HBM(a.shape, a.dtype) for a in list(parts) + list(zones)],
        input_output_aliases={k: k for k in range(2 * n)},
        compiler_params=pltpu.CompilerParams(has_side_effects=EFFECT),
    )(*parts, *zones, *sems, after)
    return list(outs[:n]), list(outs[n:])


SMALL_ROWS = 88
FIN_ROWS = 72


def small_finish(g3, g4, c_ctx, lbp, name):
    def body(g3_ref, g4_ref, cc_ref, lbp_ref, o_ref, s_ref):
        s = g3_ref[0]
        for k in range(1, 8):
            s = s + g3_ref[k]
        s_ref[...] = s
        for i in range(DEPTH):
            o_ref[8 * i:8 * i + 8, :] = s_ref[16 * i:16 * i + 8, :] + s_ref[16 * i + 8:16 * i + 16, :]
        acc = g4_ref[0]
        for k in (2, 4, 6):
            acc = acc + g4_ref[k]
        cc = cc_ref[...]
        sg = _sigmoid(cc)
        row = jnp.sum(acc, axis=0, keepdims=True) * (sg * (1.0 + cc * (1.0 - sg)))
        o_ref[32:40, :] = jnp.broadcast_to(row, (8, D))
        o_ref[40:64, :] = s_ref[64:88, :]
        o_ref[64:72, :] = jnp.zeros((8, D), F32)
        for d in range(2):
            pp = lbp_ref[2 * d:2 * d + 1, :] * lbp_ref[2 * d + 1:2 * d + 2, :] * s_ref[75 + d:76 + d, :]
            o_ref[64 + 2 * d:65 + 2 * d, :] = -pp
            o_ref[65 + 2 * d:66 + 2 * d, :] = pp

    return pl.pallas_call(
        body, name=name, in_specs=[VMEM_SPEC] * 4, out_specs=VMEM_SPEC,
        out_shape=jax.ShapeDtypeStruct((FIN_ROWS, D), F32),
        scratch_shapes=[pltpu.VMEM((SMALL_ROWS, D), F32)],
    )(g3, g4, c_ctx, lbp)


def _pack_rows(arrs):
    flat = jnp.concatenate([a.reshape(-1) for a in arrs])
    n = -(-flat.shape[0] // (8 * D)) * 8 * D
    return jnp.pad(flat, (0, n - flat.shape[0])).reshape(n // D, D)


def _unpack_rows(packed, shapes):
    flat = packed.reshape(-1)
    outs, off = [], 0
    for s in shapes:
        size = 1
        for k in s:
            size *= k
        outs.append(flat[off:off + size].reshape(s))
        off += size
    return outs


def _pad8(a):
    return jnp.pad(a, ((0, 8 - a.shape[0]), (0, 0)))


def kernel(x, c, ctx, c_ctx, ada_w, ada_b, norm1, norm2, norm_f, mlp_w1, mlp_w2, hgrn_w_in, hgrn_lb, hgrn_gnorm, hgrn_w_out, conv_w_in, conv_w, conv_b, conv_w_out, loss_target, m_c_ctx, m_ada_w, m_ada_b, m_norm1, m_norm2, m_norm_f, m_mlp_w1, m_mlp_w2, m_hgrn_w_in, m_hgrn_lb, m_hgrn_gnorm, m_hgrn_w_out, m_conv_w_in, m_conv_w, m_conv_b, m_conv_w_out, v_c_ctx, v_ada_w, v_ada_b, v_norm1, v_norm2, v_norm_f, v_mlp_w1, v_mlp_w2, v_hgrn_w_in, v_hgrn_lb, v_hgrn_gnorm, v_hgrn_w_out, v_conv_w_in, v_conv_w, v_conv_b, v_conv_w_out):
    xi, yi, ci = _place()
    me = 4 * xi + 2 * yi + ci
    chip = 2 * xi + yi
    seq = x.shape[1]
    assert ctx.shape[1] == TM and seq % TM == 0 and (seq + TM) % TMW == 0
    n_lat = seq // TM
    sd = D // 4
    nca = ada_w.shape[2]
    xs = jnp.concatenate([x[0], ctx[0]], axis=0)

    big = [(mlp_w1, m_mlp_w1, v_mlp_w1), (mlp_w2, m_mlp_w2, v_mlp_w2), (hgrn_w_in, m_hgrn_w_in, v_hgrn_w_in),
           (hgrn_w_out, m_hgrn_w_out, v_hgrn_w_out), (conv_w_in, m_conv_w_in, v_conv_w_in), (conv_w_out, m_conv_w_out, v_conv_w_out)]
    big_names = ["w1", "w2", "hin", "hout", "cin", "cout"]
    flat2 = lambda a: a.reshape(a.shape[0] * a.shape[1], a.shape[2])
    tensors = dict(zip(big_names, big))
    chip1 = jnp.reshape(chip, (1,)).astype(jnp.int32)
    order = []
    for i in range(DEPTH):
        order += [("hin", i // 2), ("hout", i // 2)] if i % 2 == 0 else [("cin", i // 2), ("cout", i // 2)]
        order += [("w1", i), ("w2", i)]
    lands = [cast_to_slot(flat2(tensors[n][0]), idx, tensors[n][0].shape[1], chip1, f"cast_{n}_{idx}") for n, idx in order]
    sh_rows = jnp.concatenate([hgrn_lb.reshape(4, sd), conv_w.reshape(6, sd), conv_b.reshape(2, sd)], axis=0)
    buf1 = jnp.concatenate([c, jnp.pad(sh_rows, ((0, 0), (0, D - sd))), jnp.zeros((3, D), F32)], axis=0)
    g1 = small_allgather(buf1, [], "gather_small_in")
    first_sems, first_lands, _, first_token = split_start(lands[:1], None, g1, "gather_start_first")
    cvec = jnp.concatenate([g1[:, 0, :], jnp.broadcast_to(c_ctx[None], (8, D))], axis=0)
    shf = g1[0::2, 1:13, :sd].transpose(1, 0, 2).reshape(12, D)
    lb_p = jax.nn.softmax(shf[0:4].reshape(2, 2, D), axis=1)
    lower = jnp.cumsum(lb_p, axis=1) - lb_p[:, :1]
    lbs = [lower[:, 0], lower[:, 1]]
    cw8 = [_pad8(shf[4:7]), _pad8(shf[7:10])]
    cb = [shf[10:11], shf[11:12]]

    bias = lax.dynamic_slice_in_dim(ada_b, chip * nca, nca, axis=1).reshape(DEPTH, 1, nca)
    ada_part = ada_fwd(cvec, ada_w, bias, "ada_fwd")
    g2 = small_allgather(ada_part.reshape(DEPTH * 16, nca), [first_token] + lands[1:], "gather_ada")
    ada_full = g2[0::2].reshape(4, DEPTH, 16, nca).transpose(1, 2, 0, 3).reshape(DEPTH, 16, 4 * nca)
    lat = lax.dynamic_slice_in_dim(ada_full, me, 1, axis=1)[:, 0]
    mods = [jnp.stack([_pad8(lat[i].reshape(6, D)), _pad8(ada_full[i, 8].reshape(6, D))]) for i in range(DEPTH)]

    rest_sems, rest_lands, _, rest_token = split_start(lands[1:], None, g2, "gather_start")
    w_sems = first_sems + rest_sems
    lands = first_lands + rest_lands
    unit = {key: u for u, key in enumerate(order)}

    def wts(n, idx, after):
        u = unit[(n, idx)]
        if u == 0:
            after = rest_token
        (w,), _ = split_wait([lands[u]], None, w_sems[2 * u:2 * u + 2], after, f"gather_wait_{n}_{idx}")
        return w.reshape(w.shape[0] * w.shape[1], w.shape[2]) if n in ("w2", "hout", "cout") else w

    started = []

    def on_grads(i, tag, g):
        names = sorted(g)
        gs = [g[n].reshape(4, g[n].shape[0] * g[n].shape[1] // 4, g[n].shape[2]) for n in names]
        sems, zones, srcs, token = split_start([lax.empty(a.shape, BF16) for a in gs], gs, chip1, f"grad_start_{tag}_{i}")
        started.append(([(n, i if n in ("w1", "w2") else i // 2) for n in names], sems, zones, srcs))
        return token

    done = {}
    acc = {n: lax.empty(flat2(w).shape, F32) for n, (w, _, _) in tensors.items()}
    early_names = ["w1", "w2", "cin", "hout", "cout"]
    late_names = ["hin"]

    def finish_units(group, after, name):
        units = [(key, sems[2 * u:2 * u + 2], zones[u], srcs[u]) for ks, sems, zones, srcs in group for u, key in enumerate(ks)]
        zones, srcs = split_wait([u[2] for u in units], [u[3] for u in units], [s for u in units for s in u[1]], after, name)
        for (key, _, _, _), zone, own in zip(units, zones, srcs):
            acc[key[0]] = sum_slots(own, zone, acc[key[0]], key[1], chip1, f"sum_{key[0]}_{key[1]}")

    def after_backward(small, token):
        rows3 = jnp.concatenate(small["dmod"] + small["norm1"] + small["norm2"] + [small["norm_f"]] + small["gnorm"]
                                + [small["lb"][1]] + small["cw"] + small["cb"] + [jnp.tile(token[0:3], (1, D // 128))], axis=0)
        g3 = small_allgather(rows3, [], "gather_small_out")
        dmat = g3[:, :64].reshape(8, DEPTH, 2, 8, D)[:, :, :, :6].transpose(1, 2, 0, 3, 4).reshape(DEPTH, 16, 6 * D)
        dcols = lax.dynamic_slice_in_dim(dmat, chip * nca, nca, axis=2)
        *done["ada"], acc4 = ada_bwd(cvec, dcols, ada_w, m_ada_w, v_ada_w, "ada_bwd")
        g4 = small_allgather(acc4, [], "gather_cctx")
        done["fin"] = small_finish(g3, g4, c_ctx[None], _pad8(lb_p.reshape(4, D)), "small_finish")
        finish_units(list(started), done["fin"], "grad_wait_early")
        done["sib_early"] = sibling_start([acc[n] for n in early_names], "sibling_start_early")
        return done["sib_early"][0][-1]

    lane, dx, last_token = local_step(xs, loss_target[0], mods, norm1, norm2, norm_f[None], lbs, hgrn_gnorm, cw8, cb, wts, n_lat,
                                      on_grads, after_backward)
    loss = lax.psum(0.5 * jnp.sum(lane) / D, ("x", "y", "c"))
    grad_x = dx[None]
    g_ada_w, d_ada_w, nm_ada_w, nv_ada_w = done["ada"]
    fin = done["fin"]
    cols = lambda a: lax.dynamic_slice_in_dim(a, chip * sd, sd, axis=a.ndim - 1)
    small_g = [fin[32], fin[0:32].reshape(DEPTH, 8, D)[:, :6].reshape(DEPTH, 6 * D), fin[40:44], fin[44:48], fin[48], fin[49:51],
               cols(fin[64:68].reshape(2, 2, D)), cols(fin[53:59].reshape(2, 3, D)), cols(fin[59:61])]
    small_w = [c_ctx, ada_b, norm1, norm2, norm_f, hgrn_gnorm, hgrn_lb, conv_w, conv_b]
    small_m = [m_c_ctx, m_ada_b, m_norm1, m_norm2, m_norm_f, m_hgrn_gnorm, m_hgrn_lb, m_conv_w, m_conv_b]
    small_v = [v_c_ctx, v_ada_b, v_norm1, v_norm2, v_norm_f, v_hgrn_gnorm, v_hgrn_lb, v_conv_w, v_conv_b]
    shapes = [w.shape for w in small_w]
    packed = adamw([_pack_rows(small_g)], _pack_rows(small_w), _pack_rows(small_m), _pack_rows(small_v), "adamw_small")
    s_g, s_d, s_m, s_v = [_unpack_rows(p, shapes) for p in packed]

    results = {}

    def finish_tensors(names, sib, after, name):
        mine, other = sibling_wait(*sib, after, name)
        for n, pm, po in zip(names, mine, other):
            w, m, v = tensors[n]
            results[n] = [a.reshape(w.shape) for a in adamw([pm, po], flat2(w), flat2(m), flat2(v), f"adamw_{n}")]

    finish_tensors(early_names, done["sib_early"], last_token, "sibling_wait_early")
    finish_units(started[-1:], results["cout"][0], "grad_wait_late")
    sib_late = sibling_start([acc[n] for n in late_names], "sibling_start_late")
    finish_tensors(late_names, sib_late, results["cin"][0], "sibling_wait_late")
    b_g, b_d, b_m, b_v = [[results[n][k] for n in big_names] for k in range(4)]

    def ordered(s, a, b):
        return [s[0], a, s[1], s[2], s[3], s[4], b[0], b[1], b[2], s[6], s[5], b[3], b[4], s[7], s[8], b[5]]

    return (loss, grad_x, *ordered(s_g, g_ada_w, b_g), *ordered(s_d, d_ada_w, b_d), *ordered(s_m, nm_ada_w, b_m),
            *ordered(s_v, nv_ada_w, b_v))
```

```python
import functools

import jax
import jax.numpy as jnp
from jax import lax
from jax.experimental import pallas as pl
from jax.experimental.pallas import tpu as pltpu

F32 = jnp.float32
BF16 = jnp.bfloat16
MESH = pl.DeviceIdType.MESH

D = 1024
HD = 128
NH = D // HD
CH = 64
TM = 256
TMW = 2816
EPS = 1e-6
DEPTH = 4
V7X_VMEM_BYTES = 64 * 1024 * 1024
VMEM_LIMIT = V7X_VMEM_BYTES - 8 * 1024 * 1024

ADAM_LR = 0.001
ADAM_B1 = 0.9
ADAM_B2 = 0.999
ADAM_EPS = 1e-08
ADAM_WD = 0.01
ADAM_STEP = 10


def _cp(n_grid):
    return pltpu.CompilerParams(dimension_semantics=("arbitrary",) * n_grid, vmem_limit_bytes=VMEM_LIMIT)


def _dot(a, b):
    return jnp.dot(a, b, preferred_element_type=F32)


def _dot_nt(a, b):
    return lax.dot_general(a, b, (((1,), (1,)), ((), ())), preferred_element_type=F32)


def _dot_tn(a, b):
    return lax.dot_general(a, b, (((0,), (0,)), ((), ())), preferred_element_type=F32)


def _sigmoid(z):
    return 1.0 / (1.0 + jnp.exp(-z))


def _norm_mod(x, gain, sh, sc):
    r = lax.rsqrt(jnp.mean(x * x, axis=-1, keepdims=True) + EPS)
    xn = x * r
    yn = xn * gain
    return r, xn, yn, yn * (1.0 + sc) + sh


def _row_spec(width):
    return pl.BlockSpec((TM, width), lambda i: (i, 0))


def _col_spec(col):
    return pl.BlockSpec((TM, D), lambda i: (i, col))


def _full_spec(shape):
    nd = len(shape)
    return pl.BlockSpec(shape, lambda i: (0,) * nd)


def _mod_spec(n_lat):
    return pl.BlockSpec((1, 8, D), lambda i: (i // n_lat, 0, 0))


def _f32(ref):
    return ref[...].astype(F32)


def proj_fwd(x, gain, mod, m0, w4, n_lat, dtype, name):
    t = x.shape[0]
    nb, _, ns = w4.shape

    def body(x_ref, gain_ref, mod_ref, w_ref, p_ref):
        _, _, _, h = _norm_mod(x_ref[...], gain_ref[...], mod_ref[0, m0:m0 + 1, :], mod_ref[0, m0 + 1:m0 + 2, :])
        hb = h.astype(BF16)
        for c in range(nb):
            p_ref[:, c * ns:(c + 1) * ns] = _dot(hb, w_ref[c]).astype(dtype)

    return pl.pallas_call(
        body, name=name, grid=(t // TM,),
        in_specs=[_row_spec(D), _full_spec((1, D)), _mod_spec(n_lat), _full_spec(w4.shape)],
        out_specs=_row_spec(nb * ns),
        out_shape=jax.ShapeDtypeStruct((t, nb * ns), dtype),
        compiler_params=_cp(1),
    )(x, gain, mod, w4)


def proj_bwd(parts, w4, x, gain, mod, m0, dx_in, n_lat, lat_only, name):
    t = x.shape[0]
    nb, _, ns = w4.shape
    n = nb * ns
    n_parts = len(parts)
    widths = [p.shape[1] for p in parts]
    offs = [sum(widths[:k]) for k in range(n_parts)]
    assert sum(widths) == n
    single = n_parts == 1

    def body(*refs):
        part_refs = refs[:n_parts]
        w_ref, x_ref, gain_ref, mod_ref, dxin_ref = refs[n_parts:n_parts + 5]
        rest = refs[n_parts + 5:]
        if single:
            dx_ref, hb_ref, acc_ref = rest
            src = part_refs[0]
        else:
            dx_ref, hb_ref, acc_ref, dpb_ref = rest
            for p_ref, off, w in zip(part_refs, offs, widths):
                dpb_ref[:, off:off + w] = p_ref[...]
            src = dpb_ref
        i = pl.program_id(0)

        @pl.when(i == 0)
        def _():
            acc_ref[...] = jnp.zeros_like(acc_ref)

        gain = gain_ref[...]
        sc = mod_ref[0, m0 + 1:m0 + 2, :]
        r, xn, yn, h = _norm_mod(x_ref[...], gain, mod_ref[0, m0:m0 + 1, :], sc)
        hb_ref[...] = h.astype(BF16)
        dh = _dot_nt(src[:, 0:ns], w_ref[0])
        for c in range(1, nb):
            dh = dh + _dot_nt(src[:, c * ns:(c + 1) * ns], w_ref[c])
        dsh = jnp.sum(dh, axis=0, keepdims=True)
        dsc = jnp.sum(dh * yn, axis=0, keepdims=True)
        dyn = dh * (1.0 + sc)
        dgain = jnp.sum(dyn * xn, axis=0, keepdims=True)
        dxn = dyn * gain
        dx = dxin_ref[...] + r * (dxn - xn * jnp.mean(dxn * xn, axis=-1, keepdims=True))
        if lat_only:
            @pl.when(i < n_lat)
            def _():
                dx_ref[...] = dx
        else:
            dx_ref[...] = dx
        latf = (i < n_lat).astype(F32)
        ctxf = 1.0 - latf
        acc_ref[0:1, :] += dgain
        acc_ref[1:2, :] += dsh * latf
        acc_ref[2:3, :] += dsc * latf
        acc_ref[3:4, :] += dsh * ctxf
        acc_ref[4:5, :] += dsc * ctxf

    dx_rows = n_lat * TM if lat_only else t
    dx_spec = pl.BlockSpec((TM, D), lambda i: (jnp.minimum(i, n_lat - 1), 0)) if lat_only else _row_spec(D)
    out_specs = [dx_spec, _row_spec(D), _full_spec((8, D))]
    out_shape = [jax.ShapeDtypeStruct((dx_rows, D), F32), jax.ShapeDtypeStruct((t, D), BF16), jax.ShapeDtypeStruct((8, D), F32)]
    if not single:
        out_specs.append(_row_spec(n))
        out_shape.append(jax.ShapeDtypeStruct((t, n), BF16))
    outs = pl.pallas_call(
        body, name=name, grid=(t // TM,),
        in_specs=[_row_spec(w) for w in widths]
        + [_full_spec(w4.shape), _row_spec(D), _full_spec((1, D)), _mod_spec(n_lat), _row_spec(D)],
        out_specs=out_specs, out_shape=out_shape, compiler_params=_cp(1),
    )(*parts, w4, x, gain, mod, dx_in)
    if single:
        return outs[0], outs[1], parts[0], outs[2]
    return outs[0], outs[1], outs[3], outs[2]


def dw_tn(a, b, nb, a_blocked, square_a, dep, name):
    t = a.shape[0]
    ka = a.shape[1] // nb if a_blocked else a.shape[1]
    kb = b.shape[1] if a_blocked else b.shape[1] // nb
    n_k = t // TMW

    def body(a_ref, b_ref, _, o_ref, acc):
        k = pl.program_id(1)

        @pl.when(k == 0)
        def _():
            acc[...] = jnp.zeros_like(acc)

        a = a_ref[...]
        acc[...] += _dot_tn(a * a if square_a else a, b_ref[...])

        @pl.when(k == n_k - 1)
        def _():
            o_ref[0] = acc[...].astype(BF16)

    a_spec = pl.BlockSpec((TMW, ka), (lambda j, k: (k, j)) if a_blocked else (lambda j, k: (k, 0)))
    b_spec = pl.BlockSpec((TMW, kb), (lambda j, k: (k, 0)) if a_blocked else (lambda j, k: (k, j)))
    return pl.pallas_call(
        body, name=name, grid=(nb, n_k),
        in_specs=[a_spec, b_spec, ANY],
        out_specs=pl.BlockSpec((1, ka, kb), lambda j, k: (j, 0, 0)),
        out_shape=jax.ShapeDtypeStruct((nb, ka, kb), BF16),
        scratch_shapes=[pltpu.VMEM((ka, kb), F32)],
        compiler_params=_cp(2),
    )(a, b, dep)


def outproj_fwd(prologue, extras, extra_specs, w, x, mod, m0, n_lat, name):
    t = x.shape[0]
    k = w.shape[0]
    n_extra = len(extras)

    def body(*refs):
        ex = refs[:n_extra]
        w_ref, x_ref, mod_ref, xo_ref, y_ref, ab_ref = refs[n_extra:]
        ab = prologue(pl.program_id(0), *ex).astype(BF16)
        ab_ref[...] = ab
        y = _dot(ab, w_ref[...])
        y_ref[...] = y.astype(BF16)
        xo_ref[...] = x_ref[...] + mod_ref[0, m0 + 2:m0 + 3, :] * y

    return pl.pallas_call(
        body, name=name, grid=(t // TM,),
        in_specs=list(extra_specs) + [_full_spec(w.shape), _row_spec(D), _mod_spec(n_lat)],
        out_specs=[_row_spec(D), _row_spec(D), _row_spec(k)],
        out_shape=[jax.ShapeDtypeStruct((t, D), F32), jax.ShapeDtypeStruct((t, D), BF16), jax.ShapeDtypeStruct((t, k), BF16)],
        compiler_params=_cp(1),
    )(*extras, w, x, mod)


def outproj_bwd(epilogue, extras, extra_specs, ep_out_specs, ep_out_shapes, w, dxn, y, mod, m0, n_lat, dep, name):
    t = dxn.shape[0]
    n_extra = len(extras)

    def body(*refs):
        ex = refs[:n_extra]
        w_ref, dxn_ref, y_ref, mod_ref, _, dyb_ref, acc_ref = refs[n_extra:n_extra + 7]
        ep_outs = refs[n_extra + 7:]
        i = pl.program_id(0)

        @pl.when(i == 0)
        def _():
            acc_ref[...] = jnp.zeros_like(acc_ref)

        dxv = dxn_ref[...]
        dyb = (dxv * mod_ref[0, m0 + 2:m0 + 3, :]).astype(BF16)
        dyb_ref[...] = dyb
        dg = jnp.sum(dxv * _f32(y_ref), axis=0, keepdims=True)
        latf = (i < n_lat).astype(F32)
        acc_ref[0:1, :] += dg * latf
        acc_ref[1:2, :] += dg * (1.0 - latf)
        epilogue(i, _dot_nt(dyb, w_ref[...]), ex, ep_outs, acc_ref)

    outs = pl.pallas_call(
        body, name=name, grid=(t // TM,),
        in_specs=list(extra_specs) + [_full_spec(w.shape), _row_spec(D), _row_spec(D), _mod_spec(n_lat), ANY],
        out_specs=[_row_spec(D), _full_spec((8, D))] + list(ep_out_specs),
        out_shape=[jax.ShapeDtypeStruct((t, D), BF16), jax.ShapeDtypeStruct((8, D), F32)] + list(ep_out_shapes),
        compiler_params=_cp(1),
    )(*extras, w, dxn, y, mod, dep)
    return outs[0], outs[1], outs[2:]


def mlp_fwd(x, gain, mod, w1, w2, n_lat, name):
    t = x.shape[0]
    nb, _, ns = w1.shape

    def body(x_ref, gain_ref, mod_ref, w1_ref, w2_ref, xo_ref, y_ref, rb_ref):
        x = x_ref[...]
        _, _, _, h = _norm_mod(x, gain_ref[...], mod_ref[0, 3:4, :], mod_ref[0, 4:5, :])
        hb = h.astype(BF16)
        y = None
        for c in range(nb):
            r = jnp.maximum(_dot(hb, w1_ref[c]), 0.0)
            rb_ref[:, c * ns:(c + 1) * ns] = r.astype(BF16)
            yc = _dot((r * r).astype(BF16), w2_ref[c * ns:(c + 1) * ns, :])
            y = yc if y is None else y + yc
        y_ref[...] = y.astype(BF16)
        xo_ref[...] = x + mod_ref[0, 5:6, :] * y

    return pl.pallas_call(
        body, name=name, grid=(t // TM,),
        in_specs=[_row_spec(D), _full_spec((1, D)), _mod_spec(n_lat), _full_spec(w1.shape), _full_spec(w2.shape)],
        out_specs=[_row_spec(D), _row_spec(D), _row_spec(nb * ns)],
        out_shape=[jax.ShapeDtypeStruct((t, D), F32), jax.ShapeDtypeStruct((t, D), BF16), jax.ShapeDtypeStruct((t, nb * ns), BF16)],
        compiler_params=_cp(1),
    )(x, gain, mod, w1, w2)


def mlp_bwd(dxn, y, ab, x, gain, mod, w1, w2, n_lat, dep, name):
    t = x.shape[0]
    nb, _, ns = w1.shape

    def body(dxn_ref, y_ref, rb_ref, x_ref, gain_ref, mod_ref, w1_ref, w2_ref, _, dx_ref, dyb_ref, dp_ref, hb_ref, acc_ref):
        i = pl.program_id(0)

        @pl.when(i == 0)
        def _():
            acc_ref[...] = jnp.zeros_like(acc_ref)

        dxv = dxn_ref[...]
        dyb = (dxv * mod_ref[0, 5:6, :]).astype(BF16)
        dyb_ref[...] = dyb
        dg = jnp.sum(dxv * _f32(y_ref), axis=0, keepdims=True)
        gain = gain_ref[...]
        sc = mod_ref[0, 4:5, :]
        r, xn, yn, h = _norm_mod(x_ref[...], gain, mod_ref[0, 3:4, :], sc)
        hb_ref[...] = h.astype(BF16)
        dps = []
        for c in range(nb):
            cols = slice(c * ns, (c + 1) * ns)
            dp = (_dot_nt(dyb, w2_ref[cols, :]) * (2.0 * rb_ref[:, cols].astype(F32))).astype(BF16)
            dp_ref[:, cols] = dp
            dps.append(dp)
        dh = _dot_nt(dps[0], w1_ref[0])
        for c in range(1, nb):
            dh = dh + _dot_nt(dps[c], w1_ref[c])
        dsh = jnp.sum(dh, axis=0, keepdims=True)
        dsc = jnp.sum(dh * yn, axis=0, keepdims=True)
        dyn = dh * (1.0 + sc)
        dgain = jnp.sum(dyn * xn, axis=0, keepdims=True)
        dxn_ = dyn * gain
        dx_ref[...] = dxv + r * (dxn_ - xn * jnp.mean(dxn_ * xn, axis=-1, keepdims=True))
        latf = (i < n_lat).astype(F32)
        ctxf = 1.0 - latf
        acc_ref[0:1, :] += dgain
        acc_ref[1:2, :] += dsh * latf
        acc_ref[2:3, :] += dsc * latf
        acc_ref[3:4, :] += dsh * ctxf
        acc_ref[4:5, :] += dsc * ctxf
        acc_ref[5:6, :] += dg * latf
        acc_ref[6:7, :] += dg * ctxf

    return pl.pallas_call(
        body, name=name, grid=(t // TM,),
        in_specs=[_row_spec(D), _row_spec(D), _row_spec(nb * ns), _row_spec(D), _full_spec((1, D)), _mod_spec(n_lat),
                  _full_spec(w1.shape), _full_spec(w2.shape), ANY],
        out_specs=[_row_spec(D), _row_spec(D), _row_spec(nb * ns), _row_spec(D), _full_spec((8, D))],
        out_shape=[jax.ShapeDtypeStruct((t, D), F32), jax.ShapeDtypeStruct((t, D), BF16), jax.ShapeDtypeStruct((t, nb * ns), BF16),
                   jax.ShapeDtypeStruct((t, D), BF16), jax.ShapeDtypeStruct((8, D), F32)],
        compiler_params=_cp(1),
    )(dxn, y, ab, x, gain, mod, w1, w2, dep)


def readout_prologue(i, o0_ref, o1_ref, gate_ref, gn_ref):
    o = _f32(o0_ref) + _f32(o1_ref)
    gate = gate_ref[...]
    w = gn_ref[...] * (gate * _sigmoid(gate))
    pieces = []
    for h in range(NH):
        sl = slice(h * HD, (h + 1) * HD)
        oh = o[:, sl]
        pieces.append(oh * lax.rsqrt(jnp.mean(oh * oh, axis=-1, keepdims=True) + EPS) * w[:, sl])
    return jnp.concatenate(pieces, axis=1)


def readout_epilogue(i, da, ex, outs, acc_ref):
    o0_ref, o1_ref, gate_ref, gn_ref = ex
    do_ref, dgate_ref = outs
    o = _f32(o0_ref) + _f32(o1_ref)
    gate = gate_ref[...]
    gn = gn_ref[...]
    sg = _sigmoid(gate)
    silu = gate * sg
    dsilu = sg * (1.0 + gate * (1.0 - sg))
    for h in range(NH):
        sl = slice(h * HD, (h + 1) * HD)
        oh = o[:, sl]
        r = lax.rsqrt(jnp.mean(oh * oh, axis=-1, keepdims=True) + EPS)
        nh = oh * r
        dah = da[:, sl]
        acc_ref[2:3, sl] += jnp.sum(dah * nh * silu[:, sl], axis=0, keepdims=True)
        dgate_ref[:, sl] = (dah * nh * gn[:, sl] * dsilu[:, sl]).astype(BF16)
        dn = dah * gn[:, sl] * silu[:, sl]
        do_ref[:, sl] = r * (dn - nh * jnp.mean(dn * nh, axis=-1, keepdims=True))


def _seg_masks(i, n_lat):
    rows = lax.broadcasted_iota(jnp.int32, (TM, 1), 0)
    latf = (i < n_lat).astype(F32)
    ctxf = 1.0 - latf
    prev_ok = (rows % CH != 0).astype(F32) * latf + (rows != 0).astype(F32) * ctxf
    next_ok = (rows % CH != CH - 1).astype(F32) * latf + (rows != TM - 1).astype(F32) * ctxf
    return prev_ok, next_ok


def _shifts(i, n_lat, sft, cur, halo_prev, halo_next):
    if sft == 1:
        prev_ok, next_ok = _seg_masks(i, n_lat)
        return pltpu.roll(cur, 1, 0) * prev_ok, pltpu.roll(cur, TM - 1, 0) * next_ok
    has_prev = jnp.logical_and(i > 0, i < n_lat).astype(F32)
    has_next = (i < n_lat - 1).astype(F32)
    prev = jnp.concatenate([halo_prev * has_prev, cur[:TM - CH]], axis=0)
    nxt = jnp.concatenate([cur[CH:], halo_next * has_next], axis=0)
    return prev, nxt


def _conv_u(sft, ex):
    if sft == 1:
        gb_ref, gc_ref, xi_ref, cw_ref, cb_ref = ex
        return gb_ref, _f32(gc_ref) * _f32(xi_ref), None, None, cw_ref, cb_ref
    gb_ref, gc_ref, xi_ref, gcp_ref, xip_ref, gcn_ref, xin_ref, cw_ref, cb_ref = ex
    return gb_ref, _f32(gc_ref) * _f32(xi_ref), _f32(gcp_ref) * _f32(xip_ref), _f32(gcn_ref) * _f32(xin_ref), cw_ref, cb_ref


def _conv_value(i, n_lat, sft, ex):
    gb_ref, u, up, un, cw_ref, cb_ref = _conv_u(sft, ex)
    u_prev, u_next = _shifts(i, n_lat, sft, u, up, un)
    return gb_ref, cb_ref[...] + cw_ref[0:1, :] * u_prev + cw_ref[1:2, :] * u + cw_ref[2:3, :] * u_next


def make_conv_prologue(n_lat, sft):
    def prologue(i, *ex):
        gb_ref, conv = _conv_value(i, n_lat, sft, ex)
        return _f32(gb_ref) * conv
    return prologue


def make_conv_epilogue(n_lat, sft):
    def epilogue(i, da, ex, outs, acc_ref):
        gb_ref, conv = _conv_value(i, n_lat, sft, ex)
        outs[0][...] = (da * _f32(gb_ref)).astype(BF16)
        outs[1][...] = (da * conv).astype(BF16)
    return epilogue


def _conv_specs(sft, t):
    specs = [_col_spec(0), _col_spec(1), _col_spec(2)]
    if sft != 1:
        per = TM // CH
        last = t // CH - 1
        for fn in (lambda i: jnp.maximum(i * per - 1, 0), lambda i: jnp.minimum(i * per + per, last)):
            for col in (1, 2):
                specs.append(pl.BlockSpec((CH, D), functools.partial(lambda i, f, c: (f(i), c), f=fn, c=col)))
    return specs + [_full_spec((8, D)), _full_spec((1, D))]


def _conv_args(sft, p, cw8, cb):
    return [p] * (3 if sft == 1 else 7) + [cw8, cb]


def conv_bwd(dconv, p, cw8, sft, n_lat, name):
    t = dconv.shape[0]
    halo = sft != 1

    def body(*refs):
        if halo:
            dc_ref, dcp_ref, dcn_ref, gc_ref, xi_ref, gcp_ref, xip_ref, gcn_ref, xin_ref, cw_ref, dgc_ref, dxi_ref, acc_ref = refs
            up, un = _f32(gcp_ref) * _f32(xip_ref), _f32(gcn_ref) * _f32(xin_ref)
            dcp, dcn = _f32(dcp_ref), _f32(dcn_ref)
        else:
            dc_ref, gc_ref, xi_ref, cw_ref, dgc_ref, dxi_ref, acc_ref = refs
            up = un = dcp = dcn = None
        i = pl.program_id(0)

        @pl.when(i == 0)
        def _():
            acc_ref[...] = jnp.zeros_like(acc_ref)

        gc = _f32(gc_ref)
        xi = _f32(xi_ref)
        u = gc * xi
        dc = _f32(dc_ref)
        u_prev, u_next = _shifts(i, n_lat, sft, u, up, un)
        dc_prev, dc_next = _shifts(i, n_lat, sft, dc, dcp, dcn)
        acc_ref[0:1, :] += jnp.sum(dc * u_prev, axis=0, keepdims=True)
        acc_ref[1:2, :] += jnp.sum(dc * u, axis=0, keepdims=True)
        acc_ref[2:3, :] += jnp.sum(dc * u_next, axis=0, keepdims=True)
        acc_ref[3:4, :] += jnp.sum(dc, axis=0, keepdims=True)
        du = cw_ref[0:1, :] * dc_next + cw_ref[1:2, :] * dc + cw_ref[2:3, :] * dc_prev
        dgc_ref[...] = (du * xi).astype(BF16)
        dxi_ref[...] = (du * gc).astype(BF16)

    per = TM // CH
    last = t // CH - 1
    prev_i = lambda i: jnp.maximum(i * per - 1, 0)
    next_i = lambda i: jnp.minimum(i * per + per, last)
    if halo:
        in_specs = [_row_spec(D), pl.BlockSpec((CH, D), lambda i: (prev_i(i), 0)), pl.BlockSpec((CH, D), lambda i: (next_i(i), 0)),
                    _col_spec(1), _col_spec(2),
                    pl.BlockSpec((CH, D), lambda i: (prev_i(i), 1)), pl.BlockSpec((CH, D), lambda i: (prev_i(i), 2)),
                    pl.BlockSpec((CH, D), lambda i: (next_i(i), 1)), pl.BlockSpec((CH, D), lambda i: (next_i(i), 2)),
                    _full_spec((8, D))]
        args = [dconv, dconv, dconv, p, p, p, p, p, p, cw8]
    else:
        in_specs = [_row_spec(D), _col_spec(1), _col_spec(2), _full_spec((8, D))]
        args = [dconv, p, p, cw8]
    return pl.pallas_call(
        body, name=name, grid=(t // TM,), in_specs=in_specs,
        out_specs=[_row_spec(D), _row_spec(D), _full_spec((8, D))],
        out_shape=[jax.ShapeDtypeStruct((t, D), BF16), jax.ShapeDtypeStruct((t, D), BF16), jax.ShapeDtypeStruct((8, D), F32)],
        compiler_params=_cp(1),
    )(*args)


LOG2E = 1.4426950408889634


def _cumsum_matrix(reverse):
    r = lax.broadcasted_iota(jnp.int32, (CH, CH), 0)
    c = lax.broadcasted_iota(jnp.int32, (CH, CH), 1)
    return (r <= c if reverse else r >= c).astype(BF16)


def _chunk_cumsum(g, tri):
    hi = g.astype(BF16)
    lo = (g - hi.astype(F32)).astype(BF16)
    return _dot(tri, hi) + _dot(tri, lo)


def _gate_values(z, lb):
    sig = _sigmoid(z)
    f = lb + (1.0 - lb) * sig
    return sig, f


def _tri(direction, transposed):
    r = lax.broadcasted_iota(jnp.int32, (CH, CH), 0)
    c = lax.broadcasted_iota(jnp.int32, (CH, CH), 1)
    lower = (direction == 0) != transposed
    return r >= c if lower else r <= c


def _gla_rows(direction):
    return (CH // 2 - 1, CH - 1) if direction == 0 else (CH // 2, 0)


def _fwd_tile(direction, nt):
    return (lambda i: (i + nt - 1) % nt) if direction == 0 else (lambda i: nt - 1 - i)


def gla_fwd(p, lb2, name):
    t = p.shape[0]
    nt = t // TM
    per = TM // CH

    def body(z0_ref, v0_ref, q0_ref, z1_ref, v1_ref, q1_ref, lb_ref, o0_ref, s0_ref, o1_ref, s1_ref, st, q_s, k_s, c_s):
        @pl.when(pl.program_id(0) == 0)
        def _():
            st[...] = jnp.zeros_like(st)

        ins = ((z0_ref, v0_ref, q0_ref, o0_ref, s0_ref), (z1_ref, v1_ref, q1_ref, o1_ref, s1_ref))
        for d in range(2):
            z_ref, _, qr_ref, _, _ = ins[d]
            tri = _cumsum_matrix(d == 1)
            lb = lb_ref[d:d + 1, :]
            for ci in range(per):
                rows = slice(ci * CH, (ci + 1) * CH)
                _, f = _gate_values(z_ref[rows, :], lb)
                k_s[d, rows, :] = 1.0 - f
                c_s[d, rows, :] = _chunk_cumsum(jnp.log(f) * LOG2E, tri)
                qr = qr_ref[rows, :]
                q_s[d, rows, :] = qr * _sigmoid(qr)
        masks = (_tri(0, False), _tri(1, False))
        state = [[st[d, h] for h in range(NH)] for d in range(2)]
        for it in range(per):
            chunk = []
            for d in range(2):
                ref_row, last_row = _gla_rows(d)
                ci = it if d == 0 else per - 1 - it
                r0 = ci * CH
                rows = slice(r0, r0 + CH)
                cum = c_s[d, rows, :]
                ref = c_s[d, r0 + ref_row:r0 + ref_row + 1, :]
                last = c_s[d, r0 + last_row:r0 + last_row + 1, :]
                q = q_s[d, rows, :]
                k = k_s[d, rows, :]
                chunk.append(dict(
                    ci=ci, rows=rows, qh=(q * jnp.exp2(cum)).astype(BF16), qt=(q * jnp.exp2(cum - ref)).astype(BF16),
                    kt=(k * jnp.exp2(ref - cum)).astype(BF16), kb=(k * jnp.exp2(last - cum)).astype(BF16),
                    el=jnp.exp2(last), vb=ins[d][1][rows, :].astype(BF16)))
            heads = [slice(h * HD, (h + 1) * HD) for h in range(NH)]
            first = {}
            for h, sl in enumerate(heads):
                for d in range(2):
                    c = chunk[d]
                    s_t = state[d][h]
                    ins[d][4][c["ci"], h] = s_t
                    first[d, h] = (jnp.where(masks[d], _dot_nt(c["qt"][:, sl], c["kt"][:, sl]), 0.0).astype(BF16),
                                   _dot_nt(c["qh"][:, sl], s_t.astype(BF16)),
                                   s_t * c["el"][:, sl] + _dot_tn(c["vb"][:, sl], c["kb"][:, sl]))
            for h, sl in enumerate(heads):
                for d in range(2):
                    c = chunk[d]
                    sc, o_inter, s_new = first[d, h]
                    ins[d][3][c["rows"], sl] = (o_inter + _dot(sc, c["vb"][:, sl])).astype(BF16)
                    state[d][h] = s_new
        for d in range(2):
            for h in range(NH):
                st[d, h] = state[d][h]

    tiles = (_fwd_tile(0, nt), _fwd_tile(1, nt))
    tspec = lambda d, col: pl.BlockSpec((TM, D), lambda i: (tiles[d](i), col))
    sspec = lambda d: pl.BlockSpec((per, NH, HD, HD), lambda i: (tiles[d](i), 0, 0, 0))
    o_shape = jax.ShapeDtypeStruct((t, D), BF16)
    s_shape = jax.ShapeDtypeStruct((t // CH, NH, HD, HD), F32)
    return pl.pallas_call(
        body, name=name, grid=(nt,),
        in_specs=[tspec(0, 0), tspec(0, 2), tspec(0, 3), tspec(1, 1), tspec(1, 2), tspec(1, 3), _full_spec((2, D))],
        out_specs=[tspec(0, 0), sspec(0), tspec(1, 0), sspec(1)],
        out_shape=[o_shape, s_shape, o_shape, s_shape],
        scratch_shapes=[pltpu.VMEM((2, NH, HD, HD), F32)] + [pltpu.VMEM((2, TM, D), F32)] * 3,
        compiler_params=_cp(1),
    )(p, p, p, p, p, p, lb2)


def gla_bwd(p, lb2, do, states, direction, prev, name):
    t = p.shape[0]
    nt = t // TM
    per = TM // CH
    ref_row, last_row = _gla_rows(direction)
    tile = (lambda i: (2 * nt - 2 - i) % nt) if direction == 0 else (lambda i: i)
    final = prev is not None
    n_in = 8 if final else 6

    def body(*refs):
        z_ref, v_ref, qr_ref, lb_ref, do_ref, s_ref = refs[:6]
        dz_ref, dv_ref, dq_ref, acc_ref, dst, q_s, k_s, c_s, dq_s, dk_s, f_s, sg_s, dl_s = refs[n_in:]

        @pl.when(pl.program_id(0) == 0)
        def _():
            dst[...] = jnp.zeros_like(dst)
            acc_ref[...] = jnp.zeros_like(acc_ref)

        mask = _tri(direction, False)
        mask_t = _tri(direction, True)
        tri = _cumsum_matrix(direction == 1)
        tri_t = _cumsum_matrix(direction == 0)
        is_last = lax.broadcasted_iota(jnp.int32, (CH, 1), 0) == last_row
        lb = lb_ref[direction:direction + 1, :]
        for ci in range(per):
            rows = slice(ci * CH, (ci + 1) * CH)
            sig, f = _gate_values(z_ref[rows, :], lb)
            f_s[rows, :] = f
            sg_s[rows, :] = sig
            k_s[rows, :] = 1.0 - f
            c_s[rows, :] = _chunk_cumsum(jnp.log(f) * LOG2E, tri)
            qr = qr_ref[rows, :]
            q_s[rows, :] = qr * _sigmoid(qr)
        state = [dst[h] for h in range(NH)]
        for it in range(per):
            ci = per - 1 - it if direction == 0 else it
            r0 = ci * CH
            rows = slice(r0, r0 + CH)
            cum = c_s[rows, :]
            ref = c_s[r0 + ref_row:r0 + ref_row + 1, :]
            last = c_s[r0 + last_row:r0 + last_row + 1, :]
            q = q_s[rows, :]
            k = k_s[rows, :]
            e_h = jnp.exp2(cum)
            e_t = jnp.exp2(cum - ref)
            e_kt = jnp.exp2(ref - cum)
            e_kb = jnp.exp2(last - cum)
            el = jnp.exp2(last)
            qh = (q * e_h).astype(BF16)
            qt = (q * e_t).astype(BF16)
            kt = (k * e_kt).astype(BF16)
            kbf = k * e_kb
            kb = kbf.astype(BF16)
            vb = v_ref[rows, :].astype(BF16)
            dob = do_ref[rows, :].astype(BF16)
            heads = [slice(h * HD, (h + 1) * HD) for h in range(NH)]
            first = []
            for h, sl in enumerate(heads):
                s_t = s_ref[ci, h]
                ds_t = state[h]
                ds_b = ds_t.astype(BF16)
                d_a = jnp.where(mask, _dot_nt(dob[:, sl], vb[:, sl]), 0.0).astype(BF16)
                a_t = jnp.where(mask_t, _dot_nt(kt[:, sl], qt[:, sl]), 0.0).astype(BF16)
                d_at = jnp.where(mask_t, _dot_nt(vb[:, sl], dob[:, sl]), 0.0).astype(BF16)
                dkb = _dot(vb[:, sl], ds_b)
                dl_s[it:it + 1, sl] = (el[:, sl] * jnp.sum(ds_t * s_t, axis=0, keepdims=True)
                                       + jnp.sum(dkb * kbf[:, sl], axis=0, keepdims=True))
                state[h] = ds_t * el[:, sl] + _dot_tn(dob[:, sl], qh[:, sl])
                first.append((d_a, a_t, d_at, dkb, _dot_nt(kb[:, sl], ds_b), _dot(dob[:, sl], s_t.astype(BF16))))
            for h, sl in enumerate(heads):
                d_a, a_t, d_at, dkb, dv_state, dq_state = first[h]
                dv = _dot(a_t, dob[:, sl]) + dv_state
                dq_s[rows, sl] = dq_state * e_h[:, sl] + _dot(d_a, kt[:, sl]) * e_t[:, sl]
                dk_s[rows, sl] = _dot(d_at, qt[:, sl]) * e_kt[:, sl] + dkb * e_kb[:, sl]
                if final:
                    dv_ref[rows, sl] = (refs[6][rows, sl] + dv).astype(BF16)
                else:
                    dv_ref[rows, sl] = dv
        for h in range(NH):
            dst[h] = state[h]
        for it in range(per):
            ci = per - 1 - it if direction == 0 else it
            rows = slice(ci * CH, (ci + 1) * CH)
            dq = dq_s[rows, :]
            dk = dk_s[rows, :]
            dg = _chunk_cumsum(dq * q_s[rows, :] - dk * k_s[rows, :] + jnp.where(is_last, dl_s[it:it + 1, :], 0.0), tri_t)
            sig = sg_s[rows, :]
            f = f_s[rows, :]
            df = dg / f - dk
            acc_ref[0:1, :] += jnp.sum(df * (1.0 - sig), axis=0, keepdims=True)
            dz_ref[rows, :] = (df * (1.0 - lb) * sig * (1.0 - sig)).astype(BF16)
            if final:
                qr = qr_ref[rows, :]
                sq = _sigmoid(qr)
                dq_ref[rows, :] = ((refs[7][rows, :] + dq) * (sq * (1.0 + qr * (1.0 - sq)))).astype(BF16)
            else:
                dq_ref[rows, :] = dq

    tspec = lambda col: pl.BlockSpec((TM, D), lambda i: (tile(i), col))
    sspec = pl.BlockSpec((per, NH, HD, HD), lambda i: (tile(i), 0, 0, 0))
    in_specs = [tspec(direction), tspec(2), tspec(3), _full_spec((2, D)), tspec(0), sspec]
    args = [p, p, p, lb2, do, states]
    if final:
        in_specs += [tspec(0), tspec(0)]
        args += list(prev)
    odt = BF16 if final else F32
    return pl.pallas_call(
        body, name=name, grid=(nt,), in_specs=in_specs,
        out_specs=[tspec(0), tspec(0), tspec(0), _full_spec((8, D))],
        out_shape=[jax.ShapeDtypeStruct((t, D), BF16), jax.ShapeDtypeStruct((t, D), odt), jax.ShapeDtypeStruct((t, D), odt),
                   jax.ShapeDtypeStruct((8, D), F32)],
        scratch_shapes=[pltpu.VMEM((NH, HD, HD), F32)] + [pltpu.VMEM((TM, D), F32)] * 7 + [pltpu.VMEM((8, D), F32)],
        compiler_params=_cp(1),
    )(*args)


def loss_bwd(x, gain, target, n_lat, name):
    t = x.shape[0]

    def body(x_ref, gain_ref, tg_ref, dx_ref, acc_ref):
        i = pl.program_id(0)

        @pl.when(i == 0)
        def _():
            acc_ref[...] = jnp.zeros_like(acc_ref)

        latf = (i < n_lat).astype(F32)
        x = x_ref[...]
        gain = gain_ref[...]
        r = lax.rsqrt(jnp.mean(x * x, axis=-1, keepdims=True) + EPS)
        xn = x * r
        err = (xn * gain - tg_ref[...]) * latf
        dy = err * (1.0 / D)
        dxn = dy * gain
        dx_ref[...] = r * (dxn - xn * jnp.mean(dxn * xn, axis=-1, keepdims=True))
        acc_ref[0:1, :] += jnp.sum(dy * xn, axis=0, keepdims=True)
        acc_ref[1:2, :] += jnp.sum(err * err, axis=0, keepdims=True)

    return pl.pallas_call(
        body, name=name, grid=(t // TM,),
        in_specs=[_row_spec(D), _full_spec((1, D)), pl.BlockSpec((TM, D), lambda i: (jnp.minimum(i, n_lat - 1), 0))],
        out_specs=[_row_spec(D), _full_spec((8, D))],
        out_shape=[jax.ShapeDtypeStruct((t, D), F32), jax.ShapeDtypeStruct((8, D), F32)],
        compiler_params=_cp(1),
    )(x, gain, target)


def local_step(xs, target, mods, norm1, norm2, norm_f, lbs, gnorm, cw8, cb, wts, n_lat, on_grads, after_backward):
    t = xs.shape[0]
    saved = []
    cache = {}

    def W(name, idx, after=None):
        if (name, idx) not in cache:
            cache[(name, idx)] = wts(name, idx, after)
        return cache[(name, idx)]

    x = xs
    for i in range(DEPTH):
        j = i // 2
        rec = i % 2 == 0
        n1 = norm1[i:i + 1]
        n2 = norm2[i:i + 1]
        s = {"x_in": x}
        if rec:
            p = proj_fwd(x, n1, mods[i], 0, W("hin", j, x), n_lat, F32, f"hin_fwd_{i}")
            o0, st0, o1, st1 = gla_fwd(p, lbs[j], f"gla_fwd_{i}")
            ex = [o0, o1, p, gnorm[j:j + 1]]
            ex_specs = [_row_spec(D), _row_spec(D), _col_spec(4), _full_spec((1, D))]
            xm, y, ab = outproj_fwd(readout_prologue, ex, ex_specs, W("hout", j, o1), x, mods[i], 0, n_lat, f"hout_fwd_{i}")
            s.update(st0=st0, st1=st1)
        else:
            sft = 1 if j % 2 == 0 else CH
            p = proj_fwd(x, n1, mods[i], 0, W("cin", j, x), n_lat, BF16, f"cin_fwd_{i}")
            ex = _conv_args(sft, p, cw8[j], cb[j])
            ex_specs = _conv_specs(sft, t)
            xm, y, ab = outproj_fwd(make_conv_prologue(n_lat, sft), ex, ex_specs, W("cout", j, p), x, mods[i], 0, n_lat, f"cout_fwd_{i}")
        s.update(p=p, ex=ex, ex_specs=ex_specs, y_mix=y, ab_mix=ab, x_mid=xm)
        x, y2, ab2 = mlp_fwd(xm, n2, mods[i], W("w1", i, xm), W("w2", i, xm), n_lat, f"mlp_fwd_{i}")
        s.update(y_mlp=y2, ab_mlp=ab2)
        saved.append(s)

    dx, acc_loss = loss_bwd(x, norm_f, target, n_lat, "loss")
    small = {"norm_f": acc_loss[0:1], "norm1": [None] * DEPTH, "norm2": [None] * DEPTH, "dmod": [None] * DEPTH,
             "gnorm": [None] * 2, "lb": [None] * 2, "cw": [None] * 2, "cb": [None] * 2}
    bshape = lambda w: jax.ShapeDtypeStruct((t, w), BF16)
    token = jnp.zeros((8, 128), F32)
    for i in reversed(range(DEPTH)):
        j = i // 2
        rec = i % 2 == 0
        s = saved[i]
        n1 = norm1[i:i + 1]
        n2 = norm2[i:i + 1]
        dx, dyb, dp1, hb, acc_n2 = mlp_bwd(dx, s["y_mlp"], s["ab_mlp"], s["x_mid"], n2, mods[i], W("w1", i), W("w2", i), n_lat, token,
                                           f"mlp_bwd_{i}")
        token = on_grads(i, "mlp", {"w2": dw_tn(s["ab_mlp"], dyb, 4, True, True, token, f"w2_dw_{i}"),
                                    "w1": dw_tn(hb, dp1, 4, False, False, token, f"w1_dw_{i}")})
        if rec:
            dyb, acc_g1, (do, dgate) = outproj_bwd(
                readout_epilogue, s["ex"], s["ex_specs"], [_row_spec(D), _row_spec(D)],
                [jax.ShapeDtypeStruct((t, D), F32), bshape(D)], W("hout", j), dx, s["y_mix"], mods[i], 0, n_lat, token, f"hout_bwd_{i}")
            dz0, dv0, dq0, acc_l0 = gla_bwd(s["p"], lbs[j], do, s["st0"], 0, None, f"gla_bwd0_{i}")
            dz1, dv, dq, acc_l1 = gla_bwd(s["p"], lbs[j], do, s["st1"], 1, (dv0, dq0), f"gla_bwd1_{i}")
            dx, hb, dpb, acc_n1 = proj_bwd([dz0, dz1, dv, dq, dgate], W("hin", j), s["x_in"], n1, mods[i], 0, dx, n_lat, i == 0,
                                           f"hin_bwd_{i}")
            small["gnorm"][j] = acc_g1[2:3]
            small["lb"][j] = jnp.concatenate([acc_l0[0:1], acc_l1[0:1]], axis=0)
            mix = ("hout", "hin")
        else:
            sft = 1 if j % 2 == 0 else CH
            dyb, acc_g1, (dconv, dgb) = outproj_bwd(
                make_conv_epilogue(n_lat, sft), s["ex"], s["ex_specs"], [_row_spec(D), _row_spec(D)],
                [bshape(D), bshape(D)], W("cout", j), dx, s["y_mix"], mods[i], 0, n_lat, token, f"cout_bwd_{i}")
            dgc, dxi, acc_c = conv_bwd(dconv, s["p"], cw8[j], sft, n_lat, f"conv_bwd_{i}")
            dx, hb, dpb, acc_n1 = proj_bwd([dgb, dgc, dxi], W("cin", j), s["x_in"], n1, mods[i], 0, dx, n_lat, False, f"cin_bwd_{i}")
            small["cw"][j] = acc_c[0:3]
            small["cb"][j] = acc_c[3:4]
            mix = ("cout", "cin")
        small["norm1"][i] = acc_n1[0:1]
        small["norm2"][i] = acc_n2[0:1]
        z2 = jnp.zeros((2, D), F32)
        small["dmod"][i] = jnp.concatenate([acc_n1[1:3], acc_g1[0:1], acc_n2[1:3], acc_n2[5:6], z2,
                                            acc_n1[3:5], acc_g1[1:2], acc_n2[3:5], acc_n2[6:7], z2], axis=0)
        if i == 0:
            token = after_backward(small, token)
        token = on_grads(i, "mix", {mix[0]: dw_tn(s["ab_mix"], dyb, 1, False, False, token, f"{mix[0]}_dw_{i}"),
                                    mix[1]: dw_tn(hb, dpb, 4, False, False, token, f"{mix[1]}_dw_{i}")})
    return acc_loss[1:2], dx, token


RB = 256


def cast_to_slot(w2d, layer, k, chip1, name):
    c = w2d.shape[1]
    nblk = k // RB

    def body(chip_ref, w_ref, o_ref):
        o_ref[0] = w_ref[...].astype(BF16)

    return pl.pallas_call(
        body, name=name,
        grid_spec=pltpu.PrefetchScalarGridSpec(
            num_scalar_prefetch=1, grid=(nblk,),
            in_specs=[pl.BlockSpec((RB, c), lambda i, ch: (layer * nblk + i, 0))],
            out_specs=pl.BlockSpec((1, RB, c), lambda i, ch: (ch[0], i, 0))),
        out_shape=jax.ShapeDtypeStruct((4, k, c), BF16), compiler_params=_cp(1))(chip1, w2d)


def sum_slots(own, land, acc, layer, chip1, name):
    _, k, c = own.shape
    nblk = k // RB

    def body(chip_ref, own_ref, l1_ref, l2_ref, l3_ref, acc_ref, o_ref):
        o_ref[...] = ((own_ref[0].astype(F32) + l1_ref[0].astype(F32)) + l2_ref[0].astype(F32)) + l3_ref[0].astype(F32)

    slot = lambda d: pl.BlockSpec((1, RB, c), lambda i, ch: ((ch[0] + d) % 4, i, 0))
    return pl.pallas_call(
        body, name=name,
        grid_spec=pltpu.PrefetchScalarGridSpec(
            num_scalar_prefetch=1, grid=(nblk,),
            in_specs=[slot(0), slot(1), slot(2), slot(3), ANY],
            out_specs=pl.BlockSpec((RB, c), lambda i, ch: (layer * nblk + i, 0))),
        out_shape=jax.ShapeDtypeStruct(acc.shape, F32), input_output_aliases={5: 0}, compiler_params=_cp(1),
    )(chip1, own, land, land, land, acc)


def _adamw_math(w, g, m, v):
    m = ADAM_B1 * m + (1.0 - ADAM_B1) * g
    v = ADAM_B2 * v + (1.0 - ADAM_B2) * jnp.square(g)
    m_hat = m / (1.0 - ADAM_B1 ** ADAM_STEP)
    v_hat = v / (1.0 - ADAM_B2 ** ADAM_STEP)
    delta = -ADAM_LR * (m_hat / (jnp.sqrt(v_hat) + ADAM_EPS) + ADAM_WD * w)
    return delta, m, v


def adamw(gsrcs, w, m, v, name):
    r, c = w.shape
    rb = RB if r % RB == 0 else r
    n_g = len(gsrcs)

    def body(*refs):
        g = refs[0][...]
        for k in range(1, n_g):
            g = g + refs[k][...]
        w_ref, m_ref, v_ref, g_ref, d_ref, mo_ref, vo_ref = refs[n_g:]
        delta, mo, vo = _adamw_math(w_ref[...], g, m_ref[...], v_ref[...])
        g_ref[...] = g
        d_ref[...] = delta
        mo_ref[...] = mo
        vo_ref[...] = vo

    spec = pl.BlockSpec((rb, c), lambda i: (i, 0))
    shp = jax.ShapeDtypeStruct((r, c), F32)
    return pl.pallas_call(body, name=name, grid=(r // rb,), in_specs=[spec] * (n_g + 3), out_specs=[spec] * 4,
                          out_shape=[shp] * 4, compiler_params=_cp(1))(*gsrcs, w, m, v)


ADA_CB = 512


def ada_fwd(cvec, ada_w, bias, name):
    _, _, nc = ada_w.shape

    def body(c_ref, w_ref, b_ref, o_ref):
        cv = c_ref[...]
        a = (cv * _sigmoid(cv)).astype(BF16)
        o_ref[0] = _dot(a, w_ref[0].astype(BF16)) + b_ref[0]

    return pl.pallas_call(
        body, name=name, grid=(DEPTH, nc // ADA_CB),
        in_specs=[pl.BlockSpec((16, D), lambda i, j: (0, 0)), pl.BlockSpec((1, D, ADA_CB), lambda i, j: (i, 0, j)),
                  pl.BlockSpec((1, 1, ADA_CB), lambda i, j: (i, 0, j))],
        out_specs=pl.BlockSpec((1, 16, ADA_CB), lambda i, j: (i, 0, j)),
        out_shape=jax.ShapeDtypeStruct((DEPTH, 16, nc), F32), compiler_params=_cp(2),
    )(cvec, ada_w, bias)


def ada_bwd(cvec, dcols, ada_w, m, v, name):
    _, _, nc = ada_w.shape

    def body(c_ref, d_ref, w_ref, m_ref, v_ref, g_ref, dl_ref, mo_ref, vo_ref, acc_ref):
        @pl.when(jnp.logical_and(pl.program_id(0) == 0, pl.program_id(1) == 0))
        def _():
            acc_ref[...] = jnp.zeros_like(acc_ref)

        cv = c_ref[...]
        a = (cv * _sigmoid(cv)).astype(BF16)
        db = d_ref[0].astype(BF16)
        w = w_ref[0]
        g = _dot_tn(a, db)
        delta, mo, vo = _adamw_math(w, g, m_ref[0], v_ref[0])
        g_ref[0] = g
        dl_ref[0] = delta
        mo_ref[0] = mo
        vo_ref[0] = vo
        acc_ref[...] += _dot_nt(db[8:16, :], w.astype(BF16))

    wspec = pl.BlockSpec((1, D, ADA_CB), lambda i, j: (i, 0, j))
    wshape = jax.ShapeDtypeStruct(ada_w.shape, F32)
    return pl.pallas_call(
        body, name=name, grid=(DEPTH, nc // ADA_CB),
        in_specs=[pl.BlockSpec((16, D), lambda i, j: (0, 0)), pl.BlockSpec((1, 16, ADA_CB), lambda i, j: (i, 0, j)), wspec, wspec, wspec],
        out_specs=[wspec, wspec, wspec, wspec, pl.BlockSpec((8, D), lambda i, j: (0, 0))],
        out_shape=[wshape, wshape, wshape, wshape, jax.ShapeDtypeStruct((8, D), F32)], compiler_params=_cp(2),
    )(cvec, dcols, ada_w, m, v)


def _place():
    return lax.axis_index("x"), lax.axis_index("y"), lax.axis_index("c")


ANY = pl.BlockSpec(memory_space=pl.ANY)
VMEM_SPEC = pl.BlockSpec(memory_space=pltpu.VMEM)


def small_allgather(buf, deps, name):
    r, c = buf.shape
    n_dep = len(deps)

    def body(*refs):
        in_ref = refs[0]
        out_ref, send_sems, recv_sems, loc_sem = refs[1 + n_dep:]
        x, y, cc = _place()
        me = 4 * x + 2 * y + cc
        loc = pltpu.make_async_copy(in_ref, out_ref.at[me], loc_sem)
        loc.start()
        peers = []
        for k in range(1, 8):
            px = 1 - x if k & 4 else x
            py = 1 - y if k & 2 else y
            pc = 1 - cc if k & 1 else cc
            peers.append((px, py, pc))
        sends = []
        for k, peer in enumerate(peers):
            cp = pltpu.make_async_remote_copy(src_ref=in_ref, dst_ref=out_ref.at[me], send_sem=send_sems.at[k],
                                              recv_sem=recv_sems.at[k], device_id=peer, device_id_type=MESH)
            cp.start()
            sends.append(cp)
        for k, (px, py, pc) in enumerate(peers):
            pltpu.make_async_remote_copy(src_ref=in_ref, dst_ref=out_ref.at[4 * px + 2 * py + pc], send_sem=send_sems.at[k],
                                         recv_sem=recv_sems.at[k], device_id=(px, py, pc), device_id_type=MESH).wait_recv()
        for cp in sends:
            cp.wait_send()
        loc.wait()

    return pl.pallas_call(
        body, name=name, in_specs=[VMEM_SPEC] + [ANY] * n_dep, out_specs=VMEM_SPEC,
        out_shape=jax.ShapeDtypeStruct((8, r, c), buf.dtype),
        scratch_shapes=[pltpu.SemaphoreType.DMA((7,)), pltpu.SemaphoreType.DMA((7,)), pltpu.SemaphoreType.DMA],
    )(buf, *deps)


def _chip_peers(x, y):
    return [(1 - x, y), (x, 1 - y), (1 - x, 1 - y)]


HBM_SPEC = pl.BlockSpec(memory_space=pltpu.HBM)
SEM_SPEC = pl.BlockSpec(memory_space=pltpu.SEMAPHORE)
EFFECT = pltpu.SideEffectType.DATAFLOW_SIDE_EFFECTING


def _hbm(a):
    return pltpu.with_memory_space_constraint(a, pltpu.HBM)


def _split_copy(u, p, peer, dst_slot, chip, land_refs, src_refs, sem_refs, cc):
    px, py = peer
    src = land_refs[u].at[chip] if src_refs is None else src_refs[u].at[2 * px + py]
    return pltpu.make_async_remote_copy(src_ref=src, dst_ref=land_refs[u].at[dst_slot], send_sem=sem_refs[2 * u].at[p],
                                        recv_sem=sem_refs[2 * u + 1].at[p], device_id=(px, py, cc), device_id_type=MESH)


def split_start(lands, srcs, after, name):
    n = len(lands)
    ops = list(lands) + (list(srcs) if srcs is not None else [])
    n_ops = len(ops)

    def body(*refs):
        land_refs = refs[:n]
        src_refs = refs[n:n_ops] if srcs is not None else None
        sem_refs = refs[n_ops + 1:n_ops + 1 + 2 * n]
        x, y, cc = _place()
        chip = 2 * x + y
        for u in range(n):
            for p, peer in enumerate(_chip_peers(x, y)):
                _split_copy(u, p, peer, chip, chip, land_refs, src_refs, sem_refs, cc).start()
        refs[-1][...] = jnp.zeros((8, 128), F32)

    outs = pl.pallas_call(
        body, name=name, in_specs=[HBM_SPEC] * n_ops + [ANY],
        out_specs=[SEM_SPEC] * (2 * n) + [HBM_SPEC] * n_ops + [VMEM_SPEC],
        out_shape=[pltpu.SemaphoreType.DMA((3,))] * (2 * n) + [pltpu.HBM(a.shape, a.dtype) for a in ops]
        + [jax.ShapeDtypeStruct((8, 128), F32)],
        input_output_aliases={k: 2 * n + k for k in range(n_ops)},
        compiler_params=pltpu.CompilerParams(has_side_effects=EFFECT),
    )(*[_hbm(a) for a in ops], after)
    sems = list(outs[:2 * n])
    thru = list(outs[2 * n:2 * n + n_ops])
    return sems, thru[:n], thru[n:], outs[-1]


def split_wait(lands, srcs, sems, after, name):
    n = len(lands)
    ops = list(lands) + (list(srcs) if srcs is not None else [])
    n_ops = len(ops)

    def body(*refs):
        land_refs = refs[:n]
        src_refs = refs[n:n_ops] if srcs is not None else None
        sem_refs = refs[n_ops:n_ops + 2 * n]
        x, y, cc = _place()
        chip = 2 * x + y
        for u in range(n):
            for p, peer in enumerate(_chip_peers(x, y)):
                cp = _split_copy(u, p, peer, 2 * peer[0] + peer[1], chip, land_refs, src_refs, sem_refs, cc)
                cp.wait_send()
                cp.wait_recv()

    outs = pl.pallas_call(
        body, name=name, in_specs=[HBM_SPEC] * n_ops + [SEM_SPEC] * (2 * n) + [ANY],
        out_specs=[HBM_SPEC] * n_ops, out_shape=[pltpu.HBM(a.shape, a.dtype) for a in ops],
        input_output_aliases={k: k for k in range(n_ops)},
        compiler_params=pltpu.CompilerParams(has_side_effects=EFFECT),
    )(*ops, *sems, after)
    return list(outs[:n]), list(outs[n:])


def _sibling_copy(k, src_refs, zone_refs, sem_refs):
    x, y, cc = _place()
    return pltpu.make_async_remote_copy(src_ref=src_refs[k], dst_ref=zone_refs[k], send_sem=sem_refs[2 * k], recv_sem=sem_refs[2 * k + 1],
                                        device_id=(x, y, 1 - cc), device_id_type=MESH)


def sibling_start(parts, name):
    n = len(parts)
    ops = list(parts) + [lax.empty(p.shape, p.dtype) for p in parts]

    def body(*refs):
        for k in range(n):
            _sibling_copy(k, refs[:n], refs[n:2 * n], refs[2 * n:4 * n]).start()

    outs = pl.pallas_call(
        body, name=name, in_specs=[HBM_SPEC] * (2 * n),
        out_specs=[SEM_SPEC] * (2 * n) + [HBM_SPEC] * (2 * n),
        out_shape=[pltpu.SemaphoreType.DMA(())] * (2 * n) + [pltpu.HBM(a.shape, a.dtype) for a in ops],
        input_output_aliases={k: 2 * n + k for k in range(2 * n)},
        compiler_params=pltpu.CompilerParams(has_side_effects=EFFECT),
    )(*[_hbm(a) for a in ops])
    return list(outs[2 * n:3 * n]), list(outs[3 * n:]), list(outs[:2 * n])


def sibling_wait(parts, zones, sems, after, name):
    n = len(parts)

    def body(*refs):
        for k in range(n):
            cp = _sibling_copy(k, refs[:n], refs[n:2 * n], refs[2 * n:4 * n])
            cp.wait_send()
            cp.wait_recv()

    outs = pl.pallas_call(
        body, name=name, in_specs=[HBM_SPEC] * (2 * n) + [SEM_SPEC] * (2 * n) + [ANY],
        out_specs=[HBM_SPEC] * (2 * n), out_shape=[pltpu.HBM(a.shape, a.dtype) for a in list(parts) + list(zones)],
        input_output_aliases={k: k for k in range(2 * n)},
        compiler_params=pltpu.CompilerParams(has_side_effects=EFFECT),
    )(*parts, *zones, *sems, after)
    return list(outs[:n]), list(outs[n:])


SMALL_ROWS = 88
FIN_ROWS = 72


def small_finish(g3, g4, c_ctx, lbp, name):
    def body(g3_ref, g4_ref, cc_ref, lbp_ref, o_ref, s_ref):
        s = g3_ref[0]
        for k in range(1, 8):
            s = s + g3_ref[k]
        s_ref[...] = s
        for i in range(DEPTH):
            o_ref[8 * i:8 * i + 8, :] = s_ref[16 * i:16 * i + 8, :] + s_ref[16 * i + 8:16 * i + 16, :]
        acc = g4_ref[0]
        for k in (2, 4, 6):
            acc = acc + g4_ref[k]
        cc = cc_ref[...]
        sg = _sigmoid(cc)
        row = jnp.sum(acc, axis=0, keepdims=True) * (sg * (1.0 + cc * (1.0 - sg)))
        o_ref[32:40, :] = jnp.broadcast_to(row, (8, D))
        o_ref[40:64, :] = s_ref[64:88, :]
        o_ref[64:72, :] = jnp.zeros((8, D), F32)
        for d in range(2):
            pp = lbp_ref[2 * d:2 * d + 1, :] * lbp_ref[2 * d + 1:2 * d + 2, :] * s_ref[75 + d:76 + d, :]
            o_ref[64 + 2 * d:65 + 2 * d, :] = -pp
            o_ref[65 + 2 * d:66 + 2 * d, :] = pp

    return pl.pallas_call(
        body, name=name, in_specs=[VMEM_SPEC] * 4, out_specs=VMEM_SPEC,
        out_shape=jax.ShapeDtypeStruct((FIN_ROWS, D), F32),
        scratch_shapes=[pltpu.VMEM((SMALL_ROWS, D), F32)],
    )(g3, g4, c_ctx, lbp)


def _pack_rows(arrs):
    flat = jnp.concatenate([a.reshape(-1) for a in arrs])
    n = -(-flat.shape[0] // (8 * D)) * 8 * D
    return jnp.pad(flat, (0, n - flat.shape[0])).reshape(n // D, D)


def _unpack_rows(packed, shapes):
    flat = packed.reshape(-1)
    outs, off = [], 0
    for s in shapes:
        size = 1
        for k in s:
            size *= k
        outs.append(flat[off:off + size].reshape(s))
        off += size
    return outs


def _pad8(a):
    return jnp.pad(a, ((0, 8 - a.shape[0]), (0, 0)))


def kernel(x, c, ctx, c_ctx, ada_w, ada_b, norm1, norm2, norm_f, mlp_w1, mlp_w2, hgrn_w_in, hgrn_lb, hgrn_gnorm, hgrn_w_out, conv_w_in, conv_w, conv_b, conv_w_out, loss_target, m_c_ctx, m_ada_w, m_ada_b, m_norm1, m_norm2, m_norm_f, m_mlp_w1, m_mlp_w2, m_hgrn_w_in, m_hgrn_lb, m_hgrn_gnorm, m_hgrn_w_out, m_conv_w_in, m_conv_w, m_conv_b, m_conv_w_out, v_c_ctx, v_ada_w, v_ada_b, v_norm1, v_norm2, v_norm_f, v_mlp_w1, v_mlp_w2, v_hgrn_w_in, v_hgrn_lb, v_hgrn_gnorm, v_hgrn_w_out, v_conv_w_in, v_conv_w, v_conv_b, v_conv_w_out):
    xi, yi, ci = _place()
    me = 4 * xi + 2 * yi + ci
    chip = 2 * xi + yi
    seq = x.shape[1]
    assert ctx.shape[1] == TM and seq % TM == 0 and (seq + TM) % TMW == 0
    n_lat = seq // TM
    sd = D // 4
    nca = ada_w.shape[2]
    xs = jnp.concatenate([x[0], ctx[0]], axis=0)

    big = [(mlp_w1, m_mlp_w1, v_mlp_w1), (mlp_w2, m_mlp_w2, v_mlp_w2), (hgrn_w_in, m_hgrn_w_in, v_hgrn_w_in),
           (hgrn_w_out, m_hgrn_w_out, v_hgrn_w_out), (conv_w_in, m_conv_w_in, v_conv_w_in), (conv_w_out, m_conv_w_out, v_conv_w_out)]
    big_names = ["w1", "w2", "hin", "hout", "cin", "cout"]
    flat2 = lambda a: a.reshape(a.shape[0] * a.shape[1], a.shape[2])
    tensors = dict(zip(big_names, big))
    chip1 = jnp.reshape(chip, (1,)).astype(jnp.int32)
    order = []
    for i in range(DEPTH):
        order += [("hin", i // 2), ("hout", i // 2)] if i % 2 == 0 else [("cin", i // 2), ("cout", i // 2)]
        order += [("w1", i), ("w2", i)]
    lands = [cast_to_slot(flat2(tensors[n][0]), idx, tensors[n][0].shape[1], chip1, f"cast_{n}_{idx}") for n, idx in order]
    sh_rows = jnp.concatenate([hgrn_lb.reshape(4, sd), conv_w.reshape(6, sd), conv_b.reshape(2, sd)], axis=0)
    buf1 = jnp.concatenate([c, jnp.pad(sh_rows, ((0, 0), (0, D - sd))), jnp.zeros((3, D), F32)], axis=0)
    g1 = small_allgather(buf1, [], "gather_small_in")
    first_sems, first_lands, _, first_token = split_start(lands[:1], None, g1, "gather_start_first")
    cvec = jnp.concatenate([g1[:, 0, :], jnp.broadcast_to(c_ctx[None], (8, D))], axis=0)
    shf = g1[0::2, 1:13, :sd].transpose(1, 0, 2).reshape(12, D)
    lb_p = jax.nn.softmax(shf[0:4].reshape(2, 2, D), axis=1)
    lower = jnp.cumsum(lb_p, axis=1) - lb_p[:, :1]
    lbs = [lower[:, 0], lower[:, 1]]
    cw8 = [_pad8(shf[4:7]), _pad8(shf[7:10])]
    cb = [shf[10:11], shf[11:12]]

    bias = lax.dynamic_slice_in_dim(ada_b, chip * nca, nca, axis=1).reshape(DEPTH, 1, nca)
    ada_part = ada_fwd(cvec, ada_w, bias, "ada_fwd")
    g2 = small_allgather(ada_part.reshape(DEPTH * 16, nca), [first_token] + lands[1:], "gather_ada")
    ada_full = g2[0::2].reshape(4, DEPTH, 16, nca).transpose(1, 2, 0, 3).reshape(DEPTH, 16, 4 * nca)
    lat = lax.dynamic_slice_in_dim(ada_full, me, 1, axis=1)[:, 0]
    mods = [jnp.stack([_pad8(lat[i].reshape(6, D)), _pad8(ada_full[i, 8].reshape(6, D))]) for i in range(DEPTH)]

    rest_sems, rest_lands, _, rest_token = split_start(lands[1:], None, g2, "gather_start")
    w_sems = first_sems + rest_sems
    lands = first_lands + rest_lands
    unit = {key: u for u, key in enumerate(order)}

    def wts(n, idx, after):
        u = unit[(n, idx)]
        if u == 0:
            after = rest_token
        (w,), _ = split_wait([lands[u]], None, w_sems[2 * u:2 * u + 2], after, f"gather_wait_{n}_{idx}")
        return w.reshape(w.shape[0] * w.shape[1], w.shape[2]) if n in ("w2", "hout", "cout") else w

    started = []

    def on_grads(i, tag, g):
        names = sorted(g)
        gs = [g[n].reshape(4, g[n].shape[0] * g[n].shape[1] // 4, g[n].shape[2]) for n in names]
        sems, zones, srcs, token = split_start([lax.empty(a.shape, BF16) for a in gs], gs, chip1, f"grad_start_{tag}_{i}")
        started.append(([(n, i if n in ("w1", "w2") else i // 2) for n in names], sems, zones, srcs))
        return token

    done = {}
    acc = {n: lax.empty(flat2(w).shape, F32) for n, (w, _, _) in tensors.items()}
    early_names = ["w1", "w2", "cin", "cout"]
    late_names = ["hin", "hout"]

    def finish_units(group, after, name):
        units = [(key, sems[2 * u:2 * u + 2], zones[u], srcs[u]) for ks, sems, zones, srcs in group for u, key in enumerate(ks)]
        zones, srcs = split_wait([u[2] for u in units], [u[3] for u in units], [s for u in units for s in u[1]], after, name)
        for (key, _, _, _), zone, own in zip(units, zones, srcs):
            acc[key[0]] = sum_slots(own, zone, acc[key[0]], key[1], chip1, f"sum_{key[0]}_{key[1]}")

    def after_backward(small, token):
        rows3 = jnp.concatenate(small["dmod"] + small["norm1"] + small["norm2"] + [small["norm_f"]] + small["gnorm"]
                                + [small["lb"][1]] + small["cw"] + small["cb"] + [jnp.tile(token[0:3], (1, D // 128))], axis=0)
        g3 = small_allgather(rows3, [], "gather_small_out")
        dmat = g3[:, :64].reshape(8, DEPTH, 2, 8, D)[:, :, :, :6].transpose(1, 2, 0, 3, 4).reshape(DEPTH, 16, 6 * D)
        dcols = lax.dynamic_slice_in_dim(dmat, chip * nca, nca, axis=2)
        *done["ada"], acc4 = ada_bwd(cvec, dcols, ada_w, m_ada_w, v_ada_w, "ada_bwd")
        g4 = small_allgather(acc4, [], "gather_cctx")
        done["fin"] = small_finish(g3, g4, c_ctx[None], _pad8(lb_p.reshape(4, D)), "small_finish")
        finish_units(list(started), done["fin"], "grad_wait_early")
        done["sib_early"] = sibling_start([acc[n] for n in early_names], "sibling_start_early")
        return done["sib_early"][0][-1]

    lane, dx, last_token = local_step(xs, loss_target[0], mods, norm1, norm2, norm_f[None], lbs, hgrn_gnorm, cw8, cb, wts, n_lat,
                                      on_grads, after_backward)
    loss = lax.psum(0.5 * jnp.sum(lane) / D, ("x", "y", "c"))
    grad_x = dx[None]
    g_ada_w, d_ada_w, nm_ada_w, nv_ada_w = done["ada"]
    fin = done["fin"]
    cols = lambda a: lax.dynamic_slice_in_dim(a, chip * sd, sd, axis=a.ndim - 1)
    small_g = [fin[32], fin[0:32].reshape(DEPTH, 8, D)[:, :6].reshape(DEPTH, 6 * D), fin[40:44], fin[44:48], fin[48], fin[49:51],
               cols(fin[64:68].reshape(2, 2, D)), cols(fin[53:59].reshape(2, 3, D)), cols(fin[59:61])]
    small_w = [c_ctx, ada_b, norm1, norm2, norm_f, hgrn_gnorm, hgrn_lb, conv_w, conv_b]
    small_m = [m_c_ctx, m_ada_b, m_norm1, m_norm2, m_norm_f, m_hgrn_gnorm, m_hgrn_lb, m_conv_w, m_conv_b]
    small_v = [v_c_ctx, v_ada_b, v_norm1, v_norm2, v_norm_f, v_hgrn_gnorm, v_hgrn_lb, v_conv_w, v_conv_b]
    shapes = [w.shape for w in small_w]
    packed = adamw([_pack_rows(small_g)], _pack_rows(small_w), _pack_rows(small_m), _pack_rows(small_v), "adamw_small")
    s_g, s_d, s_m, s_v = [_unpack_rows(p, shapes) for p in packed]

    results = {}

    def finish_tensors(names, sib, after, name):
        mine, other = sibling_wait(*sib, after, name)
        for n, pm, po in zip(names, mine, other):
            w, m, v = tensors[n]
            results[n] = [a.reshape(w.shape) for a in adamw([pm, po], flat2(w), flat2(m), flat2(v), f"adamw_{n}")]

    finish_tensors(early_names, done["sib_early"], last_token, "sibling_wait_early")
    finish_units(started[-1:], results["cout"][0], "grad_wait_late")
    sib_late = sibling_start([acc[n] for n in late_names], "sibling_start_late")
    finish_tensors(late_names, sib_late, results["cin"][0], "sibling_wait_late")
    b_g, b_d, b_m, b_v = [[results[n][k] for n in big_names] for k in range(4)]

    def ordered(s, a, b):
        return [s[0], a, s[1], s[2], s[3], s[4], b[0], b[1], b[2], s[6], s[5], b[3], b[4], s[7], s[8], b[5]]

    return (loss, grad_x, *ordered(s_g, g_ada_w, b_g), *ordered(s_d, d_ada_w, b_d), *ordered(s_m, nm_ada_w, b_m),
            *ordered(s_v, nv_ada_w, b_v))
```
